```python
import math
import jax
import jax.numpy as jnp
from jax import lax
import numpy as np

D_MODEL = 1024
BATCH = 16
SEQ = 2048
DEPTH = 2

GRID_W = 64
CTX_LEN = 256
EPS = 1e-6
N_MOD = 6

H_A = 4
DK = 64
DV = 2 * DK
QK_W = H_A * 2 * DK
W_A = H_A * DV
ROT_AX = DK // 2
ROPE_BASE = 10000.0
Q_BLOCK = 128

W_B = 512
G_B = 4
CHUNK_B = 128

H_C = 4
DH_C = 128
W_C = H_C * DH_C
CONV_K = 3
MLSTM_CHUNK = 128

N_BRANCH = 3
W_BR = 512

A_K0 = 0
A_V0 = A_K0 + QK_W
C_X0 = A_V0 + W_A
MEM_COLS = C_X0 + W_C
A_Q0 = MEM_COLS
B_UV0 = A_Q0 + QK_W
C_Z0 = B_UV0 + 2 * W_B
GATE0 = C_Z0 + W_C
IN_COLS = GATE0 + N_BRANCH * D_MODEL

N_GROUPS = 4
EXP_PER_GROUP = 8
N_EXPERTS = N_GROUPS * EXP_PER_GROUP
TOP_K = 2
D_EXPERT = 512
MOE_BLOCK = 128

kernel_name = 'hybrid_prefix_diffattn_gmlp_mlstm_hmoe'


def rmsnorm(x, g):
    xf = x.astype(jnp.float32)
    y = xf * lax.rsqrt(jnp.mean(xf * xf, axis=-1, keepdims=True) + EPS)
    return y.astype(x.dtype) * g


def modulate(h, shift, scale):
    return h * (1 + scale) + shift


def axial_angles(n_tokens):
    rows = n_tokens // GRID_W
    row = jnp.broadcast_to(jnp.arange(rows)[:, None], (rows, GRID_W)).reshape(-1).astype(jnp.float32)
    col = jnp.broadcast_to(jnp.arange(GRID_W)[None, :], (rows, GRID_W)).reshape(-1).astype(jnp.float32)
    inv = ROPE_BASE ** (-jnp.arange(0, ROT_AX, 2, dtype=jnp.float32) / ROT_AX)
    return row[:, None] * inv, col[:, None] * inv


def rope_rotate(x, ang):
    cos = jnp.cos(ang)[:, None, None, :].astype(x.dtype)
    sin = jnp.sin(ang)[:, None, None, :].astype(x.dtype)
    x1, x2 = jnp.split(x, 2, axis=-1)
    return jnp.concatenate([x1 * cos - x2 * sin, x1 * sin + x2 * cos], axis=-1)


def apply_axial_rope(x, ang_r, ang_c):
    return jnp.concatenate([rope_rotate(x[..., :ROT_AX], ang_r), rope_rotate(x[..., ROT_AX:], ang_c)], axis=-1)


def diff_lambda_value(lp, layer_idx):
    lam_init = 0.8 - 0.6 * math.exp(-0.3 * layer_idx)
    lpf = lp.astype(jnp.float32)
    lam = jnp.exp(jnp.sum(lpf[0] * lpf[1])) - jnp.exp(jnp.sum(lpf[2] * lpf[3])) + lam_init
    return lam, lam_init


def diff_attention(q, k, v, lam):
    s = jnp.einsum('bqhmd,bkhmd->bhmqk', q.astype(jnp.float32), k.astype(jnp.float32)) * (DK ** -0.5)
    p = jax.nn.softmax(s, axis=-1)
    a = p[:, :, 0] - lam * p[:, :, 1]
    return jnp.einsum('bhqk,bkhd->bqhd', a, v.astype(jnp.float32)).astype(v.dtype)


def blocked_diff_attention(q, k, v, lam):
    b, t = q.shape[:2]
    nb = t // Q_BLOCK
    qb = jnp.moveaxis(q.reshape(b, nb, Q_BLOCK, *q.shape[2:]), 1, 0)
    o = lax.map(lambda blk: diff_attention(blk, k, v, lam), qb)
    return jnp.moveaxis(o, 0, 1).reshape(b, t, *o.shape[3:])


def diff_heads_out(o, g, lam_init):
    b, t = o.shape[:2]
    return (rmsnorm(o, g) * (1.0 - lam_init)).reshape(b, t, W_A)


def chunk_spatial_gate(uv, g_v, w_s, b_s):
    b, t, _ = uv.shape
    u, v = jnp.split(uv, 2, axis=-1)
    v = rmsnorm(v, g_v).reshape(b, t // CHUNK_B, CHUNK_B, G_B, W_B // G_B)
    mixed = jnp.einsum('gts,bcsgd->bctgd', w_s, v) + jnp.swapaxes(b_s, 0, 1)[:, :, None]
    return u * mixed.reshape(b, t, W_B)


def depthwise_conv(x, w, b):
    out = lax.conv_general_dilated(x, w[:, None, :], window_strides=(1,),
                                   padding=[(CONV_K // 2, CONV_K // 2)],
                                   dimension_numbers=('NWC', 'WIO', 'NWC'),
                                   feature_group_count=x.shape[-1])
    return out + b


def mlstm_features(xm, w_conv, b_conv, w_qkv, w_if, b_if):
    b, t, _ = xm.shape
    xconv = jax.nn.silu(depthwise_conv(xm, w_conv, b_conv))
    heads = lambda a: a.reshape(b, t, H_C, DH_C)
    q = jnp.einsum('bthd,hde->bthe', heads(xconv), w_qkv[0])
    k = jnp.einsum('bthd,hde->bthe', heads(xconv), w_qkv[1])
    v = jnp.einsum('bthd,hde->bthe', heads(xm), w_qkv[2])
    qkv = jnp.concatenate([q.reshape(b, t, W_C), k.reshape(b, t, W_C), v.reshape(b, t, W_C)], axis=-1)
    gates = (qkv @ w_if + b_if).reshape(b, t, 4, H_C)
    return xconv, q, k * (DH_C ** -0.5), v, gates


def mlstm_zero_state(b):
    return (jnp.zeros((b, H_C, DH_C, DH_C), jnp.float32),
            jnp.zeros((b, H_C, DH_C), jnp.float32),
            jnp.zeros((b, H_C), jnp.float32))


def mlstm_scan(q, k, v, i_pre, f_pre, state, with_h):
    b, t, h, dh = q.shape
    nc = t // MLSTM_CHUNK
    f32 = jnp.float32

    def chunks(a):
        a = a.astype(f32).reshape(b, nc, MLSTM_CHUNK, *a.shape[2:])
        return jnp.moveaxis(a, 3, 1)

    qc, kc, vc = chunks(q), chunks(k), chunks(v)
    ig = chunks(i_pre)
    bcum = jnp.cumsum(jax.nn.log_sigmoid(chunks(f_pre)), axis=-1)
    g = bcum[..., -1]
    a_end = g[..., None] - bcum + ig
    m_loc = jnp.max(a_end, axis=-1)
    w_end = jnp.exp(a_end - m_loc[..., None])
    c_loc = jnp.einsum('bhcs,bhcsd,bhcse->bhcde', w_end, vc, kc)
    n_loc = jnp.einsum('bhcs,bhcse->bhce', w_end, kc)

    def step(carry, xs):
        c_st, n_st, m_st = carry
        c_l, n_l, m_l, g_c = xs
        m_new = jnp.maximum(g_c + m_st, m_l)
        a_old = jnp.exp(g_c + m_st - m_new)
        a_new = jnp.exp(m_l - m_new)
        c_new = a_old[..., None, None] * c_st + a_new[..., None, None] * c_l
        n_new = a_old[..., None] * n_st + a_new[..., None] * n_l
        return (c_new, n_new, m_new), carry

    xs = tuple(jnp.moveaxis(a, 2, 0) for a in (c_loc, n_loc, m_loc, g))
    init = tuple(s.astype(f32) for s in state)
    final, starts = lax.scan(step, init, xs)
    if not with_h:
        return final
    c0, n0, m0 = (jnp.moveaxis(s, 0, 2) for s in starts)
    ln = MLSTM_CHUNK
    causal = jnp.tril(jnp.ones((ln, ln), dtype=bool))
    log_d = jnp.where(causal, bcum[..., :, None] - bcum[..., None, :] + ig[..., None, :], -jnp.inf)
    log_inter = bcum + m0[..., None]
    m_t = jnp.maximum(log_inter, jnp.max(log_d, axis=-1))
    s = jnp.einsum('bhctd,bhcsd->bhcts', qc, kc) * jnp.exp(log_d - m_t[..., None])
    w_inter = jnp.exp(log_inter - m_t)
    num = (jnp.einsum('bhcts,bhcsd->bhctd', s, vc)
           + w_inter[..., None] * jnp.einsum('bhcde,bhcte->bhctd', c0, qc))
    den = jnp.sum(s, axis=-1) + w_inter * jnp.einsum('bhce,bhcte->bhct', n0, qc)
    hh = num / jnp.maximum(jnp.abs(den), jnp.exp(-m_t))[..., None]
    hh = jnp.moveaxis(hh, 1, 3).reshape(b, t, h, dh)
    return hh.astype(q.dtype), final


def mlstm_bidir(q, k, v, gates, init_f, init_b, with_h):
    rev = lambda a: jnp.flip(a, axis=1)
    out_f = mlstm_scan(q, k, v, gates[:, :, 0], gates[:, :, 1], init_f, with_h)
    out_b = mlstm_scan(rev(q), rev(k), rev(v), rev(gates[:, :, 2]), rev(gates[:, :, 3]), init_b, with_h)
    if not with_h:
        return None, out_f, out_b
    h_f, st_f = out_f
    h_b, st_b = out_b
    return h_f + rev(h_b), st_f, st_b


def mlstm_out(h_sum, xconv, z, g, skip):
    b, t = h_sum.shape[:2]
    hn = rmsnorm(h_sum, g.reshape(H_C, DH_C)).reshape(b, t, W_C)
    return (hn + skip * xconv) * jax.nn.sigmoid(z)


def merge_branches(ys, gate_pre, w_br, w_o):
    g = jax.nn.sigmoid(gate_pre.reshape(*gate_pre.shape[:-1], N_BRANCH, D_MODEL))
    merged = (g[..., 0, :] * (ys[0] @ w_br[0]) + g[..., 1, :] * (ys[1] @ w_br[1])
              + g[..., 2, :] * (ys[2] @ w_br[2]))
    return merged @ w_o


def hier_route(h, w_rg, b_rg, w_re, b_re):
    f32 = jnp.float32
    hf = h.astype(f32)
    p_group = jax.nn.softmax(hf @ w_rg.astype(f32) + b_rg.astype(f32), axis=-1)
    g_star = jnp.argmax(p_group, axis=-1)
    logit_e = (hf @ w_re.astype(f32) + b_re.astype(f32)).reshape(-1, N_GROUPS, EXP_PER_GROUP)
    logit_sel = jnp.take_along_axis(logit_e, g_star[:, None, None], axis=1)[:, 0]
    top_v, top_i = lax.top_k(logit_sel, TOP_K)
    p_g = jnp.take_along_axis(p_group, g_star[:, None], axis=1)
    wt = jax.nn.softmax(top_v, axis=-1) * p_g
    eid = (g_star[:, None] * EXP_PER_GROUP + top_i).astype(jnp.int32)
    return eid, wt.astype(h.dtype)


def moe_experts(h, eid, wt, w_gate, w_up, w_down):
    n, d = h.shape
    a = n * TOP_K
    n_blocks = -(-(a + N_EXPERTS * (MOE_BLOCK - 1)) // MOE_BLOCK)
    slots = n_blocks * MOE_BLOCK
    flat_e = eid.reshape(a)
    flat_tok = jnp.repeat(jnp.arange(n, dtype=jnp.int32), TOP_K)
    flat_w = wt.reshape(a)
    order = jnp.argsort(flat_e)
    se, stok, sw = flat_e[order], flat_tok[order], flat_w[order]
    counts = jnp.bincount(flat_e, length=N_EXPERTS)
    start = jnp.cumsum(counts) - counts
    padded = (counts + MOE_BLOCK - 1) // MOE_BLOCK * MOE_BLOCK
    pend = jnp.cumsum(padded)
    dest = (pend - padded)[se] + jnp.arange(a) - start[se]
    buf_tok = jnp.full((slots,), n, jnp.int32).at[dest].set(stok)
    buf_w = jnp.zeros((slots,), wt.dtype).at[dest].set(sw)
    block_e = jnp.minimum(jnp.searchsorted(pend, jnp.arange(n_blocks) * MOE_BLOCK, side='right'),
                          N_EXPERTS - 1)
    xb = jnp.concatenate([h, jnp.zeros((1, d), h.dtype)], axis=0)[buf_tok].reshape(n_blocks, MOE_BLOCK, d)

    def run_block(args):
        xblk, e = args
        return (jax.nn.silu(xblk @ w_gate[e]) * (xblk @ w_up[e])) @ w_down[e]

    yb = lax.map(run_block, (xb, block_e)).reshape(slots, d)
    return jax.ops.segment_sum(yb * buf_w[:, None], buf_tok, num_segments=n + 1)[:n]


def setup_inputs(seed: int = 0) -> dict:
    key = jax.random.key(seed)
    ks = list(jax.random.split(key, 40))
    nrm = lambda shape, s: jax.random.normal(ks.pop(), shape, jnp.float32) * s
    D = D_MODEL
    L = DEPTH
    forget_bias = jnp.linspace(3.0, 6.0, H_C)
    b_if = (jnp.zeros((L, 4, H_C), jnp.float32).at[:, 1].set(forget_bias).at[:, 3].set(forget_bias)
            + nrm((L, 4, H_C), 0.1)).reshape(L, 4 * H_C)
    return {
        'x': nrm((BATCH, SEQ, D), 1.0),
        'c': nrm((BATCH, D), 1.0),
        'ctx': nrm((BATCH, CTX_LEN, D), 1.0),
        'c_ctx': nrm((D,), 1.0),
        'w_mod': nrm((L, D, N_MOD * D), 0.5 * D ** -0.5),
        'b_mod': nrm((L, N_MOD * D), 0.02),
        'g_norm1': 1.0 + nrm((L, D), 0.02),
        'g_norm2': 1.0 + nrm((L, D), 0.02),
        'w_in': nrm((L, D, IN_COLS), D ** -0.5),
        'diff_lambda': nrm((L, 4, DK), 0.1),
        'g_diff_subln': 1.0 + nrm((L, DV), 0.02),
        'w_gmlp_s': nrm((L, G_B, CHUNK_B, CHUNK_B), CHUNK_B ** -0.5),
        'b_gmlp_s': 1.0 + nrm((L, G_B, CHUNK_B), 0.02),
        'g_gmlp_v': 1.0 + nrm((L, W_B), 0.02),
        'w_conv_m': nrm((L, CONV_K, W_C), CONV_K ** -0.5),
        'b_conv_m': nrm((L, W_C), 0.02),
        'w_qkv_m': nrm((L, 3, H_C, DH_C, DH_C), DH_C ** -0.5),
        'w_if_m': nrm((L, 3 * W_C, 4 * H_C), (3 * W_C) ** -0.5),
        'b_if_m': b_if,
        'g_mlstm_norm': 1.0 + nrm((L, W_C), 0.02),
        'skip_m': 1.0 + nrm((L, W_C), 0.02),
        'w_branch': nrm((L, N_BRANCH, W_BR, D), W_BR ** -0.5),
        'w_out': nrm((L, D, D), D ** -0.5),
        'w_route_g': nrm((L, D, N_GROUPS), D ** -0.5),
        'b_route_g': nrm((L, N_GROUPS), 0.01),
        'w_route_e': nrm((L, D, N_EXPERTS), D ** -0.5),
        'b_route_e': nrm((L, N_EXPERTS), 0.01),
        'w_e_gate': nrm((L, N_EXPERTS, D, D_EXPERT), D ** -0.5),
        'w_e_up': nrm((L, N_EXPERTS, D, D_EXPERT), D ** -0.5),
        'w_e_down': nrm((L, N_EXPERTS, D_EXPERT, D), D_EXPERT ** -0.5),
        'g_final': 1.0 + nrm((D,), 0.02),
    }


def reference(x, c, ctx, c_ctx, w_mod, b_mod, g_norm1, g_norm2, w_in, diff_lambda, g_diff_subln,
              w_gmlp_s, b_gmlp_s, g_gmlp_v, w_conv_m, b_conv_m, w_qkv_m, w_if_m, b_if_m,
              g_mlstm_norm, skip_m, w_branch, w_out, w_route_g, b_route_g, w_route_e, b_route_e,
              w_e_gate, w_e_up, w_e_down, g_final):
    B, T, D = x.shape
    tc = ctx.shape[1]
    ang_r, ang_c = axial_angles(T)
    xc = ctx
    for l in range(DEPTH):
        last = l == DEPTH - 1
        mod = (jax.nn.silu(c) @ w_mod[l] + b_mod[l]).reshape(B, N_MOD, 1, D)
        mod_c = (jax.nn.silu(c_ctx) @ w_mod[l] + b_mod[l]).reshape(N_MOD, D)

        h = modulate(rmsnorm(x, g_norm1[l]), mod[:, 0], mod[:, 1])
        hc = modulate(rmsnorm(xc, g_norm1[l]), mod_c[0], mod_c[1])
        p = h @ w_in[l]
        pc = hc @ (w_in[l, :, :MEM_COLS] if last else w_in[l])
        lam, lam_init = diff_lambda_value(diff_lambda[l], l)

        q_lat = apply_axial_rope(p[..., A_Q0:A_Q0 + QK_W].reshape(B, T, H_A, 2, DK), ang_r, ang_c)
        k_lat = apply_axial_rope(p[..., A_K0:A_K0 + QK_W].reshape(B, T, H_A, 2, DK), ang_r, ang_c)
        v_lat = p[..., A_V0:A_V0 + W_A].reshape(B, T, H_A, DV)
        k_ctx = pc[..., A_K0:A_K0 + QK_W].reshape(B, tc, H_A, 2, DK)
        v_ctx = pc[..., A_V0:A_V0 + W_A].reshape(B, tc, H_A, DV)
        o_a = blocked_diff_attention(q_lat, jnp.concatenate([k_ctx, k_lat], axis=1),
                                     jnp.concatenate([v_ctx, v_lat], axis=1), lam)
        y_a = diff_heads_out(o_a, g_diff_subln[l], lam_init)

        y_b = chunk_spatial_gate(jax.nn.gelu(p[..., B_UV0:B_UV0 + 2 * W_B]), g_gmlp_v[l], w_gmlp_s[l], b_gmlp_s[l])

        mparams = (w_conv_m[l], b_conv_m[l], w_qkv_m[l], w_if_m[l], b_if_m[l])
        xconv_c, q_c, k_c, v_c, gates_c = mlstm_features(pc[..., C_X0:C_X0 + W_C], *mparams)
        zero = mlstm_zero_state(B)
        h_mc, st_f, st_b = mlstm_bidir(q_c, k_c, v_c, gates_c, zero, zero, not last)
        xconv, q_m, k_m, v_m, gates_m = mlstm_features(p[..., C_X0:C_X0 + W_C], *mparams)
        h_m, _, _ = mlstm_bidir(q_m, k_m, v_m, gates_m, st_f, st_b, True)
        y_c = mlstm_out(h_m, xconv, p[..., C_Z0:C_Z0 + W_C], g_mlstm_norm[l], skip_m[l])

        y = merge_branches((y_a, y_b, y_c), p[..., GATE0:], w_branch[l], w_out[l])
        x = x + mod[:, 2] * y

        if not last:
            q_ctx = pc[..., A_Q0:A_Q0 + QK_W].reshape(B, tc, H_A, 2, DK)
            yc_a = diff_heads_out(diff_attention(q_ctx, k_ctx, v_ctx, lam), g_diff_subln[l], lam_init)
            yc_b = chunk_spatial_gate(jax.nn.gelu(pc[..., B_UV0:B_UV0 + 2 * W_B]), g_gmlp_v[l],
                                      w_gmlp_s[l], b_gmlp_s[l])
            yc_c = mlstm_out(h_mc, xconv_c, pc[..., C_Z0:C_Z0 + W_C], g_mlstm_norm[l], skip_m[l])
            yc = merge_branches((yc_a, yc_b, yc_c), pc[..., GATE0:], w_branch[l], w_out[l])
            xc = xc + mod_c[2] * yc

        h2 = modulate(rmsnorm(x, g_norm2[l]), mod[:, 3], mod[:, 4]).reshape(B * T, D)
        if last:
            tokens = h2
        else:
            hc2 = modulate(rmsnorm(xc, g_norm2[l]), mod_c[3], mod_c[4]).reshape(B * tc, D)
            tokens = jnp.concatenate([hc2, h2], axis=0)
        eid, wt = hier_route(tokens, w_route_g[l], b_route_g[l], w_route_e[l], b_route_e[l])
        f = moe_experts(tokens, eid, wt, w_e_gate[l], w_e_up[l], w_e_down[l])
        if not last:
            xc = xc + mod_c[5] * f[:B * tc].reshape(B, tc, D)
        x = x + mod[:, 5] * f[f.shape[0] - B * T:].reshape(B, T, D)

    return rmsnorm(x, g_final)
```

```python
import functools
import math

import jax
import jax.numpy as jnp
from jax import lax
from jax.experimental import pallas as pl
from jax.experimental.pallas import tpu as pltpu

F32 = jnp.float32
MXU_DT = jnp.bfloat16
ACT_DT = jnp.bfloat16
HIGHEST = lax.Precision.HIGHEST

D_MODEL = 1024
N_MOD = 6
EPS = 1e-6
GRID_W = 64
H_A, DK = 4, 64
DV = 2 * DK
ROT_AX = DK // 2
ROPE_BASE = 10000.0
W_B, G_B, CHUNK_B = 512, 4, 128
H_C, DH_C, CONV_K, MLSTM_CHUNK = 4, 128, 3, 128
W_C = H_C * DH_C
N_GROUPS, EXP_PER_GROUP, TOP_K, D_EXPERT = 4, 8, 2, 512
N_EXPERTS = N_GROUPS * EXP_PER_GROUP
W_BR = 512

GATE_W = 3 * D_MODEL
P_GATE = 0
P_K = GATE_W
P_V = P_K + 512
P_XM = P_V + 512
P_Q = P_XM + 512
P_UV = P_Q + 512
P_Z = P_UV + 2 * W_B
IN_COLS = P_Z + W_C
ORIG_GATE0 = IN_COLS - GATE_W

LANE = 128
ROW_TILE = 256
PROJ_CHUNK = 512
MOE_BLOCK = 256
HALO = 16
VMEM_LIMIT = 56 * 1024 * 1024
NEG_BIG = -1e30


def _cparams(*sem):
    return pltpu.CompilerParams(dimension_semantics=sem, vmem_limit_bytes=VMEM_LIMIT)


def _rms(x):
    return x * lax.rsqrt(jnp.mean(x * x, axis=-1, keepdims=True) + EPS)


def _mod_kernel(c_ref, w_ref, b_ref, o_ref):
    c = c_ref[...]
    s = c * jax.nn.sigmoid(c)
    o_ref[...] = jnp.dot(s, w_ref[...], preferred_element_type=F32, precision=HIGHEST) + b_ref[...]


def _modulation(c_all, w_mod, b_mod):
    n_layer = w_mod.shape[0]
    mb, d = c_all.shape
    tn = 1024
    return pl.pallas_call(
        _mod_kernel,
        grid=(n_layer, N_MOD * d // tn),
        in_specs=[
            pl.BlockSpec((mb, d), lambda l, j: (0, 0)),
            pl.BlockSpec((None, d, tn), lambda l, j: (l, 0, j)),
            pl.BlockSpec((None, 1, tn), lambda l, j: (l, 0, j)),
        ],
        out_specs=pl.BlockSpec((None, mb, tn), lambda l, j: (l, 0, j)),
        out_shape=jax.ShapeDtypeStruct((n_layer, mb, N_MOD * d), F32),
        compiler_params=_cparams("parallel", "parallel"),
        name="modulation",
    )(c_all, w_mod, b_mod.reshape(n_layer, 1, N_MOD * d))


def _rope(acc, cos, sin_signed):
    w = acc.shape[1]
    lane = lax.broadcasted_iota(jnp.int32, acc.shape, 1)
    partner = jnp.where((lane & 16) == 0, pltpu.roll(acc, w - 16, 1), pltpu.roll(acc, 16, 1))
    reps = w // cos.shape[1]
    return acc * jnp.tile(cos, (1, reps)) + partner * jnp.tile(sin_signed, (1, reps))


def _inproj_kernel(x_ref, g_ref, mod_ref, cos_ref, sin_ref, w_ref, o_ref):
    x = x_ref[...]
    h = (_rms(x) * g_ref[...] * (1.0 + mod_ref[1:2, :]) + mod_ref[0:1, :]).astype(MXU_DT)
    k_chunk = P_K // PROJ_CHUNK
    q_chunk = P_Q // PROJ_CHUNK
    for c in range(IN_COLS // PROJ_CHUNK):
        cols = slice(c * PROJ_CHUNK, (c + 1) * PROJ_CHUNK)
        acc = jnp.dot(h, w_ref[:, cols], preferred_element_type=F32)
        if c == k_chunk:
            acc = _rope(acc, cos_ref[...], sin_ref[...])
        elif c == q_chunk:
            acc = _rope(acc, cos_ref[...], sin_ref[...]) * (DK ** -0.5)
        o_ref[:, cols] = acc.astype(o_ref.dtype)


def _inproj(xall, g1, mod, cos, sin_signed, w_p, n_batch, tps):
    r, d = xall.shape
    tm = ROW_TILE

    def mod_idx(i):
        return (jnp.where(i % tps == 0, n_batch, i // tps), 0, 0)

    return pl.pallas_call(
        _inproj_kernel,
        grid=(r // tm,),
        in_specs=[
            pl.BlockSpec((tm, d), lambda i: (i, 0)),
            pl.BlockSpec((1, d), lambda i: (0, 0)),
            pl.BlockSpec((None, N_MOD, d), mod_idx),
            pl.BlockSpec((tm, LANE), lambda i: (i % tps, 0)),
            pl.BlockSpec((tm, LANE), lambda i: (i % tps, 0)),
            pl.BlockSpec((d, IN_COLS), lambda i: (0, 0)),
        ],
        out_specs=pl.BlockSpec((tm, IN_COLS), lambda i: (i, 0)),
        out_shape=jax.ShapeDtypeStruct((r, IN_COLS), ACT_DT),
        compiler_params=_cparams("parallel"),
        name="inproj",
    )(xall, g1, mod, cos, sin_signed, w_p)


def _attn_kernel(dl_ref, g_ref, q_ref, k_ref, v_ref, o_ref, *, lam_init, ctx_rows):
    dl = dl_ref[...]
    lam = (jnp.exp(jnp.sum(dl[0:1] * dl[1:2], keepdims=True))
           - jnp.exp(jnp.sum(dl[2:3] * dl[3:4], keepdims=True)) + lam_init)
    nt = (((1,), (1,)), ((), ()))

    def attend(kv_rows):
        q = q_ref[...]
        k = k_ref[0:kv_rows, :]
        lane = lax.broadcasted_iota(jnp.int32, q.shape, 1)
        zero = jnp.zeros_like(q)

        def softmax_parts(qm):
            s = lax.dot_general(qm, k, nt, preferred_element_type=F32)
            p = jnp.exp(s - jnp.max(s, axis=-1, keepdims=True))
            return p, jnp.sum(p, axis=-1, keepdims=True)

        p0, l0 = softmax_parts(jnp.where(lane < DK, q, zero))
        p1, l1 = softmax_parts(jnp.where(lane >= DK, q, zero))
        a = p0 * (1.0 / l0) - p1 * (lam / l1)
        o = jnp.dot(a.astype(MXU_DT), v_ref[0:kv_rows, :], preferred_element_type=F32)
        o_ref[...] = (_rms(o) * g_ref[...] * (1.0 - lam_init)).astype(o_ref.dtype)

    is_ctx = pl.program_id(2) == 0

    @pl.when(is_ctx)
    def _():
        attend(ctx_rows)

    @pl.when(jnp.logical_not(is_ctx))
    def _():
        attend(k_ref.shape[0])


def _attention(p, diff_lambda, g_sub, lam_init, n_batch, tps, ctx_rows):
    tq = ROW_TILE
    seq = tps * tq
    kern = functools.partial(_attn_kernel, lam_init=lam_init, ctx_rows=ctx_rows)
    return pl.pallas_call(
        kern,
        grid=(n_batch, H_A, tps),
        in_specs=[
            pl.BlockSpec((4, DK), lambda b, h, i: (0, 0)),
            pl.BlockSpec((1, DV), lambda b, h, i: (0, 0)),
            pl.BlockSpec((tq, LANE), lambda b, h, i: (b * tps + i, P_Q // LANE + h)),
            pl.BlockSpec((seq, LANE), lambda b, h, i: (b, P_K // LANE + h)),
            pl.BlockSpec((seq, LANE), lambda b, h, i: (b, P_V // LANE + h)),
        ],
        out_specs=pl.BlockSpec((tq, LANE), lambda b, h, i: (b * tps + i, h)),
        out_shape=jax.ShapeDtypeStruct((p.shape[0], H_A * DV), ACT_DT),
        compiler_params=_cparams("parallel", "parallel", "arbitrary"),
        name="diff_attention",
    )(diff_lambda, g_sub, p, p, p)


def _gmlp_kernel(uv_ref, gv_ref, ws_ref, bs_ref, o_ref):
    uv = jax.nn.gelu(uv_ref[...].astype(F32))
    u = uv[:, :W_B]
    v = (_rms(uv[:, W_B:]) * gv_ref[...]).astype(MXU_DT)
    gw = W_B // G_B
    for c in range(uv.shape[0] // CHUNK_B):
        rows = slice(c * CHUNK_B, (c + 1) * CHUNK_B)
        for g in range(G_B):
            cols = slice(g * gw, (g + 1) * gw)
            mixed = jnp.dot(ws_ref[g], v[rows, cols], preferred_element_type=F32) + bs_ref[:, cols]
            o_ref[rows, cols] = (u[rows, cols] * mixed).astype(o_ref.dtype)


def _gmlp(p, g_v, w_s, b_full):
    r = p.shape[0]
    tm = ROW_TILE
    return pl.pallas_call(
        _gmlp_kernel,
        grid=(r // tm,),
        in_specs=[
            pl.BlockSpec((tm, 2 * W_B), lambda i: (i, P_UV // (2 * W_B))),
            pl.BlockSpec((1, W_B), lambda i: (0, 0)),
            pl.BlockSpec((G_B, CHUNK_B, CHUNK_B), lambda i: (0, 0, 0)),
            pl.BlockSpec((CHUNK_B, W_B), lambda i: (0, 0)),
        ],
        out_specs=pl.BlockSpec((tm, W_B), lambda i: (i, 0)),
        out_shape=jax.ShapeDtypeStruct((r, W_B), ACT_DT),
        compiler_params=_cparams("parallel"),
        name="gmlp",
    )(p, g_v, w_s, b_full)


def _mfeat_kernel(x_ref, prev_ref, next_ref, wc_ref, bc_ref, wqkv_ref, wif_ref, bif_ref,
                  xc_ref, q_ref, k_ref, v_ref, g_ref, *, tps):
    tm = x_ref.shape[0]
    j = pl.program_id(0) % tps
    seg_start = jnp.logical_or(j == 0, j == 1)
    seg_end = jnp.logical_or(j == 0, j == tps - 1)
    x = x_ref[...].astype(F32)
    prow = jnp.where(seg_start, 0.0, prev_ref[...].astype(F32)[HALO - 1:HALO, :])
    nrow = jnp.where(seg_end, 0.0, next_ref[...].astype(F32)[0:1, :])
    rid = lax.broadcasted_iota(jnp.int32, x.shape, 0)
    xp = jnp.where(rid == 0, prow, pltpu.roll(x, 1, 0))
    xn = jnp.where(rid == tm - 1, nrow, pltpu.roll(x, tm - 1, 0))
    xc = wc_ref[0:1, :] * xp + wc_ref[1:2, :] * x + wc_ref[2:3, :] * xn + bc_ref[...]
    xc = xc * jax.nn.sigmoid(xc)
    xc_ref[...] = xc.astype(xc_ref.dtype)
    xcb = xc.astype(MXU_DT)
    xb = x_ref[...]
    qs, ks, vs = [], [], []
    for h in range(H_C):
        cols = slice(h * DH_C, (h + 1) * DH_C)
        qs.append(jnp.dot(xcb[:, cols], wqkv_ref[0, h], preferred_element_type=F32))
        ks.append(jnp.dot(xcb[:, cols], wqkv_ref[1, h], preferred_element_type=F32))
        vs.append(jnp.dot(xb[:, cols], wqkv_ref[2, h], preferred_element_type=F32))
    q = jnp.concatenate(qs, axis=1)
    k = jnp.concatenate(ks, axis=1)
    v = jnp.concatenate(vs, axis=1)
    qkv = jnp.concatenate([q, k, v], axis=1).astype(MXU_DT)
    g_ref[...] = jnp.dot(qkv, wif_ref[...], preferred_element_type=F32) + bif_ref[...]
    q_ref[...] = q.astype(q_ref.dtype)
    k_ref[...] = (k * (DH_C ** -0.5)).astype(k_ref.dtype)
    v_ref[...] = v.astype(v_ref.dtype)


def _mlstm_features(p, w_conv, b_conv, w_qkv, w_if_pad, b_if_pad, tps):
    r = p.shape[0]
    tm = ROW_TILE
    hpt = tm // HALO
    xm_blk = P_XM // W_C
    last_halo = r // HALO - 1
    kern = functools.partial(_mfeat_kernel, tps=tps)
    act = jax.ShapeDtypeStruct((r, W_C), ACT_DT)
    row_spec = pl.BlockSpec((tm, W_C), lambda i: (i, 0))
    return pl.pallas_call(
        kern,
        grid=(r // tm,),
        in_specs=[
            pl.BlockSpec((tm, W_C), lambda i: (i, xm_blk)),
            pl.BlockSpec((HALO, W_C), lambda i: (jnp.maximum(i * hpt - 1, 0), xm_blk)),
            pl.BlockSpec((HALO, W_C), lambda i: (jnp.minimum((i + 1) * hpt, last_halo), xm_blk)),
            pl.BlockSpec((CONV_K, W_C), lambda i: (0, 0)),
            pl.BlockSpec((1, W_C), lambda i: (0, 0)),
            pl.BlockSpec((3, H_C, DH_C, DH_C), lambda i: (0, 0, 0, 0)),
            pl.BlockSpec((3 * W_C, LANE), lambda i: (0, 0)),
            pl.BlockSpec((1, LANE), lambda i: (0, 0)),
        ],
        out_specs=[row_spec, row_spec, row_spec, row_spec,
                   pl.BlockSpec((tm, LANE), lambda i: (i, 0))],
        out_shape=[act, act, act, act, jax.ShapeDtypeStruct((r, LANE), F32)],
        compiler_params=_cparams("parallel"),
        name="mlstm_features",
    )(p, p, p, w_conv, b_conv, w_qkv, w_if_pad, b_if_pad)


def _lane_scan(x, op, reverse):
    n = x.shape[1]
    lane = lax.broadcasted_iota(jnp.int32, x.shape, 1)
    sh = 1
    while sh < n:
        if reverse:
            shifted, valid = pltpu.roll(x, n - sh, 1), lane < n - sh
        else:
            shifted, valid = pltpu.roll(x, sh, 1), lane >= sh
        x = jnp.where(valid, op(x, shifted), x)
        sh *= 2
    return x


def _mlstm_direction(gt, q_ref, k_ref, v_ref, c_ref, m_ref, h_ref, reverse):
    ln = gt.shape[1]
    nr = 2 * H_C
    ig = gt
    bcum = _lane_scan(jax.nn.log_sigmoid(pltpu.roll(gt, H_C, 0)), jnp.add, reverse)
    r = ig - bcum
    cmx = _lane_scan(r, jnp.maximum, reverse)
    g = bcum[:, 0:1] if reverse else bcum[:, ln - 1:ln]
    a_end = g + r
    m_loc = jnp.max(a_end, axis=1, keepdims=True)
    w_end = jnp.exp(a_end - m_loc)
    m_old = m_ref[:, 0:1]
    m_new = jnp.maximum(g + m_old, m_loc)
    a_old = jnp.exp(g + m_old - m_new)
    a_new = jnp.exp(m_loc - m_new)
    zt = jnp.concatenate([bcum, cmx, w_end, jnp.zeros((LANE - 3 * nr, ln), F32)], axis=0).T
    t_id = lax.broadcasted_iota(jnp.int32, (ln, ln), 0)
    s_id = lax.broadcasted_iota(jnp.int32, (ln, ln), 1)
    keep = (s_id >= t_id) if reverse else (s_id <= t_id)
    ones_col = (lax.broadcasted_iota(jnp.int32, (ln, DH_C), 1) == 0).astype(F32)
    nt = (((1,), (1,)), ((), ()))
    for h in range(H_C):
        cols = slice(h * DH_C, (h + 1) * DH_C)
        q = q_ref[:, cols]
        k = k_ref[:, cols]
        vext = jnp.concatenate([v_ref[:, cols].astype(F32), ones_col], axis=1)
        bc = zt[:, h:h + 1]
        cm = zt[:, nr + h:nr + h + 1]
        we = zt[:, 2 * nr + h:2 * nr + h + 1]
        m0 = m_old[h:h + 1, :]
        log_inter = bc + m0
        m_t = jnp.maximum(log_inter, bc + cm)
        dmat = jnp.exp(jnp.where(keep, bc + r[h:h + 1, :] - m_t, NEG_BIG))
        s = lax.dot_general(q, k, nt, preferred_element_type=F32) * dmat
        cext = c_ref[h]
        r1 = jnp.dot(s.astype(MXU_DT), vext.astype(MXU_DT), preferred_element_type=F32)
        r2 = lax.dot_general(q, cext.astype(MXU_DT), nt, preferred_element_type=F32)
        w_inter = jnp.exp(log_inter - m_t)
        num = r1[:, :DH_C] + w_inter * r2[:, :DH_C]
        den = r1[:, DH_C:DH_C + 1] + w_inter * r2[:, DH_C:DH_C + 1]
        hh = num / jnp.maximum(jnp.abs(den), jnp.exp(-m_t))
        h_ref[:, cols] = hh.astype(h_ref.dtype)
        vw_t = (vext * we).T.astype(MXU_DT)
        c_loc = jnp.dot(vw_t, k, preferred_element_type=F32)
        c_ref[h] = a_old[h:h + 1, :] * cext + a_new[h:h + 1, :] * c_loc
    m_ref[...] = jnp.broadcast_to(m_new, m_ref.shape)


def _mscan_kernel(gf_ref, qf_ref, kf_ref, vf_ref, gb_ref, qb_ref, kb_ref, vb_ref,
                  hf_ref, hb_ref, cf_ref, mf_ref, cb_ref, mb_ref):
    @pl.when(pl.program_id(1) == 0)
    def _():
        cf_ref[...] = jnp.zeros_like(cf_ref)
        cb_ref[...] = jnp.zeros_like(cb_ref)
        mf_ref[...] = jnp.zeros_like(mf_ref)
        mb_ref[...] = jnp.zeros_like(mb_ref)

    _mlstm_direction(gf_ref[...].T[0:2 * H_C], qf_ref, kf_ref, vf_ref, cf_ref, mf_ref, hf_ref, False)
    _mlstm_direction(gb_ref[...].T[2 * H_C:4 * H_C], qb_ref, kb_ref, vb_ref, cb_ref, mb_ref, hb_ref, True)


def _mlstm_scan(q, k, v, gates, n_batch, seq, ctx_rows):
    ln = MLSTM_CHUNK
    nck = seq // ln
    ncc = ctx_rows // ln

    def fwd(b, s):
        return (b * nck + s, 0)

    def bwd(b, s):
        return (b * nck + jnp.where(s < ncc, ncc - 1 - s, nck - 1 + ncc - s), 0)

    act = lambda idx: pl.BlockSpec((ln, W_C), idx)
    gsp = lambda idx: pl.BlockSpec((ln, LANE), idx)
    out = jax.ShapeDtypeStruct(q.shape, ACT_DT)
    return pl.pallas_call(
        _mscan_kernel,
        grid=(n_batch, nck),
        in_specs=[gsp(fwd), act(fwd), act(fwd), act(fwd), gsp(bwd), act(bwd), act(bwd), act(bwd)],
        out_specs=[act(fwd), act(bwd)],
        out_shape=[out, out],
        scratch_shapes=[
            pltpu.VMEM((H_C, 2 * DH_C, DH_C), F32), pltpu.VMEM((8, LANE), F32),
            pltpu.VMEM((H_C, 2 * DH_C, DH_C), F32), pltpu.VMEM((8, LANE), F32),
        ],
        compiler_params=_cparams("parallel", "arbitrary"),
        name="mlstm_scan",
    )(gates, q, k, v, gates, q, k, v)


def _route(logits):
    lane = lax.broadcasted_iota(jnp.int32, logits.shape, 1).astype(F32)
    big = float(4 * LANE)
    gl = jnp.where(lane < N_GROUPS, logits, NEG_BIG)
    gmax = jnp.max(gl, axis=-1, keepdims=True)
    g_star = jnp.min(jnp.where(gl == gmax, lane, big), axis=-1, keepdims=True)
    p_g = 1.0 / jnp.sum(jnp.exp(gl - gmax), axis=-1, keepdims=True)
    e_lo = g_star * EXP_PER_GROUP + N_GROUPS
    in_group = jnp.logical_and(lane >= e_lo, lane < e_lo + EXP_PER_GROUP)
    el = jnp.where(in_group, logits, NEG_BIG)
    v1 = jnp.max(el, axis=-1, keepdims=True)
    i1 = jnp.min(jnp.where(el == v1, lane, big), axis=-1, keepdims=True)
    el2 = jnp.where(lane == i1, NEG_BIG, el)
    v2 = jnp.max(el2, axis=-1, keepdims=True)
    i2 = jnp.min(jnp.where(el2 == v2, lane, big), axis=-1, keepdims=True)
    e21 = jnp.exp(v2 - v1)
    w1 = p_g / (1.0 + e21)
    w2 = p_g * e21 / (1.0 + e21)
    out = jnp.where(lane == 0, i1 - N_GROUPS, 0.0)
    out = jnp.where(lane == 1, i2 - N_GROUPS, out)
    out = jnp.where(lane == 2, w1, out)
    return jnp.where(lane == 3, w2, out)


def _merge_kernel(x_ref, gate_ref, z_ref, ya_ref, yb_ref, hf_ref, hb_ref, xc_ref, gm_ref, sk_ref,
                  wbr_ref, wo_ref, mod_ref, g2_ref, wr_ref, br_ref, xo_ref, h2_ref, rt_ref):
    hs = hf_ref[...].astype(F32) + hb_ref[...].astype(F32)
    hn = jnp.concatenate([_rms(hs[:, h * DH_C:(h + 1) * DH_C]) for h in range(H_C)], axis=1) * gm_ref[...]
    yc = (hn + sk_ref[...] * xc_ref[...].astype(F32)) * jax.nn.sigmoid(z_ref[...].astype(F32))
    ys = (ya_ref[...], yb_ref[...], yc.astype(MXU_DT))
    merged = None
    for i in range(3):
        gate = jax.nn.sigmoid(gate_ref[:, i * D_MODEL:(i + 1) * D_MODEL].astype(F32))
        term = gate * jnp.dot(ys[i], wbr_ref[i], preferred_element_type=F32)
        merged = term if merged is None else merged + term
    y = jnp.dot(merged.astype(MXU_DT), wo_ref[...], preferred_element_type=F32)
    xn = x_ref[...] + mod_ref[2:3, :] * y
    xo_ref[...] = xn
    h2 = _rms(xn) * g2_ref[...] * (1.0 + mod_ref[4:5, :]) + mod_ref[3:4, :]
    h2_ref[...] = h2
    logits = jnp.dot(h2, wr_ref[...], preferred_element_type=F32, precision=HIGHEST) + br_ref[...]
    rt_ref[...] = _route(logits)


def _merge(xall, p, ya, yb, hf, hb, xconv, g_m, skip, w_br, w_o, mod, g2, w_route, b_route, n_batch, tps):
    r, d = xall.shape
    tm = ROW_TILE

    def mod_idx(i):
        return (jnp.where(i % tps == 0, n_batch, i // tps), 0, 0)

    row = lambda w: pl.BlockSpec((tm, w), lambda i: (i, 0))
    full = lambda shape: pl.BlockSpec(shape, lambda i: (0,) * len(shape))
    return pl.pallas_call(
        _merge_kernel,
        grid=(r // tm,),
        in_specs=[
            row(d),
            pl.BlockSpec((tm, GATE_W), lambda i: (i, P_GATE // GATE_W)),
            pl.BlockSpec((tm, W_C), lambda i: (i, P_Z // W_C)),
            row(W_BR), row(W_BR), row(W_C), row(W_C), row(W_C),
            full((1, W_C)), full((1, W_C)),
            full((3, W_BR, d)), full((d, d)),
            pl.BlockSpec((None, N_MOD, d), mod_idx),
            full((1, d)), full((d, LANE)), full((1, LANE)),
        ],
        out_specs=[row(d), row(d), row(LANE)],
        out_shape=[jax.ShapeDtypeStruct((r, d), F32), jax.ShapeDtypeStruct((r, d), F32),
                   jax.ShapeDtypeStruct((r, LANE), F32)],
        compiler_params=_cparams("parallel"),
        name="merge_route",
    )(xall, p, p, ya, yb, hf, hb, xconv, g_m, skip, w_br, w_o, mod, g2, w_route, b_route)


def _moe_kernel(be_ref, nb_ref, idx_ref, idx_next_ref, h_hbm, wg_ref, wu_ref, wd_ref, y_ref, xbuf, sem):
    i = pl.program_id(0)
    n_live = nb_ref[0]
    bm = xbuf.shape[1]

    def start_gather(ids_ref, slot):
        def body(r, carry):
            t = ids_ref[0, r]
            pltpu.make_async_copy(h_hbm.at[pl.ds(t, 1), :], xbuf.at[slot, pl.ds(r, 1), :], sem.at[slot]).start()
            return carry
        lax.fori_loop(0, bm, body, 0)

    @pl.when(jnp.logical_and(i == 0, n_live > 0))
    def _():
        start_gather(idx_ref, 0)

    @pl.when(i + 1 < n_live)
    def _():
        start_gather(idx_next_ref, (i + 1) % 2)

    @pl.when(i < n_live)
    def _():
        slot = i % 2
        pltpu.make_async_copy(h_hbm.at[pl.ds(0, bm), :], xbuf.at[slot], sem.at[slot]).wait()
        x = xbuf[slot].astype(MXU_DT)
        a = jnp.dot(x, wg_ref[...], preferred_element_type=F32)
        u = jnp.dot(x, wu_ref[...], preferred_element_type=F32)
        act = (a * jax.nn.sigmoid(a) * u).astype(MXU_DT)
        y_ref[...] = jnp.dot(act, wd_ref[...], preferred_element_type=F32)

    @pl.when(i >= n_live)
    def _():
        y_ref[...] = jnp.zeros_like(y_ref)


def _moe_experts(h2, block_e, n_live, buf_tok, w_gate, w_up, w_down):
    n, d = h2.shape
    bm = MOE_BLOCK
    n_blocks = buf_tok.shape[0]
    last = n_blocks - 1
    grid_spec = pltpu.PrefetchScalarGridSpec(
        num_scalar_prefetch=2,
        grid=(n_blocks,),
        in_specs=[
            pl.BlockSpec((None, 1, bm), lambda i, be, nb: (i, 0, 0), memory_space=pltpu.SMEM),
            pl.BlockSpec((None, 1, bm), lambda i, be, nb: (jnp.minimum(i + 1, last), 0, 0),
                         memory_space=pltpu.SMEM),
            pl.BlockSpec(memory_space=pl.ANY),
            pl.BlockSpec((None, d, D_EXPERT), lambda i, be, nb: (be[i], 0, 0)),
            pl.BlockSpec((None, d, D_EXPERT), lambda i, be, nb: (be[i], 0, 0)),
            pl.BlockSpec((None, D_EXPERT, d), lambda i, be, nb: (be[i], 0, 0)),
        ],
        out_specs=pl.BlockSpec((bm, d), lambda i, be, nb: (i, 0)),
        scratch_shapes=[pltpu.VMEM((2, bm, d), F32), pltpu.SemaphoreType.DMA((2,))],
    )
    return pl.pallas_call(
        _moe_kernel,
        grid_spec=grid_spec,
        out_shape=jax.ShapeDtypeStruct((n_blocks * bm, d), F32),
        compiler_params=_cparams("arbitrary"),
        name="moe_experts",
    )(block_e, n_live, buf_tok, buf_tok, h2, w_gate, w_up, w_down)


def _combine_kernel(pos_ref, pos_next_ref, x_ref, rt_ref, mod_ref, gf_ref, yb_hbm, xo_ref, xf_ref, ybuf, sem):
    i = pl.program_id(0)
    n = pl.num_programs(0)
    tm = x_ref.shape[0]

    def start_gather(ids_ref, slot):
        def body(r, carry):
            s = ids_ref[0, r]
            pltpu.make_async_copy(yb_hbm.at[pl.ds(s, 1), :], ybuf.at[slot, pl.ds(r, 1), :], sem.at[slot]).start()
            return carry
        lax.fori_loop(0, 2 * tm, body, 0)

    @pl.when(i == 0)
    def _():
        start_gather(pos_ref, 0)

    @pl.when(i + 1 < n)
    def _():
        start_gather(pos_next_ref, (i + 1) % 2)

    slot = i % 2
    pltpu.make_async_copy(yb_hbm.at[pl.ds(0, 2 * tm), :], ybuf.at[slot], sem.at[slot]).wait()
    rt = rt_ref[...]
    f = rt[:, 2:3] * ybuf[slot, 0:tm, :] + rt[:, 3:4] * ybuf[slot, tm:2 * tm, :]
    xn = x_ref[...] + mod_ref[5:6, :] * f
    xo_ref[...] = xn
    xf_ref[...] = _rms(xn) * gf_ref[...]


def _combine(xall, route, mod, g_final, yb, pos, n_batch, tps):
    r, d = xall.shape
    tm = ROW_TILE
    nt = r // tm
    last = nt - 1

    def mod_idx(i):
        return (jnp.where(i % tps == 0, n_batch, i // tps), 0, 0)

    row = lambda w: pl.BlockSpec((tm, w), lambda i: (i, 0))
    return pl.pallas_call(
        _combine_kernel,
        grid=(nt,),
        in_specs=[
            pl.BlockSpec((None, 1, 2 * tm), lambda i: (i, 0, 0), memory_space=pltpu.SMEM),
            pl.BlockSpec((None, 1, 2 * tm), lambda i: (jnp.minimum(i + 1, last), 0, 0), memory_space=pltpu.SMEM),
            row(d), row(LANE),
            pl.BlockSpec((None, N_MOD, d), mod_idx),
            pl.BlockSpec((1, d), lambda i: (0, 0)),
            pl.BlockSpec(memory_space=pl.ANY),
        ],
        out_specs=[row(d), row(d)],
        out_shape=[jax.ShapeDtypeStruct((r, d), F32), jax.ShapeDtypeStruct((r, d), F32)],
        scratch_shapes=[pltpu.VMEM((2, 2 * tm, d), F32), pltpu.SemaphoreType.DMA((2,))],
        compiler_params=_cparams("arbitrary"),
        name="moe_combine",
    )(pos, pos, xall, route, mod, g_final, yb)


def _dispatch_plan(eid, n_tiles):
    n = eid.shape[0]
    a = n * TOP_K
    bm = MOE_BLOCK
    n_blocks = -(-(a + N_EXPERTS * (bm - 1)) // bm)
    flat_e = eid.reshape(a)
    order = jnp.argsort(flat_e, stable=True).astype(jnp.int32)
    se = flat_e[order]
    counts = jnp.bincount(flat_e, length=N_EXPERTS).astype(jnp.int32)
    start = jnp.cumsum(counts) - counts
    padded = (counts + bm - 1) // bm * bm
    pend = jnp.cumsum(padded)
    dest = (pend - padded)[se] + jnp.arange(a, dtype=jnp.int32) - start[se]
    buf_tok = jnp.zeros((n_blocks * bm,), jnp.int32).at[dest].set(order // TOP_K)
    pos = jnp.zeros((a,), jnp.int32).at[order].set(dest).reshape(n, TOP_K)
    block_e = jnp.minimum(jnp.searchsorted(pend, jnp.arange(n_blocks, dtype=jnp.int32) * bm, side='right'),
                          N_EXPERTS - 1).astype(jnp.int32)
    n_live = (pend[-1] // bm).astype(jnp.int32).reshape(1)
    pos_tiles = pos.reshape(n_tiles, n // n_tiles, TOP_K).transpose(0, 2, 1).reshape(n_tiles, 1, -1)
    return block_e, n_live, buf_tok.reshape(n_blocks, 1, bm), pos_tiles


def _rope_tables(seq, ctx_rows):
    t = jnp.arange(seq - ctx_rows)
    inv = ROPE_BASE ** (-jnp.arange(0, ROT_AX, 2, dtype=F32) / ROT_AX)
    ang_r = (t // GRID_W).astype(F32)[:, None] * inv
    ang_c = (t % GRID_W).astype(F32)[:, None] * inv
    cos64 = jnp.concatenate([jnp.cos(ang_r)] * 2 + [jnp.cos(ang_c)] * 2, axis=1)
    sin64 = jnp.concatenate([-jnp.sin(ang_r), jnp.sin(ang_r), -jnp.sin(ang_c), jnp.sin(ang_c)], axis=1)
    cos = jnp.concatenate([jnp.ones((ctx_rows, DK), F32), cos64], axis=0)
    sin = jnp.concatenate([jnp.zeros((ctx_rows, DK), F32), sin64], axis=0)
    return jnp.tile(cos, (1, LANE // DK)), jnp.tile(sin, (1, LANE // DK))


def kernel(x, c, ctx, c_ctx, w_mod, b_mod, g_norm1, g_norm2, w_in, diff_lambda, g_diff_subln, w_gmlp_s, b_gmlp_s, g_gmlp_v, w_conv_m, b_conv_m, w_qkv_m, w_if_m, b_if_m, g_mlstm_norm, skip_m, w_branch, w_out, w_route_g, b_route_g, w_route_e, b_route_e, w_e_gate, w_e_up, w_e_down, g_final):
    n_batch, t_lat, d = x.shape
    ctx_rows = ctx.shape[1]
    depth = w_in.shape[0]
    seq = ctx_rows + t_lat
    assert d == D_MODEL and ctx_rows == ROW_TILE and t_lat % ROW_TILE == 0 and t_lat % GRID_W == 0
    tps = seq // ROW_TILE
    n_tiles = n_batch * tps

    xall = jnp.concatenate([ctx, x], axis=1).reshape(n_batch * seq, d)
    mb = -(-(n_batch + 1) // 8) * 8
    c_all = jnp.zeros((mb, d), F32).at[:n_batch].set(c).at[n_batch].set(c_ctx)
    mod_all = _modulation(c_all, w_mod, b_mod)[:, :n_batch + 1].reshape(depth, n_batch + 1, N_MOD, d)
    cos, sin_signed = _rope_tables(seq, ctx_rows)

    xfinal = None
    for l in range(depth):
        lam_init = 0.8 - 0.6 * math.exp(-0.3 * l)
        mod = mod_all[l]
        w_p = jnp.concatenate([w_in[l][:, ORIG_GATE0:], w_in[l][:, :ORIG_GATE0]], axis=1).astype(MXU_DT)
        p = _inproj(xall, g_norm1[l][None], mod, cos, sin_signed, w_p, n_batch, tps)

        ya = _attention(p, diff_lambda[l], g_diff_subln[l][None], lam_init, n_batch, tps, ctx_rows)

        b_full = jnp.repeat(b_gmlp_s[l].T, W_B // G_B, axis=1)
        yb = _gmlp(p, g_gmlp_v[l][None], w_gmlp_s[l].astype(MXU_DT), b_full)

        w_if_pad = jnp.zeros((3 * W_C, LANE), F32).at[:, :4 * H_C].set(w_if_m[l]).astype(MXU_DT)
        b_if_pad = jnp.zeros((1, LANE), F32).at[0, :4 * H_C].set(b_if_m[l])
        xconv, q_m, k_m, v_m, gates = _mlstm_features(
            p, w_conv_m[l], b_conv_m[l][None], w_qkv_m[l].astype(MXU_DT), w_if_pad, b_if_pad, tps)
        hf, hb = _mlstm_scan(q_m, k_m, v_m, gates, n_batch, seq, ctx_rows)

        w_route = (jnp.zeros((d, LANE), F32).at[:, :N_GROUPS].set(w_route_g[l])
                   .at[:, N_GROUPS:N_GROUPS + N_EXPERTS].set(w_route_e[l]))
        b_route = (jnp.zeros((1, LANE), F32).at[0, :N_GROUPS].set(b_route_g[l])
                   .at[0, N_GROUPS:N_GROUPS + N_EXPERTS].set(b_route_e[l]))
        xall, h2, route = _merge(xall, p, ya, yb, hf, hb, xconv, g_mlstm_norm[l][None], skip_m[l][None],
                                 w_branch[l].astype(MXU_DT), w_out[l].astype(MXU_DT), mod, g_norm2[l][None],
                                 w_route, b_route, n_batch, tps)

        eid = route[:, 0:TOP_K].astype(jnp.int32)
        block_e, n_live, buf_tok, pos_tiles = _dispatch_plan(eid, n_tiles)
        y_sorted = _moe_experts(h2, block_e, n_live, buf_tok, w_e_gate[l].astype(MXU_DT),
                                w_e_up[l].astype(MXU_DT), w_e_down[l].astype(MXU_DT))
        xall, xfinal = _combine(xall, route, mod, g_final[None], y_sorted, pos_tiles, n_batch, tps)

    return xfinal.reshape(n_batch, seq, d)[:, ctx_rows:]
```

```python
import functools
import math

import jax
import jax.numpy as jnp
from jax import lax
from jax.experimental import pallas as pl
from jax.experimental.pallas import tpu as pltpu

F32 = jnp.float32
MXU_DT = jnp.bfloat16
ACT_DT = jnp.bfloat16
HIGHEST = lax.Precision.HIGHEST

D_MODEL = 1024
N_MOD = 6
EPS = 1e-6
GRID_W = 64
H_A, DK = 4, 64
DV = 2 * DK
ROT_AX = DK // 2
ROPE_BASE = 10000.0
W_B, G_B, CHUNK_B = 512, 4, 128
H_C, DH_C, CONV_K, MLSTM_CHUNK = 4, 128, 3, 128
W_C = H_C * DH_C
N_GROUPS, EXP_PER_GROUP, TOP_K, D_EXPERT = 4, 8, 2, 512
N_EXPERTS = N_GROUPS * EXP_PER_GROUP
W_BR = 512

GATE_W = 3 * D_MODEL
P_GATE = 0
P_K = GATE_W
P_V = P_K + 512
P_XM = P_V + 512
P_Q = P_XM + 512
P_UV = P_Q + 512
P_Z = P_UV + 2 * W_B
IN_COLS = P_Z + W_C
ORIG_GATE0 = IN_COLS - GATE_W

LANE = 128
ROW_TILE = 256
PROJ_CHUNK = 512
MOE_BLOCK = 256
HALO = 16
DMA_UNROLL = 8
VMEM_LIMIT = 56 * 1024 * 1024
NEG_BIG = -1e30


def _cparams(*sem):
    return pltpu.CompilerParams(dimension_semantics=sem, vmem_limit_bytes=VMEM_LIMIT)


def _rms(x):
    return x * lax.rsqrt(jnp.mean(x * x, axis=-1, keepdims=True) + EPS)


def _mod_kernel(c_ref, w_ref, b_ref, o_ref):
    c = c_ref[...]
    s = c * jax.nn.sigmoid(c)
    o_ref[...] = jnp.dot(s, w_ref[...], preferred_element_type=F32, precision=HIGHEST) + b_ref[...]


def _modulation(c_all, w_mod, b_mod):
    n_layer = w_mod.shape[0]
    mb, d = c_all.shape
    tn = 1024
    return pl.pallas_call(
        _mod_kernel,
        grid=(n_layer, N_MOD * d // tn),
        in_specs=[
            pl.BlockSpec((mb, d), lambda l, j: (0, 0)),
            pl.BlockSpec((None, d, tn), lambda l, j: (l, 0, j)),
            pl.BlockSpec((None, 1, tn), lambda l, j: (l, 0, j)),
        ],
        out_specs=pl.BlockSpec((None, mb, tn), lambda l, j: (l, 0, j)),
        out_shape=jax.ShapeDtypeStruct((n_layer, mb, N_MOD * d), F32),
        compiler_params=_cparams("parallel", "parallel"),
        name="modulation",
    )(c_all, w_mod, b_mod.reshape(n_layer, 1, N_MOD * d))


def _rope(acc, cos, sin_signed):
    w = acc.shape[1]
    lane = lax.broadcasted_iota(jnp.int32, acc.shape, 1)
    partner = jnp.where((lane & 16) == 0, pltpu.roll(acc, w - 16, 1), pltpu.roll(acc, 16, 1))
    reps = w // cos.shape[1]
    return acc * jnp.tile(cos, (1, reps)) + partner * jnp.tile(sin_signed, (1, reps))


def _inproj_kernel(x_ref, g_ref, mod_ref, cos_ref, sin_ref, w_ref, o_ref):
    x = x_ref[...]
    h = (_rms(x) * g_ref[...] * (1.0 + mod_ref[1:2, :]) + mod_ref[0:1, :]).astype(MXU_DT)
    k_chunk = P_K // PROJ_CHUNK
    q_chunk = P_Q // PROJ_CHUNK
    for c in range(IN_COLS // PROJ_CHUNK):
        cols = slice(c * PROJ_CHUNK, (c + 1) * PROJ_CHUNK)
        acc = jnp.dot(h, w_ref[:, cols], preferred_element_type=F32)
        if c == k_chunk:
            acc = _rope(acc, cos_ref[...], sin_ref[...])
        elif c == q_chunk:
            acc = _rope(acc, cos_ref[...], sin_ref[...]) * (DK ** -0.5)
        o_ref[:, cols] = acc.astype(o_ref.dtype)


def _inproj(xall, g1, mod, cos, sin_signed, w_p, n_batch, tps):
    r, d = xall.shape
    tm = ROW_TILE

    def mod_idx(i):
        return (jnp.where(i % tps == 0, n_batch, i // tps), 0, 0)

    return pl.pallas_call(
        _inproj_kernel,
        grid=(r // tm,),
        in_specs=[
            pl.BlockSpec((tm, d), lambda i: (i, 0)),
            pl.BlockSpec((1, d), lambda i: (0, 0)),
            pl.BlockSpec((None, N_MOD, d), mod_idx),
            pl.BlockSpec((tm, LANE), lambda i: (i % tps, 0)),
            pl.BlockSpec((tm, LANE), lambda i: (i % tps, 0)),
            pl.BlockSpec((d, IN_COLS), lambda i: (0, 0)),
        ],
        out_specs=pl.BlockSpec((tm, IN_COLS), lambda i: (i, 0)),
        out_shape=jax.ShapeDtypeStruct((r, IN_COLS), ACT_DT),
        compiler_params=_cparams("parallel"),
        name="inproj",
    )(xall, g1, mod, cos, sin_signed, w_p)


def _attn_kernel(dl_ref, g_ref, q_ref, k_ref, v_ref, o_ref, *, lam_init, ctx_rows):
    dl = dl_ref[...]
    lam = (jnp.exp(jnp.sum(dl[0:1] * dl[1:2], keepdims=True))
           - jnp.exp(jnp.sum(dl[2:3] * dl[3:4], keepdims=True)) + lam_init)
    nt = (((1,), (1,)), ((), ()))

    def attend(kv_rows):
        q = q_ref[...]
        k = k_ref[0:kv_rows, :]
        lane = lax.broadcasted_iota(jnp.int32, q.shape, 1)
        zero = jnp.zeros_like(q)

        def softmax_parts(qm):
            s = lax.dot_general(qm, k, nt, preferred_element_type=F32)
            p = jnp.exp(s - jnp.max(s, axis=-1, keepdims=True))
            return p, jnp.sum(p, axis=-1, keepdims=True)

        p0, l0 = softmax_parts(jnp.where(lane < DK, q, zero))
        p1, l1 = softmax_parts(jnp.where(lane >= DK, q, zero))
        a = p0 * (1.0 / l0) - p1 * (lam / l1)
        o = jnp.dot(a.astype(MXU_DT), v_ref[0:kv_rows, :], preferred_element_type=F32)
        o_ref[...] = (_rms(o) * g_ref[...] * (1.0 - lam_init)).astype(o_ref.dtype)

    is_ctx = pl.program_id(2) == 0

    @pl.when(is_ctx)
    def _():
        attend(ctx_rows)

    @pl.when(jnp.logical_not(is_ctx))
    def _():
        attend(k_ref.shape[0])


def _attention(p, diff_lambda, g_sub, lam_init, n_batch, tps, ctx_rows):
    tq = ROW_TILE
    seq = tps * tq
    kern = functools.partial(_attn_kernel, lam_init=lam_init, ctx_rows=ctx_rows)
    return pl.pallas_call(
        kern,
        grid=(n_batch, H_A, tps),
        in_specs=[
            pl.BlockSpec((4, DK), lambda b, h, i: (0, 0)),
            pl.BlockSpec((1, DV), lambda b, h, i: (0, 0)),
            pl.BlockSpec((tq, LANE), lambda b, h, i: (b * tps + i, P_Q // LANE + h)),
            pl.BlockSpec((seq, LANE), lambda b, h, i: (b, P_K // LANE + h)),
            pl.BlockSpec((seq, LANE), lambda b, h, i: (b, P_V // LANE + h)),
        ],
        out_specs=pl.BlockSpec((tq, LANE), lambda b, h, i: (b * tps + i, h)),
        out_shape=jax.ShapeDtypeStruct((p.shape[0], H_A * DV), ACT_DT),
        compiler_params=_cparams("parallel", "parallel", "arbitrary"),
        name="diff_attention",
    )(diff_lambda, g_sub, p, p, p)


def _gmlp_kernel(uv_ref, gv_ref, ws_ref, bs_ref, o_ref):
    uv = jax.nn.gelu(uv_ref[...].astype(F32))
    u = uv[:, :W_B]
    v = (_rms(uv[:, W_B:]) * gv_ref[...]).astype(MXU_DT)
    gw = W_B // G_B
    for c in range(uv.shape[0] // CHUNK_B):
        rows = slice(c * CHUNK_B, (c + 1) * CHUNK_B)
        for g in range(G_B):
            cols = slice(g * gw, (g + 1) * gw)
            mixed = jnp.dot(ws_ref[g], v[rows, cols], preferred_element_type=F32) + bs_ref[:, cols]
            o_ref[rows, cols] = (u[rows, cols] * mixed).astype(o_ref.dtype)


def _gmlp(p, g_v, w_s, b_full):
    r = p.shape[0]
    tm = ROW_TILE
    return pl.pallas_call(
        _gmlp_kernel,
        grid=(r // tm,),
        in_specs=[
            pl.BlockSpec((tm, 2 * W_B), lambda i: (i, P_UV // (2 * W_B))),
            pl.BlockSpec((1, W_B), lambda i: (0, 0)),
            pl.BlockSpec((G_B, CHUNK_B, CHUNK_B), lambda i: (0, 0, 0)),
            pl.BlockSpec((CHUNK_B, W_B), lambda i: (0, 0)),
        ],
        out_specs=pl.BlockSpec((tm, W_B), lambda i: (i, 0)),
        out_shape=jax.ShapeDtypeStruct((r, W_B), ACT_DT),
        compiler_params=_cparams("parallel"),
        name="gmlp",
    )(p, g_v, w_s, b_full)


def _mfeat_kernel(x_ref, prev_ref, next_ref, wc_ref, bc_ref, wqkv_ref, wif_ref, bif_ref,
                  xc_ref, q_ref, k_ref, v_ref, g_ref, *, tps):
    tm = x_ref.shape[0]
    j = pl.program_id(0) % tps
    seg_start = jnp.logical_or(j == 0, j == 1)
    seg_end = jnp.logical_or(j == 0, j == tps - 1)
    x = x_ref[...].astype(F32)
    prow = jnp.where(seg_start, 0.0, prev_ref[...].astype(F32)[HALO - 1:HALO, :])
    nrow = jnp.where(seg_end, 0.0, next_ref[...].astype(F32)[0:1, :])
    rid = lax.broadcasted_iota(jnp.int32, x.shape, 0)
    xp = jnp.where(rid == 0, prow, pltpu.roll(x, 1, 0))
    xn = jnp.where(rid == tm - 1, nrow, pltpu.roll(x, tm - 1, 0))
    xc = wc_ref[0:1, :] * xp + wc_ref[1:2, :] * x + wc_ref[2:3, :] * xn + bc_ref[...]
    xc = xc * jax.nn.sigmoid(xc)
    xc_ref[...] = xc.astype(xc_ref.dtype)
    xcb = xc.astype(MXU_DT)
    xb = x_ref[...]
    qs, ks, vs = [], [], []
    for h in range(H_C):
        cols = slice(h * DH_C, (h + 1) * DH_C)
        qs.append(jnp.dot(xcb[:, cols], wqkv_ref[0, h], preferred_element_type=F32))
        ks.append(jnp.dot(xcb[:, cols], wqkv_ref[1, h], preferred_element_type=F32))
        vs.append(jnp.dot(xb[:, cols], wqkv_ref[2, h], preferred_element_type=F32))
    q = jnp.concatenate(qs, axis=1)
    k = jnp.concatenate(ks, axis=1)
    v = jnp.concatenate(vs, axis=1)
    qkv = jnp.concatenate([q, k, v], axis=1).astype(MXU_DT)
    g_ref[...] = jnp.dot(qkv, wif_ref[...], preferred_element_type=F32) + bif_ref[...]
    q_ref[...] = q.astype(q_ref.dtype)
    k_ref[...] = (k * (DH_C ** -0.5)).astype(k_ref.dtype)
    v_ref[...] = v.astype(v_ref.dtype)


def _mlstm_features(p, w_conv, b_conv, w_qkv, w_if_pad, b_if_pad, tps):
    r = p.shape[0]
    tm = ROW_TILE
    hpt = tm // HALO
    xm_blk = P_XM // W_C
    last_halo = r // HALO - 1
    kern = functools.partial(_mfeat_kernel, tps=tps)
    act = jax.ShapeDtypeStruct((r, W_C), ACT_DT)
    row_spec = pl.BlockSpec((tm, W_C), lambda i: (i, 0))
    return pl.pallas_call(
        kern,
        grid=(r // tm,),
        in_specs=[
            pl.BlockSpec((tm, W_C), lambda i: (i, xm_blk)),
            pl.BlockSpec((HALO, W_C), lambda i: (jnp.maximum(i * hpt - 1, 0), xm_blk)),
            pl.BlockSpec((HALO, W_C), lambda i: (jnp.minimum((i + 1) * hpt, last_halo), xm_blk)),
            pl.BlockSpec((CONV_K, W_C), lambda i: (0, 0)),
            pl.BlockSpec((1, W_C), lambda i: (0, 0)),
            pl.BlockSpec((3, H_C, DH_C, DH_C), lambda i: (0, 0, 0, 0)),
            pl.BlockSpec((3 * W_C, LANE), lambda i: (0, 0)),
            pl.BlockSpec((1, LANE), lambda i: (0, 0)),
        ],
        out_specs=[row_spec, row_spec, row_spec, row_spec,
                   pl.BlockSpec((tm, LANE), lambda i: (i, 0))],
        out_shape=[act, act, act, act, jax.ShapeDtypeStruct((r, LANE), F32)],
        compiler_params=_cparams("parallel"),
        name="mlstm_features",
    )(p, p, p, w_conv, b_conv, w_qkv, w_if_pad, b_if_pad)


def _lane_scan(x, op, reverse):
    n = x.shape[1]
    lane = lax.broadcasted_iota(jnp.int32, x.shape, 1)
    sh = 1
    while sh < n:
        if reverse:
            shifted, valid = pltpu.roll(x, n - sh, 1), lane < n - sh
        else:
            shifted, valid = pltpu.roll(x, sh, 1), lane >= sh
        x = jnp.where(valid, op(x, shifted), x)
        sh *= 2
    return x


def _mlstm_direction(gt, q_ref, k_ref, v_ref, c_ref, m_ref, h_ref, reverse):
    ln = gt.shape[1]
    nr = 2 * H_C
    ig = gt
    bcum = _lane_scan(jax.nn.log_sigmoid(pltpu.roll(gt, H_C, 0)), jnp.add, reverse)
    r = ig - bcum
    cmx = _lane_scan(r, jnp.maximum, reverse)
    g = bcum[:, 0:1] if reverse else bcum[:, ln - 1:ln]
    a_end = g + r
    m_loc = jnp.max(a_end, axis=1, keepdims=True)
    w_end = jnp.exp(a_end - m_loc)
    m_old = m_ref[:, 0:1]
    m_new = jnp.maximum(g + m_old, m_loc)
    a_old = jnp.exp(g + m_old - m_new)
    a_new = jnp.exp(m_loc - m_new)
    zt = jnp.concatenate([bcum, cmx, w_end, jnp.zeros((LANE - 3 * nr, ln), F32)], axis=0).T
    t_id = lax.broadcasted_iota(jnp.int32, (ln, ln), 0)
    s_id = lax.broadcasted_iota(jnp.int32, (ln, ln), 1)
    keep = (s_id >= t_id) if reverse else (s_id <= t_id)
    ones_col = (lax.broadcasted_iota(jnp.int32, (ln, DH_C), 1) == 0).astype(F32)
    nt = (((1,), (1,)), ((), ()))
    for h in range(H_C):
        cols = slice(h * DH_C, (h + 1) * DH_C)
        q = q_ref[:, cols]
        k = k_ref[:, cols]
        vext = jnp.concatenate([v_ref[:, cols].astype(F32), ones_col], axis=1)
        bc = zt[:, h:h + 1]
        cm = zt[:, nr + h:nr + h + 1]
        we = zt[:, 2 * nr + h:2 * nr + h + 1]
        m0 = m_old[h:h + 1, :]
        log_inter = bc + m0
        m_t = jnp.maximum(log_inter, bc + cm)
        dmat = jnp.exp(jnp.where(keep, bc + r[h:h + 1, :] - m_t, NEG_BIG))
        s = lax.dot_general(q, k, nt, preferred_element_type=F32) * dmat
        cext = c_ref[h]
        r1 = jnp.dot(s.astype(MXU_DT), vext.astype(MXU_DT), preferred_element_type=F32)
        r2 = lax.dot_general(q, cext.astype(MXU_DT), nt, preferred_element_type=F32)
        w_inter = jnp.exp(log_inter - m_t)
        num = r1[:, :DH_C] + w_inter * r2[:, :DH_C]
        den = r1[:, DH_C:DH_C + 1] + w_inter * r2[:, DH_C:DH_C + 1]
        hh = num / jnp.maximum(jnp.abs(den), jnp.exp(-m_t))
        h_ref[:, cols] = hh.astype(h_ref.dtype)
        vw_t = (vext * we).T.astype(MXU_DT)
        c_loc = jnp.dot(vw_t, k, preferred_element_type=F32)
        c_ref[h] = a_old[h:h + 1, :] * cext + a_new[h:h + 1, :] * c_loc
    m_ref[...] = jnp.broadcast_to(m_new, m_ref.shape)


def _mscan_kernel(gf_ref, qf_ref, kf_ref, vf_ref, gb_ref, qb_ref, kb_ref, vb_ref,
                  hf_ref, hb_ref, cf_ref, mf_ref, cb_ref, mb_ref):
    @pl.when(pl.program_id(1) == 0)
    def _():
        cf_ref[...] = jnp.zeros_like(cf_ref)
        cb_ref[...] = jnp.zeros_like(cb_ref)
        mf_ref[...] = jnp.zeros_like(mf_ref)
        mb_ref[...] = jnp.zeros_like(mb_ref)

    _mlstm_direction(gf_ref[...].T[0:2 * H_C], qf_ref, kf_ref, vf_ref, cf_ref, mf_ref, hf_ref, False)
    _mlstm_direction(gb_ref[...].T[2 * H_C:4 * H_C], qb_ref, kb_ref, vb_ref, cb_ref, mb_ref, hb_ref, True)


def _mlstm_scan(q, k, v, gates, n_batch, seq, ctx_rows):
    ln = MLSTM_CHUNK
    nck = seq // ln
    ncc = ctx_rows // ln

    def fwd(b, s):
        return (b * nck + s, 0)

    def bwd(b, s):
        return (b * nck + jnp.where(s < ncc, ncc - 1 - s, nck - 1 + ncc - s), 0)

    act = lambda idx: pl.BlockSpec((ln, W_C), idx)
    gsp = lambda idx: pl.BlockSpec((ln, LANE), idx)
    out = jax.ShapeDtypeStruct(q.shape, ACT_DT)
    return pl.pallas_call(
        _mscan_kernel,
        grid=(n_batch, nck),
        in_specs=[gsp(fwd), act(fwd), act(fwd), act(fwd), gsp(bwd), act(bwd), act(bwd), act(bwd)],
        out_specs=[act(fwd), act(bwd)],
        out_shape=[out, out],
        scratch_shapes=[
            pltpu.VMEM((H_C, 2 * DH_C, DH_C), F32), pltpu.VMEM((8, LANE), F32),
            pltpu.VMEM((H_C, 2 * DH_C, DH_C), F32), pltpu.VMEM((8, LANE), F32),
        ],
        compiler_params=_cparams("parallel", "arbitrary"),
        name="mlstm_scan",
    )(gates, q, k, v, gates, q, k, v)


def _route(logits, carry):
    lane = lax.broadcasted_iota(jnp.int32, logits.shape, 1).astype(F32)
    big = float(4 * LANE)
    gl = jnp.where(lane < N_GROUPS, logits, NEG_BIG)
    gmax = jnp.max(gl, axis=-1, keepdims=True)
    g_star = jnp.min(jnp.where(gl == gmax, lane, big), axis=-1, keepdims=True)
    p_g = 1.0 / jnp.sum(jnp.exp(gl - gmax), axis=-1, keepdims=True)
    e_lo = g_star * EXP_PER_GROUP + N_GROUPS
    in_group = jnp.logical_and(lane >= e_lo, lane < e_lo + EXP_PER_GROUP)
    el = jnp.where(in_group, logits, NEG_BIG)
    v1 = jnp.max(el, axis=-1, keepdims=True)
    i1 = jnp.min(jnp.where(el == v1, lane, big), axis=-1, keepdims=True)
    el2 = jnp.where(lane == i1, NEG_BIG, el)
    v2 = jnp.max(el2, axis=-1, keepdims=True)
    i2 = jnp.min(jnp.where(el2 == v2, lane, big), axis=-1, keepdims=True)
    e21 = jnp.exp(v2 - v1)
    w1 = p_g / (1.0 + e21)
    w2 = p_g * e21 / (1.0 + e21)
    e1 = i1 - N_GROUPS
    e2 = i2 - N_GROUPS
    oh1 = lane == e1
    oh2 = lane == e2
    both = jnp.where(jnp.logical_or(oh1, oh2), 1.0, 0.0)
    tm = logits.shape[0]
    lower = (lax.broadcasted_iota(jnp.int32, (tm, tm), 1) < lax.broadcasted_iota(jnp.int32, (tm, tm), 0))
    before = jnp.dot(jnp.where(lower, 1.0, 0.0).astype(MXU_DT), both.astype(MXU_DT),
                     preferred_element_type=F32) + carry
    rank1 = jnp.sum(jnp.where(oh1, before, 0.0), axis=-1, keepdims=True)
    rank2 = jnp.sum(jnp.where(oh2, before, 0.0), axis=-1, keepdims=True)
    out = jnp.where(lane == 0, e1, 0.0)
    out = jnp.where(lane == 1, e2, out)
    out = jnp.where(lane == 2, w1, out)
    out = jnp.where(lane == 3, w2, out)
    out = jnp.where(lane == 4, rank1, out)
    out = jnp.where(lane == 5, rank2, out)
    return out, carry + jnp.sum(both, axis=0, keepdims=True)


def _merge_kernel(x_ref, gate_ref, z_ref, ya_ref, yb_ref, hf_ref, hb_ref, xc_ref, gm_ref, sk_ref,
                  wbr_ref, wo_ref, mod_ref, g2_ref, wr_ref, br_ref, xo_ref, h2_ref, rt_ref, cnt_ref, carry_ref):
    @pl.when(pl.program_id(0) == 0)
    def _():
        carry_ref[...] = jnp.zeros_like(carry_ref)

    hs = hf_ref[...].astype(F32) + hb_ref[...].astype(F32)
    hn = jnp.concatenate([_rms(hs[:, h * DH_C:(h + 1) * DH_C]) for h in range(H_C)], axis=1) * gm_ref[...]
    yc = (hn + sk_ref[...] * xc_ref[...].astype(F32)) * jax.nn.sigmoid(z_ref[...].astype(F32))
    ys = (ya_ref[...], yb_ref[...], yc.astype(MXU_DT))
    merged = None
    for i in range(3):
        gate = jax.nn.sigmoid(gate_ref[:, i * D_MODEL:(i + 1) * D_MODEL].astype(F32))
        term = gate * jnp.dot(ys[i], wbr_ref[i], preferred_element_type=F32)
        merged = term if merged is None else merged + term
    y = jnp.dot(merged.astype(MXU_DT), wo_ref[...], preferred_element_type=F32)
    xn = x_ref[...] + mod_ref[2:3, :] * y
    xo_ref[...] = xn
    h2 = _rms(xn) * g2_ref[...] * (1.0 + mod_ref[4:5, :]) + mod_ref[3:4, :]
    h2_ref[...] = h2
    logits = jnp.dot(h2, wr_ref[...], preferred_element_type=F32, precision=HIGHEST) + br_ref[...]
    route, carry = _route(logits, carry_ref[0:1, :])
    rt_ref[...] = route
    carry_ref[0:1, :] = carry
    cnt_ref[...] = jnp.broadcast_to(carry, cnt_ref.shape)


def _merge(xall, p, ya, yb, hf, hb, xconv, g_m, skip, w_br, w_o, mod, g2, w_route, b_route, n_batch, tps):
    r, d = xall.shape
    tm = ROW_TILE

    def mod_idx(i):
        return (jnp.where(i % tps == 0, n_batch, i // tps), 0, 0)

    row = lambda w: pl.BlockSpec((tm, w), lambda i: (i, 0))
    full = lambda shape: pl.BlockSpec(shape, lambda i: (0,) * len(shape))
    return pl.pallas_call(
        _merge_kernel,
        grid=(r // tm,),
        in_specs=[
            row(d),
            pl.BlockSpec((tm, GATE_W), lambda i: (i, P_GATE // GATE_W)),
            pl.BlockSpec((tm, W_C), lambda i: (i, P_Z // W_C)),
            row(W_BR), row(W_BR), row(W_C), row(W_C), row(W_C),
            full((1, W_C)), full((1, W_C)),
            full((3, W_BR, d)), full((d, d)),
            pl.BlockSpec((None, N_MOD, d), mod_idx),
            full((1, d)), full((d, LANE)), full((1, LANE)),
        ],
        out_specs=[row(d), row(d), row(LANE), full((8, LANE))],
        out_shape=[jax.ShapeDtypeStruct((r, d), F32), jax.ShapeDtypeStruct((r, d), F32),
                   jax.ShapeDtypeStruct((r, LANE), F32), jax.ShapeDtypeStruct((8, LANE), F32)],
        scratch_shapes=[pltpu.VMEM((8, LANE), F32)],
        compiler_params=_cparams("arbitrary"),
        name="merge_route",
    )(xall, p, p, ya, yb, hf, hb, xconv, g_m, skip, w_br, w_o, mod, g2, w_route, b_route)


def _row_copy(src_hbm, src_row, dst_ref, dst_row, sem):
    return pltpu.make_async_copy(src_hbm.at[pl.ds(src_row, 1), :], dst_ref.at[pl.ds(dst_row, 1), :], sem)


def _dispatch_kernel(fill_ref, nlive_ref, pos_ref, h_hbm, xs_hbm, zbuf, zsem, sem):
    i = pl.program_id(0)
    n = pl.num_programs(0)
    tm = pos_ref.shape[1] // TOP_K
    base = i * tm
    bm = zbuf.shape[0]
    n_blocks = xs_hbm.shape[0] // bm

    @pl.when(i == 0)
    def _():
        zbuf[...] = jnp.zeros_like(zbuf)

        def fill_block(b):
            return pltpu.make_async_copy(zbuf, xs_hbm.at[pl.ds(b * bm, bm), :], zsem)

        def fill_last(k, carry):
            fill_block(fill_ref[k]).start()
            return carry

        def fill_unused(b, carry):
            fill_block(b).start()
            return carry

        def wait_fill(k, carry):
            fill_block(0).wait()
            return carry

        n_live = nlive_ref[0]
        n_used = nlive_ref[1]
        lax.fori_loop(0, n_used, fill_last, 0)
        lax.fori_loop(n_live, n_blocks, fill_unused, 0)
        lax.fori_loop(0, n_used + n_blocks - n_live, wait_fill, 0)

    def body(c, carry):
        for u in range(DMA_UNROLL):
            r = c * DMA_UNROLL + u
            for j in range(TOP_K):
                _row_copy(h_hbm, base + r, xs_hbm, pos_ref[0, j * tm + r], sem).start()
        return carry

    lax.fori_loop(0, tm // DMA_UNROLL, body, 0)

    def wait_tile():
        pltpu.make_async_copy(h_hbm.at[pl.ds(0, TOP_K * tm), :], xs_hbm.at[pl.ds(0, TOP_K * tm), :], sem).wait()

    @pl.when(i > 0)
    def _():
        wait_tile()

    @pl.when(i == n - 1)
    def _():
        wait_tile()


def _moe_dispatch(h2, fill_blocks, n_live, pos_tiles, n_slots):
    n, d = h2.shape
    n_tiles = pos_tiles.shape[0]
    grid_spec = pltpu.PrefetchScalarGridSpec(
        num_scalar_prefetch=2,
        grid=(n_tiles,),
        in_specs=[
            pl.BlockSpec((None, 1, pos_tiles.shape[2]), lambda i, fs, nl: (i, 0, 0), memory_space=pltpu.SMEM),
            pl.BlockSpec(memory_space=pl.ANY),
        ],
        out_specs=pl.BlockSpec(memory_space=pl.ANY),
        scratch_shapes=[pltpu.VMEM((MOE_BLOCK, d), h2.dtype), pltpu.SemaphoreType.DMA(()),
                        pltpu.SemaphoreType.DMA(())],
    )
    return pl.pallas_call(
        _dispatch_kernel,
        grid_spec=grid_spec,
        out_shape=jax.ShapeDtypeStruct((n_slots, d), h2.dtype),
        compiler_params=_cparams("arbitrary"),
        name="moe_dispatch",
    )(fill_blocks, n_live, pos_tiles, h2)


def _moe_kernel(be_ref, nv_ref, x_ref, wg_ref, wu_ref, wd_ref, y_ref):
    n_valid = nv_ref[pl.program_id(0)]

    @pl.when(n_valid > 0)
    def _():
        x = x_ref[...].astype(MXU_DT)
        a = jnp.dot(x, wg_ref[...], preferred_element_type=F32)
        u = jnp.dot(x, wu_ref[...], preferred_element_type=F32)
        act = (a * jax.nn.sigmoid(a) * u).astype(MXU_DT)
        y_ref[...] = jnp.dot(act, wd_ref[...], preferred_element_type=F32)

    @pl.when(n_valid <= 0)
    def _():
        y_ref[...] = jnp.zeros_like(y_ref)


def _moe_experts(xs, block_e, n_valid, w_gate, w_up, w_down):
    n_slots, d = xs.shape
    bm = MOE_BLOCK
    grid_spec = pltpu.PrefetchScalarGridSpec(
        num_scalar_prefetch=2,
        grid=(n_slots // bm,),
        in_specs=[
            pl.BlockSpec((bm, d), lambda i, be, nv: (i, 0)),
            pl.BlockSpec((None, d, D_EXPERT), lambda i, be, nv: (be[i], 0, 0)),
            pl.BlockSpec((None, d, D_EXPERT), lambda i, be, nv: (be[i], 0, 0)),
            pl.BlockSpec((None, D_EXPERT, d), lambda i, be, nv: (be[i], 0, 0)),
        ],
        out_specs=pl.BlockSpec((bm, d), lambda i, be, nv: (i, 0)),
    )
    return pl.pallas_call(
        _moe_kernel,
        grid_spec=grid_spec,
        out_shape=jax.ShapeDtypeStruct((n_slots, d), F32),
        compiler_params=_cparams("arbitrary"),
        name="moe_experts",
    )(block_e, n_valid, xs, w_gate, w_up, w_down)


def _combine_kernel(pos_ref, pos_next_ref, x_ref, rt_ref, mod_ref, gf_ref, yb_hbm, xo_ref, ybuf, sem, *, final):
    i = pl.program_id(0)
    n = pl.num_programs(0)
    tm = x_ref.shape[0]

    def start_gather(ids_ref, slot):
        def body(c, carry):
            for u in range(DMA_UNROLL):
                r = c * DMA_UNROLL + u
                _row_copy(yb_hbm, ids_ref[0, r], ybuf.at[slot], r, sem.at[slot]).start()
            return carry
        lax.fori_loop(0, TOP_K * tm // DMA_UNROLL, body, 0)

    @pl.when(i == 0)
    def _():
        start_gather(pos_ref, 0)

    @pl.when(i + 1 < n)
    def _():
        start_gather(pos_next_ref, (i + 1) % 2)

    slot = i % 2
    pltpu.make_async_copy(yb_hbm.at[pl.ds(0, TOP_K * tm), :], ybuf.at[slot], sem.at[slot]).wait()
    rt = rt_ref[...]
    f = rt[:, 2:3] * ybuf[slot, 0:tm, :] + rt[:, 3:4] * ybuf[slot, tm:2 * tm, :]
    xn = x_ref[...] + mod_ref[5:6, :] * f
    xo_ref[...] = _rms(xn) * gf_ref[...] if final else xn


def _combine(xall, route, mod, g_final, yb, pos, n_batch, tps, final):
    r, d = xall.shape
    tm = ROW_TILE
    nt = r // tm
    last = nt - 1

    def mod_idx(i):
        return (jnp.where(i % tps == 0, n_batch, i // tps), 0, 0)

    row = lambda w: pl.BlockSpec((tm, w), lambda i: (i, 0))
    return pl.pallas_call(
        functools.partial(_combine_kernel, final=final),
        grid=(nt,),
        in_specs=[
            pl.BlockSpec((None, 1, TOP_K * tm), lambda i: (i, 0, 0), memory_space=pltpu.SMEM),
            pl.BlockSpec((None, 1, TOP_K * tm), lambda i: (jnp.minimum(i + 1, last), 0, 0),
                         memory_space=pltpu.SMEM),
            row(d), row(LANE),
            pl.BlockSpec((None, N_MOD, d), mod_idx),
            pl.BlockSpec((1, d), lambda i: (0, 0)),
            pl.BlockSpec(memory_space=pl.ANY),
        ],
        out_specs=row(d),
        out_shape=jax.ShapeDtypeStruct((r, d), F32),
        scratch_shapes=[pltpu.VMEM((2, TOP_K * tm, d), F32), pltpu.SemaphoreType.DMA((2,))],
        compiler_params=_cparams("arbitrary"),
        name="moe_combine",
    )(pos, pos, xall, route, mod, g_final, yb)


def _dispatch_plan(route, counts_f, n_tiles):
    n = route.shape[0]
    bm = MOE_BLOCK
    n_blocks = -(-(n * TOP_K + N_EXPERTS * (bm - 1)) // bm)
    counts = counts_f[:N_EXPERTS].astype(jnp.int32)
    padded = (counts + bm - 1) // bm * bm
    pend = jnp.cumsum(padded)
    seg_off = pend - padded
    eid = route[:, 0:TOP_K].astype(jnp.int32)
    rank = route[:, 4:4 + TOP_K].astype(jnp.int32)
    experts = jnp.arange(N_EXPERTS, dtype=jnp.int32)
    pos = jnp.sum(jnp.where(eid[..., None] == experts, seg_off, 0), axis=-1) + rank
    blk0 = jnp.arange(n_blocks, dtype=jnp.int32) * bm
    block_e = jnp.minimum(jnp.sum((blk0[:, None] >= pend[None, :]).astype(jnp.int32), axis=1), N_EXPERTS - 1)
    seg_end = seg_off + counts
    n_valid = jnp.clip(seg_end[block_e] - blk0, 0, bm).astype(jnp.int32)
    pos_tiles = pos.reshape(n_tiles, n // n_tiles, TOP_K).transpose(0, 2, 1).reshape(n_tiles, 1, -1)
    used = counts > 0
    last_blocks = jnp.sort(jnp.where(used, pend // bm - 1, n_blocks)).astype(jnp.int32)
    n_live = jnp.stack([pend[-1] // bm, jnp.sum(used)]).astype(jnp.int32)
    return block_e.astype(jnp.int32), n_valid, last_blocks, n_live, pos_tiles, n_blocks * bm


def _rope_tables(seq, ctx_rows):
    t = jnp.arange(seq - ctx_rows)
    inv = ROPE_BASE ** (-jnp.arange(0, ROT_AX, 2, dtype=F32) / ROT_AX)
    ang_r = (t // GRID_W).astype(F32)[:, None] * inv
    ang_c = (t % GRID_W).astype(F32)[:, None] * inv
    cos64 = jnp.concatenate([jnp.cos(ang_r)] * 2 + [jnp.cos(ang_c)] * 2, axis=1)
    sin64 = jnp.concatenate([-jnp.sin(ang_r), jnp.sin(ang_r), -jnp.sin(ang_c), jnp.sin(ang_c)], axis=1)
    cos = jnp.concatenate([jnp.ones((ctx_rows, DK), F32), cos64], axis=0)
    sin = jnp.concatenate([jnp.zeros((ctx_rows, DK), F32), sin64], axis=0)
    return jnp.tile(cos, (1, LANE // DK)), jnp.tile(sin, (1, LANE // DK))


def kernel(x, c, ctx, c_ctx, w_mod, b_mod, g_norm1, g_norm2, w_in, diff_lambda, g_diff_subln, w_gmlp_s, b_gmlp_s, g_gmlp_v, w_conv_m, b_conv_m, w_qkv_m, w_if_m, b_if_m, g_mlstm_norm, skip_m, w_branch, w_out, w_route_g, b_route_g, w_route_e, b_route_e, w_e_gate, w_e_up, w_e_down, g_final):
    n_batch, t_lat, d = x.shape
    ctx_rows = ctx.shape[1]
    depth = w_in.shape[0]
    seq = ctx_rows + t_lat
    assert d == D_MODEL and ctx_rows == ROW_TILE and t_lat % ROW_TILE == 0 and t_lat % GRID_W == 0
    tps = seq // ROW_TILE
    n_tiles = n_batch * tps

    xall = jnp.concatenate([ctx, x], axis=1).reshape(n_batch * seq, d)
    mb = -(-(n_batch + 1) // 8) * 8
    c_all = jnp.zeros((mb, d), F32).at[:n_batch].set(c).at[n_batch].set(c_ctx)
    mod_all = _modulation(c_all, w_mod, b_mod)[:, :n_batch + 1].reshape(depth, n_batch + 1, N_MOD, d)
    cos, sin_signed = _rope_tables(seq, ctx_rows)

    for l in range(depth):
        lam_init = 0.8 - 0.6 * math.exp(-0.3 * l)
        mod = mod_all[l]
        w_p = jnp.concatenate([w_in[l][:, ORIG_GATE0:], w_in[l][:, :ORIG_GATE0]], axis=1).astype(MXU_DT)
        p = _inproj(xall, g_norm1[l][None], mod, cos, sin_signed, w_p, n_batch, tps)

        ya = _attention(p, diff_lambda[l], g_diff_subln[l][None], lam_init, n_batch, tps, ctx_rows)

        b_full = jnp.repeat(b_gmlp_s[l].T, W_B // G_B, axis=1)
        yb = _gmlp(p, g_gmlp_v[l][None], w_gmlp_s[l].astype(MXU_DT), b_full)

        w_if_pad = jnp.zeros((3 * W_C, LANE), F32).at[:, :4 * H_C].set(w_if_m[l]).astype(MXU_DT)
        b_if_pad = jnp.zeros((1, LANE), F32).at[0, :4 * H_C].set(b_if_m[l])
        xconv, q_m, k_m, v_m, gates = _mlstm_features(
            p, w_conv_m[l], b_conv_m[l][None], w_qkv_m[l].astype(MXU_DT), w_if_pad, b_if_pad, tps)
        hf, hb = _mlstm_scan(q_m, k_m, v_m, gates, n_batch, seq, ctx_rows)

        w_route = (jnp.zeros((d, LANE), F32).at[:, :N_GROUPS].set(w_route_g[l])
                   .at[:, N_GROUPS:N_GROUPS + N_EXPERTS].set(w_route_e[l]))
        b_route = (jnp.zeros((1, LANE), F32).at[0, :N_GROUPS].set(b_route_g[l])
                   .at[0, N_GROUPS:N_GROUPS + N_EXPERTS].set(b_route_e[l]))
        xall, h2, route, counts = _merge(xall, p, ya, yb, hf, hb, xconv, g_mlstm_norm[l][None], skip_m[l][None],
                                         w_branch[l].astype(MXU_DT), w_out[l].astype(MXU_DT), mod,
                                         g_norm2[l][None], w_route, b_route, n_batch, tps)

        block_e, n_valid, fill_blocks, n_live, pos_tiles, n_slots = _dispatch_plan(route, counts[0], n_tiles)
        xs = _moe_dispatch(h2, fill_blocks, n_live, pos_tiles, n_slots)
        y_sorted = _moe_experts(xs, block_e, n_valid, w_e_gate[l].astype(MXU_DT),
                                w_e_up[l].astype(MXU_DT), w_e_down[l].astype(MXU_DT))
        xall = _combine(xall, route, mod, g_final[None], y_sorted, pos_tiles, n_batch, tps, l == depth - 1)

    return xall.reshape(n_batch, seq, d)[:, ctx_rows:]
```

```python
import functools
import math

import jax
import jax.numpy as jnp
from jax import lax
from jax.experimental import pallas as pl
from jax.experimental.pallas import tpu as pltpu

F32 = jnp.float32
MXU_DT = jnp.bfloat16
ACT_DT = jnp.bfloat16
HIGHEST = lax.Precision.HIGHEST

D_MODEL = 1024
N_MOD = 6
EPS = 1e-6
GRID_W = 64
H_A, DK = 4, 64
DV = 2 * DK
ROT_AX = DK // 2
ROPE_BASE = 10000.0
W_B, G_B, CHUNK_B = 512, 4, 128
H_C, DH_C, CONV_K, MLSTM_CHUNK = 4, 128, 3, 128
W_C = H_C * DH_C
N_GROUPS, EXP_PER_GROUP, TOP_K, D_EXPERT = 4, 8, 2, 512
N_EXPERTS = N_GROUPS * EXP_PER_GROUP
W_BR = 512

GATE_W = 3 * D_MODEL
P_GATE = 0
P_K = GATE_W
P_V = P_K + 512
P_XM = P_V + 512
P_Q = P_XM + 512
P_UV = P_Q + 512
P_Z = P_UV + 2 * W_B
IN_COLS = P_Z + W_C
ORIG_GATE0 = IN_COLS - GATE_W

LANE = 128
ROW_TILE = 256
PROJ_CHUNK = 512
MOE_BLOCK = 256
HALO = 16
DMA_UNROLL = 8
VMEM_LIMIT = 56 * 1024 * 1024
NEG_BIG = -1e30


def _cparams(*sem):
    return pltpu.CompilerParams(dimension_semantics=sem, vmem_limit_bytes=VMEM_LIMIT)


def _rms(x):
    return x * lax.rsqrt(jnp.mean(x * x, axis=-1, keepdims=True) + EPS)


def _mod_kernel(c_ref, w_ref, b_ref, o_ref):
    c = c_ref[...]
    s = c * jax.nn.sigmoid(c)
    o_ref[...] = jnp.dot(s, w_ref[...], preferred_element_type=F32, precision=HIGHEST) + b_ref[...]


def _modulation(c_all, w_mod, b_mod):
    n_layer = w_mod.shape[0]
    mb, d = c_all.shape
    tn = 1024
    return pl.pallas_call(
        _mod_kernel,
        grid=(n_layer, N_MOD * d // tn),
        in_specs=[
            pl.BlockSpec((mb, d), lambda l, j: (0, 0)),
            pl.BlockSpec((None, d, tn), lambda l, j: (l, 0, j)),
            pl.BlockSpec((None, 1, tn), lambda l, j: (l, 0, j)),
        ],
        out_specs=pl.BlockSpec((None, mb, tn), lambda l, j: (l, 0, j)),
        out_shape=jax.ShapeDtypeStruct((n_layer, mb, N_MOD * d), F32),
        compiler_params=_cparams("parallel", "parallel"),
        name="modulation",
    )(c_all, w_mod, b_mod.reshape(n_layer, 1, N_MOD * d))


def _rope(acc, cos, sin_signed):
    w = acc.shape[1]
    lane = lax.broadcasted_iota(jnp.int32, acc.shape, 1)
    partner = jnp.where((lane & 16) == 0, pltpu.roll(acc, w - 16, 1), pltpu.roll(acc, 16, 1))
    reps = w // cos.shape[1]
    return acc * jnp.tile(cos, (1, reps)) + partner * jnp.tile(sin_signed, (1, reps))


def _inproj_kernel(x_ref, g_ref, mod_ref, cos_ref, sin_ref, w_ref, o_ref):
    x = x_ref[...]
    h = (_rms(x) * g_ref[...] * (1.0 + mod_ref[1:2, :]) + mod_ref[0:1, :]).astype(MXU_DT)
    k_chunk = P_K // PROJ_CHUNK
    q_chunk = P_Q // PROJ_CHUNK
    for c in range(IN_COLS // PROJ_CHUNK):
        cols = slice(c * PROJ_CHUNK, (c + 1) * PROJ_CHUNK)
        acc = jnp.dot(h, w_ref[:, cols], preferred_element_type=F32)
        if c == k_chunk:
            acc = _rope(acc, cos_ref[...], sin_ref[...])
        elif c == q_chunk:
            acc = _rope(acc, cos_ref[...], sin_ref[...]) * (DK ** -0.5)
        o_ref[:, cols] = acc.astype(o_ref.dtype)


def _inproj(xall, g1, mod, cos, sin_signed, w_p, n_batch, tps):
    r, d = xall.shape
    tm = ROW_TILE

    def mod_idx(i):
        return (jnp.where(i % tps == 0, n_batch, i // tps), 0, 0)

    return pl.pallas_call(
        _inproj_kernel,
        grid=(r // tm,),
        in_specs=[
            pl.BlockSpec((tm, d), lambda i: (i, 0)),
            pl.BlockSpec((1, d), lambda i: (0, 0)),
            pl.BlockSpec((None, N_MOD, d), mod_idx),
            pl.BlockSpec((tm, LANE), lambda i: (i % tps, 0)),
            pl.BlockSpec((tm, LANE), lambda i: (i % tps, 0)),
            pl.BlockSpec((d, IN_COLS), lambda i: (0, 0)),
        ],
        out_specs=pl.BlockSpec((tm, IN_COLS), lambda i: (i, 0)),
        out_shape=jax.ShapeDtypeStruct((r, IN_COLS), ACT_DT),
        compiler_params=_cparams("parallel"),
        name="inproj",
    )(xall, g1, mod, cos, sin_signed, w_p)


def _attn_kernel(dl_ref, g_ref, q_ref, k_ref, v_ref, o_ref, *, lam_init, ctx_rows):
    dl = dl_ref[...]
    lam = (jnp.exp(jnp.sum(dl[0:1] * dl[1:2], keepdims=True))
           - jnp.exp(jnp.sum(dl[2:3] * dl[3:4], keepdims=True)) + lam_init)
    nt = (((1,), (1,)), ((), ()))

    def attend(kv_rows):
        q = q_ref[...]
        k = k_ref[0:kv_rows, :]
        lane = lax.broadcasted_iota(jnp.int32, q.shape, 1)
        zero = jnp.zeros_like(q)

        def softmax_parts(qm):
            s = lax.dot_general(qm, k, nt, preferred_element_type=F32)
            p = jnp.exp(s - jnp.max(s, axis=-1, keepdims=True))
            return p, jnp.sum(p, axis=-1, keepdims=True)

        p0, l0 = softmax_parts(jnp.where(lane < DK, q, zero))
        p1, l1 = softmax_parts(jnp.where(lane >= DK, q, zero))
        a = p0 * (1.0 / l0) - p1 * (lam / l1)
        o = jnp.dot(a.astype(MXU_DT), v_ref[0:kv_rows, :], preferred_element_type=F32)
        o_ref[...] = (_rms(o) * g_ref[...] * (1.0 - lam_init)).astype(o_ref.dtype)

    is_ctx = pl.program_id(2) == 0

    @pl.when(is_ctx)
    def _():
        attend(ctx_rows)

    @pl.when(jnp.logical_not(is_ctx))
    def _():
        attend(k_ref.shape[0])


def _attention(p, diff_lambda, g_sub, lam_init, n_batch, tps, ctx_rows):
    tq = ROW_TILE
    seq = tps * tq
    kern = functools.partial(_attn_kernel, lam_init=lam_init, ctx_rows=ctx_rows)
    return pl.pallas_call(
        kern,
        grid=(n_batch, H_A, tps),
        in_specs=[
            pl.BlockSpec((4, DK), lambda b, h, i: (0, 0)),
            pl.BlockSpec((1, DV), lambda b, h, i: (0, 0)),
            pl.BlockSpec((tq, LANE), lambda b, h, i: (b * tps + i, P_Q // LANE + h)),
            pl.BlockSpec((seq, LANE), lambda b, h, i: (b, P_K // LANE + h)),
            pl.BlockSpec((seq, LANE), lambda b, h, i: (b, P_V // LANE + h)),
        ],
        out_specs=pl.BlockSpec((tq, LANE), lambda b, h, i: (b * tps + i, h)),
        out_shape=jax.ShapeDtypeStruct((p.shape[0], H_A * DV), ACT_DT),
        compiler_params=_cparams("parallel", "parallel", "arbitrary"),
        name="diff_attention",
    )(diff_lambda, g_sub, p, p, p)


def _gmlp_kernel(uv_ref, gv_ref, ws_ref, bs_ref, o_ref):
    uv = jax.nn.gelu(uv_ref[...].astype(F32))
    u = uv[:, :W_B]
    v = (_rms(uv[:, W_B:]) * gv_ref[...]).astype(MXU_DT)
    gw = W_B // G_B
    for c in range(uv.shape[0] // CHUNK_B):
        rows = slice(c * CHUNK_B, (c + 1) * CHUNK_B)
        for g in range(G_B):
            cols = slice(g * gw, (g + 1) * gw)
            mixed = jnp.dot(ws_ref[g], v[rows, cols], preferred_element_type=F32) + bs_ref[:, cols]
            o_ref[rows, cols] = (u[rows, cols] * mixed).astype(o_ref.dtype)


def _gmlp(p, g_v, w_s, b_full):
    r = p.shape[0]
    tm = ROW_TILE
    return pl.pallas_call(
        _gmlp_kernel,
        grid=(r // tm,),
        in_specs=[
            pl.BlockSpec((tm, 2 * W_B), lambda i: (i, P_UV // (2 * W_B))),
            pl.BlockSpec((1, W_B), lambda i: (0, 0)),
            pl.BlockSpec((G_B, CHUNK_B, CHUNK_B), lambda i: (0, 0, 0)),
            pl.BlockSpec((CHUNK_B, W_B), lambda i: (0, 0)),
        ],
        out_specs=pl.BlockSpec((tm, W_B), lambda i: (i, 0)),
        out_shape=jax.ShapeDtypeStruct((r, W_B), ACT_DT),
        compiler_params=_cparams("parallel"),
        name="gmlp",
    )(p, g_v, w_s, b_full)


def _mfeat_kernel(x_ref, prev_ref, next_ref, wc_ref, bc_ref, wqkv_ref, wif_ref, bif_ref,
                  xc_ref, q_ref, k_ref, v_ref, g_ref, *, tps):
    tm = x_ref.shape[0]
    j = pl.program_id(0) % tps
    seg_start = jnp.logical_or(j == 0, j == 1)
    seg_end = jnp.logical_or(j == 0, j == tps - 1)
    x = x_ref[...].astype(F32)
    prow = jnp.where(seg_start, 0.0, prev_ref[...].astype(F32)[HALO - 1:HALO, :])
    nrow = jnp.where(seg_end, 0.0, next_ref[...].astype(F32)[0:1, :])
    rid = lax.broadcasted_iota(jnp.int32, x.shape, 0)
    xp = jnp.where(rid == 0, prow, pltpu.roll(x, 1, 0))
    xn = jnp.where(rid == tm - 1, nrow, pltpu.roll(x, tm - 1, 0))
    xc = wc_ref[0:1, :] * xp + wc_ref[1:2, :] * x + wc_ref[2:3, :] * xn + bc_ref[...]
    xc = xc * jax.nn.sigmoid(xc)
    xc_ref[...] = xc.astype(xc_ref.dtype)
    xcb = xc.astype(MXU_DT)
    xb = x_ref[...]
    qs, ks, vs = [], [], []
    for h in range(H_C):
        cols = slice(h * DH_C, (h + 1) * DH_C)
        qs.append(jnp.dot(xcb[:, cols], wqkv_ref[0, h], preferred_element_type=F32))
        ks.append(jnp.dot(xcb[:, cols], wqkv_ref[1, h], preferred_element_type=F32))
        vs.append(jnp.dot(xb[:, cols], wqkv_ref[2, h], preferred_element_type=F32))
    q = jnp.concatenate(qs, axis=1)
    k = jnp.concatenate(ks, axis=1)
    v = jnp.concatenate(vs, axis=1)
    qkv = jnp.concatenate([q, k, v], axis=1).astype(MXU_DT)
    g_ref[...] = jnp.dot(qkv, wif_ref[...], preferred_element_type=F32) + bif_ref[...]
    q_ref[...] = q.astype(q_ref.dtype)
    k_ref[...] = (k * (DH_C ** -0.5)).astype(k_ref.dtype)
    v_ref[...] = v.astype(v_ref.dtype)


def _mlstm_features(p, w_conv, b_conv, w_qkv, w_if_pad, b_if_pad, tps):
    r = p.shape[0]
    tm = ROW_TILE
    hpt = tm // HALO
    xm_blk = P_XM // W_C
    last_halo = r // HALO - 1
    kern = functools.partial(_mfeat_kernel, tps=tps)
    act = jax.ShapeDtypeStruct((r, W_C), ACT_DT)
    row_spec = pl.BlockSpec((tm, W_C), lambda i: (i, 0))
    return pl.pallas_call(
        kern,
        grid=(r // tm,),
        in_specs=[
            pl.BlockSpec((tm, W_C), lambda i: (i, xm_blk)),
            pl.BlockSpec((HALO, W_C), lambda i: (jnp.maximum(i * hpt - 1, 0), xm_blk)),
            pl.BlockSpec((HALO, W_C), lambda i: (jnp.minimum((i + 1) * hpt, last_halo), xm_blk)),
            pl.BlockSpec((CONV_K, W_C), lambda i: (0, 0)),
            pl.BlockSpec((1, W_C), lambda i: (0, 0)),
            pl.BlockSpec((3, H_C, DH_C, DH_C), lambda i: (0, 0, 0, 0)),
            pl.BlockSpec((3 * W_C, LANE), lambda i: (0, 0)),
            pl.BlockSpec((1, LANE), lambda i: (0, 0)),
        ],
        out_specs=[row_spec, row_spec, row_spec, row_spec,
                   pl.BlockSpec((tm, LANE), lambda i: (i, 0))],
        out_shape=[act, act, act, act, jax.ShapeDtypeStruct((r, LANE), F32)],
        compiler_params=_cparams("parallel"),
        name="mlstm_features",
    )(p, p, p, w_conv, b_conv, w_qkv, w_if_pad, b_if_pad)


def _lane_scan(x, op, reverse):
    n = x.shape[1]
    lane = lax.broadcasted_iota(jnp.int32, x.shape, 1)
    sh = 1
    while sh < n:
        if reverse:
            shifted, valid = pltpu.roll(x, n - sh, 1), lane < n - sh
        else:
            shifted, valid = pltpu.roll(x, sh, 1), lane >= sh
        x = jnp.where(valid, op(x, shifted), x)
        sh *= 2
    return x


def _mlstm_direction(gt, q_ref, k_ref, v_ref, c_ref, m_ref, h_ref, reverse):
    ln = gt.shape[1]
    nr = 2 * H_C
    ig = gt
    bcum = _lane_scan(jax.nn.log_sigmoid(pltpu.roll(gt, H_C, 0)), jnp.add, reverse)
    r = ig - bcum
    cmx = _lane_scan(r, jnp.maximum, reverse)
    g = bcum[:, 0:1] if reverse else bcum[:, ln - 1:ln]
    a_end = g + r
    m_loc = jnp.max(a_end, axis=1, keepdims=True)
    w_end = jnp.exp(a_end - m_loc)
    m_old = m_ref[:, 0:1]
    m_new = jnp.maximum(g + m_old, m_loc)
    a_old = jnp.exp(g + m_old - m_new)
    a_new = jnp.exp(m_loc - m_new)
    zt = jnp.concatenate([bcum, cmx, w_end, jnp.zeros((LANE - 3 * nr, ln), F32)], axis=0).T
    t_id = lax.broadcasted_iota(jnp.int32, (ln, ln), 0)
    s_id = lax.broadcasted_iota(jnp.int32, (ln, ln), 1)
    keep = (s_id >= t_id) if reverse else (s_id <= t_id)
    ones_col = (lax.broadcasted_iota(jnp.int32, (ln, DH_C), 1) == 0).astype(F32)
    nt = (((1,), (1,)), ((), ()))
    for h in range(H_C):
        cols = slice(h * DH_C, (h + 1) * DH_C)
        q = q_ref[:, cols]
        k = k_ref[:, cols]
        vext = jnp.concatenate([v_ref[:, cols].astype(F32), ones_col], axis=1)
        bc = zt[:, h:h + 1]
        cm = zt[:, nr + h:nr + h + 1]
        we = zt[:, 2 * nr + h:2 * nr + h + 1]
        m0 = m_old[h:h + 1, :]
        log_inter = bc + m0
        m_t = jnp.maximum(log_inter, bc + cm)
        dmat = jnp.exp(jnp.where(keep, bc + r[h:h + 1, :] - m_t, NEG_BIG))
        s = lax.dot_general(q, k, nt, preferred_element_type=F32) * dmat
        cext = c_ref[h]
        r1 = jnp.dot(s.astype(MXU_DT), vext.astype(MXU_DT), preferred_element_type=F32)
        r2 = lax.dot_general(q, cext.astype(MXU_DT), nt, preferred_element_type=F32)
        w_inter = jnp.exp(log_inter - m_t)
        num = r1[:, :DH_C] + w_inter * r2[:, :DH_C]
        den = r1[:, DH_C:DH_C + 1] + w_inter * r2[:, DH_C:DH_C + 1]
        hh = num / jnp.maximum(jnp.abs(den), jnp.exp(-m_t))
        h_ref[:, cols] = hh.astype(h_ref.dtype)
        vw_t = (vext * we).T.astype(MXU_DT)
        c_loc = jnp.dot(vw_t, k, preferred_element_type=F32)
        c_ref[h] = a_old[h:h + 1, :] * cext + a_new[h:h + 1, :] * c_loc
    m_ref[...] = jnp.broadcast_to(m_new, m_ref.shape)


def _mscan_kernel(gf_ref, qf_ref, kf_ref, vf_ref, gb_ref, qb_ref, kb_ref, vb_ref,
                  hf_ref, hb_ref, cf_ref, mf_ref, cb_ref, mb_ref):
    @pl.when(pl.program_id(1) == 0)
    def _():
        cf_ref[...] = jnp.zeros_like(cf_ref)
        cb_ref[...] = jnp.zeros_like(cb_ref)
        mf_ref[...] = jnp.zeros_like(mf_ref)
        mb_ref[...] = jnp.zeros_like(mb_ref)

    _mlstm_direction(gf_ref[...].T[0:2 * H_C], qf_ref, kf_ref, vf_ref, cf_ref, mf_ref, hf_ref, False)
    _mlstm_direction(gb_ref[...].T[2 * H_C:4 * H_C], qb_ref, kb_ref, vb_ref, cb_ref, mb_ref, hb_ref, True)


def _mlstm_scan(q, k, v, gates, n_batch, seq, ctx_rows):
    ln = MLSTM_CHUNK
    nck = seq // ln
    ncc = ctx_rows // ln

    def fwd(b, s):
        return (b * nck + s, 0)

    def bwd(b, s):
        return (b * nck + jnp.where(s < ncc, ncc - 1 - s, nck - 1 + ncc - s), 0)

    act = lambda idx: pl.BlockSpec((ln, W_C), idx)
    gsp = lambda idx: pl.BlockSpec((ln, LANE), idx)
    out = jax.ShapeDtypeStruct(q.shape, ACT_DT)
    return pl.pallas_call(
        _mscan_kernel,
        grid=(n_batch, nck),
        in_specs=[gsp(fwd), act(fwd), act(fwd), act(fwd), gsp(bwd), act(bwd), act(bwd), act(bwd)],
        out_specs=[act(fwd), act(bwd)],
        out_shape=[out, out],
        scratch_shapes=[
            pltpu.VMEM((H_C, 2 * DH_C, DH_C), F32), pltpu.VMEM((8, LANE), F32),
            pltpu.VMEM((H_C, 2 * DH_C, DH_C), F32), pltpu.VMEM((8, LANE), F32),
        ],
        compiler_params=_cparams("parallel", "arbitrary"),
        name="mlstm_scan",
    )(gates, q, k, v, gates, q, k, v)


def _route(logits, carry):
    lane = lax.broadcasted_iota(jnp.int32, logits.shape, 1).astype(F32)
    big = float(4 * LANE)
    gl = jnp.where(lane < N_GROUPS, logits, NEG_BIG)
    gmax = jnp.max(gl, axis=-1, keepdims=True)
    g_star = jnp.min(jnp.where(gl == gmax, lane, big), axis=-1, keepdims=True)
    p_g = 1.0 / jnp.sum(jnp.exp(gl - gmax), axis=-1, keepdims=True)
    e_lo = g_star * EXP_PER_GROUP + N_GROUPS
    in_group = jnp.logical_and(lane >= e_lo, lane < e_lo + EXP_PER_GROUP)
    el = jnp.where(in_group, logits, NEG_BIG)
    v1 = jnp.max(el, axis=-1, keepdims=True)
    i1 = jnp.min(jnp.where(el == v1, lane, big), axis=-1, keepdims=True)
    el2 = jnp.where(lane == i1, NEG_BIG, el)
    v2 = jnp.max(el2, axis=-1, keepdims=True)
    i2 = jnp.min(jnp.where(el2 == v2, lane, big), axis=-1, keepdims=True)
    e21 = jnp.exp(v2 - v1)
    w1 = p_g / (1.0 + e21)
    w2 = p_g * e21 / (1.0 + e21)
    e1 = i1 - N_GROUPS
    e2 = i2 - N_GROUPS
    oh1 = lane == e1
    oh2 = lane == e2
    both = jnp.where(jnp.logical_or(oh1, oh2), 1.0, 0.0)
    tm = logits.shape[0]
    lower = (lax.broadcasted_iota(jnp.int32, (tm, tm), 1) < lax.broadcasted_iota(jnp.int32, (tm, tm), 0))
    before = jnp.dot(jnp.where(lower, 1.0, 0.0).astype(MXU_DT), both.astype(MXU_DT),
                     preferred_element_type=F32) + carry
    rank1 = jnp.sum(jnp.where(oh1, before, 0.0), axis=-1, keepdims=True)
    rank2 = jnp.sum(jnp.where(oh2, before, 0.0), axis=-1, keepdims=True)
    out = jnp.where(lane == 0, e1, 0.0)
    out = jnp.where(lane == 1, e2, out)
    out = jnp.where(lane == 2, w1, out)
    out = jnp.where(lane == 3, w2, out)
    out = jnp.where(lane == 4, rank1, out)
    out = jnp.where(lane == 5, rank2, out)
    return out, carry + jnp.sum(both, axis=0, keepdims=True)


def _merge_kernel(x_ref, gate_ref, z_ref, ya_ref, yb_ref, hf_ref, hb_ref, xc_ref, gm_ref, sk_ref,
                  wbr_ref, wo_ref, mod_ref, g2_ref, wr_ref, br_ref, xo_ref, rt_ref, cnt_ref, carry_ref):
    @pl.when(pl.program_id(0) == 0)
    def _():
        carry_ref[...] = jnp.zeros_like(carry_ref)

    hs = hf_ref[...].astype(F32) + hb_ref[...].astype(F32)
    hn = jnp.concatenate([_rms(hs[:, h * DH_C:(h + 1) * DH_C]) for h in range(H_C)], axis=1) * gm_ref[...]
    yc = (hn + sk_ref[...] * xc_ref[...].astype(F32)) * jax.nn.sigmoid(z_ref[...].astype(F32))
    ys = (ya_ref[...], yb_ref[...], yc.astype(MXU_DT))
    merged = None
    for i in range(3):
        gate = jax.nn.sigmoid(gate_ref[:, i * D_MODEL:(i + 1) * D_MODEL].astype(F32))
        term = gate * jnp.dot(ys[i], wbr_ref[i], preferred_element_type=F32)
        merged = term if merged is None else merged + term
    y = jnp.dot(merged.astype(MXU_DT), wo_ref[...], preferred_element_type=F32)
    xn = x_ref[...] + mod_ref[2:3, :] * y
    xo_ref[...] = xn
    h2 = _rms(xn) * g2_ref[...] * (1.0 + mod_ref[4:5, :]) + mod_ref[3:4, :]
    logits = jnp.dot(h2, wr_ref[...], preferred_element_type=F32, precision=HIGHEST) + br_ref[...]
    route, carry = _route(logits, carry_ref[0:1, :])
    rt_ref[...] = route
    carry_ref[0:1, :] = carry
    cnt_ref[...] = jnp.broadcast_to(carry, cnt_ref.shape)


def _merge(xall, p, ya, yb, hf, hb, xconv, g_m, skip, w_br, w_o, mod, g2, w_route, b_route, n_batch, tps):
    r, d = xall.shape
    tm = ROW_TILE

    def mod_idx(i):
        return (jnp.where(i % tps == 0, n_batch, i // tps), 0, 0)

    row = lambda w: pl.BlockSpec((tm, w), lambda i: (i, 0))
    full = lambda shape: pl.BlockSpec(shape, lambda i: (0,) * len(shape))
    return pl.pallas_call(
        _merge_kernel,
        grid=(r // tm,),
        in_specs=[
            row(d),
            pl.BlockSpec((tm, GATE_W), lambda i: (i, P_GATE // GATE_W)),
            pl.BlockSpec((tm, W_C), lambda i: (i, P_Z // W_C)),
            row(W_BR), row(W_BR), row(W_C), row(W_C), row(W_C),
            full((1, W_C)), full((1, W_C)),
            full((3, W_BR, d)), full((d, d)),
            pl.BlockSpec((None, N_MOD, d), mod_idx),
            full((1, d)), full((d, LANE)), full((1, LANE)),
        ],
        out_specs=[row(d), row(LANE), full((8, LANE))],
        out_shape=[jax.ShapeDtypeStruct((r, d), F32), jax.ShapeDtypeStruct((r, LANE), F32),
                   jax.ShapeDtypeStruct((8, LANE), F32)],
        scratch_shapes=[pltpu.VMEM((8, LANE), F32)],
        compiler_params=_cparams("arbitrary"),
        name="merge_route",
    )(xall, p, p, ya, yb, hf, hb, xconv, g_m, skip, w_br, w_o, mod, g2, w_route, b_route)


def _row_copy(src_hbm, src_row, dst_ref, dst_row, sem):
    return pltpu.make_async_copy(src_hbm.at[pl.ds(src_row, 1), :], dst_ref.at[pl.ds(dst_row, 1), :], sem)


def _dispatch_kernel(fill_ref, nlive_ref, pos_ref, x_ref, g2_ref, mod_ref, xs_hbm, hbuf, zbuf, zsem, sem):
    i = pl.program_id(0)
    n = pl.num_programs(0)
    tm = x_ref.shape[0]
    slot = i % 2
    bm = zbuf.shape[0]
    n_blocks = xs_hbm.shape[0] // bm

    @pl.when(i == 0)
    def _():
        zbuf[...] = jnp.zeros_like(zbuf)

        def fill_block(b):
            return pltpu.make_async_copy(zbuf, xs_hbm.at[pl.ds(b * bm, bm), :], zsem)

        def fill_last(k, carry):
            fill_block(fill_ref[k]).start()
            return carry

        def fill_unused(b, carry):
            fill_block(b).start()
            return carry

        def wait_fill(k, carry):
            fill_block(0).wait()
            return carry

        n_live = nlive_ref[0]
        n_used = nlive_ref[1]
        lax.fori_loop(0, n_used, fill_last, 0)
        lax.fori_loop(n_live, n_blocks, fill_unused, 0)
        lax.fori_loop(0, n_used + n_blocks - n_live, wait_fill, 0)

    hbuf[slot] = _rms(x_ref[...]) * g2_ref[...] * (1.0 + mod_ref[4:5, :]) + mod_ref[3:4, :]

    def body(c, carry):
        for u in range(DMA_UNROLL):
            r = c * DMA_UNROLL + u
            for j in range(TOP_K):
                pltpu.make_async_copy(hbuf.at[slot, pl.ds(r, 1), :],
                                      xs_hbm.at[pl.ds(pos_ref[0, j * tm + r], 1), :], sem.at[slot]).start()
        return carry

    lax.fori_loop(0, tm // DMA_UNROLL, body, 0)

    def wait_tile(s):
        pltpu.make_async_copy(hbuf.at[s], xs_hbm.at[pl.ds(0, tm), :], sem.at[s]).wait()
        pltpu.make_async_copy(hbuf.at[s], xs_hbm.at[pl.ds(0, tm), :], sem.at[s]).wait()

    @pl.when(i > 0)
    def _():
        wait_tile(1 - slot)

    @pl.when(i == n - 1)
    def _():
        wait_tile(slot)


def _moe_dispatch(xall, g2, mod, fill_blocks, n_live, pos_tiles, n_slots, n_batch, tps):
    r, d = xall.shape
    tm = ROW_TILE
    assert TOP_K == 2

    def mod_idx(i, fb, nl):
        return (jnp.where(i % tps == 0, n_batch, i // tps), 0, 0)

    grid_spec = pltpu.PrefetchScalarGridSpec(
        num_scalar_prefetch=2,
        grid=(r // tm,),
        in_specs=[
            pl.BlockSpec((None, 1, TOP_K * tm), lambda i, fb, nl: (i, 0, 0), memory_space=pltpu.SMEM),
            pl.BlockSpec((tm, d), lambda i, fb, nl: (i, 0)),
            pl.BlockSpec((1, d), lambda i, fb, nl: (0, 0)),
            pl.BlockSpec((None, N_MOD, d), mod_idx),
        ],
        out_specs=pl.BlockSpec(memory_space=pl.ANY),
        scratch_shapes=[pltpu.VMEM((2, tm, d), F32), pltpu.VMEM((MOE_BLOCK, d), F32),
                        pltpu.SemaphoreType.DMA(()), pltpu.SemaphoreType.DMA((2,))],
    )
    return pl.pallas_call(
        _dispatch_kernel,
        grid_spec=grid_spec,
        out_shape=jax.ShapeDtypeStruct((n_slots, d), F32),
        compiler_params=_cparams("arbitrary"),
        name="moe_dispatch",
    )(fill_blocks, n_live, pos_tiles, xall, g2, mod)


def _moe_kernel(be_ref, nv_ref, x_ref, wg_ref, wu_ref, wd_ref, y_ref):
    n_valid = nv_ref[pl.program_id(0)]

    @pl.when(n_valid > 0)
    def _():
        x = x_ref[...].astype(MXU_DT)
        a = jnp.dot(x, wg_ref[...], preferred_element_type=F32)
        u = jnp.dot(x, wu_ref[...], preferred_element_type=F32)
        act = (a * jax.nn.sigmoid(a) * u).astype(MXU_DT)
        y_ref[...] = jnp.dot(act, wd_ref[...], preferred_element_type=F32)

    @pl.when(n_valid <= 0)
    def _():
        y_ref[...] = jnp.zeros_like(y_ref)


def _moe_experts(xs, block_e, n_valid, w_gate, w_up, w_down):
    n_slots, d = xs.shape
    bm = MOE_BLOCK
    grid_spec = pltpu.PrefetchScalarGridSpec(
        num_scalar_prefetch=2,
        grid=(n_slots // bm,),
        in_specs=[
            pl.BlockSpec((bm, d), lambda i, be, nv: (i, 0)),
            pl.BlockSpec((None, d, D_EXPERT), lambda i, be, nv: (be[i], 0, 0)),
            pl.BlockSpec((None, d, D_EXPERT), lambda i, be, nv: (be[i], 0, 0)),
            pl.BlockSpec((None, D_EXPERT, d), lambda i, be, nv: (be[i], 0, 0)),
        ],
        out_specs=pl.BlockSpec((bm, d), lambda i, be, nv: (i, 0)),
    )
    return pl.pallas_call(
        _moe_kernel,
        grid_spec=grid_spec,
        out_shape=jax.ShapeDtypeStruct((n_slots, d), F32),
        compiler_params=_cparams("arbitrary"),
        name="moe_experts",
    )(block_e, n_valid, xs, w_gate, w_up, w_down)


def _combine_kernel(pos_ref, pos_next_ref, x_ref, rt_ref, mod_ref, gf_ref, yb_hbm, xo_ref, ybuf, sem, *, final):
    i = pl.program_id(0)
    n = pl.num_programs(0)
    tm = x_ref.shape[0]

    def start_gather(ids_ref, slot):
        def body(c, carry):
            for u in range(DMA_UNROLL):
                r = c * DMA_UNROLL + u
                _row_copy(yb_hbm, ids_ref[0, r], ybuf.at[slot], r, sem.at[slot]).start()
            return carry
        lax.fori_loop(0, TOP_K * tm // DMA_UNROLL, body, 0)

    @pl.when(i == 0)
    def _():
        start_gather(pos_ref, 0)

    @pl.when(i + 1 < n)
    def _():
        start_gather(pos_next_ref, (i + 1) % 2)

    slot = i % 2
    pltpu.make_async_copy(yb_hbm.at[pl.ds(0, TOP_K * tm), :], ybuf.at[slot], sem.at[slot]).wait()
    rt = rt_ref[...]
    f = rt[:, 2:3] * ybuf[slot, 0:tm, :] + rt[:, 3:4] * ybuf[slot, tm:2 * tm, :]
    xn = x_ref[...] + mod_ref[5:6, :] * f
    xo_ref[...] = _rms(xn) * gf_ref[...] if final else xn


def _combine(xall, route, mod, g_final, yb, pos, n_batch, tps, final):
    r, d = xall.shape
    tm = ROW_TILE
    nt = r // tm
    last = nt - 1

    def mod_idx(i):
        return (jnp.where(i % tps == 0, n_batch, i // tps), 0, 0)

    row = lambda w: pl.BlockSpec((tm, w), lambda i: (i, 0))
    return pl.pallas_call(
        functools.partial(_combine_kernel, final=final),
        grid=(nt,),
        in_specs=[
            pl.BlockSpec((None, 1, TOP_K * tm), lambda i: (i, 0, 0), memory_space=pltpu.SMEM),
            pl.BlockSpec((None, 1, TOP_K * tm), lambda i: (jnp.minimum(i + 1, last), 0, 0),
                         memory_space=pltpu.SMEM),
            row(d), row(LANE),
            pl.BlockSpec((None, N_MOD, d), mod_idx),
            pl.BlockSpec((1, d), lambda i: (0, 0)),
            pl.BlockSpec(memory_space=pl.ANY),
        ],
        out_specs=row(d),
        out_shape=jax.ShapeDtypeStruct((r, d), F32),
        scratch_shapes=[pltpu.VMEM((2, TOP_K * tm, d), F32), pltpu.SemaphoreType.DMA((2,))],
        compiler_params=_cparams("arbitrary"),
        name="moe_combine",
    )(pos, pos, xall, route, mod, g_final, yb)


def _dispatch_plan(route, counts_f, n_tiles):
    n = route.shape[0]
    bm = MOE_BLOCK
    n_blocks = -(-(n * TOP_K + N_EXPERTS * (bm - 1)) // bm)
    counts = counts_f[:N_EXPERTS].astype(jnp.int32)
    padded = (counts + bm - 1) // bm * bm
    pend = jnp.cumsum(padded)
    seg_off = pend - padded
    eid = route[:, 0:TOP_K].astype(jnp.int32)
    rank = route[:, 4:4 + TOP_K].astype(jnp.int32)
    experts = jnp.arange(N_EXPERTS, dtype=jnp.int32)
    pos = jnp.sum(jnp.where(eid[..., None] == experts, seg_off, 0), axis=-1) + rank
    blk0 = jnp.arange(n_blocks, dtype=jnp.int32) * bm
    block_e = jnp.minimum(jnp.sum((blk0[:, None] >= pend[None, :]).astype(jnp.int32), axis=1), N_EXPERTS - 1)
    seg_end = seg_off + counts
    n_valid = jnp.clip(seg_end[block_e] - blk0, 0, bm).astype(jnp.int32)
    pos_tiles = pos.reshape(n_tiles, n // n_tiles, TOP_K).transpose(0, 2, 1).reshape(n_tiles, 1, -1)
    used = counts > 0
    last_blocks = jnp.sort(jnp.where(used, pend // bm - 1, n_blocks)).astype(jnp.int32)
    n_live = jnp.stack([pend[-1] // bm, jnp.sum(used)]).astype(jnp.int32)
    return block_e.astype(jnp.int32), n_valid, last_blocks, n_live, pos_tiles, n_blocks * bm


def _rope_tables(seq, ctx_rows):
    t = jnp.arange(seq - ctx_rows)
    inv = ROPE_BASE ** (-jnp.arange(0, ROT_AX, 2, dtype=F32) / ROT_AX)
    ang_r = (t // GRID_W).astype(F32)[:, None] * inv
    ang_c = (t % GRID_W).astype(F32)[:, None] * inv
    cos64 = jnp.concatenate([jnp.cos(ang_r)] * 2 + [jnp.cos(ang_c)] * 2, axis=1)
    sin64 = jnp.concatenate([-jnp.sin(ang_r), jnp.sin(ang_r), -jnp.sin(ang_c), jnp.sin(ang_c)], axis=1)
    cos = jnp.concatenate([jnp.ones((ctx_rows, DK), F32), cos64], axis=0)
    sin = jnp.concatenate([jnp.zeros((ctx_rows, DK), F32), sin64], axis=0)
    return jnp.tile(cos, (1, LANE // DK)), jnp.tile(sin, (1, LANE // DK))


def kernel(x, c, ctx, c_ctx, w_mod, b_mod, g_norm1, g_norm2, w_in, diff_lambda, g_diff_subln, w_gmlp_s, b_gmlp_s, g_gmlp_v, w_conv_m, b_conv_m, w_qkv_m, w_if_m, b_if_m, g_mlstm_norm, skip_m, w_branch, w_out, w_route_g, b_route_g, w_route_e, b_route_e, w_e_gate, w_e_up, w_e_down, g_final):
    n_batch, t_lat, d = x.shape
    ctx_rows = ctx.shape[1]
    depth = w_in.shape[0]
    seq = ctx_rows + t_lat
    assert d == D_MODEL and ctx_rows == ROW_TILE and t_lat % ROW_TILE == 0 and t_lat % GRID_W == 0
    tps = seq // ROW_TILE
    n_tiles = n_batch * tps

    xall = jnp.concatenate([ctx, x], axis=1).reshape(n_batch * seq, d)
    mb = -(-(n_batch + 1) // 8) * 8
    c_all = jnp.zeros((mb, d), F32).at[:n_batch].set(c).at[n_batch].set(c_ctx)
    mod_all = _modulation(c_all, w_mod, b_mod)[:, :n_batch + 1].reshape(depth, n_batch + 1, N_MOD, d)
    cos, sin_signed = _rope_tables(seq, ctx_rows)

    for l in range(depth):
        lam_init = 0.8 - 0.6 * math.exp(-0.3 * l)
        mod = mod_all[l]
        w_p = jnp.concatenate([w_in[l][:, ORIG_GATE0:], w_in[l][:, :ORIG_GATE0]], axis=1).astype(MXU_DT)
        p = _inproj(xall, g_norm1[l][None], mod, cos, sin_signed, w_p, n_batch, tps)

        ya = _attention(p, diff_lambda[l], g_diff_subln[l][None], lam_init, n_batch, tps, ctx_rows)

        b_full = jnp.repeat(b_gmlp_s[l].T, W_B // G_B, axis=1)
        yb = _gmlp(p, g_gmlp_v[l][None], w_gmlp_s[l].astype(MXU_DT), b_full)

        w_if_pad = jnp.zeros((3 * W_C, LANE), F32).at[:, :4 * H_C].set(w_if_m[l]).astype(MXU_DT)
        b_if_pad = jnp.zeros((1, LANE), F32).at[0, :4 * H_C].set(b_if_m[l])
        xconv, q_m, k_m, v_m, gates = _mlstm_features(
            p, w_conv_m[l], b_conv_m[l][None], w_qkv_m[l].astype(MXU_DT), w_if_pad, b_if_pad, tps)
        hf, hb = _mlstm_scan(q_m, k_m, v_m, gates, n_batch, seq, ctx_rows)

        w_route = (jnp.zeros((d, LANE), F32).at[:, :N_GROUPS].set(w_route_g[l])
                   .at[:, N_GROUPS:N_GROUPS + N_EXPERTS].set(w_route_e[l]))
        b_route = (jnp.zeros((1, LANE), F32).at[0, :N_GROUPS].set(b_route_g[l])
                   .at[0, N_GROUPS:N_GROUPS + N_EXPERTS].set(b_route_e[l]))
        xall, route, counts = _merge(xall, p, ya, yb, hf, hb, xconv, g_mlstm_norm[l][None], skip_m[l][None],
                                         w_branch[l].astype(MXU_DT), w_out[l].astype(MXU_DT), mod,
                                         g_norm2[l][None], w_route, b_route, n_batch, tps)

        block_e, n_valid, fill_blocks, n_live, pos_tiles, n_slots = _dispatch_plan(route, counts[0], n_tiles)
        xs = _moe_dispatch(xall, g_norm2[l][None], mod, fill_blocks, n_live, pos_tiles, n_slots, n_batch, tps)
        y_sorted = _moe_experts(xs, block_e, n_valid, w_e_gate[l].astype(MXU_DT),
                                w_e_up[l].astype(MXU_DT), w_e_down[l].astype(MXU_DT))
        xall = _combine(xall, route, mod, g_final[None], y_sorted, pos_tiles, n_batch, tps, l == depth - 1)

    return xall.reshape(n_batch, seq, d)[:, ctx_rows:]
```

```python
import functools
import math

import jax
import jax.numpy as jnp
from jax import lax
from jax.experimental import pallas as pl
from jax.experimental.pallas import tpu as pltpu

F32 = jnp.float32
MXU_DT = jnp.bfloat16
ACT_DT = jnp.bfloat16
HIGHEST = lax.Precision.HIGHEST

D_MODEL = 1024
N_MOD = 6
EPS = 1e-6
GRID_W = 64
H_A, DK = 4, 64
DV = 2 * DK
ROT_AX = DK // 2
ROPE_BASE = 10000.0
W_B, G_B, CHUNK_B = 512, 4, 128
H_C, DH_C, CONV_K, MLSTM_CHUNK = 4, 128, 3, 128
W_C = H_C * DH_C
N_GROUPS, EXP_PER_GROUP, TOP_K, D_EXPERT = 4, 8, 2, 512
N_EXPERTS = N_GROUPS * EXP_PER_GROUP
W_BR = 512

GATE_W = 3 * D_MODEL
P_GATE = 0
P_K = GATE_W
P_V = P_K + 512
P_XM = P_V + 512
P_Q = P_XM + 512
P_UV = P_Q + 512
P_Z = P_UV + 2 * W_B
IN_COLS = P_Z + W_C
ORIG_GATE0 = IN_COLS - GATE_W

LANE = 128
ROW_TILE = 256
PROJ_CHUNK = 512
MOE_BLOCK = 256
HALO = 16
DMA_UNROLL = 8
ATTN_KV_CHUNK = 256
ATTN_Q_ROWS = 256
LOG2_E = math.log2(math.e)
VMEM_LIMIT = 56 * 1024 * 1024
NEG_BIG = -1e30


def _cparams(*sem):
    return pltpu.CompilerParams(dimension_semantics=sem, vmem_limit_bytes=VMEM_LIMIT)


def _rms(x):
    return x * lax.rsqrt(jnp.mean(x * x, axis=-1, keepdims=True) + EPS)


def _sigmoid(x):
    return 0.5 * jnp.tanh(0.5 * x) + 0.5


def _split_bf16(x):
    hi = x.astype(jnp.bfloat16)
    return hi, (x - hi.astype(F32)).astype(jnp.bfloat16)


def _mod_kernel(c_ref, w_ref, b_ref, o_ref):
    c = c_ref[...]
    s = c * jax.nn.sigmoid(c)
    o_ref[...] = jnp.dot(s, w_ref[...], preferred_element_type=F32, precision=HIGHEST) + b_ref[...]


def _modulation(c_all, w_mod, b_mod):
    n_layer = w_mod.shape[0]
    mb, d = c_all.shape
    tn = 1024
    return pl.pallas_call(
        _mod_kernel,
        grid=(n_layer, N_MOD * d // tn),
        in_specs=[
            pl.BlockSpec((mb, d), lambda l, j: (0, 0)),
            pl.BlockSpec((None, d, tn), lambda l, j: (l, 0, j)),
            pl.BlockSpec((None, 1, tn), lambda l, j: (l, 0, j)),
        ],
        out_specs=pl.BlockSpec((None, mb, tn), lambda l, j: (l, 0, j)),
        out_shape=jax.ShapeDtypeStruct((n_layer, mb, N_MOD * d), F32),
        compiler_params=_cparams("parallel", "parallel"),
        name="modulation",
    )(c_all, w_mod, b_mod.reshape(n_layer, 1, N_MOD * d))


def _rope(acc, cos, sin_signed):
    w = acc.shape[1]
    lane = lax.broadcasted_iota(jnp.int32, acc.shape, 1)
    partner = jnp.where((lane & 16) == 0, pltpu.roll(acc, w - 16, 1), pltpu.roll(acc, 16, 1))
    reps = w // cos.shape[1]
    return acc * jnp.tile(cos, (1, reps)) + partner * jnp.tile(sin_signed, (1, reps))


def _inproj_kernel(x_ref, g_ref, mod_ref, cos_ref, sin_ref, w_ref, o_ref):
    x = x_ref[...]
    h = (_rms(x) * g_ref[...] * (1.0 + mod_ref[1:2, :]) + mod_ref[0:1, :]).astype(MXU_DT)
    k_chunk = P_K // PROJ_CHUNK
    q_chunk = P_Q // PROJ_CHUNK
    for c in range(IN_COLS // PROJ_CHUNK):
        cols = slice(c * PROJ_CHUNK, (c + 1) * PROJ_CHUNK)
        w0 = (ORIG_GATE0 + c * PROJ_CHUNK) % IN_COLS
        acc = jnp.dot(h, w_ref[:, w0:w0 + PROJ_CHUNK], preferred_element_type=F32)
        if c == k_chunk:
            acc = _rope(acc, cos_ref[...], sin_ref[...])
        elif c == q_chunk:
            acc = _rope(acc, cos_ref[...], sin_ref[...]) * (DK ** -0.5 * LOG2_E)
        o_ref[:, cols] = acc.astype(o_ref.dtype)


def _inproj(xall, g1, mod, cos, sin_signed, w_in, layer, n_batch, tps):
    r, d = xall.shape
    tm = ROW_TILE
    assert ORIG_GATE0 % PROJ_CHUNK == 0

    def mod_idx(i):
        return (jnp.where(i % tps == 0, n_batch, i // tps), 0, 0)

    return pl.pallas_call(
        _inproj_kernel,
        grid=(r // tm,),
        in_specs=[
            pl.BlockSpec((tm, d), lambda i: (i, 0)),
            pl.BlockSpec((1, d), lambda i: (0, 0)),
            pl.BlockSpec((None, N_MOD, d), mod_idx),
            pl.BlockSpec((tm, LANE), lambda i: (i % tps, 0)),
            pl.BlockSpec((tm, LANE), lambda i: (i % tps, 0)),
            pl.BlockSpec((None, d, IN_COLS), lambda i: (layer, 0, 0)),
        ],
        out_specs=pl.BlockSpec((tm, IN_COLS), lambda i: (i, 0)),
        out_shape=jax.ShapeDtypeStruct((r, IN_COLS), ACT_DT),
        compiler_params=_cparams("parallel"),
        name="inproj",
    )(xall, g1, mod, cos, sin_signed, w_in)


def _attn_kernel(dl_ref, g_ref, q_ref, k_ref, v_ref, o_ref, kt_ref, vext_ref, *, lam_init, ctx_rows):
    @pl.when(pl.program_id(2) == 0)
    def _():
        kt_ref[...] = k_ref[...].astype(F32).T.astype(kt_ref.dtype)
        one_hot = lax.broadcasted_iota(jnp.int32, v_ref.shape, 1) == 0
        vext_ref[:, :DV] = v_ref[...]
        vext_ref[:, DV:] = jnp.where(one_hot, 1.0, 0.0).astype(vext_ref.dtype)

    dl = dl_ref[...]
    lam = (jnp.exp(jnp.sum(dl[0:1] * dl[1:2], keepdims=True))
           - jnp.exp(jnp.sum(dl[2:3] * dl[3:4], keepdims=True)) + lam_init)
    kc = ATTN_KV_CHUNK

    def attend(kv_rows):
        for r0 in range(0, q_ref.shape[0], ATTN_Q_ROWS):
            attend_rows(kv_rows, slice(r0, r0 + ATTN_Q_ROWS))

    def attend_rows(kv_rows, rows):
        q = q_ref[rows, :]
        lane = lax.broadcasted_iota(jnp.int32, q.shape, 1)
        zero = jnp.zeros_like(q)
        qms = (jnp.where(lane < DK, q, zero), jnp.where(lane >= DK, q, zero))
        chunks = [slice(c * kc, (c + 1) * kc) for c in range(kv_rows // kc)]

        def scores(m, cols):
            return jnp.dot(qms[m], kt_ref[:, cols], preferred_element_type=F32)

        mxs = []
        for m in range(2):
            mrun = None
            for cols in chunks:
                s = scores(m, cols)
                for g in range(kc // LANE):
                    blk = s[:, g * LANE:(g + 1) * LANE]
                    mrun = blk if mrun is None else jnp.maximum(mrun, blk)
            mxs.append(jnp.max(mrun, axis=-1, keepdims=True))
        accs = [jnp.zeros((q.shape[0], 2 * DV), F32)] * 2
        for cols in chunks:
            for m in range(2):
                p = jnp.exp2(scores(m, cols) - mxs[m])
                accs[m] = accs[m] + jnp.dot(p.astype(MXU_DT), vext_ref[cols, :], preferred_element_type=F32)
        o = (accs[0][:, :DV] * (1.0 / accs[0][:, DV:DV + 1])
             - accs[1][:, :DV] * (lam / accs[1][:, DV:DV + 1]))
        o_ref[rows, :] = (_rms(o) * g_ref[...] * (1.0 - lam_init)).astype(o_ref.dtype)

    is_ctx = pl.program_id(2) == 0

    @pl.when(is_ctx)
    def _():
        attend(ctx_rows)

    @pl.when(jnp.logical_not(is_ctx))
    def _():
        attend(k_ref.shape[0])


def _attention(p, diff_lambda, g_sub, lam_init, n_batch, tps, ctx_rows):
    tq = ROW_TILE
    seq = tps * tq
    kern = functools.partial(_attn_kernel, lam_init=lam_init, ctx_rows=ctx_rows)
    return pl.pallas_call(
        kern,
        grid=(n_batch, H_A, tps),
        in_specs=[
            pl.BlockSpec((4, DK), lambda b, h, i: (0, 0)),
            pl.BlockSpec((1, DV), lambda b, h, i: (0, 0)),
            pl.BlockSpec((tq, LANE), lambda b, h, i: (b * tps + i, P_Q // LANE + h)),
            pl.BlockSpec((seq, LANE), lambda b, h, i: (b, P_K // LANE + h)),
            pl.BlockSpec((seq, LANE), lambda b, h, i: (b, P_V // LANE + h)),
        ],
        out_specs=pl.BlockSpec((tq, LANE), lambda b, h, i: (b * tps + i, h)),
        out_shape=jax.ShapeDtypeStruct((p.shape[0], H_A * DV), ACT_DT),
        scratch_shapes=[pltpu.VMEM((LANE, seq), MXU_DT), pltpu.VMEM((seq, 2 * DV), MXU_DT)],
        compiler_params=_cparams("parallel", "parallel", "arbitrary"),
        name="diff_attention",
    )(diff_lambda, g_sub, p, p, p)


def _gmlp_kernel(uv_ref, gv_ref, ws_ref, bs_ref, o_ref):
    uv = jax.nn.gelu(uv_ref[...].astype(F32))
    u = uv[:, :W_B]
    v = (_rms(uv[:, W_B:]) * gv_ref[...]).astype(MXU_DT)
    gw = W_B // G_B
    for c in range(uv.shape[0] // CHUNK_B):
        rows = slice(c * CHUNK_B, (c + 1) * CHUNK_B)
        for g in range(G_B):
            cols = slice(g * gw, (g + 1) * gw)
            mixed = jnp.dot(ws_ref[g], v[rows, cols], preferred_element_type=F32) + bs_ref[:, cols]
            o_ref[rows, cols] = (u[rows, cols] * mixed).astype(o_ref.dtype)


def _gmlp(p, g_v, w_s, b_full):
    r = p.shape[0]
    tm = ROW_TILE
    return pl.pallas_call(
        _gmlp_kernel,
        grid=(r // tm,),
        in_specs=[
            pl.BlockSpec((tm, 2 * W_B), lambda i: (i, P_UV // (2 * W_B))),
            pl.BlockSpec((1, W_B), lambda i: (0, 0)),
            pl.BlockSpec((G_B, CHUNK_B, CHUNK_B), lambda i: (0, 0, 0)),
            pl.BlockSpec((CHUNK_B, W_B), lambda i: (0, 0)),
        ],
        out_specs=pl.BlockSpec((tm, W_B), lambda i: (i, 0)),
        out_shape=jax.ShapeDtypeStruct((r, W_B), ACT_DT),
        compiler_params=_cparams("parallel"),
        name="gmlp",
    )(p, g_v, w_s, b_full)


def _mfeat_kernel(x_ref, prev_ref, next_ref, wc_ref, bc_ref, wqkv_ref, wif_ref, bif_ref,
                  xc_ref, q_ref, k_ref, v_ref, g_ref, *, tps):
    tm = x_ref.shape[0]
    j = pl.program_id(0) % tps
    seg_start = jnp.logical_or(j == 0, j == 1)
    seg_end = jnp.logical_or(j == 0, j == tps - 1)
    x = x_ref[...].astype(F32)
    prow = jnp.where(seg_start, 0.0, prev_ref[...].astype(F32)[HALO - 1:HALO, :])
    nrow = jnp.where(seg_end, 0.0, next_ref[...].astype(F32)[0:1, :])
    rid = lax.broadcasted_iota(jnp.int32, x.shape, 0)
    xp = jnp.where(rid == 0, prow, pltpu.roll(x, 1, 0))
    xn = jnp.where(rid == tm - 1, nrow, pltpu.roll(x, tm - 1, 0))
    xc = wc_ref[0:1, :] * xp + wc_ref[1:2, :] * x + wc_ref[2:3, :] * xn + bc_ref[...]
    xc = xc * _sigmoid(xc)
    xc_ref[...] = xc.astype(xc_ref.dtype)
    xcb = xc.astype(MXU_DT)
    xb = x_ref[...]
    qs, ks, vs = [], [], []
    for h in range(H_C):
        cols = slice(h * DH_C, (h + 1) * DH_C)
        qs.append(jnp.dot(xcb[:, cols], wqkv_ref[0, h], preferred_element_type=F32))
        ks.append(jnp.dot(xcb[:, cols], wqkv_ref[1, h], preferred_element_type=F32))
        vs.append(jnp.dot(xb[:, cols], wqkv_ref[2, h], preferred_element_type=F32))
    q = jnp.concatenate(qs, axis=1)
    k = jnp.concatenate(ks, axis=1)
    v = jnp.concatenate(vs, axis=1)
    qkv = jnp.concatenate([q, k, v], axis=1).astype(MXU_DT)
    g_ref[...] = jnp.dot(qkv, wif_ref[...], preferred_element_type=F32) + bif_ref[...]
    q_ref[...] = q.astype(q_ref.dtype)
    k_ref[...] = (k * (DH_C ** -0.5)).astype(k_ref.dtype)
    v_ref[...] = v.astype(v_ref.dtype)


def _mlstm_features(p, w_conv, b_conv, w_qkv, w_if_pad, b_if_pad, tps):
    r = p.shape[0]
    tm = ROW_TILE
    hpt = tm // HALO
    xm_blk = P_XM // W_C
    last_halo = r // HALO - 1
    kern = functools.partial(_mfeat_kernel, tps=tps)
    act = jax.ShapeDtypeStruct((r, W_C), ACT_DT)
    row_spec = pl.BlockSpec((tm, W_C), lambda i: (i, 0))
    return pl.pallas_call(
        kern,
        grid=(r // tm,),
        in_specs=[
            pl.BlockSpec((tm, W_C), lambda i: (i, xm_blk)),
            pl.BlockSpec((HALO, W_C), lambda i: (jnp.maximum(i * hpt - 1, 0), xm_blk)),
            pl.BlockSpec((HALO, W_C), lambda i: (jnp.minimum((i + 1) * hpt, last_halo), xm_blk)),
            pl.BlockSpec((CONV_K, W_C), lambda i: (0, 0)),
            pl.BlockSpec((1, W_C), lambda i: (0, 0)),
            pl.BlockSpec((3, H_C, DH_C, DH_C), lambda i: (0, 0, 0, 0)),
            pl.BlockSpec((3 * W_C, LANE), lambda i: (0, 0)),
            pl.BlockSpec((1, LANE), lambda i: (0, 0)),
        ],
        out_specs=[row_spec, row_spec, row_spec, row_spec,
                   pl.BlockSpec((tm, LANE), lambda i: (i, 0))],
        out_shape=[act, act, act, act, jax.ShapeDtypeStruct((r, LANE), F32)],
        compiler_params=_cparams("parallel"),
        name="mlstm_features",
    )(p, p, p, w_conv, b_conv, w_qkv, w_if_pad, b_if_pad)


def _lane_scan(x, op, reverse):
    n = x.shape[1]
    lane = lax.broadcasted_iota(jnp.int32, x.shape, 1)
    sh = 1
    while sh < n:
        if reverse:
            shifted, valid = pltpu.roll(x, n - sh, 1), lane < n - sh
        else:
            shifted, valid = pltpu.roll(x, sh, 1), lane >= sh
        x = jnp.where(valid, op(x, shifted), x)
        sh *= 2
    return x


def _mlstm_direction(gt, q_ref, k_ref, v_ref, c_ref, m_ref, h_ref, reverse):
    ln = gt.shape[1]
    nr = 2 * H_C
    ig = gt
    bcum = _lane_scan(jax.nn.log_sigmoid(pltpu.roll(gt, H_C, 0)), jnp.add, reverse)
    r = ig - bcum
    cmx = _lane_scan(r, jnp.maximum, reverse)
    g = bcum[:, 0:1] if reverse else bcum[:, ln - 1:ln]
    a_end = g + r
    m_loc = jnp.max(a_end, axis=1, keepdims=True)
    w_end = jnp.exp(a_end - m_loc)
    m_old = m_ref[:, 0:1]
    m_new = jnp.maximum(g + m_old, m_loc)
    a_old = jnp.exp(g + m_old - m_new)
    a_new = jnp.exp(m_loc - m_new)
    zt = jnp.concatenate([bcum, cmx, w_end, jnp.zeros((LANE - 3 * nr, ln), F32)], axis=0).T
    t_id = lax.broadcasted_iota(jnp.int32, (ln, ln), 0)
    s_id = lax.broadcasted_iota(jnp.int32, (ln, ln), 1)
    keep = (s_id >= t_id) if reverse else (s_id <= t_id)
    ones_col = (lax.broadcasted_iota(jnp.int32, (ln, DH_C), 1) == 0).astype(F32)
    nt = (((1,), (1,)), ((), ()))
    for h in range(H_C):
        cols = slice(h * DH_C, (h + 1) * DH_C)
        q = q_ref[:, cols]
        k = k_ref[:, cols]
        vext = jnp.concatenate([v_ref[:, cols].astype(F32), ones_col], axis=1)
        bc = zt[:, h:h + 1]
        cm = zt[:, nr + h:nr + h + 1]
        we = zt[:, 2 * nr + h:2 * nr + h + 1]
        m0 = m_old[h:h + 1, :]
        log_inter = bc + m0
        m_t = jnp.maximum(log_inter, bc + cm)
        dmat = jnp.exp(jnp.where(keep, bc + r[h:h + 1, :] - m_t, NEG_BIG))
        s = lax.dot_general(q, k, nt, preferred_element_type=F32) * dmat
        cext = c_ref[h]
        r1 = jnp.dot(s.astype(MXU_DT), vext.astype(MXU_DT), preferred_element_type=F32)
        r2 = lax.dot_general(q, cext.astype(MXU_DT), nt, preferred_element_type=F32)
        w_inter = jnp.exp(log_inter - m_t)
        num = r1[:, :DH_C] + w_inter * r2[:, :DH_C]
        den = r1[:, DH_C:DH_C + 1] + w_inter * r2[:, DH_C:DH_C + 1]
        hh = num / jnp.maximum(jnp.abs(den), jnp.exp(-m_t))
        h_ref[:, cols] = hh.astype(h_ref.dtype)
        vw_t = (vext * we).T.astype(MXU_DT)
        c_loc = jnp.dot(vw_t, k, preferred_element_type=F32)
        c_ref[h] = a_old[h:h + 1, :] * cext + a_new[h:h + 1, :] * c_loc
    m_ref[...] = jnp.broadcast_to(m_new, m_ref.shape)


def _mscan_kernel(gf_ref, qf_ref, kf_ref, vf_ref, gb_ref, qb_ref, kb_ref, vb_ref,
                  hf_ref, hb_ref, cf_ref, mf_ref, cb_ref, mb_ref):
    @pl.when(pl.program_id(1) == 0)
    def _():
        cf_ref[...] = jnp.zeros_like(cf_ref)
        cb_ref[...] = jnp.zeros_like(cb_ref)
        mf_ref[...] = jnp.zeros_like(mf_ref)
        mb_ref[...] = jnp.zeros_like(mb_ref)

    _mlstm_direction(gf_ref[...].T[0:2 * H_C], qf_ref, kf_ref, vf_ref, cf_ref, mf_ref, hf_ref, False)
    _mlstm_direction(gb_ref[...].T[2 * H_C:4 * H_C], qb_ref, kb_ref, vb_ref, cb_ref, mb_ref, hb_ref, True)


def _mlstm_scan(q, k, v, gates, n_batch, seq, ctx_rows):
    ln = MLSTM_CHUNK
    nck = seq // ln
    ncc = ctx_rows // ln

    def fwd(b, s):
        return (b * nck + s, 0)

    def bwd(b, s):
        return (b * nck + jnp.where(s < ncc, ncc - 1 - s, nck - 1 + ncc - s), 0)

    act = lambda idx: pl.BlockSpec((ln, W_C), idx)
    gsp = lambda idx: pl.BlockSpec((ln, LANE), idx)
    out = jax.ShapeDtypeStruct(q.shape, ACT_DT)
    return pl.pallas_call(
        _mscan_kernel,
        grid=(n_batch, nck),
        in_specs=[gsp(fwd), act(fwd), act(fwd), act(fwd), gsp(bwd), act(bwd), act(bwd), act(bwd)],
        out_specs=[act(fwd), act(bwd)],
        out_shape=[out, out],
        scratch_shapes=[
            pltpu.VMEM((H_C, 2 * DH_C, DH_C), F32), pltpu.VMEM((8, LANE), F32),
            pltpu.VMEM((H_C, 2 * DH_C, DH_C), F32), pltpu.VMEM((8, LANE), F32),
        ],
        compiler_params=_cparams("parallel", "arbitrary"),
        name="mlstm_scan",
    )(gates, q, k, v, gates, q, k, v)


def _route(logits, carry):
    lane = lax.broadcasted_iota(jnp.int32, logits.shape, 1).astype(F32)
    big = float(4 * LANE)
    gl = jnp.where(lane < N_GROUPS, logits, NEG_BIG)
    gmax = jnp.max(gl, axis=-1, keepdims=True)
    g_star = jnp.min(jnp.where(gl == gmax, lane, big), axis=-1, keepdims=True)
    p_g = 1.0 / jnp.sum(jnp.exp(gl - gmax), axis=-1, keepdims=True)
    e_lo = g_star * EXP_PER_GROUP + N_GROUPS
    in_group = jnp.logical_and(lane >= e_lo, lane < e_lo + EXP_PER_GROUP)
    el = jnp.where(in_group, logits, NEG_BIG)
    v1 = jnp.max(el, axis=-1, keepdims=True)
    i1 = jnp.min(jnp.where(el == v1, lane, big), axis=-1, keepdims=True)
    el2 = jnp.where(lane == i1, NEG_BIG, el)
    v2 = jnp.max(el2, axis=-1, keepdims=True)
    i2 = jnp.min(jnp.where(el2 == v2, lane, big), axis=-1, keepdims=True)
    e21 = jnp.exp(v2 - v1)
    w1 = p_g / (1.0 + e21)
    w2 = p_g * e21 / (1.0 + e21)
    e1 = i1 - N_GROUPS
    e2 = i2 - N_GROUPS
    oh1 = lane == e1
    oh2 = lane == e2
    both = jnp.where(jnp.logical_or(oh1, oh2), 1.0, 0.0)
    tm = logits.shape[0]
    lower = (lax.broadcasted_iota(jnp.int32, (tm, tm), 1) < lax.broadcasted_iota(jnp.int32, (tm, tm), 0))
    before = jnp.dot(jnp.where(lower, 1.0, 0.0).astype(MXU_DT), both.astype(MXU_DT),
                     preferred_element_type=F32) + carry
    rank1 = jnp.sum(jnp.where(oh1, before, 0.0), axis=-1, keepdims=True)
    rank2 = jnp.sum(jnp.where(oh2, before, 0.0), axis=-1, keepdims=True)
    out = jnp.where(lane == 0, e1, 0.0)
    out = jnp.where(lane == 1, e2, out)
    out = jnp.where(lane == 2, w1, out)
    out = jnp.where(lane == 3, w2, out)
    out = jnp.where(lane == 4, rank1, out)
    out = jnp.where(lane == 5, rank2, out)
    return out, carry + jnp.sum(both, axis=0, keepdims=True)


def _merge_kernel(x_ref, gate_ref, z_ref, ya_ref, yb_ref, hf_ref, hb_ref, xc_ref, gm_ref, sk_ref,
                  wbr_ref, wo_ref, mod_ref, g2_ref, wr_ref, br_ref, xo_ref, rt_ref, cnt_ref, carry_ref):
    @pl.when(pl.program_id(0) == 0)
    def _():
        carry_ref[...] = jnp.zeros_like(carry_ref)

    hs = hf_ref[...].astype(F32) + hb_ref[...].astype(F32)
    hn = jnp.concatenate([_rms(hs[:, h * DH_C:(h + 1) * DH_C]) for h in range(H_C)], axis=1) * gm_ref[...]
    yc = (hn + sk_ref[...] * xc_ref[...].astype(F32)) * _sigmoid(z_ref[...].astype(F32))
    ys = (ya_ref[...], yb_ref[...], yc.astype(MXU_DT))
    merged = None
    for i in range(3):
        gate = _sigmoid(gate_ref[:, i * D_MODEL:(i + 1) * D_MODEL].astype(F32))
        term = gate * jnp.dot(ys[i], wbr_ref[i], preferred_element_type=F32)
        merged = term if merged is None else merged + term
    y = jnp.dot(merged.astype(MXU_DT), wo_ref[...], preferred_element_type=F32)
    xn = x_ref[...] + mod_ref[2:3, :] * y
    xo_ref[...] = xn
    h2 = _rms(xn) * g2_ref[...] * (1.0 + mod_ref[4:5, :]) + mod_ref[3:4, :]
    h_hi, h_lo = _split_bf16(h2)
    logits = (jnp.dot(h_hi, wr_ref[0], preferred_element_type=F32)
              + jnp.dot(h_lo, wr_ref[0], preferred_element_type=F32)
              + jnp.dot(h_hi, wr_ref[1], preferred_element_type=F32)) + br_ref[...]
    route, carry = _route(logits, carry_ref[0:1, :])
    rt_ref[...] = route
    carry_ref[0:1, :] = carry
    cnt_ref[...] = jnp.broadcast_to(carry, cnt_ref.shape)


def _merge(xall, p, ya, yb, hf, hb, xconv, g_m, skip, w_br, w_o, mod, g2, w_route, b_route, n_batch, tps):
    r, d = xall.shape
    tm = ROW_TILE

    def mod_idx(i):
        return (jnp.where(i % tps == 0, n_batch, i // tps), 0, 0)

    row = lambda w: pl.BlockSpec((tm, w), lambda i: (i, 0))
    full = lambda shape: pl.BlockSpec(shape, lambda i: (0,) * len(shape))
    return pl.pallas_call(
        _merge_kernel,
        grid=(r // tm,),
        in_specs=[
            row(d),
            pl.BlockSpec((tm, GATE_W), lambda i: (i, P_GATE // GATE_W)),
            pl.BlockSpec((tm, W_C), lambda i: (i, P_Z // W_C)),
            row(W_BR), row(W_BR), row(W_C), row(W_C), row(W_C),
            full((1, W_C)), full((1, W_C)),
            full((3, W_BR, d)), full((d, d)),
            pl.BlockSpec((None, N_MOD, d), mod_idx),
            full((1, d)), full((2, d, LANE)), full((1, LANE)),
        ],
        out_specs=[row(d), row(LANE), full((8, LANE))],
        out_shape=[jax.ShapeDtypeStruct((r, d), F32), jax.ShapeDtypeStruct((r, LANE), F32),
                   jax.ShapeDtypeStruct((8, LANE), F32)],
        scratch_shapes=[pltpu.VMEM((8, LANE), F32)],
        compiler_params=_cparams("arbitrary"),
        name="merge_route",
    )(xall, p, p, ya, yb, hf, hb, xconv, g_m, skip, w_br, w_o, mod, g2, w_route, b_route)


def _row_copy(src_hbm, src_row, dst_ref, dst_row, sem):
    return pltpu.make_async_copy(src_hbm.at[pl.ds(src_row, 1), :], dst_ref.at[pl.ds(dst_row, 1), :], sem)


def _dispatch_kernel(fill_ref, nlive_ref, pos_ref, x_ref, g2_ref, mod_ref, xs_hbm, hbuf, zbuf, zsem, sem):
    i = pl.program_id(0)
    n = pl.num_programs(0)
    tm = x_ref.shape[0]
    slot = i % 2
    bm = zbuf.shape[0]
    n_blocks = xs_hbm.shape[0] // bm

    @pl.when(i == 0)
    def _():
        zbuf[...] = jnp.zeros_like(zbuf)

        def fill_block(b):
            return pltpu.make_async_copy(zbuf, xs_hbm.at[pl.ds(b * bm, bm), :], zsem)

        def fill_last(k, carry):
            fill_block(fill_ref[k]).start()
            return carry

        def fill_unused(b, carry):
            fill_block(b).start()
            return carry

        def wait_fill(k, carry):
            fill_block(0).wait()
            return carry

        n_live = nlive_ref[0]
        n_used = nlive_ref[1]
        lax.fori_loop(0, n_used, fill_last, 0)
        lax.fori_loop(n_live, n_blocks, fill_unused, 0)
        lax.fori_loop(0, n_used + n_blocks - n_live, wait_fill, 0)

    hbuf[slot] = _rms(x_ref[...]) * g2_ref[...] * (1.0 + mod_ref[4:5, :]) + mod_ref[3:4, :]

    def body(c, carry):
        for u in range(DMA_UNROLL):
            r = c * DMA_UNROLL + u
            for j in range(TOP_K):
                pltpu.make_async_copy(hbuf.at[slot, pl.ds(r, 1), :],
                                      xs_hbm.at[pl.ds(pos_ref[0, j * tm + r], 1), :], sem.at[slot]).start()
        return carry

    lax.fori_loop(0, tm // DMA_UNROLL, body, 0)

    def wait_tile(s):
        pltpu.make_async_copy(hbuf.at[s], xs_hbm.at[pl.ds(0, tm), :], sem.at[s]).wait()
        pltpu.make_async_copy(hbuf.at[s], xs_hbm.at[pl.ds(0, tm), :], sem.at[s]).wait()

    @pl.when(i > 0)
    def _():
        wait_tile(1 - slot)

    @pl.when(i == n - 1)
    def _():
        wait_tile(slot)


def _moe_dispatch(xall, g2, mod, fill_blocks, n_live, pos_tiles, n_slots, n_batch, tps):
    r, d = xall.shape
    tm = ROW_TILE
    assert TOP_K == 2

    def mod_idx(i, fb, nl):
        return (jnp.where(i % tps == 0, n_batch, i // tps), 0, 0)

    grid_spec = pltpu.PrefetchScalarGridSpec(
        num_scalar_prefetch=2,
        grid=(r // tm,),
        in_specs=[
            pl.BlockSpec((None, 1, TOP_K * tm), lambda i, fb, nl: (i, 0, 0), memory_space=pltpu.SMEM),
            pl.BlockSpec((tm, d), lambda i, fb, nl: (i, 0)),
            pl.BlockSpec((1, d), lambda i, fb, nl: (0, 0)),
            pl.BlockSpec((None, N_MOD, d), mod_idx),
        ],
        out_specs=pl.BlockSpec(memory_space=pl.ANY),
        scratch_shapes=[pltpu.VMEM((2, tm, d), F32), pltpu.VMEM((MOE_BLOCK, d), F32),
                        pltpu.SemaphoreType.DMA(()), pltpu.SemaphoreType.DMA((2,))],
    )
    return pl.pallas_call(
        _dispatch_kernel,
        grid_spec=grid_spec,
        out_shape=jax.ShapeDtypeStruct((n_slots, d), F32),
        compiler_params=_cparams("arbitrary"),
        name="moe_dispatch",
    )(fill_blocks, n_live, pos_tiles, xall, g2, mod)


def _moe_kernel(be_ref, nv_ref, x_ref, wg_ref, wu_ref, wd_ref, y_ref):
    n_valid = nv_ref[pl.program_id(0)]

    @pl.when(n_valid > 0)
    def _():
        x = x_ref[...].astype(MXU_DT)
        a = jnp.dot(x, wg_ref[...], preferred_element_type=F32)
        u = jnp.dot(x, wu_ref[...], preferred_element_type=F32)
        act = (a * _sigmoid(a) * u).astype(MXU_DT)
        y_ref[...] = jnp.dot(act, wd_ref[...], preferred_element_type=F32)

    @pl.when(n_valid <= 0)
    def _():
        y_ref[...] = jnp.zeros_like(y_ref)


def _moe_experts(xs, block_e, n_valid, w_gate, w_up, w_down, layer):
    n_slots, d = xs.shape
    bm = MOE_BLOCK
    grid_spec = pltpu.PrefetchScalarGridSpec(
        num_scalar_prefetch=2,
        grid=(n_slots // bm,),
        in_specs=[
            pl.BlockSpec((bm, d), lambda i, be, nv: (i, 0)),
            pl.BlockSpec((None, None, d, D_EXPERT), lambda i, be, nv: (layer, be[i], 0, 0)),
            pl.BlockSpec((None, None, d, D_EXPERT), lambda i, be, nv: (layer, be[i], 0, 0)),
            pl.BlockSpec((None, None, D_EXPERT, d), lambda i, be, nv: (layer, be[i], 0, 0)),
        ],
        out_specs=pl.BlockSpec((bm, d), lambda i, be, nv: (i, 0)),
    )
    return pl.pallas_call(
        _moe_kernel,
        grid_spec=grid_spec,
        out_shape=jax.ShapeDtypeStruct((n_slots, d), F32),
        compiler_params=_cparams("arbitrary"),
        name="moe_experts",
    )(block_e, n_valid, xs, w_gate, w_up, w_down)


def _combine_kernel(pos_ref, pos_next_ref, x_ref, rt_ref, mod_ref, gf_ref, yb_hbm, xo_ref, ybuf, sem, *, final):
    i = pl.program_id(0)
    n = pl.num_programs(0)
    tm = x_ref.shape[0]

    def start_gather(ids_ref, slot):
        def body(c, carry):
            for u in range(DMA_UNROLL):
                r = c * DMA_UNROLL + u
                _row_copy(yb_hbm, ids_ref[0, r], ybuf.at[slot], r, sem.at[slot]).start()
            return carry
        lax.fori_loop(0, TOP_K * tm // DMA_UNROLL, body, 0)

    @pl.when(i == 0)
    def _():
        start_gather(pos_ref, 0)

    @pl.when(i + 1 < n)
    def _():
        start_gather(pos_next_ref, (i + 1) % 2)

    slot = i % 2
    pltpu.make_async_copy(yb_hbm.at[pl.ds(0, TOP_K * tm), :], ybuf.at[slot], sem.at[slot]).wait()
    rt = rt_ref[...]
    f = rt[:, 2:3] * ybuf[slot, 0:tm, :] + rt[:, 3:4] * ybuf[slot, tm:2 * tm, :]
    xn = x_ref[...] + mod_ref[5:6, :] * f
    xo_ref[...] = _rms(xn) * gf_ref[...] if final else xn


def _combine(xall, route, mod, g_final, yb, pos, n_batch, tps, final):
    r, d = xall.shape
    tm = ROW_TILE
    nt = r // tm
    last = nt - 1

    def mod_idx(i):
        return (jnp.where(i % tps == 0, n_batch, i // tps), 0, 0)

    row = lambda w: pl.BlockSpec((tm, w), lambda i: (i, 0))
    return pl.pallas_call(
        functools.partial(_combine_kernel, final=final),
        grid=(nt,),
        in_specs=[
            pl.BlockSpec((None, 1, TOP_K * tm), lambda i: (i, 0, 0), memory_space=pltpu.SMEM),
            pl.BlockSpec((None, 1, TOP_K * tm), lambda i: (jnp.minimum(i + 1, last), 0, 0),
                         memory_space=pltpu.SMEM),
            row(d), row(LANE),
            pl.BlockSpec((None, N_MOD, d), mod_idx),
            pl.BlockSpec((1, d), lambda i: (0, 0)),
            pl.BlockSpec(memory_space=pl.ANY),
        ],
        out_specs=row(d),
        out_shape=jax.ShapeDtypeStruct((r, d), F32),
        scratch_shapes=[pltpu.VMEM((2, TOP_K * tm, d), F32), pltpu.SemaphoreType.DMA((2,))],
        compiler_params=_cparams("arbitrary"),
        name="moe_combine",
    )(pos, pos, xall, route, mod, g_final, yb)


def _dispatch_plan(route, counts_f, n_tiles):
    n = route.shape[0]
    bm = MOE_BLOCK
    n_blocks = -(-(n * TOP_K + N_EXPERTS * (bm - 1)) // bm)
    counts = counts_f[:N_EXPERTS].astype(jnp.int32)
    padded = (counts + bm - 1) // bm * bm
    pend = jnp.cumsum(padded)
    seg_off = pend - padded
    eid = route[:, 0:TOP_K].astype(jnp.int32)
    rank = route[:, 4:4 + TOP_K].astype(jnp.int32)
    experts = jnp.arange(N_EXPERTS, dtype=jnp.int32)
    pos = jnp.sum(jnp.where(eid[..., None] == experts, seg_off, 0), axis=-1) + rank
    blk0 = jnp.arange(n_blocks, dtype=jnp.int32) * bm
    block_e = jnp.minimum(jnp.sum((blk0[:, None] >= pend[None, :]).astype(jnp.int32), axis=1), N_EXPERTS - 1)
    seg_end = seg_off + counts
    n_valid = jnp.clip(seg_end[block_e] - blk0, 0, bm).astype(jnp.int32)
    pos_tiles = pos.reshape(n_tiles, n // n_tiles, TOP_K).transpose(0, 2, 1).reshape(n_tiles, 1, -1)
    used = counts > 0
    last_blocks = jnp.sort(jnp.where(used, pend // bm - 1, n_blocks)).astype(jnp.int32)
    n_live = jnp.stack([pend[-1] // bm, jnp.sum(used)]).astype(jnp.int32)
    return block_e.astype(jnp.int32), n_valid, last_blocks, n_live, pos_tiles, n_blocks * bm


def _rope_tables(seq, ctx_rows):
    t = jnp.arange(seq - ctx_rows)
    inv = ROPE_BASE ** (-jnp.arange(0, ROT_AX, 2, dtype=F32) / ROT_AX)
    ang_r = (t // GRID_W).astype(F32)[:, None] * inv
    ang_c = (t % GRID_W).astype(F32)[:, None] * inv
    cos64 = jnp.concatenate([jnp.cos(ang_r)] * 2 + [jnp.cos(ang_c)] * 2, axis=1)
    sin64 = jnp.concatenate([-jnp.sin(ang_r), jnp.sin(ang_r), -jnp.sin(ang_c), jnp.sin(ang_c)], axis=1)
    cos = jnp.concatenate([jnp.ones((ctx_rows, DK), F32), cos64], axis=0)
    sin = jnp.concatenate([jnp.zeros((ctx_rows, DK), F32), sin64], axis=0)
    return jnp.tile(cos, (1, LANE // DK)), jnp.tile(sin, (1, LANE // DK))


def kernel(x, c, ctx, c_ctx, w_mod, b_mod, g_norm1, g_norm2, w_in, diff_lambda, g_diff_subln, w_gmlp_s, b_gmlp_s, g_gmlp_v, w_conv_m, b_conv_m, w_qkv_m, w_if_m, b_if_m, g_mlstm_norm, skip_m, w_branch, w_out, w_route_g, b_route_g, w_route_e, b_route_e, w_e_gate, w_e_up, w_e_down, g_final):
    n_batch, t_lat, d = x.shape
    ctx_rows = ctx.shape[1]
    depth = w_in.shape[0]
    seq = ctx_rows + t_lat
    assert d == D_MODEL and ctx_rows == ROW_TILE and t_lat % ROW_TILE == 0 and t_lat % GRID_W == 0
    tps = seq // ROW_TILE
    n_tiles = n_batch * tps

    xall = jnp.concatenate([ctx, x], axis=1).reshape(n_batch * seq, d)
    mb = -(-(n_batch + 1) // 8) * 8
    c_all = jnp.zeros((mb, d), F32).at[:n_batch].set(c).at[n_batch].set(c_ctx)
    mod_all = _modulation(c_all, w_mod, b_mod)[:, :n_batch + 1].reshape(depth, n_batch + 1, N_MOD, d)
    cos, sin_signed = _rope_tables(seq, ctx_rows)
    w_in_b = w_in.astype(MXU_DT)
    w_eg_b, w_eu_b, w_ed_b = w_e_gate.astype(MXU_DT), w_e_up.astype(MXU_DT), w_e_down.astype(MXU_DT)

    for l in range(depth):
        lam_init = 0.8 - 0.6 * math.exp(-0.3 * l)
        mod = mod_all[l]
        p = _inproj(xall, g_norm1[l][None], mod, cos, sin_signed, w_in_b, l, n_batch, tps)

        ya = _attention(p, diff_lambda[l], g_diff_subln[l][None], lam_init, n_batch, tps, ctx_rows)

        b_full = jnp.repeat(b_gmlp_s[l].T, W_B // G_B, axis=1)
        yb = _gmlp(p, g_gmlp_v[l][None], w_gmlp_s[l].astype(MXU_DT), b_full)

        w_if_pad = jnp.zeros((3 * W_C, LANE), F32).at[:, :4 * H_C].set(w_if_m[l]).astype(MXU_DT)
        b_if_pad = jnp.zeros((1, LANE), F32).at[0, :4 * H_C].set(b_if_m[l])
        xconv, q_m, k_m, v_m, gates = _mlstm_features(
            p, w_conv_m[l], b_conv_m[l][None], w_qkv_m[l].astype(MXU_DT), w_if_pad, b_if_pad, tps)
        hf, hb = _mlstm_scan(q_m, k_m, v_m, gates, n_batch, seq, ctx_rows)

        w_route = (jnp.zeros((d, LANE), F32).at[:, :N_GROUPS].set(w_route_g[l])
                   .at[:, N_GROUPS:N_GROUPS + N_EXPERTS].set(w_route_e[l]))
        w_route = jnp.stack(_split_bf16(w_route))
        b_route = (jnp.zeros((1, LANE), F32).at[0, :N_GROUPS].set(b_route_g[l])
                   .at[0, N_GROUPS:N_GROUPS + N_EXPERTS].set(b_route_e[l]))
        xall, route, counts = _merge(xall, p, ya, yb, hf, hb, xconv, g_mlstm_norm[l][None], skip_m[l][None],
                                         w_branch[l].astype(MXU_DT), w_out[l].astype(MXU_DT), mod,
                                         g_norm2[l][None], w_route, b_route, n_batch, tps)

        block_e, n_valid, fill_blocks, n_live, pos_tiles, n_slots = _dispatch_plan(route, counts[0], n_tiles)
        xs = _moe_dispatch(xall, g_norm2[l][None], mod, fill_blocks, n_live, pos_tiles, n_slots, n_batch, tps)
        y_sorted = _moe_experts(xs, block_e, n_valid, w_eg_b, w_eu_b, w_ed_b, l)
        xall = _combine(xall, route, mod, g_final[None], y_sorted, pos_tiles, n_batch, tps, l == depth - 1)

    return xall.reshape(n_batch, seq, d)[:, ctx_rows:]
```

```python
import functools
import math

import jax
import jax.numpy as jnp
from jax import lax
from jax.experimental import pallas as pl
from jax.experimental.pallas import tpu as pltpu

F32 = jnp.float32
MXU_DT = jnp.bfloat16
ACT_DT = jnp.bfloat16
HIGHEST = lax.Precision.HIGHEST

D_MODEL = 1024
N_MOD = 6
EPS = 1e-6
GRID_W = 64
H_A, DK = 4, 64
DV = 2 * DK
ROT_AX = DK // 2
ROPE_BASE = 10000.0
W_B, G_B, CHUNK_B = 512, 4, 128
H_C, DH_C, CONV_K, MLSTM_CHUNK = 4, 128, 3, 128
W_C = H_C * DH_C
N_GROUPS, EXP_PER_GROUP, TOP_K, D_EXPERT = 4, 8, 2, 512
N_EXPERTS = N_GROUPS * EXP_PER_GROUP
W_BR = 512

GATE_W = 3 * D_MODEL
P_GATE = 0
P_K = GATE_W
P_V = P_K + 512
P_XM = P_V + 512
P_Q = P_XM + 512
P_UV = P_Q + 512
P_Z = P_UV + 2 * W_B
IN_COLS = P_Z + W_C
ORIG_GATE0 = IN_COLS - GATE_W

LANE = 128
ROW_TILE = 256
PROJ_CHUNK = 512
MOE_BLOCK = 256
HALO = 16
DMA_UNROLL = 8
ATTN_KV_CHUNK = 256
ATTN_Q_ROWS = 256
LOG2_E = math.log2(math.e)
VMEM_LIMIT = 56 * 1024 * 1024
NEG_BIG = -1e30


def _cparams(*sem):
    return pltpu.CompilerParams(dimension_semantics=sem, vmem_limit_bytes=VMEM_LIMIT)


def _rms(x):
    return x * lax.rsqrt(jnp.mean(x * x, axis=-1, keepdims=True) + EPS)


def _sigmoid(x):
    return 0.5 * jnp.tanh(0.5 * x) + 0.5


def _split_bf16(x):
    hi = x.astype(jnp.bfloat16)
    return hi, (x - hi.astype(F32)).astype(jnp.bfloat16)


def _mod_kernel(c_ref, w_ref, b_ref, o_ref):
    c = c_ref[...]
    s = c * jax.nn.sigmoid(c)
    o_ref[...] = jnp.dot(s, w_ref[...], preferred_element_type=F32, precision=HIGHEST) + b_ref[...]


def _modulation(c_all, w_mod, b_mod):
    n_layer = w_mod.shape[0]
    mb, d = c_all.shape
    tn = 1024
    return pl.pallas_call(
        _mod_kernel,
        grid=(n_layer, N_MOD * d // tn),
        in_specs=[
            pl.BlockSpec((mb, d), lambda l, j: (0, 0)),
            pl.BlockSpec((None, d, tn), lambda l, j: (l, 0, j)),
            pl.BlockSpec((None, 1, tn), lambda l, j: (l, 0, j)),
        ],
        out_specs=pl.BlockSpec((None, mb, tn), lambda l, j: (l, 0, j)),
        out_shape=jax.ShapeDtypeStruct((n_layer, mb, N_MOD * d), F32),
        compiler_params=_cparams("parallel", "parallel"),
        name="modulation",
    )(c_all, w_mod, b_mod.reshape(n_layer, 1, N_MOD * d))


def _rope(acc, cos, sin_signed):
    w = acc.shape[1]
    lane = lax.broadcasted_iota(jnp.int32, acc.shape, 1)
    partner = jnp.where((lane & 16) == 0, pltpu.roll(acc, w - 16, 1), pltpu.roll(acc, 16, 1))
    reps = w // cos.shape[1]
    return acc * jnp.tile(cos, (1, reps)) + partner * jnp.tile(sin_signed, (1, reps))


def _inproj_kernel(x_ref, g_ref, mod_ref, cos_ref, sin_ref, w_ref, o_ref):
    x = x_ref[...]
    h = (_rms(x) * g_ref[...] * (1.0 + mod_ref[1:2, :]) + mod_ref[0:1, :]).astype(MXU_DT)
    k_chunk = P_K // PROJ_CHUNK
    q_chunk = P_Q // PROJ_CHUNK
    for c in range(IN_COLS // PROJ_CHUNK):
        cols = slice(c * PROJ_CHUNK, (c + 1) * PROJ_CHUNK)
        w0 = (ORIG_GATE0 + c * PROJ_CHUNK) % IN_COLS
        acc = jnp.dot(h, w_ref[:, w0:w0 + PROJ_CHUNK], preferred_element_type=F32)
        if c == k_chunk:
            acc = _rope(acc, cos_ref[...], sin_ref[...])
        elif c == q_chunk:
            acc = _rope(acc, cos_ref[...], sin_ref[...]) * (DK ** -0.5 * LOG2_E)
        o_ref[:, cols] = acc.astype(o_ref.dtype)


def _inproj(xall, g1, mod, cos, sin_signed, w_in, layer, n_batch, tps):
    r, d = xall.shape
    tm = ROW_TILE
    assert ORIG_GATE0 % PROJ_CHUNK == 0

    def mod_idx(i):
        return (jnp.where(i % tps == 0, n_batch, i // tps), 0, 0)

    return pl.pallas_call(
        _inproj_kernel,
        grid=(r // tm,),
        in_specs=[
            pl.BlockSpec((tm, d), lambda i: (i, 0)),
            pl.BlockSpec((1, d), lambda i: (0, 0)),
            pl.BlockSpec((None, N_MOD, d), mod_idx),
            pl.BlockSpec((tm, LANE), lambda i: (i % tps, 0)),
            pl.BlockSpec((tm, LANE), lambda i: (i % tps, 0)),
            pl.BlockSpec((None, d, IN_COLS), lambda i: (layer, 0, 0)),
        ],
        out_specs=pl.BlockSpec((tm, IN_COLS), lambda i: (i, 0)),
        out_shape=jax.ShapeDtypeStruct((r, IN_COLS), ACT_DT),
        compiler_params=_cparams("parallel"),
        name="inproj",
    )(xall, g1, mod, cos, sin_signed, w_in)


def _attn_kernel(dl_ref, g_ref, q_ref, k_ref, v_ref, o_ref, kt_ref, vext_ref, *, lam_init, ctx_rows):
    @pl.when(pl.program_id(2) == 0)
    def _():
        kt_ref[...] = k_ref[...].astype(F32).T.astype(kt_ref.dtype)
        one_hot = lax.broadcasted_iota(jnp.int32, v_ref.shape, 1) == 0
        vext_ref[:, :DV] = v_ref[...]
        vext_ref[:, DV:] = jnp.where(one_hot, 1.0, 0.0).astype(vext_ref.dtype)

    dl = dl_ref[...]
    lam = (jnp.exp(jnp.sum(dl[0:1] * dl[1:2], keepdims=True))
           - jnp.exp(jnp.sum(dl[2:3] * dl[3:4], keepdims=True)) + lam_init)
    kc = ATTN_KV_CHUNK

    def attend(kv_rows):
        for r0 in range(0, q_ref.shape[0], ATTN_Q_ROWS):
            attend_rows(kv_rows, slice(r0, r0 + ATTN_Q_ROWS))

    def attend_rows(kv_rows, rows):
        q = q_ref[rows, :]
        lane = lax.broadcasted_iota(jnp.int32, q.shape, 1)
        zero = jnp.zeros_like(q)
        qms = (jnp.where(lane < DK, q, zero), jnp.where(lane >= DK, q, zero))
        chunks = [slice(c * kc, (c + 1) * kc) for c in range(kv_rows // kc)]

        def scores(m, cols):
            return jnp.dot(qms[m], kt_ref[:, cols], preferred_element_type=F32)

        mxs = []
        for m in range(2):
            mrun = None
            for cols in chunks:
                s = scores(m, cols)
                for g in range(kc // LANE):
                    blk = s[:, g * LANE:(g + 1) * LANE]
                    mrun = blk if mrun is None else jnp.maximum(mrun, blk)
            mxs.append(jnp.max(mrun, axis=-1, keepdims=True))
        accs = [jnp.zeros((q.shape[0], 2 * DV), F32)] * 2
        for cols in chunks:
            for m in range(2):
                p = jnp.exp2(scores(m, cols) - mxs[m])
                accs[m] = accs[m] + jnp.dot(p.astype(MXU_DT), vext_ref[cols, :], preferred_element_type=F32)
        o = (accs[0][:, :DV] * (1.0 / accs[0][:, DV:DV + 1])
             - accs[1][:, :DV] * (lam / accs[1][:, DV:DV + 1]))
        o_ref[rows, :] = (_rms(o) * g_ref[...] * (1.0 - lam_init)).astype(o_ref.dtype)

    is_ctx = pl.program_id(2) == 0

    @pl.when(is_ctx)
    def _():
        attend(ctx_rows)

    @pl.when(jnp.logical_not(is_ctx))
    def _():
        attend(k_ref.shape[0])


def _attention(p, diff_lambda, g_sub, lam_init, n_batch, tps, ctx_rows):
    tq = ROW_TILE
    seq = tps * tq
    kern = functools.partial(_attn_kernel, lam_init=lam_init, ctx_rows=ctx_rows)
    return pl.pallas_call(
        kern,
        grid=(n_batch, H_A, tps),
        in_specs=[
            pl.BlockSpec((4, DK), lambda b, h, i: (0, 0)),
            pl.BlockSpec((1, DV), lambda b, h, i: (0, 0)),
            pl.BlockSpec((tq, LANE), lambda b, h, i: (b * tps + i, P_Q // LANE + h)),
            pl.BlockSpec((seq, LANE), lambda b, h, i: (b, P_K // LANE + h)),
            pl.BlockSpec((seq, LANE), lambda b, h, i: (b, P_V // LANE + h)),
        ],
        out_specs=pl.BlockSpec((tq, LANE), lambda b, h, i: (b * tps + i, h)),
        out_shape=jax.ShapeDtypeStruct((p.shape[0], H_A * DV), ACT_DT),
        scratch_shapes=[pltpu.VMEM((LANE, seq), MXU_DT), pltpu.VMEM((seq, 2 * DV), MXU_DT)],
        compiler_params=_cparams("parallel", "parallel", "arbitrary"),
        name="diff_attention",
    )(diff_lambda, g_sub, p, p, p)


def _gmlp_kernel(uv_ref, gv_ref, ws_ref, bs_ref, o_ref):
    uv = jax.nn.gelu(uv_ref[...].astype(F32))
    u = uv[:, :W_B]
    v = (_rms(uv[:, W_B:]) * gv_ref[...]).astype(MXU_DT)
    gw = W_B // G_B
    for c in range(uv.shape[0] // CHUNK_B):
        rows = slice(c * CHUNK_B, (c + 1) * CHUNK_B)
        for g in range(G_B):
            cols = slice(g * gw, (g + 1) * gw)
            mixed = jnp.dot(ws_ref[g], v[rows, cols], preferred_element_type=F32) + bs_ref[:, cols]
            o_ref[rows, cols] = (u[rows, cols] * mixed).astype(o_ref.dtype)


def _gmlp(p, g_v, w_s, b_full):
    r = p.shape[0]
    tm = ROW_TILE
    return pl.pallas_call(
        _gmlp_kernel,
        grid=(r // tm,),
        in_specs=[
            pl.BlockSpec((tm, 2 * W_B), lambda i: (i, P_UV // (2 * W_B))),
            pl.BlockSpec((1, W_B), lambda i: (0, 0)),
            pl.BlockSpec((G_B, CHUNK_B, CHUNK_B), lambda i: (0, 0, 0)),
            pl.BlockSpec((CHUNK_B, W_B), lambda i: (0, 0)),
        ],
        out_specs=pl.BlockSpec((tm, W_B), lambda i: (i, 0)),
        out_shape=jax.ShapeDtypeStruct((r, W_B), ACT_DT),
        compiler_params=_cparams("parallel"),
        name="gmlp",
    )(p, g_v, w_s, b_full)


def _mfeat_kernel(x_ref, prev_ref, next_ref, wc_ref, bc_ref, wqkv_ref, wif_ref, bif_ref,
                  xc_ref, q_ref, k_ref, v_ref, g_ref, *, tps):
    tm = x_ref.shape[0]
    j = pl.program_id(0) % tps
    seg_start = jnp.logical_or(j == 0, j == 1)
    seg_end = jnp.logical_or(j == 0, j == tps - 1)
    x = x_ref[...].astype(F32)
    prow = jnp.where(seg_start, 0.0, prev_ref[...].astype(F32)[HALO - 1:HALO, :])
    nrow = jnp.where(seg_end, 0.0, next_ref[...].astype(F32)[0:1, :])
    rid = lax.broadcasted_iota(jnp.int32, x.shape, 0)
    xp = jnp.where(rid == 0, prow, pltpu.roll(x, 1, 0))
    xn = jnp.where(rid == tm - 1, nrow, pltpu.roll(x, tm - 1, 0))
    xc = wc_ref[0:1, :] * xp + wc_ref[1:2, :] * x + wc_ref[2:3, :] * xn + bc_ref[...]
    xc = xc * _sigmoid(xc)
    xc_ref[...] = xc.astype(xc_ref.dtype)
    xcb = xc.astype(MXU_DT)
    xb = x_ref[...]
    qs, ks, vs = [], [], []
    for h in range(H_C):
        cols = slice(h * DH_C, (h + 1) * DH_C)
        qs.append(jnp.dot(xcb[:, cols], wqkv_ref[0, h], preferred_element_type=F32))
        ks.append(jnp.dot(xcb[:, cols], wqkv_ref[1, h], preferred_element_type=F32))
        vs.append(jnp.dot(xb[:, cols], wqkv_ref[2, h], preferred_element_type=F32))
    q = jnp.concatenate(qs, axis=1)
    k = jnp.concatenate(ks, axis=1)
    v = jnp.concatenate(vs, axis=1)
    qkv = jnp.concatenate([q, k, v], axis=1).astype(MXU_DT)
    gates = jnp.dot(qkv, wif_ref[...], preferred_element_type=F32) + bif_ref[...]
    g_ref[...] = gates.T[0:g_ref.shape[0], :]
    q_ref[...] = q.T.astype(q_ref.dtype)
    k_ref[...] = (k * (DH_C ** -0.5)).astype(k_ref.dtype)
    v_ref[...] = v.T.astype(v_ref.dtype)


def _mlstm_features(p, w_conv, b_conv, w_qkv, w_if_pad, b_if_pad, tps):
    r = p.shape[0]
    tm = ROW_TILE
    hpt = tm // HALO
    xm_blk = P_XM // W_C
    last_halo = r // HALO - 1
    kern = functools.partial(_mfeat_kernel, tps=tps)
    act = jax.ShapeDtypeStruct((r, W_C), ACT_DT)
    act_t = jax.ShapeDtypeStruct((W_C, r), ACT_DT)
    row_spec = pl.BlockSpec((tm, W_C), lambda i: (i, 0))
    col_spec = pl.BlockSpec((W_C, tm), lambda i: (0, i))
    return pl.pallas_call(
        kern,
        grid=(r // tm,),
        in_specs=[
            pl.BlockSpec((tm, W_C), lambda i: (i, xm_blk)),
            pl.BlockSpec((HALO, W_C), lambda i: (jnp.maximum(i * hpt - 1, 0), xm_blk)),
            pl.BlockSpec((HALO, W_C), lambda i: (jnp.minimum((i + 1) * hpt, last_halo), xm_blk)),
            pl.BlockSpec((CONV_K, W_C), lambda i: (0, 0)),
            pl.BlockSpec((1, W_C), lambda i: (0, 0)),
            pl.BlockSpec((3, H_C, DH_C, DH_C), lambda i: (0, 0, 0, 0)),
            pl.BlockSpec((3 * W_C, LANE), lambda i: (0, 0)),
            pl.BlockSpec((1, LANE), lambda i: (0, 0)),
        ],
        out_specs=[row_spec, col_spec, row_spec, col_spec,
                   pl.BlockSpec((4 * H_C, tm), lambda i: (0, i))],
        out_shape=[act, act_t, act, act_t, jax.ShapeDtypeStruct((4 * H_C, r), F32)],
        compiler_params=_cparams("parallel"),
        name="mlstm_features",
    )(p, p, p, w_conv, b_conv, w_qkv, w_if_pad, b_if_pad)


def _lane_scan(x, op, reverse):
    n = x.shape[1]
    lane = lax.broadcasted_iota(jnp.int32, x.shape, 1)
    sh = 1
    while sh < n:
        if reverse:
            shifted, valid = pltpu.roll(x, n - sh, 1), lane < n - sh
        else:
            shifted, valid = pltpu.roll(x, sh, 1), lane >= sh
        x = jnp.where(valid, op(x, shifted), x)
        sh *= 2
    return x


def _mlstm_gates(gt, m_ref, reverse):
    ln = gt.shape[1]
    nr = 2 * H_C
    ig = gt
    bcum = _lane_scan(jax.nn.log_sigmoid(pltpu.roll(gt, H_C, 0)), jnp.add, reverse)
    r = ig - bcum
    cmx = _lane_scan(r, jnp.maximum, reverse)
    g = bcum[:, 0:1] if reverse else bcum[:, ln - 1:ln]
    a_end = g + r
    m_loc = jnp.max(a_end, axis=1, keepdims=True)
    w_end = jnp.exp(a_end - m_loc)
    m_old = m_ref[:, 0:1]
    m_new = jnp.maximum(g + m_old, m_loc)
    a_old = jnp.exp(g + m_old - m_new)
    a_new = jnp.exp(m_loc - m_new)
    mx = jnp.maximum(m_old, cmx)
    w_inter = jnp.exp(m_old - mx)
    e_neg = jnp.exp(-(bcum + mx))
    r_cols = jnp.concatenate([r, jnp.zeros((LANE - nr, ln), F32)], axis=0).T
    s_id = lax.broadcasted_iota(jnp.int32, (ln, ln), 0)
    t_id = lax.broadcasted_iota(jnp.int32, (ln, ln), 1)
    keep = (s_id >= t_id) if reverse else (s_id <= t_id)
    return dict(mx=mx, w_inter=w_inter, e_neg=e_neg, w_end=w_end, a_old=a_old, a_new=a_new, m_new=m_new,
                r_cols=r_cols, keep=keep)


def _mscan_kernel(gf_ref, qf_ref, kf_ref, vf_ref, gb_ref, qb_ref, kb_ref, vb_ref,
                  hf_ref, hb_ref, cf_ref, mf_ref, cb_ref, mb_ref):
    @pl.when(pl.program_id(1) == 0)
    def _():
        cf_ref[...] = jnp.zeros_like(cf_ref)
        cb_ref[...] = jnp.zeros_like(cb_ref)
        mf_ref[...] = jnp.zeros_like(mf_ref)
        mb_ref[...] = jnp.zeros_like(mb_ref)

    ln = qf_ref.shape[1]
    dirs = ((_mlstm_gates(gf_ref[0:2 * H_C, :], mf_ref, False), qf_ref, kf_ref, vf_ref, cf_ref, hf_ref),
            (_mlstm_gates(gb_ref[2 * H_C:4 * H_C, :], mb_ref, True), qb_ref, kb_ref, vb_ref, cb_ref, hb_ref))
    pairs = [(d, h) for d in range(2) for h in range(H_C)]
    ones_row = jnp.where(lax.broadcasted_iota(jnp.int32, (DH_C, ln), 0) == 0, 1.0, 0.0).astype(MXU_DT)
    head = lambda h: slice(h * DH_C, (h + 1) * DH_C)
    row = lambda h: slice(h, h + 1)

    ks = [dirs[d][2][:, head(h)] for d, h in pairs]
    qts = [dirs[d][1][head(h), :] for d, h in pairs]
    vexts = [jnp.concatenate([dirs[d][3][head(h), :], ones_row], axis=0) for d, h in pairs]
    cexts = [dirs[d][4][h] for d, h in pairs]
    qks = [jnp.dot(k, q_t, preferred_element_type=F32) for k, q_t in zip(ks, qts)]
    rhss = []
    for i, (d, h) in enumerate(pairs):
        g = dirs[d][0]
        d_t = jnp.exp(jnp.where(g["keep"], g["r_cols"][:, row(h)] - g["mx"][row(h), :], NEG_BIG))
        wq = (qts[i].astype(F32) * g["w_inter"][row(h), :]).astype(MXU_DT)
        rhss.append(jnp.concatenate([(qks[i] * d_t).astype(MXU_DT), wq], axis=0))
    ress = [jnp.dot(jnp.concatenate([vexts[i], cexts[i].astype(MXU_DT)], axis=1), rhss[i],
                    preferred_element_type=F32) for i in range(len(pairs))]
    for i, (d, h) in enumerate(pairs):
        g = dirs[d][0]
        den = ress[i][DH_C:DH_C + 1, :]
        hh = ress[i][:DH_C, :] / jnp.maximum(jnp.abs(den), g["e_neg"][row(h), :])
        dirs[d][5][head(h), :] = hh.astype(dirs[d][5].dtype)
    clocs = []
    for i, (d, h) in enumerate(pairs):
        vw_t = (vexts[i].astype(F32) * dirs[d][0]["w_end"][row(h), :]).astype(MXU_DT)
        clocs.append(jnp.dot(vw_t, ks[i], preferred_element_type=F32))
    for i, (d, h) in enumerate(pairs):
        g = dirs[d][0]
        dirs[d][4][h] = g["a_old"][row(h), :] * cexts[i] + g["a_new"][row(h), :] * clocs[i]
    mf_ref[...] = jnp.broadcast_to(dirs[0][0]["m_new"], mf_ref.shape)
    mb_ref[...] = jnp.broadcast_to(dirs[1][0]["m_new"], mb_ref.shape)


def _mlstm_scan(q_t, k, v_t, gates_t, n_batch, seq, ctx_rows):
    ln = MLSTM_CHUNK
    nck = seq // ln
    ncc = ctx_rows // ln

    def fwd_chunk(b, s):
        return b * nck + s

    def bwd_chunk(b, s):
        return b * nck + jnp.where(s < ncc, ncc - 1 - s, nck - 1 + ncc - s)

    def specs(chunk):
        rows = pl.BlockSpec((ln, W_C), lambda b, s: (chunk(b, s), 0))
        cols = pl.BlockSpec((W_C, ln), lambda b, s: (0, chunk(b, s)))
        gate = pl.BlockSpec((4 * H_C, ln), lambda b, s: (0, chunk(b, s)))
        return [gate, cols, rows, cols], cols

    in_f, out_f = specs(fwd_chunk)
    in_b, out_b = specs(bwd_chunk)
    out = jax.ShapeDtypeStruct(q_t.shape, ACT_DT)
    return pl.pallas_call(
        _mscan_kernel,
        grid=(n_batch, nck),
        in_specs=in_f + in_b,
        out_specs=[out_f, out_b],
        out_shape=[out, out],
        scratch_shapes=[
            pltpu.VMEM((H_C, 2 * DH_C, DH_C), F32), pltpu.VMEM((8, LANE), F32),
            pltpu.VMEM((H_C, 2 * DH_C, DH_C), F32), pltpu.VMEM((8, LANE), F32),
        ],
        compiler_params=_cparams("parallel", "arbitrary"),
        name="mlstm_scan",
    )(gates_t, q_t, k, v_t, gates_t, q_t, k, v_t)


def _route(logits, carry):
    lane = lax.broadcasted_iota(jnp.int32, logits.shape, 1).astype(F32)
    big = float(4 * LANE)
    gl = jnp.where(lane < N_GROUPS, logits, NEG_BIG)
    gmax = jnp.max(gl, axis=-1, keepdims=True)
    g_star = jnp.min(jnp.where(gl == gmax, lane, big), axis=-1, keepdims=True)
    p_g = 1.0 / jnp.sum(jnp.exp(gl - gmax), axis=-1, keepdims=True)
    e_lo = g_star * EXP_PER_GROUP + N_GROUPS
    in_group = jnp.logical_and(lane >= e_lo, lane < e_lo + EXP_PER_GROUP)
    el = jnp.where(in_group, logits, NEG_BIG)
    v1 = jnp.max(el, axis=-1, keepdims=True)
    i1 = jnp.min(jnp.where(el == v1, lane, big), axis=-1, keepdims=True)
    el2 = jnp.where(lane == i1, NEG_BIG, el)
    v2 = jnp.max(el2, axis=-1, keepdims=True)
    i2 = jnp.min(jnp.where(el2 == v2, lane, big), axis=-1, keepdims=True)
    e21 = jnp.exp(v2 - v1)
    w1 = p_g / (1.0 + e21)
    w2 = p_g * e21 / (1.0 + e21)
    e1 = i1 - N_GROUPS
    e2 = i2 - N_GROUPS
    oh1 = lane == e1
    oh2 = lane == e2
    both = jnp.where(jnp.logical_or(oh1, oh2), 1.0, 0.0)
    tm = logits.shape[0]
    lower = (lax.broadcasted_iota(jnp.int32, (tm, tm), 1) < lax.broadcasted_iota(jnp.int32, (tm, tm), 0))
    before = jnp.dot(jnp.where(lower, 1.0, 0.0).astype(MXU_DT), both.astype(MXU_DT),
                     preferred_element_type=F32) + carry
    rank1 = jnp.sum(jnp.where(oh1, before, 0.0), axis=-1, keepdims=True)
    rank2 = jnp.sum(jnp.where(oh2, before, 0.0), axis=-1, keepdims=True)
    out = jnp.where(lane == 0, e1, 0.0)
    out = jnp.where(lane == 1, e2, out)
    out = jnp.where(lane == 2, w1, out)
    out = jnp.where(lane == 3, w2, out)
    out = jnp.where(lane == 4, rank1, out)
    out = jnp.where(lane == 5, rank2, out)
    return out, carry + jnp.sum(both, axis=0, keepdims=True)


def _merge_kernel(x_ref, gate_ref, z_ref, ya_ref, yb_ref, hf_ref, hb_ref, xc_ref, gm_ref, sk_ref,
                  wbr_ref, wo_ref, mod_ref, g2_ref, wr_ref, br_ref, xo_ref, rt_ref, cnt_ref, carry_ref):
    @pl.when(pl.program_id(0) == 0)
    def _():
        carry_ref[...] = jnp.zeros_like(carry_ref)

    hs_t = hf_ref[...].astype(F32) + hb_ref[...].astype(F32)
    hn_t = []
    for h in range(H_C):
        part = hs_t[h * DH_C:(h + 1) * DH_C, :]
        hn_t.append(part * lax.rsqrt(jnp.mean(part * part, axis=0, keepdims=True) + EPS))
    hn = jnp.concatenate(hn_t, axis=0).T * gm_ref[...]
    yc = (hn + sk_ref[...] * xc_ref[...].astype(F32)) * _sigmoid(z_ref[...].astype(F32))
    ys = (ya_ref[...], yb_ref[...], yc.astype(MXU_DT))
    merged = None
    for i in range(3):
        gate = _sigmoid(gate_ref[:, i * D_MODEL:(i + 1) * D_MODEL].astype(F32))
        term = gate * jnp.dot(ys[i], wbr_ref[i], preferred_element_type=F32)
        merged = term if merged is None else merged + term
    y = jnp.dot(merged.astype(MXU_DT), wo_ref[...], preferred_element_type=F32)
    xn = x_ref[...] + mod_ref[2:3, :] * y
    xo_ref[...] = xn
    h2 = _rms(xn) * g2_ref[...] * (1.0 + mod_ref[4:5, :]) + mod_ref[3:4, :]
    h_hi, h_lo = _split_bf16(h2)
    logits = (jnp.dot(h_hi, wr_ref[0], preferred_element_type=F32)
              + jnp.dot(h_lo, wr_ref[0], preferred_element_type=F32)
              + jnp.dot(h_hi, wr_ref[1], preferred_element_type=F32)) + br_ref[...]
    route, carry = _route(logits, carry_ref[0:1, :])
    rt_ref[...] = route
    carry_ref[0:1, :] = carry
    cnt_ref[...] = jnp.broadcast_to(carry, cnt_ref.shape)


def _merge(xall, p, ya, yb, hf, hb, xconv, g_m, skip, w_br, w_o, mod, g2, w_route, b_route, n_batch, tps):
    r, d = xall.shape
    tm = ROW_TILE

    def mod_idx(i):
        return (jnp.where(i % tps == 0, n_batch, i // tps), 0, 0)

    row = lambda w: pl.BlockSpec((tm, w), lambda i: (i, 0))
    col = lambda w: pl.BlockSpec((w, tm), lambda i: (0, i))
    full = lambda shape: pl.BlockSpec(shape, lambda i: (0,) * len(shape))
    return pl.pallas_call(
        _merge_kernel,
        grid=(r // tm,),
        in_specs=[
            row(d),
            pl.BlockSpec((tm, GATE_W), lambda i: (i, P_GATE // GATE_W)),
            pl.BlockSpec((tm, W_C), lambda i: (i, P_Z // W_C)),
            row(W_BR), row(W_BR), col(W_C), col(W_C), row(W_C),
            full((1, W_C)), full((1, W_C)),
            full((3, W_BR, d)), full((d, d)),
            pl.BlockSpec((None, N_MOD, d), mod_idx),
            full((1, d)), full((2, d, LANE)), full((1, LANE)),
        ],
        out_specs=[row(d), row(LANE), full((8, LANE))],
        out_shape=[jax.ShapeDtypeStruct((r, d), F32), jax.ShapeDtypeStruct((r, LANE), F32),
                   jax.ShapeDtypeStruct((8, LANE), F32)],
        scratch_shapes=[pltpu.VMEM((8, LANE), F32)],
        compiler_params=_cparams("arbitrary"),
        name="merge_route",
    )(xall, p, p, ya, yb, hf, hb, xconv, g_m, skip, w_br, w_o, mod, g2, w_route, b_route)


def _row_copy(src_hbm, src_row, dst_ref, dst_row, sem):
    return pltpu.make_async_copy(src_hbm.at[pl.ds(src_row, 1), :], dst_ref.at[pl.ds(dst_row, 1), :], sem)


def _dispatch_kernel(fill_ref, nlive_ref, pos_ref, x_ref, g2_ref, mod_ref, xs_hbm, hbuf, zbuf, zsem, sem):
    i = pl.program_id(0)
    n = pl.num_programs(0)
    tm = x_ref.shape[0]
    slot = i % 2
    bm = zbuf.shape[0]
    n_blocks = xs_hbm.shape[0] // bm

    @pl.when(i == 0)
    def _():
        zbuf[...] = jnp.zeros_like(zbuf)

        def fill_block(b):
            return pltpu.make_async_copy(zbuf, xs_hbm.at[pl.ds(b * bm, bm), :], zsem)

        def fill_last(k, carry):
            fill_block(fill_ref[k]).start()
            return carry

        def fill_unused(b, carry):
            fill_block(b).start()
            return carry

        def wait_fill(k, carry):
            fill_block(0).wait()
            return carry

        n_live = nlive_ref[0]
        n_used = nlive_ref[1]
        lax.fori_loop(0, n_used, fill_last, 0)
        lax.fori_loop(n_live, n_blocks, fill_unused, 0)
        lax.fori_loop(0, n_used + n_blocks - n_live, wait_fill, 0)

    hbuf[slot] = _rms(x_ref[...]) * g2_ref[...] * (1.0 + mod_ref[4:5, :]) + mod_ref[3:4, :]

    def body(c, carry):
        for u in range(DMA_UNROLL):
            r = c * DMA_UNROLL + u
            for j in range(TOP_K):
                pltpu.make_async_copy(hbuf.at[slot, pl.ds(r, 1), :],
                                      xs_hbm.at[pl.ds(pos_ref[0, j * tm + r], 1), :], sem.at[slot]).start()
        return carry

    lax.fori_loop(0, tm // DMA_UNROLL, body, 0)

    def wait_tile(s):
        pltpu.make_async_copy(hbuf.at[s], xs_hbm.at[pl.ds(0, tm), :], sem.at[s]).wait()
        pltpu.make_async_copy(hbuf.at[s], xs_hbm.at[pl.ds(0, tm), :], sem.at[s]).wait()

    @pl.when(i > 0)
    def _():
        wait_tile(1 - slot)

    @pl.when(i == n - 1)
    def _():
        wait_tile(slot)


def _moe_dispatch(xall, g2, mod, fill_blocks, n_live, pos_tiles, n_slots, n_batch, tps):
    r, d = xall.shape
    tm = ROW_TILE
    assert TOP_K == 2

    def mod_idx(i, fb, nl):
        return (jnp.where(i % tps == 0, n_batch, i // tps), 0, 0)

    grid_spec = pltpu.PrefetchScalarGridSpec(
        num_scalar_prefetch=2,
        grid=(r // tm,),
        in_specs=[
            pl.BlockSpec((None, 1, TOP_K * tm), lambda i, fb, nl: (i, 0, 0), memory_space=pltpu.SMEM),
            pl.BlockSpec((tm, d), lambda i, fb, nl: (i, 0)),
            pl.BlockSpec((1, d), lambda i, fb, nl: (0, 0)),
            pl.BlockSpec((None, N_MOD, d), mod_idx),
        ],
        out_specs=pl.BlockSpec(memory_space=pl.ANY),
        scratch_shapes=[pltpu.VMEM((2, tm, d), F32), pltpu.VMEM((MOE_BLOCK, d), F32),
                        pltpu.SemaphoreType.DMA(()), pltpu.SemaphoreType.DMA((2,))],
    )
    return pl.pallas_call(
        _dispatch_kernel,
        grid_spec=grid_spec,
        out_shape=jax.ShapeDtypeStruct((n_slots, d), F32),
        compiler_params=_cparams("arbitrary"),
        name="moe_dispatch",
    )(fill_blocks, n_live, pos_tiles, xall, g2, mod)


def _moe_kernel(be_ref, nv_ref, x_ref, wg_ref, wu_ref, wd_ref, y_ref):
    n_valid = nv_ref[pl.program_id(0)]

    @pl.when(n_valid > 0)
    def _():
        x = x_ref[...].astype(MXU_DT)
        a = jnp.dot(x, wg_ref[...], preferred_element_type=F32)
        u = jnp.dot(x, wu_ref[...], preferred_element_type=F32)
        act = (a * _sigmoid(a) * u).astype(MXU_DT)
        y_ref[...] = jnp.dot(act, wd_ref[...], preferred_element_type=F32)

    @pl.when(n_valid <= 0)
    def _():
        y_ref[...] = jnp.zeros_like(y_ref)


def _moe_experts(xs, block_e, n_valid, w_gate, w_up, w_down, layer):
    n_slots, d = xs.shape
    bm = MOE_BLOCK
    grid_spec = pltpu.PrefetchScalarGridSpec(
        num_scalar_prefetch=2,
        grid=(n_slots // bm,),
        in_specs=[
            pl.BlockSpec((bm, d), lambda i, be, nv: (i, 0)),
            pl.BlockSpec((None, None, d, D_EXPERT), lambda i, be, nv: (layer, be[i], 0, 0)),
            pl.BlockSpec((None, None, d, D_EXPERT), lambda i, be, nv: (layer, be[i], 0, 0)),
            pl.BlockSpec((None, None, D_EXPERT, d), lambda i, be, nv: (layer, be[i], 0, 0)),
        ],
        out_specs=pl.BlockSpec((bm, d), lambda i, be, nv: (i, 0)),
    )
    return pl.pallas_call(
        _moe_kernel,
        grid_spec=grid_spec,
        out_shape=jax.ShapeDtypeStruct((n_slots, d), F32),
        compiler_params=_cparams("arbitrary"),
        name="moe_experts",
    )(block_e, n_valid, xs, w_gate, w_up, w_down)


def _combine_kernel(pos_ref, pos_next_ref, x_ref, rt_ref, mod_ref, gf_ref, yb_hbm, xo_ref, ybuf, sem, *, final):
    i = pl.program_id(0)
    n = pl.num_programs(0)
    tm = x_ref.shape[0]

    def start_gather(ids_ref, slot):
        def body(c, carry):
            for u in range(DMA_UNROLL):
                r = c * DMA_UNROLL + u
                _row_copy(yb_hbm, ids_ref[0, r], ybuf.at[slot], r, sem.at[slot]).start()
            return carry
        lax.fori_loop(0, TOP_K * tm // DMA_UNROLL, body, 0)

    @pl.when(i == 0)
    def _():
        start_gather(pos_ref, 0)

    @pl.when(i + 1 < n)
    def _():
        start_gather(pos_next_ref, (i + 1) % 2)

    slot = i % 2
    pltpu.make_async_copy(yb_hbm.at[pl.ds(0, TOP_K * tm), :], ybuf.at[slot], sem.at[slot]).wait()
    rt = rt_ref[...]
    f = rt[:, 2:3] * ybuf[slot, 0:tm, :] + rt[:, 3:4] * ybuf[slot, tm:2 * tm, :]
    xn = x_ref[...] + mod_ref[5:6, :] * f
    xo_ref[...] = _rms(xn) * gf_ref[...] if final else xn


def _combine(xall, route, mod, g_final, yb, pos, n_batch, tps, final):
    r, d = xall.shape
    tm = ROW_TILE
    nt = r // tm
    last = nt - 1

    def mod_idx(i):
        return (jnp.where(i % tps == 0, n_batch, i // tps), 0, 0)

    row = lambda w: pl.BlockSpec((tm, w), lambda i: (i, 0))
    return pl.pallas_call(
        functools.partial(_combine_kernel, final=final),
        grid=(nt,),
        in_specs=[
            pl.BlockSpec((None, 1, TOP_K * tm), lambda i: (i, 0, 0), memory_space=pltpu.SMEM),
            pl.BlockSpec((None, 1, TOP_K * tm), lambda i: (jnp.minimum(i + 1, last), 0, 0),
                         memory_space=pltpu.SMEM),
            row(d), row(LANE),
            pl.BlockSpec((None, N_MOD, d), mod_idx),
            pl.BlockSpec((1, d), lambda i: (0, 0)),
            pl.BlockSpec(memory_space=pl.ANY),
        ],
        out_specs=row(d),
        out_shape=jax.ShapeDtypeStruct((r, d), F32),
        scratch_shapes=[pltpu.VMEM((2, TOP_K * tm, d), F32), pltpu.SemaphoreType.DMA((2,))],
        compiler_params=_cparams("arbitrary"),
        name="moe_combine",
    )(pos, pos, xall, route, mod, g_final, yb)


def _dispatch_plan(route, counts_f, n_tiles):
    n = route.shape[0]
    bm = MOE_BLOCK
    n_blocks = -(-(n * TOP_K + N_EXPERTS * (bm - 1)) // bm)
    counts = counts_f[:N_EXPERTS].astype(jnp.int32)
    padded = (counts + bm - 1) // bm * bm
    pend = jnp.cumsum(padded)
    seg_off = pend - padded
    eid = route[:, 0:TOP_K].astype(jnp.int32)
    rank = route[:, 4:4 + TOP_K].astype(jnp.int32)
    experts = jnp.arange(N_EXPERTS, dtype=jnp.int32)
    pos = jnp.sum(jnp.where(eid[..., None] == experts, seg_off, 0), axis=-1) + rank
    blk0 = jnp.arange(n_blocks, dtype=jnp.int32) * bm
    block_e = jnp.minimum(jnp.sum((blk0[:, None] >= pend[None, :]).astype(jnp.int32), axis=1), N_EXPERTS - 1)
    seg_end = seg_off + counts
    n_valid = jnp.clip(seg_end[block_e] - blk0, 0, bm).astype(jnp.int32)
    pos_tiles = pos.reshape(n_tiles, n // n_tiles, TOP_K).transpose(0, 2, 1).reshape(n_tiles, 1, -1)
    used = counts > 0
    last_blocks = jnp.sort(jnp.where(used, pend // bm - 1, n_blocks)).astype(jnp.int32)
    n_live = jnp.stack([pend[-1] // bm, jnp.sum(used)]).astype(jnp.int32)
    return block_e.astype(jnp.int32), n_valid, last_blocks, n_live, pos_tiles, n_blocks * bm


def _rope_tables(seq, ctx_rows):
    t = jnp.arange(seq - ctx_rows)
    inv = ROPE_BASE ** (-jnp.arange(0, ROT_AX, 2, dtype=F32) / ROT_AX)
    ang_r = (t // GRID_W).astype(F32)[:, None] * inv
    ang_c = (t % GRID_W).astype(F32)[:, None] * inv
    cos64 = jnp.concatenate([jnp.cos(ang_r)] * 2 + [jnp.cos(ang_c)] * 2, axis=1)
    sin64 = jnp.concatenate([-jnp.sin(ang_r), jnp.sin(ang_r), -jnp.sin(ang_c), jnp.sin(ang_c)], axis=1)
    cos = jnp.concatenate([jnp.ones((ctx_rows, DK), F32), cos64], axis=0)
    sin = jnp.concatenate([jnp.zeros((ctx_rows, DK), F32), sin64], axis=0)
    return jnp.tile(cos, (1, LANE // DK)), jnp.tile(sin, (1, LANE // DK))


def kernel(x, c, ctx, c_ctx, w_mod, b_mod, g_norm1, g_norm2, w_in, diff_lambda, g_diff_subln, w_gmlp_s, b_gmlp_s, g_gmlp_v, w_conv_m, b_conv_m, w_qkv_m, w_if_m, b_if_m, g_mlstm_norm, skip_m, w_branch, w_out, w_route_g, b_route_g, w_route_e, b_route_e, w_e_gate, w_e_up, w_e_down, g_final):
    n_batch, t_lat, d = x.shape
    ctx_rows = ctx.shape[1]
    depth = w_in.shape[0]
    seq = ctx_rows + t_lat
    assert d == D_MODEL and ctx_rows == ROW_TILE and t_lat % ROW_TILE == 0 and t_lat % GRID_W == 0
    tps = seq // ROW_TILE
    n_tiles = n_batch * tps

    xall = jnp.concatenate([ctx, x], axis=1).reshape(n_batch * seq, d)
    mb = -(-(n_batch + 1) // 8) * 8
    c_all = jnp.zeros((mb, d), F32).at[:n_batch].set(c).at[n_batch].set(c_ctx)
    mod_all = _modulation(c_all, w_mod, b_mod)[:, :n_batch + 1].reshape(depth, n_batch + 1, N_MOD, d)
    cos, sin_signed = _rope_tables(seq, ctx_rows)
    w_in_b = w_in.astype(MXU_DT)
    w_eg_b, w_eu_b, w_ed_b = w_e_gate.astype(MXU_DT), w_e_up.astype(MXU_DT), w_e_down.astype(MXU_DT)

    for l in range(depth):
        lam_init = 0.8 - 0.6 * math.exp(-0.3 * l)
        mod = mod_all[l]
        p = _inproj(xall, g_norm1[l][None], mod, cos, sin_signed, w_in_b, l, n_batch, tps)

        ya = _attention(p, diff_lambda[l], g_diff_subln[l][None], lam_init, n_batch, tps, ctx_rows)

        b_full = jnp.repeat(b_gmlp_s[l].T, W_B // G_B, axis=1)
        yb = _gmlp(p, g_gmlp_v[l][None], w_gmlp_s[l].astype(MXU_DT), b_full)

        w_if_pad = jnp.zeros((3 * W_C, LANE), F32).at[:, :4 * H_C].set(w_if_m[l]).astype(MXU_DT)
        b_if_pad = jnp.zeros((1, LANE), F32).at[0, :4 * H_C].set(b_if_m[l])
        xconv, q_m, k_m, v_m, gates = _mlstm_features(
            p, w_conv_m[l], b_conv_m[l][None], w_qkv_m[l].astype(MXU_DT), w_if_pad, b_if_pad, tps)
        hf, hb = _mlstm_scan(q_m, k_m, v_m, gates, n_batch, seq, ctx_rows)

        w_route = (jnp.zeros((d, LANE), F32).at[:, :N_GROUPS].set(w_route_g[l])
                   .at[:, N_GROUPS:N_GROUPS + N_EXPERTS].set(w_route_e[l]))
        w_route = jnp.stack(_split_bf16(w_route))
        b_route = (jnp.zeros((1, LANE), F32).at[0, :N_GROUPS].set(b_route_g[l])
                   .at[0, N_GROUPS:N_GROUPS + N_EXPERTS].set(b_route_e[l]))
        xall, route, counts = _merge(xall, p, ya, yb, hf, hb, xconv, g_mlstm_norm[l][None], skip_m[l][None],
                                         w_branch[l].astype(MXU_DT), w_out[l].astype(MXU_DT), mod,
                                         g_norm2[l][None], w_route, b_route, n_batch, tps)

        block_e, n_valid, fill_blocks, n_live, pos_tiles, n_slots = _dispatch_plan(route, counts[0], n_tiles)
        xs = _moe_dispatch(xall, g_norm2[l][None], mod, fill_blocks, n_live, pos_tiles, n_slots, n_batch, tps)
        y_sorted = _moe_experts(xs, block_e, n_valid, w_eg_b, w_eu_b, w_ed_b, l)
        xall = _combine(xall, route, mod, g_final[None], y_sorted, pos_tiles, n_batch, tps, l == depth - 1)

    return xall.reshape(n_batch, seq, d)[:, ctx_rows:]
```

```python
import functools
import math

import jax
import jax.numpy as jnp
from jax import lax
from jax.experimental import pallas as pl
from jax.experimental.pallas import tpu as pltpu

F32 = jnp.float32
MXU_DT = jnp.bfloat16
ACT_DT = jnp.bfloat16
HIGHEST = lax.Precision.HIGHEST

D_MODEL = 1024
N_MOD = 6
EPS = 1e-6
GRID_W = 64
H_A, DK = 4, 64
DV = 2 * DK
ROT_AX = DK // 2
ROPE_BASE = 10000.0
W_B, G_B, CHUNK_B = 512, 4, 128
H_C, DH_C, CONV_K, MLSTM_CHUNK = 4, 128, 3, 128
W_C = H_C * DH_C
N_GROUPS, EXP_PER_GROUP, TOP_K, D_EXPERT = 4, 8, 2, 512
N_EXPERTS = N_GROUPS * EXP_PER_GROUP
W_BR = 512

GATE_W = 3 * D_MODEL
P_GATE = 0
P_K = GATE_W
P_V = P_K + 512
P_XM = P_V + 512
P_Q = P_XM + 512
P_UV = P_Q + 512
P_Z = P_UV + 2 * W_B
IN_COLS = P_Z + W_C
ORIG_GATE0 = IN_COLS - GATE_W

LANE = 128
ROW_TILE = 256
PROJ_CHUNK = 512
MOE_BLOCK = 256
HALO = 16
DMA_UNROLL = 8
ATTN_KV_CHUNK = 256
ATTN_Q_ROWS = 256
LOG2_E = math.log2(math.e)
VMEM_LIMIT = 56 * 1024 * 1024
NEG_BIG = -1e30


def _cparams(*sem):
    return pltpu.CompilerParams(dimension_semantics=sem, vmem_limit_bytes=VMEM_LIMIT)


def _rms(x):
    return x * lax.rsqrt(jnp.mean(x * x, axis=-1, keepdims=True) + EPS)


def _sigmoid(x):
    return 0.5 * jnp.tanh(0.5 * x) + 0.5


def _split_bf16(x):
    hi = x.astype(jnp.bfloat16)
    return hi, (x - hi.astype(F32)).astype(jnp.bfloat16)


def _mod_kernel(c_ref, w_ref, b_ref, o_ref):
    c = c_ref[...]
    s = c * jax.nn.sigmoid(c)
    o_ref[...] = jnp.dot(s, w_ref[...], preferred_element_type=F32, precision=HIGHEST) + b_ref[...]


def _modulation(c_all, w_mod, b_mod):
    n_layer = w_mod.shape[0]
    mb, d = c_all.shape
    tn = 1024
    return pl.pallas_call(
        _mod_kernel,
        grid=(n_layer, N_MOD * d // tn),
        in_specs=[
            pl.BlockSpec((mb, d), lambda l, j: (0, 0)),
            pl.BlockSpec((None, d, tn), lambda l, j: (l, 0, j)),
            pl.BlockSpec((None, 1, tn), lambda l, j: (l, 0, j)),
        ],
        out_specs=pl.BlockSpec((None, mb, tn), lambda l, j: (l, 0, j)),
        out_shape=jax.ShapeDtypeStruct((n_layer, mb, N_MOD * d), F32),
        compiler_params=_cparams("parallel", "parallel"),
        name="modulation",
    )(c_all, w_mod, b_mod.reshape(n_layer, 1, N_MOD * d))


def _rope(acc, cos, sin_signed):
    w = acc.shape[1]
    lane = lax.broadcasted_iota(jnp.int32, acc.shape, 1)
    partner = jnp.where((lane & 16) == 0, pltpu.roll(acc, w - 16, 1), pltpu.roll(acc, 16, 1))
    reps = w // cos.shape[1]
    return acc * jnp.tile(cos, (1, reps)) + partner * jnp.tile(sin_signed, (1, reps))


def _inproj_kernel(x_ref, g_ref, mod_ref, cos_ref, sin_ref, w_ref, o_ref):
    x = x_ref[...]
    h = (_rms(x) * g_ref[...] * (1.0 + mod_ref[1:2, :]) + mod_ref[0:1, :]).astype(MXU_DT)
    k_chunk = P_K // PROJ_CHUNK
    q_chunk = P_Q // PROJ_CHUNK
    for c in range(IN_COLS // PROJ_CHUNK):
        cols = slice(c * PROJ_CHUNK, (c + 1) * PROJ_CHUNK)
        w0 = (ORIG_GATE0 + c * PROJ_CHUNK) % IN_COLS
        acc = jnp.dot(h, w_ref[:, w0:w0 + PROJ_CHUNK], preferred_element_type=F32)
        if c == k_chunk:
            acc = _rope(acc, cos_ref[...], sin_ref[...])
        elif c == q_chunk:
            acc = _rope(acc, cos_ref[...], sin_ref[...]) * (DK ** -0.5 * LOG2_E)
        o_ref[:, cols] = acc.astype(o_ref.dtype)


def _inproj(xall, g1, mod, cos, sin_signed, w_in, layer, n_batch, tps):
    r, d = xall.shape
    tm = ROW_TILE
    assert ORIG_GATE0 % PROJ_CHUNK == 0

    def mod_idx(i):
        return (jnp.where(i % tps == 0, n_batch, i // tps), 0, 0)

    return pl.pallas_call(
        _inproj_kernel,
        grid=(r // tm,),
        in_specs=[
            pl.BlockSpec((tm, d), lambda i: (i, 0)),
            pl.BlockSpec((1, d), lambda i: (0, 0)),
            pl.BlockSpec((None, N_MOD, d), mod_idx),
            pl.BlockSpec((tm, LANE), lambda i: (i % tps, 0)),
            pl.BlockSpec((tm, LANE), lambda i: (i % tps, 0)),
            pl.BlockSpec((None, d, IN_COLS), lambda i: (layer, 0, 0)),
        ],
        out_specs=pl.BlockSpec((tm, IN_COLS), lambda i: (i, 0)),
        out_shape=jax.ShapeDtypeStruct((r, IN_COLS), ACT_DT),
        compiler_params=_cparams("parallel"),
        name="inproj",
    )(xall, g1, mod, cos, sin_signed, w_in)


def _attn_kernel(dl_ref, g_ref, q_ref, k_ref, v_ref, o_ref, kt_ref, vext_ref, s_ref, *, lam_init, ctx_rows):
    @pl.when(pl.program_id(2) == 0)
    def _():
        kt_ref[...] = k_ref[...].astype(F32).T.astype(kt_ref.dtype)
        one_hot = lax.broadcasted_iota(jnp.int32, v_ref.shape, 1) == 0
        vext_ref[:, :DV] = v_ref[...]
        vext_ref[:, DV:] = jnp.where(one_hot, 1.0, 0.0).astype(vext_ref.dtype)

    dl = dl_ref[...]
    lam = (jnp.exp(jnp.sum(dl[0:1] * dl[1:2], keepdims=True))
           - jnp.exp(jnp.sum(dl[2:3] * dl[3:4], keepdims=True)) + lam_init)
    kc = ATTN_KV_CHUNK

    def attend(kv_rows):
        for r0 in range(0, q_ref.shape[0], ATTN_Q_ROWS):
            attend_rows(kv_rows, slice(r0, r0 + ATTN_Q_ROWS))

    def attend_rows(kv_rows, rows):
        q = q_ref[rows, :]
        lane = lax.broadcasted_iota(jnp.int32, q.shape, 1)
        zero = jnp.zeros_like(q)
        qms = (jnp.where(lane < DK, q, zero), jnp.where(lane >= DK, q, zero))
        chunks = [slice(c * kc, (c + 1) * kc) for c in range(kv_rows // kc)]

        def scores(m, cols):
            return jnp.dot(qms[m], kt_ref[:, cols], preferred_element_type=F32)

        mxs = []
        for m in range(2):
            mrun = None
            for cols in chunks:
                s = scores(m, cols)
                s_ref[m, rows, cols] = s
                for g in range(kc // LANE):
                    blk = s[:, g * LANE:(g + 1) * LANE]
                    mrun = blk if mrun is None else jnp.maximum(mrun, blk)
            mxs.append(jnp.max(mrun, axis=-1, keepdims=True))
        accs = [jnp.zeros((q.shape[0], 2 * DV), F32)] * 2
        for cols in chunks:
            for m in range(2):
                p = jnp.exp2(s_ref[m, rows, cols] - mxs[m])
                accs[m] = accs[m] + jnp.dot(p.astype(MXU_DT), vext_ref[cols, :], preferred_element_type=F32)
        o = (accs[0][:, :DV] * (1.0 / accs[0][:, DV:DV + 1])
             - accs[1][:, :DV] * (lam / accs[1][:, DV:DV + 1]))
        o_ref[rows, :] = (_rms(o) * g_ref[...] * (1.0 - lam_init)).astype(o_ref.dtype)

    is_ctx = pl.program_id(2) == 0

    @pl.when(is_ctx)
    def _():
        attend(ctx_rows)

    @pl.when(jnp.logical_not(is_ctx))
    def _():
        attend(k_ref.shape[0])


def _attention(p, diff_lambda, g_sub, lam_init, n_batch, tps, ctx_rows):
    tq = ROW_TILE
    seq = tps * tq
    kern = functools.partial(_attn_kernel, lam_init=lam_init, ctx_rows=ctx_rows)
    return pl.pallas_call(
        kern,
        grid=(n_batch, H_A, tps),
        in_specs=[
            pl.BlockSpec((4, DK), lambda b, h, i: (0, 0)),
            pl.BlockSpec((1, DV), lambda b, h, i: (0, 0)),
            pl.BlockSpec((tq, LANE), lambda b, h, i: (b * tps + i, P_Q // LANE + h)),
            pl.BlockSpec((seq, LANE), lambda b, h, i: (b, P_K // LANE + h)),
            pl.BlockSpec((seq, LANE), lambda b, h, i: (b, P_V // LANE + h)),
        ],
        out_specs=pl.BlockSpec((tq, LANE), lambda b, h, i: (b * tps + i, h)),
        out_shape=jax.ShapeDtypeStruct((p.shape[0], H_A * DV), ACT_DT),
        scratch_shapes=[pltpu.VMEM((LANE, seq), MXU_DT), pltpu.VMEM((seq, 2 * DV), MXU_DT),
                        pltpu.VMEM((2, tq, seq), F32)],
        compiler_params=_cparams("parallel", "parallel", "arbitrary"),
        name="diff_attention",
    )(diff_lambda, g_sub, p, p, p)


def _gmlp_kernel(uv_ref, gv_ref, ws_ref, bs_ref, o_ref):
    uv = jax.nn.gelu(uv_ref[...].astype(F32))
    u = uv[:, :W_B]
    v = (_rms(uv[:, W_B:]) * gv_ref[...]).astype(MXU_DT)
    gw = W_B // G_B
    for c in range(uv.shape[0] // CHUNK_B):
        rows = slice(c * CHUNK_B, (c + 1) * CHUNK_B)
        for g in range(G_B):
            cols = slice(g * gw, (g + 1) * gw)
            mixed = jnp.dot(ws_ref[g], v[rows, cols], preferred_element_type=F32) + bs_ref[:, cols]
            o_ref[rows, cols] = (u[rows, cols] * mixed).astype(o_ref.dtype)


def _gmlp(p, g_v, w_s, b_full):
    r = p.shape[0]
    tm = ROW_TILE
    return pl.pallas_call(
        _gmlp_kernel,
        grid=(r // tm,),
        in_specs=[
            pl.BlockSpec((tm, 2 * W_B), lambda i: (i, P_UV // (2 * W_B))),
            pl.BlockSpec((1, W_B), lambda i: (0, 0)),
            pl.BlockSpec((G_B, CHUNK_B, CHUNK_B), lambda i: (0, 0, 0)),
            pl.BlockSpec((CHUNK_B, W_B), lambda i: (0, 0)),
        ],
        out_specs=pl.BlockSpec((tm, W_B), lambda i: (i, 0)),
        out_shape=jax.ShapeDtypeStruct((r, W_B), ACT_DT),
        compiler_params=_cparams("parallel"),
        name="gmlp",
    )(p, g_v, w_s, b_full)


def _mfeat_kernel(x_ref, prev_ref, next_ref, wc_ref, bc_ref, wqkv_ref, wif_ref, bif_ref,
                  xc_ref, q_ref, k_ref, v_ref, g_ref, *, tps):
    tm = x_ref.shape[0]
    j = pl.program_id(0) % tps
    seg_start = jnp.logical_or(j == 0, j == 1)
    seg_end = jnp.logical_or(j == 0, j == tps - 1)
    x = x_ref[...].astype(F32)
    prow = jnp.where(seg_start, 0.0, prev_ref[...].astype(F32)[HALO - 1:HALO, :])
    nrow = jnp.where(seg_end, 0.0, next_ref[...].astype(F32)[0:1, :])
    rid = lax.broadcasted_iota(jnp.int32, x.shape, 0)
    xp = jnp.where(rid == 0, prow, pltpu.roll(x, 1, 0))
    xn = jnp.where(rid == tm - 1, nrow, pltpu.roll(x, tm - 1, 0))
    xc = wc_ref[0:1, :] * xp + wc_ref[1:2, :] * x + wc_ref[2:3, :] * xn + bc_ref[...]
    xc = xc * _sigmoid(xc)
    xc_ref[...] = xc.astype(xc_ref.dtype)
    xcb = xc.astype(MXU_DT)
    xb = x_ref[...]
    qs, ks, vs = [], [], []
    for h in range(H_C):
        cols = slice(h * DH_C, (h + 1) * DH_C)
        qs.append(jnp.dot(xcb[:, cols], wqkv_ref[0, h], preferred_element_type=F32))
        ks.append(jnp.dot(xcb[:, cols], wqkv_ref[1, h], preferred_element_type=F32))
        vs.append(jnp.dot(xb[:, cols], wqkv_ref[2, h], preferred_element_type=F32))
    q = jnp.concatenate(qs, axis=1)
    k = jnp.concatenate(ks, axis=1)
    v = jnp.concatenate(vs, axis=1)
    qkv = jnp.concatenate([q, k, v], axis=1).astype(MXU_DT)
    gates = jnp.dot(qkv, wif_ref[...], preferred_element_type=F32) + bif_ref[...]
    g_ref[...] = gates.T[0:g_ref.shape[0], :]
    q_ref[...] = q.T.astype(q_ref.dtype)
    k_ref[...] = (k * (DH_C ** -0.5)).astype(k_ref.dtype)
    v_ref[...] = v.T.astype(v_ref.dtype)


def _mlstm_features(p, w_conv, b_conv, w_qkv, w_if_pad, b_if_pad, tps):
    r = p.shape[0]
    tm = ROW_TILE
    hpt = tm // HALO
    xm_blk = P_XM // W_C
    last_halo = r // HALO - 1
    kern = functools.partial(_mfeat_kernel, tps=tps)
    act = jax.ShapeDtypeStruct((r, W_C), ACT_DT)
    act_t = jax.ShapeDtypeStruct((W_C, r), ACT_DT)
    row_spec = pl.BlockSpec((tm, W_C), lambda i: (i, 0))
    col_spec = pl.BlockSpec((W_C, tm), lambda i: (0, i))
    return pl.pallas_call(
        kern,
        grid=(r // tm,),
        in_specs=[
            pl.BlockSpec((tm, W_C), lambda i: (i, xm_blk)),
            pl.BlockSpec((HALO, W_C), lambda i: (jnp.maximum(i * hpt - 1, 0), xm_blk)),
            pl.BlockSpec((HALO, W_C), lambda i: (jnp.minimum((i + 1) * hpt, last_halo), xm_blk)),
            pl.BlockSpec((CONV_K, W_C), lambda i: (0, 0)),
            pl.BlockSpec((1, W_C), lambda i: (0, 0)),
            pl.BlockSpec((3, H_C, DH_C, DH_C), lambda i: (0, 0, 0, 0)),
            pl.BlockSpec((3 * W_C, LANE), lambda i: (0, 0)),
            pl.BlockSpec((1, LANE), lambda i: (0, 0)),
        ],
        out_specs=[row_spec, col_spec, row_spec, col_spec,
                   pl.BlockSpec((4 * H_C, tm), lambda i: (0, i))],
        out_shape=[act, act_t, act, act_t, jax.ShapeDtypeStruct((4 * H_C, r), F32)],
        compiler_params=_cparams("parallel"),
        name="mlstm_features",
    )(p, p, p, w_conv, b_conv, w_qkv, w_if_pad, b_if_pad)


def _lane_scan(x, op, reverse):
    n = x.shape[1]
    lane = lax.broadcasted_iota(jnp.int32, x.shape, 1)
    sh = 1
    while sh < n:
        if reverse:
            shifted, valid = pltpu.roll(x, n - sh, 1), lane < n - sh
        else:
            shifted, valid = pltpu.roll(x, sh, 1), lane >= sh
        x = jnp.where(valid, op(x, shifted), x)
        sh *= 2
    return x


def _mlstm_gates(gt, m_ref, reverse):
    ln = gt.shape[1]
    nr = 2 * H_C
    ig = gt
    bcum = _lane_scan(jax.nn.log_sigmoid(pltpu.roll(gt, H_C, 0)), jnp.add, reverse)
    r = ig - bcum
    cmx = _lane_scan(r, jnp.maximum, reverse)
    g = bcum[:, 0:1] if reverse else bcum[:, ln - 1:ln]
    a_end = g + r
    m_loc = jnp.max(a_end, axis=1, keepdims=True)
    w_end = jnp.exp(a_end - m_loc)
    m_old = m_ref[:, 0:1]
    m_new = jnp.maximum(g + m_old, m_loc)
    a_old = jnp.exp(g + m_old - m_new)
    a_new = jnp.exp(m_loc - m_new)
    mx = jnp.maximum(m_old, cmx)
    w_inter = jnp.exp(m_old - mx)
    e_neg = jnp.exp(-(bcum + mx))
    r_cols = jnp.concatenate([r, jnp.zeros((LANE - nr, ln), F32)], axis=0).T
    s_id = lax.broadcasted_iota(jnp.int32, (ln, ln), 0)
    t_id = lax.broadcasted_iota(jnp.int32, (ln, ln), 1)
    keep = (s_id >= t_id) if reverse else (s_id <= t_id)
    return dict(mx=mx, w_inter=w_inter, e_neg=e_neg, w_end=w_end, a_old=a_old, a_new=a_new, m_new=m_new,
                r_cols=r_cols, keep=keep)


def _mscan_kernel(gf_ref, qf_ref, kf_ref, vf_ref, gb_ref, qb_ref, kb_ref, vb_ref,
                  hf_ref, hb_ref, cf_ref, mf_ref, cb_ref, mb_ref):
    @pl.when(pl.program_id(1) == 0)
    def _():
        cf_ref[...] = jnp.zeros_like(cf_ref)
        cb_ref[...] = jnp.zeros_like(cb_ref)
        mf_ref[...] = jnp.zeros_like(mf_ref)
        mb_ref[...] = jnp.zeros_like(mb_ref)

    ln = qf_ref.shape[1]
    dirs = ((_mlstm_gates(gf_ref[0:2 * H_C, :], mf_ref, False), qf_ref, kf_ref, vf_ref, cf_ref, hf_ref),
            (_mlstm_gates(gb_ref[2 * H_C:4 * H_C, :], mb_ref, True), qb_ref, kb_ref, vb_ref, cb_ref, hb_ref))
    pairs = [(d, h) for d in range(2) for h in range(H_C)]
    ones_row = jnp.where(lax.broadcasted_iota(jnp.int32, (DH_C, ln), 0) == 0, 1.0, 0.0).astype(MXU_DT)
    head = lambda h: slice(h * DH_C, (h + 1) * DH_C)
    row = lambda h: slice(h, h + 1)

    ks = [dirs[d][2][:, head(h)] for d, h in pairs]
    qts = [dirs[d][1][head(h), :] for d, h in pairs]
    vexts = [jnp.concatenate([dirs[d][3][head(h), :], ones_row], axis=0) for d, h in pairs]
    cexts = [dirs[d][4][h] for d, h in pairs]
    qks = [jnp.dot(k, q_t, preferred_element_type=F32) for k, q_t in zip(ks, qts)]
    rhss = []
    for i, (d, h) in enumerate(pairs):
        g = dirs[d][0]
        d_t = jnp.exp(jnp.where(g["keep"], g["r_cols"][:, row(h)] - g["mx"][row(h), :], NEG_BIG))
        wq = (qts[i].astype(F32) * g["w_inter"][row(h), :]).astype(MXU_DT)
        rhss.append(jnp.concatenate([(qks[i] * d_t).astype(MXU_DT), wq], axis=0))
    ress = [jnp.dot(jnp.concatenate([vexts[i], cexts[i].astype(MXU_DT)], axis=1), rhss[i],
                    preferred_element_type=F32) for i in range(len(pairs))]
    for i, (d, h) in enumerate(pairs):
        g = dirs[d][0]
        den = ress[i][DH_C:DH_C + 1, :]
        hh = ress[i][:DH_C, :] / jnp.maximum(jnp.abs(den), g["e_neg"][row(h), :])
        dirs[d][5][head(h), :] = hh.astype(dirs[d][5].dtype)
    clocs = []
    for i, (d, h) in enumerate(pairs):
        vw_t = (vexts[i].astype(F32) * dirs[d][0]["w_end"][row(h), :]).astype(MXU_DT)
        clocs.append(jnp.dot(vw_t, ks[i], preferred_element_type=F32))
    for i, (d, h) in enumerate(pairs):
        g = dirs[d][0]
        dirs[d][4][h] = g["a_old"][row(h), :] * cexts[i] + g["a_new"][row(h), :] * clocs[i]
    mf_ref[...] = jnp.broadcast_to(dirs[0][0]["m_new"], mf_ref.shape)
    mb_ref[...] = jnp.broadcast_to(dirs[1][0]["m_new"], mb_ref.shape)


def _mlstm_scan(q_t, k, v_t, gates_t, n_batch, seq, ctx_rows):
    ln = MLSTM_CHUNK
    nck = seq // ln
    ncc = ctx_rows // ln

    def fwd_chunk(b, s):
        return b * nck + s

    def bwd_chunk(b, s):
        return b * nck + jnp.where(s < ncc, ncc - 1 - s, nck - 1 + ncc - s)

    def specs(chunk):
        rows = pl.BlockSpec((ln, W_C), lambda b, s: (chunk(b, s), 0))
        cols = pl.BlockSpec((W_C, ln), lambda b, s: (0, chunk(b, s)))
        gate = pl.BlockSpec((4 * H_C, ln), lambda b, s: (0, chunk(b, s)))
        return [gate, cols, rows, cols], cols

    in_f, out_f = specs(fwd_chunk)
    in_b, out_b = specs(bwd_chunk)
    out = jax.ShapeDtypeStruct(q_t.shape, ACT_DT)
    return pl.pallas_call(
        _mscan_kernel,
        grid=(n_batch, nck),
        in_specs=in_f + in_b,
        out_specs=[out_f, out_b],
        out_shape=[out, out],
        scratch_shapes=[
            pltpu.VMEM((H_C, 2 * DH_C, DH_C), F32), pltpu.VMEM((8, LANE), F32),
            pltpu.VMEM((H_C, 2 * DH_C, DH_C), F32), pltpu.VMEM((8, LANE), F32),
        ],
        compiler_params=_cparams("parallel", "arbitrary"),
        name="mlstm_scan",
    )(gates_t, q_t, k, v_t, gates_t, q_t, k, v_t)


def _route(logits, carry):
    lane = lax.broadcasted_iota(jnp.int32, logits.shape, 1).astype(F32)
    big = float(4 * LANE)
    gl = jnp.where(lane < N_GROUPS, logits, NEG_BIG)
    gmax = jnp.max(gl, axis=-1, keepdims=True)
    g_star = jnp.min(jnp.where(gl == gmax, lane, big), axis=-1, keepdims=True)
    p_g = 1.0 / jnp.sum(jnp.exp(gl - gmax), axis=-1, keepdims=True)
    e_lo = g_star * EXP_PER_GROUP + N_GROUPS
    in_group = jnp.logical_and(lane >= e_lo, lane < e_lo + EXP_PER_GROUP)
    el = jnp.where(in_group, logits, NEG_BIG)
    v1 = jnp.max(el, axis=-1, keepdims=True)
    i1 = jnp.min(jnp.where(el == v1, lane, big), axis=-1, keepdims=True)
    el2 = jnp.where(lane == i1, NEG_BIG, el)
    v2 = jnp.max(el2, axis=-1, keepdims=True)
    i2 = jnp.min(jnp.where(el2 == v2, lane, big), axis=-1, keepdims=True)
    e21 = jnp.exp(v2 - v1)
    w1 = p_g / (1.0 + e21)
    w2 = p_g * e21 / (1.0 + e21)
    e1 = i1 - N_GROUPS
    e2 = i2 - N_GROUPS
    oh1 = lane == e1
    oh2 = lane == e2
    both = jnp.where(jnp.logical_or(oh1, oh2), 1.0, 0.0)
    tm = logits.shape[0]
    lower = (lax.broadcasted_iota(jnp.int32, (tm, tm), 1) < lax.broadcasted_iota(jnp.int32, (tm, tm), 0))
    before = jnp.dot(jnp.where(lower, 1.0, 0.0).astype(MXU_DT), both.astype(MXU_DT),
                     preferred_element_type=F32) + carry
    rank1 = jnp.sum(jnp.where(oh1, before, 0.0), axis=-1, keepdims=True)
    rank2 = jnp.sum(jnp.where(oh2, before, 0.0), axis=-1, keepdims=True)
    out = jnp.where(lane == 0, e1, 0.0)
    out = jnp.where(lane == 1, e2, out)
    out = jnp.where(lane == 2, w1, out)
    out = jnp.where(lane == 3, w2, out)
    out = jnp.where(lane == 4, rank1, out)
    out = jnp.where(lane == 5, rank2, out)
    return out, carry + jnp.sum(both, axis=0, keepdims=True)


def _merge_kernel(x_ref, gate_ref, z_ref, ya_ref, yb_ref, hf_ref, hb_ref, xc_ref, gm_ref, sk_ref,
                  wbr_ref, wo_ref, mod_ref, g2_ref, wr_ref, br_ref, xo_ref, rt_ref, cnt_ref, carry_ref):
    @pl.when(pl.program_id(0) == 0)
    def _():
        carry_ref[...] = jnp.zeros_like(carry_ref)

    hs_t = hf_ref[...].astype(F32) + hb_ref[...].astype(F32)
    hn_t = []
    for h in range(H_C):
        part = hs_t[h * DH_C:(h + 1) * DH_C, :]
        hn_t.append(part * lax.rsqrt(jnp.mean(part * part, axis=0, keepdims=True) + EPS))
    hn = jnp.concatenate(hn_t, axis=0).T * gm_ref[...]
    yc = (hn + sk_ref[...] * xc_ref[...].astype(F32)) * _sigmoid(z_ref[...].astype(F32))
    ys = (ya_ref[...], yb_ref[...], yc.astype(MXU_DT))
    merged = None
    for i in range(3):
        gate = _sigmoid(gate_ref[:, i * D_MODEL:(i + 1) * D_MODEL].astype(F32))
        term = gate * jnp.dot(ys[i], wbr_ref[i], preferred_element_type=F32)
        merged = term if merged is None else merged + term
    y = jnp.dot(merged.astype(MXU_DT), wo_ref[...], preferred_element_type=F32)
    xn = x_ref[...] + mod_ref[2:3, :] * y
    xo_ref[...] = xn
    h2 = _rms(xn) * g2_ref[...] * (1.0 + mod_ref[4:5, :]) + mod_ref[3:4, :]
    h_hi, h_lo = _split_bf16(h2)
    logits = (jnp.dot(h_hi, wr_ref[0], preferred_element_type=F32)
              + jnp.dot(h_lo, wr_ref[0], preferred_element_type=F32)
              + jnp.dot(h_hi, wr_ref[1], preferred_element_type=F32)) + br_ref[...]
    route, carry = _route(logits, carry_ref[0:1, :])
    rt_ref[...] = route
    carry_ref[0:1, :] = carry
    cnt_ref[...] = jnp.broadcast_to(carry, cnt_ref.shape)


def _merge(xall, p, ya, yb, hf, hb, xconv, g_m, skip, w_br, w_o, mod, g2, w_route, b_route, n_batch, tps):
    r, d = xall.shape
    tm = ROW_TILE

    def mod_idx(i):
        return (jnp.where(i % tps == 0, n_batch, i // tps), 0, 0)

    row = lambda w: pl.BlockSpec((tm, w), lambda i: (i, 0))
    col = lambda w: pl.BlockSpec((w, tm), lambda i: (0, i))
    full = lambda shape: pl.BlockSpec(shape, lambda i: (0,) * len(shape))
    return pl.pallas_call(
        _merge_kernel,
        grid=(r // tm,),
        in_specs=[
            row(d),
            pl.BlockSpec((tm, GATE_W), lambda i: (i, P_GATE // GATE_W)),
            pl.BlockSpec((tm, W_C), lambda i: (i, P_Z // W_C)),
            row(W_BR), row(W_BR), col(W_C), col(W_C), row(W_C),
            full((1, W_C)), full((1, W_C)),
            full((3, W_BR, d)), full((d, d)),
            pl.BlockSpec((None, N_MOD, d), mod_idx),
            full((1, d)), full((2, d, LANE)), full((1, LANE)),
        ],
        out_specs=[row(d), row(LANE), full((8, LANE))],
        out_shape=[jax.ShapeDtypeStruct((r, d), F32), jax.ShapeDtypeStruct((r, LANE), F32),
                   jax.ShapeDtypeStruct((8, LANE), F32)],
        scratch_shapes=[pltpu.VMEM((8, LANE), F32)],
        compiler_params=_cparams("arbitrary"),
        name="merge_route",
    )(xall, p, p, ya, yb, hf, hb, xconv, g_m, skip, w_br, w_o, mod, g2, w_route, b_route)


def _row_copy(src_hbm, src_row, dst_ref, dst_row, sem):
    return pltpu.make_async_copy(src_hbm.at[pl.ds(src_row, 1), :], dst_ref.at[pl.ds(dst_row, 1), :], sem)


def _dispatch_kernel(fill_ref, nlive_ref, pos_ref, x_ref, g2_ref, mod_ref, xs_hbm, hbuf, zbuf, zsem, sem):
    i = pl.program_id(0)
    n = pl.num_programs(0)
    tm = x_ref.shape[0]
    slot = i % 2
    bm = zbuf.shape[0]
    n_blocks = xs_hbm.shape[0] // bm

    @pl.when(i == 0)
    def _():
        zbuf[...] = jnp.zeros_like(zbuf)

        def fill_block(b):
            return pltpu.make_async_copy(zbuf, xs_hbm.at[pl.ds(b * bm, bm), :], zsem)

        def fill_last(k, carry):
            fill_block(fill_ref[k]).start()
            return carry

        def fill_unused(b, carry):
            fill_block(b).start()
            return carry

        def wait_fill(k, carry):
            fill_block(0).wait()
            return carry

        n_live = nlive_ref[0]
        n_used = nlive_ref[1]
        lax.fori_loop(0, n_used, fill_last, 0)
        lax.fori_loop(n_live, n_blocks, fill_unused, 0)
        lax.fori_loop(0, n_used + n_blocks - n_live, wait_fill, 0)

    hbuf[slot] = _rms(x_ref[...]) * g2_ref[...] * (1.0 + mod_ref[4:5, :]) + mod_ref[3:4, :]

    def body(c, carry):
        for u in range(DMA_UNROLL):
            r = c * DMA_UNROLL + u
            for j in range(TOP_K):
                pltpu.make_async_copy(hbuf.at[slot, pl.ds(r, 1), :],
                                      xs_hbm.at[pl.ds(pos_ref[0, j * tm + r], 1), :], sem.at[slot]).start()
        return carry

    lax.fori_loop(0, tm // DMA_UNROLL, body, 0)

    def wait_tile(s):
        pltpu.make_async_copy(hbuf.at[s], xs_hbm.at[pl.ds(0, tm), :], sem.at[s]).wait()
        pltpu.make_async_copy(hbuf.at[s], xs_hbm.at[pl.ds(0, tm), :], sem.at[s]).wait()

    @pl.when(i > 0)
    def _():
        wait_tile(1 - slot)

    @pl.when(i == n - 1)
    def _():
        wait_tile(slot)


def _moe_dispatch(xall, g2, mod, fill_blocks, n_live, pos_tiles, n_slots, n_batch, tps):
    r, d = xall.shape
    tm = ROW_TILE
    assert TOP_K == 2

    def mod_idx(i, fb, nl):
        return (jnp.where(i % tps == 0, n_batch, i // tps), 0, 0)

    grid_spec = pltpu.PrefetchScalarGridSpec(
        num_scalar_prefetch=2,
        grid=(r // tm,),
        in_specs=[
            pl.BlockSpec((None, 1, TOP_K * tm), lambda i, fb, nl: (i, 0, 0), memory_space=pltpu.SMEM),
            pl.BlockSpec((tm, d), lambda i, fb, nl: (i, 0)),
            pl.BlockSpec((1, d), lambda i, fb, nl: (0, 0)),
            pl.BlockSpec((None, N_MOD, d), mod_idx),
        ],
        out_specs=pl.BlockSpec(memory_space=pl.ANY),
        scratch_shapes=[pltpu.VMEM((2, tm, d), F32), pltpu.VMEM((MOE_BLOCK, d), F32),
                        pltpu.SemaphoreType.DMA(()), pltpu.SemaphoreType.DMA((2,))],
    )
    return pl.pallas_call(
        _dispatch_kernel,
        grid_spec=grid_spec,
        out_shape=jax.ShapeDtypeStruct((n_slots, d), F32),
        compiler_params=_cparams("arbitrary"),
        name="moe_dispatch",
    )(fill_blocks, n_live, pos_tiles, xall, g2, mod)


def _moe_kernel(be_ref, nv_ref, x_ref, wg_ref, wu_ref, wd_ref, y_ref):
    n_valid = nv_ref[pl.program_id(0)]

    @pl.when(n_valid > 0)
    def _():
        x = x_ref[...].astype(MXU_DT)
        a = jnp.dot(x, wg_ref[...], preferred_element_type=F32)
        u = jnp.dot(x, wu_ref[...], preferred_element_type=F32)
        act = (a * _sigmoid(a) * u).astype(MXU_DT)
        y_ref[...] = jnp.dot(act, wd_ref[...], preferred_element_type=F32)

    @pl.when(n_valid <= 0)
    def _():
        y_ref[...] = jnp.zeros_like(y_ref)


def _moe_experts(xs, block_e, n_valid, w_gate, w_up, w_down, layer):
    n_slots, d = xs.shape
    bm = MOE_BLOCK
    grid_spec = pltpu.PrefetchScalarGridSpec(
        num_scalar_prefetch=2,
        grid=(n_slots // bm,),
        in_specs=[
            pl.BlockSpec((bm, d), lambda i, be, nv: (i, 0)),
            pl.BlockSpec((None, None, d, D_EXPERT), lambda i, be, nv: (layer, be[i], 0, 0)),
            pl.BlockSpec((None, None, d, D_EXPERT), lambda i, be, nv: (layer, be[i], 0, 0)),
            pl.BlockSpec((None, None, D_EXPERT, d), lambda i, be, nv: (layer, be[i], 0, 0)),
        ],
        out_specs=pl.BlockSpec((bm, d), lambda i, be, nv: (i, 0)),
    )
    return pl.pallas_call(
        _moe_kernel,
        grid_spec=grid_spec,
        out_shape=jax.ShapeDtypeStruct((n_slots, d), F32),
        compiler_params=_cparams("arbitrary"),
        name="moe_experts",
    )(block_e, n_valid, xs, w_gate, w_up, w_down)


def _combine_kernel(pos_ref, pos_next_ref, x_ref, rt_ref, mod_ref, gf_ref, yb_hbm, xo_ref, ybuf, sem, *, final):
    i = pl.program_id(0)
    n = pl.num_programs(0)
    tm = x_ref.shape[0]

    def start_gather(ids_ref, slot):
        def body(c, carry):
            for u in range(DMA_UNROLL):
                r = c * DMA_UNROLL + u
                _row_copy(yb_hbm, ids_ref[0, r], ybuf.at[slot], r, sem.at[slot]).start()
            return carry
        lax.fori_loop(0, TOP_K * tm // DMA_UNROLL, body, 0)

    @pl.when(i == 0)
    def _():
        start_gather(pos_ref, 0)

    @pl.when(i + 1 < n)
    def _():
        start_gather(pos_next_ref, (i + 1) % 2)

    slot = i % 2
    pltpu.make_async_copy(yb_hbm.at[pl.ds(0, TOP_K * tm), :], ybuf.at[slot], sem.at[slot]).wait()
    rt = rt_ref[...]
    f = rt[:, 2:3] * ybuf[slot, 0:tm, :] + rt[:, 3:4] * ybuf[slot, tm:2 * tm, :]
    xn = x_ref[...] + mod_ref[5:6, :] * f
    xo_ref[...] = _rms(xn) * gf_ref[...] if final else xn


def _combine(xall, route, mod, g_final, yb, pos, n_batch, tps, final):
    r, d = xall.shape
    tm = ROW_TILE
    nt = r // tm
    last = nt - 1

    def mod_idx(i):
        return (jnp.where(i % tps == 0, n_batch, i // tps), 0, 0)

    row = lambda w: pl.BlockSpec((tm, w), lambda i: (i, 0))
    return pl.pallas_call(
        functools.partial(_combine_kernel, final=final),
        grid=(nt,),
        in_specs=[
            pl.BlockSpec((None, 1, TOP_K * tm), lambda i: (i, 0, 0), memory_space=pltpu.SMEM),
            pl.BlockSpec((None, 1, TOP_K * tm), lambda i: (jnp.minimum(i + 1, last), 0, 0),
                         memory_space=pltpu.SMEM),
            row(d), row(LANE),
            pl.BlockSpec((None, N_MOD, d), mod_idx),
            pl.BlockSpec((1, d), lambda i: (0, 0)),
            pl.BlockSpec(memory_space=pl.ANY),
        ],
        out_specs=row(d),
        out_shape=jax.ShapeDtypeStruct((r, d), F32),
        scratch_shapes=[pltpu.VMEM((2, TOP_K * tm, d), F32), pltpu.SemaphoreType.DMA((2,))],
        compiler_params=_cparams("arbitrary"),
        name="moe_combine",
    )(pos, pos, xall, route, mod, g_final, yb)


def _dispatch_plan(route, counts_f, n_tiles):
    n = route.shape[0]
    bm = MOE_BLOCK
    n_blocks = -(-(n * TOP_K + N_EXPERTS * (bm - 1)) // bm)
    counts = counts_f[:N_EXPERTS].astype(jnp.int32)
    padded = (counts + bm - 1) // bm * bm
    pend = jnp.cumsum(padded)
    seg_off = pend - padded
    eid = route[:, 0:TOP_K].astype(jnp.int32)
    rank = route[:, 4:4 + TOP_K].astype(jnp.int32)
    experts = jnp.arange(N_EXPERTS, dtype=jnp.int32)
    pos = jnp.sum(jnp.where(eid[..., None] == experts, seg_off, 0), axis=-1) + rank
    blk0 = jnp.arange(n_blocks, dtype=jnp.int32) * bm
    block_e = jnp.minimum(jnp.sum((blk0[:, None] >= pend[None, :]).astype(jnp.int32), axis=1), N_EXPERTS - 1)
    seg_end = seg_off + counts
    n_valid = jnp.clip(seg_end[block_e] - blk0, 0, bm).astype(jnp.int32)
    pos_tiles = pos.reshape(n_tiles, n // n_tiles, TOP_K).transpose(0, 2, 1).reshape(n_tiles, 1, -1)
    used = counts > 0
    last_blocks = jnp.sort(jnp.where(used, pend // bm - 1, n_blocks)).astype(jnp.int32)
    n_live = jnp.stack([pend[-1] // bm, jnp.sum(used)]).astype(jnp.int32)
    return block_e.astype(jnp.int32), n_valid, last_blocks, n_live, pos_tiles, n_blocks * bm


def _rope_tables(seq, ctx_rows):
    t = jnp.arange(seq - ctx_rows)
    inv = ROPE_BASE ** (-jnp.arange(0, ROT_AX, 2, dtype=F32) / ROT_AX)
    ang_r = (t // GRID_W).astype(F32)[:, None] * inv
    ang_c = (t % GRID_W).astype(F32)[:, None] * inv
    cos64 = jnp.concatenate([jnp.cos(ang_r)] * 2 + [jnp.cos(ang_c)] * 2, axis=1)
    sin64 = jnp.concatenate([-jnp.sin(ang_r), jnp.sin(ang_r), -jnp.sin(ang_c), jnp.sin(ang_c)], axis=1)
    cos = jnp.concatenate([jnp.ones((ctx_rows, DK), F32), cos64], axis=0)
    sin = jnp.concatenate([jnp.zeros((ctx_rows, DK), F32), sin64], axis=0)
    return jnp.tile(cos, (1, LANE // DK)), jnp.tile(sin, (1, LANE // DK))


def kernel(x, c, ctx, c_ctx, w_mod, b_mod, g_norm1, g_norm2, w_in, diff_lambda, g_diff_subln, w_gmlp_s, b_gmlp_s, g_gmlp_v, w_conv_m, b_conv_m, w_qkv_m, w_if_m, b_if_m, g_mlstm_norm, skip_m, w_branch, w_out, w_route_g, b_route_g, w_route_e, b_route_e, w_e_gate, w_e_up, w_e_down, g_final):
    n_batch, t_lat, d = x.shape
    ctx_rows = ctx.shape[1]
    depth = w_in.shape[0]
    seq = ctx_rows + t_lat
    assert d == D_MODEL and ctx_rows == ROW_TILE and t_lat % ROW_TILE == 0 and t_lat % GRID_W == 0
    tps = seq // ROW_TILE
    n_tiles = n_batch * tps

    xall = jnp.concatenate([ctx, x], axis=1).reshape(n_batch * seq, d)
    mb = -(-(n_batch + 1) // 8) * 8
    c_all = jnp.zeros((mb, d), F32).at[:n_batch].set(c).at[n_batch].set(c_ctx)
    mod_all = _modulation(c_all, w_mod, b_mod)[:, :n_batch + 1].reshape(depth, n_batch + 1, N_MOD, d)
    cos, sin_signed = _rope_tables(seq, ctx_rows)
    w_in_b = w_in.astype(MXU_DT)
    w_eg_b, w_eu_b, w_ed_b = w_e_gate.astype(MXU_DT), w_e_up.astype(MXU_DT), w_e_down.astype(MXU_DT)

    for l in range(depth):
        lam_init = 0.8 - 0.6 * math.exp(-0.3 * l)
        mod = mod_all[l]
        p = _inproj(xall, g_norm1[l][None], mod, cos, sin_signed, w_in_b, l, n_batch, tps)

        ya = _attention(p, diff_lambda[l], g_diff_subln[l][None], lam_init, n_batch, tps, ctx_rows)

        b_full = jnp.repeat(b_gmlp_s[l].T, W_B // G_B, axis=1)
        yb = _gmlp(p, g_gmlp_v[l][None], w_gmlp_s[l].astype(MXU_DT), b_full)

        w_if_pad = jnp.zeros((3 * W_C, LANE), F32).at[:, :4 * H_C].set(w_if_m[l]).astype(MXU_DT)
        b_if_pad = jnp.zeros((1, LANE), F32).at[0, :4 * H_C].set(b_if_m[l])
        xconv, q_m, k_m, v_m, gates = _mlstm_features(
            p, w_conv_m[l], b_conv_m[l][None], w_qkv_m[l].astype(MXU_DT), w_if_pad, b_if_pad, tps)
        hf, hb = _mlstm_scan(q_m, k_m, v_m, gates, n_batch, seq, ctx_rows)

        w_route = (jnp.zeros((d, LANE), F32).at[:, :N_GROUPS].set(w_route_g[l])
                   .at[:, N_GROUPS:N_GROUPS + N_EXPERTS].set(w_route_e[l]))
        w_route = jnp.stack(_split_bf16(w_route))
        b_route = (jnp.zeros((1, LANE), F32).at[0, :N_GROUPS].set(b_route_g[l])
                   .at[0, N_GROUPS:N_GROUPS + N_EXPERTS].set(b_route_e[l]))
        xall, route, counts = _merge(xall, p, ya, yb, hf, hb, xconv, g_mlstm_norm[l][None], skip_m[l][None],
                                         w_branch[l].astype(MXU_DT), w_out[l].astype(MXU_DT), mod,
                                         g_norm2[l][None], w_route, b_route, n_batch, tps)

        block_e, n_valid, fill_blocks, n_live, pos_tiles, n_slots = _dispatch_plan(route, counts[0], n_tiles)
        xs = _moe_dispatch(xall, g_norm2[l][None], mod, fill_blocks, n_live, pos_tiles, n_slots, n_batch, tps)
        y_sorted = _moe_experts(xs, block_e, n_valid, w_eg_b, w_eu_b, w_ed_b, l)
        xall = _combine(xall, route, mod, g_final[None], y_sorted, pos_tiles, n_batch, tps, l == depth - 1)

    return xall.reshape(n_batch, seq, d)[:, ctx_rows:]
```

```python
import functools
import math
from typing import NamedTuple

import jax
import jax.numpy as jnp
from jax import lax
from jax.experimental import pallas as pl
from jax.experimental.pallas import tpu as pltpu

F32 = jnp.float32
MXU_DT = jnp.bfloat16
ACT_DT = jnp.bfloat16
HIGHEST = lax.Precision.HIGHEST

D_MODEL = 1024
N_MOD = 6
EPS = 1e-6
GRID_W = 64
H_A, DK = 4, 64
DV = 2 * DK
ROT_AX = DK // 2
ROPE_BASE = 10000.0
W_B, G_B, CHUNK_B = 512, 4, 128
H_C, DH_C, CONV_K, MLSTM_CHUNK = 4, 128, 3, 128
W_C = H_C * DH_C
N_GROUPS, EXP_PER_GROUP, TOP_K, D_EXPERT = 4, 8, 2, 512
N_EXPERTS = N_GROUPS * EXP_PER_GROUP
W_BR = 512

GATE_W = 3 * D_MODEL
P_GATE = 0
P_K = GATE_W
P_V = P_K + 512
P_XM = P_V + 512
P_Q = P_XM + 512
P_UV = P_Q + 512
P_Z = P_UV + 2 * W_B
IN_COLS = P_Z + W_C
ORIG_GATE0 = IN_COLS - GATE_W

LANE = 128
ROW_TILE = 256
PROJ_CHUNK = 512
MOE_BLOCK = 256
HALO = 16
DMA_UNROLL = 8
ATTN_KV_CHUNK = 256
ATTN_Q_ROWS = 256
LOG2_E = math.log2(math.e)
VMEM_LIMIT = 56 * 1024 * 1024
NEG_BIG = -1e30


class _Tiles(NamedTuple):
    n_batch: int
    tps: int
    skip_ctx: bool

    @property
    def per_seq(self):
        return self.tps - 1 if self.skip_ctx else self.tps

    @property
    def count(self):
        return self.n_batch * self.per_seq

    def full(self, i):
        if not self.skip_ctx:
            return i
        return (i // self.per_seq) * self.tps + 1 + i % self.per_seq

    def mod_row(self, i):
        if self.skip_ctx:
            return i // self.per_seq
        return jnp.where(i % self.tps == 0, self.n_batch, i // self.tps)


def _cparams(*sem):
    return pltpu.CompilerParams(dimension_semantics=sem, vmem_limit_bytes=VMEM_LIMIT)


def _rms(x):
    return x * lax.rsqrt(jnp.mean(x * x, axis=-1, keepdims=True) + EPS)


def _sigmoid(x):
    return 0.5 * jnp.tanh(0.5 * x) + 0.5


def _split_bf16(x):
    hi = x.astype(jnp.bfloat16)
    return hi, (x - hi.astype(F32)).astype(jnp.bfloat16)


def _mod_kernel(c_ref, w_ref, b_ref, o_ref):
    c = c_ref[...]
    s = c * jax.nn.sigmoid(c)
    o_ref[...] = jnp.dot(s, w_ref[...], preferred_element_type=F32, precision=HIGHEST) + b_ref[...]


def _modulation(c_all, w_mod, b_mod):
    n_layer = w_mod.shape[0]
    mb, d = c_all.shape
    tn = 1024
    return pl.pallas_call(
        _mod_kernel,
        grid=(n_layer, N_MOD * d // tn),
        in_specs=[
            pl.BlockSpec((mb, d), lambda l, j: (0, 0)),
            pl.BlockSpec((None, d, tn), lambda l, j: (l, 0, j)),
            pl.BlockSpec((None, 1, tn), lambda l, j: (l, 0, j)),
        ],
        out_specs=pl.BlockSpec((None, mb, tn), lambda l, j: (l, 0, j)),
        out_shape=jax.ShapeDtypeStruct((n_layer, mb, N_MOD * d), F32),
        compiler_params=_cparams("parallel", "parallel"),
        name="modulation",
    )(c_all, w_mod, b_mod.reshape(n_layer, 1, N_MOD * d))


def _rope(acc, cos, sin_signed):
    w = acc.shape[1]
    lane = lax.broadcasted_iota(jnp.int32, acc.shape, 1)
    partner = jnp.where((lane & 16) == 0, pltpu.roll(acc, w - 16, 1), pltpu.roll(acc, 16, 1))
    reps = w // cos.shape[1]
    return acc * jnp.tile(cos, (1, reps)) + partner * jnp.tile(sin_signed, (1, reps))


def _inproj_kernel(x_ref, g_ref, mod_ref, cos_ref, sin_ref, w_ref, o_ref, kt_ref):
    x = x_ref[...]
    h = (_rms(x) * g_ref[...] * (1.0 + mod_ref[1:2, :]) + mod_ref[0:1, :]).astype(MXU_DT)
    k_chunk = P_K // PROJ_CHUNK
    q_chunk = P_Q // PROJ_CHUNK
    for c in range(IN_COLS // PROJ_CHUNK):
        cols = slice(c * PROJ_CHUNK, (c + 1) * PROJ_CHUNK)
        w0 = (ORIG_GATE0 + c * PROJ_CHUNK) % IN_COLS
        acc = jnp.dot(h, w_ref[:, w0:w0 + PROJ_CHUNK], preferred_element_type=F32)
        if c == k_chunk:
            acc = _rope(acc, cos_ref[...], sin_ref[...])
            kt_ref[...] = acc.T.astype(kt_ref.dtype)
        elif c == q_chunk:
            acc = _rope(acc, cos_ref[...], sin_ref[...]) * (DK ** -0.5 * LOG2_E)
        o_ref[:, cols] = acc.astype(o_ref.dtype)


def _inproj(xall, g1, mod, cos, sin_signed, w_in, layer, tiles):
    r, d = xall.shape
    tm = ROW_TILE
    tps = tiles.tps
    assert ORIG_GATE0 % PROJ_CHUNK == 0 and not tiles.skip_ctx

    def mod_idx(i):
        return (tiles.mod_row(i), 0, 0)

    return pl.pallas_call(
        _inproj_kernel,
        grid=(r // tm,),
        in_specs=[
            pl.BlockSpec((tm, d), lambda i: (i, 0)),
            pl.BlockSpec((1, d), lambda i: (0, 0)),
            pl.BlockSpec((None, N_MOD, d), mod_idx),
            pl.BlockSpec((tm, LANE), lambda i: (i % tps, 0)),
            pl.BlockSpec((tm, LANE), lambda i: (i % tps, 0)),
            pl.BlockSpec((None, d, IN_COLS), lambda i: (layer, 0, 0)),
        ],
        out_specs=[pl.BlockSpec((tm, IN_COLS), lambda i: (i, 0)),
                   pl.BlockSpec((PROJ_CHUNK, tm), lambda i: (0, i))],
        out_shape=[jax.ShapeDtypeStruct((r, IN_COLS), ACT_DT), jax.ShapeDtypeStruct((PROJ_CHUNK, r), ACT_DT)],
        compiler_params=_cparams("parallel"),
        name="inproj",
    )(xall, g1, mod, cos, sin_signed, w_in)


def _attn_kernel(dl_ref, g_ref, q_ref, kt_ref, v_ref, o_ref, vext_ref, s_ref, *, lam_init, ctx_rows, has_ctx):
    @pl.when(pl.program_id(2) == 0)
    def _():
        one_hot = lax.broadcasted_iota(jnp.int32, v_ref.shape, 1) == 0
        vext_ref[:, :DV] = v_ref[...]
        vext_ref[:, DV:] = jnp.where(one_hot, 1.0, 0.0).astype(vext_ref.dtype)

    dl = dl_ref[...]
    lam = (jnp.exp(jnp.sum(dl[0:1] * dl[1:2], keepdims=True))
           - jnp.exp(jnp.sum(dl[2:3] * dl[3:4], keepdims=True)) + lam_init)
    kc = ATTN_KV_CHUNK

    def attend(kv_rows):
        for r0 in range(0, q_ref.shape[0], ATTN_Q_ROWS):
            attend_rows(kv_rows, slice(r0, r0 + ATTN_Q_ROWS))

    def attend_rows(kv_rows, rows):
        q = q_ref[rows, :]
        lane = lax.broadcasted_iota(jnp.int32, q.shape, 1)
        zero = jnp.zeros_like(q)
        qms = (jnp.where(lane < DK, q, zero), jnp.where(lane >= DK, q, zero))
        chunks = [slice(c * kc, (c + 1) * kc) for c in range(kv_rows // kc)]

        def scores(m, cols):
            return jnp.dot(qms[m], kt_ref[:, cols], preferred_element_type=F32)

        mxs = []
        for m in range(2):
            mrun = None
            for cols in chunks:
                s = scores(m, cols)
                s_ref[m, rows, cols] = s
                for g in range(kc // LANE):
                    blk = s[:, g * LANE:(g + 1) * LANE]
                    mrun = blk if mrun is None else jnp.maximum(mrun, blk)
            mxs.append(jnp.max(mrun, axis=-1, keepdims=True))
        accs = [jnp.zeros((q.shape[0], 2 * DV), F32)] * 2
        for cols in chunks:
            for m in range(2):
                p = jnp.exp2(s_ref[m, rows, cols] - mxs[m])
                accs[m] = accs[m] + jnp.dot(p.astype(MXU_DT), vext_ref[cols, :], preferred_element_type=F32)
        o = (accs[0][:, :DV] * (1.0 / accs[0][:, DV:DV + 1])
             - accs[1][:, :DV] * (lam / accs[1][:, DV:DV + 1]))
        o_ref[rows, :] = (_rms(o) * g_ref[...] * (1.0 - lam_init)).astype(o_ref.dtype)

    if not has_ctx:
        attend(kt_ref.shape[1])
        return

    is_ctx = pl.program_id(2) == 0

    @pl.when(is_ctx)
    def _():
        attend(ctx_rows)

    @pl.when(jnp.logical_not(is_ctx))
    def _():
        attend(kt_ref.shape[1])


def _attention(p, k_t, diff_lambda, g_sub, lam_init, tiles, ctx_rows):
    tq = ROW_TILE
    tps = tiles.tps
    seq = tps * tq
    q_off = tps - tiles.per_seq
    kern = functools.partial(_attn_kernel, lam_init=lam_init, ctx_rows=ctx_rows, has_ctx=not tiles.skip_ctx)
    return pl.pallas_call(
        kern,
        grid=(tiles.n_batch, H_A, tiles.per_seq),
        in_specs=[
            pl.BlockSpec((4, DK), lambda b, h, i: (0, 0)),
            pl.BlockSpec((1, DV), lambda b, h, i: (0, 0)),
            pl.BlockSpec((tq, LANE), lambda b, h, i: (b * tps + q_off + i, P_Q // LANE + h)),
            pl.BlockSpec((LANE, seq), lambda b, h, i: (h, b)),
            pl.BlockSpec((seq, LANE), lambda b, h, i: (b, P_V // LANE + h)),
        ],
        out_specs=pl.BlockSpec((tq, LANE), lambda b, h, i: (b * tiles.per_seq + i, h)),
        out_shape=jax.ShapeDtypeStruct((tiles.count * tq, H_A * DV), ACT_DT),
        scratch_shapes=[pltpu.VMEM((seq, 2 * DV), MXU_DT), pltpu.VMEM((2, tq, seq), F32)],
        compiler_params=_cparams("parallel", "parallel", "arbitrary"),
        name="diff_attention",
    )(diff_lambda, g_sub, p, k_t, p)


def _gmlp_kernel(uv_ref, gv_ref, ws_ref, bs_ref, o_ref):
    uv = jax.nn.gelu(uv_ref[...].astype(F32))
    u = uv[:, :W_B]
    v = (_rms(uv[:, W_B:]) * gv_ref[...]).astype(MXU_DT)
    gw = W_B // G_B
    for c in range(uv.shape[0] // CHUNK_B):
        rows = slice(c * CHUNK_B, (c + 1) * CHUNK_B)
        for g in range(G_B):
            cols = slice(g * gw, (g + 1) * gw)
            mixed = jnp.dot(ws_ref[g], v[rows, cols], preferred_element_type=F32) + bs_ref[:, cols]
            o_ref[rows, cols] = (u[rows, cols] * mixed).astype(o_ref.dtype)


def _gmlp(p, g_v, w_s, b_full, tiles):
    tm = ROW_TILE
    r = tiles.count * tm
    return pl.pallas_call(
        _gmlp_kernel,
        grid=(tiles.count,),
        in_specs=[
            pl.BlockSpec((tm, 2 * W_B), lambda i: (tiles.full(i), P_UV // (2 * W_B))),
            pl.BlockSpec((1, W_B), lambda i: (0, 0)),
            pl.BlockSpec((G_B, CHUNK_B, CHUNK_B), lambda i: (0, 0, 0)),
            pl.BlockSpec((CHUNK_B, W_B), lambda i: (0, 0)),
        ],
        out_specs=pl.BlockSpec((tm, W_B), lambda i: (i, 0)),
        out_shape=jax.ShapeDtypeStruct((r, W_B), ACT_DT),
        compiler_params=_cparams("parallel"),
        name="gmlp",
    )(p, g_v, w_s, b_full)


def _mfeat_kernel(x_ref, prev_ref, next_ref, wc_ref, bc_ref, wqkv_ref, wif_ref, bif_ref,
                  xc_ref, q_ref, k_ref, v_ref, g_ref, *, tps):
    tm = x_ref.shape[0]
    j = pl.program_id(0) % tps
    seg_start = jnp.logical_or(j == 0, j == 1)
    seg_end = jnp.logical_or(j == 0, j == tps - 1)
    x = x_ref[...].astype(F32)
    prow = jnp.where(seg_start, 0.0, prev_ref[...].astype(F32)[HALO - 1:HALO, :])
    nrow = jnp.where(seg_end, 0.0, next_ref[...].astype(F32)[0:1, :])
    rid = lax.broadcasted_iota(jnp.int32, x.shape, 0)
    xp = jnp.where(rid == 0, prow, pltpu.roll(x, 1, 0))
    xn = jnp.where(rid == tm - 1, nrow, pltpu.roll(x, tm - 1, 0))
    xc = wc_ref[0:1, :] * xp + wc_ref[1:2, :] * x + wc_ref[2:3, :] * xn + bc_ref[...]
    xc = xc * _sigmoid(xc)
    xc_ref[...] = xc.astype(xc_ref.dtype)
    xcb = xc.astype(MXU_DT)
    xb = x_ref[...]
    qs, ks, vs = [], [], []
    for h in range(H_C):
        cols = slice(h * DH_C, (h + 1) * DH_C)
        qs.append(jnp.dot(xcb[:, cols], wqkv_ref[0, h], preferred_element_type=F32))
        ks.append(jnp.dot(xcb[:, cols], wqkv_ref[1, h], preferred_element_type=F32))
        vs.append(jnp.dot(xb[:, cols], wqkv_ref[2, h], preferred_element_type=F32))
    q = jnp.concatenate(qs, axis=1)
    k = jnp.concatenate(ks, axis=1)
    v = jnp.concatenate(vs, axis=1)
    qkv = jnp.concatenate([q, k, v], axis=1).astype(MXU_DT)
    gates = jnp.dot(qkv, wif_ref[...], preferred_element_type=F32) + bif_ref[...]
    g_ref[...] = gates.T[0:g_ref.shape[0], :]
    q_ref[...] = q.T.astype(q_ref.dtype)
    k_ref[...] = (k * (DH_C ** -0.5)).astype(k_ref.dtype)
    v_ref[...] = v.T.astype(v_ref.dtype)


def _mlstm_features(p, w_conv, b_conv, w_qkv, w_if_pad, b_if_pad, tps):
    r = p.shape[0]
    tm = ROW_TILE
    hpt = tm // HALO
    xm_blk = P_XM // W_C
    last_halo = r // HALO - 1
    kern = functools.partial(_mfeat_kernel, tps=tps)
    act = jax.ShapeDtypeStruct((r, W_C), ACT_DT)
    act_t = jax.ShapeDtypeStruct((W_C, r), ACT_DT)
    row_spec = pl.BlockSpec((tm, W_C), lambda i: (i, 0))
    col_spec = pl.BlockSpec((W_C, tm), lambda i: (0, i))
    return pl.pallas_call(
        kern,
        grid=(r // tm,),
        in_specs=[
            pl.BlockSpec((tm, W_C), lambda i: (i, xm_blk)),
            pl.BlockSpec((HALO, W_C), lambda i: (jnp.maximum(i * hpt - 1, 0), xm_blk)),
            pl.BlockSpec((HALO, W_C), lambda i: (jnp.minimum((i + 1) * hpt, last_halo), xm_blk)),
            pl.BlockSpec((CONV_K, W_C), lambda i: (0, 0)),
            pl.BlockSpec((1, W_C), lambda i: (0, 0)),
            pl.BlockSpec((3, H_C, DH_C, DH_C), lambda i: (0, 0, 0, 0)),
            pl.BlockSpec((3 * W_C, LANE), lambda i: (0, 0)),
            pl.BlockSpec((1, LANE), lambda i: (0, 0)),
        ],
        out_specs=[row_spec, col_spec, row_spec, col_spec,
                   pl.BlockSpec((4 * H_C, tm), lambda i: (0, i))],
        out_shape=[act, act_t, act, act_t, jax.ShapeDtypeStruct((4 * H_C, r), F32)],
        compiler_params=_cparams("parallel"),
        name="mlstm_features",
    )(p, p, p, w_conv, b_conv, w_qkv, w_if_pad, b_if_pad)


def _lane_scan(x, op, reverse):
    n = x.shape[1]
    lane = lax.broadcasted_iota(jnp.int32, x.shape, 1)
    sh = 1
    while sh < n:
        if reverse:
            shifted, valid = pltpu.roll(x, n - sh, 1), lane < n - sh
        else:
            shifted, valid = pltpu.roll(x, sh, 1), lane >= sh
        x = jnp.where(valid, op(x, shifted), x)
        sh *= 2
    return x


def _mlstm_gates(gt, m_ref, reverse):
    ln = gt.shape[1]
    nr = 2 * H_C
    ig = gt
    bcum = _lane_scan(jax.nn.log_sigmoid(pltpu.roll(gt, H_C, 0)), jnp.add, reverse)
    r = ig - bcum
    cmx = _lane_scan(r, jnp.maximum, reverse)
    g = bcum[:, 0:1] if reverse else bcum[:, ln - 1:ln]
    a_end = g + r
    m_loc = jnp.max(a_end, axis=1, keepdims=True)
    w_end = jnp.exp(a_end - m_loc)
    m_old = m_ref[:, 0:1]
    m_new = jnp.maximum(g + m_old, m_loc)
    a_old = jnp.exp(g + m_old - m_new)
    a_new = jnp.exp(m_loc - m_new)
    mx = jnp.maximum(m_old, cmx)
    w_inter = jnp.exp(m_old - mx)
    e_neg = jnp.exp(-(bcum + mx))
    r_cols = jnp.concatenate([r, jnp.zeros((LANE - nr, ln), F32)], axis=0).T
    s_id = lax.broadcasted_iota(jnp.int32, (ln, ln), 0)
    t_id = lax.broadcasted_iota(jnp.int32, (ln, ln), 1)
    keep = (s_id >= t_id) if reverse else (s_id <= t_id)
    return dict(mx=mx, w_inter=w_inter, e_neg=e_neg, w_end=w_end, a_old=a_old, a_new=a_new, m_new=m_new,
                r_cols=r_cols, keep=keep)


def _mscan_kernel(gf_ref, qf_ref, kf_ref, vf_ref, gb_ref, qb_ref, kb_ref, vb_ref,
                  hf_ref, hb_ref, cf_ref, mf_ref, cb_ref, mb_ref):
    @pl.when(pl.program_id(1) == 0)
    def _():
        cf_ref[...] = jnp.zeros_like(cf_ref)
        cb_ref[...] = jnp.zeros_like(cb_ref)
        mf_ref[...] = jnp.zeros_like(mf_ref)
        mb_ref[...] = jnp.zeros_like(mb_ref)

    ln = qf_ref.shape[1]
    dirs = ((_mlstm_gates(gf_ref[0:2 * H_C, :], mf_ref, False), qf_ref, kf_ref, vf_ref, cf_ref, hf_ref),
            (_mlstm_gates(gb_ref[2 * H_C:4 * H_C, :], mb_ref, True), qb_ref, kb_ref, vb_ref, cb_ref, hb_ref))
    pairs = [(d, h) for d in range(2) for h in range(H_C)]
    ones_row = jnp.where(lax.broadcasted_iota(jnp.int32, (DH_C, ln), 0) == 0, 1.0, 0.0).astype(MXU_DT)
    head = lambda h: slice(h * DH_C, (h + 1) * DH_C)
    row = lambda h: slice(h, h + 1)

    ks = [dirs[d][2][:, head(h)] for d, h in pairs]
    qts = [dirs[d][1][head(h), :] for d, h in pairs]
    vexts = [jnp.concatenate([dirs[d][3][head(h), :], ones_row], axis=0) for d, h in pairs]
    cexts = [dirs[d][4][h] for d, h in pairs]
    qks = [jnp.dot(k, q_t, preferred_element_type=F32) for k, q_t in zip(ks, qts)]
    rhss = []
    for i, (d, h) in enumerate(pairs):
        g = dirs[d][0]
        d_t = jnp.exp(jnp.where(g["keep"], g["r_cols"][:, row(h)] - g["mx"][row(h), :], NEG_BIG))
        wq = (qts[i].astype(F32) * g["w_inter"][row(h), :]).astype(MXU_DT)
        rhss.append(jnp.concatenate([(qks[i] * d_t).astype(MXU_DT), wq], axis=0))
    ress = [jnp.dot(jnp.concatenate([vexts[i], cexts[i].astype(MXU_DT)], axis=1), rhss[i],
                    preferred_element_type=F32) for i in range(len(pairs))]
    for i, (d, h) in enumerate(pairs):
        g = dirs[d][0]
        den = ress[i][DH_C:DH_C + 1, :]
        hh = ress[i][:DH_C, :] / jnp.maximum(jnp.abs(den), g["e_neg"][row(h), :])
        dirs[d][5][head(h), :] = hh.astype(dirs[d][5].dtype)
    clocs = []
    for i, (d, h) in enumerate(pairs):
        vw_t = (vexts[i].astype(F32) * dirs[d][0]["w_end"][row(h), :]).astype(MXU_DT)
        clocs.append(jnp.dot(vw_t, ks[i], preferred_element_type=F32))
    for i, (d, h) in enumerate(pairs):
        g = dirs[d][0]
        dirs[d][4][h] = g["a_old"][row(h), :] * cexts[i] + g["a_new"][row(h), :] * clocs[i]
    mf_ref[...] = jnp.broadcast_to(dirs[0][0]["m_new"], mf_ref.shape)
    mb_ref[...] = jnp.broadcast_to(dirs[1][0]["m_new"], mb_ref.shape)


def _mlstm_scan(q_t, k, v_t, gates_t, n_batch, seq, ctx_rows):
    ln = MLSTM_CHUNK
    nck = seq // ln
    ncc = ctx_rows // ln

    def fwd_chunk(b, s):
        return b * nck + s

    def bwd_chunk(b, s):
        return b * nck + jnp.where(s < ncc, ncc - 1 - s, nck - 1 + ncc - s)

    def specs(chunk):
        rows = pl.BlockSpec((ln, W_C), lambda b, s: (chunk(b, s), 0))
        cols = pl.BlockSpec((W_C, ln), lambda b, s: (0, chunk(b, s)))
        gate = pl.BlockSpec((4 * H_C, ln), lambda b, s: (0, chunk(b, s)))
        return [gate, cols, rows, cols], cols

    in_f, out_f = specs(fwd_chunk)
    in_b, out_b = specs(bwd_chunk)
    out = jax.ShapeDtypeStruct(q_t.shape, ACT_DT)
    return pl.pallas_call(
        _mscan_kernel,
        grid=(n_batch, nck),
        in_specs=in_f + in_b,
        out_specs=[out_f, out_b],
        out_shape=[out, out],
        scratch_shapes=[
            pltpu.VMEM((H_C, 2 * DH_C, DH_C), F32), pltpu.VMEM((8, LANE), F32),
            pltpu.VMEM((H_C, 2 * DH_C, DH_C), F32), pltpu.VMEM((8, LANE), F32),
        ],
        compiler_params=_cparams("parallel", "arbitrary"),
        name="mlstm_scan",
    )(gates_t, q_t, k, v_t, gates_t, q_t, k, v_t)


def _route(logits, carry):
    lane = lax.broadcasted_iota(jnp.int32, logits.shape, 1).astype(F32)
    big = float(4 * LANE)
    gl = jnp.where(lane < N_GROUPS, logits, NEG_BIG)
    gmax = jnp.max(gl, axis=-1, keepdims=True)
    g_star = jnp.min(jnp.where(gl == gmax, lane, big), axis=-1, keepdims=True)
    p_g = 1.0 / jnp.sum(jnp.exp(gl - gmax), axis=-1, keepdims=True)
    e_lo = g_star * EXP_PER_GROUP + N_GROUPS
    in_group = jnp.logical_and(lane >= e_lo, lane < e_lo + EXP_PER_GROUP)
    el = jnp.where(in_group, logits, NEG_BIG)
    v1 = jnp.max(el, axis=-1, keepdims=True)
    i1 = jnp.min(jnp.where(el == v1, lane, big), axis=-1, keepdims=True)
    el2 = jnp.where(lane == i1, NEG_BIG, el)
    v2 = jnp.max(el2, axis=-1, keepdims=True)
    i2 = jnp.min(jnp.where(el2 == v2, lane, big), axis=-1, keepdims=True)
    e21 = jnp.exp(v2 - v1)
    w1 = p_g / (1.0 + e21)
    w2 = p_g * e21 / (1.0 + e21)
    e1 = i1 - N_GROUPS
    e2 = i2 - N_GROUPS
    oh1 = lane == e1
    oh2 = lane == e2
    both = jnp.where(jnp.logical_or(oh1, oh2), 1.0, 0.0)
    tm = logits.shape[0]
    lower = (lax.broadcasted_iota(jnp.int32, (tm, tm), 1) < lax.broadcasted_iota(jnp.int32, (tm, tm), 0))
    before = jnp.dot(jnp.where(lower, 1.0, 0.0).astype(MXU_DT), both.astype(MXU_DT),
                     preferred_element_type=F32) + carry
    rank1 = jnp.sum(jnp.where(oh1, before, 0.0), axis=-1, keepdims=True)
    rank2 = jnp.sum(jnp.where(oh2, before, 0.0), axis=-1, keepdims=True)
    out = jnp.where(lane == 0, e1, 0.0)
    out = jnp.where(lane == 1, e2, out)
    out = jnp.where(lane == 2, w1, out)
    out = jnp.where(lane == 3, w2, out)
    out = jnp.where(lane == 4, rank1, out)
    out = jnp.where(lane == 5, rank2, out)
    return out, carry + jnp.sum(both, axis=0, keepdims=True)


def _merge_kernel(x_ref, gate_ref, z_ref, ya_ref, yb_ref, hf_ref, hb_ref, xc_ref, gm_ref, sk_ref,
                  wbr_ref, wo_ref, mod_ref, g2_ref, wr_ref, br_ref, xo_ref, rt_ref, cnt_ref, carry_ref):
    @pl.when(pl.program_id(0) == 0)
    def _():
        carry_ref[...] = jnp.zeros_like(carry_ref)

    hs_t = hf_ref[...].astype(F32) + hb_ref[...].astype(F32)
    hn_t = []
    for h in range(H_C):
        part = hs_t[h * DH_C:(h + 1) * DH_C, :]
        hn_t.append(part * lax.rsqrt(jnp.mean(part * part, axis=0, keepdims=True) + EPS))
    hn = jnp.concatenate(hn_t, axis=0).T * gm_ref[...]
    yc = (hn + sk_ref[...] * xc_ref[...].astype(F32)) * _sigmoid(z_ref[...].astype(F32))
    ys = (ya_ref[...], yb_ref[...], yc.astype(MXU_DT))
    merged = None
    for i in range(3):
        gate = _sigmoid(gate_ref[:, i * D_MODEL:(i + 1) * D_MODEL].astype(MXU_DT))
        term = gate * jnp.dot(ys[i], wbr_ref[i], preferred_element_type=F32).astype(MXU_DT)
        merged = term if merged is None else merged + term
    y = jnp.dot(merged, wo_ref[...], preferred_element_type=F32)
    xn = x_ref[...] + mod_ref[2:3, :] * y
    xo_ref[...] = xn
    h2 = _rms(xn) * g2_ref[...] * (1.0 + mod_ref[4:5, :]) + mod_ref[3:4, :]
    h_hi, h_lo = _split_bf16(h2)
    logits = (jnp.dot(h_hi, wr_ref[0], preferred_element_type=F32)
              + jnp.dot(h_lo, wr_ref[0], preferred_element_type=F32)
              + jnp.dot(h_hi, wr_ref[1], preferred_element_type=F32)) + br_ref[...]
    route, carry = _route(logits, carry_ref[0:1, :])
    rt_ref[...] = route
    carry_ref[0:1, :] = carry
    cnt_ref[...] = jnp.broadcast_to(carry, cnt_ref.shape)


def _merge(xall, p, ya, yb, hf, hb, xconv, g_m, skip, w_br, w_o, mod, g2, w_route, b_route, tiles):
    d = xall.shape[1]
    tm = ROW_TILE
    r = tiles.count * tm

    def mod_idx(i):
        return (tiles.mod_row(i), 0, 0)

    row = lambda w: pl.BlockSpec((tm, w), lambda i: (i, 0))
    frow = lambda w: pl.BlockSpec((tm, w), lambda i: (tiles.full(i), 0))
    fcol = lambda w: pl.BlockSpec((w, tm), lambda i: (0, tiles.full(i)))
    full = lambda shape: pl.BlockSpec(shape, lambda i: (0,) * len(shape))
    return pl.pallas_call(
        _merge_kernel,
        grid=(tiles.count,),
        in_specs=[
            frow(d),
            pl.BlockSpec((tm, GATE_W), lambda i: (tiles.full(i), P_GATE // GATE_W)),
            pl.BlockSpec((tm, W_C), lambda i: (tiles.full(i), P_Z // W_C)),
            row(W_BR), row(W_BR), fcol(W_C), fcol(W_C), frow(W_C),
            full((1, W_C)), full((1, W_C)),
            full((3, W_BR, d)), full((d, d)),
            pl.BlockSpec((None, N_MOD, d), mod_idx),
            full((1, d)), full((2, d, LANE)), full((1, LANE)),
        ],
        out_specs=[row(d), row(LANE), full((8, LANE))],
        out_shape=[jax.ShapeDtypeStruct((r, d), F32), jax.ShapeDtypeStruct((r, LANE), F32),
                   jax.ShapeDtypeStruct((8, LANE), F32)],
        scratch_shapes=[pltpu.VMEM((8, LANE), F32)],
        compiler_params=_cparams("arbitrary"),
        name="merge_route",
    )(xall, p, p, ya, yb, hf, hb, xconv, g_m, skip, w_br, w_o, mod, g2, w_route, b_route)


def _row_copy(src_hbm, src_row, dst_ref, dst_row, sem):
    return pltpu.make_async_copy(src_hbm.at[pl.ds(src_row, 1), :], dst_ref.at[pl.ds(dst_row, 1), :], sem)


def _dispatch_kernel(fill_ref, nlive_ref, pos_ref, x_ref, g2_ref, mod_ref, xs_hbm, hbuf, zbuf, zsem, sem):
    i = pl.program_id(0)
    n = pl.num_programs(0)
    tm = x_ref.shape[0]
    slot = i % 2
    bm = zbuf.shape[0]
    n_blocks = xs_hbm.shape[0] // bm

    @pl.when(i == 0)
    def _():
        zbuf[...] = jnp.zeros_like(zbuf)

        def fill_block(b):
            return pltpu.make_async_copy(zbuf, xs_hbm.at[pl.ds(b * bm, bm), :], zsem)

        def fill_last(k, carry):
            fill_block(fill_ref[k]).start()
            return carry

        def fill_unused(b, carry):
            fill_block(b).start()
            return carry

        def wait_fill(k, carry):
            fill_block(0).wait()
            return carry

        n_live = nlive_ref[0]
        n_used = nlive_ref[1]
        lax.fori_loop(0, n_used, fill_last, 0)
        lax.fori_loop(n_live, n_blocks, fill_unused, 0)
        lax.fori_loop(0, n_used + n_blocks - n_live, wait_fill, 0)

    hbuf[slot] = _rms(x_ref[...]) * g2_ref[...] * (1.0 + mod_ref[4:5, :]) + mod_ref[3:4, :]

    def body(c, carry):
        for u in range(DMA_UNROLL):
            r = c * DMA_UNROLL + u
            for j in range(TOP_K):
                pltpu.make_async_copy(hbuf.at[slot, pl.ds(r, 1), :],
                                      xs_hbm.at[pl.ds(pos_ref[0, j * tm + r], 1), :], sem.at[slot]).start()
        return carry

    lax.fori_loop(0, tm // DMA_UNROLL, body, 0)

    def wait_tile(s):
        pltpu.make_async_copy(hbuf.at[s], xs_hbm.at[pl.ds(0, tm), :], sem.at[s]).wait()
        pltpu.make_async_copy(hbuf.at[s], xs_hbm.at[pl.ds(0, tm), :], sem.at[s]).wait()

    @pl.when(i > 0)
    def _():
        wait_tile(1 - slot)

    @pl.when(i == n - 1)
    def _():
        wait_tile(slot)


def _moe_dispatch(xall, g2, mod, fill_blocks, n_live, pos_tiles, n_slots, tiles):
    r, d = xall.shape
    tm = ROW_TILE
    assert TOP_K == 2 and r == tiles.count * tm

    def mod_idx(i, fb, nl):
        return (tiles.mod_row(i), 0, 0)

    grid_spec = pltpu.PrefetchScalarGridSpec(
        num_scalar_prefetch=2,
        grid=(r // tm,),
        in_specs=[
            pl.BlockSpec((None, 1, TOP_K * tm), lambda i, fb, nl: (i, 0, 0), memory_space=pltpu.SMEM),
            pl.BlockSpec((tm, d), lambda i, fb, nl: (i, 0)),
            pl.BlockSpec((1, d), lambda i, fb, nl: (0, 0)),
            pl.BlockSpec((None, N_MOD, d), mod_idx),
        ],
        out_specs=pl.BlockSpec(memory_space=pl.ANY),
        scratch_shapes=[pltpu.VMEM((2, tm, d), F32), pltpu.VMEM((MOE_BLOCK, d), F32),
                        pltpu.SemaphoreType.DMA(()), pltpu.SemaphoreType.DMA((2,))],
    )
    return pl.pallas_call(
        _dispatch_kernel,
        grid_spec=grid_spec,
        out_shape=jax.ShapeDtypeStruct((n_slots, d), F32),
        compiler_params=_cparams("arbitrary"),
        name="moe_dispatch",
    )(fill_blocks, n_live, pos_tiles, xall, g2, mod)


def _moe_kernel(be_ref, nv_ref, x_ref, wg_ref, wu_ref, wd_ref, y_ref):
    n_valid = nv_ref[pl.program_id(0)]

    @pl.when(n_valid > 0)
    def _():
        x = x_ref[...].astype(MXU_DT)
        a = jnp.dot(x, wg_ref[...], preferred_element_type=F32)
        u = jnp.dot(x, wu_ref[...], preferred_element_type=F32)
        act = (a * _sigmoid(a) * u).astype(MXU_DT)
        y_ref[...] = jnp.dot(act, wd_ref[...], preferred_element_type=F32)

    @pl.when(n_valid <= 0)
    def _():
        y_ref[...] = jnp.zeros_like(y_ref)


def _moe_experts(xs, block_e, n_valid, w_gate, w_up, w_down, layer):
    n_slots, d = xs.shape
    bm = MOE_BLOCK
    grid_spec = pltpu.PrefetchScalarGridSpec(
        num_scalar_prefetch=2,
        grid=(n_slots // bm,),
        in_specs=[
            pl.BlockSpec((bm, d), lambda i, be, nv: (i, 0)),
            pl.BlockSpec((None, None, d, D_EXPERT), lambda i, be, nv: (layer, be[i], 0, 0)),
            pl.BlockSpec((None, None, d, D_EXPERT), lambda i, be, nv: (layer, be[i], 0, 0)),
            pl.BlockSpec((None, None, D_EXPERT, d), lambda i, be, nv: (layer, be[i], 0, 0)),
        ],
        out_specs=pl.BlockSpec((bm, d), lambda i, be, nv: (i, 0)),
    )
    return pl.pallas_call(
        _moe_kernel,
        grid_spec=grid_spec,
        out_shape=jax.ShapeDtypeStruct((n_slots, d), F32),
        compiler_params=_cparams("arbitrary"),
        name="moe_experts",
    )(block_e, n_valid, xs, w_gate, w_up, w_down)


def _combine_kernel(pos_ref, pos_next_ref, x_ref, rt_ref, mod_ref, gf_ref, yb_hbm, xo_ref, ybuf, sem, *, final):
    i = pl.program_id(0)
    n = pl.num_programs(0)
    tm = x_ref.shape[0]

    def start_gather(ids_ref, slot):
        def body(c, carry):
            for u in range(DMA_UNROLL):
                r = c * DMA_UNROLL + u
                _row_copy(yb_hbm, ids_ref[0, r], ybuf.at[slot], r, sem.at[slot]).start()
            return carry
        lax.fori_loop(0, TOP_K * tm // DMA_UNROLL, body, 0)

    @pl.when(i == 0)
    def _():
        start_gather(pos_ref, 0)

    @pl.when(i + 1 < n)
    def _():
        start_gather(pos_next_ref, (i + 1) % 2)

    slot = i % 2
    pltpu.make_async_copy(yb_hbm.at[pl.ds(0, TOP_K * tm), :], ybuf.at[slot], sem.at[slot]).wait()
    rt = rt_ref[...]
    f = rt[:, 2:3] * ybuf[slot, 0:tm, :] + rt[:, 3:4] * ybuf[slot, tm:2 * tm, :]
    xn = x_ref[...] + mod_ref[5:6, :] * f
    xo_ref[...] = _rms(xn) * gf_ref[...] if final else xn


def _combine(xall, route, mod, g_final, yb, pos, tiles, final):
    r, d = xall.shape
    tm = ROW_TILE
    nt = tiles.count
    last = nt - 1
    assert r == nt * tm

    def mod_idx(i):
        return (tiles.mod_row(i), 0, 0)

    row = lambda w: pl.BlockSpec((tm, w), lambda i: (i, 0))
    return pl.pallas_call(
        functools.partial(_combine_kernel, final=final),
        grid=(nt,),
        in_specs=[
            pl.BlockSpec((None, 1, TOP_K * tm), lambda i: (i, 0, 0), memory_space=pltpu.SMEM),
            pl.BlockSpec((None, 1, TOP_K * tm), lambda i: (jnp.minimum(i + 1, last), 0, 0),
                         memory_space=pltpu.SMEM),
            row(d), row(LANE),
            pl.BlockSpec((None, N_MOD, d), mod_idx),
            pl.BlockSpec((1, d), lambda i: (0, 0)),
            pl.BlockSpec(memory_space=pl.ANY),
        ],
        out_specs=row(d),
        out_shape=jax.ShapeDtypeStruct((r, d), F32),
        scratch_shapes=[pltpu.VMEM((2, TOP_K * tm, d), F32), pltpu.SemaphoreType.DMA((2,))],
        compiler_params=_cparams("arbitrary"),
        name="moe_combine",
    )(pos, pos, xall, route, mod, g_final, yb)


def _dispatch_plan(route, counts_f, n_tiles):
    n = route.shape[0]
    bm = MOE_BLOCK
    n_blocks = -(-(n * TOP_K + N_EXPERTS * (bm - 1)) // bm)
    counts = counts_f[:N_EXPERTS].astype(jnp.int32)
    padded = (counts + bm - 1) // bm * bm
    pend = jnp.cumsum(padded)
    seg_off = pend - padded
    eid = route[:, 0:TOP_K].astype(jnp.int32)
    rank = route[:, 4:4 + TOP_K].astype(jnp.int32)
    experts = jnp.arange(N_EXPERTS, dtype=jnp.int32)
    pos = jnp.sum(jnp.where(eid[..., None] == experts, seg_off, 0), axis=-1) + rank
    blk0 = jnp.arange(n_blocks, dtype=jnp.int32) * bm
    block_e = jnp.minimum(jnp.sum((blk0[:, None] >= pend[None, :]).astype(jnp.int32), axis=1), N_EXPERTS - 1)
    seg_end = seg_off + counts
    n_valid = jnp.clip(seg_end[block_e] - blk0, 0, bm).astype(jnp.int32)
    pos_tiles = pos.reshape(n_tiles, n // n_tiles, TOP_K).transpose(0, 2, 1).reshape(n_tiles, 1, -1)
    used = counts > 0
    last_blocks = jnp.sort(jnp.where(used, pend // bm - 1, n_blocks)).astype(jnp.int32)
    n_live = jnp.stack([pend[-1] // bm, jnp.sum(used)]).astype(jnp.int32)
    return block_e.astype(jnp.int32), n_valid, last_blocks, n_live, pos_tiles, n_blocks * bm


def _rope_tables(seq, ctx_rows):
    t = jnp.arange(seq - ctx_rows)
    inv = ROPE_BASE ** (-jnp.arange(0, ROT_AX, 2, dtype=F32) / ROT_AX)
    ang_r = (t // GRID_W).astype(F32)[:, None] * inv
    ang_c = (t % GRID_W).astype(F32)[:, None] * inv
    cos64 = jnp.concatenate([jnp.cos(ang_r)] * 2 + [jnp.cos(ang_c)] * 2, axis=1)
    sin64 = jnp.concatenate([-jnp.sin(ang_r), jnp.sin(ang_r), -jnp.sin(ang_c), jnp.sin(ang_c)], axis=1)
    cos = jnp.concatenate([jnp.ones((ctx_rows, DK), F32), cos64], axis=0)
    sin = jnp.concatenate([jnp.zeros((ctx_rows, DK), F32), sin64], axis=0)
    return jnp.tile(cos, (1, LANE // DK)), jnp.tile(sin, (1, LANE // DK))


def kernel(x, c, ctx, c_ctx, w_mod, b_mod, g_norm1, g_norm2, w_in, diff_lambda, g_diff_subln, w_gmlp_s, b_gmlp_s, g_gmlp_v, w_conv_m, b_conv_m, w_qkv_m, w_if_m, b_if_m, g_mlstm_norm, skip_m, w_branch, w_out, w_route_g, b_route_g, w_route_e, b_route_e, w_e_gate, w_e_up, w_e_down, g_final):
    n_batch, t_lat, d = x.shape
    ctx_rows = ctx.shape[1]
    depth = w_in.shape[0]
    seq = ctx_rows + t_lat
    assert d == D_MODEL and ctx_rows == ROW_TILE and t_lat % ROW_TILE == 0 and t_lat % GRID_W == 0
    tps = seq // ROW_TILE
    all_tiles = _Tiles(n_batch, tps, False)

    xall = jnp.concatenate([ctx, x], axis=1).reshape(n_batch * seq, d)
    mb = -(-(n_batch + 1) // 8) * 8
    c_all = jnp.zeros((mb, d), F32).at[:n_batch].set(c).at[n_batch].set(c_ctx)
    mod_all = _modulation(c_all, w_mod, b_mod)[:, :n_batch + 1].reshape(depth, n_batch + 1, N_MOD, d)
    cos, sin_signed = _rope_tables(seq, ctx_rows)
    w_in_b = w_in.astype(MXU_DT)
    w_eg_b, w_eu_b, w_ed_b = w_e_gate.astype(MXU_DT), w_e_up.astype(MXU_DT), w_e_down.astype(MXU_DT)

    for l in range(depth):
        lam_init = 0.8 - 0.6 * math.exp(-0.3 * l)
        mod = mod_all[l]
        last = l == depth - 1
        tiles = _Tiles(n_batch, tps, last)
        p, k_t = _inproj(xall, g_norm1[l][None], mod, cos, sin_signed, w_in_b, l, all_tiles)

        ya = _attention(p, k_t, diff_lambda[l], g_diff_subln[l][None], lam_init, tiles, ctx_rows)

        b_full = jnp.repeat(b_gmlp_s[l].T, W_B // G_B, axis=1)
        yb = _gmlp(p, g_gmlp_v[l][None], w_gmlp_s[l].astype(MXU_DT), b_full, tiles)

        w_if_pad = jnp.zeros((3 * W_C, LANE), F32).at[:, :4 * H_C].set(w_if_m[l]).astype(MXU_DT)
        b_if_pad = jnp.zeros((1, LANE), F32).at[0, :4 * H_C].set(b_if_m[l])
        xconv, q_m, k_m, v_m, gates = _mlstm_features(
            p, w_conv_m[l], b_conv_m[l][None], w_qkv_m[l].astype(MXU_DT), w_if_pad, b_if_pad, tps)
        hf, hb = _mlstm_scan(q_m, k_m, v_m, gates, n_batch, seq, ctx_rows)

        w_route = (jnp.zeros((d, LANE), F32).at[:, :N_GROUPS].set(w_route_g[l])
                   .at[:, N_GROUPS:N_GROUPS + N_EXPERTS].set(w_route_e[l]))
        w_route = jnp.stack(_split_bf16(w_route))
        b_route = (jnp.zeros((1, LANE), F32).at[0, :N_GROUPS].set(b_route_g[l])
                   .at[0, N_GROUPS:N_GROUPS + N_EXPERTS].set(b_route_e[l]))
        xall, route, counts = _merge(xall, p, ya, yb, hf, hb, xconv, g_mlstm_norm[l][None], skip_m[l][None],
                                         w_branch[l].astype(MXU_DT), w_out[l].astype(MXU_DT), mod,
                                         g_norm2[l][None], w_route, b_route, tiles)

        block_e, n_valid, fill_blocks, n_live, pos_tiles, n_slots = _dispatch_plan(route, counts[0], tiles.count)
        xs = _moe_dispatch(xall, g_norm2[l][None], mod, fill_blocks, n_live, pos_tiles, n_slots, tiles)
        y_sorted = _moe_experts(xs, block_e, n_valid, w_eg_b, w_eu_b, w_ed_b, l)
        xall = _combine(xall, route, mod, g_final[None], y_sorted, pos_tiles, tiles, last)

    return xall.reshape(n_batch, t_lat, d)
```

```python
import functools
import math
from typing import NamedTuple

import jax
import jax.numpy as jnp
from jax import lax
from jax.experimental import pallas as pl
from jax.experimental.pallas import tpu as pltpu

F32 = jnp.float32
MXU_DT = jnp.bfloat16
ACT_DT = jnp.bfloat16
HIGHEST = lax.Precision.HIGHEST

D_MODEL = 1024
N_MOD = 6
EPS = 1e-6
GRID_W = 64
H_A, DK = 4, 64
DV = 2 * DK
ROT_AX = DK // 2
ROPE_BASE = 10000.0
W_B, G_B, CHUNK_B = 512, 4, 128
H_C, DH_C, CONV_K, MLSTM_CHUNK = 4, 128, 3, 128
W_C = H_C * DH_C
N_GROUPS, EXP_PER_GROUP, TOP_K, D_EXPERT = 4, 8, 2, 512
N_EXPERTS = N_GROUPS * EXP_PER_GROUP
W_BR = 512

GATE_W = 3 * D_MODEL
P_GATE = 0
P_K = GATE_W
P_V = P_K + 512
P_XM = P_V + 512
P_Q = P_XM + 512
P_UV = P_Q + 512
P_Z = P_UV + 2 * W_B
IN_COLS = P_Z + W_C
ORIG_GATE0 = IN_COLS - GATE_W

LANE = 128
ROW_TILE = 256
PROJ_CHUNK = 512
MOE_BLOCK = 256
HALO = 16
DMA_UNROLL = 8
ATTN_KV_CHUNK = 256
ATTN_Q_ROWS = 256
LOG2_E = math.log2(math.e)
VMEM_LIMIT = 56 * 1024 * 1024
NEG_BIG = -1e30


class _Tiles(NamedTuple):
    n_batch: int
    tps: int
    skip_ctx: bool

    @property
    def per_seq(self):
        return self.tps - 1 if self.skip_ctx else self.tps

    @property
    def count(self):
        return self.n_batch * self.per_seq

    def full(self, i):
        if not self.skip_ctx:
            return i
        return (i // self.per_seq) * self.tps + 1 + i % self.per_seq

    def mod_row(self, i):
        if self.skip_ctx:
            return i // self.per_seq
        return jnp.where(i % self.tps == 0, self.n_batch, i // self.tps)


def _cparams(*sem):
    return pltpu.CompilerParams(dimension_semantics=sem, vmem_limit_bytes=VMEM_LIMIT)


def _rms(x):
    return x * lax.rsqrt(jnp.mean(x * x, axis=-1, keepdims=True) + EPS)


def _sigmoid(x):
    return 0.5 * jnp.tanh(0.5 * x) + 0.5


def _split_bf16(x):
    hi = x.astype(jnp.bfloat16)
    return hi, (x - hi.astype(F32)).astype(jnp.bfloat16)


def _mod_kernel(c_ref, w_ref, b_ref, o_ref):
    c = c_ref[...]
    s = c * jax.nn.sigmoid(c)
    o_ref[...] = jnp.dot(s, w_ref[...], preferred_element_type=F32, precision=HIGHEST) + b_ref[...]


def _modulation(c_all, w_mod, b_mod):
    n_layer = w_mod.shape[0]
    mb, d = c_all.shape
    tn = 1024
    return pl.pallas_call(
        _mod_kernel,
        grid=(n_layer, N_MOD * d // tn),
        in_specs=[
            pl.BlockSpec((mb, d), lambda l, j: (0, 0)),
            pl.BlockSpec((None, d, tn), lambda l, j: (l, 0, j)),
            pl.BlockSpec((None, 1, tn), lambda l, j: (l, 0, j)),
        ],
        out_specs=pl.BlockSpec((None, mb, tn), lambda l, j: (l, 0, j)),
        out_shape=jax.ShapeDtypeStruct((n_layer, mb, N_MOD * d), F32),
        compiler_params=_cparams("parallel", "parallel"),
        name="modulation",
    )(c_all, w_mod, b_mod.reshape(n_layer, 1, N_MOD * d))


def _rope(acc, cos, sin_signed):
    w = acc.shape[1]
    lane = lax.broadcasted_iota(jnp.int32, acc.shape, 1)
    partner = jnp.where((lane & 16) == 0, pltpu.roll(acc, w - 16, 1), pltpu.roll(acc, 16, 1))
    reps = w // cos.shape[1]
    return acc * jnp.tile(cos, (1, reps)) + partner * jnp.tile(sin_signed, (1, reps))


def _inproj_kernel(x_ref, g_ref, mod_ref, cos_ref, sin_ref, w_ref, o_ref, kt_ref):
    x = x_ref[...]
    h = (_rms(x) * g_ref[...] * (1.0 + mod_ref[1:2, :]) + mod_ref[0:1, :]).astype(MXU_DT)
    k_chunk = P_K // PROJ_CHUNK
    q_chunk = P_Q // PROJ_CHUNK
    for c in range(IN_COLS // PROJ_CHUNK):
        cols = slice(c * PROJ_CHUNK, (c + 1) * PROJ_CHUNK)
        w0 = (ORIG_GATE0 + c * PROJ_CHUNK) % IN_COLS
        acc = jnp.dot(h, w_ref[:, w0:w0 + PROJ_CHUNK], preferred_element_type=F32)
        if c == k_chunk:
            acc = _rope(acc, cos_ref[...], sin_ref[...])
            kt_ref[...] = acc.T.astype(kt_ref.dtype)
        elif c == q_chunk:
            acc = _rope(acc, cos_ref[...], sin_ref[...]) * (DK ** -0.5 * LOG2_E)
        o_ref[:, cols] = acc.astype(o_ref.dtype)


def _inproj(xall, g1, mod, cos, sin_signed, w_in, layer, tiles):
    r, d = xall.shape
    tm = ROW_TILE
    tps = tiles.tps
    assert ORIG_GATE0 % PROJ_CHUNK == 0 and not tiles.skip_ctx

    def mod_idx(i):
        return (tiles.mod_row(i), 0, 0)

    return pl.pallas_call(
        _inproj_kernel,
        grid=(r // tm,),
        in_specs=[
            pl.BlockSpec((tm, d), lambda i: (i, 0)),
            pl.BlockSpec((1, d), lambda i: (0, 0)),
            pl.BlockSpec((None, N_MOD, d), mod_idx),
            pl.BlockSpec((tm, LANE), lambda i: (i % tps, 0)),
            pl.BlockSpec((tm, LANE), lambda i: (i % tps, 0)),
            pl.BlockSpec((None, d, IN_COLS), lambda i: (layer, 0, 0)),
        ],
        out_specs=[pl.BlockSpec((tm, IN_COLS), lambda i: (i, 0)),
                   pl.BlockSpec((PROJ_CHUNK, tm), lambda i: (0, i))],
        out_shape=[jax.ShapeDtypeStruct((r, IN_COLS), ACT_DT), jax.ShapeDtypeStruct((PROJ_CHUNK, r), ACT_DT)],
        compiler_params=_cparams("parallel"),
        name="inproj",
    )(xall, g1, mod, cos, sin_signed, w_in)


def _attn_kernel(dl_ref, g_ref, q_ref, kt_ref, v_ref, o_ref, vext_ref, s_ref, *, lam_init, ctx_rows, has_ctx):
    @pl.when(pl.program_id(2) == 0)
    def _():
        one_hot = lax.broadcasted_iota(jnp.int32, v_ref.shape, 1) == 0
        vext_ref[:, :DV] = v_ref[...]
        vext_ref[:, DV:] = jnp.where(one_hot, 1.0, 0.0).astype(vext_ref.dtype)

    dl = dl_ref[...]
    lam = (jnp.exp(jnp.sum(dl[0:1] * dl[1:2], keepdims=True))
           - jnp.exp(jnp.sum(dl[2:3] * dl[3:4], keepdims=True)) + lam_init)
    kc = ATTN_KV_CHUNK

    def attend(kv_rows):
        for r0 in range(0, q_ref.shape[0], ATTN_Q_ROWS):
            attend_rows(kv_rows, slice(r0, r0 + ATTN_Q_ROWS))

    def attend_rows(kv_rows, rows):
        q = q_ref[rows, :]
        lane = lax.broadcasted_iota(jnp.int32, q.shape, 1)
        zero = jnp.zeros_like(q)
        qms = (jnp.where(lane < DK, q, zero), jnp.where(lane >= DK, q, zero))
        chunks = [slice(c * kc, (c + 1) * kc) for c in range(kv_rows // kc)]

        def scores(m, cols):
            return jnp.dot(qms[m], kt_ref[:, cols], preferred_element_type=F32)

        mxs = []
        for m in range(2):
            mrun = None
            for cols in chunks:
                s = scores(m, cols)
                s_ref[m, rows, cols] = s
                for g in range(kc // LANE):
                    blk = s[:, g * LANE:(g + 1) * LANE]
                    mrun = blk if mrun is None else jnp.maximum(mrun, blk)
            mxs.append(jnp.max(mrun, axis=-1, keepdims=True))
        accs = [jnp.zeros((q.shape[0], 2 * DV), F32)] * 2
        for cols in chunks:
            for m in range(2):
                p = jnp.exp2(s_ref[m, rows, cols] - mxs[m])
                accs[m] = accs[m] + jnp.dot(p.astype(MXU_DT), vext_ref[cols, :], preferred_element_type=F32)
        o = (accs[0][:, :DV] * (1.0 / accs[0][:, DV:DV + 1])
             - accs[1][:, :DV] * (lam / accs[1][:, DV:DV + 1]))
        o_ref[rows, :] = (_rms(o) * g_ref[...] * (1.0 - lam_init)).astype(o_ref.dtype)

    if not has_ctx:
        attend(kt_ref.shape[1])
        return

    is_ctx = pl.program_id(2) == 0

    @pl.when(is_ctx)
    def _():
        attend(ctx_rows)

    @pl.when(jnp.logical_not(is_ctx))
    def _():
        attend(kt_ref.shape[1])


def _attention(p, k_t, diff_lambda, g_sub, lam_init, tiles, ctx_rows):
    tq = ROW_TILE
    tps = tiles.tps
    seq = tps * tq
    q_off = tps - tiles.per_seq
    kern = functools.partial(_attn_kernel, lam_init=lam_init, ctx_rows=ctx_rows, has_ctx=not tiles.skip_ctx)
    return pl.pallas_call(
        kern,
        grid=(tiles.n_batch, H_A, tiles.per_seq),
        in_specs=[
            pl.BlockSpec((4, DK), lambda b, h, i: (0, 0)),
            pl.BlockSpec((1, DV), lambda b, h, i: (0, 0)),
            pl.BlockSpec((tq, LANE), lambda b, h, i: (b * tps + q_off + i, P_Q // LANE + h)),
            pl.BlockSpec((LANE, seq), lambda b, h, i: (h, b)),
            pl.BlockSpec((seq, LANE), lambda b, h, i: (b, P_V // LANE + h)),
        ],
        out_specs=pl.BlockSpec((tq, LANE), lambda b, h, i: (b * tiles.per_seq + i, h)),
        out_shape=jax.ShapeDtypeStruct((tiles.count * tq, H_A * DV), ACT_DT),
        scratch_shapes=[pltpu.VMEM((seq, 2 * DV), MXU_DT), pltpu.VMEM((2, tq, seq), F32)],
        compiler_params=_cparams("parallel", "parallel", "arbitrary"),
        name="diff_attention",
    )(diff_lambda, g_sub, p, k_t, p)


def _gmlp_kernel(uv_ref, gv_ref, ws_ref, bs_ref, o_ref):
    uv = jax.nn.gelu(uv_ref[...].astype(F32))
    u = uv[:, :W_B]
    v = (_rms(uv[:, W_B:]) * gv_ref[...]).astype(MXU_DT)
    gw = W_B // G_B
    for c in range(uv.shape[0] // CHUNK_B):
        rows = slice(c * CHUNK_B, (c + 1) * CHUNK_B)
        for g in range(G_B):
            cols = slice(g * gw, (g + 1) * gw)
            mixed = jnp.dot(ws_ref[g], v[rows, cols], preferred_element_type=F32) + bs_ref[:, cols]
            o_ref[rows, cols] = (u[rows, cols] * mixed).astype(o_ref.dtype)


def _gmlp(p, g_v, w_s, b_full, tiles):
    tm = ROW_TILE
    r = tiles.count * tm
    return pl.pallas_call(
        _gmlp_kernel,
        grid=(tiles.count,),
        in_specs=[
            pl.BlockSpec((tm, 2 * W_B), lambda i: (tiles.full(i), P_UV // (2 * W_B))),
            pl.BlockSpec((1, W_B), lambda i: (0, 0)),
            pl.BlockSpec((G_B, CHUNK_B, CHUNK_B), lambda i: (0, 0, 0)),
            pl.BlockSpec((CHUNK_B, W_B), lambda i: (0, 0)),
        ],
        out_specs=pl.BlockSpec((tm, W_B), lambda i: (i, 0)),
        out_shape=jax.ShapeDtypeStruct((r, W_B), ACT_DT),
        compiler_params=_cparams("parallel"),
        name="gmlp",
    )(p, g_v, w_s, b_full)


def _mfeat_kernel(x_ref, prev_ref, next_ref, wc_ref, bc_ref, wqkv_ref, wif_ref, bif_ref,
                  xc_ref, q_ref, k_ref, v_ref, g_ref, *, tps):
    tm = x_ref.shape[0]
    j = pl.program_id(0) % tps
    seg_start = jnp.logical_or(j == 0, j == 1)
    seg_end = jnp.logical_or(j == 0, j == tps - 1)
    x = x_ref[...].astype(F32)
    prow = jnp.where(seg_start, 0.0, prev_ref[...].astype(F32)[HALO - 1:HALO, :])
    nrow = jnp.where(seg_end, 0.0, next_ref[...].astype(F32)[0:1, :])
    rid = lax.broadcasted_iota(jnp.int32, x.shape, 0)
    xp = jnp.where(rid == 0, prow, pltpu.roll(x, 1, 0))
    xn = jnp.where(rid == tm - 1, nrow, pltpu.roll(x, tm - 1, 0))
    xc = wc_ref[0:1, :] * xp + wc_ref[1:2, :] * x + wc_ref[2:3, :] * xn + bc_ref[...]
    xc = xc * _sigmoid(xc)
    xc_ref[...] = xc.astype(xc_ref.dtype)
    xcb = xc.astype(MXU_DT)
    xb = x_ref[...]
    qs, ks, vs = [], [], []
    for h in range(H_C):
        cols = slice(h * DH_C, (h + 1) * DH_C)
        qs.append(jnp.dot(xcb[:, cols], wqkv_ref[0, h], preferred_element_type=F32))
        ks.append(jnp.dot(xcb[:, cols], wqkv_ref[1, h], preferred_element_type=F32))
        vs.append(jnp.dot(xb[:, cols], wqkv_ref[2, h], preferred_element_type=F32))
    q = jnp.concatenate(qs, axis=1)
    k = jnp.concatenate(ks, axis=1)
    v = jnp.concatenate(vs, axis=1)
    qkv = jnp.concatenate([q, k, v], axis=1).astype(MXU_DT)
    gates = jnp.dot(qkv, wif_ref[...], preferred_element_type=F32) + bif_ref[...]
    g_ref[...] = gates.T[0:g_ref.shape[0], :]
    q_ref[...] = q.T.astype(q_ref.dtype)
    k_ref[...] = (k * (DH_C ** -0.5)).astype(k_ref.dtype)
    v_ref[...] = v.T.astype(v_ref.dtype)


def _mlstm_features(p, w_conv, b_conv, w_qkv, w_if_pad, b_if_pad, tps):
    r = p.shape[0]
    tm = ROW_TILE
    hpt = tm // HALO
    xm_blk = P_XM // W_C
    last_halo = r // HALO - 1
    kern = functools.partial(_mfeat_kernel, tps=tps)
    act = jax.ShapeDtypeStruct((r, W_C), ACT_DT)
    act_t = jax.ShapeDtypeStruct((W_C, r), ACT_DT)
    row_spec = pl.BlockSpec((tm, W_C), lambda i: (i, 0))
    col_spec = pl.BlockSpec((W_C, tm), lambda i: (0, i))
    return pl.pallas_call(
        kern,
        grid=(r // tm,),
        in_specs=[
            pl.BlockSpec((tm, W_C), lambda i: (i, xm_blk)),
            pl.BlockSpec((HALO, W_C), lambda i: (jnp.maximum(i * hpt - 1, 0), xm_blk)),
            pl.BlockSpec((HALO, W_C), lambda i: (jnp.minimum((i + 1) * hpt, last_halo), xm_blk)),
            pl.BlockSpec((CONV_K, W_C), lambda i: (0, 0)),
            pl.BlockSpec((1, W_C), lambda i: (0, 0)),
            pl.BlockSpec((3, H_C, DH_C, DH_C), lambda i: (0, 0, 0, 0)),
            pl.BlockSpec((3 * W_C, LANE), lambda i: (0, 0)),
            pl.BlockSpec((1, LANE), lambda i: (0, 0)),
        ],
        out_specs=[row_spec, col_spec, row_spec, col_spec,
                   pl.BlockSpec((2 * GATE_ROWS, tm), lambda i: (0, i))],
        out_shape=[act, act_t, act, act_t, jax.ShapeDtypeStruct((2 * GATE_ROWS, r), F32)],
        compiler_params=_cparams("parallel"),
        name="mlstm_features",
    )(p, p, p, w_conv, b_conv, w_qkv, w_if_pad, b_if_pad)


GATE_ROWS = 2 * H_C
GATE_PACK = ("r", "cmx", "bcum")


def _gate_prep_kernel(g_ref, pack_ref, rt_ref):
    seg = MLSTM_CHUNK
    gt_fwd = g_ref[0:GATE_ROWS, :]
    gt_bwd = g_ref[GATE_ROWS:2 * GATE_ROWS, :]
    n = gt_fwd.shape[1]
    lane = lax.broadcasted_iota(jnp.int32, gt_fwd.shape, 1)
    in_seg = lane & (seg - 1)

    def scan_both(xf, xb, op):
        sh = 1
        while sh < seg:
            xf = jnp.where(in_seg >= sh, op(xf, pltpu.roll(xf, sh, 1)), xf)
            xb = jnp.where(in_seg < seg - sh, op(xb, pltpu.roll(xb, n - sh, 1)), xb)
            sh *= 2
        return xf, xb

    bcum_f, bcum_b = scan_both(jax.nn.log_sigmoid(pltpu.roll(gt_fwd, H_C, 0)),
                               jax.nn.log_sigmoid(pltpu.roll(gt_bwd, H_C, 0)), jnp.add)
    r_f, r_b = gt_fwd - bcum_f, gt_bwd - bcum_b
    cmx_f, cmx_b = scan_both(r_f, r_b, jnp.maximum)
    pack_ref[...] = jnp.concatenate([r_f, cmx_f, bcum_f, r_b, cmx_b, bcum_b], axis=0)
    rt_ref[...] = jnp.concatenate([r_f, r_b, jnp.zeros((LANE - 2 * GATE_ROWS, n), F32)], axis=0).T


def _mlstm_gate_prep(gates_t, n_batch, seq):
    rows = 2 * len(GATE_PACK) * GATE_ROWS
    r = gates_t.shape[1]
    return pl.pallas_call(
        _gate_prep_kernel,
        grid=(n_batch,),
        in_specs=[pl.BlockSpec((2 * GATE_ROWS, seq), lambda b: (0, b))],
        out_specs=[pl.BlockSpec((rows, seq), lambda b: (0, b)), pl.BlockSpec((seq, LANE), lambda b: (b, 0))],
        out_shape=[jax.ShapeDtypeStruct((rows, r), F32), jax.ShapeDtypeStruct((r, LANE), F32)],
        compiler_params=_cparams("parallel"),
        name="mlstm_gate_prep",
    )(gates_t)


def _mlstm_gates(pack, m_ref, reverse):
    ln = pack.shape[1]
    part = {name: pack[i * GATE_ROWS:(i + 1) * GATE_ROWS] for i, name in enumerate(GATE_PACK)}
    end = slice(0, 1) if reverse else slice(ln - 1, ln)
    g = part["bcum"][:, end]
    r_max = part["cmx"][:, end]
    m_loc = g + r_max
    w_end = jnp.exp(part["r"] - r_max)
    m_old = m_ref[:, 0:1]
    m_new = jnp.maximum(g + m_old, m_loc)
    a_old = jnp.exp(g + m_old - m_new)
    a_new = jnp.exp(m_loc - m_new)
    mx = jnp.maximum(m_old, part["cmx"])
    w_inter = jnp.exp(m_old - mx)
    e_neg = jnp.exp(-(part["bcum"] + mx))
    s_id = lax.broadcasted_iota(jnp.int32, (ln, ln), 0)
    t_id = lax.broadcasted_iota(jnp.int32, (ln, ln), 1)
    keep = (s_id >= t_id) if reverse else (s_id <= t_id)
    return dict(mx=mx, w_inter=w_inter, e_neg=e_neg, w_end=w_end, a_old=a_old, a_new=a_new,
                m_new=m_new, keep=keep)


def _mscan_kernel(gf_ref, rf_ref, qf_ref, kf_ref, vf_ref, gb_ref, rb_ref, qb_ref, kb_ref, vb_ref,
                  hf_ref, hb_ref, cf_ref, mf_ref, cb_ref, mb_ref):
    @pl.when(pl.program_id(1) == 0)
    def _():
        cf_ref[...] = jnp.zeros_like(cf_ref)
        cb_ref[...] = jnp.zeros_like(cb_ref)
        mf_ref[...] = jnp.zeros_like(mf_ref)
        mb_ref[...] = jnp.zeros_like(mb_ref)

    ln = qf_ref.shape[1]
    pr = len(GATE_PACK) * GATE_ROWS
    dirs = ((_mlstm_gates(gf_ref[0:pr, :], mf_ref, False), qf_ref, kf_ref, vf_ref, cf_ref, hf_ref, rf_ref),
            (_mlstm_gates(gb_ref[pr:2 * pr, :], mb_ref, True), qb_ref, kb_ref, vb_ref, cb_ref, hb_ref, rb_ref))
    pairs = [(d, h) for d in range(2) for h in range(H_C)]
    ones_row = jnp.where(lax.broadcasted_iota(jnp.int32, (DH_C, ln), 0) == 0, 1.0, 0.0).astype(MXU_DT)
    head = lambda h: slice(h * DH_C, (h + 1) * DH_C)
    row = lambda h: slice(h, h + 1)

    ks = [dirs[d][2][:, head(h)] for d, h in pairs]
    qts = [dirs[d][1][head(h), :] for d, h in pairs]
    vexts = [jnp.concatenate([dirs[d][3][head(h), :], ones_row], axis=0) for d, h in pairs]
    cexts = [dirs[d][4][h] for d, h in pairs]
    qks = [jnp.dot(k, q_t, preferred_element_type=F32) for k, q_t in zip(ks, qts)]
    rhss = []
    for i, (d, h) in enumerate(pairs):
        g = dirs[d][0]
        r_col = dirs[d][6][:, d * GATE_ROWS + h:d * GATE_ROWS + h + 1]
        d_t = jnp.exp(jnp.where(g["keep"], r_col - g["mx"][row(h), :], NEG_BIG))
        wq = (qts[i].astype(F32) * g["w_inter"][row(h), :]).astype(MXU_DT)
        rhss.append(jnp.concatenate([(qks[i] * d_t).astype(MXU_DT), wq], axis=0))
    ress = [jnp.dot(jnp.concatenate([vexts[i], cexts[i].astype(MXU_DT)], axis=1), rhss[i],
                    preferred_element_type=F32) for i in range(len(pairs))]
    for i, (d, h) in enumerate(pairs):
        g = dirs[d][0]
        den = ress[i][DH_C:DH_C + 1, :]
        hh = ress[i][:DH_C, :] / jnp.maximum(jnp.abs(den), g["e_neg"][row(h), :])
        dirs[d][5][head(h), :] = hh.astype(dirs[d][5].dtype)
    clocs = []
    for i, (d, h) in enumerate(pairs):
        vw_t = (vexts[i].astype(F32) * dirs[d][0]["w_end"][row(h), :]).astype(MXU_DT)
        clocs.append(jnp.dot(vw_t, ks[i], preferred_element_type=F32))
    for i, (d, h) in enumerate(pairs):
        g = dirs[d][0]
        dirs[d][4][h] = g["a_old"][row(h), :] * cexts[i] + g["a_new"][row(h), :] * clocs[i]
    mf_ref[...] = jnp.broadcast_to(dirs[0][0]["m_new"], mf_ref.shape)
    mb_ref[...] = jnp.broadcast_to(dirs[1][0]["m_new"], mb_ref.shape)


def _mlstm_scan(q_t, k, v_t, gate_pack, r_t, n_batch, seq, ctx_rows):
    ln = MLSTM_CHUNK
    nck = seq // ln
    ncc = ctx_rows // ln

    def fwd_chunk(b, s):
        return b * nck + s

    def bwd_chunk(b, s):
        return b * nck + jnp.where(s < ncc, ncc - 1 - s, nck - 1 + ncc - s)

    def specs(chunk):
        rows = pl.BlockSpec((ln, W_C), lambda b, s: (chunk(b, s), 0))
        cols = pl.BlockSpec((W_C, ln), lambda b, s: (0, chunk(b, s)))
        gate = pl.BlockSpec((gate_pack.shape[0], ln), lambda b, s: (0, chunk(b, s)))
        r_cols = pl.BlockSpec((ln, LANE), lambda b, s: (chunk(b, s), 0))
        return [gate, r_cols, cols, rows, cols], cols

    in_f, out_f = specs(fwd_chunk)
    in_b, out_b = specs(bwd_chunk)
    out = jax.ShapeDtypeStruct(q_t.shape, ACT_DT)
    return pl.pallas_call(
        _mscan_kernel,
        grid=(n_batch, nck),
        in_specs=in_f + in_b,
        out_specs=[out_f, out_b],
        out_shape=[out, out],
        scratch_shapes=[
            pltpu.VMEM((H_C, 2 * DH_C, DH_C), F32), pltpu.VMEM((8, LANE), F32),
            pltpu.VMEM((H_C, 2 * DH_C, DH_C), F32), pltpu.VMEM((8, LANE), F32),
        ],
        compiler_params=_cparams("parallel", "arbitrary"),
        name="mlstm_scan",
    )(gate_pack, r_t, q_t, k, v_t, gate_pack, r_t, q_t, k, v_t)


def _route(logits, carry):
    lane = lax.broadcasted_iota(jnp.int32, logits.shape, 1).astype(F32)
    big = float(4 * LANE)
    gl = jnp.where(lane < N_GROUPS, logits, NEG_BIG)
    gmax = jnp.max(gl, axis=-1, keepdims=True)
    g_star = jnp.min(jnp.where(gl == gmax, lane, big), axis=-1, keepdims=True)
    p_g = 1.0 / jnp.sum(jnp.exp(gl - gmax), axis=-1, keepdims=True)
    e_lo = g_star * EXP_PER_GROUP + N_GROUPS
    in_group = jnp.logical_and(lane >= e_lo, lane < e_lo + EXP_PER_GROUP)
    el = jnp.where(in_group, logits, NEG_BIG)
    v1 = jnp.max(el, axis=-1, keepdims=True)
    i1 = jnp.min(jnp.where(el == v1, lane, big), axis=-1, keepdims=True)
    el2 = jnp.where(lane == i1, NEG_BIG, el)
    v2 = jnp.max(el2, axis=-1, keepdims=True)
    i2 = jnp.min(jnp.where(el2 == v2, lane, big), axis=-1, keepdims=True)
    e21 = jnp.exp(v2 - v1)
    w1 = p_g / (1.0 + e21)
    w2 = p_g * e21 / (1.0 + e21)
    e1 = i1 - N_GROUPS
    e2 = i2 - N_GROUPS
    oh1 = lane == e1
    oh2 = lane == e2
    both = jnp.where(jnp.logical_or(oh1, oh2), 1.0, 0.0)
    tm = logits.shape[0]
    lower = (lax.broadcasted_iota(jnp.int32, (tm, tm), 1) < lax.broadcasted_iota(jnp.int32, (tm, tm), 0))
    before = jnp.dot(jnp.where(lower, 1.0, 0.0).astype(MXU_DT), both.astype(MXU_DT),
                     preferred_element_type=F32) + carry
    rank1 = jnp.sum(jnp.where(oh1, before, 0.0), axis=-1, keepdims=True)
    rank2 = jnp.sum(jnp.where(oh2, before, 0.0), axis=-1, keepdims=True)
    out = jnp.where(lane == 0, e1, 0.0)
    out = jnp.where(lane == 1, e2, out)
    out = jnp.where(lane == 2, w1, out)
    out = jnp.where(lane == 3, w2, out)
    out = jnp.where(lane == 4, rank1, out)
    out = jnp.where(lane == 5, rank2, out)
    return out, carry + jnp.sum(both, axis=0, keepdims=True)


def _merge_kernel(x_ref, gate_ref, z_ref, ya_ref, yb_ref, hf_ref, hb_ref, xc_ref, gm_ref, sk_ref,
                  wbr_ref, wo_ref, mod_ref, g2_ref, wr_ref, br_ref, xo_ref, rt_ref, cnt_ref, carry_ref):
    @pl.when(pl.program_id(0) == 0)
    def _():
        carry_ref[...] = jnp.zeros_like(carry_ref)

    hs_t = hf_ref[...].astype(F32) + hb_ref[...].astype(F32)
    hn_t = []
    for h in range(H_C):
        part = hs_t[h * DH_C:(h + 1) * DH_C, :]
        hn_t.append(part * lax.rsqrt(jnp.mean(part * part, axis=0, keepdims=True) + EPS))
    hn = jnp.concatenate(hn_t, axis=0).T * gm_ref[...]
    yc = (hn + sk_ref[...] * xc_ref[...].astype(F32)) * _sigmoid(z_ref[...].astype(F32))
    ys = (ya_ref[...], yb_ref[...], yc.astype(MXU_DT))
    merged = None
    for i in range(3):
        gate = _sigmoid(gate_ref[:, i * D_MODEL:(i + 1) * D_MODEL].astype(MXU_DT))
        term = gate * jnp.dot(ys[i], wbr_ref[i], preferred_element_type=F32).astype(MXU_DT)
        merged = term if merged is None else merged + term
    y = jnp.dot(merged, wo_ref[...], preferred_element_type=F32)
    xn = x_ref[...] + mod_ref[2:3, :] * y
    xo_ref[...] = xn
    h2 = _rms(xn) * g2_ref[...] * (1.0 + mod_ref[4:5, :]) + mod_ref[3:4, :]
    h_hi, h_lo = _split_bf16(h2)
    logits = (jnp.dot(h_hi, wr_ref[0], preferred_element_type=F32)
              + jnp.dot(h_lo, wr_ref[0], preferred_element_type=F32)
              + jnp.dot(h_hi, wr_ref[1], preferred_element_type=F32)) + br_ref[...]
    route, carry = _route(logits, carry_ref[0:1, :])
    rt_ref[...] = route
    carry_ref[0:1, :] = carry
    cnt_ref[...] = jnp.broadcast_to(carry, cnt_ref.shape)


def _merge(xall, p, ya, yb, hf, hb, xconv, g_m, skip, w_br, w_o, mod, g2, w_route, b_route, tiles):
    d = xall.shape[1]
    tm = ROW_TILE
    r = tiles.count * tm

    def mod_idx(i):
        return (tiles.mod_row(i), 0, 0)

    row = lambda w: pl.BlockSpec((tm, w), lambda i: (i, 0))
    frow = lambda w: pl.BlockSpec((tm, w), lambda i: (tiles.full(i), 0))
    fcol = lambda w: pl.BlockSpec((w, tm), lambda i: (0, tiles.full(i)))
    full = lambda shape: pl.BlockSpec(shape, lambda i: (0,) * len(shape))
    return pl.pallas_call(
        _merge_kernel,
        grid=(tiles.count,),
        in_specs=[
            frow(d),
            pl.BlockSpec((tm, GATE_W), lambda i: (tiles.full(i), P_GATE // GATE_W)),
            pl.BlockSpec((tm, W_C), lambda i: (tiles.full(i), P_Z // W_C)),
            row(W_BR), row(W_BR), fcol(W_C), fcol(W_C), frow(W_C),
            full((1, W_C)), full((1, W_C)),
            full((3, W_BR, d)), full((d, d)),
            pl.BlockSpec((None, N_MOD, d), mod_idx),
            full((1, d)), full((2, d, LANE)), full((1, LANE)),
        ],
        out_specs=[row(d), row(LANE), full((8, LANE))],
        out_shape=[jax.ShapeDtypeStruct((r, d), F32), jax.ShapeDtypeStruct((r, LANE), F32),
                   jax.ShapeDtypeStruct((8, LANE), F32)],
        scratch_shapes=[pltpu.VMEM((8, LANE), F32)],
        compiler_params=_cparams("arbitrary"),
        name="merge_route",
    )(xall, p, p, ya, yb, hf, hb, xconv, g_m, skip, w_br, w_o, mod, g2, w_route, b_route)


def _row_copy(src_hbm, src_row, dst_ref, dst_row, sem):
    return pltpu.make_async_copy(src_hbm.at[pl.ds(src_row, 1), :], dst_ref.at[pl.ds(dst_row, 1), :], sem)


def _dispatch_kernel(fill_ref, nlive_ref, pos_ref, x_ref, g2_ref, mod_ref, xs_hbm, hbuf, zbuf, zsem, sem):
    i = pl.program_id(0)
    n = pl.num_programs(0)
    tm = x_ref.shape[0]
    slot = i % 2
    bm = zbuf.shape[0]
    n_blocks = xs_hbm.shape[0] // bm

    @pl.when(i == 0)
    def _():
        zbuf[...] = jnp.zeros_like(zbuf)

        def fill_block(b):
            return pltpu.make_async_copy(zbuf, xs_hbm.at[pl.ds(b * bm, bm), :], zsem)

        def fill_last(k, carry):
            fill_block(fill_ref[k]).start()
            return carry

        def fill_unused(b, carry):
            fill_block(b).start()
            return carry

        def wait_fill(k, carry):
            fill_block(0).wait()
            return carry

        n_live = nlive_ref[0]
        n_used = nlive_ref[1]
        lax.fori_loop(0, n_used, fill_last, 0)
        lax.fori_loop(n_live, n_blocks, fill_unused, 0)
        lax.fori_loop(0, n_used + n_blocks - n_live, wait_fill, 0)

    hbuf[slot] = _rms(x_ref[...]) * g2_ref[...] * (1.0 + mod_ref[4:5, :]) + mod_ref[3:4, :]

    def body(c, carry):
        for u in range(DMA_UNROLL):
            r = c * DMA_UNROLL + u
            for j in range(TOP_K):
                pltpu.make_async_copy(hbuf.at[slot, pl.ds(r, 1), :],
                                      xs_hbm.at[pl.ds(pos_ref[0, j * tm + r], 1), :], sem.at[slot]).start()
        return carry

    lax.fori_loop(0, tm // DMA_UNROLL, body, 0)

    def wait_tile(s):
        pltpu.make_async_copy(hbuf.at[s], xs_hbm.at[pl.ds(0, tm), :], sem.at[s]).wait()
        pltpu.make_async_copy(hbuf.at[s], xs_hbm.at[pl.ds(0, tm), :], sem.at[s]).wait()

    @pl.when(i > 0)
    def _():
        wait_tile(1 - slot)

    @pl.when(i == n - 1)
    def _():
        wait_tile(slot)


def _moe_dispatch(xall, g2, mod, fill_blocks, n_live, pos_tiles, n_slots, tiles):
    r, d = xall.shape
    tm = ROW_TILE
    assert TOP_K == 2 and r == tiles.count * tm

    def mod_idx(i, fb, nl):
        return (tiles.mod_row(i), 0, 0)

    grid_spec = pltpu.PrefetchScalarGridSpec(
        num_scalar_prefetch=2,
        grid=(r // tm,),
        in_specs=[
            pl.BlockSpec((None, 1, TOP_K * tm), lambda i, fb, nl: (i, 0, 0), memory_space=pltpu.SMEM),
            pl.BlockSpec((tm, d), lambda i, fb, nl: (i, 0)),
            pl.BlockSpec((1, d), lambda i, fb, nl: (0, 0)),
            pl.BlockSpec((None, N_MOD, d), mod_idx),
        ],
        out_specs=pl.BlockSpec(memory_space=pl.ANY),
        scratch_shapes=[pltpu.VMEM((2, tm, d), F32), pltpu.VMEM((MOE_BLOCK, d), F32),
                        pltpu.SemaphoreType.DMA(()), pltpu.SemaphoreType.DMA((2,))],
    )
    return pl.pallas_call(
        _dispatch_kernel,
        grid_spec=grid_spec,
        out_shape=jax.ShapeDtypeStruct((n_slots, d), F32),
        compiler_params=_cparams("arbitrary"),
        name="moe_dispatch",
    )(fill_blocks, n_live, pos_tiles, xall, g2, mod)


def _moe_kernel(be_ref, nv_ref, x_ref, wg_ref, wu_ref, wd_ref, y_ref):
    n_valid = nv_ref[pl.program_id(0)]

    @pl.when(n_valid > 0)
    def _():
        x = x_ref[...].astype(MXU_DT)
        a = jnp.dot(x, wg_ref[...], preferred_element_type=F32)
        u = jnp.dot(x, wu_ref[...], preferred_element_type=F32)
        act = (a * _sigmoid(a) * u).astype(MXU_DT)
        y_ref[...] = jnp.dot(act, wd_ref[...], preferred_element_type=F32)

    @pl.when(n_valid <= 0)
    def _():
        y_ref[...] = jnp.zeros_like(y_ref)


def _moe_experts(xs, block_e, n_valid, w_gate, w_up, w_down, layer):
    n_slots, d = xs.shape
    bm = MOE_BLOCK
    grid_spec = pltpu.PrefetchScalarGridSpec(
        num_scalar_prefetch=2,
        grid=(n_slots // bm,),
        in_specs=[
            pl.BlockSpec((bm, d), lambda i, be, nv: (i, 0)),
            pl.BlockSpec((None, None, d, D_EXPERT), lambda i, be, nv: (layer, be[i], 0, 0)),
            pl.BlockSpec((None, None, d, D_EXPERT), lambda i, be, nv: (layer, be[i], 0, 0)),
            pl.BlockSpec((None, None, D_EXPERT, d), lambda i, be, nv: (layer, be[i], 0, 0)),
        ],
        out_specs=pl.BlockSpec((bm, d), lambda i, be, nv: (i, 0)),
    )
    return pl.pallas_call(
        _moe_kernel,
        grid_spec=grid_spec,
        out_shape=jax.ShapeDtypeStruct((n_slots, d), F32),
        compiler_params=_cparams("arbitrary"),
        name="moe_experts",
    )(block_e, n_valid, xs, w_gate, w_up, w_down)


def _combine_kernel(pos_ref, pos_next_ref, x_ref, rt_ref, mod_ref, gf_ref, yb_hbm, xo_ref, ybuf, sem, *, final):
    i = pl.program_id(0)
    n = pl.num_programs(0)
    tm = x_ref.shape[0]

    def start_gather(ids_ref, slot):
        def body(c, carry):
            for u in range(DMA_UNROLL):
                r = c * DMA_UNROLL + u
                _row_copy(yb_hbm, ids_ref[0, r], ybuf.at[slot], r, sem.at[slot]).start()
            return carry
        lax.fori_loop(0, TOP_K * tm // DMA_UNROLL, body, 0)

    @pl.when(i == 0)
    def _():
        start_gather(pos_ref, 0)

    @pl.when(i + 1 < n)
    def _():
        start_gather(pos_next_ref, (i + 1) % 2)

    slot = i % 2
    pltpu.make_async_copy(yb_hbm.at[pl.ds(0, TOP_K * tm), :], ybuf.at[slot], sem.at[slot]).wait()
    rt = rt_ref[...]
    f = rt[:, 2:3] * ybuf[slot, 0:tm, :] + rt[:, 3:4] * ybuf[slot, tm:2 * tm, :]
    xn = x_ref[...] + mod_ref[5:6, :] * f
    xo_ref[...] = _rms(xn) * gf_ref[...] if final else xn


def _combine(xall, route, mod, g_final, yb, pos, tiles, final):
    r, d = xall.shape
    tm = ROW_TILE
    nt = tiles.count
    last = nt - 1
    assert r == nt * tm

    def mod_idx(i):
        return (tiles.mod_row(i), 0, 0)

    row = lambda w: pl.BlockSpec((tm, w), lambda i: (i, 0))
    return pl.pallas_call(
        functools.partial(_combine_kernel, final=final),
        grid=(nt,),
        in_specs=[
            pl.BlockSpec((None, 1, TOP_K * tm), lambda i: (i, 0, 0), memory_space=pltpu.SMEM),
            pl.BlockSpec((None, 1, TOP_K * tm), lambda i: (jnp.minimum(i + 1, last), 0, 0),
                         memory_space=pltpu.SMEM),
            row(d), row(LANE),
            pl.BlockSpec((None, N_MOD, d), mod_idx),
            pl.BlockSpec((1, d), lambda i: (0, 0)),
            pl.BlockSpec(memory_space=pl.ANY),
        ],
        out_specs=row(d),
        out_shape=jax.ShapeDtypeStruct((r, d), F32),
        scratch_shapes=[pltpu.VMEM((2, TOP_K * tm, d), F32), pltpu.SemaphoreType.DMA((2,))],
        compiler_params=_cparams("arbitrary"),
        name="moe_combine",
    )(pos, pos, xall, route, mod, g_final, yb)


def _dispatch_plan(route, counts_f, n_tiles):
    n = route.shape[0]
    bm = MOE_BLOCK
    n_blocks = -(-(n * TOP_K + N_EXPERTS * (bm - 1)) // bm)
    counts = counts_f[:N_EXPERTS].astype(jnp.int32)
    padded = (counts + bm - 1) // bm * bm
    pend = jnp.cumsum(padded)
    seg_off = pend - padded
    eid = route[:, 0:TOP_K].astype(jnp.int32)
    rank = route[:, 4:4 + TOP_K].astype(jnp.int32)
    experts = jnp.arange(N_EXPERTS, dtype=jnp.int32)
    pos = jnp.sum(jnp.where(eid[..., None] == experts, seg_off, 0), axis=-1) + rank
    blk0 = jnp.arange(n_blocks, dtype=jnp.int32) * bm
    block_e = jnp.minimum(jnp.sum((blk0[:, None] >= pend[None, :]).astype(jnp.int32), axis=1), N_EXPERTS - 1)
    seg_end = seg_off + counts
    n_valid = jnp.clip(seg_end[block_e] - blk0, 0, bm).astype(jnp.int32)
    pos_tiles = pos.reshape(n_tiles, n // n_tiles, TOP_K).transpose(0, 2, 1).reshape(n_tiles, 1, -1)
    used = counts > 0
    last_blocks = jnp.sort(jnp.where(used, pend // bm - 1, n_blocks)).astype(jnp.int32)
    n_live = jnp.stack([pend[-1] // bm, jnp.sum(used)]).astype(jnp.int32)
    return block_e.astype(jnp.int32), n_valid, last_blocks, n_live, pos_tiles, n_blocks * bm


def _rope_tables(seq, ctx_rows):
    t = jnp.arange(seq - ctx_rows)
    inv = ROPE_BASE ** (-jnp.arange(0, ROT_AX, 2, dtype=F32) / ROT_AX)
    ang_r = (t // GRID_W).astype(F32)[:, None] * inv
    ang_c = (t % GRID_W).astype(F32)[:, None] * inv
    cos64 = jnp.concatenate([jnp.cos(ang_r)] * 2 + [jnp.cos(ang_c)] * 2, axis=1)
    sin64 = jnp.concatenate([-jnp.sin(ang_r), jnp.sin(ang_r), -jnp.sin(ang_c), jnp.sin(ang_c)], axis=1)
    cos = jnp.concatenate([jnp.ones((ctx_rows, DK), F32), cos64], axis=0)
    sin = jnp.concatenate([jnp.zeros((ctx_rows, DK), F32), sin64], axis=0)
    return jnp.tile(cos, (1, LANE // DK)), jnp.tile(sin, (1, LANE // DK))


def kernel(x, c, ctx, c_ctx, w_mod, b_mod, g_norm1, g_norm2, w_in, diff_lambda, g_diff_subln, w_gmlp_s, b_gmlp_s, g_gmlp_v, w_conv_m, b_conv_m, w_qkv_m, w_if_m, b_if_m, g_mlstm_norm, skip_m, w_branch, w_out, w_route_g, b_route_g, w_route_e, b_route_e, w_e_gate, w_e_up, w_e_down, g_final):
    n_batch, t_lat, d = x.shape
    ctx_rows = ctx.shape[1]
    depth = w_in.shape[0]
    seq = ctx_rows + t_lat
    assert d == D_MODEL and ctx_rows == ROW_TILE and t_lat % ROW_TILE == 0 and t_lat % GRID_W == 0
    tps = seq // ROW_TILE
    all_tiles = _Tiles(n_batch, tps, False)

    xall = jnp.concatenate([ctx, x], axis=1).reshape(n_batch * seq, d)
    mb = -(-(n_batch + 1) // 8) * 8
    c_all = jnp.zeros((mb, d), F32).at[:n_batch].set(c).at[n_batch].set(c_ctx)
    mod_all = _modulation(c_all, w_mod, b_mod)[:, :n_batch + 1].reshape(depth, n_batch + 1, N_MOD, d)
    cos, sin_signed = _rope_tables(seq, ctx_rows)
    w_in_b = w_in.astype(MXU_DT)
    w_eg_b, w_eu_b, w_ed_b = w_e_gate.astype(MXU_DT), w_e_up.astype(MXU_DT), w_e_down.astype(MXU_DT)

    for l in range(depth):
        lam_init = 0.8 - 0.6 * math.exp(-0.3 * l)
        mod = mod_all[l]
        last = l == depth - 1
        tiles = _Tiles(n_batch, tps, last)
        p, k_t = _inproj(xall, g_norm1[l][None], mod, cos, sin_signed, w_in_b, l, all_tiles)

        ya = _attention(p, k_t, diff_lambda[l], g_diff_subln[l][None], lam_init, tiles, ctx_rows)

        b_full = jnp.repeat(b_gmlp_s[l].T, W_B // G_B, axis=1)
        yb = _gmlp(p, g_gmlp_v[l][None], w_gmlp_s[l].astype(MXU_DT), b_full, tiles)

        w_if_pad = jnp.zeros((3 * W_C, LANE), F32).at[:, :4 * H_C].set(w_if_m[l]).astype(MXU_DT)
        b_if_pad = jnp.zeros((1, LANE), F32).at[0, :4 * H_C].set(b_if_m[l])
        xconv, q_m, k_m, v_m, gates_t = _mlstm_features(
            p, w_conv_m[l], b_conv_m[l][None], w_qkv_m[l].astype(MXU_DT), w_if_pad, b_if_pad, tps)
        gate_pack, r_t = _mlstm_gate_prep(gates_t, n_batch, seq)
        hf, hb = _mlstm_scan(q_m, k_m, v_m, gate_pack, r_t, n_batch, seq, ctx_rows)

        w_route = (jnp.zeros((d, LANE), F32).at[:, :N_GROUPS].set(w_route_g[l])
                   .at[:, N_GROUPS:N_GROUPS + N_EXPERTS].set(w_route_e[l]))
        w_route = jnp.stack(_split_bf16(w_route))
        b_route = (jnp.zeros((1, LANE), F32).at[0, :N_GROUPS].set(b_route_g[l])
                   .at[0, N_GROUPS:N_GROUPS + N_EXPERTS].set(b_route_e[l]))
        xall, route, counts = _merge(xall, p, ya, yb, hf, hb, xconv, g_mlstm_norm[l][None], skip_m[l][None],
                                         w_branch[l].astype(MXU_DT), w_out[l].astype(MXU_DT), mod,
                                         g_norm2[l][None], w_route, b_route, tiles)

        block_e, n_valid, fill_blocks, n_live, pos_tiles, n_slots = _dispatch_plan(route, counts[0], tiles.count)
        xs = _moe_dispatch(xall, g_norm2[l][None], mod, fill_blocks, n_live, pos_tiles, n_slots, tiles)
        y_sorted = _moe_experts(xs, block_e, n_valid, w_eg_b, w_eu_b, w_ed_b, l)
        xall = _combine(xall, route, mod, g_final[None], y_sorted, pos_tiles, tiles, last)

    return xall.reshape(n_batch, t_lat, d)
```

```python
import functools
import math
from typing import NamedTuple

import jax
import jax.numpy as jnp
from jax import lax
from jax.experimental import pallas as pl
from jax.experimental.pallas import tpu as pltpu

F32 = jnp.float32
MXU_DT = jnp.bfloat16
ACT_DT = jnp.bfloat16
HIGHEST = lax.Precision.HIGHEST

D_MODEL = 1024
N_MOD = 6
EPS = 1e-6
GRID_W = 64
H_A, DK = 4, 64
DV = 2 * DK
ROT_AX = DK // 2
ROPE_BASE = 10000.0
W_B, G_B, CHUNK_B = 512, 4, 128
H_C, DH_C, CONV_K, MLSTM_CHUNK = 4, 128, 3, 128
W_C = H_C * DH_C
N_GROUPS, EXP_PER_GROUP, TOP_K, D_EXPERT = 4, 8, 2, 512
N_EXPERTS = N_GROUPS * EXP_PER_GROUP
W_BR = 512

GATE_W = 3 * D_MODEL
P_GATE = 0
P_K = GATE_W
P_V = P_K + 512
P_XM = P_V + 512
P_Q = P_XM + 512
P_UV = P_Q + 512
P_Z = P_UV + 2 * W_B
IN_COLS = P_Z + W_C
ORIG_GATE0 = IN_COLS - GATE_W

LANE = 128
ROW_TILE = 256
PROJ_CHUNK = 512
MOE_BLOCK = 512
HALO = 16
DMA_UNROLL = 8
ATTN_KV_CHUNK = 256
LOG2_E = math.log2(math.e)
VMEM_LIMIT = 56 * 1024 * 1024
NEG_BIG = -1e30


class _Tiles(NamedTuple):
    n_batch: int
    tps: int
    skip_ctx: bool

    @property
    def per_seq(self):
        return self.tps - 1 if self.skip_ctx else self.tps

    @property
    def count(self):
        return self.n_batch * self.per_seq

    def full(self, i):
        if not self.skip_ctx:
            return i
        return (i // self.per_seq) * self.tps + 1 + i % self.per_seq

    def mod_row(self, i):
        if self.skip_ctx:
            return i // self.per_seq
        return jnp.where(i % self.tps == 0, self.n_batch, i // self.tps)


def _cparams(*sem):
    return pltpu.CompilerParams(dimension_semantics=sem, vmem_limit_bytes=VMEM_LIMIT)


def _rms(x):
    return x * lax.rsqrt(jnp.mean(x * x, axis=-1, keepdims=True) + EPS)


def _sigmoid(x):
    return 0.5 * jnp.tanh(0.5 * x) + 0.5


def _split_bf16(x):
    hi = x.astype(jnp.bfloat16)
    return hi, (x - hi.astype(F32)).astype(jnp.bfloat16)


def _mod_kernel(c_ref, w_ref, b_ref, o_ref):
    c = c_ref[...]
    s = c * jax.nn.sigmoid(c)
    o_ref[...] = jnp.dot(s, w_ref[...], preferred_element_type=F32, precision=HIGHEST) + b_ref[...]


def _modulation(c_all, w_mod, b_mod):
    n_layer = w_mod.shape[0]
    mb, d = c_all.shape
    tn = 1024
    return pl.pallas_call(
        _mod_kernel,
        grid=(n_layer, N_MOD * d // tn),
        in_specs=[
            pl.BlockSpec((mb, d), lambda l, j: (0, 0)),
            pl.BlockSpec((None, d, tn), lambda l, j: (l, 0, j)),
            pl.BlockSpec((None, 1, tn), lambda l, j: (l, 0, j)),
        ],
        out_specs=pl.BlockSpec((None, mb, tn), lambda l, j: (l, 0, j)),
        out_shape=jax.ShapeDtypeStruct((n_layer, mb, N_MOD * d), F32),
        compiler_params=_cparams("parallel", "parallel"),
        name="modulation",
    )(c_all, w_mod, b_mod.reshape(n_layer, 1, N_MOD * d))


def _rope(acc, cos, sin_signed):
    w = acc.shape[1]
    lane = lax.broadcasted_iota(jnp.int32, acc.shape, 1)
    partner = jnp.where((lane & 16) == 0, pltpu.roll(acc, w - 16, 1), pltpu.roll(acc, 16, 1))
    reps = w // cos.shape[1]
    return acc * jnp.tile(cos, (1, reps)) + partner * jnp.tile(sin_signed, (1, reps))


def _inproj_kernel(x_ref, g_ref, mod_ref, cos_ref, sin_ref, w_ref, o_ref, kt_ref):
    x = x_ref[...]
    h = (_rms(x) * g_ref[...] * (1.0 + mod_ref[1:2, :]) + mod_ref[0:1, :]).astype(MXU_DT)
    k_chunk = P_K // PROJ_CHUNK
    q_chunk = P_Q // PROJ_CHUNK
    for c in range(IN_COLS // PROJ_CHUNK):
        cols = slice(c * PROJ_CHUNK, (c + 1) * PROJ_CHUNK)
        w0 = (ORIG_GATE0 + c * PROJ_CHUNK) % IN_COLS
        acc = jnp.dot(h, w_ref[:, w0:w0 + PROJ_CHUNK], preferred_element_type=F32)
        if c == k_chunk:
            acc = _rope(acc, cos_ref[...], sin_ref[...])
            kt_ref[...] = acc.T.astype(kt_ref.dtype)
        elif c == q_chunk:
            acc = _rope(acc, cos_ref[...], sin_ref[...]) * (DK ** -0.5 * LOG2_E)
        o_ref[:, cols] = acc.astype(o_ref.dtype)


def _inproj(xall, g1, mod, cos, sin_signed, w_in, layer, tiles):
    r, d = xall.shape
    tm = ROW_TILE
    tps = tiles.tps
    assert ORIG_GATE0 % PROJ_CHUNK == 0 and not tiles.skip_ctx

    def mod_idx(i):
        return (tiles.mod_row(i), 0, 0)

    return pl.pallas_call(
        _inproj_kernel,
        grid=(r // tm,),
        in_specs=[
            pl.BlockSpec((tm, d), lambda i: (i, 0)),
            pl.BlockSpec((1, d), lambda i: (0, 0)),
            pl.BlockSpec((None, N_MOD, d), mod_idx),
            pl.BlockSpec((tm, LANE), lambda i: (i % tps, 0)),
            pl.BlockSpec((tm, LANE), lambda i: (i % tps, 0)),
            pl.BlockSpec((None, d, IN_COLS), lambda i: (layer, 0, 0)),
        ],
        out_specs=[pl.BlockSpec((tm, IN_COLS), lambda i: (i, 0)),
                   pl.BlockSpec((PROJ_CHUNK, tm), lambda i: (0, i))],
        out_shape=[jax.ShapeDtypeStruct((r, IN_COLS), ACT_DT), jax.ShapeDtypeStruct((PROJ_CHUNK, r), ACT_DT)],
        compiler_params=_cparams("parallel"),
        name="inproj",
    )(xall, g1, mod, cos, sin_signed, w_in)


def _attn_kernel(dl_ref, g_ref, q_ref, kt_ref, v_ref, o_ref, vext_ref, s_ref, *, lam_init, ctx_rows, has_ctx):
    tq = s_ref.shape[1]
    seq = kt_ref.shape[1]
    one_hot = lax.broadcasted_iota(jnp.int32, v_ref.shape, 1) == 0
    vext_ref[:, :DV] = v_ref[...]
    vext_ref[:, DV:] = jnp.where(one_hot, 1.0, 0.0).astype(vext_ref.dtype)

    dl = dl_ref[...]
    lam = (jnp.exp(jnp.sum(dl[0:1] * dl[1:2], keepdims=True))
           - jnp.exp(jnp.sum(dl[2:3] * dl[3:4], keepdims=True)) + lam_init)
    kc = ATTN_KV_CHUNK

    def attend_rows(kv_rows, q_rows, out_rows):
        q = q_ref[q_rows, :]
        lane = lax.broadcasted_iota(jnp.int32, q.shape, 1)
        zero = jnp.zeros_like(q)
        qms = (jnp.where(lane < DK, q, zero), jnp.where(lane >= DK, q, zero))
        chunks = [slice(c * kc, (c + 1) * kc) for c in range(kv_rows // kc)]

        def scores(m, cols):
            return jnp.dot(qms[m], kt_ref[:, cols], preferred_element_type=F32)

        mxs = []
        for m in range(2):
            mrun = None
            for cols in chunks:
                s = scores(m, cols)
                s_ref[m, :, cols] = s
                for g in range(kc // LANE):
                    blk = s[:, g * LANE:(g + 1) * LANE]
                    mrun = blk if mrun is None else jnp.maximum(mrun, blk)
            mxs.append(jnp.max(mrun, axis=-1, keepdims=True))
        accs = [jnp.zeros((q.shape[0], 2 * DV), F32)] * 2
        for cols in chunks:
            for m in range(2):
                p = jnp.exp2(s_ref[m, :, cols] - mxs[m])
                accs[m] = accs[m] + jnp.dot(p.astype(MXU_DT), vext_ref[cols, :], preferred_element_type=F32)
        o = (accs[0][:, :DV] * (1.0 / accs[0][:, DV:DV + 1])
             - accs[1][:, :DV] * (lam / accs[1][:, DV:DV + 1]))
        o_ref[out_rows, :] = (_rms(o) * g_ref[...] * (1.0 - lam_init)).astype(o_ref.dtype)

    out_off = 0
    if has_ctx:
        attend_rows(ctx_rows, slice(0, ctx_rows), slice(0, ctx_rows))
        out_off = ctx_rows

    def latent_tile(i, carry):
        q0 = pl.multiple_of(ctx_rows + i * tq, tq)
        o0 = pl.multiple_of(out_off + i * tq, tq)
        attend_rows(seq, pl.ds(q0, tq), pl.ds(o0, tq))
        return carry

    lax.fori_loop(0, (seq - ctx_rows) // tq, latent_tile, 0)


def _attention(p, k_t, diff_lambda, g_sub, lam_init, tiles, ctx_rows):
    tq = ROW_TILE
    tps = tiles.tps
    seq = tps * tq
    out_rows = tiles.per_seq * tq
    assert ctx_rows == tq
    kern = functools.partial(_attn_kernel, lam_init=lam_init, ctx_rows=ctx_rows, has_ctx=not tiles.skip_ctx)
    return pl.pallas_call(
        kern,
        grid=(tiles.n_batch, H_A),
        in_specs=[
            pl.BlockSpec((4, DK), lambda b, h: (0, 0)),
            pl.BlockSpec((1, DV), lambda b, h: (0, 0)),
            pl.BlockSpec((seq, LANE), lambda b, h: (b, P_Q // LANE + h)),
            pl.BlockSpec((LANE, seq), lambda b, h: (h, b)),
            pl.BlockSpec((seq, LANE), lambda b, h: (b, P_V // LANE + h)),
        ],
        out_specs=pl.BlockSpec((out_rows, LANE), lambda b, h: (b, h)),
        out_shape=jax.ShapeDtypeStruct((tiles.count * tq, H_A * DV), ACT_DT),
        scratch_shapes=[pltpu.VMEM((seq, 2 * DV), MXU_DT), pltpu.VMEM((2, tq, seq), F32)],
        compiler_params=_cparams("parallel", "parallel"),
        name="diff_attention",
    )(diff_lambda, g_sub, p, k_t, p)


def _gmlp_kernel(uv_ref, gv_ref, ws_ref, bs_ref, o_ref):
    uv = jax.nn.gelu(uv_ref[...].astype(F32))
    u = uv[:, :W_B]
    v = (_rms(uv[:, W_B:]) * gv_ref[...]).astype(MXU_DT)
    gw = W_B // G_B
    for c in range(uv.shape[0] // CHUNK_B):
        rows = slice(c * CHUNK_B, (c + 1) * CHUNK_B)
        for g in range(G_B):
            cols = slice(g * gw, (g + 1) * gw)
            mixed = jnp.dot(ws_ref[g], v[rows, cols], preferred_element_type=F32) + bs_ref[:, cols]
            o_ref[rows, cols] = (u[rows, cols] * mixed).astype(o_ref.dtype)


def _gmlp(p, g_v, w_s, b_full, tiles):
    tm = ROW_TILE
    r = tiles.count * tm
    return pl.pallas_call(
        _gmlp_kernel,
        grid=(tiles.count,),
        in_specs=[
            pl.BlockSpec((tm, 2 * W_B), lambda i: (tiles.full(i), P_UV // (2 * W_B))),
            pl.BlockSpec((1, W_B), lambda i: (0, 0)),
            pl.BlockSpec((G_B, CHUNK_B, CHUNK_B), lambda i: (0, 0, 0)),
            pl.BlockSpec((CHUNK_B, W_B), lambda i: (0, 0)),
        ],
        out_specs=pl.BlockSpec((tm, W_B), lambda i: (i, 0)),
        out_shape=jax.ShapeDtypeStruct((r, W_B), ACT_DT),
        compiler_params=_cparams("parallel"),
        name="gmlp",
    )(p, g_v, w_s, b_full)


def _mfeat_kernel(x_ref, prev_ref, next_ref, wc_ref, bc_ref, wqkv_ref, wif_ref, bif_ref,
                  xc_ref, q_ref, k_ref, v_ref, g_ref, *, tps):
    tm = x_ref.shape[0]
    j = pl.program_id(0) % tps
    seg_start = jnp.logical_or(j == 0, j == 1)
    seg_end = jnp.logical_or(j == 0, j == tps - 1)
    x = x_ref[...].astype(F32)
    prow = jnp.where(seg_start, 0.0, prev_ref[...].astype(F32)[HALO - 1:HALO, :])
    nrow = jnp.where(seg_end, 0.0, next_ref[...].astype(F32)[0:1, :])
    rid = lax.broadcasted_iota(jnp.int32, x.shape, 0)
    xp = jnp.where(rid == 0, prow, pltpu.roll(x, 1, 0))
    xn = jnp.where(rid == tm - 1, nrow, pltpu.roll(x, tm - 1, 0))
    xc = wc_ref[0:1, :] * xp + wc_ref[1:2, :] * x + wc_ref[2:3, :] * xn + bc_ref[...]
    xc = xc * _sigmoid(xc)
    xc_ref[...] = xc.astype(xc_ref.dtype)
    xcb = xc.astype(MXU_DT)
    xb = x_ref[...]
    qs, ks, vs = [], [], []
    for h in range(H_C):
        cols = slice(h * DH_C, (h + 1) * DH_C)
        qs.append(jnp.dot(xcb[:, cols], wqkv_ref[0, h], preferred_element_type=F32))
        ks.append(jnp.dot(xcb[:, cols], wqkv_ref[1, h], preferred_element_type=F32))
        vs.append(jnp.dot(xb[:, cols], wqkv_ref[2, h], preferred_element_type=F32))
    q = jnp.concatenate(qs, axis=1)
    k = jnp.concatenate(ks, axis=1)
    v = jnp.concatenate(vs, axis=1)
    qkv = jnp.concatenate([q, k, v], axis=1).astype(MXU_DT)
    gates = jnp.dot(qkv, wif_ref[...], preferred_element_type=F32) + bif_ref[...]
    g_ref[...] = gates.T[0:g_ref.shape[0], :]
    q_ref[...] = q.T.astype(q_ref.dtype)
    k_ref[...] = (k * (DH_C ** -0.5)).astype(k_ref.dtype)
    v_ref[...] = v.T.astype(v_ref.dtype)


def _mlstm_features(p, w_conv, b_conv, w_qkv, w_if_pad, b_if_pad, tps):
    r = p.shape[0]
    tm = ROW_TILE
    hpt = tm // HALO
    xm_blk = P_XM // W_C
    last_halo = r // HALO - 1
    kern = functools.partial(_mfeat_kernel, tps=tps)
    act = jax.ShapeDtypeStruct((r, W_C), ACT_DT)
    act_t = jax.ShapeDtypeStruct((W_C, r), ACT_DT)
    row_spec = pl.BlockSpec((tm, W_C), lambda i: (i, 0))
    col_spec = pl.BlockSpec((W_C, tm), lambda i: (0, i))
    return pl.pallas_call(
        kern,
        grid=(r // tm,),
        in_specs=[
            pl.BlockSpec((tm, W_C), lambda i: (i, xm_blk)),
            pl.BlockSpec((HALO, W_C), lambda i: (jnp.maximum(i * hpt - 1, 0), xm_blk)),
            pl.BlockSpec((HALO, W_C), lambda i: (jnp.minimum((i + 1) * hpt, last_halo), xm_blk)),
            pl.BlockSpec((CONV_K, W_C), lambda i: (0, 0)),
            pl.BlockSpec((1, W_C), lambda i: (0, 0)),
            pl.BlockSpec((3, H_C, DH_C, DH_C), lambda i: (0, 0, 0, 0)),
            pl.BlockSpec((3 * W_C, LANE), lambda i: (0, 0)),
            pl.BlockSpec((1, LANE), lambda i: (0, 0)),
        ],
        out_specs=[row_spec, col_spec, row_spec, col_spec,
                   pl.BlockSpec((2 * GATE_ROWS, tm), lambda i: (0, i))],
        out_shape=[act, act_t, act, act_t, jax.ShapeDtypeStruct((2 * GATE_ROWS, r), F32)],
        compiler_params=_cparams("parallel"),
        name="mlstm_features",
    )(p, p, p, w_conv, b_conv, w_qkv, w_if_pad, b_if_pad)


GATE_ROWS = 2 * H_C
GATE_PACK = ("r", "cmx", "bcum")


def _gate_prep_kernel(g_ref, pack_ref, rt_ref):
    seg = MLSTM_CHUNK
    gt_fwd = g_ref[0:GATE_ROWS, :]
    gt_bwd = g_ref[GATE_ROWS:2 * GATE_ROWS, :]
    n = gt_fwd.shape[1]
    lane = lax.broadcasted_iota(jnp.int32, gt_fwd.shape, 1)
    in_seg = lane & (seg - 1)

    def scan_both(xf, xb, op):
        sh = 1
        while sh < seg:
            xf = jnp.where(in_seg >= sh, op(xf, pltpu.roll(xf, sh, 1)), xf)
            xb = jnp.where(in_seg < seg - sh, op(xb, pltpu.roll(xb, n - sh, 1)), xb)
            sh *= 2
        return xf, xb

    bcum_f, bcum_b = scan_both(jax.nn.log_sigmoid(pltpu.roll(gt_fwd, H_C, 0)),
                               jax.nn.log_sigmoid(pltpu.roll(gt_bwd, H_C, 0)), jnp.add)
    r_f, r_b = gt_fwd - bcum_f, gt_bwd - bcum_b
    cmx_f, cmx_b = scan_both(r_f, r_b, jnp.maximum)
    pack_ref[...] = jnp.concatenate([r_f, cmx_f, bcum_f, r_b, cmx_b, bcum_b], axis=0)
    rt_ref[...] = jnp.concatenate([r_f, r_b, jnp.zeros((LANE - 2 * GATE_ROWS, n), F32)], axis=0).T


def _mlstm_gate_prep(gates_t, n_batch, seq):
    rows = 2 * len(GATE_PACK) * GATE_ROWS
    r = gates_t.shape[1]
    return pl.pallas_call(
        _gate_prep_kernel,
        grid=(n_batch,),
        in_specs=[pl.BlockSpec((2 * GATE_ROWS, seq), lambda b: (0, b))],
        out_specs=[pl.BlockSpec((rows, seq), lambda b: (0, b)), pl.BlockSpec((seq, LANE), lambda b: (b, 0))],
        out_shape=[jax.ShapeDtypeStruct((rows, r), F32), jax.ShapeDtypeStruct((r, LANE), F32)],
        compiler_params=_cparams("parallel"),
        name="mlstm_gate_prep",
    )(gates_t)


def _mlstm_gates(pack, m_ref, reverse):
    ln = pack.shape[1]
    part = {name: pack[i * GATE_ROWS:(i + 1) * GATE_ROWS] for i, name in enumerate(GATE_PACK)}
    end = slice(0, 1) if reverse else slice(ln - 1, ln)
    g = part["bcum"][:, end]
    r_max = part["cmx"][:, end]
    m_loc = g + r_max
    w_end = jnp.exp(part["r"] - r_max)
    m_old = m_ref[:, 0:1]
    m_new = jnp.maximum(g + m_old, m_loc)
    a_old = jnp.exp(g + m_old - m_new)
    a_new = jnp.exp(m_loc - m_new)
    mx = jnp.maximum(m_old, part["cmx"])
    w_inter = jnp.exp(m_old - mx)
    e_neg = jnp.exp(-(part["bcum"] + mx))
    s_id = lax.broadcasted_iota(jnp.int32, (ln, ln), 0)
    t_id = lax.broadcasted_iota(jnp.int32, (ln, ln), 1)
    keep = (s_id >= t_id) if reverse else (s_id <= t_id)
    return dict(mx=mx, w_inter=w_inter, e_neg=e_neg, w_end=w_end, a_old=a_old, a_new=a_new,
                m_new=m_new, keep=keep)


def _mscan_kernel(gf_ref, rf_ref, qf_ref, kf_ref, vf_ref, gb_ref, rb_ref, qb_ref, kb_ref, vb_ref,
                  hf_ref, hb_ref, cf_ref, mf_ref, cb_ref, mb_ref):
    @pl.when(pl.program_id(1) == 0)
    def _():
        cf_ref[...] = jnp.zeros_like(cf_ref)
        cb_ref[...] = jnp.zeros_like(cb_ref)
        mf_ref[...] = jnp.zeros_like(mf_ref)
        mb_ref[...] = jnp.zeros_like(mb_ref)

    ln = qf_ref.shape[1]
    pr = len(GATE_PACK) * GATE_ROWS
    dirs = ((_mlstm_gates(gf_ref[0:pr, :], mf_ref, False), qf_ref, kf_ref, vf_ref, cf_ref, hf_ref, rf_ref),
            (_mlstm_gates(gb_ref[pr:2 * pr, :], mb_ref, True), qb_ref, kb_ref, vb_ref, cb_ref, hb_ref, rb_ref))
    pairs = [(d, h) for d in range(2) for h in range(H_C)]
    ones_row = jnp.where(lax.broadcasted_iota(jnp.int32, (DH_C, ln), 0) == 0, 1.0, 0.0).astype(MXU_DT)
    head = lambda h: slice(h * DH_C, (h + 1) * DH_C)
    row = lambda h: slice(h, h + 1)

    ks = [dirs[d][2][:, head(h)] for d, h in pairs]
    qts = [dirs[d][1][head(h), :] for d, h in pairs]
    vexts = [jnp.concatenate([dirs[d][3][head(h), :], ones_row], axis=0) for d, h in pairs]
    cexts = [dirs[d][4][h] for d, h in pairs]
    qks = [jnp.dot(k, q_t, preferred_element_type=F32) for k, q_t in zip(ks, qts)]
    rhss = []
    for i, (d, h) in enumerate(pairs):
        g = dirs[d][0]
        r_col = dirs[d][6][:, d * GATE_ROWS + h:d * GATE_ROWS + h + 1]
        d_t = jnp.exp(jnp.where(g["keep"], r_col - g["mx"][row(h), :], NEG_BIG))
        wq = (qts[i].astype(F32) * g["w_inter"][row(h), :]).astype(MXU_DT)
        rhss.append(jnp.concatenate([(qks[i] * d_t).astype(MXU_DT), wq], axis=0))
    ress = [jnp.dot(jnp.concatenate([vexts[i], cexts[i].astype(MXU_DT)], axis=1), rhss[i],
                    preferred_element_type=F32) for i in range(len(pairs))]
    for i, (d, h) in enumerate(pairs):
        g = dirs[d][0]
        den = ress[i][DH_C:DH_C + 1, :]
        hh = ress[i][:DH_C, :] / jnp.maximum(jnp.abs(den), g["e_neg"][row(h), :])
        dirs[d][5][head(h), :] = hh.astype(dirs[d][5].dtype)
    clocs = []
    for i, (d, h) in enumerate(pairs):
        vw_t = (vexts[i].astype(F32) * dirs[d][0]["w_end"][row(h), :]).astype(MXU_DT)
        clocs.append(jnp.dot(vw_t, ks[i], preferred_element_type=F32))
    for i, (d, h) in enumerate(pairs):
        g = dirs[d][0]
        dirs[d][4][h] = g["a_old"][row(h), :] * cexts[i] + g["a_new"][row(h), :] * clocs[i]
    mf_ref[...] = jnp.broadcast_to(dirs[0][0]["m_new"], mf_ref.shape)
    mb_ref[...] = jnp.broadcast_to(dirs[1][0]["m_new"], mb_ref.shape)


def _mlstm_scan(q_t, k, v_t, gate_pack, r_t, n_batch, seq, ctx_rows):
    ln = MLSTM_CHUNK
    nck = seq // ln
    ncc = ctx_rows // ln

    def fwd_chunk(b, s):
        return b * nck + s

    def bwd_chunk(b, s):
        return b * nck + jnp.where(s < ncc, ncc - 1 - s, nck - 1 + ncc - s)

    def specs(chunk):
        rows = pl.BlockSpec((ln, W_C), lambda b, s: (chunk(b, s), 0))
        cols = pl.BlockSpec((W_C, ln), lambda b, s: (0, chunk(b, s)))
        gate = pl.BlockSpec((gate_pack.shape[0], ln), lambda b, s: (0, chunk(b, s)))
        r_cols = pl.BlockSpec((ln, LANE), lambda b, s: (chunk(b, s), 0))
        return [gate, r_cols, cols, rows, cols], cols

    in_f, out_f = specs(fwd_chunk)
    in_b, out_b = specs(bwd_chunk)
    out = jax.ShapeDtypeStruct(q_t.shape, ACT_DT)
    return pl.pallas_call(
        _mscan_kernel,
        grid=(n_batch, nck),
        in_specs=in_f + in_b,
        out_specs=[out_f, out_b],
        out_shape=[out, out],
        scratch_shapes=[
            pltpu.VMEM((H_C, 2 * DH_C, DH_C), F32), pltpu.VMEM((8, LANE), F32),
            pltpu.VMEM((H_C, 2 * DH_C, DH_C), F32), pltpu.VMEM((8, LANE), F32),
        ],
        compiler_params=_cparams("parallel", "arbitrary"),
        name="mlstm_scan",
    )(gate_pack, r_t, q_t, k, v_t, gate_pack, r_t, q_t, k, v_t)


def _route(logits, carry):
    lane = lax.broadcasted_iota(jnp.int32, logits.shape, 1).astype(F32)
    big = float(4 * LANE)
    gl = jnp.where(lane < N_GROUPS, logits, NEG_BIG)
    gmax = jnp.max(gl, axis=-1, keepdims=True)
    g_star = jnp.min(jnp.where(gl == gmax, lane, big), axis=-1, keepdims=True)
    p_g = 1.0 / jnp.sum(jnp.exp(gl - gmax), axis=-1, keepdims=True)
    e_lo = g_star * EXP_PER_GROUP + N_GROUPS
    in_group = jnp.logical_and(lane >= e_lo, lane < e_lo + EXP_PER_GROUP)
    el = jnp.where(in_group, logits, NEG_BIG)
    v1 = jnp.max(el, axis=-1, keepdims=True)
    i1 = jnp.min(jnp.where(el == v1, lane, big), axis=-1, keepdims=True)
    el2 = jnp.where(lane == i1, NEG_BIG, el)
    v2 = jnp.max(el2, axis=-1, keepdims=True)
    i2 = jnp.min(jnp.where(el2 == v2, lane, big), axis=-1, keepdims=True)
    e21 = jnp.exp(v2 - v1)
    w1 = p_g / (1.0 + e21)
    w2 = p_g * e21 / (1.0 + e21)
    e1 = i1 - N_GROUPS
    e2 = i2 - N_GROUPS
    oh1 = lane == e1
    oh2 = lane == e2
    both = jnp.where(jnp.logical_or(oh1, oh2), 1.0, 0.0)
    tm = logits.shape[0]
    lower = (lax.broadcasted_iota(jnp.int32, (tm, tm), 1) < lax.broadcasted_iota(jnp.int32, (tm, tm), 0))
    before = jnp.dot(jnp.where(lower, 1.0, 0.0).astype(MXU_DT), both.astype(MXU_DT),
                     preferred_element_type=F32) + carry
    rank1 = jnp.sum(jnp.where(oh1, before, 0.0), axis=-1, keepdims=True)
    rank2 = jnp.sum(jnp.where(oh2, before, 0.0), axis=-1, keepdims=True)
    out = jnp.where(lane == 0, e1, 0.0)
    out = jnp.where(lane == 1, e2, out)
    out = jnp.where(lane == 2, w1, out)
    out = jnp.where(lane == 3, w2, out)
    out = jnp.where(lane == 4, rank1, out)
    out = jnp.where(lane == 5, rank2, out)
    return out, carry + jnp.sum(both, axis=0, keepdims=True)


def _merge_kernel(x_ref, gate_ref, z_ref, ya_ref, yb_ref, hf_ref, hb_ref, xc_ref, gm_ref, sk_ref,
                  wbr_ref, wo_ref, mod_ref, g2_ref, wr_ref, br_ref, xo_ref, rt_ref, cnt_ref, carry_ref):
    @pl.when(pl.program_id(0) == 0)
    def _():
        carry_ref[...] = jnp.zeros_like(carry_ref)

    hs_t = hf_ref[...].astype(F32) + hb_ref[...].astype(F32)
    hn_t = []
    for h in range(H_C):
        part = hs_t[h * DH_C:(h + 1) * DH_C, :]
        hn_t.append(part * lax.rsqrt(jnp.mean(part * part, axis=0, keepdims=True) + EPS))
    hn = jnp.concatenate(hn_t, axis=0).T * gm_ref[...]
    yc = (hn + sk_ref[...] * xc_ref[...].astype(F32)) * _sigmoid(z_ref[...].astype(F32))
    ys = (ya_ref[...], yb_ref[...], yc.astype(MXU_DT))
    merged = None
    for i in range(3):
        gate = _sigmoid(gate_ref[:, i * D_MODEL:(i + 1) * D_MODEL].astype(MXU_DT))
        term = gate * jnp.dot(ys[i], wbr_ref[i], preferred_element_type=F32).astype(MXU_DT)
        merged = term if merged is None else merged + term
    y = jnp.dot(merged, wo_ref[...], preferred_element_type=F32)
    xn = x_ref[...] + mod_ref[2:3, :] * y
    xo_ref[...] = xn
    h2 = _rms(xn) * g2_ref[...] * (1.0 + mod_ref[4:5, :]) + mod_ref[3:4, :]
    h_hi, h_lo = _split_bf16(h2)
    logits = (jnp.dot(h_hi, wr_ref[0], preferred_element_type=F32)
              + jnp.dot(h_lo, wr_ref[0], preferred_element_type=F32)
              + jnp.dot(h_hi, wr_ref[1], preferred_element_type=F32)) + br_ref[...]
    route, carry = _route(logits, carry_ref[0:1, :])
    rt_ref[...] = route
    carry_ref[0:1, :] = carry
    cnt_ref[...] = jnp.broadcast_to(carry, cnt_ref.shape)


def _merge(xall, p, ya, yb, hf, hb, xconv, g_m, skip, w_br, w_o, mod, g2, w_route, b_route, tiles):
    d = xall.shape[1]
    tm = ROW_TILE
    r = tiles.count * tm

    def mod_idx(i):
        return (tiles.mod_row(i), 0, 0)

    row = lambda w: pl.BlockSpec((tm, w), lambda i: (i, 0))
    frow = lambda w: pl.BlockSpec((tm, w), lambda i: (tiles.full(i), 0))
    fcol = lambda w: pl.BlockSpec((w, tm), lambda i: (0, tiles.full(i)))
    full = lambda shape: pl.BlockSpec(shape, lambda i: (0,) * len(shape))
    return pl.pallas_call(
        _merge_kernel,
        grid=(tiles.count,),
        in_specs=[
            frow(d),
            pl.BlockSpec((tm, GATE_W), lambda i: (tiles.full(i), P_GATE // GATE_W)),
            pl.BlockSpec((tm, W_C), lambda i: (tiles.full(i), P_Z // W_C)),
            row(W_BR), row(W_BR), fcol(W_C), fcol(W_C), frow(W_C),
            full((1, W_C)), full((1, W_C)),
            full((3, W_BR, d)), full((d, d)),
            pl.BlockSpec((None, N_MOD, d), mod_idx),
            full((1, d)), full((2, d, LANE)), full((1, LANE)),
        ],
        out_specs=[row(d), row(LANE), full((8, LANE))],
        out_shape=[jax.ShapeDtypeStruct((r, d), F32), jax.ShapeDtypeStruct((r, LANE), F32),
                   jax.ShapeDtypeStruct((8, LANE), F32)],
        scratch_shapes=[pltpu.VMEM((8, LANE), F32)],
        compiler_params=_cparams("arbitrary"),
        name="merge_route",
    )(xall, p, p, ya, yb, hf, hb, xconv, g_m, skip, w_br, w_o, mod, g2, w_route, b_route)


def _row_copy(src_hbm, src_row, dst_ref, dst_row, sem):
    return pltpu.make_async_copy(src_hbm.at[pl.ds(src_row, 1), :], dst_ref.at[pl.ds(dst_row, 1), :], sem)


def _dispatch_kernel(fill_ref, nlive_ref, pos_ref, x_ref, g2_ref, mod_ref, xs_hbm, hbuf, zbuf, zsem, sem):
    i = pl.program_id(0)
    n = pl.num_programs(0)
    tm = x_ref.shape[0]
    slot = i % 2
    bm = zbuf.shape[0]
    n_blocks = xs_hbm.shape[0] // bm

    @pl.when(i == 0)
    def _():
        zbuf[...] = jnp.zeros_like(zbuf)

        def fill_block(b):
            return pltpu.make_async_copy(zbuf, xs_hbm.at[pl.ds(b * bm, bm), :], zsem)

        def fill_last(k, carry):
            fill_block(fill_ref[k]).start()
            return carry

        def fill_unused(b, carry):
            fill_block(b).start()
            return carry

        def wait_fill(k, carry):
            fill_block(0).wait()
            return carry

        n_live = nlive_ref[0]
        n_used = nlive_ref[1]
        lax.fori_loop(0, n_used, fill_last, 0)
        lax.fori_loop(n_live, n_blocks, fill_unused, 0)
        lax.fori_loop(0, n_used + n_blocks - n_live, wait_fill, 0)

    hbuf[slot] = _rms(x_ref[...]) * g2_ref[...] * (1.0 + mod_ref[4:5, :]) + mod_ref[3:4, :]

    def body(c, carry):
        for u in range(DMA_UNROLL):
            r = c * DMA_UNROLL + u
            for j in range(TOP_K):
                pltpu.make_async_copy(hbuf.at[slot, pl.ds(r, 1), :],
                                      xs_hbm.at[pl.ds(pos_ref[0, j * tm + r], 1), :], sem.at[slot]).start()
        return carry

    lax.fori_loop(0, tm // DMA_UNROLL, body, 0)

    def wait_tile(s):
        pltpu.make_async_copy(hbuf.at[s], xs_hbm.at[pl.ds(0, tm), :], sem.at[s]).wait()
        pltpu.make_async_copy(hbuf.at[s], xs_hbm.at[pl.ds(0, tm), :], sem.at[s]).wait()

    @pl.when(i > 0)
    def _():
        wait_tile(1 - slot)

    @pl.when(i == n - 1)
    def _():
        wait_tile(slot)


def _moe_dispatch(xall, g2, mod, fill_blocks, n_live, pos_tiles, n_slots, tiles):
    r, d = xall.shape
    tm = ROW_TILE
    assert TOP_K == 2 and r == tiles.count * tm

    def mod_idx(i, fb, nl):
        return (tiles.mod_row(i), 0, 0)

    grid_spec = pltpu.PrefetchScalarGridSpec(
        num_scalar_prefetch=2,
        grid=(r // tm,),
        in_specs=[
            pl.BlockSpec((None, 1, TOP_K * tm), lambda i, fb, nl: (i, 0, 0), memory_space=pltpu.SMEM),
            pl.BlockSpec((tm, d), lambda i, fb, nl: (i, 0)),
            pl.BlockSpec((1, d), lambda i, fb, nl: (0, 0)),
            pl.BlockSpec((None, N_MOD, d), mod_idx),
        ],
        out_specs=pl.BlockSpec(memory_space=pl.ANY),
        scratch_shapes=[pltpu.VMEM((2, tm, d), F32), pltpu.VMEM((MOE_BLOCK, d), F32),
                        pltpu.SemaphoreType.DMA(()), pltpu.SemaphoreType.DMA((2,))],
    )
    return pl.pallas_call(
        _dispatch_kernel,
        grid_spec=grid_spec,
        out_shape=jax.ShapeDtypeStruct((n_slots, d), F32),
        compiler_params=_cparams("arbitrary"),
        name="moe_dispatch",
    )(fill_blocks, n_live, pos_tiles, xall, g2, mod)


def _moe_kernel(be_ref, nv_ref, x_ref, wg_ref, wu_ref, wd_ref, y_ref):
    n_valid = nv_ref[pl.program_id(0)]

    @pl.when(n_valid > 0)
    def _():
        x = x_ref[...].astype(MXU_DT)
        a = jnp.dot(x, wg_ref[...], preferred_element_type=F32)
        u = jnp.dot(x, wu_ref[...], preferred_element_type=F32)
        act = (a * _sigmoid(a) * u).astype(MXU_DT)
        y_ref[...] = jnp.dot(act, wd_ref[...], preferred_element_type=F32)

    @pl.when(n_valid <= 0)
    def _():
        y_ref[...] = jnp.zeros_like(y_ref)


def _moe_experts(xs, block_e, n_valid, w_gate, w_up, w_down, layer):
    n_slots, d = xs.shape
    bm = MOE_BLOCK
    grid_spec = pltpu.PrefetchScalarGridSpec(
        num_scalar_prefetch=2,
        grid=(n_slots // bm,),
        in_specs=[
            pl.BlockSpec((bm, d), lambda i, be, nv: (i, 0)),
            pl.BlockSpec((None, None, d, D_EXPERT), lambda i, be, nv: (layer, be[i], 0, 0)),
            pl.BlockSpec((None, None, d, D_EXPERT), lambda i, be, nv: (layer, be[i], 0, 0)),
            pl.BlockSpec((None, None, D_EXPERT, d), lambda i, be, nv: (layer, be[i], 0, 0)),
        ],
        out_specs=pl.BlockSpec((bm, d), lambda i, be, nv: (i, 0)),
    )
    return pl.pallas_call(
        _moe_kernel,
        grid_spec=grid_spec,
        out_shape=jax.ShapeDtypeStruct((n_slots, d), F32),
        compiler_params=_cparams("arbitrary"),
        name="moe_experts",
    )(block_e, n_valid, xs, w_gate, w_up, w_down)


def _combine_kernel(pos_ref, pos_next_ref, x_ref, rt_ref, mod_ref, gf_ref, yb_hbm, xo_ref, ybuf, sem, *, final):
    i = pl.program_id(0)
    n = pl.num_programs(0)
    tm = x_ref.shape[0]

    def start_gather(ids_ref, slot):
        def body(c, carry):
            for u in range(DMA_UNROLL):
                r = c * DMA_UNROLL + u
                _row_copy(yb_hbm, ids_ref[0, r], ybuf.at[slot], r, sem.at[slot]).start()
            return carry
        lax.fori_loop(0, TOP_K * tm // DMA_UNROLL, body, 0)

    @pl.when(i == 0)
    def _():
        start_gather(pos_ref, 0)

    @pl.when(i + 1 < n)
    def _():
        start_gather(pos_next_ref, (i + 1) % 2)

    slot = i % 2
    pltpu.make_async_copy(yb_hbm.at[pl.ds(0, TOP_K * tm), :], ybuf.at[slot], sem.at[slot]).wait()
    rt = rt_ref[...]
    f = rt[:, 2:3] * ybuf[slot, 0:tm, :] + rt[:, 3:4] * ybuf[slot, tm:2 * tm, :]
    xn = x_ref[...] + mod_ref[5:6, :] * f
    xo_ref[...] = _rms(xn) * gf_ref[...] if final else xn


def _combine(xall, route, mod, g_final, yb, pos, tiles, final):
    r, d = xall.shape
    tm = ROW_TILE
    nt = tiles.count
    last = nt - 1
    assert r == nt * tm

    def mod_idx(i):
        return (tiles.mod_row(i), 0, 0)

    row = lambda w: pl.BlockSpec((tm, w), lambda i: (i, 0))
    return pl.pallas_call(
        functools.partial(_combine_kernel, final=final),
        grid=(nt,),
        in_specs=[
            pl.BlockSpec((None, 1, TOP_K * tm), lambda i: (i, 0, 0), memory_space=pltpu.SMEM),
            pl.BlockSpec((None, 1, TOP_K * tm), lambda i: (jnp.minimum(i + 1, last), 0, 0),
                         memory_space=pltpu.SMEM),
            row(d), row(LANE),
            pl.BlockSpec((None, N_MOD, d), mod_idx),
            pl.BlockSpec((1, d), lambda i: (0, 0)),
            pl.BlockSpec(memory_space=pl.ANY),
        ],
        out_specs=row(d),
        out_shape=jax.ShapeDtypeStruct((r, d), F32),
        scratch_shapes=[pltpu.VMEM((2, TOP_K * tm, d), F32), pltpu.SemaphoreType.DMA((2,))],
        compiler_params=_cparams("arbitrary"),
        name="moe_combine",
    )(pos, pos, xall, route, mod, g_final, yb)


def _dispatch_plan(route, counts_f, n_tiles):
    n = route.shape[0]
    bm = MOE_BLOCK
    n_blocks = -(-(n * TOP_K + N_EXPERTS * (bm - 1)) // bm)
    counts = counts_f[:N_EXPERTS].astype(jnp.int32)
    padded = (counts + bm - 1) // bm * bm
    pend = jnp.cumsum(padded)
    seg_off = pend - padded
    eid = route[:, 0:TOP_K].astype(jnp.int32)
    rank = route[:, 4:4 + TOP_K].astype(jnp.int32)
    experts = jnp.arange(N_EXPERTS, dtype=jnp.int32)
    pos = jnp.sum(jnp.where(eid[..., None] == experts, seg_off, 0), axis=-1) + rank
    blk0 = jnp.arange(n_blocks, dtype=jnp.int32) * bm
    block_e = jnp.minimum(jnp.sum((blk0[:, None] >= pend[None, :]).astype(jnp.int32), axis=1), N_EXPERTS - 1)
    seg_end = seg_off + counts
    n_valid = jnp.clip(seg_end[block_e] - blk0, 0, bm).astype(jnp.int32)
    pos_tiles = pos.reshape(n_tiles, n // n_tiles, TOP_K).transpose(0, 2, 1).reshape(n_tiles, 1, -1)
    used = counts > 0
    last_blocks = jnp.sort(jnp.where(used, pend // bm - 1, n_blocks)).astype(jnp.int32)
    n_live = jnp.stack([pend[-1] // bm, jnp.sum(used)]).astype(jnp.int32)
    return block_e.astype(jnp.int32), n_valid, last_blocks, n_live, pos_tiles, n_blocks * bm


def _rope_tables(seq, ctx_rows):
    t = jnp.arange(seq - ctx_rows)
    inv = ROPE_BASE ** (-jnp.arange(0, ROT_AX, 2, dtype=F32) / ROT_AX)
    ang_r = (t // GRID_W).astype(F32)[:, None] * inv
    ang_c = (t % GRID_W).astype(F32)[:, None] * inv
    cos64 = jnp.concatenate([jnp.cos(ang_r)] * 2 + [jnp.cos(ang_c)] * 2, axis=1)
    sin64 = jnp.concatenate([-jnp.sin(ang_r), jnp.sin(ang_r), -jnp.sin(ang_c), jnp.sin(ang_c)], axis=1)
    cos = jnp.concatenate([jnp.ones((ctx_rows, DK), F32), cos64], axis=0)
    sin = jnp.concatenate([jnp.zeros((ctx_rows, DK), F32), sin64], axis=0)
    return jnp.tile(cos, (1, LANE // DK)), jnp.tile(sin, (1, LANE // DK))


def kernel(x, c, ctx, c_ctx, w_mod, b_mod, g_norm1, g_norm2, w_in, diff_lambda, g_diff_subln, w_gmlp_s, b_gmlp_s, g_gmlp_v, w_conv_m, b_conv_m, w_qkv_m, w_if_m, b_if_m, g_mlstm_norm, skip_m, w_branch, w_out, w_route_g, b_route_g, w_route_e, b_route_e, w_e_gate, w_e_up, w_e_down, g_final):
    n_batch, t_lat, d = x.shape
    ctx_rows = ctx.shape[1]
    depth = w_in.shape[0]
    seq = ctx_rows + t_lat
    assert d == D_MODEL and ctx_rows == ROW_TILE and t_lat % ROW_TILE == 0 and t_lat % GRID_W == 0
    tps = seq // ROW_TILE
    all_tiles = _Tiles(n_batch, tps, False)

    xall = jnp.concatenate([ctx, x], axis=1).reshape(n_batch * seq, d)
    mb = -(-(n_batch + 1) // 8) * 8
    c_all = jnp.zeros((mb, d), F32).at[:n_batch].set(c).at[n_batch].set(c_ctx)
    mod_all = _modulation(c_all, w_mod, b_mod)[:, :n_batch + 1].reshape(depth, n_batch + 1, N_MOD, d)
    cos, sin_signed = _rope_tables(seq, ctx_rows)
    w_in_b = w_in.astype(MXU_DT)
    w_eg_b, w_eu_b, w_ed_b = w_e_gate.astype(MXU_DT), w_e_up.astype(MXU_DT), w_e_down.astype(MXU_DT)

    for l in range(depth):
        lam_init = 0.8 - 0.6 * math.exp(-0.3 * l)
        mod = mod_all[l]
        last = l == depth - 1
        tiles = _Tiles(n_batch, tps, last)
        p, k_t = _inproj(xall, g_norm1[l][None], mod, cos, sin_signed, w_in_b, l, all_tiles)

        ya = _attention(p, k_t, diff_lambda[l], g_diff_subln[l][None], lam_init, tiles, ctx_rows)

        b_full = jnp.repeat(b_gmlp_s[l].T, W_B // G_B, axis=1)
        yb = _gmlp(p, g_gmlp_v[l][None], w_gmlp_s[l].astype(MXU_DT), b_full, tiles)

        w_if_pad = jnp.zeros((3 * W_C, LANE), F32).at[:, :4 * H_C].set(w_if_m[l]).astype(MXU_DT)
        b_if_pad = jnp.zeros((1, LANE), F32).at[0, :4 * H_C].set(b_if_m[l])
        xconv, q_m, k_m, v_m, gates_t = _mlstm_features(
            p, w_conv_m[l], b_conv_m[l][None], w_qkv_m[l].astype(MXU_DT), w_if_pad, b_if_pad, tps)
        gate_pack, r_t = _mlstm_gate_prep(gates_t, n_batch, seq)
        hf, hb = _mlstm_scan(q_m, k_m, v_m, gate_pack, r_t, n_batch, seq, ctx_rows)

        w_route = (jnp.zeros((d, LANE), F32).at[:, :N_GROUPS].set(w_route_g[l])
                   .at[:, N_GROUPS:N_GROUPS + N_EXPERTS].set(w_route_e[l]))
        w_route = jnp.stack(_split_bf16(w_route))
        b_route = (jnp.zeros((1, LANE), F32).at[0, :N_GROUPS].set(b_route_g[l])
                   .at[0, N_GROUPS:N_GROUPS + N_EXPERTS].set(b_route_e[l]))
        xall, route, counts = _merge(xall, p, ya, yb, hf, hb, xconv, g_mlstm_norm[l][None], skip_m[l][None],
                                         w_branch[l].astype(MXU_DT), w_out[l].astype(MXU_DT), mod,
                                         g_norm2[l][None], w_route, b_route, tiles)

        block_e, n_valid, fill_blocks, n_live, pos_tiles, n_slots = _dispatch_plan(route, counts[0], tiles.count)
        xs = _moe_dispatch(xall, g_norm2[l][None], mod, fill_blocks, n_live, pos_tiles, n_slots, tiles)
        y_sorted = _moe_experts(xs, block_e, n_valid, w_eg_b, w_eu_b, w_ed_b, l)
        xall = _combine(xall, route, mod, g_final[None], y_sorted, pos_tiles, tiles, last)

    return xall.reshape(n_batch, t_lat, d)
```

```python
import functools
import math
from typing import NamedTuple

import jax
import jax.numpy as jnp
from jax import lax
from jax.experimental import pallas as pl
from jax.experimental.pallas import tpu as pltpu

F32 = jnp.float32
MXU_DT = jnp.bfloat16
ACT_DT = jnp.bfloat16
HIGHEST = lax.Precision.HIGHEST

D_MODEL = 1024
N_MOD = 6
EPS = 1e-6
GRID_W = 64
H_A, DK = 4, 64
DV = 2 * DK
ROT_AX = DK // 2
ROPE_BASE = 10000.0
W_B, G_B, CHUNK_B = 512, 4, 128
H_C, DH_C, CONV_K, MLSTM_CHUNK = 4, 128, 3, 128
W_C = H_C * DH_C
N_GROUPS, EXP_PER_GROUP, TOP_K, D_EXPERT = 4, 8, 2, 512
N_EXPERTS = N_GROUPS * EXP_PER_GROUP
W_BR = 512

GATE_W = 3 * D_MODEL
P_GATE = 0
P_K = GATE_W
P_V = P_K + 512
P_XM = P_V + 512
P_Q = P_XM + 512
P_UV = P_Q + 512
P_Z = P_UV + 2 * W_B
IN_COLS = P_Z + W_C
ORIG_GATE0 = IN_COLS - GATE_W

LANE = 128
ROW_TILE = 256
PROJ_CHUNK = 512
MOE_BLOCK = 512
HALO = 16
DMA_UNROLL = 8
ATTN_KV_CHUNK = 256
LOG2_E = math.log2(math.e)
VMEM_LIMIT = 56 * 1024 * 1024
NEG_BIG = -1e30


class _Tiles(NamedTuple):
    n_batch: int
    tps: int
    skip_ctx: bool

    @property
    def per_seq(self):
        return self.tps - 1 if self.skip_ctx else self.tps

    @property
    def count(self):
        return self.n_batch * self.per_seq

    def full(self, i):
        if not self.skip_ctx:
            return i
        return (i // self.per_seq) * self.tps + 1 + i % self.per_seq

    def mod_row(self, i):
        if self.skip_ctx:
            return i // self.per_seq
        return jnp.where(i % self.tps == 0, self.n_batch, i // self.tps)


def _cparams(*sem):
    return pltpu.CompilerParams(dimension_semantics=sem, vmem_limit_bytes=VMEM_LIMIT)


def _rms(x):
    return x * lax.rsqrt(jnp.mean(x * x, axis=-1, keepdims=True) + EPS)


def _sigmoid(x):
    return 0.5 * jnp.tanh(0.5 * x) + 0.5


def _split_bf16(x):
    hi = x.astype(jnp.bfloat16)
    return hi, (x - hi.astype(F32)).astype(jnp.bfloat16)


def _mod_kernel(c_ref, w_ref, b_ref, o_ref):
    c = c_ref[...]
    s = c * jax.nn.sigmoid(c)
    o_ref[...] = jnp.dot(s, w_ref[...], preferred_element_type=F32, precision=HIGHEST) + b_ref[...]


def _modulation(c_all, w_mod, b_mod):
    n_layer = w_mod.shape[0]
    mb, d = c_all.shape
    tn = 1024
    return pl.pallas_call(
        _mod_kernel,
        grid=(n_layer, N_MOD * d // tn),
        in_specs=[
            pl.BlockSpec((mb, d), lambda l, j: (0, 0)),
            pl.BlockSpec((None, d, tn), lambda l, j: (l, 0, j)),
            pl.BlockSpec((None, 1, tn), lambda l, j: (l, 0, j)),
        ],
        out_specs=pl.BlockSpec((None, mb, tn), lambda l, j: (l, 0, j)),
        out_shape=jax.ShapeDtypeStruct((n_layer, mb, N_MOD * d), F32),
        compiler_params=_cparams("parallel", "parallel"),
        name="modulation",
    )(c_all, w_mod, b_mod.reshape(n_layer, 1, N_MOD * d))


def _rope(acc, cos, sin_signed):
    w = acc.shape[1]
    lane = lax.broadcasted_iota(jnp.int32, acc.shape, 1)
    partner = jnp.where((lane & 16) == 0, pltpu.roll(acc, w - 16, 1), pltpu.roll(acc, 16, 1))
    reps = w // cos.shape[1]
    return acc * jnp.tile(cos, (1, reps)) + partner * jnp.tile(sin_signed, (1, reps))


def _inproj_kernel(x_ref, g_ref, mod_ref, cos_ref, sin_ref, w_ref, o_ref, kt_ref):
    x = x_ref[...]
    h = (_rms(x) * g_ref[...] * (1.0 + mod_ref[1:2, :]) + mod_ref[0:1, :]).astype(MXU_DT)
    k_chunk = P_K // PROJ_CHUNK
    q_chunk = P_Q // PROJ_CHUNK
    for c in range(IN_COLS // PROJ_CHUNK):
        cols = slice(c * PROJ_CHUNK, (c + 1) * PROJ_CHUNK)
        w0 = (ORIG_GATE0 + c * PROJ_CHUNK) % IN_COLS
        acc = jnp.dot(h, w_ref[:, w0:w0 + PROJ_CHUNK], preferred_element_type=F32)
        if c == k_chunk:
            acc = _rope(acc, cos_ref[...], sin_ref[...])
            kt_ref[...] = acc.T.astype(kt_ref.dtype)
        elif c == q_chunk:
            acc = _rope(acc, cos_ref[...], sin_ref[...]) * (DK ** -0.5 * LOG2_E)
        o_ref[:, cols] = acc.astype(o_ref.dtype)


def _inproj(xall, g1, mod, cos, sin_signed, w_in, layer, tiles):
    r, d = xall.shape
    tm = ROW_TILE
    tps = tiles.tps
    assert ORIG_GATE0 % PROJ_CHUNK == 0 and not tiles.skip_ctx

    def mod_idx(i):
        return (tiles.mod_row(i), 0, 0)

    return pl.pallas_call(
        _inproj_kernel,
        grid=(r // tm,),
        in_specs=[
            pl.BlockSpec((tm, d), lambda i: (i, 0)),
            pl.BlockSpec((1, d), lambda i: (0, 0)),
            pl.BlockSpec((None, N_MOD, d), mod_idx),
            pl.BlockSpec((tm, LANE), lambda i: (i % tps, 0)),
            pl.BlockSpec((tm, LANE), lambda i: (i % tps, 0)),
            pl.BlockSpec((None, d, IN_COLS), lambda i: (layer, 0, 0)),
        ],
        out_specs=[pl.BlockSpec((tm, IN_COLS), lambda i: (i, 0)),
                   pl.BlockSpec((PROJ_CHUNK, tm), lambda i: (0, i))],
        out_shape=[jax.ShapeDtypeStruct((r, IN_COLS), ACT_DT), jax.ShapeDtypeStruct((PROJ_CHUNK, r), ACT_DT)],
        compiler_params=_cparams("parallel"),
        name="inproj",
    )(xall, g1, mod, cos, sin_signed, w_in)


def _attn_kernel(dl_ref, g_ref, q_ref, kt_ref, v_ref, o_ref, vext_ref, s_ref, *, lam_init, ctx_rows, has_ctx):
    tq = s_ref.shape[1]
    seq = kt_ref.shape[1]
    one_hot = lax.broadcasted_iota(jnp.int32, v_ref.shape, 1) == 0
    vext_ref[:, :DV] = v_ref[...]
    vext_ref[:, DV:] = jnp.where(one_hot, 1.0, 0.0).astype(vext_ref.dtype)

    dl = dl_ref[...]
    lam = (jnp.exp(jnp.sum(dl[0:1] * dl[1:2], keepdims=True))
           - jnp.exp(jnp.sum(dl[2:3] * dl[3:4], keepdims=True)) + lam_init)
    kc = ATTN_KV_CHUNK

    def attend_rows(kv_rows, q_rows, out_rows):
        q = q_ref[q_rows, :]
        lane = lax.broadcasted_iota(jnp.int32, q.shape, 1)
        zero = jnp.zeros_like(q)
        qms = (jnp.where(lane < DK, q, zero), jnp.where(lane >= DK, q, zero))
        chunks = [slice(c * kc, (c + 1) * kc) for c in range(kv_rows // kc)]

        def scores(m, cols):
            return jnp.dot(qms[m], kt_ref[:, cols], preferred_element_type=F32)

        mxs = []
        for m in range(2):
            mrun = None
            for cols in chunks:
                s = scores(m, cols)
                s_ref[m, :, cols] = s
                for g in range(kc // LANE):
                    blk = s[:, g * LANE:(g + 1) * LANE]
                    mrun = blk if mrun is None else jnp.maximum(mrun, blk)
            mxs.append(jnp.max(mrun, axis=-1, keepdims=True))
        accs = [jnp.zeros((q.shape[0], 2 * DV), F32)] * 2
        for cols in chunks:
            for m in range(2):
                p = jnp.exp2(s_ref[m, :, cols] - mxs[m])
                accs[m] = accs[m] + jnp.dot(p.astype(MXU_DT), vext_ref[cols, :], preferred_element_type=F32)
        o = (accs[0][:, :DV] * (1.0 / accs[0][:, DV:DV + 1])
             - accs[1][:, :DV] * (lam / accs[1][:, DV:DV + 1]))
        o_ref[out_rows, :] = (_rms(o) * g_ref[...] * (1.0 - lam_init)).astype(o_ref.dtype)

    out_off = 0
    if has_ctx:
        attend_rows(ctx_rows, slice(0, ctx_rows), slice(0, ctx_rows))
        out_off = ctx_rows

    def latent_tile(i, carry):
        q0 = pl.multiple_of(ctx_rows + i * tq, tq)
        o0 = pl.multiple_of(out_off + i * tq, tq)
        attend_rows(seq, pl.ds(q0, tq), pl.ds(o0, tq))
        return carry

    lax.fori_loop(0, (seq - ctx_rows) // tq, latent_tile, 0)


def _attention(p, k_t, diff_lambda, g_sub, lam_init, tiles, ctx_rows):
    tq = ROW_TILE
    tps = tiles.tps
    seq = tps * tq
    out_rows = tiles.per_seq * tq
    assert ctx_rows == tq
    kern = functools.partial(_attn_kernel, lam_init=lam_init, ctx_rows=ctx_rows, has_ctx=not tiles.skip_ctx)
    return pl.pallas_call(
        kern,
        grid=(tiles.n_batch, H_A),
        in_specs=[
            pl.BlockSpec((4, DK), lambda b, h: (0, 0)),
            pl.BlockSpec((1, DV), lambda b, h: (0, 0)),
            pl.BlockSpec((seq, LANE), lambda b, h: (b, P_Q // LANE + h)),
            pl.BlockSpec((LANE, seq), lambda b, h: (h, b)),
            pl.BlockSpec((seq, LANE), lambda b, h: (b, P_V // LANE + h)),
        ],
        out_specs=pl.BlockSpec((out_rows, LANE), lambda b, h: (b, h)),
        out_shape=jax.ShapeDtypeStruct((tiles.count * tq, H_A * DV), ACT_DT),
        scratch_shapes=[pltpu.VMEM((seq, 2 * DV), MXU_DT), pltpu.VMEM((2, tq, seq), F32)],
        compiler_params=_cparams("parallel", "parallel"),
        name="diff_attention",
    )(diff_lambda, g_sub, p, k_t, p)


def _gmlp_kernel(uv_ref, gv_ref, ws_ref, bs_ref, o_ref):
    uv = jax.nn.gelu(uv_ref[...])
    u = uv[:, :W_B].astype(F32)
    v = (_rms(uv[:, W_B:].astype(F32)) * gv_ref[...]).astype(MXU_DT)
    gw = W_B // G_B
    for c in range(uv.shape[0] // CHUNK_B):
        rows = slice(c * CHUNK_B, (c + 1) * CHUNK_B)
        for g in range(G_B):
            cols = slice(g * gw, (g + 1) * gw)
            mixed = jnp.dot(ws_ref[g], v[rows, cols], preferred_element_type=F32) + bs_ref[:, cols]
            o_ref[rows, cols] = (u[rows, cols] * mixed).astype(o_ref.dtype)


def _gmlp(p, g_v, w_s, b_full, tiles):
    tm = ROW_TILE
    r = tiles.count * tm
    return pl.pallas_call(
        _gmlp_kernel,
        grid=(tiles.count,),
        in_specs=[
            pl.BlockSpec((tm, 2 * W_B), lambda i: (tiles.full(i), P_UV // (2 * W_B))),
            pl.BlockSpec((1, W_B), lambda i: (0, 0)),
            pl.BlockSpec((G_B, CHUNK_B, CHUNK_B), lambda i: (0, 0, 0)),
            pl.BlockSpec((CHUNK_B, W_B), lambda i: (0, 0)),
        ],
        out_specs=pl.BlockSpec((tm, W_B), lambda i: (i, 0)),
        out_shape=jax.ShapeDtypeStruct((r, W_B), ACT_DT),
        compiler_params=_cparams("parallel"),
        name="gmlp",
    )(p, g_v, w_s, b_full)


def _mfeat_kernel(x_ref, prev_ref, next_ref, wc_ref, bc_ref, wqkv_ref, wif_ref, bif_ref,
                  xc_ref, q_ref, k_ref, v_ref, g_ref, *, tps):
    tm = x_ref.shape[0]
    j = pl.program_id(0) % tps
    seg_start = jnp.logical_or(j == 0, j == 1)
    seg_end = jnp.logical_or(j == 0, j == tps - 1)
    x = x_ref[...].astype(F32)
    prow = jnp.where(seg_start, 0.0, prev_ref[...].astype(F32)[HALO - 1:HALO, :])
    nrow = jnp.where(seg_end, 0.0, next_ref[...].astype(F32)[0:1, :])
    rid = lax.broadcasted_iota(jnp.int32, x.shape, 0)
    xp = jnp.where(rid == 0, prow, pltpu.roll(x, 1, 0))
    xn = jnp.where(rid == tm - 1, nrow, pltpu.roll(x, tm - 1, 0))
    xc = wc_ref[0:1, :] * xp + wc_ref[1:2, :] * x + wc_ref[2:3, :] * xn + bc_ref[...]
    xc = xc * _sigmoid(xc)
    xc_ref[...] = xc.astype(xc_ref.dtype)
    xcb = xc.astype(MXU_DT)
    xb = x_ref[...]
    qs, ks, vs = [], [], []
    for h in range(H_C):
        cols = slice(h * DH_C, (h + 1) * DH_C)
        qs.append(jnp.dot(xcb[:, cols], wqkv_ref[0, h], preferred_element_type=F32))
        ks.append(jnp.dot(xcb[:, cols], wqkv_ref[1, h], preferred_element_type=F32))
        vs.append(jnp.dot(xb[:, cols], wqkv_ref[2, h], preferred_element_type=F32))
    q = jnp.concatenate(qs, axis=1)
    k = jnp.concatenate(ks, axis=1)
    v = jnp.concatenate(vs, axis=1)
    qkv = jnp.concatenate([q, k, v], axis=1).astype(MXU_DT)
    gates = jnp.dot(qkv, wif_ref[...], preferred_element_type=F32) + bif_ref[...]
    g_ref[...] = gates.T[0:g_ref.shape[0], :]
    q_ref[...] = q.T.astype(q_ref.dtype)
    k_ref[...] = (k * (DH_C ** -0.5)).astype(k_ref.dtype)
    v_ref[...] = v.T.astype(v_ref.dtype)


def _mlstm_features(p, w_conv, b_conv, w_qkv, w_if_pad, b_if_pad, tps):
    r = p.shape[0]
    tm = ROW_TILE
    hpt = tm // HALO
    xm_blk = P_XM // W_C
    last_halo = r // HALO - 1
    kern = functools.partial(_mfeat_kernel, tps=tps)
    act = jax.ShapeDtypeStruct((r, W_C), ACT_DT)
    act_t = jax.ShapeDtypeStruct((W_C, r), ACT_DT)
    row_spec = pl.BlockSpec((tm, W_C), lambda i: (i, 0))
    col_spec = pl.BlockSpec((W_C, tm), lambda i: (0, i))
    return pl.pallas_call(
        kern,
        grid=(r // tm,),
        in_specs=[
            pl.BlockSpec((tm, W_C), lambda i: (i, xm_blk)),
            pl.BlockSpec((HALO, W_C), lambda i: (jnp.maximum(i * hpt - 1, 0), xm_blk)),
            pl.BlockSpec((HALO, W_C), lambda i: (jnp.minimum((i + 1) * hpt, last_halo), xm_blk)),
            pl.BlockSpec((CONV_K, W_C), lambda i: (0, 0)),
            pl.BlockSpec((1, W_C), lambda i: (0, 0)),
            pl.BlockSpec((3, H_C, DH_C, DH_C), lambda i: (0, 0, 0, 0)),
            pl.BlockSpec((3 * W_C, LANE), lambda i: (0, 0)),
            pl.BlockSpec((1, LANE), lambda i: (0, 0)),
        ],
        out_specs=[row_spec, col_spec, row_spec, col_spec,
                   pl.BlockSpec((2 * GATE_ROWS, tm), lambda i: (0, i))],
        out_shape=[act, act_t, act, act_t, jax.ShapeDtypeStruct((2 * GATE_ROWS, r), F32)],
        compiler_params=_cparams("parallel"),
        name="mlstm_features",
    )(p, p, p, w_conv, b_conv, w_qkv, w_if_pad, b_if_pad)


GATE_ROWS = 2 * H_C
GATE_PACK = ("r", "cmx", "bcum")


def _gate_prep_kernel(g_ref, pack_ref, rt_ref):
    seg = MLSTM_CHUNK
    gt_fwd = g_ref[0:GATE_ROWS, :]
    gt_bwd = g_ref[GATE_ROWS:2 * GATE_ROWS, :]
    n = gt_fwd.shape[1]
    lane = lax.broadcasted_iota(jnp.int32, gt_fwd.shape, 1)
    in_seg = lane & (seg - 1)

    def scan_both(xf, xb, op):
        sh = 1
        while sh < seg:
            xf = jnp.where(in_seg >= sh, op(xf, pltpu.roll(xf, sh, 1)), xf)
            xb = jnp.where(in_seg < seg - sh, op(xb, pltpu.roll(xb, n - sh, 1)), xb)
            sh *= 2
        return xf, xb

    bcum_f, bcum_b = scan_both(jax.nn.log_sigmoid(pltpu.roll(gt_fwd, H_C, 0)),
                               jax.nn.log_sigmoid(pltpu.roll(gt_bwd, H_C, 0)), jnp.add)
    r_f, r_b = gt_fwd - bcum_f, gt_bwd - bcum_b
    cmx_f, cmx_b = scan_both(r_f, r_b, jnp.maximum)
    pack_ref[...] = jnp.concatenate([r_f, cmx_f, bcum_f, r_b, cmx_b, bcum_b], axis=0)
    rt_ref[...] = jnp.concatenate([r_f, r_b, jnp.zeros((LANE - 2 * GATE_ROWS, n), F32)], axis=0).T


def _mlstm_gate_prep(gates_t, n_batch, seq):
    rows = 2 * len(GATE_PACK) * GATE_ROWS
    r = gates_t.shape[1]
    return pl.pallas_call(
        _gate_prep_kernel,
        grid=(n_batch,),
        in_specs=[pl.BlockSpec((2 * GATE_ROWS, seq), lambda b: (0, b))],
        out_specs=[pl.BlockSpec((rows, seq), lambda b: (0, b)), pl.BlockSpec((seq, LANE), lambda b: (b, 0))],
        out_shape=[jax.ShapeDtypeStruct((rows, r), F32), jax.ShapeDtypeStruct((r, LANE), F32)],
        compiler_params=_cparams("parallel"),
        name="mlstm_gate_prep",
    )(gates_t)


def _mlstm_gates(pack, m_ref, reverse):
    ln = pack.shape[1]
    part = {name: pack[i * GATE_ROWS:(i + 1) * GATE_ROWS] for i, name in enumerate(GATE_PACK)}
    end = slice(0, 1) if reverse else slice(ln - 1, ln)
    g = part["bcum"][:, end]
    r_max = part["cmx"][:, end]
    m_loc = g + r_max
    w_end = jnp.exp(part["r"] - r_max)
    m_old = m_ref[:, 0:1]
    m_new = jnp.maximum(g + m_old, m_loc)
    a_old = jnp.exp(g + m_old - m_new)
    a_new = jnp.exp(m_loc - m_new)
    mx = jnp.maximum(m_old, part["cmx"])
    w_inter = jnp.exp(m_old - mx)
    e_neg = jnp.exp(-(part["bcum"] + mx))
    s_id = lax.broadcasted_iota(jnp.int32, (ln, ln), 0)
    t_id = lax.broadcasted_iota(jnp.int32, (ln, ln), 1)
    keep = (s_id >= t_id) if reverse else (s_id <= t_id)
    return dict(mx=mx, w_inter=w_inter, e_neg=e_neg, w_end=w_end, a_old=a_old, a_new=a_new,
                m_new=m_new, keep=keep)


def _mscan_chunk(gf_ref, rf_ref, qf_ref, kf_ref, vf_ref, gb_ref, rb_ref, qb_ref, kb_ref, vb_ref,
                 hf_ref, hb_ref, cf_ref, mf_ref, cb_ref, mb_ref):
    ln = qf_ref.shape[1]
    pr = len(GATE_PACK) * GATE_ROWS
    dirs = ((_mlstm_gates(gf_ref[0:pr, :], mf_ref, False), qf_ref, kf_ref, vf_ref, cf_ref, hf_ref, rf_ref),
            (_mlstm_gates(gb_ref[pr:2 * pr, :], mb_ref, True), qb_ref, kb_ref, vb_ref, cb_ref, hb_ref, rb_ref))
    pairs = [(d, h) for d in range(2) for h in range(H_C)]
    ones_row = jnp.where(lax.broadcasted_iota(jnp.int32, (DH_C, ln), 0) == 0, 1.0, 0.0).astype(MXU_DT)
    head = lambda h: slice(h * DH_C, (h + 1) * DH_C)
    row = lambda h: slice(h, h + 1)

    ks = [dirs[d][2][:, head(h)] for d, h in pairs]
    qts = [dirs[d][1][head(h), :] for d, h in pairs]
    vexts = [jnp.concatenate([dirs[d][3][head(h), :], ones_row], axis=0) for d, h in pairs]
    cexts = [dirs[d][4][h] for d, h in pairs]
    qks = [jnp.dot(k, q_t, preferred_element_type=F32) for k, q_t in zip(ks, qts)]
    rhss = []
    for i, (d, h) in enumerate(pairs):
        g = dirs[d][0]
        r_col = dirs[d][6][:, d * GATE_ROWS + h:d * GATE_ROWS + h + 1]
        d_t = jnp.exp(jnp.where(g["keep"], r_col - g["mx"][row(h), :], NEG_BIG))
        wq = (qts[i].astype(F32) * g["w_inter"][row(h), :]).astype(MXU_DT)
        rhss.append(jnp.concatenate([(qks[i] * d_t).astype(MXU_DT), wq], axis=0))
    ress = [jnp.dot(jnp.concatenate([vexts[i], cexts[i].astype(MXU_DT)], axis=1), rhss[i],
                    preferred_element_type=F32) for i in range(len(pairs))]
    for i, (d, h) in enumerate(pairs):
        g = dirs[d][0]
        den = ress[i][DH_C:DH_C + 1, :]
        hh = ress[i][:DH_C, :] / jnp.maximum(jnp.abs(den), g["e_neg"][row(h), :])
        dirs[d][5][head(h), :] = hh.astype(dirs[d][5].dtype)
    clocs = []
    for i, (d, h) in enumerate(pairs):
        vw_t = (vexts[i].astype(F32) * dirs[d][0]["w_end"][row(h), :]).astype(MXU_DT)
        clocs.append(jnp.dot(vw_t, ks[i], preferred_element_type=F32))
    for i, (d, h) in enumerate(pairs):
        g = dirs[d][0]
        dirs[d][4][h] = g["a_old"][row(h), :] * cexts[i] + g["a_new"][row(h), :] * clocs[i]
    mf_ref[...] = jnp.broadcast_to(dirs[0][0]["m_new"], mf_ref.shape)
    mb_ref[...] = jnp.broadcast_to(dirs[1][0]["m_new"], mb_ref.shape)


def _mscan_kernel(g_ref, r_ref, q_ref, k_ref, v_ref, hf_ref, hb_ref, cf_ref, mf_ref, cb_ref, mb_ref, *, ncc):
    ln = MLSTM_CHUNK
    nck = q_ref.shape[1] // ln
    cf_ref[...] = jnp.zeros_like(cf_ref)
    cb_ref[...] = jnp.zeros_like(cb_ref)
    mf_ref[...] = jnp.zeros_like(mf_ref)
    mb_ref[...] = jnp.zeros_like(mb_ref)

    def chunk_views(c):
        at = pl.ds(pl.multiple_of(c * ln, ln), ln)
        return ((g_ref.at[:, at], r_ref.at[at, :], q_ref.at[:, at], k_ref.at[at, :], v_ref.at[:, at]),
                (hf_ref.at[:, at], hb_ref.at[:, at]))

    def trip(s, carry):
        ins_f, (hf, _) = chunk_views(s)
        ins_b, (_, hb) = chunk_views(jnp.where(s < ncc, ncc - 1 - s, nck - 1 + ncc - s))
        _mscan_chunk(*ins_f, *ins_b, hf, hb, cf_ref, mf_ref, cb_ref, mb_ref)
        return carry

    lax.fori_loop(0, nck, trip, 0)


def _mlstm_scan(q_t, k, v_t, gate_pack, r_t, n_batch, seq, ctx_rows):
    rows = lambda w: pl.BlockSpec((seq, w), lambda b: (b, 0))
    cols = lambda h: pl.BlockSpec((h, seq), lambda b: (0, b))
    out = jax.ShapeDtypeStruct(q_t.shape, ACT_DT)
    return pl.pallas_call(
        functools.partial(_mscan_kernel, ncc=ctx_rows // MLSTM_CHUNK),
        grid=(n_batch,),
        in_specs=[cols(gate_pack.shape[0]), rows(LANE), cols(W_C), rows(W_C), cols(W_C)],
        out_specs=[cols(W_C), cols(W_C)],
        out_shape=[out, out],
        scratch_shapes=[
            pltpu.VMEM((H_C, 2 * DH_C, DH_C), F32), pltpu.VMEM((8, LANE), F32),
            pltpu.VMEM((H_C, 2 * DH_C, DH_C), F32), pltpu.VMEM((8, LANE), F32),
        ],
        compiler_params=_cparams("parallel"),
        name="mlstm_scan",
    )(gate_pack, r_t, q_t, k, v_t)


def _route(logits, carry):
    lane = lax.broadcasted_iota(jnp.int32, logits.shape, 1).astype(F32)
    big = float(4 * LANE)
    gl = jnp.where(lane < N_GROUPS, logits, NEG_BIG)
    gmax = jnp.max(gl, axis=-1, keepdims=True)
    g_star = jnp.min(jnp.where(gl == gmax, lane, big), axis=-1, keepdims=True)
    p_g = 1.0 / jnp.sum(jnp.exp(gl - gmax), axis=-1, keepdims=True)
    e_lo = g_star * EXP_PER_GROUP + N_GROUPS
    in_group = jnp.logical_and(lane >= e_lo, lane < e_lo + EXP_PER_GROUP)
    el = jnp.where(in_group, logits, NEG_BIG)
    v1 = jnp.max(el, axis=-1, keepdims=True)
    i1 = jnp.min(jnp.where(el == v1, lane, big), axis=-1, keepdims=True)
    el2 = jnp.where(lane == i1, NEG_BIG, el)
    v2 = jnp.max(el2, axis=-1, keepdims=True)
    i2 = jnp.min(jnp.where(el2 == v2, lane, big), axis=-1, keepdims=True)
    e21 = jnp.exp(v2 - v1)
    w1 = p_g / (1.0 + e21)
    w2 = p_g * e21 / (1.0 + e21)
    e1 = i1 - N_GROUPS
    e2 = i2 - N_GROUPS
    oh1 = lane == e1
    oh2 = lane == e2
    both = jnp.where(jnp.logical_or(oh1, oh2), 1.0, 0.0)
    tm = logits.shape[0]
    lower = (lax.broadcasted_iota(jnp.int32, (tm, tm), 1) < lax.broadcasted_iota(jnp.int32, (tm, tm), 0))
    before = jnp.dot(jnp.where(lower, 1.0, 0.0).astype(MXU_DT), both.astype(MXU_DT),
                     preferred_element_type=F32) + carry
    rank1 = jnp.sum(jnp.where(oh1, before, 0.0), axis=-1, keepdims=True)
    rank2 = jnp.sum(jnp.where(oh2, before, 0.0), axis=-1, keepdims=True)
    out = jnp.where(lane == 0, e1, 0.0)
    out = jnp.where(lane == 1, e2, out)
    out = jnp.where(lane == 2, w1, out)
    out = jnp.where(lane == 3, w2, out)
    out = jnp.where(lane == 4, rank1, out)
    out = jnp.where(lane == 5, rank2, out)
    return out, carry + jnp.sum(both, axis=0, keepdims=True)


def _merge_kernel(x_ref, gate_ref, z_ref, ya_ref, yb_ref, hf_ref, hb_ref, xc_ref, gm_ref, sk_ref,
                  wbr_ref, wo_ref, mod_ref, g2_ref, wr_ref, br_ref, xo_ref, rt_ref, cnt_ref, carry_ref):
    @pl.when(pl.program_id(0) == 0)
    def _():
        carry_ref[...] = jnp.zeros_like(carry_ref)

    hs_t = hf_ref[...].astype(F32) + hb_ref[...].astype(F32)
    hn_t = []
    for h in range(H_C):
        part = hs_t[h * DH_C:(h + 1) * DH_C, :]
        hn_t.append(part * lax.rsqrt(jnp.mean(part * part, axis=0, keepdims=True) + EPS))
    hn = jnp.concatenate(hn_t, axis=0).T * gm_ref[...]
    yc = (hn + sk_ref[...] * xc_ref[...].astype(F32)) * _sigmoid(z_ref[...].astype(F32))
    ys = (ya_ref[...], yb_ref[...], yc.astype(MXU_DT))
    merged = None
    for i in range(3):
        gate = _sigmoid(gate_ref[:, i * D_MODEL:(i + 1) * D_MODEL].astype(MXU_DT))
        term = gate * jnp.dot(ys[i], wbr_ref[i], preferred_element_type=F32).astype(MXU_DT)
        merged = term if merged is None else merged + term
    y = jnp.dot(merged, wo_ref[...], preferred_element_type=F32)
    xn = x_ref[...] + mod_ref[2:3, :] * y
    xo_ref[...] = xn
    h2 = _rms(xn) * g2_ref[...] * (1.0 + mod_ref[4:5, :]) + mod_ref[3:4, :]
    h_hi, h_lo = _split_bf16(h2)
    logits = (jnp.dot(h_hi, wr_ref[0], preferred_element_type=F32)
              + jnp.dot(h_lo, wr_ref[0], preferred_element_type=F32)
              + jnp.dot(h_hi, wr_ref[1], preferred_element_type=F32)) + br_ref[...]
    route, carry = _route(logits, carry_ref[0:1, :])
    rt_ref[...] = route
    carry_ref[0:1, :] = carry
    cnt_ref[...] = jnp.broadcast_to(carry, cnt_ref.shape)


def _merge(xall, p, ya, yb, hf, hb, xconv, g_m, skip, w_br, w_o, mod, g2, w_route, b_route, tiles):
    d = xall.shape[1]
    tm = ROW_TILE
    r = tiles.count * tm

    def mod_idx(i):
        return (tiles.mod_row(i), 0, 0)

    row = lambda w: pl.BlockSpec((tm, w), lambda i: (i, 0))
    frow = lambda w: pl.BlockSpec((tm, w), lambda i: (tiles.full(i), 0))
    fcol = lambda w: pl.BlockSpec((w, tm), lambda i: (0, tiles.full(i)))
    full = lambda shape: pl.BlockSpec(shape, lambda i: (0,) * len(shape))
    return pl.pallas_call(
        _merge_kernel,
        grid=(tiles.count,),
        in_specs=[
            frow(d),
            pl.BlockSpec((tm, GATE_W), lambda i: (tiles.full(i), P_GATE // GATE_W)),
            pl.BlockSpec((tm, W_C), lambda i: (tiles.full(i), P_Z // W_C)),
            row(W_BR), row(W_BR), fcol(W_C), fcol(W_C), frow(W_C),
            full((1, W_C)), full((1, W_C)),
            full((3, W_BR, d)), full((d, d)),
            pl.BlockSpec((None, N_MOD, d), mod_idx),
            full((1, d)), full((2, d, LANE)), full((1, LANE)),
        ],
        out_specs=[row(d), row(LANE), full((8, LANE))],
        out_shape=[jax.ShapeDtypeStruct((r, d), F32), jax.ShapeDtypeStruct((r, LANE), F32),
                   jax.ShapeDtypeStruct((8, LANE), F32)],
        scratch_shapes=[pltpu.VMEM((8, LANE), F32)],
        compiler_params=_cparams("arbitrary"),
        name="merge_route",
    )(xall, p, p, ya, yb, hf, hb, xconv, g_m, skip, w_br, w_o, mod, g2, w_route, b_route)


def _row_copy(src_hbm, src_row, dst_ref, dst_row, sem):
    return pltpu.make_async_copy(src_hbm.at[pl.ds(src_row, 1), :], dst_ref.at[pl.ds(dst_row, 1), :], sem)


def _dispatch_kernel(fill_ref, nlive_ref, pos_ref, x_ref, g2_ref, mod_ref, xs_hbm, hbuf, zbuf, zsem, sem):
    i = pl.program_id(0)
    n = pl.num_programs(0)
    tm = x_ref.shape[0]
    slot = i % 2
    bm = zbuf.shape[0]
    n_blocks = xs_hbm.shape[0] // bm

    @pl.when(i == 0)
    def _():
        zbuf[...] = jnp.zeros_like(zbuf)

        def fill_block(b):
            return pltpu.make_async_copy(zbuf, xs_hbm.at[pl.ds(b * bm, bm), :], zsem)

        def fill_last(k, carry):
            fill_block(fill_ref[k]).start()
            return carry

        def fill_unused(b, carry):
            fill_block(b).start()
            return carry

        def wait_fill(k, carry):
            fill_block(0).wait()
            return carry

        n_live = nlive_ref[0]
        n_used = nlive_ref[1]
        lax.fori_loop(0, n_used, fill_last, 0)
        lax.fori_loop(n_live, n_blocks, fill_unused, 0)
        lax.fori_loop(0, n_used + n_blocks - n_live, wait_fill, 0)

    hbuf[slot] = _rms(x_ref[...]) * g2_ref[...] * (1.0 + mod_ref[4:5, :]) + mod_ref[3:4, :]

    def body(c, carry):
        for u in range(DMA_UNROLL):
            r = c * DMA_UNROLL + u
            for j in range(TOP_K):
                pltpu.make_async_copy(hbuf.at[slot, pl.ds(r, 1), :],
                                      xs_hbm.at[pl.ds(pos_ref[0, j * tm + r], 1), :], sem.at[slot]).start()
        return carry

    lax.fori_loop(0, tm // DMA_UNROLL, body, 0)

    def wait_tile(s):
        pltpu.make_async_copy(hbuf.at[s], xs_hbm.at[pl.ds(0, tm), :], sem.at[s]).wait()
        pltpu.make_async_copy(hbuf.at[s], xs_hbm.at[pl.ds(0, tm), :], sem.at[s]).wait()

    @pl.when(i > 0)
    def _():
        wait_tile(1 - slot)

    @pl.when(i == n - 1)
    def _():
        wait_tile(slot)


def _moe_dispatch(xall, g2, mod, fill_blocks, n_live, pos_tiles, n_slots, tiles):
    r, d = xall.shape
    tm = ROW_TILE
    assert TOP_K == 2 and r == tiles.count * tm

    def mod_idx(i, fb, nl):
        return (tiles.mod_row(i), 0, 0)

    grid_spec = pltpu.PrefetchScalarGridSpec(
        num_scalar_prefetch=2,
        grid=(r // tm,),
        in_specs=[
            pl.BlockSpec((None, 1, TOP_K * tm), lambda i, fb, nl: (i, 0, 0), memory_space=pltpu.SMEM),
            pl.BlockSpec((tm, d), lambda i, fb, nl: (i, 0)),
            pl.BlockSpec((1, d), lambda i, fb, nl: (0, 0)),
            pl.BlockSpec((None, N_MOD, d), mod_idx),
        ],
        out_specs=pl.BlockSpec(memory_space=pl.ANY),
        scratch_shapes=[pltpu.VMEM((2, tm, d), F32), pltpu.VMEM((MOE_BLOCK, d), F32),
                        pltpu.SemaphoreType.DMA(()), pltpu.SemaphoreType.DMA((2,))],
    )
    return pl.pallas_call(
        _dispatch_kernel,
        grid_spec=grid_spec,
        out_shape=jax.ShapeDtypeStruct((n_slots, d), F32),
        compiler_params=_cparams("arbitrary"),
        name="moe_dispatch",
    )(fill_blocks, n_live, pos_tiles, xall, g2, mod)


def _moe_kernel(be_ref, nv_ref, x_ref, wg_ref, wu_ref, wd_ref, y_ref):
    n_valid = nv_ref[pl.program_id(0)]

    @pl.when(n_valid > 0)
    def _():
        x = x_ref[...].astype(MXU_DT)
        a = jnp.dot(x, wg_ref[...], preferred_element_type=F32)
        u = jnp.dot(x, wu_ref[...], preferred_element_type=F32)
        act = (a * _sigmoid(a) * u).astype(MXU_DT)
        y_ref[...] = jnp.dot(act, wd_ref[...], preferred_element_type=F32)

    @pl.when(n_valid <= 0)
    def _():
        y_ref[...] = jnp.zeros_like(y_ref)


def _moe_experts(xs, block_e, n_valid, w_gate, w_up, w_down, layer):
    n_slots, d = xs.shape
    bm = MOE_BLOCK
    grid_spec = pltpu.PrefetchScalarGridSpec(
        num_scalar_prefetch=2,
        grid=(n_slots // bm,),
        in_specs=[
            pl.BlockSpec((bm, d), lambda i, be, nv: (i, 0)),
            pl.BlockSpec((None, None, d, D_EXPERT), lambda i, be, nv: (layer, be[i], 0, 0)),
            pl.BlockSpec((None, None, d, D_EXPERT), lambda i, be, nv: (layer, be[i], 0, 0)),
            pl.BlockSpec((None, None, D_EXPERT, d), lambda i, be, nv: (layer, be[i], 0, 0)),
        ],
        out_specs=pl.BlockSpec((bm, d), lambda i, be, nv: (i, 0)),
    )
    return pl.pallas_call(
        _moe_kernel,
        grid_spec=grid_spec,
        out_shape=jax.ShapeDtypeStruct((n_slots, d), F32),
        compiler_params=_cparams("arbitrary"),
        name="moe_experts",
    )(block_e, n_valid, xs, w_gate, w_up, w_down)


def _combine_kernel(pos_ref, pos_next_ref, x_ref, rt_ref, mod_ref, gf_ref, yb_hbm, xo_ref, ybuf, sem, *, final):
    i = pl.program_id(0)
    n = pl.num_programs(0)
    tm = x_ref.shape[0]

    def start_gather(ids_ref, slot):
        def body(c, carry):
            for u in range(DMA_UNROLL):
                r = c * DMA_UNROLL + u
                _row_copy(yb_hbm, ids_ref[0, r], ybuf.at[slot], r, sem.at[slot]).start()
            return carry
        lax.fori_loop(0, TOP_K * tm // DMA_UNROLL, body, 0)

    @pl.when(i == 0)
    def _():
        start_gather(pos_ref, 0)

    @pl.when(i + 1 < n)
    def _():
        start_gather(pos_next_ref, (i + 1) % 2)

    slot = i % 2
    pltpu.make_async_copy(yb_hbm.at[pl.ds(0, TOP_K * tm), :], ybuf.at[slot], sem.at[slot]).wait()
    rt = rt_ref[...]
    f = rt[:, 2:3] * ybuf[slot, 0:tm, :] + rt[:, 3:4] * ybuf[slot, tm:2 * tm, :]
    xn = x_ref[...] + mod_ref[5:6, :] * f
    xo_ref[...] = _rms(xn) * gf_ref[...] if final else xn


def _combine(xall, route, mod, g_final, yb, pos, tiles, final):
    r, d = xall.shape
    tm = ROW_TILE
    nt = tiles.count
    last = nt - 1
    assert r == nt * tm

    def mod_idx(i):
        return (tiles.mod_row(i), 0, 0)

    row = lambda w: pl.BlockSpec((tm, w), lambda i: (i, 0))
    return pl.pallas_call(
        functools.partial(_combine_kernel, final=final),
        grid=(nt,),
        in_specs=[
            pl.BlockSpec((None, 1, TOP_K * tm), lambda i: (i, 0, 0), memory_space=pltpu.SMEM),
            pl.BlockSpec((None, 1, TOP_K * tm), lambda i: (jnp.minimum(i + 1, last), 0, 0),
                         memory_space=pltpu.SMEM),
            row(d), row(LANE),
            pl.BlockSpec((None, N_MOD, d), mod_idx),
            pl.BlockSpec((1, d), lambda i: (0, 0)),
            pl.BlockSpec(memory_space=pl.ANY),
        ],
        out_specs=row(d),
        out_shape=jax.ShapeDtypeStruct((r, d), F32),
        scratch_shapes=[pltpu.VMEM((2, TOP_K * tm, d), F32), pltpu.SemaphoreType.DMA((2,))],
        compiler_params=_cparams("arbitrary"),
        name="moe_combine",
    )(pos, pos, xall, route, mod, g_final, yb)


def _dispatch_plan(route, counts_f, n_tiles):
    n = route.shape[0]
    bm = MOE_BLOCK
    n_blocks = -(-(n * TOP_K + N_EXPERTS * (bm - 1)) // bm)
    counts = counts_f[:N_EXPERTS].astype(jnp.int32)
    padded = (counts + bm - 1) // bm * bm
    pend = jnp.cumsum(padded)
    seg_off = pend - padded
    eid = route[:, 0:TOP_K].astype(jnp.int32)
    rank = route[:, 4:4 + TOP_K].astype(jnp.int32)
    experts = jnp.arange(N_EXPERTS, dtype=jnp.int32)
    pos = jnp.sum(jnp.where(eid[..., None] == experts, seg_off, 0), axis=-1) + rank
    blk0 = jnp.arange(n_blocks, dtype=jnp.int32) * bm
    block_e = jnp.minimum(jnp.sum((blk0[:, None] >= pend[None, :]).astype(jnp.int32), axis=1), N_EXPERTS - 1)
    seg_end = seg_off + counts
    n_valid = jnp.clip(seg_end[block_e] - blk0, 0, bm).astype(jnp.int32)
    pos_tiles = pos.reshape(n_tiles, n // n_tiles, TOP_K).transpose(0, 2, 1).reshape(n_tiles, 1, -1)
    used = counts > 0
    last_blocks = jnp.sort(jnp.where(used, pend // bm - 1, n_blocks)).astype(jnp.int32)
    n_live = jnp.stack([pend[-1] // bm, jnp.sum(used)]).astype(jnp.int32)
    return block_e.astype(jnp.int32), n_valid, last_blocks, n_live, pos_tiles, n_blocks * bm


def _rope_tables(seq, ctx_rows):
    t = jnp.arange(seq - ctx_rows)
    inv = ROPE_BASE ** (-jnp.arange(0, ROT_AX, 2, dtype=F32) / ROT_AX)
    ang_r = (t // GRID_W).astype(F32)[:, None] * inv
    ang_c = (t % GRID_W).astype(F32)[:, None] * inv
    cos64 = jnp.concatenate([jnp.cos(ang_r)] * 2 + [jnp.cos(ang_c)] * 2, axis=1)
    sin64 = jnp.concatenate([-jnp.sin(ang_r), jnp.sin(ang_r), -jnp.sin(ang_c), jnp.sin(ang_c)], axis=1)
    cos = jnp.concatenate([jnp.ones((ctx_rows, DK), F32), cos64], axis=0)
    sin = jnp.concatenate([jnp.zeros((ctx_rows, DK), F32), sin64], axis=0)
    return jnp.tile(cos, (1, LANE // DK)), jnp.tile(sin, (1, LANE // DK))


def kernel(x, c, ctx, c_ctx, w_mod, b_mod, g_norm1, g_norm2, w_in, diff_lambda, g_diff_subln, w_gmlp_s, b_gmlp_s, g_gmlp_v, w_conv_m, b_conv_m, w_qkv_m, w_if_m, b_if_m, g_mlstm_norm, skip_m, w_branch, w_out, w_route_g, b_route_g, w_route_e, b_route_e, w_e_gate, w_e_up, w_e_down, g_final):
    n_batch, t_lat, d = x.shape
    ctx_rows = ctx.shape[1]
    depth = w_in.shape[0]
    seq = ctx_rows + t_lat
    assert d == D_MODEL and ctx_rows == ROW_TILE and t_lat % ROW_TILE == 0 and t_lat % GRID_W == 0
    tps = seq // ROW_TILE
    all_tiles = _Tiles(n_batch, tps, False)

    xall = jnp.concatenate([ctx, x], axis=1).reshape(n_batch * seq, d)
    mb = -(-(n_batch + 1) // 8) * 8
    c_all = jnp.zeros((mb, d), F32).at[:n_batch].set(c).at[n_batch].set(c_ctx)
    mod_all = _modulation(c_all, w_mod, b_mod)[:, :n_batch + 1].reshape(depth, n_batch + 1, N_MOD, d)
    cos, sin_signed = _rope_tables(seq, ctx_rows)
    w_in_b = w_in.astype(MXU_DT)
    w_eg_b, w_eu_b, w_ed_b = w_e_gate.astype(MXU_DT), w_e_up.astype(MXU_DT), w_e_down.astype(MXU_DT)

    for l in range(depth):
        lam_init = 0.8 - 0.6 * math.exp(-0.3 * l)
        mod = mod_all[l]
        last = l == depth - 1
        tiles = _Tiles(n_batch, tps, last)
        p, k_t = _inproj(xall, g_norm1[l][None], mod, cos, sin_signed, w_in_b, l, all_tiles)

        ya = _attention(p, k_t, diff_lambda[l], g_diff_subln[l][None], lam_init, tiles, ctx_rows)

        b_full = jnp.repeat(b_gmlp_s[l].T, W_B // G_B, axis=1)
        yb = _gmlp(p, g_gmlp_v[l][None], w_gmlp_s[l].astype(MXU_DT), b_full, tiles)

        w_if_pad = jnp.zeros((3 * W_C, LANE), F32).at[:, :4 * H_C].set(w_if_m[l]).astype(MXU_DT)
        b_if_pad = jnp.zeros((1, LANE), F32).at[0, :4 * H_C].set(b_if_m[l])
        xconv, q_m, k_m, v_m, gates_t = _mlstm_features(
            p, w_conv_m[l], b_conv_m[l][None], w_qkv_m[l].astype(MXU_DT), w_if_pad, b_if_pad, tps)
        gate_pack, r_t = _mlstm_gate_prep(gates_t, n_batch, seq)
        hf, hb = _mlstm_scan(q_m, k_m, v_m, gate_pack, r_t, n_batch, seq, ctx_rows)

        w_route = (jnp.zeros((d, LANE), F32).at[:, :N_GROUPS].set(w_route_g[l])
                   .at[:, N_GROUPS:N_GROUPS + N_EXPERTS].set(w_route_e[l]))
        w_route = jnp.stack(_split_bf16(w_route))
        b_route = (jnp.zeros((1, LANE), F32).at[0, :N_GROUPS].set(b_route_g[l])
                   .at[0, N_GROUPS:N_GROUPS + N_EXPERTS].set(b_route_e[l]))
        xall, route, counts = _merge(xall, p, ya, yb, hf, hb, xconv, g_mlstm_norm[l][None], skip_m[l][None],
                                         w_branch[l].astype(MXU_DT), w_out[l].astype(MXU_DT), mod,
                                         g_norm2[l][None], w_route, b_route, tiles)

        block_e, n_valid, fill_blocks, n_live, pos_tiles, n_slots = _dispatch_plan(route, counts[0], tiles.count)
        xs = _moe_dispatch(xall, g_norm2[l][None], mod, fill_blocks, n_live, pos_tiles, n_slots, tiles)
        y_sorted = _moe_experts(xs, block_e, n_valid, w_eg_b, w_eu_b, w_ed_b, l)
        xall = _combine(xall, route, mod, g_final[None], y_sorted, pos_tiles, tiles, last)

    return xall.reshape(n_batch, t_lat, d)
```

```python
import functools
import math
from typing import NamedTuple

import jax
import jax.numpy as jnp
from jax import lax
from jax.experimental import pallas as pl
from jax.experimental.pallas import tpu as pltpu

F32 = jnp.float32
MXU_DT = jnp.bfloat16
ACT_DT = jnp.bfloat16
HIGHEST = lax.Precision.HIGHEST

D_MODEL = 1024
N_MOD = 6
EPS = 1e-6
GRID_W = 64
H_A, DK = 4, 64
DV = 2 * DK
ROT_AX = DK // 2
ROPE_BASE = 10000.0
W_B, G_B, CHUNK_B = 512, 4, 128
H_C, DH_C, CONV_K, MLSTM_CHUNK = 4, 128, 3, 128
W_C = H_C * DH_C
N_GROUPS, EXP_PER_GROUP, TOP_K, D_EXPERT = 4, 8, 2, 512
N_EXPERTS = N_GROUPS * EXP_PER_GROUP
W_BR = 512

GATE_W = 3 * D_MODEL
P_GATE = 0
P_K = GATE_W
P_V = P_K + 512
P_XM = P_V + 512
P_Q = P_XM + 512
P_UV = P_Q + 512
P_Z = P_UV + 2 * W_B
IN_COLS = P_Z + W_C
ORIG_GATE0 = IN_COLS - GATE_W

LANE = 128
ROW_TILE = 256
PROJ_CHUNK = 512
MOE_BLOCK = 512
HALO = 16
DMA_UNROLL = 8
ATTN_KV_CHUNK = 256
LOG2_E = math.log2(math.e)
VMEM_LIMIT = 56 * 1024 * 1024
NEG_BIG = -1e30


class _Tiles(NamedTuple):
    n_batch: int
    tps: int
    skip_ctx: bool

    @property
    def per_seq(self):
        return self.tps - 1 if self.skip_ctx else self.tps

    @property
    def count(self):
        return self.n_batch * self.per_seq

    def full(self, i):
        if not self.skip_ctx:
            return i
        return (i // self.per_seq) * self.tps + 1 + i % self.per_seq

    def mod_row(self, i):
        if self.skip_ctx:
            return i // self.per_seq
        return jnp.where(i % self.tps == 0, self.n_batch, i // self.tps)


def _cparams(*sem):
    return pltpu.CompilerParams(dimension_semantics=sem, vmem_limit_bytes=VMEM_LIMIT)


def _rms(x):
    return x * lax.rsqrt(jnp.mean(x * x, axis=-1, keepdims=True) + EPS)


def _sigmoid(x):
    return 0.5 * jnp.tanh(0.5 * x) + 0.5


def _split_bf16(x):
    hi = x.astype(jnp.bfloat16)
    return hi, (x - hi.astype(F32)).astype(jnp.bfloat16)


def _mod_kernel(c_ref, w_ref, b_ref, o_ref):
    c = c_ref[...]
    s = c * jax.nn.sigmoid(c)
    o_ref[...] = jnp.dot(s, w_ref[...], preferred_element_type=F32, precision=HIGHEST) + b_ref[...]


def _modulation(c_all, w_mod, b_mod):
    n_layer = w_mod.shape[0]
    mb, d = c_all.shape
    tn = 1024
    return pl.pallas_call(
        _mod_kernel,
        grid=(n_layer, N_MOD * d // tn),
        in_specs=[
            pl.BlockSpec((mb, d), lambda l, j: (0, 0)),
            pl.BlockSpec((None, d, tn), lambda l, j: (l, 0, j)),
            pl.BlockSpec((None, 1, tn), lambda l, j: (l, 0, j)),
        ],
        out_specs=pl.BlockSpec((None, mb, tn), lambda l, j: (l, 0, j)),
        out_shape=jax.ShapeDtypeStruct((n_layer, mb, N_MOD * d), F32),
        compiler_params=_cparams("parallel", "parallel"),
        name="modulation",
    )(c_all, w_mod, b_mod.reshape(n_layer, 1, N_MOD * d))


def _rope(acc, cos, sin_signed):
    w = acc.shape[1]
    half = ROT_AX // 2
    lane = lax.broadcasted_iota(jnp.int32, acc.shape, 1)
    partner = jnp.where((lane & half) == 0, pltpu.roll(acc, w - half, 1), pltpu.roll(acc, half, 1))
    reps = w // cos.shape[1]
    return acc * jnp.tile(cos, (1, reps)) + partner * jnp.tile(sin_signed, (1, reps))


def _inproj_kernel(x_ref, g_ref, mod_ref, cos_ref, sin_ref, w_ref, o_ref, kt_ref, *, tps, ctx_memory_only):
    x = x_ref[...]
    h = (_rms(x) * g_ref[...] * (1.0 + mod_ref[1:2, :]) + mod_ref[0:1, :]).astype(MXU_DT)
    k_chunk = P_K // PROJ_CHUNK
    q_chunk = P_Q // PROJ_CHUNK
    chunks = range(IN_COLS // PROJ_CHUNK)
    cols = lambda c: slice(c * PROJ_CHUNK, (c + 1) * PROJ_CHUNK)

    def project(c):
        w0 = (ORIG_GATE0 + c * PROJ_CHUNK) % IN_COLS
        acc = jnp.dot(h, w_ref[:, w0:w0 + PROJ_CHUNK], preferred_element_type=F32)
        if c == k_chunk:
            acc = _rope(acc, cos_ref[...], sin_ref[...])
            kt_ref[...] = acc.T.astype(kt_ref.dtype)
        elif c == q_chunk:
            acc = _rope(acc, cos_ref[...], sin_ref[...]) * (DK ** -0.5 * LOG2_E)
        o_ref[:, cols(c)] = acc.astype(o_ref.dtype)

    if not ctx_memory_only:
        for c in chunks:
            project(c)
        return
    memory = [c for c in chunks if P_K <= c * PROJ_CHUNK < P_Q]
    rest = [c for c in chunks if c not in memory]
    for c in memory:
        project(c)
    is_ctx = pl.program_id(0) % tps == 0

    @pl.when(is_ctx)
    def _():
        for c in rest:
            o_ref[:, cols(c)] = jnp.zeros((o_ref.shape[0], PROJ_CHUNK), o_ref.dtype)

    @pl.when(jnp.logical_not(is_ctx))
    def _():
        for c in rest:
            project(c)


def _inproj(xall, g1, mod, cos, sin_signed, w_in, layer, tiles, ctx_memory_only):
    r, d = xall.shape
    tm = ROW_TILE
    tps = tiles.tps
    assert ORIG_GATE0 % PROJ_CHUNK == 0 and not tiles.skip_ctx

    def mod_idx(i):
        return (tiles.mod_row(i), 0, 0)

    return pl.pallas_call(
        functools.partial(_inproj_kernel, tps=tps, ctx_memory_only=ctx_memory_only),
        grid=(r // tm,),
        in_specs=[
            pl.BlockSpec((tm, d), lambda i: (i, 0)),
            pl.BlockSpec((1, d), lambda i: (0, 0)),
            pl.BlockSpec((None, N_MOD, d), mod_idx),
            pl.BlockSpec((tm, LANE), lambda i: (i % tps, 0)),
            pl.BlockSpec((tm, LANE), lambda i: (i % tps, 0)),
            pl.BlockSpec((None, d, IN_COLS), lambda i: (layer, 0, 0)),
        ],
        out_specs=[pl.BlockSpec((tm, IN_COLS), lambda i: (i, 0)),
                   pl.BlockSpec((PROJ_CHUNK, tm), lambda i: (0, i))],
        out_shape=[jax.ShapeDtypeStruct((r, IN_COLS), ACT_DT), jax.ShapeDtypeStruct((PROJ_CHUNK, r), ACT_DT)],
        compiler_params=_cparams("parallel"),
        name="inproj",
    )(xall, g1, mod, cos, sin_signed, w_in)


def _attn_kernel(dl_ref, g_ref, q_ref, kt_ref, v_ref, o_ref, vext_ref, s_ref, *, lam_init, ctx_rows, has_ctx):
    tq = s_ref.shape[1]
    seq = kt_ref.shape[1]
    one_hot = lax.broadcasted_iota(jnp.int32, v_ref.shape, 1) == 0
    vext_ref[:, :DV] = v_ref[...]
    vext_ref[:, DV:] = jnp.where(one_hot, 1.0, 0.0).astype(vext_ref.dtype)

    dl = dl_ref[...]
    lam = (jnp.exp(jnp.sum(dl[0:1] * dl[1:2], keepdims=True))
           - jnp.exp(jnp.sum(dl[2:3] * dl[3:4], keepdims=True)) + lam_init)
    kc = ATTN_KV_CHUNK

    def attend_rows(kv_rows, q_rows, out_rows):
        q = q_ref[q_rows, :]
        lane = lax.broadcasted_iota(jnp.int32, q.shape, 1)
        zero = jnp.zeros_like(q)
        qms = (jnp.where(lane < DK, q, zero), jnp.where(lane >= DK, q, zero))
        chunks = [slice(c * kc, (c + 1) * kc) for c in range(kv_rows // kc)]

        def scores(m, cols):
            return jnp.dot(qms[m], kt_ref[:, cols], preferred_element_type=F32)

        mxs = []
        for m in range(2):
            mrun = None
            for cols in chunks:
                s = scores(m, cols)
                s_ref[m, :, cols] = s
                for g in range(kc // LANE):
                    blk = s[:, g * LANE:(g + 1) * LANE]
                    mrun = blk if mrun is None else jnp.maximum(mrun, blk)
            mxs.append(jnp.max(mrun, axis=-1, keepdims=True))
        accs = [jnp.zeros((q.shape[0], 2 * DV), F32)] * 2
        for cols in chunks:
            for m in range(2):
                p = jnp.exp2(s_ref[m, :, cols] - mxs[m])
                accs[m] = accs[m] + jnp.dot(p.astype(MXU_DT), vext_ref[cols, :], preferred_element_type=F32)
        o = (accs[0][:, :DV] * (1.0 / accs[0][:, DV:DV + 1])
             - accs[1][:, :DV] * (lam / accs[1][:, DV:DV + 1]))
        o_ref[out_rows, :] = (_rms(o) * g_ref[...] * (1.0 - lam_init)).astype(o_ref.dtype)

    out_off = 0
    if has_ctx:
        attend_rows(ctx_rows, slice(0, ctx_rows), slice(0, ctx_rows))
        out_off = ctx_rows

    def latent_tile(i, carry):
        q0 = pl.multiple_of(ctx_rows + i * tq, tq)
        o0 = pl.multiple_of(out_off + i * tq, tq)
        attend_rows(seq, pl.ds(q0, tq), pl.ds(o0, tq))
        return carry

    lax.fori_loop(0, (seq - ctx_rows) // tq, latent_tile, 0)


def _attention(p, k_t, diff_lambda, g_sub, lam_init, tiles, ctx_rows):
    tq = ROW_TILE
    tps = tiles.tps
    seq = tps * tq
    out_rows = tiles.per_seq * tq
    assert ctx_rows == tq
    kern = functools.partial(_attn_kernel, lam_init=lam_init, ctx_rows=ctx_rows, has_ctx=not tiles.skip_ctx)
    return pl.pallas_call(
        kern,
        grid=(tiles.n_batch, H_A),
        in_specs=[
            pl.BlockSpec((4, DK), lambda b, h: (0, 0)),
            pl.BlockSpec((1, DV), lambda b, h: (0, 0)),
            pl.BlockSpec((seq, LANE), lambda b, h: (b, P_Q // LANE + h)),
            pl.BlockSpec((LANE, seq), lambda b, h: (h, b)),
            pl.BlockSpec((seq, LANE), lambda b, h: (b, P_V // LANE + h)),
        ],
        out_specs=pl.BlockSpec((out_rows, LANE), lambda b, h: (b, h)),
        out_shape=jax.ShapeDtypeStruct((tiles.count * tq, H_A * DV), ACT_DT),
        scratch_shapes=[pltpu.VMEM((seq, 2 * DV), MXU_DT), pltpu.VMEM((2, tq, seq), F32)],
        compiler_params=_cparams("parallel", "parallel"),
        name="diff_attention",
    )(diff_lambda, g_sub, p, k_t, p)


def _gmlp_kernel(uv_ref, gv_ref, ws_ref, bs_ref, o_ref):
    uv = jax.nn.gelu(uv_ref[...])
    u = uv[:, :W_B].astype(F32)
    v = (_rms(uv[:, W_B:].astype(F32)) * gv_ref[...]).astype(MXU_DT)
    gw = W_B // G_B
    for c in range(uv.shape[0] // CHUNK_B):
        rows = slice(c * CHUNK_B, (c + 1) * CHUNK_B)
        for g in range(G_B):
            cols = slice(g * gw, (g + 1) * gw)
            mixed = jnp.dot(ws_ref[g], v[rows, cols], preferred_element_type=F32) + bs_ref[:, cols]
            o_ref[rows, cols] = (u[rows, cols] * mixed).astype(o_ref.dtype)


def _gmlp(p, g_v, w_s, b_full, tiles):
    tm = ROW_TILE
    r = tiles.count * tm
    return pl.pallas_call(
        _gmlp_kernel,
        grid=(tiles.count,),
        in_specs=[
            pl.BlockSpec((tm, 2 * W_B), lambda i: (tiles.full(i), P_UV // (2 * W_B))),
            pl.BlockSpec((1, W_B), lambda i: (0, 0)),
            pl.BlockSpec((G_B, CHUNK_B, CHUNK_B), lambda i: (0, 0, 0)),
            pl.BlockSpec((CHUNK_B, W_B), lambda i: (0, 0)),
        ],
        out_specs=pl.BlockSpec((tm, W_B), lambda i: (i, 0)),
        out_shape=jax.ShapeDtypeStruct((r, W_B), ACT_DT),
        compiler_params=_cparams("parallel"),
        name="gmlp",
    )(p, g_v, w_s, b_full)


def _mfeat_kernel(x_ref, prev_ref, next_ref, wc_ref, bc_ref, wqkv_ref, wif_ref, bif_ref,
                  xc_ref, q_ref, k_ref, v_ref, g_ref, *, tps):
    tm = x_ref.shape[0]
    j = pl.program_id(0) % tps
    seg_start = jnp.logical_or(j == 0, j == 1)
    seg_end = jnp.logical_or(j == 0, j == tps - 1)
    x = x_ref[...].astype(F32)
    prow = jnp.where(seg_start, 0.0, prev_ref[...].astype(F32)[HALO - 1:HALO, :])
    nrow = jnp.where(seg_end, 0.0, next_ref[...].astype(F32)[0:1, :])
    rid = lax.broadcasted_iota(jnp.int32, x.shape, 0)
    xp = jnp.where(rid == 0, prow, pltpu.roll(x, 1, 0))
    xn = jnp.where(rid == tm - 1, nrow, pltpu.roll(x, tm - 1, 0))
    xc = wc_ref[0:1, :] * xp + wc_ref[1:2, :] * x + wc_ref[2:3, :] * xn + bc_ref[...]
    xc = xc * _sigmoid(xc)
    xc_ref[...] = xc.astype(xc_ref.dtype)
    xcb = xc.astype(MXU_DT)
    xb = x_ref[...]
    qs, ks, vs = [], [], []
    for h in range(H_C):
        cols = slice(h * DH_C, (h + 1) * DH_C)
        qs.append(jnp.dot(xcb[:, cols], wqkv_ref[0, h], preferred_element_type=F32))
        ks.append(jnp.dot(xcb[:, cols], wqkv_ref[1, h], preferred_element_type=F32))
        vs.append(jnp.dot(xb[:, cols], wqkv_ref[2, h], preferred_element_type=F32))
    q = jnp.concatenate(qs, axis=1)
    k = jnp.concatenate(ks, axis=1)
    v = jnp.concatenate(vs, axis=1)
    qkv = jnp.concatenate([q, k, v], axis=1).astype(MXU_DT)
    gates = jnp.dot(qkv, wif_ref[...], preferred_element_type=F32) + bif_ref[...]
    g_ref[...] = gates.T[0:g_ref.shape[0], :]
    q_ref[...] = q.T.astype(q_ref.dtype)
    k_ref[...] = (k * (DH_C ** -0.5)).astype(k_ref.dtype)
    v_ref[...] = v.T.astype(v_ref.dtype)


def _mlstm_features(p, w_conv, b_conv, w_qkv, w_if_pad, b_if_pad, tps):
    r = p.shape[0]
    tm = ROW_TILE
    hpt = tm // HALO
    xm_blk = P_XM // W_C
    last_halo = r // HALO - 1
    kern = functools.partial(_mfeat_kernel, tps=tps)
    act = jax.ShapeDtypeStruct((r, W_C), ACT_DT)
    act_t = jax.ShapeDtypeStruct((W_C, r), ACT_DT)
    row_spec = pl.BlockSpec((tm, W_C), lambda i: (i, 0))
    col_spec = pl.BlockSpec((W_C, tm), lambda i: (0, i))
    return pl.pallas_call(
        kern,
        grid=(r // tm,),
        in_specs=[
            pl.BlockSpec((tm, W_C), lambda i: (i, xm_blk)),
            pl.BlockSpec((HALO, W_C), lambda i: (jnp.maximum(i * hpt - 1, 0), xm_blk)),
            pl.BlockSpec((HALO, W_C), lambda i: (jnp.minimum((i + 1) * hpt, last_halo), xm_blk)),
            pl.BlockSpec((CONV_K, W_C), lambda i: (0, 0)),
            pl.BlockSpec((1, W_C), lambda i: (0, 0)),
            pl.BlockSpec((3, H_C, DH_C, DH_C), lambda i: (0, 0, 0, 0)),
            pl.BlockSpec((3 * W_C, LANE), lambda i: (0, 0)),
            pl.BlockSpec((1, LANE), lambda i: (0, 0)),
        ],
        out_specs=[row_spec, col_spec, row_spec, col_spec,
                   pl.BlockSpec((2 * GATE_ROWS, tm), lambda i: (0, i))],
        out_shape=[act, act_t, act, act_t, jax.ShapeDtypeStruct((2 * GATE_ROWS, r), F32)],
        compiler_params=_cparams("parallel"),
        name="mlstm_features",
    )(p, p, p, w_conv, b_conv, w_qkv, w_if_pad, b_if_pad)


GATE_ROWS = 2 * H_C
GATE_PACK = ("r", "cmx", "bcum")


def _gate_prep_kernel(g_ref, pack_ref, rt_ref):
    seg = MLSTM_CHUNK
    gt_fwd = g_ref[0:GATE_ROWS, :]
    gt_bwd = g_ref[GATE_ROWS:2 * GATE_ROWS, :]
    n = gt_fwd.shape[1]
    lane = lax.broadcasted_iota(jnp.int32, gt_fwd.shape, 1)
    in_seg = lane & (seg - 1)

    def scan_both(xf, xb, op):
        sh = 1
        while sh < seg:
            xf = jnp.where(in_seg >= sh, op(xf, pltpu.roll(xf, sh, 1)), xf)
            xb = jnp.where(in_seg < seg - sh, op(xb, pltpu.roll(xb, n - sh, 1)), xb)
            sh *= 2
        return xf, xb

    bcum_f, bcum_b = scan_both(jax.nn.log_sigmoid(pltpu.roll(gt_fwd, H_C, 0)),
                               jax.nn.log_sigmoid(pltpu.roll(gt_bwd, H_C, 0)), jnp.add)
    r_f, r_b = gt_fwd - bcum_f, gt_bwd - bcum_b
    cmx_f, cmx_b = scan_both(r_f, r_b, jnp.maximum)
    pack_ref[...] = jnp.concatenate([r_f, cmx_f, bcum_f, r_b, cmx_b, bcum_b], axis=0)
    rt_ref[...] = jnp.concatenate([r_f, r_b, jnp.zeros((LANE - 2 * GATE_ROWS, n), F32)], axis=0).T


def _mlstm_gate_prep(gates_t, n_batch, seq):
    rows = 2 * len(GATE_PACK) * GATE_ROWS
    r = gates_t.shape[1]
    return pl.pallas_call(
        _gate_prep_kernel,
        grid=(n_batch,),
        in_specs=[pl.BlockSpec((2 * GATE_ROWS, seq), lambda b: (0, b))],
        out_specs=[pl.BlockSpec((rows, seq), lambda b: (0, b)), pl.BlockSpec((seq, LANE), lambda b: (b, 0))],
        out_shape=[jax.ShapeDtypeStruct((rows, r), F32), jax.ShapeDtypeStruct((r, LANE), F32)],
        compiler_params=_cparams("parallel"),
        name="mlstm_gate_prep",
    )(gates_t)


def _mlstm_gates(pack, m_ref, reverse):
    ln = pack.shape[1]
    part = {name: pack[i * GATE_ROWS:(i + 1) * GATE_ROWS] for i, name in enumerate(GATE_PACK)}
    end = slice(0, 1) if reverse else slice(ln - 1, ln)
    g = part["bcum"][:, end]
    r_max = part["cmx"][:, end]
    m_loc = g + r_max
    w_end = jnp.exp(part["r"] - r_max)
    m_old = m_ref[:, 0:1]
    m_new = jnp.maximum(g + m_old, m_loc)
    a_old = jnp.exp(g + m_old - m_new)
    a_new = jnp.exp(m_loc - m_new)
    mx = jnp.maximum(m_old, part["cmx"])
    w_inter = jnp.exp(m_old - mx)
    e_neg = jnp.exp(-(part["bcum"] + mx))
    s_id = lax.broadcasted_iota(jnp.int32, (ln, ln), 0)
    t_id = lax.broadcasted_iota(jnp.int32, (ln, ln), 1)
    keep = (s_id >= t_id) if reverse else (s_id <= t_id)
    return dict(mx=mx, w_inter=w_inter, e_neg=e_neg, w_end=w_end, a_old=a_old, a_new=a_new,
                m_new=m_new, keep=keep)


def _mscan_chunk(gf_ref, rf_ref, qf_ref, kf_ref, vf_ref, gb_ref, rb_ref, qb_ref, kb_ref, vb_ref,
                 hf_ref, hb_ref, cf_ref, mf_ref, cb_ref, mb_ref):
    ln = qf_ref.shape[1]
    pr = len(GATE_PACK) * GATE_ROWS
    dirs = ((_mlstm_gates(gf_ref[0:pr, :], mf_ref, False), qf_ref, kf_ref, vf_ref, cf_ref, hf_ref, rf_ref),
            (_mlstm_gates(gb_ref[pr:2 * pr, :], mb_ref, True), qb_ref, kb_ref, vb_ref, cb_ref, hb_ref, rb_ref))
    pairs = [(d, h) for d in range(2) for h in range(H_C)]
    ones_row = jnp.where(lax.broadcasted_iota(jnp.int32, (DH_C, ln), 0) == 0, 1.0, 0.0).astype(MXU_DT)
    head = lambda h: slice(h * DH_C, (h + 1) * DH_C)
    row = lambda h: slice(h, h + 1)

    ks = [dirs[d][2][:, head(h)] for d, h in pairs]
    qts = [dirs[d][1][head(h), :] for d, h in pairs]
    vexts = [jnp.concatenate([dirs[d][3][head(h), :], ones_row], axis=0) for d, h in pairs]
    cexts = [dirs[d][4][h] for d, h in pairs]
    qks = [jnp.dot(k, q_t, preferred_element_type=F32) for k, q_t in zip(ks, qts)]
    rhss = []
    for i, (d, h) in enumerate(pairs):
        g = dirs[d][0]
        r_col = dirs[d][6][:, d * GATE_ROWS + h:d * GATE_ROWS + h + 1]
        d_t = jnp.exp(jnp.where(g["keep"], r_col - g["mx"][row(h), :], NEG_BIG))
        wq = (qts[i].astype(F32) * g["w_inter"][row(h), :]).astype(MXU_DT)
        rhss.append(jnp.concatenate([(qks[i] * d_t).astype(MXU_DT), wq], axis=0))
    ress = [jnp.dot(jnp.concatenate([vexts[i], cexts[i].astype(MXU_DT)], axis=1), rhss[i],
                    preferred_element_type=F32) for i in range(len(pairs))]
    for i, (d, h) in enumerate(pairs):
        g = dirs[d][0]
        den = ress[i][DH_C:DH_C + 1, :]
        hh = ress[i][:DH_C, :] / jnp.maximum(jnp.abs(den), g["e_neg"][row(h), :])
        dirs[d][5][head(h), :] = hh.astype(dirs[d][5].dtype)
    clocs = []
    for i, (d, h) in enumerate(pairs):
        vw_t = (vexts[i].astype(F32) * dirs[d][0]["w_end"][row(h), :]).astype(MXU_DT)
        clocs.append(jnp.dot(vw_t, ks[i], preferred_element_type=F32))
    for i, (d, h) in enumerate(pairs):
        g = dirs[d][0]
        dirs[d][4][h] = g["a_old"][row(h), :] * cexts[i] + g["a_new"][row(h), :] * clocs[i]
    mf_ref[...] = jnp.broadcast_to(dirs[0][0]["m_new"], mf_ref.shape)
    mb_ref[...] = jnp.broadcast_to(dirs[1][0]["m_new"], mb_ref.shape)


def _mscan_kernel(g_ref, r_ref, q_ref, k_ref, v_ref, hf_ref, hb_ref, cf_ref, mf_ref, cb_ref, mb_ref, *, ncc):
    ln = MLSTM_CHUNK
    nck = q_ref.shape[1] // ln
    cf_ref[...] = jnp.zeros_like(cf_ref)
    cb_ref[...] = jnp.zeros_like(cb_ref)
    mf_ref[...] = jnp.zeros_like(mf_ref)
    mb_ref[...] = jnp.zeros_like(mb_ref)

    def chunk_views(c):
        at = pl.ds(pl.multiple_of(c * ln, ln), ln)
        return ((g_ref.at[:, at], r_ref.at[at, :], q_ref.at[:, at], k_ref.at[at, :], v_ref.at[:, at]),
                (hf_ref.at[:, at], hb_ref.at[:, at]))

    def trip(s, carry):
        ins_f, (hf, _) = chunk_views(s)
        ins_b, (_, hb) = chunk_views(jnp.where(s < ncc, ncc - 1 - s, nck - 1 + ncc - s))
        _mscan_chunk(*ins_f, *ins_b, hf, hb, cf_ref, mf_ref, cb_ref, mb_ref)
        return carry

    lax.fori_loop(0, nck, trip, 0)


def _mlstm_scan(q_t, k, v_t, gate_pack, r_t, n_batch, seq, ctx_rows):
    rows = lambda w: pl.BlockSpec((seq, w), lambda b: (b, 0))
    cols = lambda h: pl.BlockSpec((h, seq), lambda b: (0, b))
    out = jax.ShapeDtypeStruct(q_t.shape, ACT_DT)
    return pl.pallas_call(
        functools.partial(_mscan_kernel, ncc=ctx_rows // MLSTM_CHUNK),
        grid=(n_batch,),
        in_specs=[cols(gate_pack.shape[0]), rows(LANE), cols(W_C), rows(W_C), cols(W_C)],
        out_specs=[cols(W_C), cols(W_C)],
        out_shape=[out, out],
        scratch_shapes=[
            pltpu.VMEM((H_C, 2 * DH_C, DH_C), F32), pltpu.VMEM((8, LANE), F32),
            pltpu.VMEM((H_C, 2 * DH_C, DH_C), F32), pltpu.VMEM((8, LANE), F32),
        ],
        compiler_params=_cparams("parallel"),
        name="mlstm_scan",
    )(gate_pack, r_t, q_t, k, v_t)


def _route(logits, carry):
    lane = lax.broadcasted_iota(jnp.int32, logits.shape, 1).astype(F32)
    big = float(4 * LANE)
    gl = jnp.where(lane < N_GROUPS, logits, NEG_BIG)
    gmax = jnp.max(gl, axis=-1, keepdims=True)
    g_star = jnp.min(jnp.where(gl == gmax, lane, big), axis=-1, keepdims=True)
    p_g = 1.0 / jnp.sum(jnp.exp(gl - gmax), axis=-1, keepdims=True)
    e_lo = g_star * EXP_PER_GROUP + N_GROUPS
    in_group = jnp.logical_and(lane >= e_lo, lane < e_lo + EXP_PER_GROUP)
    el = jnp.where(in_group, logits, NEG_BIG)
    v1 = jnp.max(el, axis=-1, keepdims=True)
    i1 = jnp.min(jnp.where(el == v1, lane, big), axis=-1, keepdims=True)
    el2 = jnp.where(lane == i1, NEG_BIG, el)
    v2 = jnp.max(el2, axis=-1, keepdims=True)
    i2 = jnp.min(jnp.where(el2 == v2, lane, big), axis=-1, keepdims=True)
    e21 = jnp.exp(v2 - v1)
    w1 = p_g / (1.0 + e21)
    w2 = p_g * e21 / (1.0 + e21)
    e1 = i1 - N_GROUPS
    e2 = i2 - N_GROUPS
    oh1 = lane == e1
    oh2 = lane == e2
    both = jnp.where(jnp.logical_or(oh1, oh2), 1.0, 0.0)
    tm = logits.shape[0]
    lower = (lax.broadcasted_iota(jnp.int32, (tm, tm), 1) < lax.broadcasted_iota(jnp.int32, (tm, tm), 0))
    before = jnp.dot(jnp.where(lower, 1.0, 0.0).astype(MXU_DT), both.astype(MXU_DT),
                     preferred_element_type=F32) + carry
    rank1 = jnp.sum(jnp.where(oh1, before, 0.0), axis=-1, keepdims=True)
    rank2 = jnp.sum(jnp.where(oh2, before, 0.0), axis=-1, keepdims=True)
    out = jnp.where(lane == 0, e1, 0.0)
    out = jnp.where(lane == 1, e2, out)
    out = jnp.where(lane == 2, w1, out)
    out = jnp.where(lane == 3, w2, out)
    out = jnp.where(lane == 4, rank1, out)
    out = jnp.where(lane == 5, rank2, out)
    return out, carry + jnp.sum(both, axis=0, keepdims=True)


def _merge_kernel(x_ref, gate_ref, z_ref, ya_ref, yb_ref, hf_ref, hb_ref, xc_ref, gm_ref, sk_ref,
                  wbr_ref, wo_ref, mod_ref, g2_ref, wr_ref, br_ref, xo_ref, rt_ref, cnt_ref, carry_ref):
    @pl.when(pl.program_id(0) == 0)
    def _():
        carry_ref[...] = jnp.zeros_like(carry_ref)

    hs_t = hf_ref[...].astype(F32) + hb_ref[...].astype(F32)
    hn_t = []
    for h in range(H_C):
        part = hs_t[h * DH_C:(h + 1) * DH_C, :]
        hn_t.append(part * lax.rsqrt(jnp.mean(part * part, axis=0, keepdims=True) + EPS))
    hn = jnp.concatenate(hn_t, axis=0).T * gm_ref[...]
    yc = (hn + sk_ref[...] * xc_ref[...].astype(F32)) * _sigmoid(z_ref[...].astype(F32))
    ys = (ya_ref[...], yb_ref[...], yc.astype(MXU_DT))
    merged = None
    for i in range(3):
        gate = _sigmoid(gate_ref[:, i * D_MODEL:(i + 1) * D_MODEL].astype(MXU_DT))
        term = gate * jnp.dot(ys[i], wbr_ref[i], preferred_element_type=F32).astype(MXU_DT)
        merged = term if merged is None else merged + term
    y = jnp.dot(merged, wo_ref[...], preferred_element_type=F32)
    xn = x_ref[...] + mod_ref[2:3, :] * y
    xo_ref[...] = xn
    h2 = _rms(xn) * g2_ref[...] * (1.0 + mod_ref[4:5, :]) + mod_ref[3:4, :]
    h_hi, h_lo = _split_bf16(h2)
    logits = (jnp.dot(h_hi, wr_ref[0], preferred_element_type=F32)
              + jnp.dot(h_lo, wr_ref[0], preferred_element_type=F32)
              + jnp.dot(h_hi, wr_ref[1], preferred_element_type=F32)) + br_ref[...]
    route, carry = _route(logits, carry_ref[0:1, :])
    rt_ref[...] = route
    carry_ref[0:1, :] = carry
    cnt_ref[...] = jnp.broadcast_to(carry, cnt_ref.shape)


def _merge(xall, p, ya, yb, hf, hb, xconv, g_m, skip, w_br, w_o, mod, g2, w_route, b_route, tiles):
    d = xall.shape[1]
    tm = ROW_TILE
    r = tiles.count * tm

    def mod_idx(i):
        return (tiles.mod_row(i), 0, 0)

    row = lambda w: pl.BlockSpec((tm, w), lambda i: (i, 0))
    frow = lambda w: pl.BlockSpec((tm, w), lambda i: (tiles.full(i), 0))
    fcol = lambda w: pl.BlockSpec((w, tm), lambda i: (0, tiles.full(i)))
    full = lambda shape: pl.BlockSpec(shape, lambda i: (0,) * len(shape))
    return pl.pallas_call(
        _merge_kernel,
        grid=(tiles.count,),
        in_specs=[
            frow(d),
            pl.BlockSpec((tm, GATE_W), lambda i: (tiles.full(i), P_GATE // GATE_W)),
            pl.BlockSpec((tm, W_C), lambda i: (tiles.full(i), P_Z // W_C)),
            row(W_BR), row(W_BR), fcol(W_C), fcol(W_C), frow(W_C),
            full((1, W_C)), full((1, W_C)),
            full((3, W_BR, d)), full((d, d)),
            pl.BlockSpec((None, N_MOD, d), mod_idx),
            full((1, d)), full((2, d, LANE)), full((1, LANE)),
        ],
        out_specs=[row(d), row(LANE), full((8, LANE))],
        out_shape=[jax.ShapeDtypeStruct((r, d), F32), jax.ShapeDtypeStruct((r, LANE), F32),
                   jax.ShapeDtypeStruct((8, LANE), F32)],
        scratch_shapes=[pltpu.VMEM((8, LANE), F32)],
        compiler_params=_cparams("arbitrary"),
        name="merge_route",
    )(xall, p, p, ya, yb, hf, hb, xconv, g_m, skip, w_br, w_o, mod, g2, w_route, b_route)


def _row_copy(src_hbm, src_row, dst_ref, dst_row, sem):
    return pltpu.make_async_copy(src_hbm.at[pl.ds(src_row, 1), :], dst_ref.at[pl.ds(dst_row, 1), :], sem)


def _dispatch_kernel(fill_ref, nlive_ref, pos_ref, x_ref, g2_ref, mod_ref, xs_hbm, hbuf, zbuf, zsem, sem):
    i = pl.program_id(0)
    n = pl.num_programs(0)
    tm = x_ref.shape[0]
    slot = i % 2
    bm = zbuf.shape[0]
    n_blocks = xs_hbm.shape[0] // bm

    @pl.when(i == 0)
    def _():
        zbuf[...] = jnp.zeros_like(zbuf)

        def fill_block(b):
            return pltpu.make_async_copy(zbuf, xs_hbm.at[pl.ds(b * bm, bm), :], zsem)

        def fill_last(k, carry):
            fill_block(fill_ref[k]).start()
            return carry

        def fill_unused(b, carry):
            fill_block(b).start()
            return carry

        def wait_fill(k, carry):
            fill_block(0).wait()
            return carry

        n_live = nlive_ref[0]
        n_used = nlive_ref[1]
        lax.fori_loop(0, n_used, fill_last, 0)
        lax.fori_loop(n_live, n_blocks, fill_unused, 0)
        lax.fori_loop(0, n_used + n_blocks - n_live, wait_fill, 0)

    hbuf[slot] = _rms(x_ref[...]) * g2_ref[...] * (1.0 + mod_ref[4:5, :]) + mod_ref[3:4, :]

    def body(c, carry):
        for u in range(DMA_UNROLL):
            r = c * DMA_UNROLL + u
            for j in range(TOP_K):
                pltpu.make_async_copy(hbuf.at[slot, pl.ds(r, 1), :],
                                      xs_hbm.at[pl.ds(pos_ref[0, j * tm + r], 1), :], sem.at[slot]).start()
        return carry

    lax.fori_loop(0, tm // DMA_UNROLL, body, 0)

    def wait_tile(s):
        pltpu.make_async_copy(hbuf.at[s], xs_hbm.at[pl.ds(0, tm), :], sem.at[s]).wait()
        pltpu.make_async_copy(hbuf.at[s], xs_hbm.at[pl.ds(0, tm), :], sem.at[s]).wait()

    @pl.when(i > 0)
    def _():
        wait_tile(1 - slot)

    @pl.when(i == n - 1)
    def _():
        wait_tile(slot)


def _moe_dispatch(xall, g2, mod, fill_blocks, n_live, pos_tiles, n_slots, tiles):
    r, d = xall.shape
    tm = ROW_TILE
    assert TOP_K == 2 and r == tiles.count * tm

    def mod_idx(i, fb, nl):
        return (tiles.mod_row(i), 0, 0)

    grid_spec = pltpu.PrefetchScalarGridSpec(
        num_scalar_prefetch=2,
        grid=(r // tm,),
        in_specs=[
            pl.BlockSpec((None, 1, TOP_K * tm), lambda i, fb, nl: (i, 0, 0), memory_space=pltpu.SMEM),
            pl.BlockSpec((tm, d), lambda i, fb, nl: (i, 0)),
            pl.BlockSpec((1, d), lambda i, fb, nl: (0, 0)),
            pl.BlockSpec((None, N_MOD, d), mod_idx),
        ],
        out_specs=pl.BlockSpec(memory_space=pl.ANY),
        scratch_shapes=[pltpu.VMEM((2, tm, d), F32), pltpu.VMEM((MOE_BLOCK, d), F32),
                        pltpu.SemaphoreType.DMA(()), pltpu.SemaphoreType.DMA((2,))],
    )
    return pl.pallas_call(
        _dispatch_kernel,
        grid_spec=grid_spec,
        out_shape=jax.ShapeDtypeStruct((n_slots, d), F32),
        compiler_params=_cparams("arbitrary"),
        name="moe_dispatch",
    )(fill_blocks, n_live, pos_tiles, xall, g2, mod)


def _moe_kernel(be_ref, nv_ref, x_ref, wg_ref, wu_ref, wd_ref, y_ref, wg_mx, wu_mx, wd_mx):
    i = pl.program_id(0)
    n_valid = nv_ref[i]

    @pl.when(jnp.logical_or(i == 0, be_ref[i] != be_ref[jnp.maximum(i - 1, 0)]))
    def _():
        wg_mx[...] = wg_ref[...].astype(wg_mx.dtype)
        wu_mx[...] = wu_ref[...].astype(wu_mx.dtype)
        wd_mx[...] = wd_ref[...].astype(wd_mx.dtype)

    @pl.when(n_valid > 0)
    def _():
        x = x_ref[...].astype(MXU_DT)
        a = jnp.dot(x, wg_mx[...], preferred_element_type=F32)
        u = jnp.dot(x, wu_mx[...], preferred_element_type=F32)
        act = (a * _sigmoid(a) * u).astype(MXU_DT)
        y_ref[...] = jnp.dot(act, wd_mx[...], preferred_element_type=F32)

    @pl.when(n_valid <= 0)
    def _():
        y_ref[...] = jnp.zeros_like(y_ref)


def _moe_experts(xs, block_e, n_valid, w_gate, w_up, w_down, layer):
    n_slots, d = xs.shape
    bm = MOE_BLOCK
    grid_spec = pltpu.PrefetchScalarGridSpec(
        num_scalar_prefetch=2,
        grid=(n_slots // bm,),
        in_specs=[
            pl.BlockSpec((bm, d), lambda i, be, nv: (i, 0)),
            pl.BlockSpec((None, None, d, D_EXPERT), lambda i, be, nv: (layer, be[i], 0, 0)),
            pl.BlockSpec((None, None, d, D_EXPERT), lambda i, be, nv: (layer, be[i], 0, 0)),
            pl.BlockSpec((None, None, D_EXPERT, d), lambda i, be, nv: (layer, be[i], 0, 0)),
        ],
        out_specs=pl.BlockSpec((bm, d), lambda i, be, nv: (i, 0)),
        scratch_shapes=[pltpu.VMEM((d, D_EXPERT), MXU_DT), pltpu.VMEM((d, D_EXPERT), MXU_DT),
                        pltpu.VMEM((D_EXPERT, d), MXU_DT)],
    )
    return pl.pallas_call(
        _moe_kernel,
        grid_spec=grid_spec,
        out_shape=jax.ShapeDtypeStruct((n_slots, d), F32),
        compiler_params=_cparams("arbitrary"),
        name="moe_experts",
    )(block_e, n_valid, xs, w_gate, w_up, w_down)


def _combine_kernel(pos_ref, pos_next_ref, x_ref, rt_ref, mod_ref, gf_ref, yb_hbm, xo_ref, ybuf, sem, *, final):
    i = pl.program_id(0)
    n = pl.num_programs(0)
    tm = x_ref.shape[0]

    def start_gather(ids_ref, slot):
        def body(c, carry):
            for u in range(DMA_UNROLL):
                r = c * DMA_UNROLL + u
                _row_copy(yb_hbm, ids_ref[0, r], ybuf.at[slot], r, sem.at[slot]).start()
            return carry
        lax.fori_loop(0, TOP_K * tm // DMA_UNROLL, body, 0)

    @pl.when(i == 0)
    def _():
        start_gather(pos_ref, 0)

    @pl.when(i + 1 < n)
    def _():
        start_gather(pos_next_ref, (i + 1) % 2)

    slot = i % 2
    pltpu.make_async_copy(yb_hbm.at[pl.ds(0, TOP_K * tm), :], ybuf.at[slot], sem.at[slot]).wait()
    rt = rt_ref[...]
    f = rt[:, 2:3] * ybuf[slot, 0:tm, :] + rt[:, 3:4] * ybuf[slot, tm:2 * tm, :]
    xn = x_ref[...] + mod_ref[5:6, :] * f
    xo_ref[...] = _rms(xn) * gf_ref[...] if final else xn


def _combine(xall, route, mod, g_final, yb, pos, tiles, final):
    r, d = xall.shape
    tm = ROW_TILE
    nt = tiles.count
    last = nt - 1
    assert r == nt * tm

    def mod_idx(i):
        return (tiles.mod_row(i), 0, 0)

    row = lambda w: pl.BlockSpec((tm, w), lambda i: (i, 0))
    return pl.pallas_call(
        functools.partial(_combine_kernel, final=final),
        grid=(nt,),
        in_specs=[
            pl.BlockSpec((None, 1, TOP_K * tm), lambda i: (i, 0, 0), memory_space=pltpu.SMEM),
            pl.BlockSpec((None, 1, TOP_K * tm), lambda i: (jnp.minimum(i + 1, last), 0, 0),
                         memory_space=pltpu.SMEM),
            row(d), row(LANE),
            pl.BlockSpec((None, N_MOD, d), mod_idx),
            pl.BlockSpec((1, d), lambda i: (0, 0)),
            pl.BlockSpec(memory_space=pl.ANY),
        ],
        out_specs=row(d),
        out_shape=jax.ShapeDtypeStruct((r, d), F32),
        scratch_shapes=[pltpu.VMEM((2, TOP_K * tm, d), F32), pltpu.SemaphoreType.DMA((2,))],
        compiler_params=_cparams("arbitrary"),
        name="moe_combine",
    )(pos, pos, xall, route, mod, g_final, yb)


def _dispatch_plan(route, counts_f, n_tiles):
    n = route.shape[0]
    bm = MOE_BLOCK
    n_blocks = -(-(n * TOP_K + N_EXPERTS * (bm - 1)) // bm)
    counts = counts_f[:N_EXPERTS].astype(jnp.int32)
    padded = (counts + bm - 1) // bm * bm
    pend = jnp.cumsum(padded)
    seg_off = pend - padded
    eid = route[:, 0:TOP_K].astype(jnp.int32)
    rank = route[:, 4:4 + TOP_K].astype(jnp.int32)
    experts = jnp.arange(N_EXPERTS, dtype=jnp.int32)
    pos = jnp.sum(jnp.where(eid[..., None] == experts, seg_off, 0), axis=-1) + rank
    blk0 = jnp.arange(n_blocks, dtype=jnp.int32) * bm
    block_e = jnp.minimum(jnp.sum((blk0[:, None] >= pend[None, :]).astype(jnp.int32), axis=1), N_EXPERTS - 1)
    seg_end = seg_off + counts
    n_valid = jnp.clip(seg_end[block_e] - blk0, 0, bm).astype(jnp.int32)
    pos_tiles = pos.reshape(n_tiles, n // n_tiles, TOP_K).transpose(0, 2, 1).reshape(n_tiles, 1, -1)
    used = counts > 0
    last_blocks = jnp.sort(jnp.where(used, pend // bm - 1, n_blocks)).astype(jnp.int32)
    n_live = jnp.stack([pend[-1] // bm, jnp.sum(used)]).astype(jnp.int32)
    return block_e.astype(jnp.int32), n_valid, last_blocks, n_live, pos_tiles, n_blocks * bm


def _rope_tables(seq, ctx_rows):
    t = jnp.arange(seq - ctx_rows)
    inv = ROPE_BASE ** (-jnp.arange(0, ROT_AX, 2, dtype=F32) / ROT_AX)
    ang_r = (t // GRID_W).astype(F32)[:, None] * inv
    ang_c = (t % GRID_W).astype(F32)[:, None] * inv
    cos64 = jnp.concatenate([jnp.cos(ang_r)] * 2 + [jnp.cos(ang_c)] * 2, axis=1)
    sin64 = jnp.concatenate([-jnp.sin(ang_r), jnp.sin(ang_r), -jnp.sin(ang_c), jnp.sin(ang_c)], axis=1)
    cos = jnp.concatenate([jnp.ones((ctx_rows, DK), F32), cos64], axis=0)
    sin = jnp.concatenate([jnp.zeros((ctx_rows, DK), F32), sin64], axis=0)
    return jnp.tile(cos, (1, LANE // DK)), jnp.tile(sin, (1, LANE // DK))


def kernel(x, c, ctx, c_ctx, w_mod, b_mod, g_norm1, g_norm2, w_in, diff_lambda, g_diff_subln, w_gmlp_s, b_gmlp_s, g_gmlp_v, w_conv_m, b_conv_m, w_qkv_m, w_if_m, b_if_m, g_mlstm_norm, skip_m, w_branch, w_out, w_route_g, b_route_g, w_route_e, b_route_e, w_e_gate, w_e_up, w_e_down, g_final):
    n_batch, t_lat, d = x.shape
    ctx_rows = ctx.shape[1]
    depth = w_in.shape[0]
    seq = ctx_rows + t_lat
    assert d == D_MODEL and ctx_rows == ROW_TILE and t_lat % ROW_TILE == 0 and t_lat % GRID_W == 0
    tps = seq // ROW_TILE
    all_tiles = _Tiles(n_batch, tps, False)

    xall = jnp.concatenate([ctx, x], axis=1).reshape(n_batch * seq, d)
    mb = -(-(n_batch + 1) // 8) * 8
    c_all = jnp.zeros((mb, d), F32).at[:n_batch].set(c).at[n_batch].set(c_ctx)
    mod_all = _modulation(c_all, w_mod, b_mod)[:, :n_batch + 1].reshape(depth, n_batch + 1, N_MOD, d)
    cos, sin_signed = _rope_tables(seq, ctx_rows)
    w_in_b = w_in.astype(MXU_DT)

    for l in range(depth):
        lam_init = 0.8 - 0.6 * math.exp(-0.3 * l)
        mod = mod_all[l]
        last = l == depth - 1
        tiles = _Tiles(n_batch, tps, last)
        p, k_t = _inproj(xall, g_norm1[l][None], mod, cos, sin_signed, w_in_b, l, all_tiles, last)

        ya = _attention(p, k_t, diff_lambda[l], g_diff_subln[l][None], lam_init, tiles, ctx_rows)

        b_full = jnp.repeat(b_gmlp_s[l].T, W_B // G_B, axis=1)
        yb = _gmlp(p, g_gmlp_v[l][None], w_gmlp_s[l].astype(MXU_DT), b_full, tiles)

        w_if_pad = jnp.zeros((3 * W_C, LANE), F32).at[:, :4 * H_C].set(w_if_m[l]).astype(MXU_DT)
        b_if_pad = jnp.zeros((1, LANE), F32).at[0, :4 * H_C].set(b_if_m[l])
        xconv, q_m, k_m, v_m, gates_t = _mlstm_features(
            p, w_conv_m[l], b_conv_m[l][None], w_qkv_m[l].astype(MXU_DT), w_if_pad, b_if_pad, tps)
        gate_pack, r_t = _mlstm_gate_prep(gates_t, n_batch, seq)
        hf, hb = _mlstm_scan(q_m, k_m, v_m, gate_pack, r_t, n_batch, seq, ctx_rows)

        w_route = (jnp.zeros((d, LANE), F32).at[:, :N_GROUPS].set(w_route_g[l])
                   .at[:, N_GROUPS:N_GROUPS + N_EXPERTS].set(w_route_e[l]))
        w_route = jnp.stack(_split_bf16(w_route))
        b_route = (jnp.zeros((1, LANE), F32).at[0, :N_GROUPS].set(b_route_g[l])
                   .at[0, N_GROUPS:N_GROUPS + N_EXPERTS].set(b_route_e[l]))
        xall, route, counts = _merge(xall, p, ya, yb, hf, hb, xconv, g_mlstm_norm[l][None], skip_m[l][None],
                                         w_branch[l].astype(MXU_DT), w_out[l].astype(MXU_DT), mod,
                                         g_norm2[l][None], w_route, b_route, tiles)

        block_e, n_valid, fill_blocks, n_live, pos_tiles, n_slots = _dispatch_plan(route, counts[0], tiles.count)
        xs = _moe_dispatch(xall, g_norm2[l][None], mod, fill_blocks, n_live, pos_tiles, n_slots, tiles)
        y_sorted = _moe_experts(xs, block_e, n_valid, w_e_gate, w_e_up, w_e_down, l)
        xall = _combine(xall, route, mod, g_final[None], y_sorted, pos_tiles, tiles, last)

    return xall.reshape(n_batch, t_lat, d)
```

```python
import functools
import math
from typing import NamedTuple

import jax
import jax.numpy as jnp
from jax import lax
from jax.experimental import pallas as pl
from jax.experimental.pallas import tpu as pltpu

F32 = jnp.float32
MXU_DT = jnp.bfloat16
ACT_DT = jnp.bfloat16
HIGHEST = lax.Precision.HIGHEST

D_MODEL = 1024
N_MOD = 6
EPS = 1e-6
GRID_W = 64
H_A, DK = 4, 64
DV = 2 * DK
ROT_AX = DK // 2
ROPE_BASE = 10000.0
W_B, G_B, CHUNK_B = 512, 4, 128
H_C, DH_C, CONV_K, MLSTM_CHUNK = 4, 128, 3, 128
W_C = H_C * DH_C
N_GROUPS, EXP_PER_GROUP, TOP_K, D_EXPERT = 4, 8, 2, 512
N_EXPERTS = N_GROUPS * EXP_PER_GROUP
W_BR = 512

GATE_W = 3 * D_MODEL
P_GATE = 0
P_K = GATE_W
P_V = P_K + 512
P_XM = P_V + 512
P_Q = P_XM + 512
P_UV = P_Q + 512
P_Z = P_UV + 2 * W_B
IN_COLS = P_Z + W_C
ORIG_GATE0 = IN_COLS - GATE_W

LANE = 128
ROW_TILE = 256
PROJ_CHUNK = 512
MOE_BLOCK = 512
HALO = 16
DMA_UNROLL = 8
ATTN_KV_CHUNK = 256
LOG2_E = math.log2(math.e)
VMEM_LIMIT = 56 * 1024 * 1024
NEG_BIG = -1e30


class _Tiles(NamedTuple):
    n_batch: int
    tps: int
    skip_ctx: bool

    @property
    def per_seq(self):
        return self.tps - 1 if self.skip_ctx else self.tps

    @property
    def count(self):
        return self.n_batch * self.per_seq

    def full(self, i):
        if not self.skip_ctx:
            return i
        return (i // self.per_seq) * self.tps + 1 + i % self.per_seq

    def mod_row(self, i):
        if self.skip_ctx:
            return i // self.per_seq
        return jnp.where(i % self.tps == 0, self.n_batch, i // self.tps)


def _cparams(*sem):
    return pltpu.CompilerParams(dimension_semantics=sem, vmem_limit_bytes=VMEM_LIMIT)


def _rms(x):
    return x * lax.rsqrt(jnp.mean(x * x, axis=-1, keepdims=True) + EPS)


def _sigmoid(x):
    return 0.5 * jnp.tanh(0.5 * x) + 0.5


def _split_bf16(x):
    hi = x.astype(jnp.bfloat16)
    return hi, (x - hi.astype(F32)).astype(jnp.bfloat16)


def _mod_kernel(c_ref, w_ref, b_ref, o_ref):
    c = c_ref[...]
    s = c * jax.nn.sigmoid(c)
    o_ref[...] = jnp.dot(s, w_ref[...], preferred_element_type=F32, precision=HIGHEST) + b_ref[...]


def _modulation(c_all, w_mod, b_mod):
    n_layer = w_mod.shape[0]
    mb, d = c_all.shape
    tn = 1024
    return pl.pallas_call(
        _mod_kernel,
        grid=(n_layer, N_MOD * d // tn),
        in_specs=[
            pl.BlockSpec((mb, d), lambda l, j: (0, 0)),
            pl.BlockSpec((None, d, tn), lambda l, j: (l, 0, j)),
            pl.BlockSpec((None, 1, tn), lambda l, j: (l, 0, j)),
        ],
        out_specs=pl.BlockSpec((None, mb, tn), lambda l, j: (l, 0, j)),
        out_shape=jax.ShapeDtypeStruct((n_layer, mb, N_MOD * d), F32),
        compiler_params=_cparams("parallel", "parallel"),
        name="modulation",
    )(c_all, w_mod, b_mod.reshape(n_layer, 1, N_MOD * d))


def _rope(acc, cos, sin_signed):
    w = acc.shape[1]
    half = ROT_AX // 2
    lane = lax.broadcasted_iota(jnp.int32, acc.shape, 1)
    partner = jnp.where((lane & half) == 0, pltpu.roll(acc, w - half, 1), pltpu.roll(acc, half, 1))
    reps = w // cos.shape[1]
    return acc * jnp.tile(cos, (1, reps)) + partner * jnp.tile(sin_signed, (1, reps))


def _inproj_kernel(x_ref, g_ref, mod_ref, cos_ref, sin_ref, w_ref, o_ref, kt_ref):
    x = x_ref[...]
    h = (_rms(x) * g_ref[...] * (1.0 + mod_ref[1:2, :]) + mod_ref[0:1, :]).astype(MXU_DT)
    k_chunk = P_K // PROJ_CHUNK
    q_chunk = P_Q // PROJ_CHUNK
    chunks = range(IN_COLS // PROJ_CHUNK)
    cols = lambda c: slice(c * PROJ_CHUNK, (c + 1) * PROJ_CHUNK)

    def project(c):
        w0 = (ORIG_GATE0 + c * PROJ_CHUNK) % IN_COLS
        acc = jnp.dot(h, w_ref[:, w0:w0 + PROJ_CHUNK], preferred_element_type=F32)
        if c == k_chunk:
            acc = _rope(acc, cos_ref[...], sin_ref[...])
            kt_ref[...] = acc.T.astype(kt_ref.dtype)
        elif c == q_chunk:
            acc = _rope(acc, cos_ref[...], sin_ref[...]) * (DK ** -0.5 * LOG2_E)
        o_ref[:, cols(c)] = acc.astype(o_ref.dtype)

    for c in chunks:
        project(c)


def _inproj(xall, g1, mod, cos, sin_signed, w_in, layer, tiles):
    r, d = xall.shape
    tm = ROW_TILE
    tps = tiles.tps
    assert ORIG_GATE0 % PROJ_CHUNK == 0 and not tiles.skip_ctx

    def mod_idx(i):
        return (tiles.mod_row(i), 0, 0)

    return pl.pallas_call(
        _inproj_kernel,
        grid=(r // tm,),
        in_specs=[
            pl.BlockSpec((tm, d), lambda i: (i, 0)),
            pl.BlockSpec((1, d), lambda i: (0, 0)),
            pl.BlockSpec((None, N_MOD, d), mod_idx),
            pl.BlockSpec((tm, LANE), lambda i: (i % tps, 0)),
            pl.BlockSpec((tm, LANE), lambda i: (i % tps, 0)),
            pl.BlockSpec((None, d, IN_COLS), lambda i: (layer, 0, 0)),
        ],
        out_specs=[pl.BlockSpec((tm, IN_COLS), lambda i: (i, 0)),
                   pl.BlockSpec((PROJ_CHUNK, tm), lambda i: (0, i))],
        out_shape=[jax.ShapeDtypeStruct((r, IN_COLS), ACT_DT), jax.ShapeDtypeStruct((PROJ_CHUNK, r), ACT_DT)],
        compiler_params=_cparams("parallel"),
        name="inproj",
    )(xall, g1, mod, cos, sin_signed, w_in)


def _attn_kernel(dl_ref, g_ref, q_ref, kt_ref, v_ref, o_ref, vext_ref, s_ref, mx_ref, *,
                 lam_init, ctx_rows, has_ctx):
    tq = s_ref.shape[2]
    seq = kt_ref.shape[1]
    one_hot = lax.broadcasted_iota(jnp.int32, v_ref.shape, 1) == 0
    vext_ref[:, :DV] = v_ref[...]
    vext_ref[:, DV:] = jnp.where(one_hot, 1.0, 0.0).astype(vext_ref.dtype)

    dl = dl_ref[...]
    lam = (jnp.exp(jnp.sum(dl[0:1] * dl[1:2], keepdims=True))
           - jnp.exp(jnp.sum(dl[2:3] * dl[3:4], keepdims=True)) + lam_init)
    def chunk_list(kv_rows):
        kc = min(ATTN_KV_CHUNK, kv_rows)
        assert kv_rows % kc == 0
        return [slice(c * kc, (c + 1) * kc) for c in range(kv_rows // kc)]

    def score_pass(kv_rows, q_rows, buf):
        q = q_ref[q_rows, :]
        lane = lax.broadcasted_iota(jnp.int32, q.shape, 1)
        zero = jnp.zeros_like(q)
        qms = (jnp.where(lane < DK, q, zero), jnp.where(lane >= DK, q, zero))
        mruns = [None, None]
        for cols in chunk_list(kv_rows):
            for m in range(2):
                s = jnp.dot(qms[m], kt_ref[:, cols], preferred_element_type=F32)
                s_ref[buf, m, :, cols] = s
                for g in range(s.shape[1] // LANE):
                    blk = s[:, g * LANE:(g + 1) * LANE]
                    mruns[m] = blk if mruns[m] is None else jnp.maximum(mruns[m], blk)
            yield
        for m in range(2):
            mx = jnp.max(mruns[m], axis=-1, keepdims=True)
            mx_ref[buf, m] = jnp.broadcast_to(mx, mx_ref.shape[2:])
        yield

    def value_pass(kv_rows, out_rows, buf):
        mxs = [mx_ref[buf, m][:, 0:1] for m in range(2)]
        accs = [jnp.zeros((tq, 2 * DV), F32)] * 2
        for cols in chunk_list(kv_rows):
            for m in range(2):
                p = jnp.exp2(s_ref[buf, m, :, cols] - mxs[m])
                accs[m] = accs[m] + jnp.dot(p.astype(MXU_DT), vext_ref[cols, :], preferred_element_type=F32)
            yield
        o = (accs[0][:, :DV] * (1.0 / accs[0][:, DV:DV + 1])
             - accs[1][:, :DV] * (lam / accs[1][:, DV:DV + 1]))
        o_ref[out_rows, :] = (_rms(o) * g_ref[...] * (1.0 - lam_init)).astype(o_ref.dtype)
        yield

    def run(*passes):
        live = list(passes)
        while live:
            live = [g for g in live if next(g, "done") != "done"]

    out_off = 0
    if has_ctx:
        run(score_pass(ctx_rows, slice(0, ctx_rows), 0))
        run(value_pass(ctx_rows, slice(0, ctx_rows), 0))
        out_off = ctx_rows

    n_lat = (seq - ctx_rows) // tq
    q_at = lambda i: pl.ds(pl.multiple_of(ctx_rows + i * tq, tq), tq)
    o_at = lambda i: pl.ds(pl.multiple_of(out_off + i * tq, tq), tq)
    assert n_lat % 2 == 0
    run(score_pass(seq, q_at(0), 0))

    def tile_pair(j, carry):
        a = 2 * j
        run(score_pass(seq, q_at(a + 1), 1), value_pass(seq, o_at(a), 0))
        run(score_pass(seq, q_at(a + 2), 0), value_pass(seq, o_at(a + 1), 1))
        return carry

    lax.fori_loop(0, n_lat // 2 - 1, tile_pair, 0)
    run(score_pass(seq, q_at(n_lat - 1), 1), value_pass(seq, o_at(n_lat - 2), 0))
    run(value_pass(seq, o_at(n_lat - 1), 1))


def _attention(p, k_t, diff_lambda, g_sub, lam_init, tiles, ctx_rows):
    tq = ROW_TILE
    tps = tiles.tps
    seq = tps * tq
    out_rows = tiles.per_seq * tq
    assert ctx_rows == tq
    kern = functools.partial(_attn_kernel, lam_init=lam_init, ctx_rows=ctx_rows, has_ctx=not tiles.skip_ctx)
    return pl.pallas_call(
        kern,
        grid=(tiles.n_batch, H_A),
        in_specs=[
            pl.BlockSpec((4, DK), lambda b, h: (0, 0)),
            pl.BlockSpec((1, DV), lambda b, h: (0, 0)),
            pl.BlockSpec((seq, LANE), lambda b, h: (b, P_Q // LANE + h)),
            pl.BlockSpec((LANE, seq), lambda b, h: (h, b)),
            pl.BlockSpec((seq, LANE), lambda b, h: (b, P_V // LANE + h)),
        ],
        out_specs=pl.BlockSpec((out_rows, LANE), lambda b, h: (b, h)),
        out_shape=jax.ShapeDtypeStruct((tiles.count * tq, H_A * DV), ACT_DT),
        scratch_shapes=[pltpu.VMEM((seq, 2 * DV), MXU_DT), pltpu.VMEM((2, 2, tq, seq), F32),
                        pltpu.VMEM((2, 2, tq, LANE), F32)],
        compiler_params=_cparams("parallel", "parallel"),
        name="diff_attention",
    )(diff_lambda, g_sub, p, k_t, p)


def _gmlp_kernel(uv_ref, gv_ref, ws_ref, bs_ref, o_ref):
    uv = jax.nn.gelu(uv_ref[...])
    u = uv[:, :W_B].astype(F32)
    v = (_rms(uv[:, W_B:].astype(F32)) * gv_ref[...]).astype(MXU_DT)
    gw = W_B // G_B
    for c in range(uv.shape[0] // CHUNK_B):
        rows = slice(c * CHUNK_B, (c + 1) * CHUNK_B)
        for g in range(G_B):
            cols = slice(g * gw, (g + 1) * gw)
            mixed = jnp.dot(ws_ref[g], v[rows, cols], preferred_element_type=F32) + bs_ref[:, cols]
            o_ref[rows, cols] = (u[rows, cols] * mixed).astype(o_ref.dtype)


def _gmlp(p, g_v, w_s, b_full, tiles):
    tm = ROW_TILE
    r = tiles.count * tm
    return pl.pallas_call(
        _gmlp_kernel,
        grid=(tiles.count,),
        in_specs=[
            pl.BlockSpec((tm, 2 * W_B), lambda i: (tiles.full(i), P_UV // (2 * W_B))),
            pl.BlockSpec((1, W_B), lambda i: (0, 0)),
            pl.BlockSpec((G_B, CHUNK_B, CHUNK_B), lambda i: (0, 0, 0)),
            pl.BlockSpec((CHUNK_B, W_B), lambda i: (0, 0)),
        ],
        out_specs=pl.BlockSpec((tm, W_B), lambda i: (i, 0)),
        out_shape=jax.ShapeDtypeStruct((r, W_B), ACT_DT),
        compiler_params=_cparams("parallel"),
        name="gmlp",
    )(p, g_v, w_s, b_full)


def _mfeat_kernel(x_ref, prev_ref, next_ref, wc_ref, bc_ref, wqkv_ref, wif_ref, bif_ref,
                  xc_ref, q_ref, k_ref, v_ref, g_ref, *, tps):
    tm = x_ref.shape[0]
    j = pl.program_id(0) % tps
    seg_start = jnp.logical_or(j == 0, j == 1)
    seg_end = jnp.logical_or(j == 0, j == tps - 1)
    x = x_ref[...].astype(F32)
    prow = jnp.where(seg_start, 0.0, prev_ref[...].astype(F32)[HALO - 1:HALO, :])
    nrow = jnp.where(seg_end, 0.0, next_ref[...].astype(F32)[0:1, :])
    rid = lax.broadcasted_iota(jnp.int32, x.shape, 0)
    xp = jnp.where(rid == 0, prow, pltpu.roll(x, 1, 0))
    xn = jnp.where(rid == tm - 1, nrow, pltpu.roll(x, tm - 1, 0))
    xc = wc_ref[0:1, :] * xp + wc_ref[1:2, :] * x + wc_ref[2:3, :] * xn + bc_ref[...]
    xc = xc * _sigmoid(xc)
    xc_ref[...] = xc.astype(xc_ref.dtype)
    xcb = xc.astype(MXU_DT)
    xb = x_ref[...]
    qs, ks, vs = [], [], []
    for h in range(H_C):
        cols = slice(h * DH_C, (h + 1) * DH_C)
        qs.append(jnp.dot(xcb[:, cols], wqkv_ref[0, h], preferred_element_type=F32))
        ks.append(jnp.dot(xcb[:, cols], wqkv_ref[1, h], preferred_element_type=F32))
        vs.append(jnp.dot(xb[:, cols], wqkv_ref[2, h], preferred_element_type=F32))
    q = jnp.concatenate(qs, axis=1)
    k = jnp.concatenate(ks, axis=1)
    v = jnp.concatenate(vs, axis=1)
    qkv = jnp.concatenate([q, k, v], axis=1).astype(MXU_DT)
    gates = jnp.dot(qkv, wif_ref[...], preferred_element_type=F32) + bif_ref[...]
    g_ref[...] = gates.T[0:g_ref.shape[0], :]
    q_ref[...] = q.T.astype(q_ref.dtype)
    k_ref[...] = (k * (DH_C ** -0.5)).astype(k_ref.dtype)
    v_ref[...] = v.T.astype(v_ref.dtype)


def _mlstm_features(p, w_conv, b_conv, w_qkv, w_if_pad, b_if_pad, tps):
    r = p.shape[0]
    tm = ROW_TILE
    hpt = tm // HALO
    xm_blk = P_XM // W_C
    last_halo = r // HALO - 1
    kern = functools.partial(_mfeat_kernel, tps=tps)
    act = jax.ShapeDtypeStruct((r, W_C), ACT_DT)
    act_t = jax.ShapeDtypeStruct((W_C, r), ACT_DT)
    row_spec = pl.BlockSpec((tm, W_C), lambda i: (i, 0))
    col_spec = pl.BlockSpec((W_C, tm), lambda i: (0, i))
    return pl.pallas_call(
        kern,
        grid=(r // tm,),
        in_specs=[
            pl.BlockSpec((tm, W_C), lambda i: (i, xm_blk)),
            pl.BlockSpec((HALO, W_C), lambda i: (jnp.maximum(i * hpt - 1, 0), xm_blk)),
            pl.BlockSpec((HALO, W_C), lambda i: (jnp.minimum((i + 1) * hpt, last_halo), xm_blk)),
            pl.BlockSpec((CONV_K, W_C), lambda i: (0, 0)),
            pl.BlockSpec((1, W_C), lambda i: (0, 0)),
            pl.BlockSpec((3, H_C, DH_C, DH_C), lambda i: (0, 0, 0, 0)),
            pl.BlockSpec((3 * W_C, LANE), lambda i: (0, 0)),
            pl.BlockSpec((1, LANE), lambda i: (0, 0)),
        ],
        out_specs=[row_spec, col_spec, row_spec, col_spec,
                   pl.BlockSpec((2 * GATE_ROWS, tm), lambda i: (0, i))],
        out_shape=[act, act_t, act, act_t, jax.ShapeDtypeStruct((2 * GATE_ROWS, r), F32)],
        compiler_params=_cparams("parallel"),
        name="mlstm_features",
    )(p, p, p, w_conv, b_conv, w_qkv, w_if_pad, b_if_pad)


GATE_ROWS = 2 * H_C
GATE_PACK = ("r", "cmx", "bcum")


def _gate_prep_kernel(g_ref, pack_ref, rt_ref):
    seg = MLSTM_CHUNK
    gt_fwd = g_ref[0:GATE_ROWS, :]
    gt_bwd = g_ref[GATE_ROWS:2 * GATE_ROWS, :]
    n = gt_fwd.shape[1]
    lane = lax.broadcasted_iota(jnp.int32, gt_fwd.shape, 1)
    in_seg = lane & (seg - 1)

    def scan_both(xf, xb, op):
        sh = 1
        while sh < seg:
            xf = jnp.where(in_seg >= sh, op(xf, pltpu.roll(xf, sh, 1)), xf)
            xb = jnp.where(in_seg < seg - sh, op(xb, pltpu.roll(xb, n - sh, 1)), xb)
            sh *= 2
        return xf, xb

    bcum_f, bcum_b = scan_both(jax.nn.log_sigmoid(pltpu.roll(gt_fwd, H_C, 0)),
                               jax.nn.log_sigmoid(pltpu.roll(gt_bwd, H_C, 0)), jnp.add)
    r_f, r_b = gt_fwd - bcum_f, gt_bwd - bcum_b
    cmx_f, cmx_b = scan_both(r_f, r_b, jnp.maximum)
    pack_ref[...] = jnp.concatenate([r_f, cmx_f, bcum_f, r_b, cmx_b, bcum_b], axis=0)
    rt_ref[...] = jnp.concatenate([r_f, r_b, jnp.zeros((LANE - 2 * GATE_ROWS, n), F32)], axis=0).T


def _mlstm_gate_prep(gates_t, n_batch, seq):
    rows = 2 * len(GATE_PACK) * GATE_ROWS
    r = gates_t.shape[1]
    return pl.pallas_call(
        _gate_prep_kernel,
        grid=(n_batch,),
        in_specs=[pl.BlockSpec((2 * GATE_ROWS, seq), lambda b: (0, b))],
        out_specs=[pl.BlockSpec((rows, seq), lambda b: (0, b)), pl.BlockSpec((seq, LANE), lambda b: (b, 0))],
        out_shape=[jax.ShapeDtypeStruct((rows, r), F32), jax.ShapeDtypeStruct((r, LANE), F32)],
        compiler_params=_cparams("parallel"),
        name="mlstm_gate_prep",
    )(gates_t)


def _mlstm_gates(pack, m_ref, reverse):
    ln = pack.shape[1]
    part = {name: pack[i * GATE_ROWS:(i + 1) * GATE_ROWS] for i, name in enumerate(GATE_PACK)}
    end = slice(0, 1) if reverse else slice(ln - 1, ln)
    g = part["bcum"][:, end]
    r_max = part["cmx"][:, end]
    m_loc = g + r_max
    w_end = jnp.exp(part["r"] - r_max)
    m_old = m_ref[:, 0:1]
    m_new = jnp.maximum(g + m_old, m_loc)
    a_old = jnp.exp(g + m_old - m_new)
    a_new = jnp.exp(m_loc - m_new)
    mx = jnp.maximum(m_old, part["cmx"])
    w_inter = jnp.exp(m_old - mx)
    e_neg = jnp.exp(-(part["bcum"] + mx))
    s_id = lax.broadcasted_iota(jnp.int32, (ln, ln), 0)
    t_id = lax.broadcasted_iota(jnp.int32, (ln, ln), 1)
    keep = (s_id >= t_id) if reverse else (s_id <= t_id)
    return dict(mx=mx, w_inter=w_inter, e_neg=e_neg, w_end=w_end, a_old=a_old, a_new=a_new,
                m_new=m_new, keep=keep)


def _mscan_chunk(gf_ref, rf_ref, qf_ref, kf_ref, vf_ref, gb_ref, rb_ref, qb_ref, kb_ref, vb_ref,
                 hf_ref, hb_ref, cf_ref, mf_ref, cb_ref, mb_ref):
    ln = qf_ref.shape[1]
    pr = len(GATE_PACK) * GATE_ROWS
    dirs = ((_mlstm_gates(gf_ref[0:pr, :], mf_ref, False), qf_ref, kf_ref, vf_ref, cf_ref, hf_ref, rf_ref),
            (_mlstm_gates(gb_ref[pr:2 * pr, :], mb_ref, True), qb_ref, kb_ref, vb_ref, cb_ref, hb_ref, rb_ref))
    pairs = [(d, h) for d in range(2) for h in range(H_C)]
    ones_row = jnp.where(lax.broadcasted_iota(jnp.int32, (DH_C, ln), 0) == 0, 1.0, 0.0).astype(MXU_DT)
    head = lambda h: slice(h * DH_C, (h + 1) * DH_C)
    row = lambda h: slice(h, h + 1)

    ks = [dirs[d][2][:, head(h)] for d, h in pairs]
    qts = [dirs[d][1][head(h), :] for d, h in pairs]
    vexts = [jnp.concatenate([dirs[d][3][head(h), :], ones_row], axis=0) for d, h in pairs]
    cexts = [dirs[d][4][h] for d, h in pairs]
    qks = [jnp.dot(k, q_t, preferred_element_type=F32) for k, q_t in zip(ks, qts)]
    rhss = []
    for i, (d, h) in enumerate(pairs):
        g = dirs[d][0]
        r_col = dirs[d][6][:, d * GATE_ROWS + h:d * GATE_ROWS + h + 1]
        d_t = jnp.exp(jnp.where(g["keep"], r_col - g["mx"][row(h), :], NEG_BIG))
        wq = (qts[i].astype(F32) * g["w_inter"][row(h), :]).astype(MXU_DT)
        rhss.append(jnp.concatenate([(qks[i] * d_t).astype(MXU_DT), wq], axis=0))
    ress = [jnp.dot(jnp.concatenate([vexts[i], cexts[i].astype(MXU_DT)], axis=1), rhss[i],
                    preferred_element_type=F32) for i in range(len(pairs))]
    for i, (d, h) in enumerate(pairs):
        g = dirs[d][0]
        den = ress[i][DH_C:DH_C + 1, :]
        hh = ress[i][:DH_C, :] / jnp.maximum(jnp.abs(den), g["e_neg"][row(h), :])
        dirs[d][5][head(h), :] = hh.astype(dirs[d][5].dtype)
    clocs = []
    for i, (d, h) in enumerate(pairs):
        vw_t = (vexts[i].astype(F32) * dirs[d][0]["w_end"][row(h), :]).astype(MXU_DT)
        clocs.append(jnp.dot(vw_t, ks[i], preferred_element_type=F32))
    for i, (d, h) in enumerate(pairs):
        g = dirs[d][0]
        dirs[d][4][h] = g["a_old"][row(h), :] * cexts[i] + g["a_new"][row(h), :] * clocs[i]
    mf_ref[...] = jnp.broadcast_to(dirs[0][0]["m_new"], mf_ref.shape)
    mb_ref[...] = jnp.broadcast_to(dirs[1][0]["m_new"], mb_ref.shape)


def _mscan_kernel(g_ref, r_ref, q_ref, k_ref, v_ref, hf_ref, hb_ref, cf_ref, mf_ref, cb_ref, mb_ref, *, ncc):
    ln = MLSTM_CHUNK
    nck = q_ref.shape[1] // ln
    cf_ref[...] = jnp.zeros_like(cf_ref)
    cb_ref[...] = jnp.zeros_like(cb_ref)
    mf_ref[...] = jnp.zeros_like(mf_ref)
    mb_ref[...] = jnp.zeros_like(mb_ref)

    def chunk_views(c):
        at = pl.ds(pl.multiple_of(c * ln, ln), ln)
        return ((g_ref.at[:, at], r_ref.at[at, :], q_ref.at[:, at], k_ref.at[at, :], v_ref.at[:, at]),
                (hf_ref.at[:, at], hb_ref.at[:, at]))

    def trip(s, carry):
        ins_f, (hf, _) = chunk_views(s)
        ins_b, (_, hb) = chunk_views(jnp.where(s < ncc, ncc - 1 - s, nck - 1 + ncc - s))
        _mscan_chunk(*ins_f, *ins_b, hf, hb, cf_ref, mf_ref, cb_ref, mb_ref)
        return carry

    lax.fori_loop(0, nck, trip, 0)


def _mlstm_scan(q_t, k, v_t, gate_pack, r_t, n_batch, seq, ctx_rows):
    rows = lambda w: pl.BlockSpec((seq, w), lambda b: (b, 0))
    cols = lambda h: pl.BlockSpec((h, seq), lambda b: (0, b))
    out = jax.ShapeDtypeStruct(q_t.shape, ACT_DT)
    return pl.pallas_call(
        functools.partial(_mscan_kernel, ncc=ctx_rows // MLSTM_CHUNK),
        grid=(n_batch,),
        in_specs=[cols(gate_pack.shape[0]), rows(LANE), cols(W_C), rows(W_C), cols(W_C)],
        out_specs=[cols(W_C), cols(W_C)],
        out_shape=[out, out],
        scratch_shapes=[
            pltpu.VMEM((H_C, 2 * DH_C, DH_C), F32), pltpu.VMEM((8, LANE), F32),
            pltpu.VMEM((H_C, 2 * DH_C, DH_C), F32), pltpu.VMEM((8, LANE), F32),
        ],
        compiler_params=_cparams("parallel"),
        name="mlstm_scan",
    )(gate_pack, r_t, q_t, k, v_t)


def _route(logits, carry):
    lane = lax.broadcasted_iota(jnp.int32, logits.shape, 1).astype(F32)
    big = float(4 * LANE)
    gl = jnp.where(lane < N_GROUPS, logits, NEG_BIG)
    gmax = jnp.max(gl, axis=-1, keepdims=True)
    g_star = jnp.min(jnp.where(gl == gmax, lane, big), axis=-1, keepdims=True)
    p_g = 1.0 / jnp.sum(jnp.exp(gl - gmax), axis=-1, keepdims=True)
    e_lo = g_star * EXP_PER_GROUP + N_GROUPS
    in_group = jnp.logical_and(lane >= e_lo, lane < e_lo + EXP_PER_GROUP)
    el = jnp.where(in_group, logits, NEG_BIG)
    v1 = jnp.max(el, axis=-1, keepdims=True)
    i1 = jnp.min(jnp.where(el == v1, lane, big), axis=-1, keepdims=True)
    el2 = jnp.where(lane == i1, NEG_BIG, el)
    v2 = jnp.max(el2, axis=-1, keepdims=True)
    i2 = jnp.min(jnp.where(el2 == v2, lane, big), axis=-1, keepdims=True)
    e21 = jnp.exp(v2 - v1)
    w1 = p_g / (1.0 + e21)
    w2 = p_g * e21 / (1.0 + e21)
    e1 = i1 - N_GROUPS
    e2 = i2 - N_GROUPS
    oh1 = lane == e1
    oh2 = lane == e2
    both = jnp.where(jnp.logical_or(oh1, oh2), 1.0, 0.0)
    tm = logits.shape[0]
    lower = (lax.broadcasted_iota(jnp.int32, (tm, tm), 1) < lax.broadcasted_iota(jnp.int32, (tm, tm), 0))
    before = jnp.dot(jnp.where(lower, 1.0, 0.0).astype(MXU_DT), both.astype(MXU_DT),
                     preferred_element_type=F32) + carry
    rank1 = jnp.sum(jnp.where(oh1, before, 0.0), axis=-1, keepdims=True)
    rank2 = jnp.sum(jnp.where(oh2, before, 0.0), axis=-1, keepdims=True)
    out = jnp.where(lane == 0, e1, 0.0)
    out = jnp.where(lane == 1, e2, out)
    out = jnp.where(lane == 2, w1, out)
    out = jnp.where(lane == 3, w2, out)
    out = jnp.where(lane == 4, rank1, out)
    out = jnp.where(lane == 5, rank2, out)
    return out, carry + jnp.sum(both, axis=0, keepdims=True)


def _merge_kernel(x_ref, gate_ref, z_ref, ya_ref, yb_ref, hf_ref, hb_ref, xc_ref, gm_ref, sk_ref,
                  wbr_ref, wo_ref, mod_ref, g2_ref, wr_ref, br_ref, xo_ref, rt_ref, cnt_ref, carry_ref):
    @pl.when(pl.program_id(0) == 0)
    def _():
        carry_ref[...] = jnp.zeros_like(carry_ref)

    hs_t = hf_ref[...].astype(F32) + hb_ref[...].astype(F32)
    hn_t = []
    for h in range(H_C):
        part = hs_t[h * DH_C:(h + 1) * DH_C, :]
        hn_t.append(part * lax.rsqrt(jnp.mean(part * part, axis=0, keepdims=True) + EPS))
    hn = jnp.concatenate(hn_t, axis=0).T * gm_ref[...]
    yc = (hn + sk_ref[...] * xc_ref[...].astype(F32)) * _sigmoid(z_ref[...].astype(F32))
    ys = (ya_ref[...], yb_ref[...], yc.astype(MXU_DT))
    merged = None
    for i in range(3):
        gate = _sigmoid(gate_ref[:, i * D_MODEL:(i + 1) * D_MODEL].astype(MXU_DT))
        term = gate * jnp.dot(ys[i], wbr_ref[i], preferred_element_type=F32).astype(MXU_DT)
        merged = term if merged is None else merged + term
    y = jnp.dot(merged, wo_ref[...], preferred_element_type=F32)
    xn = x_ref[...] + mod_ref[2:3, :] * y
    xo_ref[...] = xn
    h2 = _rms(xn) * g2_ref[...] * (1.0 + mod_ref[4:5, :]) + mod_ref[3:4, :]
    h_hi, h_lo = _split_bf16(h2)
    logits = (jnp.dot(h_hi, wr_ref[0], preferred_element_type=F32)
              + jnp.dot(h_lo, wr_ref[0], preferred_element_type=F32)
              + jnp.dot(h_hi, wr_ref[1], preferred_element_type=F32)) + br_ref[...]
    route, carry = _route(logits, carry_ref[0:1, :])
    rt_ref[...] = route
    carry_ref[0:1, :] = carry
    cnt_ref[...] = jnp.broadcast_to(carry, cnt_ref.shape)


def _merge(xall, p, ya, yb, hf, hb, xconv, g_m, skip, w_br, w_o, mod, g2, w_route, b_route, tiles):
    d = xall.shape[1]
    tm = ROW_TILE
    r = tiles.count * tm

    def mod_idx(i):
        return (tiles.mod_row(i), 0, 0)

    row = lambda w: pl.BlockSpec((tm, w), lambda i: (i, 0))
    frow = lambda w: pl.BlockSpec((tm, w), lambda i: (tiles.full(i), 0))
    fcol = lambda w: pl.BlockSpec((w, tm), lambda i: (0, tiles.full(i)))
    full = lambda shape: pl.BlockSpec(shape, lambda i: (0,) * len(shape))
    return pl.pallas_call(
        _merge_kernel,
        grid=(tiles.count,),
        in_specs=[
            frow(d),
            pl.BlockSpec((tm, GATE_W), lambda i: (tiles.full(i), P_GATE // GATE_W)),
            pl.BlockSpec((tm, W_C), lambda i: (tiles.full(i), P_Z // W_C)),
            row(W_BR), row(W_BR), fcol(W_C), fcol(W_C), frow(W_C),
            full((1, W_C)), full((1, W_C)),
            full((3, W_BR, d)), full((d, d)),
            pl.BlockSpec((None, N_MOD, d), mod_idx),
            full((1, d)), full((2, d, LANE)), full((1, LANE)),
        ],
        out_specs=[row(d), row(LANE), full((8, LANE))],
        out_shape=[jax.ShapeDtypeStruct((r, d), F32), jax.ShapeDtypeStruct((r, LANE), F32),
                   jax.ShapeDtypeStruct((8, LANE), F32)],
        scratch_shapes=[pltpu.VMEM((8, LANE), F32)],
        compiler_params=_cparams("arbitrary"),
        name="merge_route",
    )(xall, p, p, ya, yb, hf, hb, xconv, g_m, skip, w_br, w_o, mod, g2, w_route, b_route)


def _row_copy(src_hbm, src_row, dst_ref, dst_row, sem):
    return pltpu.make_async_copy(src_hbm.at[pl.ds(src_row, 1), :], dst_ref.at[pl.ds(dst_row, 1), :], sem)


def _dispatch_kernel(fill_ref, nlive_ref, pos_ref, x_ref, g2_ref, mod_ref, xs_hbm, hbuf, zbuf, zsem, sem):
    i = pl.program_id(0)
    n = pl.num_programs(0)
    tm = x_ref.shape[0]
    slot = i % 2
    bm = zbuf.shape[0]
    n_blocks = xs_hbm.shape[0] // bm

    @pl.when(i == 0)
    def _():
        zbuf[...] = jnp.zeros_like(zbuf)

        def fill_block(b):
            return pltpu.make_async_copy(zbuf, xs_hbm.at[pl.ds(b * bm, bm), :], zsem)

        def fill_last(k, carry):
            fill_block(fill_ref[k]).start()
            return carry

        def fill_unused(b, carry):
            fill_block(b).start()
            return carry

        def wait_fill(k, carry):
            fill_block(0).wait()
            return carry

        n_live = nlive_ref[0]
        n_used = nlive_ref[1]
        lax.fori_loop(0, n_used, fill_last, 0)
        lax.fori_loop(n_live, n_blocks, fill_unused, 0)
        lax.fori_loop(0, n_used + n_blocks - n_live, wait_fill, 0)

    hbuf[slot] = _rms(x_ref[...]) * g2_ref[...] * (1.0 + mod_ref[4:5, :]) + mod_ref[3:4, :]

    def body(c, carry):
        for u in range(DMA_UNROLL):
            r = c * DMA_UNROLL + u
            for j in range(TOP_K):
                pltpu.make_async_copy(hbuf.at[slot, pl.ds(r, 1), :],
                                      xs_hbm.at[pl.ds(pos_ref[0, j * tm + r], 1), :], sem.at[slot]).start()
        return carry

    lax.fori_loop(0, tm // DMA_UNROLL, body, 0)

    def wait_tile(s):
        pltpu.make_async_copy(hbuf.at[s], xs_hbm.at[pl.ds(0, tm), :], sem.at[s]).wait()
        pltpu.make_async_copy(hbuf.at[s], xs_hbm.at[pl.ds(0, tm), :], sem.at[s]).wait()

    @pl.when(i > 0)
    def _():
        wait_tile(1 - slot)

    @pl.when(i == n - 1)
    def _():
        wait_tile(slot)


def _moe_dispatch(xall, g2, mod, fill_blocks, n_live, pos_tiles, n_slots, tiles):
    r, d = xall.shape
    tm = ROW_TILE
    assert TOP_K == 2 and r == tiles.count * tm

    def mod_idx(i, fb, nl):
        return (tiles.mod_row(i), 0, 0)

    grid_spec = pltpu.PrefetchScalarGridSpec(
        num_scalar_prefetch=2,
        grid=(r // tm,),
        in_specs=[
            pl.BlockSpec((None, 1, TOP_K * tm), lambda i, fb, nl: (i, 0, 0), memory_space=pltpu.SMEM),
            pl.BlockSpec((tm, d), lambda i, fb, nl: (i, 0)),
            pl.BlockSpec((1, d), lambda i, fb, nl: (0, 0)),
            pl.BlockSpec((None, N_MOD, d), mod_idx),
        ],
        out_specs=pl.BlockSpec(memory_space=pl.ANY),
        scratch_shapes=[pltpu.VMEM((2, tm, d), F32), pltpu.VMEM((MOE_BLOCK, d), F32),
                        pltpu.SemaphoreType.DMA(()), pltpu.SemaphoreType.DMA((2,))],
    )
    return pl.pallas_call(
        _dispatch_kernel,
        grid_spec=grid_spec,
        out_shape=jax.ShapeDtypeStruct((n_slots, d), F32),
        compiler_params=_cparams("arbitrary"),
        name="moe_dispatch",
    )(fill_blocks, n_live, pos_tiles, xall, g2, mod)


def _moe_kernel(be_ref, nv_ref, x_ref, wg_ref, wu_ref, wd_ref, y_ref, wg_mx, wu_mx, wd_mx):
    i = pl.program_id(0)
    n_valid = nv_ref[i]

    @pl.when(jnp.logical_or(i == 0, be_ref[i] != be_ref[jnp.maximum(i - 1, 0)]))
    def _():
        wg_mx[...] = wg_ref[...].astype(wg_mx.dtype)
        wu_mx[...] = wu_ref[...].astype(wu_mx.dtype)
        wd_mx[...] = wd_ref[...].astype(wd_mx.dtype)

    @pl.when(n_valid > 0)
    def _():
        x = x_ref[...].astype(MXU_DT)
        a = jnp.dot(x, wg_mx[...], preferred_element_type=F32)
        u = jnp.dot(x, wu_mx[...], preferred_element_type=F32)
        act = (a * _sigmoid(a) * u).astype(MXU_DT)
        y_ref[...] = jnp.dot(act, wd_mx[...], preferred_element_type=F32)

    @pl.when(n_valid <= 0)
    def _():
        y_ref[...] = jnp.zeros_like(y_ref)


def _moe_experts(xs, block_e, n_valid, w_gate, w_up, w_down, layer):
    n_slots, d = xs.shape
    bm = MOE_BLOCK
    grid_spec = pltpu.PrefetchScalarGridSpec(
        num_scalar_prefetch=2,
        grid=(n_slots // bm,),
        in_specs=[
            pl.BlockSpec((bm, d), lambda i, be, nv: (i, 0)),
            pl.BlockSpec((None, None, d, D_EXPERT), lambda i, be, nv: (layer, be[i], 0, 0)),
            pl.BlockSpec((None, None, d, D_EXPERT), lambda i, be, nv: (layer, be[i], 0, 0)),
            pl.BlockSpec((None, None, D_EXPERT, d), lambda i, be, nv: (layer, be[i], 0, 0)),
        ],
        out_specs=pl.BlockSpec((bm, d), lambda i, be, nv: (i, 0)),
        scratch_shapes=[pltpu.VMEM((d, D_EXPERT), MXU_DT), pltpu.VMEM((d, D_EXPERT), MXU_DT),
                        pltpu.VMEM((D_EXPERT, d), MXU_DT)],
    )
    return pl.pallas_call(
        _moe_kernel,
        grid_spec=grid_spec,
        out_shape=jax.ShapeDtypeStruct((n_slots, d), F32),
        compiler_params=_cparams("arbitrary"),
        name="moe_experts",
    )(block_e, n_valid, xs, w_gate, w_up, w_down)


def _combine_kernel(pos_ref, pos_next_ref, x_ref, rt_ref, mod_ref, gf_ref, yb_hbm, xo_ref, ybuf, sem, *, final):
    i = pl.program_id(0)
    n = pl.num_programs(0)
    tm = x_ref.shape[0]

    def start_gather(ids_ref, slot):
        def body(c, carry):
            for u in range(DMA_UNROLL):
                r = c * DMA_UNROLL + u
                _row_copy(yb_hbm, ids_ref[0, r], ybuf.at[slot], r, sem.at[slot]).start()
            return carry
        lax.fori_loop(0, TOP_K * tm // DMA_UNROLL, body, 0)

    @pl.when(i == 0)
    def _():
        start_gather(pos_ref, 0)

    @pl.when(i + 1 < n)
    def _():
        start_gather(pos_next_ref, (i + 1) % 2)

    slot = i % 2
    pltpu.make_async_copy(yb_hbm.at[pl.ds(0, TOP_K * tm), :], ybuf.at[slot], sem.at[slot]).wait()
    rt = rt_ref[...]
    f = rt[:, 2:3] * ybuf[slot, 0:tm, :] + rt[:, 3:4] * ybuf[slot, tm:2 * tm, :]
    xn = x_ref[...] + mod_ref[5:6, :] * f
    xo_ref[...] = _rms(xn) * gf_ref[...] if final else xn


def _combine(xall, route, mod, g_final, yb, pos, tiles, final):
    r, d = xall.shape
    tm = ROW_TILE
    nt = tiles.count
    last = nt - 1
    assert r == nt * tm

    def mod_idx(i):
        return (tiles.mod_row(i), 0, 0)

    row = lambda w: pl.BlockSpec((tm, w), lambda i: (i, 0))
    return pl.pallas_call(
        functools.partial(_combine_kernel, final=final),
        grid=(nt,),
        in_specs=[
            pl.BlockSpec((None, 1, TOP_K * tm), lambda i: (i, 0, 0), memory_space=pltpu.SMEM),
            pl.BlockSpec((None, 1, TOP_K * tm), lambda i: (jnp.minimum(i + 1, last), 0, 0),
                         memory_space=pltpu.SMEM),
            row(d), row(LANE),
            pl.BlockSpec((None, N_MOD, d), mod_idx),
            pl.BlockSpec((1, d), lambda i: (0, 0)),
            pl.BlockSpec(memory_space=pl.ANY),
        ],
        out_specs=row(d),
        out_shape=jax.ShapeDtypeStruct((r, d), F32),
        scratch_shapes=[pltpu.VMEM((2, TOP_K * tm, d), F32), pltpu.SemaphoreType.DMA((2,))],
        compiler_params=_cparams("arbitrary"),
        name="moe_combine",
    )(pos, pos, xall, route, mod, g_final, yb)


def _dispatch_plan(route, counts_f, n_tiles):
    n = route.shape[0]
    bm = MOE_BLOCK
    n_blocks = -(-(n * TOP_K + N_EXPERTS * (bm - 1)) // bm)
    counts = counts_f[:N_EXPERTS].astype(jnp.int32)
    padded = (counts + bm - 1) // bm * bm
    pend = jnp.cumsum(padded)
    seg_off = pend - padded
    eid = route[:, 0:TOP_K].astype(jnp.int32)
    rank = route[:, 4:4 + TOP_K].astype(jnp.int32)
    experts = jnp.arange(N_EXPERTS, dtype=jnp.int32)
    pos = jnp.sum(jnp.where(eid[..., None] == experts, seg_off, 0), axis=-1) + rank
    blk0 = jnp.arange(n_blocks, dtype=jnp.int32) * bm
    block_e = jnp.minimum(jnp.sum((blk0[:, None] >= pend[None, :]).astype(jnp.int32), axis=1), N_EXPERTS - 1)
    seg_end = seg_off + counts
    n_valid = jnp.clip(seg_end[block_e] - blk0, 0, bm).astype(jnp.int32)
    pos_tiles = pos.reshape(n_tiles, n // n_tiles, TOP_K).transpose(0, 2, 1).reshape(n_tiles, 1, -1)
    used = counts > 0
    last_blocks = jnp.sort(jnp.where(used, pend // bm - 1, n_blocks)).astype(jnp.int32)
    n_live = jnp.stack([pend[-1] // bm, jnp.sum(used)]).astype(jnp.int32)
    return block_e.astype(jnp.int32), n_valid, last_blocks, n_live, pos_tiles, n_blocks * bm


def _rope_tables(seq, ctx_rows):
    t = jnp.arange(seq - ctx_rows)
    inv = ROPE_BASE ** (-jnp.arange(0, ROT_AX, 2, dtype=F32) / ROT_AX)
    ang_r = (t // GRID_W).astype(F32)[:, None] * inv
    ang_c = (t % GRID_W).astype(F32)[:, None] * inv
    cos64 = jnp.concatenate([jnp.cos(ang_r)] * 2 + [jnp.cos(ang_c)] * 2, axis=1)
    sin64 = jnp.concatenate([-jnp.sin(ang_r), jnp.sin(ang_r), -jnp.sin(ang_c), jnp.sin(ang_c)], axis=1)
    cos = jnp.concatenate([jnp.ones((ctx_rows, DK), F32), cos64], axis=0)
    sin = jnp.concatenate([jnp.zeros((ctx_rows, DK), F32), sin64], axis=0)
    return jnp.tile(cos, (1, LANE // DK)), jnp.tile(sin, (1, LANE // DK))


def kernel(x, c, ctx, c_ctx, w_mod, b_mod, g_norm1, g_norm2, w_in, diff_lambda, g_diff_subln, w_gmlp_s, b_gmlp_s, g_gmlp_v, w_conv_m, b_conv_m, w_qkv_m, w_if_m, b_if_m, g_mlstm_norm, skip_m, w_branch, w_out, w_route_g, b_route_g, w_route_e, b_route_e, w_e_gate, w_e_up, w_e_down, g_final):
    n_batch, t_lat, d = x.shape
    ctx_rows = ctx.shape[1]
    depth = w_in.shape[0]
    seq = ctx_rows + t_lat
    assert d == D_MODEL and ctx_rows == ROW_TILE and t_lat % ROW_TILE == 0 and t_lat % GRID_W == 0
    tps = seq // ROW_TILE
    all_tiles = _Tiles(n_batch, tps, False)

    xall = jnp.concatenate([ctx, x], axis=1).reshape(n_batch * seq, d)
    mb = -(-(n_batch + 1) // 8) * 8
    c_all = jnp.zeros((mb, d), F32).at[:n_batch].set(c).at[n_batch].set(c_ctx)
    mod_all = _modulation(c_all, w_mod, b_mod)[:, :n_batch + 1].reshape(depth, n_batch + 1, N_MOD, d)
    cos, sin_signed = _rope_tables(seq, ctx_rows)
    w_in_b = w_in.astype(MXU_DT)

    for l in range(depth):
        lam_init = 0.8 - 0.6 * math.exp(-0.3 * l)
        mod = mod_all[l]
        last = l == depth - 1
        tiles = _Tiles(n_batch, tps, last)
        p, k_t = _inproj(xall, g_norm1[l][None], mod, cos, sin_signed, w_in_b, l, all_tiles)

        ya = _attention(p, k_t, diff_lambda[l], g_diff_subln[l][None], lam_init, tiles, ctx_rows)

        b_full = jnp.repeat(b_gmlp_s[l].T, W_B // G_B, axis=1)
        yb = _gmlp(p, g_gmlp_v[l][None], w_gmlp_s[l].astype(MXU_DT), b_full, tiles)

        w_if_pad = jnp.zeros((3 * W_C, LANE), F32).at[:, :4 * H_C].set(w_if_m[l]).astype(MXU_DT)
        b_if_pad = jnp.zeros((1, LANE), F32).at[0, :4 * H_C].set(b_if_m[l])
        xconv, q_m, k_m, v_m, gates_t = _mlstm_features(
            p, w_conv_m[l], b_conv_m[l][None], w_qkv_m[l].astype(MXU_DT), w_if_pad, b_if_pad, tps)
        gate_pack, r_t = _mlstm_gate_prep(gates_t, n_batch, seq)
        hf, hb = _mlstm_scan(q_m, k_m, v_m, gate_pack, r_t, n_batch, seq, ctx_rows)

        w_route = (jnp.zeros((d, LANE), F32).at[:, :N_GROUPS].set(w_route_g[l])
                   .at[:, N_GROUPS:N_GROUPS + N_EXPERTS].set(w_route_e[l]))
        w_route = jnp.stack(_split_bf16(w_route))
        b_route = (jnp.zeros((1, LANE), F32).at[0, :N_GROUPS].set(b_route_g[l])
                   .at[0, N_GROUPS:N_GROUPS + N_EXPERTS].set(b_route_e[l]))
        xall, route, counts = _merge(xall, p, ya, yb, hf, hb, xconv, g_mlstm_norm[l][None], skip_m[l][None],
                                         w_branch[l].astype(MXU_DT), w_out[l].astype(MXU_DT), mod,
                                         g_norm2[l][None], w_route, b_route, tiles)

        block_e, n_valid, fill_blocks, n_live, pos_tiles, n_slots = _dispatch_plan(route, counts[0], tiles.count)
        xs = _moe_dispatch(xall, g_norm2[l][None], mod, fill_blocks, n_live, pos_tiles, n_slots, tiles)
        y_sorted = _moe_experts(xs, block_e, n_valid, w_e_gate, w_e_up, w_e_down, l)
        xall = _combine(xall, route, mod, g_final[None], y_sorted, pos_tiles, tiles, last)

    return xall.reshape(n_batch, t_lat, d)
```

```python
import functools
import math
from typing import NamedTuple

import jax
import jax.numpy as jnp
from jax import lax
from jax.experimental import pallas as pl
from jax.experimental.pallas import tpu as pltpu

F32 = jnp.float32
MXU_DT = jnp.bfloat16
ACT_DT = jnp.bfloat16
HIGHEST = lax.Precision.HIGHEST

D_MODEL = 1024
N_MOD = 6
EPS = 1e-6
GRID_W = 64
H_A, DK = 4, 64
DV = 2 * DK
ROT_AX = DK // 2
ROPE_BASE = 10000.0
W_B, G_B, CHUNK_B = 512, 4, 128
H_C, DH_C, CONV_K, MLSTM_CHUNK = 4, 128, 3, 128
W_C = H_C * DH_C
N_GROUPS, EXP_PER_GROUP, TOP_K, D_EXPERT = 4, 8, 2, 512
N_EXPERTS = N_GROUPS * EXP_PER_GROUP
W_BR = 512

GATE_W = 3 * D_MODEL
P_GATE = 0
P_K = GATE_W
P_V = P_K + 512
P_XM = P_V + 512
P_Q = P_XM + 512
P_UV = P_Q + 512
P_Z = P_UV + 2 * W_B
IN_COLS = P_Z + W_C
ORIG_GATE0 = IN_COLS - GATE_W

LANE = 128
ROW_TILE = 256
PROJ_CHUNK = 512
MOE_BLOCK = 512
HALO = 16
DMA_UNROLL = 8
ATTN_KV_CHUNK = 256
GATHER_GROUPS = 16
LOG2_E = math.log2(math.e)
VMEM_LIMIT = 56 * 1024 * 1024
NEG_BIG = -1e30


class _Tiles(NamedTuple):
    n_batch: int
    tps: int
    skip_ctx: bool

    @property
    def per_seq(self):
        return self.tps - 1 if self.skip_ctx else self.tps

    @property
    def count(self):
        return self.n_batch * self.per_seq

    def full(self, i):
        if not self.skip_ctx:
            return i
        return (i // self.per_seq) * self.tps + 1 + i % self.per_seq

    def mod_row(self, i):
        if self.skip_ctx:
            return i // self.per_seq
        return jnp.where(i % self.tps == 0, self.n_batch, i // self.tps)


def _cparams(*sem):
    return pltpu.CompilerParams(dimension_semantics=sem, vmem_limit_bytes=VMEM_LIMIT)


def _rms(x):
    return x * lax.rsqrt(jnp.mean(x * x, axis=-1, keepdims=True) + EPS)


def _sigmoid(x):
    return 0.5 * jnp.tanh(0.5 * x) + 0.5


def _split_bf16(x):
    hi = x.astype(jnp.bfloat16)
    return hi, (x - hi.astype(F32)).astype(jnp.bfloat16)


def _mod_kernel(c_ref, w_ref, b_ref, o_ref):
    c = c_ref[...]
    s = c * jax.nn.sigmoid(c)
    o_ref[...] = jnp.dot(s, w_ref[...], preferred_element_type=F32, precision=HIGHEST) + b_ref[...]


def _modulation(c_all, w_mod, b_mod):
    n_layer = w_mod.shape[0]
    mb, d = c_all.shape
    tn = 1024
    return pl.pallas_call(
        _mod_kernel,
        grid=(n_layer, N_MOD * d // tn),
        in_specs=[
            pl.BlockSpec((mb, d), lambda l, j: (0, 0)),
            pl.BlockSpec((None, d, tn), lambda l, j: (l, 0, j)),
            pl.BlockSpec((None, 1, tn), lambda l, j: (l, 0, j)),
        ],
        out_specs=pl.BlockSpec((None, mb, tn), lambda l, j: (l, 0, j)),
        out_shape=jax.ShapeDtypeStruct((n_layer, mb, N_MOD * d), F32),
        compiler_params=_cparams("parallel", "parallel"),
        name="modulation",
    )(c_all, w_mod, b_mod.reshape(n_layer, 1, N_MOD * d))


def _rope(acc, cos, sin_signed):
    w = acc.shape[1]
    half = ROT_AX // 2
    lane = lax.broadcasted_iota(jnp.int32, acc.shape, 1)
    partner = jnp.where((lane & half) == 0, pltpu.roll(acc, w - half, 1), pltpu.roll(acc, half, 1))
    reps = w // cos.shape[1]
    return acc * jnp.tile(cos, (1, reps)) + partner * jnp.tile(sin_signed, (1, reps))


def _project_chunks(x, g_ref, mod_ref, cos_ref, sin_ref, w_ref, o_ref, kt_ref):
    h = (_rms(x) * g_ref[...] * (1.0 + mod_ref[1:2, :]) + mod_ref[0:1, :]).astype(MXU_DT)
    k_chunk = P_K // PROJ_CHUNK
    q_chunk = P_Q // PROJ_CHUNK
    for c in range(IN_COLS // PROJ_CHUNK):
        w0 = (ORIG_GATE0 + c * PROJ_CHUNK) % IN_COLS
        acc = jnp.dot(h, w_ref[:, w0:w0 + PROJ_CHUNK], preferred_element_type=F32)
        if c == k_chunk:
            acc = _rope(acc, cos_ref[...], sin_ref[...])
            kt_ref[...] = acc.T.astype(kt_ref.dtype)
        elif c == q_chunk:
            acc = _rope(acc, cos_ref[...], sin_ref[...]) * (DK ** -0.5 * LOG2_E)
        o_ref[:, c * PROJ_CHUNK:(c + 1) * PROJ_CHUNK] = acc.astype(o_ref.dtype)
        yield


def _inproj_kernel(x_ref, g_ref, mod_ref, cos_ref, sin_ref, w_ref, o_ref, kt_ref):
    for _ in _project_chunks(x_ref[...], g_ref, mod_ref, cos_ref, sin_ref, w_ref, o_ref, kt_ref):
        pass


def _combine_inproj_kernel(pos_ref, pos_next_ref, x_ref, rt_ref, modp_ref, yb_hbm, g_ref, mod_ref, cos_ref, sin_ref,
                           w_ref, o_ref, kt_ref, xo_ref, ybuf, sem):
    i = pl.program_id(0)
    n = pl.num_programs(0)
    tm = x_ref.shape[0]
    n_rows = TOP_K * tm

    def gather_chunks(ids_ref, slot, n_chunks):
        per = n_rows // n_chunks
        for c in range(n_chunks):
            for r in range(c * per, (c + 1) * per):
                _row_copy(yb_hbm, ids_ref[0, r], ybuf.at[slot], r, sem.at[slot]).start()
            yield

    def wait_gather(slot):
        pltpu.make_async_copy(yb_hbm.at[pl.ds(0, n_rows), :], ybuf.at[slot], sem.at[slot]).wait()

    @pl.when(i == 0)
    def _():
        for _ in gather_chunks(pos_ref, 0, 1):
            pass

    slot = i % 2
    wait_gather(slot)
    rt = rt_ref[...]
    f = rt[:, 2:3] * ybuf[slot, 0:tm, :] + rt[:, 3:4] * ybuf[slot, tm:2 * tm, :]
    xn = x_ref[...] + modp_ref[5:6, :] * f
    xo_ref[...] = xn
    project = _project_chunks(xn, g_ref, mod_ref, cos_ref, sin_ref, w_ref, o_ref, kt_ref)
    gather = gather_chunks(pos_next_ref, 1 - slot, GATHER_GROUPS)
    for _ in range(max(IN_COLS // PROJ_CHUNK, GATHER_GROUPS)):
        next(project, None)
        next(gather, None)

    @pl.when(i == n - 1)
    def _():
        wait_gather(1 - slot)


def _inproj(xall, g1, mod, cos, sin_signed, w_in, layer, tiles):
    r, d = xall.shape
    tm = ROW_TILE
    tps = tiles.tps
    assert ORIG_GATE0 % PROJ_CHUNK == 0 and not tiles.skip_ctx

    def mod_idx(i):
        return (tiles.mod_row(i), 0, 0)

    return pl.pallas_call(
        _inproj_kernel,
        grid=(r // tm,),
        in_specs=[
            pl.BlockSpec((tm, d), lambda i: (i, 0)),
            pl.BlockSpec((1, d), lambda i: (0, 0)),
            pl.BlockSpec((None, N_MOD, d), mod_idx),
            pl.BlockSpec((tm, LANE), lambda i: (i % tps, 0)),
            pl.BlockSpec((tm, LANE), lambda i: (i % tps, 0)),
            pl.BlockSpec((None, d, IN_COLS), lambda i: (layer, 0, 0)),
        ],
        out_specs=[pl.BlockSpec((tm, IN_COLS), lambda i: (i, 0)),
                   pl.BlockSpec((PROJ_CHUNK, tm), lambda i: (0, i))],
        out_shape=[jax.ShapeDtypeStruct((r, IN_COLS), ACT_DT), jax.ShapeDtypeStruct((PROJ_CHUNK, r), ACT_DT)],
        compiler_params=_cparams("parallel"),
        name="inproj",
    )(xall, g1, mod, cos, sin_signed, w_in)


def _combine_inproj(xall, route, mod_prev, y_sorted, pos, g1, mod, cos, sin_signed, w_in, layer, tiles):
    r, d = xall.shape
    tm = ROW_TILE
    tps = tiles.tps
    nt = tiles.count
    assert not tiles.skip_ctx and r == nt * tm

    def mod_idx(i):
        return (tiles.mod_row(i), 0, 0)

    row = lambda w: pl.BlockSpec((tm, w), lambda i: (i, 0))
    pos_spec = lambda idx: pl.BlockSpec((None, 1, TOP_K * tm), idx, memory_space=pltpu.SMEM)
    return pl.pallas_call(
        _combine_inproj_kernel,
        grid=(nt,),
        in_specs=[
            pos_spec(lambda i: (i, 0, 0)),
            pos_spec(lambda i: (jnp.minimum(i + 1, nt - 1), 0, 0)),
            row(d), row(LANE),
            pl.BlockSpec((None, N_MOD, d), mod_idx),
            pl.BlockSpec(memory_space=pl.ANY),
            pl.BlockSpec((1, d), lambda i: (0, 0)),
            pl.BlockSpec((None, N_MOD, d), mod_idx),
            pl.BlockSpec((tm, LANE), lambda i: (i % tps, 0)),
            pl.BlockSpec((tm, LANE), lambda i: (i % tps, 0)),
            pl.BlockSpec((None, d, IN_COLS), lambda i: (layer, 0, 0)),
        ],
        out_specs=[pl.BlockSpec((tm, IN_COLS), lambda i: (i, 0)),
                   pl.BlockSpec((PROJ_CHUNK, tm), lambda i: (0, i)),
                   row(d)],
        out_shape=[jax.ShapeDtypeStruct((r, IN_COLS), ACT_DT), jax.ShapeDtypeStruct((PROJ_CHUNK, r), ACT_DT),
                   jax.ShapeDtypeStruct((r, d), F32)],
        scratch_shapes=[pltpu.VMEM((2, TOP_K * tm, d), F32), pltpu.SemaphoreType.DMA((2,))],
        compiler_params=_cparams("arbitrary"),
        name="combine_inproj",
    )(pos, pos, xall, route, mod_prev, y_sorted, g1, mod, cos, sin_signed, w_in)


def _attn_kernel(dl_ref, g_ref, q_ref, kt_ref, v_ref, o_ref, vext_ref, s_ref, mx_ref, *,
                 lam_init, ctx_rows, has_ctx):
    tq = s_ref.shape[2]
    seq = kt_ref.shape[1]
    one_hot = lax.broadcasted_iota(jnp.int32, v_ref.shape, 1) == 0
    vext_ref[:, :DV] = v_ref[...]
    vext_ref[:, DV:] = jnp.where(one_hot, 1.0, 0.0).astype(vext_ref.dtype)

    dl = dl_ref[...]
    lam = (jnp.exp(jnp.sum(dl[0:1] * dl[1:2], keepdims=True))
           - jnp.exp(jnp.sum(dl[2:3] * dl[3:4], keepdims=True)) + lam_init)
    def chunk_list(kv_rows):
        kc = min(ATTN_KV_CHUNK, kv_rows)
        assert kv_rows % kc == 0
        return [slice(c * kc, (c + 1) * kc) for c in range(kv_rows // kc)]

    def score_pass(kv_rows, q_rows, buf):
        q = q_ref[q_rows, :]
        lane = lax.broadcasted_iota(jnp.int32, q.shape, 1)
        zero = jnp.zeros_like(q)
        qms = (jnp.where(lane < DK, q, zero), jnp.where(lane >= DK, q, zero))
        mruns = [None, None]
        for cols in chunk_list(kv_rows):
            for m in range(2):
                s = jnp.dot(qms[m], kt_ref[:, cols], preferred_element_type=F32)
                s_ref[buf, m, :, cols] = s
                for g in range(s.shape[1] // LANE):
                    blk = s[:, g * LANE:(g + 1) * LANE]
                    mruns[m] = blk if mruns[m] is None else jnp.maximum(mruns[m], blk)
            yield
        for m in range(2):
            mx = jnp.max(mruns[m], axis=-1, keepdims=True)
            mx_ref[buf, m] = jnp.broadcast_to(mx, mx_ref.shape[2:])
        yield

    def value_pass(kv_rows, out_rows, buf):
        mxs = [mx_ref[buf, m][:, 0:1] for m in range(2)]
        accs = [jnp.zeros((tq, 2 * DV), F32)] * 2
        for cols in chunk_list(kv_rows):
            for m in range(2):
                p = jnp.exp2(s_ref[buf, m, :, cols] - mxs[m])
                accs[m] = accs[m] + jnp.dot(p.astype(MXU_DT), vext_ref[cols, :], preferred_element_type=F32)
            yield
        o = (accs[0][:, :DV] * (1.0 / accs[0][:, DV:DV + 1])
             - accs[1][:, :DV] * (lam / accs[1][:, DV:DV + 1]))
        o_ref[out_rows, :] = (_rms(o) * g_ref[...] * (1.0 - lam_init)).astype(o_ref.dtype)
        yield

    def run(*passes):
        live = list(passes)
        while live:
            live = [g for g in live if next(g, "done") != "done"]

    out_off = 0
    if has_ctx:
        run(score_pass(ctx_rows, slice(0, ctx_rows), 0))
        run(value_pass(ctx_rows, slice(0, ctx_rows), 0))
        out_off = ctx_rows

    n_lat = (seq - ctx_rows) // tq
    q_at = lambda i: pl.ds(pl.multiple_of(ctx_rows + i * tq, tq), tq)
    o_at = lambda i: pl.ds(pl.multiple_of(out_off + i * tq, tq), tq)
    assert n_lat % 2 == 0
    run(score_pass(seq, q_at(0), 0))

    def tile_pair(j, carry):
        a = 2 * j
        run(score_pass(seq, q_at(a + 1), 1), value_pass(seq, o_at(a), 0))
        run(score_pass(seq, q_at(a + 2), 0), value_pass(seq, o_at(a + 1), 1))
        return carry

    lax.fori_loop(0, n_lat // 2 - 1, tile_pair, 0)
    run(score_pass(seq, q_at(n_lat - 1), 1), value_pass(seq, o_at(n_lat - 2), 0))
    run(value_pass(seq, o_at(n_lat - 1), 1))


def _attention(p, k_t, diff_lambda, g_sub, lam_init, tiles, ctx_rows):
    tq = ROW_TILE
    tps = tiles.tps
    seq = tps * tq
    out_rows = tiles.per_seq * tq
    assert ctx_rows == tq
    kern = functools.partial(_attn_kernel, lam_init=lam_init, ctx_rows=ctx_rows, has_ctx=not tiles.skip_ctx)
    return pl.pallas_call(
        kern,
        grid=(tiles.n_batch, H_A),
        in_specs=[
            pl.BlockSpec((4, DK), lambda b, h: (0, 0)),
            pl.BlockSpec((1, DV), lambda b, h: (0, 0)),
            pl.BlockSpec((seq, LANE), lambda b, h: (b, P_Q // LANE + h)),
            pl.BlockSpec((LANE, seq), lambda b, h: (h, b)),
            pl.BlockSpec((seq, LANE), lambda b, h: (b, P_V // LANE + h)),
        ],
        out_specs=pl.BlockSpec((out_rows, LANE), lambda b, h: (b, h)),
        out_shape=jax.ShapeDtypeStruct((tiles.count * tq, H_A * DV), ACT_DT),
        scratch_shapes=[pltpu.VMEM((seq, 2 * DV), MXU_DT), pltpu.VMEM((2, 2, tq, seq), F32),
                        pltpu.VMEM((2, 2, tq, LANE), F32)],
        compiler_params=_cparams("parallel", "parallel"),
        name="diff_attention",
    )(diff_lambda, g_sub, p, k_t, p)


def _gmlp_kernel(uv_ref, gv_ref, ws_ref, bs_ref, o_ref):
    uv = jax.nn.gelu(uv_ref[...])
    u = uv[:, :W_B].astype(F32)
    v = (_rms(uv[:, W_B:].astype(F32)) * gv_ref[...]).astype(MXU_DT)
    gw = W_B // G_B
    for c in range(uv.shape[0] // CHUNK_B):
        rows = slice(c * CHUNK_B, (c + 1) * CHUNK_B)
        for g in range(G_B):
            cols = slice(g * gw, (g + 1) * gw)
            mixed = jnp.dot(ws_ref[g], v[rows, cols], preferred_element_type=F32) + bs_ref[:, cols]
            o_ref[rows, cols] = (u[rows, cols] * mixed).astype(o_ref.dtype)


def _gmlp(p, g_v, w_s, b_full, tiles):
    tm = ROW_TILE
    r = tiles.count * tm
    return pl.pallas_call(
        _gmlp_kernel,
        grid=(tiles.count,),
        in_specs=[
            pl.BlockSpec((tm, 2 * W_B), lambda i: (tiles.full(i), P_UV // (2 * W_B))),
            pl.BlockSpec((1, W_B), lambda i: (0, 0)),
            pl.BlockSpec((G_B, CHUNK_B, CHUNK_B), lambda i: (0, 0, 0)),
            pl.BlockSpec((CHUNK_B, W_B), lambda i: (0, 0)),
        ],
        out_specs=pl.BlockSpec((tm, W_B), lambda i: (i, 0)),
        out_shape=jax.ShapeDtypeStruct((r, W_B), ACT_DT),
        compiler_params=_cparams("parallel"),
        name="gmlp",
    )(p, g_v, w_s, b_full)


def _mfeat_kernel(x_ref, prev_ref, next_ref, wc_ref, bc_ref, wqkv_ref, wif_ref, bif_ref,
                  xc_ref, q_ref, k_ref, v_ref, g_ref, *, tps):
    tm = x_ref.shape[0]
    j = pl.program_id(0) % tps
    seg_start = jnp.logical_or(j == 0, j == 1)
    seg_end = jnp.logical_or(j == 0, j == tps - 1)
    x = x_ref[...].astype(F32)
    prow = jnp.where(seg_start, 0.0, prev_ref[...].astype(F32)[HALO - 1:HALO, :])
    nrow = jnp.where(seg_end, 0.0, next_ref[...].astype(F32)[0:1, :])
    rid = lax.broadcasted_iota(jnp.int32, x.shape, 0)
    xp = jnp.where(rid == 0, prow, pltpu.roll(x, 1, 0))
    xn = jnp.where(rid == tm - 1, nrow, pltpu.roll(x, tm - 1, 0))
    xc = wc_ref[0:1, :] * xp + wc_ref[1:2, :] * x + wc_ref[2:3, :] * xn + bc_ref[...]
    xc = xc * _sigmoid(xc)
    xc_ref[...] = xc.astype(xc_ref.dtype)
    xcb = xc.astype(MXU_DT)
    xb = x_ref[...]
    qs, ks, vs = [], [], []
    for h in range(H_C):
        cols = slice(h * DH_C, (h + 1) * DH_C)
        qs.append(jnp.dot(xcb[:, cols], wqkv_ref[0, h], preferred_element_type=F32))
        ks.append(jnp.dot(xcb[:, cols], wqkv_ref[1, h], preferred_element_type=F32))
        vs.append(jnp.dot(xb[:, cols], wqkv_ref[2, h], preferred_element_type=F32))
    q = jnp.concatenate(qs, axis=1)
    k = jnp.concatenate(ks, axis=1)
    v = jnp.concatenate(vs, axis=1)
    qkv = jnp.concatenate([q, k, v], axis=1).astype(MXU_DT)
    gates = jnp.dot(qkv, wif_ref[...], preferred_element_type=F32) + bif_ref[...]
    g_ref[...] = gates.T[0:g_ref.shape[0], :]
    q_ref[...] = q.T.astype(q_ref.dtype)
    k_ref[...] = (k * (DH_C ** -0.5)).astype(k_ref.dtype)
    v_ref[...] = v.T.astype(v_ref.dtype)


def _mlstm_features(p, w_conv, b_conv, w_qkv, w_if_pad, b_if_pad, tps):
    r = p.shape[0]
    tm = ROW_TILE
    hpt = tm // HALO
    xm_blk = P_XM // W_C
    last_halo = r // HALO - 1
    kern = functools.partial(_mfeat_kernel, tps=tps)
    act = jax.ShapeDtypeStruct((r, W_C), ACT_DT)
    act_t = jax.ShapeDtypeStruct((W_C, r), ACT_DT)
    row_spec = pl.BlockSpec((tm, W_C), lambda i: (i, 0))
    col_spec = pl.BlockSpec((W_C, tm), lambda i: (0, i))
    return pl.pallas_call(
        kern,
        grid=(r // tm,),
        in_specs=[
            pl.BlockSpec((tm, W_C), lambda i: (i, xm_blk)),
            pl.BlockSpec((HALO, W_C), lambda i: (jnp.maximum(i * hpt - 1, 0), xm_blk)),
            pl.BlockSpec((HALO, W_C), lambda i: (jnp.minimum((i + 1) * hpt, last_halo), xm_blk)),
            pl.BlockSpec((CONV_K, W_C), lambda i: (0, 0)),
            pl.BlockSpec((1, W_C), lambda i: (0, 0)),
            pl.BlockSpec((3, H_C, DH_C, DH_C), lambda i: (0, 0, 0, 0)),
            pl.BlockSpec((3 * W_C, LANE), lambda i: (0, 0)),
            pl.BlockSpec((1, LANE), lambda i: (0, 0)),
        ],
        out_specs=[row_spec, col_spec, row_spec, col_spec,
                   pl.BlockSpec((2 * GATE_ROWS, tm), lambda i: (0, i))],
        out_shape=[act, act_t, act, act_t, jax.ShapeDtypeStruct((2 * GATE_ROWS, r), F32)],
        compiler_params=_cparams("parallel"),
        name="mlstm_features",
    )(p, p, p, w_conv, b_conv, w_qkv, w_if_pad, b_if_pad)


GATE_ROWS = 2 * H_C
GATE_PACK = ("r", "cmx", "bcum")


def _gate_prep_kernel(g_ref, pack_ref, rt_ref):
    seg = MLSTM_CHUNK
    gt_fwd = g_ref[0:GATE_ROWS, :]
    gt_bwd = g_ref[GATE_ROWS:2 * GATE_ROWS, :]
    n = gt_fwd.shape[1]
    lane = lax.broadcasted_iota(jnp.int32, gt_fwd.shape, 1)
    in_seg = lane & (seg - 1)

    def scan_both(xf, xb, op):
        sh = 1
        while sh < seg:
            xf = jnp.where(in_seg >= sh, op(xf, pltpu.roll(xf, sh, 1)), xf)
            xb = jnp.where(in_seg < seg - sh, op(xb, pltpu.roll(xb, n - sh, 1)), xb)
            sh *= 2
        return xf, xb

    bcum_f, bcum_b = scan_both(jax.nn.log_sigmoid(pltpu.roll(gt_fwd, H_C, 0)),
                               jax.nn.log_sigmoid(pltpu.roll(gt_bwd, H_C, 0)), jnp.add)
    r_f, r_b = gt_fwd - bcum_f, gt_bwd - bcum_b
    cmx_f, cmx_b = scan_both(r_f, r_b, jnp.maximum)
    pack_ref[...] = jnp.concatenate([r_f, cmx_f, bcum_f, r_b, cmx_b, bcum_b], axis=0)
    rt_ref[...] = jnp.concatenate([r_f, r_b, jnp.zeros((LANE - 2 * GATE_ROWS, n), F32)], axis=0).T


def _mlstm_gate_prep(gates_t, n_batch, seq):
    rows = 2 * len(GATE_PACK) * GATE_ROWS
    r = gates_t.shape[1]
    return pl.pallas_call(
        _gate_prep_kernel,
        grid=(n_batch,),
        in_specs=[pl.BlockSpec((2 * GATE_ROWS, seq), lambda b: (0, b))],
        out_specs=[pl.BlockSpec((rows, seq), lambda b: (0, b)), pl.BlockSpec((seq, LANE), lambda b: (b, 0))],
        out_shape=[jax.ShapeDtypeStruct((rows, r), F32), jax.ShapeDtypeStruct((r, LANE), F32)],
        compiler_params=_cparams("parallel"),
        name="mlstm_gate_prep",
    )(gates_t)


def _mlstm_gates(pack, m_ref, reverse):
    ln = pack.shape[1]
    part = {name: pack[i * GATE_ROWS:(i + 1) * GATE_ROWS] for i, name in enumerate(GATE_PACK)}
    end = slice(0, 1) if reverse else slice(ln - 1, ln)
    g = part["bcum"][:, end]
    r_max = part["cmx"][:, end]
    m_loc = g + r_max
    w_end = jnp.exp(part["r"] - r_max)
    m_old = m_ref[:, 0:1]
    m_new = jnp.maximum(g + m_old, m_loc)
    a_old = jnp.exp(g + m_old - m_new)
    a_new = jnp.exp(m_loc - m_new)
    mx = jnp.maximum(m_old, part["cmx"])
    w_inter = jnp.exp(m_old - mx)
    e_neg = jnp.exp(-(part["bcum"] + mx))
    s_id = lax.broadcasted_iota(jnp.int32, (ln, ln), 0)
    t_id = lax.broadcasted_iota(jnp.int32, (ln, ln), 1)
    keep = (s_id >= t_id) if reverse else (s_id <= t_id)
    return dict(mx=mx, w_inter=w_inter, e_neg=e_neg, w_end=w_end, a_old=a_old, a_new=a_new,
                m_new=m_new, keep=keep)


def _mscan_chunk(gf_ref, rf_ref, qf_ref, kf_ref, vf_ref, gb_ref, rb_ref, qb_ref, kb_ref, vb_ref,
                 hf_ref, hb_ref, cf_ref, mf_ref, cb_ref, mb_ref):
    ln = qf_ref.shape[1]
    pr = len(GATE_PACK) * GATE_ROWS
    dirs = ((_mlstm_gates(gf_ref[0:pr, :], mf_ref, False), qf_ref, kf_ref, vf_ref, cf_ref, hf_ref, rf_ref),
            (_mlstm_gates(gb_ref[pr:2 * pr, :], mb_ref, True), qb_ref, kb_ref, vb_ref, cb_ref, hb_ref, rb_ref))
    pairs = [(d, h) for d in range(2) for h in range(H_C)]
    ones_row = jnp.where(lax.broadcasted_iota(jnp.int32, (DH_C, ln), 0) == 0, 1.0, 0.0).astype(MXU_DT)
    head = lambda h: slice(h * DH_C, (h + 1) * DH_C)
    row = lambda h: slice(h, h + 1)

    ks = [dirs[d][2][:, head(h)] for d, h in pairs]
    qts = [dirs[d][1][head(h), :] for d, h in pairs]
    vexts = [jnp.concatenate([dirs[d][3][head(h), :], ones_row], axis=0) for d, h in pairs]
    cexts = [dirs[d][4][h] for d, h in pairs]
    qks = [jnp.dot(k, q_t, preferred_element_type=F32) for k, q_t in zip(ks, qts)]
    rhss = []
    for i, (d, h) in enumerate(pairs):
        g = dirs[d][0]
        r_col = dirs[d][6][:, d * GATE_ROWS + h:d * GATE_ROWS + h + 1]
        d_t = jnp.exp(jnp.where(g["keep"], r_col - g["mx"][row(h), :], NEG_BIG))
        wq = (qts[i].astype(F32) * g["w_inter"][row(h), :]).astype(MXU_DT)
        rhss.append(jnp.concatenate([(qks[i] * d_t).astype(MXU_DT), wq], axis=0))
    ress = [jnp.dot(jnp.concatenate([vexts[i], cexts[i].astype(MXU_DT)], axis=1), rhss[i],
                    preferred_element_type=F32) for i in range(len(pairs))]
    for i, (d, h) in enumerate(pairs):
        g = dirs[d][0]
        den = ress[i][DH_C:DH_C + 1, :]
        hh = ress[i][:DH_C, :] / jnp.maximum(jnp.abs(den), g["e_neg"][row(h), :])
        dirs[d][5][head(h), :] = hh.astype(dirs[d][5].dtype)
    clocs = []
    for i, (d, h) in enumerate(pairs):
        vw_t = (vexts[i].astype(F32) * dirs[d][0]["w_end"][row(h), :]).astype(MXU_DT)
        clocs.append(jnp.dot(vw_t, ks[i], preferred_element_type=F32))
    for i, (d, h) in enumerate(pairs):
        g = dirs[d][0]
        dirs[d][4][h] = g["a_old"][row(h), :] * cexts[i] + g["a_new"][row(h), :] * clocs[i]
    mf_ref[...] = jnp.broadcast_to(dirs[0][0]["m_new"], mf_ref.shape)
    mb_ref[...] = jnp.broadcast_to(dirs[1][0]["m_new"], mb_ref.shape)


def _mscan_kernel(g_ref, r_ref, q_ref, k_ref, v_ref, hf_ref, hb_ref, cf_ref, mf_ref, cb_ref, mb_ref, *, ncc):
    ln = MLSTM_CHUNK
    nck = q_ref.shape[1] // ln
    cf_ref[...] = jnp.zeros_like(cf_ref)
    cb_ref[...] = jnp.zeros_like(cb_ref)
    mf_ref[...] = jnp.zeros_like(mf_ref)
    mb_ref[...] = jnp.zeros_like(mb_ref)

    def chunk_views(c):
        at = pl.ds(pl.multiple_of(c * ln, ln), ln)
        return ((g_ref.at[:, at], r_ref.at[at, :], q_ref.at[:, at], k_ref.at[at, :], v_ref.at[:, at]),
                (hf_ref.at[:, at], hb_ref.at[:, at]))

    def trip(s, carry):
        ins_f, (hf, _) = chunk_views(s)
        ins_b, (_, hb) = chunk_views(jnp.where(s < ncc, ncc - 1 - s, nck - 1 + ncc - s))
        _mscan_chunk(*ins_f, *ins_b, hf, hb, cf_ref, mf_ref, cb_ref, mb_ref)
        return carry

    lax.fori_loop(0, nck, trip, 0)


def _mlstm_scan(q_t, k, v_t, gate_pack, r_t, n_batch, seq, ctx_rows):
    rows = lambda w: pl.BlockSpec((seq, w), lambda b: (b, 0))
    cols = lambda h: pl.BlockSpec((h, seq), lambda b: (0, b))
    out = jax.ShapeDtypeStruct(q_t.shape, ACT_DT)
    return pl.pallas_call(
        functools.partial(_mscan_kernel, ncc=ctx_rows // MLSTM_CHUNK),
        grid=(n_batch,),
        in_specs=[cols(gate_pack.shape[0]), rows(LANE), cols(W_C), rows(W_C), cols(W_C)],
        out_specs=[cols(W_C), cols(W_C)],
        out_shape=[out, out],
        scratch_shapes=[
            pltpu.VMEM((H_C, 2 * DH_C, DH_C), F32), pltpu.VMEM((8, LANE), F32),
            pltpu.VMEM((H_C, 2 * DH_C, DH_C), F32), pltpu.VMEM((8, LANE), F32),
        ],
        compiler_params=_cparams("parallel"),
        name="mlstm_scan",
    )(gate_pack, r_t, q_t, k, v_t)


def _route(logits, carry):
    lane = lax.broadcasted_iota(jnp.int32, logits.shape, 1).astype(F32)
    big = float(4 * LANE)
    gl = jnp.where(lane < N_GROUPS, logits, NEG_BIG)
    gmax = jnp.max(gl, axis=-1, keepdims=True)
    g_star = jnp.min(jnp.where(gl == gmax, lane, big), axis=-1, keepdims=True)
    p_g = 1.0 / jnp.sum(jnp.exp(gl - gmax), axis=-1, keepdims=True)
    e_lo = g_star * EXP_PER_GROUP + N_GROUPS
    in_group = jnp.logical_and(lane >= e_lo, lane < e_lo + EXP_PER_GROUP)
    el = jnp.where(in_group, logits, NEG_BIG)
    v1 = jnp.max(el, axis=-1, keepdims=True)
    i1 = jnp.min(jnp.where(el == v1, lane, big), axis=-1, keepdims=True)
    el2 = jnp.where(lane == i1, NEG_BIG, el)
    v2 = jnp.max(el2, axis=-1, keepdims=True)
    i2 = jnp.min(jnp.where(el2 == v2, lane, big), axis=-1, keepdims=True)
    e21 = jnp.exp(v2 - v1)
    w1 = p_g / (1.0 + e21)
    w2 = p_g * e21 / (1.0 + e21)
    e1 = i1 - N_GROUPS
    e2 = i2 - N_GROUPS
    oh1 = lane == e1
    oh2 = lane == e2
    both = jnp.where(jnp.logical_or(oh1, oh2), 1.0, 0.0)
    tm = logits.shape[0]
    lower = (lax.broadcasted_iota(jnp.int32, (tm, tm), 1) < lax.broadcasted_iota(jnp.int32, (tm, tm), 0))
    before = jnp.dot(jnp.where(lower, 1.0, 0.0).astype(MXU_DT), both.astype(MXU_DT),
                     preferred_element_type=F32) + carry
    rank1 = jnp.sum(jnp.where(oh1, before, 0.0), axis=-1, keepdims=True)
    rank2 = jnp.sum(jnp.where(oh2, before, 0.0), axis=-1, keepdims=True)
    out = jnp.where(lane == 0, e1, 0.0)
    out = jnp.where(lane == 1, e2, out)
    out = jnp.where(lane == 2, w1, out)
    out = jnp.where(lane == 3, w2, out)
    out = jnp.where(lane == 4, rank1, out)
    out = jnp.where(lane == 5, rank2, out)
    return out, carry + jnp.sum(both, axis=0, keepdims=True)


def _merge_kernel(x_ref, gate_ref, z_ref, ya_ref, yb_ref, hf_ref, hb_ref, xc_ref, gm_ref, sk_ref,
                  wbr_ref, wo_ref, mod_ref, g2_ref, wr_ref, br_ref, xo_ref, rt_ref, cnt_ref, carry_ref):
    @pl.when(pl.program_id(0) == 0)
    def _():
        carry_ref[...] = jnp.zeros_like(carry_ref)

    hs_t = hf_ref[...].astype(F32) + hb_ref[...].astype(F32)
    hn_t = []
    for h in range(H_C):
        part = hs_t[h * DH_C:(h + 1) * DH_C, :]
        hn_t.append(part * lax.rsqrt(jnp.mean(part * part, axis=0, keepdims=True) + EPS))
    hn = jnp.concatenate(hn_t, axis=0).T * gm_ref[...]
    yc = (hn + sk_ref[...] * xc_ref[...].astype(F32)) * _sigmoid(z_ref[...].astype(F32))
    ys = (ya_ref[...], yb_ref[...], yc.astype(MXU_DT))
    merged = None
    for i in range(3):
        gate = _sigmoid(gate_ref[:, i * D_MODEL:(i + 1) * D_MODEL].astype(MXU_DT))
        term = gate * jnp.dot(ys[i], wbr_ref[i], preferred_element_type=F32).astype(MXU_DT)
        merged = term if merged is None else merged + term
    y = jnp.dot(merged, wo_ref[...], preferred_element_type=F32)
    xn = x_ref[...] + mod_ref[2:3, :] * y
    xo_ref[...] = xn
    h2 = _rms(xn) * g2_ref[...] * (1.0 + mod_ref[4:5, :]) + mod_ref[3:4, :]
    h_hi, h_lo = _split_bf16(h2)
    logits = (jnp.dot(h_hi, wr_ref[0], preferred_element_type=F32)
              + jnp.dot(h_lo, wr_ref[0], preferred_element_type=F32)
              + jnp.dot(h_hi, wr_ref[1], preferred_element_type=F32)) + br_ref[...]
    route, carry = _route(logits, carry_ref[0:1, :])
    rt_ref[...] = route
    carry_ref[0:1, :] = carry
    cnt_ref[...] = jnp.broadcast_to(carry, cnt_ref.shape)


def _merge(xall, p, ya, yb, hf, hb, xconv, g_m, skip, w_br, w_o, mod, g2, w_route, b_route, tiles):
    d = xall.shape[1]
    tm = ROW_TILE
    r = tiles.count * tm

    def mod_idx(i):
        return (tiles.mod_row(i), 0, 0)

    row = lambda w: pl.BlockSpec((tm, w), lambda i: (i, 0))
    frow = lambda w: pl.BlockSpec((tm, w), lambda i: (tiles.full(i), 0))
    fcol = lambda w: pl.BlockSpec((w, tm), lambda i: (0, tiles.full(i)))
    full = lambda shape: pl.BlockSpec(shape, lambda i: (0,) * len(shape))
    return pl.pallas_call(
        _merge_kernel,
        grid=(tiles.count,),
        in_specs=[
            frow(d),
            pl.BlockSpec((tm, GATE_W), lambda i: (tiles.full(i), P_GATE // GATE_W)),
            pl.BlockSpec((tm, W_C), lambda i: (tiles.full(i), P_Z // W_C)),
            row(W_BR), row(W_BR), fcol(W_C), fcol(W_C), frow(W_C),
            full((1, W_C)), full((1, W_C)),
            full((3, W_BR, d)), full((d, d)),
            pl.BlockSpec((None, N_MOD, d), mod_idx),
            full((1, d)), full((2, d, LANE)), full((1, LANE)),
        ],
        out_specs=[row(d), row(LANE), full((8, LANE))],
        out_shape=[jax.ShapeDtypeStruct((r, d), F32), jax.ShapeDtypeStruct((r, LANE), F32),
                   jax.ShapeDtypeStruct((8, LANE), F32)],
        scratch_shapes=[pltpu.VMEM((8, LANE), F32)],
        compiler_params=_cparams("arbitrary"),
        name="merge_route",
    )(xall, p, p, ya, yb, hf, hb, xconv, g_m, skip, w_br, w_o, mod, g2, w_route, b_route)


def _row_copy(src_hbm, src_row, dst_ref, dst_row, sem):
    return pltpu.make_async_copy(src_hbm.at[pl.ds(src_row, 1), :], dst_ref.at[pl.ds(dst_row, 1), :], sem)


def _dispatch_kernel(fill_ref, nlive_ref, pos_ref, x_ref, g2_ref, mod_ref, xs_hbm, hbuf, zbuf, zsem, sem):
    i = pl.program_id(0)
    n = pl.num_programs(0)
    tm = x_ref.shape[0]
    slot = i % 2
    bm = zbuf.shape[0]
    n_blocks = xs_hbm.shape[0] // bm

    @pl.when(i == 0)
    def _():
        zbuf[...] = jnp.zeros_like(zbuf)

        def fill_block(b):
            return pltpu.make_async_copy(zbuf, xs_hbm.at[pl.ds(b * bm, bm), :], zsem)

        def fill_last(k, carry):
            fill_block(fill_ref[k]).start()
            return carry

        def fill_unused(b, carry):
            fill_block(b).start()
            return carry

        def wait_fill(k, carry):
            fill_block(0).wait()
            return carry

        n_live = nlive_ref[0]
        n_used = nlive_ref[1]
        lax.fori_loop(0, n_used, fill_last, 0)
        lax.fori_loop(n_live, n_blocks, fill_unused, 0)
        lax.fori_loop(0, n_used + n_blocks - n_live, wait_fill, 0)

    hbuf[slot] = _rms(x_ref[...]) * g2_ref[...] * (1.0 + mod_ref[4:5, :]) + mod_ref[3:4, :]

    def body(c, carry):
        for u in range(DMA_UNROLL):
            r = c * DMA_UNROLL + u
            for j in range(TOP_K):
                pltpu.make_async_copy(hbuf.at[slot, pl.ds(r, 1), :],
                                      xs_hbm.at[pl.ds(pos_ref[0, j * tm + r], 1), :], sem.at[slot]).start()
        return carry

    lax.fori_loop(0, tm // DMA_UNROLL, body, 0)

    def wait_tile(s):
        pltpu.make_async_copy(hbuf.at[s], xs_hbm.at[pl.ds(0, tm), :], sem.at[s]).wait()
        pltpu.make_async_copy(hbuf.at[s], xs_hbm.at[pl.ds(0, tm), :], sem.at[s]).wait()

    @pl.when(i > 0)
    def _():
        wait_tile(1 - slot)

    @pl.when(i == n - 1)
    def _():
        wait_tile(slot)


def _moe_dispatch(xall, g2, mod, fill_blocks, n_live, pos_tiles, n_slots, tiles):
    r, d = xall.shape
    tm = ROW_TILE
    assert TOP_K == 2 and r == tiles.count * tm

    def mod_idx(i, fb, nl):
        return (tiles.mod_row(i), 0, 0)

    grid_spec = pltpu.PrefetchScalarGridSpec(
        num_scalar_prefetch=2,
        grid=(r // tm,),
        in_specs=[
            pl.BlockSpec((None, 1, TOP_K * tm), lambda i, fb, nl: (i, 0, 0), memory_space=pltpu.SMEM),
            pl.BlockSpec((tm, d), lambda i, fb, nl: (i, 0)),
            pl.BlockSpec((1, d), lambda i, fb, nl: (0, 0)),
            pl.BlockSpec((None, N_MOD, d), mod_idx),
        ],
        out_specs=pl.BlockSpec(memory_space=pl.ANY),
        scratch_shapes=[pltpu.VMEM((2, tm, d), F32), pltpu.VMEM((MOE_BLOCK, d), F32),
                        pltpu.SemaphoreType.DMA(()), pltpu.SemaphoreType.DMA((2,))],
    )
    return pl.pallas_call(
        _dispatch_kernel,
        grid_spec=grid_spec,
        out_shape=jax.ShapeDtypeStruct((n_slots, d), F32),
        compiler_params=_cparams("arbitrary"),
        name="moe_dispatch",
    )(fill_blocks, n_live, pos_tiles, xall, g2, mod)


def _moe_kernel(be_ref, nv_ref, x_ref, wg_ref, wu_ref, wd_ref, y_ref, wg_mx, wu_mx, wd_mx):
    i = pl.program_id(0)
    n_valid = nv_ref[i]

    @pl.when(jnp.logical_or(i == 0, be_ref[i] != be_ref[jnp.maximum(i - 1, 0)]))
    def _():
        wg_mx[...] = wg_ref[...].astype(wg_mx.dtype)
        wu_mx[...] = wu_ref[...].astype(wu_mx.dtype)
        wd_mx[...] = wd_ref[...].astype(wd_mx.dtype)

    @pl.when(n_valid > 0)
    def _():
        x = x_ref[...].astype(MXU_DT)
        a = jnp.dot(x, wg_mx[...], preferred_element_type=F32)
        u = jnp.dot(x, wu_mx[...], preferred_element_type=F32)
        act = (a * _sigmoid(a) * u).astype(MXU_DT)
        y_ref[...] = jnp.dot(act, wd_mx[...], preferred_element_type=F32)

    @pl.when(n_valid <= 0)
    def _():
        y_ref[...] = jnp.zeros_like(y_ref)


def _moe_experts(xs, block_e, n_valid, w_gate, w_up, w_down, layer):
    n_slots, d = xs.shape
    bm = MOE_BLOCK
    grid_spec = pltpu.PrefetchScalarGridSpec(
        num_scalar_prefetch=2,
        grid=(n_slots // bm,),
        in_specs=[
            pl.BlockSpec((bm, d), lambda i, be, nv: (i, 0)),
            pl.BlockSpec((None, None, d, D_EXPERT), lambda i, be, nv: (layer, be[i], 0, 0)),
            pl.BlockSpec((None, None, d, D_EXPERT), lambda i, be, nv: (layer, be[i], 0, 0)),
            pl.BlockSpec((None, None, D_EXPERT, d), lambda i, be, nv: (layer, be[i], 0, 0)),
        ],
        out_specs=pl.BlockSpec((bm, d), lambda i, be, nv: (i, 0)),
        scratch_shapes=[pltpu.VMEM((d, D_EXPERT), MXU_DT), pltpu.VMEM((d, D_EXPERT), MXU_DT),
                        pltpu.VMEM((D_EXPERT, d), MXU_DT)],
    )
    return pl.pallas_call(
        _moe_kernel,
        grid_spec=grid_spec,
        out_shape=jax.ShapeDtypeStruct((n_slots, d), F32),
        compiler_params=_cparams("arbitrary"),
        name="moe_experts",
    )(block_e, n_valid, xs, w_gate, w_up, w_down)


def _combine_kernel(pos_ref, pos_next_ref, x_ref, rt_ref, mod_ref, gf_ref, yb_hbm, xo_ref, ybuf, sem, *, final):
    i = pl.program_id(0)
    n = pl.num_programs(0)
    tm = x_ref.shape[0]

    def start_gather(ids_ref, slot):
        def body(c, carry):
            for u in range(DMA_UNROLL):
                r = c * DMA_UNROLL + u
                _row_copy(yb_hbm, ids_ref[0, r], ybuf.at[slot], r, sem.at[slot]).start()
            return carry
        lax.fori_loop(0, TOP_K * tm // DMA_UNROLL, body, 0)

    @pl.when(i == 0)
    def _():
        start_gather(pos_ref, 0)

    @pl.when(i + 1 < n)
    def _():
        start_gather(pos_next_ref, (i + 1) % 2)

    slot = i % 2
    pltpu.make_async_copy(yb_hbm.at[pl.ds(0, TOP_K * tm), :], ybuf.at[slot], sem.at[slot]).wait()
    rt = rt_ref[...]
    f = rt[:, 2:3] * ybuf[slot, 0:tm, :] + rt[:, 3:4] * ybuf[slot, tm:2 * tm, :]
    xn = x_ref[...] + mod_ref[5:6, :] * f
    xo_ref[...] = _rms(xn) * gf_ref[...] if final else xn


def _combine(xall, route, mod, g_final, yb, pos, tiles, final):
    r, d = xall.shape
    tm = ROW_TILE
    nt = tiles.count
    last = nt - 1
    assert r == nt * tm

    def mod_idx(i):
        return (tiles.mod_row(i), 0, 0)

    row = lambda w: pl.BlockSpec((tm, w), lambda i: (i, 0))
    return pl.pallas_call(
        functools.partial(_combine_kernel, final=final),
        grid=(nt,),
        in_specs=[
            pl.BlockSpec((None, 1, TOP_K * tm), lambda i: (i, 0, 0), memory_space=pltpu.SMEM),
            pl.BlockSpec((None, 1, TOP_K * tm), lambda i: (jnp.minimum(i + 1, last), 0, 0),
                         memory_space=pltpu.SMEM),
            row(d), row(LANE),
            pl.BlockSpec((None, N_MOD, d), mod_idx),
            pl.BlockSpec((1, d), lambda i: (0, 0)),
            pl.BlockSpec(memory_space=pl.ANY),
        ],
        out_specs=row(d),
        out_shape=jax.ShapeDtypeStruct((r, d), F32),
        scratch_shapes=[pltpu.VMEM((2, TOP_K * tm, d), F32), pltpu.SemaphoreType.DMA((2,))],
        compiler_params=_cparams("arbitrary"),
        name="moe_combine",
    )(pos, pos, xall, route, mod, g_final, yb)


def _dispatch_plan(route, counts_f, n_tiles):
    n = route.shape[0]
    bm = MOE_BLOCK
    n_blocks = -(-(n * TOP_K + N_EXPERTS * (bm - 1)) // bm)
    counts = counts_f[:N_EXPERTS].astype(jnp.int32)
    padded = (counts + bm - 1) // bm * bm
    pend = jnp.cumsum(padded)
    seg_off = pend - padded
    eid = route[:, 0:TOP_K].astype(jnp.int32)
    rank = route[:, 4:4 + TOP_K].astype(jnp.int32)
    experts = jnp.arange(N_EXPERTS, dtype=jnp.int32)
    pos = jnp.sum(jnp.where(eid[..., None] == experts, seg_off, 0), axis=-1) + rank
    blk0 = jnp.arange(n_blocks, dtype=jnp.int32) * bm
    block_e = jnp.minimum(jnp.sum((blk0[:, None] >= pend[None, :]).astype(jnp.int32), axis=1), N_EXPERTS - 1)
    seg_end = seg_off + counts
    n_valid = jnp.clip(seg_end[block_e] - blk0, 0, bm).astype(jnp.int32)
    pos_tiles = pos.reshape(n_tiles, n // n_tiles, TOP_K).transpose(0, 2, 1).reshape(n_tiles, 1, -1)
    used = counts > 0
    last_blocks = jnp.sort(jnp.where(used, pend // bm - 1, n_blocks)).astype(jnp.int32)
    n_live = jnp.stack([pend[-1] // bm, jnp.sum(used)]).astype(jnp.int32)
    return block_e.astype(jnp.int32), n_valid, last_blocks, n_live, pos_tiles, n_blocks * bm


def _rope_tables(seq, ctx_rows):
    t = jnp.arange(seq - ctx_rows)
    inv = ROPE_BASE ** (-jnp.arange(0, ROT_AX, 2, dtype=F32) / ROT_AX)
    ang_r = (t // GRID_W).astype(F32)[:, None] * inv
    ang_c = (t % GRID_W).astype(F32)[:, None] * inv
    cos64 = jnp.concatenate([jnp.cos(ang_r)] * 2 + [jnp.cos(ang_c)] * 2, axis=1)
    sin64 = jnp.concatenate([-jnp.sin(ang_r), jnp.sin(ang_r), -jnp.sin(ang_c), jnp.sin(ang_c)], axis=1)
    cos = jnp.concatenate([jnp.ones((ctx_rows, DK), F32), cos64], axis=0)
    sin = jnp.concatenate([jnp.zeros((ctx_rows, DK), F32), sin64], axis=0)
    return jnp.tile(cos, (1, LANE // DK)), jnp.tile(sin, (1, LANE // DK))


def kernel(x, c, ctx, c_ctx, w_mod, b_mod, g_norm1, g_norm2, w_in, diff_lambda, g_diff_subln, w_gmlp_s, b_gmlp_s, g_gmlp_v, w_conv_m, b_conv_m, w_qkv_m, w_if_m, b_if_m, g_mlstm_norm, skip_m, w_branch, w_out, w_route_g, b_route_g, w_route_e, b_route_e, w_e_gate, w_e_up, w_e_down, g_final):
    n_batch, t_lat, d = x.shape
    ctx_rows = ctx.shape[1]
    depth = w_in.shape[0]
    seq = ctx_rows + t_lat
    assert d == D_MODEL and ctx_rows == ROW_TILE and t_lat % ROW_TILE == 0 and t_lat % GRID_W == 0
    tps = seq // ROW_TILE
    all_tiles = _Tiles(n_batch, tps, False)

    xall = jnp.concatenate([ctx, x], axis=1).reshape(n_batch * seq, d)
    mb = -(-(n_batch + 1) // 8) * 8
    c_all = jnp.zeros((mb, d), F32).at[:n_batch].set(c).at[n_batch].set(c_ctx)
    mod_all = _modulation(c_all, w_mod, b_mod)[:, :n_batch + 1].reshape(depth, n_batch + 1, N_MOD, d)
    cos, sin_signed = _rope_tables(seq, ctx_rows)
    w_in_b = w_in.astype(MXU_DT)

    pending = None
    for l in range(depth):
        lam_init = 0.8 - 0.6 * math.exp(-0.3 * l)
        mod = mod_all[l]
        last = l == depth - 1
        tiles = _Tiles(n_batch, tps, last)
        if pending is None:
            p, k_t = _inproj(xall, g_norm1[l][None], mod, cos, sin_signed, w_in_b, l, all_tiles)
        else:
            p, k_t, xall = _combine_inproj(xall, *pending, g_norm1[l][None], mod, cos, sin_signed, w_in_b, l,
                                           all_tiles)

        ya = _attention(p, k_t, diff_lambda[l], g_diff_subln[l][None], lam_init, tiles, ctx_rows)

        b_full = jnp.repeat(b_gmlp_s[l].T, W_B // G_B, axis=1)
        yb = _gmlp(p, g_gmlp_v[l][None], w_gmlp_s[l].astype(MXU_DT), b_full, tiles)

        w_if_pad = jnp.zeros((3 * W_C, LANE), F32).at[:, :4 * H_C].set(w_if_m[l]).astype(MXU_DT)
        b_if_pad = jnp.zeros((1, LANE), F32).at[0, :4 * H_C].set(b_if_m[l])
        xconv, q_m, k_m, v_m, gates_t = _mlstm_features(
            p, w_conv_m[l], b_conv_m[l][None], w_qkv_m[l].astype(MXU_DT), w_if_pad, b_if_pad, tps)
        gate_pack, r_t = _mlstm_gate_prep(gates_t, n_batch, seq)
        hf, hb = _mlstm_scan(q_m, k_m, v_m, gate_pack, r_t, n_batch, seq, ctx_rows)

        w_route = (jnp.zeros((d, LANE), F32).at[:, :N_GROUPS].set(w_route_g[l])
                   .at[:, N_GROUPS:N_GROUPS + N_EXPERTS].set(w_route_e[l]))
        w_route = jnp.stack(_split_bf16(w_route))
        b_route = (jnp.zeros((1, LANE), F32).at[0, :N_GROUPS].set(b_route_g[l])
                   .at[0, N_GROUPS:N_GROUPS + N_EXPERTS].set(b_route_e[l]))
        xall, route, counts = _merge(xall, p, ya, yb, hf, hb, xconv, g_mlstm_norm[l][None], skip_m[l][None],
                                         w_branch[l].astype(MXU_DT), w_out[l].astype(MXU_DT), mod,
                                         g_norm2[l][None], w_route, b_route, tiles)

        block_e, n_valid, fill_blocks, n_live, pos_tiles, n_slots = _dispatch_plan(route, counts[0], tiles.count)
        xs = _moe_dispatch(xall, g_norm2[l][None], mod, fill_blocks, n_live, pos_tiles, n_slots, tiles)
        y_sorted = _moe_experts(xs, block_e, n_valid, w_e_gate, w_e_up, w_e_down, l)
        if last:
            xall = _combine(xall, route, mod, g_final[None], y_sorted, pos_tiles, tiles, True)
        else:
            pending = (route, mod, y_sorted, pos_tiles)

    return xall.reshape(n_batch, t_lat, d)
```

```python
import functools
import math
from typing import NamedTuple

import jax
import jax.numpy as jnp
from jax import lax
from jax.experimental import pallas as pl
from jax.experimental.pallas import tpu as pltpu

F32 = jnp.float32
MXU_DT = jnp.bfloat16
ACT_DT = jnp.bfloat16
HIGHEST = lax.Precision.HIGHEST

D_MODEL = 1024
N_MOD = 6
EPS = 1e-6
GRID_W = 64
H_A, DK = 4, 64
DV = 2 * DK
ROT_AX = DK // 2
ROPE_BASE = 10000.0
W_B, G_B, CHUNK_B = 512, 4, 128
H_C, DH_C, CONV_K, MLSTM_CHUNK = 4, 128, 3, 128
W_C = H_C * DH_C
N_GROUPS, EXP_PER_GROUP, TOP_K, D_EXPERT = 4, 8, 2, 512
N_EXPERTS = N_GROUPS * EXP_PER_GROUP
W_BR = 512

GATE_W = 3 * D_MODEL
P_GATE = 0
P_K = GATE_W
P_V = P_K + 512
P_XM = P_V + 512
P_Q = P_XM + 512
P_UV = P_Q + 512
P_Z = P_UV + 2 * W_B
IN_COLS = P_Z + W_C
ORIG_GATE0 = IN_COLS - GATE_W

LANE = 128
ROW_TILE = 256
PROJ_CHUNK = 512
MOE_BLOCK = 512
HALO = 16
DMA_UNROLL = 8
ATTN_KV_CHUNK = 256
COMBINE_GROUPS = 8
LOG2_E = math.log2(math.e)
VMEM_LIMIT = 56 * 1024 * 1024
NEG_BIG = -1e30


class _Tiles(NamedTuple):
    n_batch: int
    tps: int
    skip_ctx: bool

    @property
    def per_seq(self):
        return self.tps - 1 if self.skip_ctx else self.tps

    @property
    def count(self):
        return self.n_batch * self.per_seq

    def full(self, i):
        if not self.skip_ctx:
            return i
        return (i // self.per_seq) * self.tps + 1 + i % self.per_seq

    def mod_row(self, i):
        if self.skip_ctx:
            return i // self.per_seq
        return jnp.where(i % self.tps == 0, self.n_batch, i // self.tps)


def _cparams(*sem):
    return pltpu.CompilerParams(dimension_semantics=sem, vmem_limit_bytes=VMEM_LIMIT)


def _rms(x):
    return x * lax.rsqrt(jnp.mean(x * x, axis=-1, keepdims=True) + EPS)


def _sigmoid(x):
    return 0.5 * jnp.tanh(0.5 * x) + 0.5


def _split_bf16(x):
    hi = x.astype(jnp.bfloat16)
    return hi, (x - hi.astype(F32)).astype(jnp.bfloat16)


def _mod_kernel(c_ref, w_ref, b_ref, o_ref):
    c = c_ref[...]
    s = c * jax.nn.sigmoid(c)
    o_ref[...] = jnp.dot(s, w_ref[...], preferred_element_type=F32, precision=HIGHEST) + b_ref[...]


def _modulation(c_all, w_mod, b_mod):
    n_layer = w_mod.shape[0]
    mb, d = c_all.shape
    tn = 1024
    return pl.pallas_call(
        _mod_kernel,
        grid=(n_layer, N_MOD * d // tn),
        in_specs=[
            pl.BlockSpec((mb, d), lambda l, j: (0, 0)),
            pl.BlockSpec((None, d, tn), lambda l, j: (l, 0, j)),
            pl.BlockSpec((None, 1, tn), lambda l, j: (l, 0, j)),
        ],
        out_specs=pl.BlockSpec((None, mb, tn), lambda l, j: (l, 0, j)),
        out_shape=jax.ShapeDtypeStruct((n_layer, mb, N_MOD * d), F32),
        compiler_params=_cparams("parallel", "parallel"),
        name="modulation",
    )(c_all, w_mod, b_mod.reshape(n_layer, 1, N_MOD * d))


def _rope(acc, cos, sin_signed):
    w = acc.shape[1]
    half = ROT_AX // 2
    lane = lax.broadcasted_iota(jnp.int32, acc.shape, 1)
    partner = jnp.where((lane & half) == 0, pltpu.roll(acc, w - half, 1), pltpu.roll(acc, half, 1))
    reps = w // cos.shape[1]
    return acc * jnp.tile(cos, (1, reps)) + partner * jnp.tile(sin_signed, (1, reps))


def _inproj_kernel(x_ref, g_ref, mod_ref, cos_ref, sin_ref, w_ref, o_ref, kt_ref):
    x = x_ref[...]
    h = (_rms(x) * g_ref[...] * (1.0 + mod_ref[1:2, :]) + mod_ref[0:1, :]).astype(MXU_DT)
    k_chunk = P_K // PROJ_CHUNK
    q_chunk = P_Q // PROJ_CHUNK
    chunks = range(IN_COLS // PROJ_CHUNK)
    cols = lambda c: slice(c * PROJ_CHUNK, (c + 1) * PROJ_CHUNK)

    def project(c):
        w0 = (ORIG_GATE0 + c * PROJ_CHUNK) % IN_COLS
        acc = jnp.dot(h, w_ref[:, w0:w0 + PROJ_CHUNK], preferred_element_type=F32)
        if c == k_chunk:
            acc = _rope(acc, cos_ref[...], sin_ref[...])
            kt_ref[...] = acc.T.astype(kt_ref.dtype)
        elif c == q_chunk:
            acc = _rope(acc, cos_ref[...], sin_ref[...]) * (DK ** -0.5 * LOG2_E)
        o_ref[:, cols(c)] = acc.astype(o_ref.dtype)

    for c in chunks:
        project(c)


def _inproj(xall, g1, mod, cos, sin_signed, w_in, layer, tiles):
    r, d = xall.shape
    tm = ROW_TILE
    tps = tiles.tps
    assert ORIG_GATE0 % PROJ_CHUNK == 0 and not tiles.skip_ctx

    def mod_idx(i):
        return (tiles.mod_row(i), 0, 0)

    return pl.pallas_call(
        _inproj_kernel,
        grid=(r // tm,),
        in_specs=[
            pl.BlockSpec((tm, d), lambda i: (i, 0)),
            pl.BlockSpec((1, d), lambda i: (0, 0)),
            pl.BlockSpec((None, N_MOD, d), mod_idx),
            pl.BlockSpec((tm, LANE), lambda i: (i % tps, 0)),
            pl.BlockSpec((tm, LANE), lambda i: (i % tps, 0)),
            pl.BlockSpec((None, d, IN_COLS), lambda i: (layer, 0, 0)),
        ],
        out_specs=[pl.BlockSpec((tm, IN_COLS), lambda i: (i, 0)),
                   pl.BlockSpec((PROJ_CHUNK, tm), lambda i: (0, i))],
        out_shape=[jax.ShapeDtypeStruct((r, IN_COLS), ACT_DT), jax.ShapeDtypeStruct((PROJ_CHUNK, r), ACT_DT)],
        compiler_params=_cparams("parallel"),
        name="inproj",
    )(xall, g1, mod, cos, sin_signed, w_in)


def _attn_kernel(dl_ref, g_ref, q_ref, kt_ref, v_ref, o_ref, vext_ref, s_ref, mx_ref, *,
                 lam_init, ctx_rows, has_ctx):
    tq = s_ref.shape[2]
    seq = kt_ref.shape[1]
    one_hot = lax.broadcasted_iota(jnp.int32, v_ref.shape, 1) == 0
    vext_ref[:, :DV] = v_ref[...]
    vext_ref[:, DV:] = jnp.where(one_hot, 1.0, 0.0).astype(vext_ref.dtype)

    dl = dl_ref[...]
    lam = (jnp.exp(jnp.sum(dl[0:1] * dl[1:2], keepdims=True))
           - jnp.exp(jnp.sum(dl[2:3] * dl[3:4], keepdims=True)) + lam_init)
    def chunk_list(kv_rows):
        kc = min(ATTN_KV_CHUNK, kv_rows)
        assert kv_rows % kc == 0
        return [slice(c * kc, (c + 1) * kc) for c in range(kv_rows // kc)]

    def score_pass(kv_rows, q_rows, buf):
        q = q_ref[q_rows, :]
        lane = lax.broadcasted_iota(jnp.int32, q.shape, 1)
        zero = jnp.zeros_like(q)
        qms = (jnp.where(lane < DK, q, zero), jnp.where(lane >= DK, q, zero))
        mruns = [None, None]
        for cols in chunk_list(kv_rows):
            for m in range(2):
                s = jnp.dot(qms[m], kt_ref[:, cols], preferred_element_type=F32)
                s_ref[buf, m, :, cols] = s
                for g in range(s.shape[1] // LANE):
                    blk = s[:, g * LANE:(g + 1) * LANE]
                    mruns[m] = blk if mruns[m] is None else jnp.maximum(mruns[m], blk)
            yield
        for m in range(2):
            mx = jnp.max(mruns[m], axis=-1, keepdims=True)
            mx_ref[buf, m] = jnp.broadcast_to(mx, mx_ref.shape[2:])
        yield

    def value_pass(kv_rows, out_rows, buf):
        mxs = [mx_ref[buf, m][:, 0:1] for m in range(2)]
        accs = [jnp.zeros((tq, 2 * DV), F32)] * 2
        for cols in chunk_list(kv_rows):
            for m in range(2):
                p = jnp.exp2(s_ref[buf, m, :, cols] - mxs[m])
                accs[m] = accs[m] + jnp.dot(p.astype(MXU_DT), vext_ref[cols, :], preferred_element_type=F32)
            yield
        o = (accs[0][:, :DV] * (1.0 / accs[0][:, DV:DV + 1])
             - accs[1][:, :DV] * (lam / accs[1][:, DV:DV + 1]))
        o_ref[out_rows, :] = (_rms(o) * g_ref[...] * (1.0 - lam_init)).astype(o_ref.dtype)
        yield

    def run(*passes):
        live = list(passes)
        while live:
            live = [g for g in live if next(g, "done") != "done"]

    out_off = 0
    if has_ctx:
        run(score_pass(ctx_rows, slice(0, ctx_rows), 0))
        run(value_pass(ctx_rows, slice(0, ctx_rows), 0))
        out_off = ctx_rows

    n_lat = (seq - ctx_rows) // tq
    q_at = lambda i: pl.ds(pl.multiple_of(ctx_rows + i * tq, tq), tq)
    o_at = lambda i: pl.ds(pl.multiple_of(out_off + i * tq, tq), tq)
    assert n_lat % 2 == 0
    run(score_pass(seq, q_at(0), 0))

    def tile_pair(j, carry):
        a = 2 * j
        run(score_pass(seq, q_at(a + 1), 1), value_pass(seq, o_at(a), 0))
        run(score_pass(seq, q_at(a + 2), 0), value_pass(seq, o_at(a + 1), 1))
        return carry

    lax.fori_loop(0, n_lat // 2 - 1, tile_pair, 0)
    run(score_pass(seq, q_at(n_lat - 1), 1), value_pass(seq, o_at(n_lat - 2), 0))
    run(value_pass(seq, o_at(n_lat - 1), 1))


def _attention(p, k_t, diff_lambda, g_sub, lam_init, tiles, ctx_rows):
    tq = ROW_TILE
    tps = tiles.tps
    seq = tps * tq
    out_rows = tiles.per_seq * tq
    assert ctx_rows == tq
    kern = functools.partial(_attn_kernel, lam_init=lam_init, ctx_rows=ctx_rows, has_ctx=not tiles.skip_ctx)
    return pl.pallas_call(
        kern,
        grid=(tiles.n_batch, H_A),
        in_specs=[
            pl.BlockSpec((4, DK), lambda b, h: (0, 0)),
            pl.BlockSpec((1, DV), lambda b, h: (0, 0)),
            pl.BlockSpec((seq, LANE), lambda b, h: (b, P_Q // LANE + h)),
            pl.BlockSpec((LANE, seq), lambda b, h: (h, b)),
            pl.BlockSpec((seq, LANE), lambda b, h: (b, P_V // LANE + h)),
        ],
        out_specs=pl.BlockSpec((out_rows, LANE), lambda b, h: (b, h)),
        out_shape=jax.ShapeDtypeStruct((tiles.count * tq, H_A * DV), ACT_DT),
        scratch_shapes=[pltpu.VMEM((seq, 2 * DV), MXU_DT), pltpu.VMEM((2, 2, tq, seq), F32),
                        pltpu.VMEM((2, 2, tq, LANE), F32)],
        compiler_params=_cparams("parallel", "parallel"),
        name="diff_attention",
    )(diff_lambda, g_sub, p, k_t, p)


def _gmlp_kernel(uv_ref, gv_ref, ws_ref, bs_ref, o_ref):
    uv = jax.nn.gelu(uv_ref[...])
    u = uv[:, :W_B].astype(F32)
    v = (_rms(uv[:, W_B:].astype(F32)) * gv_ref[...]).astype(MXU_DT)
    gw = W_B // G_B
    for c in range(uv.shape[0] // CHUNK_B):
        rows = slice(c * CHUNK_B, (c + 1) * CHUNK_B)
        for g in range(G_B):
            cols = slice(g * gw, (g + 1) * gw)
            mixed = jnp.dot(ws_ref[g], v[rows, cols], preferred_element_type=F32) + bs_ref[:, cols]
            o_ref[rows, cols] = (u[rows, cols] * mixed).astype(o_ref.dtype)


def _gmlp(p, g_v, w_s, b_full, tiles):
    tm = ROW_TILE
    r = tiles.count * tm
    return pl.pallas_call(
        _gmlp_kernel,
        grid=(tiles.count,),
        in_specs=[
            pl.BlockSpec((tm, 2 * W_B), lambda i: (tiles.full(i), P_UV // (2 * W_B))),
            pl.BlockSpec((1, W_B), lambda i: (0, 0)),
            pl.BlockSpec((G_B, CHUNK_B, CHUNK_B), lambda i: (0, 0, 0)),
            pl.BlockSpec((CHUNK_B, W_B), lambda i: (0, 0)),
        ],
        out_specs=pl.BlockSpec((tm, W_B), lambda i: (i, 0)),
        out_shape=jax.ShapeDtypeStruct((r, W_B), ACT_DT),
        compiler_params=_cparams("parallel"),
        name="gmlp",
    )(p, g_v, w_s, b_full)


def _mfeat_kernel(x_ref, prev_ref, next_ref, wc_ref, bc_ref, wqkv_ref, wif_ref, bif_ref,
                  xc_ref, q_ref, k_ref, v_ref, g_ref, *, tps):
    tm = x_ref.shape[0]
    j = pl.program_id(0) % tps
    seg_start = jnp.logical_or(j == 0, j == 1)
    seg_end = jnp.logical_or(j == 0, j == tps - 1)
    x = x_ref[...].astype(F32)
    prow = jnp.where(seg_start, 0.0, prev_ref[...].astype(F32)[HALO - 1:HALO, :])
    nrow = jnp.where(seg_end, 0.0, next_ref[...].astype(F32)[0:1, :])
    rid = lax.broadcasted_iota(jnp.int32, x.shape, 0)
    xp = jnp.where(rid == 0, prow, pltpu.roll(x, 1, 0))
    xn = jnp.where(rid == tm - 1, nrow, pltpu.roll(x, tm - 1, 0))
    xc = wc_ref[0:1, :] * xp + wc_ref[1:2, :] * x + wc_ref[2:3, :] * xn + bc_ref[...]
    xc = xc * _sigmoid(xc)
    xc_ref[...] = xc.astype(xc_ref.dtype)
    xcb = xc.astype(MXU_DT)
    xb = x_ref[...]
    qs, ks, vs = [], [], []
    for h in range(H_C):
        cols = slice(h * DH_C, (h + 1) * DH_C)
        qs.append(jnp.dot(xcb[:, cols], wqkv_ref[0, h], preferred_element_type=F32))
        ks.append(jnp.dot(xcb[:, cols], wqkv_ref[1, h], preferred_element_type=F32))
        vs.append(jnp.dot(xb[:, cols], wqkv_ref[2, h], preferred_element_type=F32))
    q = jnp.concatenate(qs, axis=1)
    k = jnp.concatenate(ks, axis=1)
    v = jnp.concatenate(vs, axis=1)
    qkv = jnp.concatenate([q, k, v], axis=1).astype(MXU_DT)
    gates = jnp.dot(qkv, wif_ref[...], preferred_element_type=F32) + bif_ref[...]
    g_ref[...] = gates.T[0:g_ref.shape[0], :]
    q_ref[...] = q.T.astype(q_ref.dtype)
    k_ref[...] = (k * (DH_C ** -0.5)).astype(k_ref.dtype)
    v_ref[...] = v.T.astype(v_ref.dtype)


def _mlstm_features(p, w_conv, b_conv, w_qkv, w_if_pad, b_if_pad, tps):
    r = p.shape[0]
    tm = ROW_TILE
    hpt = tm // HALO
    xm_blk = P_XM // W_C
    last_halo = r // HALO - 1
    kern = functools.partial(_mfeat_kernel, tps=tps)
    act = jax.ShapeDtypeStruct((r, W_C), ACT_DT)
    act_t = jax.ShapeDtypeStruct((W_C, r), ACT_DT)
    row_spec = pl.BlockSpec((tm, W_C), lambda i: (i, 0))
    col_spec = pl.BlockSpec((W_C, tm), lambda i: (0, i))
    return pl.pallas_call(
        kern,
        grid=(r // tm,),
        in_specs=[
            pl.BlockSpec((tm, W_C), lambda i: (i, xm_blk)),
            pl.BlockSpec((HALO, W_C), lambda i: (jnp.maximum(i * hpt - 1, 0), xm_blk)),
            pl.BlockSpec((HALO, W_C), lambda i: (jnp.minimum((i + 1) * hpt, last_halo), xm_blk)),
            pl.BlockSpec((CONV_K, W_C), lambda i: (0, 0)),
            pl.BlockSpec((1, W_C), lambda i: (0, 0)),
            pl.BlockSpec((3, H_C, DH_C, DH_C), lambda i: (0, 0, 0, 0)),
            pl.BlockSpec((3 * W_C, LANE), lambda i: (0, 0)),
            pl.BlockSpec((1, LANE), lambda i: (0, 0)),
        ],
        out_specs=[row_spec, col_spec, row_spec, col_spec,
                   pl.BlockSpec((2 * GATE_ROWS, tm), lambda i: (0, i))],
        out_shape=[act, act_t, act, act_t, jax.ShapeDtypeStruct((2 * GATE_ROWS, r), F32)],
        compiler_params=_cparams("parallel"),
        name="mlstm_features",
    )(p, p, p, w_conv, b_conv, w_qkv, w_if_pad, b_if_pad)


GATE_ROWS = 2 * H_C
GATE_PACK = ("r", "cmx", "bcum")


def _gate_prep_kernel(g_ref, pack_ref, rt_ref):
    seg = MLSTM_CHUNK
    gt_fwd = g_ref[0:GATE_ROWS, :]
    gt_bwd = g_ref[GATE_ROWS:2 * GATE_ROWS, :]
    n = gt_fwd.shape[1]
    lane = lax.broadcasted_iota(jnp.int32, gt_fwd.shape, 1)
    in_seg = lane & (seg - 1)

    def scan_both(xf, xb, op):
        sh = 1
        while sh < seg:
            xf = jnp.where(in_seg >= sh, op(xf, pltpu.roll(xf, sh, 1)), xf)
            xb = jnp.where(in_seg < seg - sh, op(xb, pltpu.roll(xb, n - sh, 1)), xb)
            sh *= 2
        return xf, xb

    bcum_f, bcum_b = scan_both(jax.nn.log_sigmoid(pltpu.roll(gt_fwd, H_C, 0)),
                               jax.nn.log_sigmoid(pltpu.roll(gt_bwd, H_C, 0)), jnp.add)
    r_f, r_b = gt_fwd - bcum_f, gt_bwd - bcum_b
    cmx_f, cmx_b = scan_both(r_f, r_b, jnp.maximum)
    pack_ref[...] = jnp.concatenate([r_f, cmx_f, bcum_f, r_b, cmx_b, bcum_b], axis=0)
    rt_ref[...] = jnp.concatenate([r_f, r_b, jnp.zeros((LANE - 2 * GATE_ROWS, n), F32)], axis=0).T


def _mlstm_gate_prep(gates_t, n_batch, seq):
    rows = 2 * len(GATE_PACK) * GATE_ROWS
    r = gates_t.shape[1]
    return pl.pallas_call(
        _gate_prep_kernel,
        grid=(n_batch,),
        in_specs=[pl.BlockSpec((2 * GATE_ROWS, seq), lambda b: (0, b))],
        out_specs=[pl.BlockSpec((rows, seq), lambda b: (0, b)), pl.BlockSpec((seq, LANE), lambda b: (b, 0))],
        out_shape=[jax.ShapeDtypeStruct((rows, r), F32), jax.ShapeDtypeStruct((r, LANE), F32)],
        compiler_params=_cparams("parallel"),
        name="mlstm_gate_prep",
    )(gates_t)


def _mlstm_gates(pack, m_ref, reverse):
    ln = pack.shape[1]
    part = {name: pack[i * GATE_ROWS:(i + 1) * GATE_ROWS] for i, name in enumerate(GATE_PACK)}
    end = slice(0, 1) if reverse else slice(ln - 1, ln)
    g = part["bcum"][:, end]
    r_max = part["cmx"][:, end]
    m_loc = g + r_max
    w_end = jnp.exp(part["r"] - r_max)
    m_old = m_ref[:, 0:1]
    m_new = jnp.maximum(g + m_old, m_loc)
    a_old = jnp.exp(g + m_old - m_new)
    a_new = jnp.exp(m_loc - m_new)
    mx = jnp.maximum(m_old, part["cmx"])
    w_inter = jnp.exp(m_old - mx)
    e_neg = jnp.exp(-(part["bcum"] + mx))
    s_id = lax.broadcasted_iota(jnp.int32, (ln, ln), 0)
    t_id = lax.broadcasted_iota(jnp.int32, (ln, ln), 1)
    keep = (s_id >= t_id) if reverse else (s_id <= t_id)
    return dict(mx=mx, w_inter=w_inter, e_neg=e_neg, w_end=w_end, a_old=a_old, a_new=a_new,
                m_new=m_new, keep=keep)


def _mscan_chunk(gf_ref, rf_ref, qf_ref, kf_ref, vf_ref, gb_ref, rb_ref, qb_ref, kb_ref, vb_ref,
                 hf_ref, hb_ref, cf_ref, mf_ref, cb_ref, mb_ref):
    ln = qf_ref.shape[1]
    pr = len(GATE_PACK) * GATE_ROWS
    dirs = ((_mlstm_gates(gf_ref[0:pr, :], mf_ref, False), qf_ref, kf_ref, vf_ref, cf_ref, hf_ref, rf_ref),
            (_mlstm_gates(gb_ref[pr:2 * pr, :], mb_ref, True), qb_ref, kb_ref, vb_ref, cb_ref, hb_ref, rb_ref))
    pairs = [(d, h) for d in range(2) for h in range(H_C)]
    ones_row = jnp.where(lax.broadcasted_iota(jnp.int32, (DH_C, ln), 0) == 0, 1.0, 0.0).astype(MXU_DT)
    head = lambda h: slice(h * DH_C, (h + 1) * DH_C)
    row = lambda h: slice(h, h + 1)

    ks = [dirs[d][2][:, head(h)] for d, h in pairs]
    qts = [dirs[d][1][head(h), :] for d, h in pairs]
    vexts = [jnp.concatenate([dirs[d][3][head(h), :], ones_row], axis=0) for d, h in pairs]
    cexts = [dirs[d][4][h] for d, h in pairs]
    qks = [jnp.dot(k, q_t, preferred_element_type=F32) for k, q_t in zip(ks, qts)]
    rhss = []
    for i, (d, h) in enumerate(pairs):
        g = dirs[d][0]
        r_col = dirs[d][6][:, d * GATE_ROWS + h:d * GATE_ROWS + h + 1]
        d_t = jnp.exp(jnp.where(g["keep"], r_col - g["mx"][row(h), :], NEG_BIG))
        wq = (qts[i].astype(F32) * g["w_inter"][row(h), :]).astype(MXU_DT)
        rhss.append(jnp.concatenate([(qks[i] * d_t).astype(MXU_DT), wq], axis=0))
    ress = [jnp.dot(jnp.concatenate([vexts[i], cexts[i].astype(MXU_DT)], axis=1), rhss[i],
                    preferred_element_type=F32) for i in range(len(pairs))]
    for i, (d, h) in enumerate(pairs):
        g = dirs[d][0]
        den = ress[i][DH_C:DH_C + 1, :]
        hh = ress[i][:DH_C, :] / jnp.maximum(jnp.abs(den), g["e_neg"][row(h), :])
        dirs[d][5][head(h), :] = hh.astype(dirs[d][5].dtype)
    clocs = []
    for i, (d, h) in enumerate(pairs):
        vw_t = (vexts[i].astype(F32) * dirs[d][0]["w_end"][row(h), :]).astype(MXU_DT)
        clocs.append(jnp.dot(vw_t, ks[i], preferred_element_type=F32))
    for i, (d, h) in enumerate(pairs):
        g = dirs[d][0]
        dirs[d][4][h] = g["a_old"][row(h), :] * cexts[i] + g["a_new"][row(h), :] * clocs[i]
    mf_ref[...] = jnp.broadcast_to(dirs[0][0]["m_new"], mf_ref.shape)
    mb_ref[...] = jnp.broadcast_to(dirs[1][0]["m_new"], mb_ref.shape)


def _mscan_kernel(g_ref, r_ref, q_ref, k_ref, v_ref, hf_ref, hb_ref, cf_ref, mf_ref, cb_ref, mb_ref, *, ncc):
    ln = MLSTM_CHUNK
    nck = q_ref.shape[1] // ln
    cf_ref[...] = jnp.zeros_like(cf_ref)
    cb_ref[...] = jnp.zeros_like(cb_ref)
    mf_ref[...] = jnp.zeros_like(mf_ref)
    mb_ref[...] = jnp.zeros_like(mb_ref)

    def chunk_views(c):
        at = pl.ds(pl.multiple_of(c * ln, ln), ln)
        return ((g_ref.at[:, at], r_ref.at[at, :], q_ref.at[:, at], k_ref.at[at, :], v_ref.at[:, at]),
                (hf_ref.at[:, at], hb_ref.at[:, at]))

    def trip(s, carry):
        ins_f, (hf, _) = chunk_views(s)
        ins_b, (_, hb) = chunk_views(jnp.where(s < ncc, ncc - 1 - s, nck - 1 + ncc - s))
        _mscan_chunk(*ins_f, *ins_b, hf, hb, cf_ref, mf_ref, cb_ref, mb_ref)
        return carry

    lax.fori_loop(0, nck, trip, 0)


def _mlstm_scan(q_t, k, v_t, gate_pack, r_t, n_batch, seq, ctx_rows):
    rows = lambda w: pl.BlockSpec((seq, w), lambda b: (b, 0))
    cols = lambda h: pl.BlockSpec((h, seq), lambda b: (0, b))
    out = jax.ShapeDtypeStruct(q_t.shape, ACT_DT)
    return pl.pallas_call(
        functools.partial(_mscan_kernel, ncc=ctx_rows // MLSTM_CHUNK),
        grid=(n_batch,),
        in_specs=[cols(gate_pack.shape[0]), rows(LANE), cols(W_C), rows(W_C), cols(W_C)],
        out_specs=[cols(W_C), cols(W_C)],
        out_shape=[out, out],
        scratch_shapes=[
            pltpu.VMEM((H_C, 2 * DH_C, DH_C), F32), pltpu.VMEM((8, LANE), F32),
            pltpu.VMEM((H_C, 2 * DH_C, DH_C), F32), pltpu.VMEM((8, LANE), F32),
        ],
        compiler_params=_cparams("parallel"),
        name="mlstm_scan",
    )(gate_pack, r_t, q_t, k, v_t)


def _route(logits, carry):
    lane = lax.broadcasted_iota(jnp.int32, logits.shape, 1).astype(F32)
    big = float(4 * LANE)
    gl = jnp.where(lane < N_GROUPS, logits, NEG_BIG)
    gmax = jnp.max(gl, axis=-1, keepdims=True)
    g_star = jnp.min(jnp.where(gl == gmax, lane, big), axis=-1, keepdims=True)
    p_g = 1.0 / jnp.sum(jnp.exp(gl - gmax), axis=-1, keepdims=True)
    e_lo = g_star * EXP_PER_GROUP + N_GROUPS
    in_group = jnp.logical_and(lane >= e_lo, lane < e_lo + EXP_PER_GROUP)
    el = jnp.where(in_group, logits, NEG_BIG)
    v1 = jnp.max(el, axis=-1, keepdims=True)
    i1 = jnp.min(jnp.where(el == v1, lane, big), axis=-1, keepdims=True)
    el2 = jnp.where(lane == i1, NEG_BIG, el)
    v2 = jnp.max(el2, axis=-1, keepdims=True)
    i2 = jnp.min(jnp.where(el2 == v2, lane, big), axis=-1, keepdims=True)
    e21 = jnp.exp(v2 - v1)
    w1 = p_g / (1.0 + e21)
    w2 = p_g * e21 / (1.0 + e21)
    e1 = i1 - N_GROUPS
    e2 = i2 - N_GROUPS
    oh1 = lane == e1
    oh2 = lane == e2
    both = jnp.where(jnp.logical_or(oh1, oh2), 1.0, 0.0)
    tm = logits.shape[0]
    lower = (lax.broadcasted_iota(jnp.int32, (tm, tm), 1) < lax.broadcasted_iota(jnp.int32, (tm, tm), 0))
    before = jnp.dot(jnp.where(lower, 1.0, 0.0).astype(MXU_DT), both.astype(MXU_DT),
                     preferred_element_type=F32) + carry
    rank1 = jnp.sum(jnp.where(oh1, before, 0.0), axis=-1, keepdims=True)
    rank2 = jnp.sum(jnp.where(oh2, before, 0.0), axis=-1, keepdims=True)
    out = jnp.where(lane == 0, e1, 0.0)
    out = jnp.where(lane == 1, e2, out)
    out = jnp.where(lane == 2, w1, out)
    out = jnp.where(lane == 3, w2, out)
    out = jnp.where(lane == 4, rank1, out)
    out = jnp.where(lane == 5, rank2, out)
    return out, carry + jnp.sum(both, axis=0, keepdims=True)


def _merge_kernel(x_ref, gate_ref, z_ref, ya_ref, yb_ref, hf_ref, hb_ref, xc_ref, gm_ref, sk_ref,
                  wbr_ref, wo_ref, mod_ref, g2_ref, wr_ref, br_ref, xo_ref, rt_ref, cnt_ref, carry_ref):
    @pl.when(pl.program_id(0) == 0)
    def _():
        carry_ref[...] = jnp.zeros_like(carry_ref)

    hs_t = hf_ref[...].astype(F32) + hb_ref[...].astype(F32)
    hn_t = []
    for h in range(H_C):
        part = hs_t[h * DH_C:(h + 1) * DH_C, :]
        hn_t.append(part * lax.rsqrt(jnp.mean(part * part, axis=0, keepdims=True) + EPS))
    hn = jnp.concatenate(hn_t, axis=0).T * gm_ref[...]
    yc = (hn + sk_ref[...] * xc_ref[...].astype(F32)) * _sigmoid(z_ref[...].astype(F32))
    ys = (ya_ref[...], yb_ref[...], yc.astype(MXU_DT))
    merged = None
    for i in range(3):
        gate = _sigmoid(gate_ref[:, i * D_MODEL:(i + 1) * D_MODEL].astype(MXU_DT))
        term = gate * jnp.dot(ys[i], wbr_ref[i], preferred_element_type=F32).astype(MXU_DT)
        merged = term if merged is None else merged + term
    y = jnp.dot(merged, wo_ref[...], preferred_element_type=F32)
    xn = x_ref[...] + mod_ref[2:3, :] * y
    xo_ref[...] = xn
    h2 = _rms(xn) * g2_ref[...] * (1.0 + mod_ref[4:5, :]) + mod_ref[3:4, :]
    h_hi, h_lo = _split_bf16(h2)
    logits = (jnp.dot(h_hi, wr_ref[0], preferred_element_type=F32)
              + jnp.dot(h_lo, wr_ref[0], preferred_element_type=F32)
              + jnp.dot(h_hi, wr_ref[1], preferred_element_type=F32)) + br_ref[...]
    route, carry = _route(logits, carry_ref[0:1, :])
    rt_ref[...] = route
    carry_ref[0:1, :] = carry
    cnt_ref[...] = jnp.broadcast_to(carry, cnt_ref.shape)


def _merge(xall, p, ya, yb, hf, hb, xconv, g_m, skip, w_br, w_o, mod, g2, w_route, b_route, tiles):
    d = xall.shape[1]
    tm = ROW_TILE
    r = tiles.count * tm

    def mod_idx(i):
        return (tiles.mod_row(i), 0, 0)

    row = lambda w: pl.BlockSpec((tm, w), lambda i: (i, 0))
    frow = lambda w: pl.BlockSpec((tm, w), lambda i: (tiles.full(i), 0))
    fcol = lambda w: pl.BlockSpec((w, tm), lambda i: (0, tiles.full(i)))
    full = lambda shape: pl.BlockSpec(shape, lambda i: (0,) * len(shape))
    return pl.pallas_call(
        _merge_kernel,
        grid=(tiles.count,),
        in_specs=[
            frow(d),
            pl.BlockSpec((tm, GATE_W), lambda i: (tiles.full(i), P_GATE // GATE_W)),
            pl.BlockSpec((tm, W_C), lambda i: (tiles.full(i), P_Z // W_C)),
            row(W_BR), row(W_BR), fcol(W_C), fcol(W_C), frow(W_C),
            full((1, W_C)), full((1, W_C)),
            full((3, W_BR, d)), full((d, d)),
            pl.BlockSpec((None, N_MOD, d), mod_idx),
            full((1, d)), full((2, d, LANE)), full((1, LANE)),
        ],
        out_specs=[row(d), row(LANE), full((8, LANE))],
        out_shape=[jax.ShapeDtypeStruct((r, d), F32), jax.ShapeDtypeStruct((r, LANE), F32),
                   jax.ShapeDtypeStruct((8, LANE), F32)],
        scratch_shapes=[pltpu.VMEM((8, LANE), F32)],
        compiler_params=_cparams("arbitrary"),
        name="merge_route",
    )(xall, p, p, ya, yb, hf, hb, xconv, g_m, skip, w_br, w_o, mod, g2, w_route, b_route)


def _row_copy(src_hbm, src_row, dst_ref, dst_row, sem):
    return pltpu.make_async_copy(src_hbm.at[pl.ds(src_row, 1), :], dst_ref.at[pl.ds(dst_row, 1), :], sem)


def _dispatch_kernel(fill_ref, nlive_ref, pos_ref, x_ref, g2_ref, mod_ref, xs_hbm, hbuf, zbuf, zsem, sem):
    i = pl.program_id(0)
    n = pl.num_programs(0)
    tm = x_ref.shape[0]
    slot = i % 2
    bm = zbuf.shape[0]
    n_blocks = xs_hbm.shape[0] // bm

    @pl.when(i == 0)
    def _():
        zbuf[...] = jnp.zeros_like(zbuf)

        def fill_block(b):
            return pltpu.make_async_copy(zbuf, xs_hbm.at[pl.ds(b * bm, bm), :], zsem)

        def fill_last(k, carry):
            fill_block(fill_ref[k]).start()
            return carry

        def fill_unused(b, carry):
            fill_block(b).start()
            return carry

        def wait_fill(k, carry):
            fill_block(0).wait()
            return carry

        n_live = nlive_ref[0]
        n_used = nlive_ref[1]
        lax.fori_loop(0, n_used, fill_last, 0)
        lax.fori_loop(n_live, n_blocks, fill_unused, 0)
        lax.fori_loop(0, n_used + n_blocks - n_live, wait_fill, 0)

    hbuf[slot] = _rms(x_ref[...]) * g2_ref[...] * (1.0 + mod_ref[4:5, :]) + mod_ref[3:4, :]

    for r in range(tm):
        for j in range(TOP_K):
            pltpu.make_async_copy(hbuf.at[slot, pl.ds(r, 1), :],
                                  xs_hbm.at[pl.ds(pos_ref[0, j * tm + r], 1), :], sem.at[slot]).start()

    def wait_tile(s):
        pltpu.make_async_copy(hbuf.at[s], xs_hbm.at[pl.ds(0, tm), :], sem.at[s]).wait()
        pltpu.make_async_copy(hbuf.at[s], xs_hbm.at[pl.ds(0, tm), :], sem.at[s]).wait()

    @pl.when(i > 0)
    def _():
        wait_tile(1 - slot)

    @pl.when(i == n - 1)
    def _():
        wait_tile(slot)


def _moe_dispatch(xall, g2, mod, fill_blocks, n_live, pos_tiles, n_slots, tiles):
    r, d = xall.shape
    tm = ROW_TILE
    assert TOP_K == 2 and r == tiles.count * tm

    def mod_idx(i, fb, nl):
        return (tiles.mod_row(i), 0, 0)

    grid_spec = pltpu.PrefetchScalarGridSpec(
        num_scalar_prefetch=2,
        grid=(r // tm,),
        in_specs=[
            pl.BlockSpec((None, 1, TOP_K * tm), lambda i, fb, nl: (i, 0, 0), memory_space=pltpu.SMEM),
            pl.BlockSpec((tm, d), lambda i, fb, nl: (i, 0)),
            pl.BlockSpec((1, d), lambda i, fb, nl: (0, 0)),
            pl.BlockSpec((None, N_MOD, d), mod_idx),
        ],
        out_specs=pl.BlockSpec(memory_space=pl.ANY),
        scratch_shapes=[pltpu.VMEM((2, tm, d), F32), pltpu.VMEM((MOE_BLOCK, d), F32),
                        pltpu.SemaphoreType.DMA(()), pltpu.SemaphoreType.DMA((2,))],
    )
    return pl.pallas_call(
        _dispatch_kernel,
        grid_spec=grid_spec,
        out_shape=jax.ShapeDtypeStruct((n_slots, d), F32),
        compiler_params=_cparams("arbitrary"),
        name="moe_dispatch",
    )(fill_blocks, n_live, pos_tiles, xall, g2, mod)


def _moe_kernel(be_ref, nv_ref, x_ref, wg_ref, wu_ref, wd_ref, y_ref, wg_mx, wu_mx, wd_mx):
    i = pl.program_id(0)
    n_valid = nv_ref[i]

    @pl.when(jnp.logical_or(i == 0, be_ref[i] != be_ref[jnp.maximum(i - 1, 0)]))
    def _():
        wg_mx[...] = wg_ref[...].astype(wg_mx.dtype)
        wu_mx[...] = wu_ref[...].astype(wu_mx.dtype)
        wd_mx[...] = wd_ref[...].astype(wd_mx.dtype)

    @pl.when(n_valid > 0)
    def _():
        x = x_ref[...].astype(MXU_DT)
        a = jnp.dot(x, wg_mx[...], preferred_element_type=F32)
        u = jnp.dot(x, wu_mx[...], preferred_element_type=F32)
        act = (a * _sigmoid(a) * u).astype(MXU_DT)
        y_ref[...] = jnp.dot(act, wd_mx[...], preferred_element_type=F32)

    @pl.when(n_valid <= 0)
    def _():
        y_ref[...] = jnp.zeros_like(y_ref)


def _moe_experts(xs, block_e, n_valid, w_gate, w_up, w_down, layer):
    n_slots, d = xs.shape
    bm = MOE_BLOCK
    grid_spec = pltpu.PrefetchScalarGridSpec(
        num_scalar_prefetch=2,
        grid=(n_slots // bm,),
        in_specs=[
            pl.BlockSpec((bm, d), lambda i, be, nv: (i, 0)),
            pl.BlockSpec((None, None, d, D_EXPERT), lambda i, be, nv: (layer, be[i], 0, 0)),
            pl.BlockSpec((None, None, d, D_EXPERT), lambda i, be, nv: (layer, be[i], 0, 0)),
            pl.BlockSpec((None, None, D_EXPERT, d), lambda i, be, nv: (layer, be[i], 0, 0)),
        ],
        out_specs=pl.BlockSpec((bm, d), lambda i, be, nv: (i, 0)),
        scratch_shapes=[pltpu.VMEM((d, D_EXPERT), MXU_DT), pltpu.VMEM((d, D_EXPERT), MXU_DT),
                        pltpu.VMEM((D_EXPERT, d), MXU_DT)],
    )
    return pl.pallas_call(
        _moe_kernel,
        grid_spec=grid_spec,
        out_shape=jax.ShapeDtypeStruct((n_slots, d), F32),
        compiler_params=_cparams("arbitrary"),
        name="moe_experts",
    )(block_e, n_valid, xs, w_gate, w_up, w_down)


def _combine_kernel(pos_ref, pos_next_ref, x_ref, rt_ref, mod_ref, gf_ref, yb_hbm, xo_ref, ybuf, sem, *, final):
    i = pl.program_id(0)
    n = pl.num_programs(0)
    tm = x_ref.shape[0]

    def start_gather(ids_ref, slot):
        def body(c, carry):
            for u in range(DMA_UNROLL):
                r = c * DMA_UNROLL + u
                _row_copy(yb_hbm, ids_ref[0, r], ybuf.at[slot], r, sem.at[slot]).start()
            return carry
        lax.fori_loop(0, TOP_K * tm // DMA_UNROLL, body, 0)

    def wait_gather(slot):
        pltpu.make_async_copy(yb_hbm.at[pl.ds(0, TOP_K * tm), :], ybuf.at[slot], sem.at[slot]).wait()

    @pl.when(i == 0)
    def _():
        start_gather(pos_ref, 0)

    slot = i % 2
    wait_gather(slot)
    per = TOP_K * tm // COMBINE_GROUPS
    rows_per = tm // COMBINE_GROUPS
    for c in range(COMBINE_GROUPS):
        for r in range(c * per, (c + 1) * per):
            _row_copy(yb_hbm, pos_next_ref[0, r], ybuf.at[1 - slot], r, sem.at[1 - slot]).start()
        rows = slice(c * rows_per, (c + 1) * rows_per)
        rt = rt_ref[rows, :]
        f = (rt[:, 2:3] * ybuf[slot, rows, :]
             + rt[:, 3:4] * ybuf[slot, tm + c * rows_per:tm + (c + 1) * rows_per, :])
        xn = x_ref[rows, :] + mod_ref[5:6, :] * f
        xo_ref[rows, :] = _rms(xn) * gf_ref[...] if final else xn

    @pl.when(i == n - 1)
    def _():
        wait_gather(1 - slot)


def _combine(xall, route, mod, g_final, yb, pos, tiles, final):
    r, d = xall.shape
    tm = ROW_TILE
    nt = tiles.count
    last = nt - 1
    assert r == nt * tm

    def mod_idx(i):
        return (tiles.mod_row(i), 0, 0)

    row = lambda w: pl.BlockSpec((tm, w), lambda i: (i, 0))
    return pl.pallas_call(
        functools.partial(_combine_kernel, final=final),
        grid=(nt,),
        in_specs=[
            pl.BlockSpec((None, 1, TOP_K * tm), lambda i: (i, 0, 0), memory_space=pltpu.SMEM),
            pl.BlockSpec((None, 1, TOP_K * tm), lambda i: (jnp.minimum(i + 1, last), 0, 0),
                         memory_space=pltpu.SMEM),
            row(d), row(LANE),
            pl.BlockSpec((None, N_MOD, d), mod_idx),
            pl.BlockSpec((1, d), lambda i: (0, 0)),
            pl.BlockSpec(memory_space=pl.ANY),
        ],
        out_specs=row(d),
        out_shape=jax.ShapeDtypeStruct((r, d), F32),
        scratch_shapes=[pltpu.VMEM((2, TOP_K * tm, d), F32), pltpu.SemaphoreType.DMA((2,))],
        compiler_params=_cparams("arbitrary"),
        name="moe_combine",
    )(pos, pos, xall, route, mod, g_final, yb)


def _dispatch_plan(route, counts_f, n_tiles):
    n = route.shape[0]
    bm = MOE_BLOCK
    n_blocks = -(-(n * TOP_K + N_EXPERTS * (bm - 1)) // bm)
    counts = counts_f[:N_EXPERTS].astype(jnp.int32)
    padded = (counts + bm - 1) // bm * bm
    pend = jnp.cumsum(padded)
    seg_off = pend - padded
    eid = route[:, 0:TOP_K].astype(jnp.int32)
    rank = route[:, 4:4 + TOP_K].astype(jnp.int32)
    experts = jnp.arange(N_EXPERTS, dtype=jnp.int32)
    pos = jnp.sum(jnp.where(eid[..., None] == experts, seg_off, 0), axis=-1) + rank
    blk0 = jnp.arange(n_blocks, dtype=jnp.int32) * bm
    block_e = jnp.minimum(jnp.sum((blk0[:, None] >= pend[None, :]).astype(jnp.int32), axis=1), N_EXPERTS - 1)
    seg_end = seg_off + counts
    n_valid = jnp.clip(seg_end[block_e] - blk0, 0, bm).astype(jnp.int32)
    pos_tiles = pos.reshape(n_tiles, n // n_tiles, TOP_K).transpose(0, 2, 1).reshape(n_tiles, 1, -1)
    used = counts > 0
    last_blocks = jnp.sort(jnp.where(used, pend // bm - 1, n_blocks)).astype(jnp.int32)
    n_live = jnp.stack([pend[-1] // bm, jnp.sum(used)]).astype(jnp.int32)
    return block_e.astype(jnp.int32), n_valid, last_blocks, n_live, pos_tiles, n_blocks * bm


def _rope_tables(seq, ctx_rows):
    t = jnp.arange(seq - ctx_rows)
    inv = ROPE_BASE ** (-jnp.arange(0, ROT_AX, 2, dtype=F32) / ROT_AX)
    ang_r = (t // GRID_W).astype(F32)[:, None] * inv
    ang_c = (t % GRID_W).astype(F32)[:, None] * inv
    cos64 = jnp.concatenate([jnp.cos(ang_r)] * 2 + [jnp.cos(ang_c)] * 2, axis=1)
    sin64 = jnp.concatenate([-jnp.sin(ang_r), jnp.sin(ang_r), -jnp.sin(ang_c), jnp.sin(ang_c)], axis=1)
    cos = jnp.concatenate([jnp.ones((ctx_rows, DK), F32), cos64], axis=0)
    sin = jnp.concatenate([jnp.zeros((ctx_rows, DK), F32), sin64], axis=0)
    return jnp.tile(cos, (1, LANE // DK)), jnp.tile(sin, (1, LANE // DK))


def kernel(x, c, ctx, c_ctx, w_mod, b_mod, g_norm1, g_norm2, w_in, diff_lambda, g_diff_subln, w_gmlp_s, b_gmlp_s, g_gmlp_v, w_conv_m, b_conv_m, w_qkv_m, w_if_m, b_if_m, g_mlstm_norm, skip_m, w_branch, w_out, w_route_g, b_route_g, w_route_e, b_route_e, w_e_gate, w_e_up, w_e_down, g_final):
    n_batch, t_lat, d = x.shape
    ctx_rows = ctx.shape[1]
    depth = w_in.shape[0]
    seq = ctx_rows + t_lat
    assert d == D_MODEL and ctx_rows == ROW_TILE and t_lat % ROW_TILE == 0 and t_lat % GRID_W == 0
    tps = seq // ROW_TILE
    all_tiles = _Tiles(n_batch, tps, False)

    xall = jnp.concatenate([ctx, x], axis=1).reshape(n_batch * seq, d)
    mb = -(-(n_batch + 1) // 8) * 8
    c_all = jnp.zeros((mb, d), F32).at[:n_batch].set(c).at[n_batch].set(c_ctx)
    mod_all = _modulation(c_all, w_mod, b_mod)[:, :n_batch + 1].reshape(depth, n_batch + 1, N_MOD, d)
    cos, sin_signed = _rope_tables(seq, ctx_rows)
    w_in_b = w_in.astype(MXU_DT)

    for l in range(depth):
        lam_init = 0.8 - 0.6 * math.exp(-0.3 * l)
        mod = mod_all[l]
        last = l == depth - 1
        tiles = _Tiles(n_batch, tps, last)
        p, k_t = _inproj(xall, g_norm1[l][None], mod, cos, sin_signed, w_in_b, l, all_tiles)

        ya = _attention(p, k_t, diff_lambda[l], g_diff_subln[l][None], lam_init, tiles, ctx_rows)

        b_full = jnp.repeat(b_gmlp_s[l].T, W_B // G_B, axis=1)
        yb = _gmlp(p, g_gmlp_v[l][None], w_gmlp_s[l].astype(MXU_DT), b_full, tiles)

        w_if_pad = jnp.zeros((3 * W_C, LANE), F32).at[:, :4 * H_C].set(w_if_m[l]).astype(MXU_DT)
        b_if_pad = jnp.zeros((1, LANE), F32).at[0, :4 * H_C].set(b_if_m[l])
        xconv, q_m, k_m, v_m, gates_t = _mlstm_features(
            p, w_conv_m[l], b_conv_m[l][None], w_qkv_m[l].astype(MXU_DT), w_if_pad, b_if_pad, tps)
        gate_pack, r_t = _mlstm_gate_prep(gates_t, n_batch, seq)
        hf, hb = _mlstm_scan(q_m, k_m, v_m, gate_pack, r_t, n_batch, seq, ctx_rows)

        w_route = (jnp.zeros((d, LANE), F32).at[:, :N_GROUPS].set(w_route_g[l])
                   .at[:, N_GROUPS:N_GROUPS + N_EXPERTS].set(w_route_e[l]))
        w_route = jnp.stack(_split_bf16(w_route))
        b_route = (jnp.zeros((1, LANE), F32).at[0, :N_GROUPS].set(b_route_g[l])
                   .at[0, N_GROUPS:N_GROUPS + N_EXPERTS].set(b_route_e[l]))
        xall, route, counts = _merge(xall, p, ya, yb, hf, hb, xconv, g_mlstm_norm[l][None], skip_m[l][None],
                                         w_branch[l].astype(MXU_DT), w_out[l].astype(MXU_DT), mod,
                                         g_norm2[l][None], w_route, b_route, tiles)

        block_e, n_valid, fill_blocks, n_live, pos_tiles, n_slots = _dispatch_plan(route, counts[0], tiles.count)
        xs = _moe_dispatch(xall, g_norm2[l][None], mod, fill_blocks, n_live, pos_tiles, n_slots, tiles)
        y_sorted = _moe_experts(xs, block_e, n_valid, w_e_gate, w_e_up, w_e_down, l)
        xall = _combine(xall, route, mod, g_final[None], y_sorted, pos_tiles, tiles, last)

    return xall.reshape(n_batch, t_lat, d)
```

```python
import functools
import math
from typing import NamedTuple

import jax
import jax.numpy as jnp
from jax import lax
from jax.experimental import pallas as pl
from jax.experimental.pallas import tpu as pltpu

F32 = jnp.float32
MXU_DT = jnp.bfloat16
ACT_DT = jnp.bfloat16
HIGHEST = lax.Precision.HIGHEST

D_MODEL = 1024
N_MOD = 6
EPS = 1e-6
GRID_W = 64
H_A, DK = 4, 64
DV = 2 * DK
ROT_AX = DK // 2
ROPE_BASE = 10000.0
W_B, G_B, CHUNK_B = 512, 4, 128
H_C, DH_C, CONV_K, MLSTM_CHUNK = 4, 128, 3, 128
W_C = H_C * DH_C
N_GROUPS, EXP_PER_GROUP, TOP_K, D_EXPERT = 4, 8, 2, 512
N_EXPERTS = N_GROUPS * EXP_PER_GROUP
W_BR = 512

GATE_W = 3 * D_MODEL
P_GATE = 0
P_K = GATE_W
P_V = P_K + 512
P_XM = P_V + 512
P_Q = P_XM + 512
P_UV = P_Q + 512
P_Z = P_UV + 2 * W_B
IN_COLS = P_Z + W_C
ORIG_GATE0 = IN_COLS - GATE_W

LANE = 128
ROW_TILE = 256
PROJ_CHUNK = 512
MOE_BLOCK = 512
HALO = 16
DMA_UNROLL = 8
ATTN_KV_CHUNK = 256
COMBINE_GROUPS = 8
N_DMA_PRIORITIES = 2
LOG2_E = math.log2(math.e)
VMEM_LIMIT = 56 * 1024 * 1024
NEG_BIG = -1e30


class _Tiles(NamedTuple):
    n_batch: int
    tps: int
    skip_ctx: bool

    @property
    def per_seq(self):
        return self.tps - 1 if self.skip_ctx else self.tps

    @property
    def count(self):
        return self.n_batch * self.per_seq

    def full(self, i):
        if not self.skip_ctx:
            return i
        return (i // self.per_seq) * self.tps + 1 + i % self.per_seq

    def mod_row(self, i):
        if self.skip_ctx:
            return i // self.per_seq
        return jnp.where(i % self.tps == 0, self.n_batch, i // self.tps)


def _cparams(*sem):
    return pltpu.CompilerParams(dimension_semantics=sem, vmem_limit_bytes=VMEM_LIMIT)


def _rms(x):
    return x * lax.rsqrt(jnp.mean(x * x, axis=-1, keepdims=True) + EPS)


def _sigmoid(x):
    return 0.5 * jnp.tanh(0.5 * x) + 0.5


def _split_bf16(x):
    hi = x.astype(jnp.bfloat16)
    return hi, (x - hi.astype(F32)).astype(jnp.bfloat16)


def _mod_kernel(c_ref, w_ref, b_ref, o_ref):
    c = c_ref[...]
    s = c * jax.nn.sigmoid(c)
    o_ref[...] = jnp.dot(s, w_ref[...], preferred_element_type=F32, precision=HIGHEST) + b_ref[...]


def _modulation(c_all, w_mod, b_mod):
    n_layer = w_mod.shape[0]
    mb, d = c_all.shape
    tn = 1024
    return pl.pallas_call(
        _mod_kernel,
        grid=(n_layer, N_MOD * d // tn),
        in_specs=[
            pl.BlockSpec((mb, d), lambda l, j: (0, 0)),
            pl.BlockSpec((None, d, tn), lambda l, j: (l, 0, j)),
            pl.BlockSpec((None, 1, tn), lambda l, j: (l, 0, j)),
        ],
        out_specs=pl.BlockSpec((None, mb, tn), lambda l, j: (l, 0, j)),
        out_shape=jax.ShapeDtypeStruct((n_layer, mb, N_MOD * d), F32),
        compiler_params=_cparams("parallel", "parallel"),
        name="modulation",
    )(c_all, w_mod, b_mod.reshape(n_layer, 1, N_MOD * d))


def _rope(acc, cos, sin_signed):
    w = acc.shape[1]
    half = ROT_AX // 2
    lane = lax.broadcasted_iota(jnp.int32, acc.shape, 1)
    partner = jnp.where((lane & half) == 0, pltpu.roll(acc, w - half, 1), pltpu.roll(acc, half, 1))
    reps = w // cos.shape[1]
    return acc * jnp.tile(cos, (1, reps)) + partner * jnp.tile(sin_signed, (1, reps))


def _inproj_kernel(x_ref, g_ref, mod_ref, cos_ref, sin_ref, w_ref, o_ref, kt_ref):
    x = x_ref[...]
    h = (_rms(x) * g_ref[...] * (1.0 + mod_ref[1:2, :]) + mod_ref[0:1, :]).astype(MXU_DT)
    k_chunk = P_K // PROJ_CHUNK
    q_chunk = P_Q // PROJ_CHUNK
    chunks = range(IN_COLS // PROJ_CHUNK)
    cols = lambda c: slice(c * PROJ_CHUNK, (c + 1) * PROJ_CHUNK)

    def project(c):
        w0 = (ORIG_GATE0 + c * PROJ_CHUNK) % IN_COLS
        acc = jnp.dot(h, w_ref[:, w0:w0 + PROJ_CHUNK], preferred_element_type=F32)
        if c == k_chunk:
            acc = _rope(acc, cos_ref[...], sin_ref[...])
            kt_ref[...] = acc.T.astype(kt_ref.dtype)
        elif c == q_chunk:
            acc = _rope(acc, cos_ref[...], sin_ref[...]) * (DK ** -0.5 * LOG2_E)
        o_ref[:, cols(c)] = acc.astype(o_ref.dtype)

    for c in chunks:
        project(c)


def _inproj(xall, g1, mod, cos, sin_signed, w_in, layer, tiles):
    r, d = xall.shape
    tm = ROW_TILE
    tps = tiles.tps
    assert ORIG_GATE0 % PROJ_CHUNK == 0 and not tiles.skip_ctx

    def mod_idx(i):
        return (tiles.mod_row(i), 0, 0)

    return pl.pallas_call(
        _inproj_kernel,
        grid=(r // tm,),
        in_specs=[
            pl.BlockSpec((tm, d), lambda i: (i, 0)),
            pl.BlockSpec((1, d), lambda i: (0, 0)),
            pl.BlockSpec((None, N_MOD, d), mod_idx),
            pl.BlockSpec((tm, LANE), lambda i: (i % tps, 0)),
            pl.BlockSpec((tm, LANE), lambda i: (i % tps, 0)),
            pl.BlockSpec((None, d, IN_COLS), lambda i: (layer, 0, 0)),
        ],
        out_specs=[pl.BlockSpec((tm, IN_COLS), lambda i: (i, 0)),
                   pl.BlockSpec((PROJ_CHUNK, tm), lambda i: (0, i))],
        out_shape=[jax.ShapeDtypeStruct((r, IN_COLS), ACT_DT), jax.ShapeDtypeStruct((PROJ_CHUNK, r), ACT_DT)],
        compiler_params=_cparams("parallel"),
        name="inproj",
    )(xall, g1, mod, cos, sin_signed, w_in)


def _attn_kernel(dl_ref, g_ref, q_ref, kt_ref, v_ref, o_ref, vext_ref, s_ref, mx_ref, *,
                 lam_init, ctx_rows, has_ctx):
    tq = s_ref.shape[2]
    seq = kt_ref.shape[1]
    one_hot = lax.broadcasted_iota(jnp.int32, v_ref.shape, 1) == 0
    vext_ref[:, :DV] = v_ref[...]
    vext_ref[:, DV:] = jnp.where(one_hot, 1.0, 0.0).astype(vext_ref.dtype)

    dl = dl_ref[...]
    lam = (jnp.exp(jnp.sum(dl[0:1] * dl[1:2], keepdims=True))
           - jnp.exp(jnp.sum(dl[2:3] * dl[3:4], keepdims=True)) + lam_init)
    def chunk_list(kv_rows):
        kc = min(ATTN_KV_CHUNK, kv_rows)
        assert kv_rows % kc == 0
        return [slice(c * kc, (c + 1) * kc) for c in range(kv_rows // kc)]

    def score_pass(kv_rows, q_rows, buf):
        q = q_ref[q_rows, :]
        lane = lax.broadcasted_iota(jnp.int32, q.shape, 1)
        zero = jnp.zeros_like(q)
        qms = (jnp.where(lane < DK, q, zero), jnp.where(lane >= DK, q, zero))
        mruns = [None, None]
        for cols in chunk_list(kv_rows):
            for m in range(2):
                s = jnp.dot(qms[m], kt_ref[:, cols], preferred_element_type=F32)
                s_ref[buf, m, :, cols] = s
                for g in range(s.shape[1] // LANE):
                    blk = s[:, g * LANE:(g + 1) * LANE]
                    mruns[m] = blk if mruns[m] is None else jnp.maximum(mruns[m], blk)
            yield
        for m in range(2):
            mx = jnp.max(mruns[m], axis=-1, keepdims=True)
            mx_ref[buf, m] = jnp.broadcast_to(mx, mx_ref.shape[2:])
        yield

    def value_pass(kv_rows, out_rows, buf):
        mxs = [mx_ref[buf, m][:, 0:1] for m in range(2)]
        accs = [jnp.zeros((tq, 2 * DV), F32)] * 2
        for cols in chunk_list(kv_rows):
            for m in range(2):
                p = jnp.exp2(s_ref[buf, m, :, cols] - mxs[m])
                accs[m] = accs[m] + jnp.dot(p.astype(MXU_DT), vext_ref[cols, :], preferred_element_type=F32)
            yield
        o = (accs[0][:, :DV] * (1.0 / accs[0][:, DV:DV + 1])
             - accs[1][:, :DV] * (lam / accs[1][:, DV:DV + 1]))
        o_ref[out_rows, :] = (_rms(o) * g_ref[...] * (1.0 - lam_init)).astype(o_ref.dtype)
        yield

    def run(*passes):
        live = list(passes)
        while live:
            live = [g for g in live if next(g, "done") != "done"]

    out_off = 0
    if has_ctx:
        run(score_pass(ctx_rows, slice(0, ctx_rows), 0))
        run(value_pass(ctx_rows, slice(0, ctx_rows), 0))
        out_off = ctx_rows

    n_lat = (seq - ctx_rows) // tq
    q_at = lambda i: pl.ds(pl.multiple_of(ctx_rows + i * tq, tq), tq)
    o_at = lambda i: pl.ds(pl.multiple_of(out_off + i * tq, tq), tq)
    assert n_lat % 2 == 0
    run(score_pass(seq, q_at(0), 0))

    def tile_pair(j, carry):
        a = 2 * j
        run(score_pass(seq, q_at(a + 1), 1), value_pass(seq, o_at(a), 0))
        run(score_pass(seq, q_at(a + 2), 0), value_pass(seq, o_at(a + 1), 1))
        return carry

    lax.fori_loop(0, n_lat // 2 - 1, tile_pair, 0)
    run(score_pass(seq, q_at(n_lat - 1), 1), value_pass(seq, o_at(n_lat - 2), 0))
    run(value_pass(seq, o_at(n_lat - 1), 1))


def _attention(p, k_t, diff_lambda, g_sub, lam_init, tiles, ctx_rows):
    tq = ROW_TILE
    tps = tiles.tps
    seq = tps * tq
    out_rows = tiles.per_seq * tq
    assert ctx_rows == tq
    kern = functools.partial(_attn_kernel, lam_init=lam_init, ctx_rows=ctx_rows, has_ctx=not tiles.skip_ctx)
    return pl.pallas_call(
        kern,
        grid=(tiles.n_batch, H_A),
        in_specs=[
            pl.BlockSpec((4, DK), lambda b, h: (0, 0)),
            pl.BlockSpec((1, DV), lambda b, h: (0, 0)),
            pl.BlockSpec((seq, LANE), lambda b, h: (b, P_Q // LANE + h)),
            pl.BlockSpec((LANE, seq), lambda b, h: (h, b)),
            pl.BlockSpec((seq, LANE), lambda b, h: (b, P_V // LANE + h)),
        ],
        out_specs=pl.BlockSpec((out_rows, LANE), lambda b, h: (b, h)),
        out_shape=jax.ShapeDtypeStruct((tiles.count * tq, H_A * DV), ACT_DT),
        scratch_shapes=[pltpu.VMEM((seq, 2 * DV), MXU_DT), pltpu.VMEM((2, 2, tq, seq), F32),
                        pltpu.VMEM((2, 2, tq, LANE), F32)],
        compiler_params=_cparams("parallel", "parallel"),
        name="diff_attention",
    )(diff_lambda, g_sub, p, k_t, p)


def _gmlp_kernel(uv_ref, gv_ref, ws_ref, bs_ref, o_ref):
    uv = jax.nn.gelu(uv_ref[...])
    u = uv[:, :W_B].astype(F32)
    v = (_rms(uv[:, W_B:].astype(F32)) * gv_ref[...]).astype(MXU_DT)
    gw = W_B // G_B
    for c in range(uv.shape[0] // CHUNK_B):
        rows = slice(c * CHUNK_B, (c + 1) * CHUNK_B)
        for g in range(G_B):
            cols = slice(g * gw, (g + 1) * gw)
            mixed = jnp.dot(ws_ref[g], v[rows, cols], preferred_element_type=F32) + bs_ref[:, cols]
            o_ref[rows, cols] = (u[rows, cols] * mixed).astype(o_ref.dtype)


def _gmlp(p, g_v, w_s, b_full, tiles):
    tm = ROW_TILE
    r = tiles.count * tm
    return pl.pallas_call(
        _gmlp_kernel,
        grid=(tiles.count,),
        in_specs=[
            pl.BlockSpec((tm, 2 * W_B), lambda i: (tiles.full(i), P_UV // (2 * W_B))),
            pl.BlockSpec((1, W_B), lambda i: (0, 0)),
            pl.BlockSpec((G_B, CHUNK_B, CHUNK_B), lambda i: (0, 0, 0)),
            pl.BlockSpec((CHUNK_B, W_B), lambda i: (0, 0)),
        ],
        out_specs=pl.BlockSpec((tm, W_B), lambda i: (i, 0)),
        out_shape=jax.ShapeDtypeStruct((r, W_B), ACT_DT),
        compiler_params=_cparams("parallel"),
        name="gmlp",
    )(p, g_v, w_s, b_full)


def _mfeat_kernel(x_ref, prev_ref, next_ref, wc_ref, bc_ref, wqkv_ref, wif_ref, bif_ref,
                  xc_ref, q_ref, k_ref, v_ref, g_ref, *, tps):
    tm = x_ref.shape[0]
    j = pl.program_id(0) % tps
    seg_start = jnp.logical_or(j == 0, j == 1)
    seg_end = jnp.logical_or(j == 0, j == tps - 1)
    x = x_ref[...].astype(F32)
    prow = jnp.where(seg_start, 0.0, prev_ref[...].astype(F32)[HALO - 1:HALO, :])
    nrow = jnp.where(seg_end, 0.0, next_ref[...].astype(F32)[0:1, :])
    rid = lax.broadcasted_iota(jnp.int32, x.shape, 0)
    xp = jnp.where(rid == 0, prow, pltpu.roll(x, 1, 0))
    xn = jnp.where(rid == tm - 1, nrow, pltpu.roll(x, tm - 1, 0))
    xc = wc_ref[0:1, :] * xp + wc_ref[1:2, :] * x + wc_ref[2:3, :] * xn + bc_ref[...]
    xc = xc * _sigmoid(xc)
    xc_ref[...] = xc.astype(xc_ref.dtype)
    xcb = xc.astype(MXU_DT)
    xb = x_ref[...]
    qs, ks, vs = [], [], []
    for h in range(H_C):
        cols = slice(h * DH_C, (h + 1) * DH_C)
        qs.append(jnp.dot(xcb[:, cols], wqkv_ref[0, h], preferred_element_type=F32))
        ks.append(jnp.dot(xcb[:, cols], wqkv_ref[1, h], preferred_element_type=F32))
        vs.append(jnp.dot(xb[:, cols], wqkv_ref[2, h], preferred_element_type=F32))
    q = jnp.concatenate(qs, axis=1)
    k = jnp.concatenate(ks, axis=1)
    v = jnp.concatenate(vs, axis=1)
    qkv = jnp.concatenate([q, k, v], axis=1).astype(MXU_DT)
    gates = jnp.dot(qkv, wif_ref[...], preferred_element_type=F32) + bif_ref[...]
    g_ref[...] = gates.T[0:g_ref.shape[0], :]
    q_ref[...] = q.T.astype(q_ref.dtype)
    k_ref[...] = (k * (DH_C ** -0.5)).astype(k_ref.dtype)
    v_ref[...] = v.T.astype(v_ref.dtype)


def _mlstm_features(p, w_conv, b_conv, w_qkv, w_if_pad, b_if_pad, tps):
    r = p.shape[0]
    tm = ROW_TILE
    hpt = tm // HALO
    xm_blk = P_XM // W_C
    last_halo = r // HALO - 1
    kern = functools.partial(_mfeat_kernel, tps=tps)
    act = jax.ShapeDtypeStruct((r, W_C), ACT_DT)
    act_t = jax.ShapeDtypeStruct((W_C, r), ACT_DT)
    row_spec = pl.BlockSpec((tm, W_C), lambda i: (i, 0))
    col_spec = pl.BlockSpec((W_C, tm), lambda i: (0, i))
    return pl.pallas_call(
        kern,
        grid=(r // tm,),
        in_specs=[
            pl.BlockSpec((tm, W_C), lambda i: (i, xm_blk)),
            pl.BlockSpec((HALO, W_C), lambda i: (jnp.maximum(i * hpt - 1, 0), xm_blk)),
            pl.BlockSpec((HALO, W_C), lambda i: (jnp.minimum((i + 1) * hpt, last_halo), xm_blk)),
            pl.BlockSpec((CONV_K, W_C), lambda i: (0, 0)),
            pl.BlockSpec((1, W_C), lambda i: (0, 0)),
            pl.BlockSpec((3, H_C, DH_C, DH_C), lambda i: (0, 0, 0, 0)),
            pl.BlockSpec((3 * W_C, LANE), lambda i: (0, 0)),
            pl.BlockSpec((1, LANE), lambda i: (0, 0)),
        ],
        out_specs=[row_spec, col_spec, row_spec, col_spec,
                   pl.BlockSpec((2 * GATE_ROWS, tm), lambda i: (0, i))],
        out_shape=[act, act_t, act, act_t, jax.ShapeDtypeStruct((2 * GATE_ROWS, r), F32)],
        compiler_params=_cparams("parallel"),
        name="mlstm_features",
    )(p, p, p, w_conv, b_conv, w_qkv, w_if_pad, b_if_pad)


GATE_ROWS = 2 * H_C
GATE_PACK = ("r", "cmx", "bcum")


def _gate_prep_kernel(g_ref, pack_ref, rt_ref):
    seg = MLSTM_CHUNK
    gt_fwd = g_ref[0:GATE_ROWS, :]
    gt_bwd = g_ref[GATE_ROWS:2 * GATE_ROWS, :]
    n = gt_fwd.shape[1]
    lane = lax.broadcasted_iota(jnp.int32, gt_fwd.shape, 1)
    in_seg = lane & (seg - 1)

    def scan_both(xf, xb, op):
        sh = 1
        while sh < seg:
            xf = jnp.where(in_seg >= sh, op(xf, pltpu.roll(xf, sh, 1)), xf)
            xb = jnp.where(in_seg < seg - sh, op(xb, pltpu.roll(xb, n - sh, 1)), xb)
            sh *= 2
        return xf, xb

    bcum_f, bcum_b = scan_both(jax.nn.log_sigmoid(pltpu.roll(gt_fwd, H_C, 0)),
                               jax.nn.log_sigmoid(pltpu.roll(gt_bwd, H_C, 0)), jnp.add)
    r_f, r_b = gt_fwd - bcum_f, gt_bwd - bcum_b
    cmx_f, cmx_b = scan_both(r_f, r_b, jnp.maximum)
    pack_ref[...] = jnp.concatenate([r_f, cmx_f, bcum_f, r_b, cmx_b, bcum_b], axis=0)
    rt_ref[...] = jnp.concatenate([r_f, r_b, jnp.zeros((LANE - 2 * GATE_ROWS, n), F32)], axis=0).T


def _mlstm_gate_prep(gates_t, n_batch, seq):
    rows = 2 * len(GATE_PACK) * GATE_ROWS
    r = gates_t.shape[1]
    return pl.pallas_call(
        _gate_prep_kernel,
        grid=(n_batch,),
        in_specs=[pl.BlockSpec((2 * GATE_ROWS, seq), lambda b: (0, b))],
        out_specs=[pl.BlockSpec((rows, seq), lambda b: (0, b)), pl.BlockSpec((seq, LANE), lambda b: (b, 0))],
        out_shape=[jax.ShapeDtypeStruct((rows, r), F32), jax.ShapeDtypeStruct((r, LANE), F32)],
        compiler_params=_cparams("parallel"),
        name="mlstm_gate_prep",
    )(gates_t)


def _mlstm_gates(pack, m_ref, reverse):
    ln = pack.shape[1]
    part = {name: pack[i * GATE_ROWS:(i + 1) * GATE_ROWS] for i, name in enumerate(GATE_PACK)}
    end = slice(0, 1) if reverse else slice(ln - 1, ln)
    g = part["bcum"][:, end]
    r_max = part["cmx"][:, end]
    m_loc = g + r_max
    w_end = jnp.exp(part["r"] - r_max)
    m_old = m_ref[:, 0:1]
    m_new = jnp.maximum(g + m_old, m_loc)
    a_old = jnp.exp(g + m_old - m_new)
    a_new = jnp.exp(m_loc - m_new)
    mx = jnp.maximum(m_old, part["cmx"])
    w_inter = jnp.exp(m_old - mx)
    e_neg = jnp.exp(-(part["bcum"] + mx))
    s_id = lax.broadcasted_iota(jnp.int32, (ln, ln), 0)
    t_id = lax.broadcasted_iota(jnp.int32, (ln, ln), 1)
    keep = (s_id >= t_id) if reverse else (s_id <= t_id)
    return dict(mx=mx, w_inter=w_inter, e_neg=e_neg, w_end=w_end, a_old=a_old, a_new=a_new,
                m_new=m_new, keep=keep)


def _mscan_chunk(gf_ref, rf_ref, qf_ref, kf_ref, vf_ref, gb_ref, rb_ref, qb_ref, kb_ref, vb_ref,
                 hf_ref, hb_ref, cf_ref, mf_ref, cb_ref, mb_ref):
    ln = qf_ref.shape[1]
    pr = len(GATE_PACK) * GATE_ROWS
    dirs = ((_mlstm_gates(gf_ref[0:pr, :], mf_ref, False), qf_ref, kf_ref, vf_ref, cf_ref, hf_ref, rf_ref),
            (_mlstm_gates(gb_ref[pr:2 * pr, :], mb_ref, True), qb_ref, kb_ref, vb_ref, cb_ref, hb_ref, rb_ref))
    pairs = [(d, h) for d in range(2) for h in range(H_C)]
    ones_row = jnp.where(lax.broadcasted_iota(jnp.int32, (DH_C, ln), 0) == 0, 1.0, 0.0).astype(MXU_DT)
    head = lambda h: slice(h * DH_C, (h + 1) * DH_C)
    row = lambda h: slice(h, h + 1)

    ks = [dirs[d][2][:, head(h)] for d, h in pairs]
    qts = [dirs[d][1][head(h), :] for d, h in pairs]
    vexts = [jnp.concatenate([dirs[d][3][head(h), :], ones_row], axis=0) for d, h in pairs]
    cexts = [dirs[d][4][h] for d, h in pairs]
    qks = [jnp.dot(k, q_t, preferred_element_type=F32) for k, q_t in zip(ks, qts)]
    rhss = []
    for i, (d, h) in enumerate(pairs):
        g = dirs[d][0]
        r_col = dirs[d][6][:, d * GATE_ROWS + h:d * GATE_ROWS + h + 1]
        d_t = jnp.exp(jnp.where(g["keep"], r_col - g["mx"][row(h), :], NEG_BIG))
        wq = (qts[i].astype(F32) * g["w_inter"][row(h), :]).astype(MXU_DT)
        rhss.append(jnp.concatenate([(qks[i] * d_t).astype(MXU_DT), wq], axis=0))
    ress = [jnp.dot(jnp.concatenate([vexts[i], cexts[i].astype(MXU_DT)], axis=1), rhss[i],
                    preferred_element_type=F32) for i in range(len(pairs))]
    for i, (d, h) in enumerate(pairs):
        g = dirs[d][0]
        den = ress[i][DH_C:DH_C + 1, :]
        hh = ress[i][:DH_C, :] / jnp.maximum(jnp.abs(den), g["e_neg"][row(h), :])
        dirs[d][5][head(h), :] = hh.astype(dirs[d][5].dtype)
    clocs = []
    for i, (d, h) in enumerate(pairs):
        vw_t = (vexts[i].astype(F32) * dirs[d][0]["w_end"][row(h), :]).astype(MXU_DT)
        clocs.append(jnp.dot(vw_t, ks[i], preferred_element_type=F32))
    for i, (d, h) in enumerate(pairs):
        g = dirs[d][0]
        dirs[d][4][h] = g["a_old"][row(h), :] * cexts[i] + g["a_new"][row(h), :] * clocs[i]
    mf_ref[...] = jnp.broadcast_to(dirs[0][0]["m_new"], mf_ref.shape)
    mb_ref[...] = jnp.broadcast_to(dirs[1][0]["m_new"], mb_ref.shape)


def _mscan_kernel(g_ref, r_ref, q_ref, k_ref, v_ref, hf_ref, hb_ref, cf_ref, mf_ref, cb_ref, mb_ref, *, ncc):
    ln = MLSTM_CHUNK
    nck = q_ref.shape[1] // ln
    cf_ref[...] = jnp.zeros_like(cf_ref)
    cb_ref[...] = jnp.zeros_like(cb_ref)
    mf_ref[...] = jnp.zeros_like(mf_ref)
    mb_ref[...] = jnp.zeros_like(mb_ref)

    def chunk_views(c):
        at = pl.ds(pl.multiple_of(c * ln, ln), ln)
        return ((g_ref.at[:, at], r_ref.at[at, :], q_ref.at[:, at], k_ref.at[at, :], v_ref.at[:, at]),
                (hf_ref.at[:, at], hb_ref.at[:, at]))

    def trip(s, carry):
        ins_f, (hf, _) = chunk_views(s)
        ins_b, (_, hb) = chunk_views(jnp.where(s < ncc, ncc - 1 - s, nck - 1 + ncc - s))
        _mscan_chunk(*ins_f, *ins_b, hf, hb, cf_ref, mf_ref, cb_ref, mb_ref)
        return carry

    lax.fori_loop(0, nck, trip, 0)


def _mlstm_scan(q_t, k, v_t, gate_pack, r_t, n_batch, seq, ctx_rows):
    rows = lambda w: pl.BlockSpec((seq, w), lambda b: (b, 0))
    cols = lambda h: pl.BlockSpec((h, seq), lambda b: (0, b))
    out = jax.ShapeDtypeStruct(q_t.shape, ACT_DT)
    return pl.pallas_call(
        functools.partial(_mscan_kernel, ncc=ctx_rows // MLSTM_CHUNK),
        grid=(n_batch,),
        in_specs=[cols(gate_pack.shape[0]), rows(LANE), cols(W_C), rows(W_C), cols(W_C)],
        out_specs=[cols(W_C), cols(W_C)],
        out_shape=[out, out],
        scratch_shapes=[
            pltpu.VMEM((H_C, 2 * DH_C, DH_C), F32), pltpu.VMEM((8, LANE), F32),
            pltpu.VMEM((H_C, 2 * DH_C, DH_C), F32), pltpu.VMEM((8, LANE), F32),
        ],
        compiler_params=_cparams("parallel"),
        name="mlstm_scan",
    )(gate_pack, r_t, q_t, k, v_t)


def _route(logits, carry):
    lane = lax.broadcasted_iota(jnp.int32, logits.shape, 1).astype(F32)
    big = float(4 * LANE)
    gl = jnp.where(lane < N_GROUPS, logits, NEG_BIG)
    gmax = jnp.max(gl, axis=-1, keepdims=True)
    g_star = jnp.min(jnp.where(gl == gmax, lane, big), axis=-1, keepdims=True)
    p_g = 1.0 / jnp.sum(jnp.exp(gl - gmax), axis=-1, keepdims=True)
    e_lo = g_star * EXP_PER_GROUP + N_GROUPS
    in_group = jnp.logical_and(lane >= e_lo, lane < e_lo + EXP_PER_GROUP)
    el = jnp.where(in_group, logits, NEG_BIG)
    v1 = jnp.max(el, axis=-1, keepdims=True)
    i1 = jnp.min(jnp.where(el == v1, lane, big), axis=-1, keepdims=True)
    el2 = jnp.where(lane == i1, NEG_BIG, el)
    v2 = jnp.max(el2, axis=-1, keepdims=True)
    i2 = jnp.min(jnp.where(el2 == v2, lane, big), axis=-1, keepdims=True)
    e21 = jnp.exp(v2 - v1)
    w1 = p_g / (1.0 + e21)
    w2 = p_g * e21 / (1.0 + e21)
    e1 = i1 - N_GROUPS
    e2 = i2 - N_GROUPS
    oh1 = lane == e1
    oh2 = lane == e2
    both = jnp.where(jnp.logical_or(oh1, oh2), 1.0, 0.0)
    tm = logits.shape[0]
    lower = (lax.broadcasted_iota(jnp.int32, (tm, tm), 1) < lax.broadcasted_iota(jnp.int32, (tm, tm), 0))
    before = jnp.dot(jnp.where(lower, 1.0, 0.0).astype(MXU_DT), both.astype(MXU_DT),
                     preferred_element_type=F32) + carry
    rank1 = jnp.sum(jnp.where(oh1, before, 0.0), axis=-1, keepdims=True)
    rank2 = jnp.sum(jnp.where(oh2, before, 0.0), axis=-1, keepdims=True)
    out = jnp.where(lane == 0, e1, 0.0)
    out = jnp.where(lane == 1, e2, out)
    out = jnp.where(lane == 2, w1, out)
    out = jnp.where(lane == 3, w2, out)
    out = jnp.where(lane == 4, rank1, out)
    out = jnp.where(lane == 5, rank2, out)
    return out, carry + jnp.sum(both, axis=0, keepdims=True)


def _merge_kernel(x_ref, gate_ref, z_ref, ya_ref, yb_ref, hf_ref, hb_ref, xc_ref, gm_ref, sk_ref,
                  wbr_ref, wo_ref, mod_ref, g2_ref, wr_ref, br_ref, xo_ref, rt_ref, cnt_ref, carry_ref):
    @pl.when(pl.program_id(0) == 0)
    def _():
        carry_ref[...] = jnp.zeros_like(carry_ref)

    hs_t = hf_ref[...].astype(F32) + hb_ref[...].astype(F32)
    hn_t = []
    for h in range(H_C):
        part = hs_t[h * DH_C:(h + 1) * DH_C, :]
        hn_t.append(part * lax.rsqrt(jnp.mean(part * part, axis=0, keepdims=True) + EPS))
    hn = jnp.concatenate(hn_t, axis=0).T * gm_ref[...]
    yc = (hn + sk_ref[...] * xc_ref[...].astype(F32)) * _sigmoid(z_ref[...].astype(F32))
    ys = (ya_ref[...], yb_ref[...], yc.astype(MXU_DT))
    merged = None
    for i in range(3):
        gate = _sigmoid(gate_ref[:, i * D_MODEL:(i + 1) * D_MODEL].astype(MXU_DT))
        term = gate * jnp.dot(ys[i], wbr_ref[i], preferred_element_type=F32).astype(MXU_DT)
        merged = term if merged is None else merged + term
    y = jnp.dot(merged, wo_ref[...], preferred_element_type=F32)
    xn = x_ref[...] + mod_ref[2:3, :] * y
    xo_ref[...] = xn
    h2 = _rms(xn) * g2_ref[...] * (1.0 + mod_ref[4:5, :]) + mod_ref[3:4, :]
    h_hi, h_lo = _split_bf16(h2)
    logits = (jnp.dot(h_hi, wr_ref[0], preferred_element_type=F32)
              + jnp.dot(h_lo, wr_ref[0], preferred_element_type=F32)
              + jnp.dot(h_hi, wr_ref[1], preferred_element_type=F32)) + br_ref[...]
    route, carry = _route(logits, carry_ref[0:1, :])
    rt_ref[...] = route
    carry_ref[0:1, :] = carry
    cnt_ref[...] = jnp.broadcast_to(carry, cnt_ref.shape)


def _merge(xall, p, ya, yb, hf, hb, xconv, g_m, skip, w_br, w_o, mod, g2, w_route, b_route, tiles):
    d = xall.shape[1]
    tm = ROW_TILE
    r = tiles.count * tm

    def mod_idx(i):
        return (tiles.mod_row(i), 0, 0)

    row = lambda w: pl.BlockSpec((tm, w), lambda i: (i, 0))
    frow = lambda w: pl.BlockSpec((tm, w), lambda i: (tiles.full(i), 0))
    fcol = lambda w: pl.BlockSpec((w, tm), lambda i: (0, tiles.full(i)))
    full = lambda shape: pl.BlockSpec(shape, lambda i: (0,) * len(shape))
    return pl.pallas_call(
        _merge_kernel,
        grid=(tiles.count,),
        in_specs=[
            frow(d),
            pl.BlockSpec((tm, GATE_W), lambda i: (tiles.full(i), P_GATE // GATE_W)),
            pl.BlockSpec((tm, W_C), lambda i: (tiles.full(i), P_Z // W_C)),
            row(W_BR), row(W_BR), fcol(W_C), fcol(W_C), frow(W_C),
            full((1, W_C)), full((1, W_C)),
            full((3, W_BR, d)), full((d, d)),
            pl.BlockSpec((None, N_MOD, d), mod_idx),
            full((1, d)), full((2, d, LANE)), full((1, LANE)),
        ],
        out_specs=[row(d), row(LANE), full((8, LANE))],
        out_shape=[jax.ShapeDtypeStruct((r, d), F32), jax.ShapeDtypeStruct((r, LANE), F32),
                   jax.ShapeDtypeStruct((8, LANE), F32)],
        scratch_shapes=[pltpu.VMEM((8, LANE), F32)],
        compiler_params=_cparams("arbitrary"),
        name="merge_route",
    )(xall, p, p, ya, yb, hf, hb, xconv, g_m, skip, w_br, w_o, mod, g2, w_route, b_route)


def _row_copy(src_hbm, src_row, dst_ref, dst_row, sem):
    return pltpu.make_async_copy(src_hbm.at[pl.ds(src_row, 1), :], dst_ref.at[pl.ds(dst_row, 1), :], sem)


def _dispatch_kernel(fill_ref, nlive_ref, pos_ref, x_ref, g2_ref, mod_ref, xs_hbm, hbuf, zbuf, zsem, sem):
    i = pl.program_id(0)
    n = pl.num_programs(0)
    tm = x_ref.shape[0]
    slot = i % 2
    bm = zbuf.shape[0]
    n_blocks = xs_hbm.shape[0] // bm

    @pl.when(i == 0)
    def _():
        zbuf[...] = jnp.zeros_like(zbuf)

        def fill_block(b):
            return pltpu.make_async_copy(zbuf, xs_hbm.at[pl.ds(b * bm, bm), :], zsem)

        def fill_last(k, carry):
            fill_block(fill_ref[k]).start()
            return carry

        def fill_unused(b, carry):
            fill_block(b).start()
            return carry

        def wait_fill(k, carry):
            fill_block(0).wait()
            return carry

        n_live = nlive_ref[0]
        n_used = nlive_ref[1]
        lax.fori_loop(0, n_used, fill_last, 0)
        lax.fori_loop(n_live, n_blocks, fill_unused, 0)
        lax.fori_loop(0, n_used + n_blocks - n_live, wait_fill, 0)

    hbuf[slot] = _rms(x_ref[...]) * g2_ref[...] * (1.0 + mod_ref[4:5, :]) + mod_ref[3:4, :]

    for r in range(tm):
        for j in range(TOP_K):
            pltpu.make_async_copy(hbuf.at[slot, pl.ds(r, 1), :], xs_hbm.at[pl.ds(pos_ref[0, j * tm + r], 1), :],
                                  sem.at[slot]).start(priority=j % N_DMA_PRIORITIES)

    def wait_tile(s):
        pltpu.make_async_copy(hbuf.at[s], xs_hbm.at[pl.ds(0, tm), :], sem.at[s]).wait()
        pltpu.make_async_copy(hbuf.at[s], xs_hbm.at[pl.ds(0, tm), :], sem.at[s]).wait()

    @pl.when(i > 0)
    def _():
        wait_tile(1 - slot)

    @pl.when(i == n - 1)
    def _():
        wait_tile(slot)


def _moe_dispatch(xall, g2, mod, fill_blocks, n_live, pos_tiles, n_slots, tiles):
    r, d = xall.shape
    tm = ROW_TILE
    assert TOP_K == 2 and r == tiles.count * tm

    def mod_idx(i, fb, nl):
        return (tiles.mod_row(i), 0, 0)

    grid_spec = pltpu.PrefetchScalarGridSpec(
        num_scalar_prefetch=2,
        grid=(r // tm,),
        in_specs=[
            pl.BlockSpec((None, 1, TOP_K * tm), lambda i, fb, nl: (i, 0, 0), memory_space=pltpu.SMEM),
            pl.BlockSpec((tm, d), lambda i, fb, nl: (i, 0)),
            pl.BlockSpec((1, d), lambda i, fb, nl: (0, 0)),
            pl.BlockSpec((None, N_MOD, d), mod_idx),
        ],
        out_specs=pl.BlockSpec(memory_space=pl.ANY),
        scratch_shapes=[pltpu.VMEM((2, tm, d), F32), pltpu.VMEM((MOE_BLOCK, d), F32),
                        pltpu.SemaphoreType.DMA(()), pltpu.SemaphoreType.DMA((2,))],
    )
    return pl.pallas_call(
        _dispatch_kernel,
        grid_spec=grid_spec,
        out_shape=jax.ShapeDtypeStruct((n_slots, d), F32),
        compiler_params=_cparams("arbitrary"),
        name="moe_dispatch",
    )(fill_blocks, n_live, pos_tiles, xall, g2, mod)


def _moe_kernel(be_ref, nv_ref, x_ref, wg_ref, wu_ref, wd_ref, y_ref, wg_mx, wu_mx, wd_mx):
    i = pl.program_id(0)
    n_valid = nv_ref[i]

    @pl.when(jnp.logical_or(i == 0, be_ref[i] != be_ref[jnp.maximum(i - 1, 0)]))
    def _():
        wg_mx[...] = wg_ref[...].astype(wg_mx.dtype)
        wu_mx[...] = wu_ref[...].astype(wu_mx.dtype)
        wd_mx[...] = wd_ref[...].astype(wd_mx.dtype)

    @pl.when(n_valid > 0)
    def _():
        x = x_ref[...].astype(MXU_DT)
        a = jnp.dot(x, wg_mx[...], preferred_element_type=F32)
        u = jnp.dot(x, wu_mx[...], preferred_element_type=F32)
        act = (a * _sigmoid(a) * u).astype(MXU_DT)
        y_ref[...] = jnp.dot(act, wd_mx[...], preferred_element_type=F32)

    @pl.when(n_valid <= 0)
    def _():
        y_ref[...] = jnp.zeros_like(y_ref)


def _moe_experts(xs, block_e, n_valid, w_gate, w_up, w_down, layer):
    n_slots, d = xs.shape
    bm = MOE_BLOCK
    grid_spec = pltpu.PrefetchScalarGridSpec(
        num_scalar_prefetch=2,
        grid=(n_slots // bm,),
        in_specs=[
            pl.BlockSpec((bm, d), lambda i, be, nv: (i, 0)),
            pl.BlockSpec((None, None, d, D_EXPERT), lambda i, be, nv: (layer, be[i], 0, 0)),
            pl.BlockSpec((None, None, d, D_EXPERT), lambda i, be, nv: (layer, be[i], 0, 0)),
            pl.BlockSpec((None, None, D_EXPERT, d), lambda i, be, nv: (layer, be[i], 0, 0)),
        ],
        out_specs=pl.BlockSpec((bm, d), lambda i, be, nv: (i, 0)),
        scratch_shapes=[pltpu.VMEM((d, D_EXPERT), MXU_DT), pltpu.VMEM((d, D_EXPERT), MXU_DT),
                        pltpu.VMEM((D_EXPERT, d), MXU_DT)],
    )
    return pl.pallas_call(
        _moe_kernel,
        grid_spec=grid_spec,
        out_shape=jax.ShapeDtypeStruct((n_slots, d), F32),
        compiler_params=_cparams("arbitrary"),
        name="moe_experts",
    )(block_e, n_valid, xs, w_gate, w_up, w_down)


def _combine_kernel(pos_ref, pos_next_ref, x_ref, rt_ref, mod_ref, gf_ref, yb_hbm, xo_ref, ybuf, sem, *, final):
    i = pl.program_id(0)
    n = pl.num_programs(0)
    tm = x_ref.shape[0]

    def start_gather(ids_ref, slot):
        def body(c, carry):
            for u in range(DMA_UNROLL):
                r = c * DMA_UNROLL + u
                _row_copy(yb_hbm, ids_ref[0, r], ybuf.at[slot], r, sem.at[slot]).start()
            return carry
        lax.fori_loop(0, TOP_K * tm // DMA_UNROLL, body, 0)

    def wait_gather(slot):
        pltpu.make_async_copy(yb_hbm.at[pl.ds(0, TOP_K * tm), :], ybuf.at[slot], sem.at[slot]).wait()

    @pl.when(i == 0)
    def _():
        start_gather(pos_ref, 0)

    slot = i % 2
    wait_gather(slot)
    per = TOP_K * tm // COMBINE_GROUPS
    rows_per = tm // COMBINE_GROUPS
    for c in range(COMBINE_GROUPS):
        for r in range(c * per, (c + 1) * per):
            _row_copy(yb_hbm, pos_next_ref[0, r], ybuf.at[1 - slot], r,
                      sem.at[1 - slot]).start(priority=r % N_DMA_PRIORITIES)
        rows = slice(c * rows_per, (c + 1) * rows_per)
        rt = rt_ref[rows, :]
        f = (rt[:, 2:3] * ybuf[slot, rows, :]
             + rt[:, 3:4] * ybuf[slot, tm + c * rows_per:tm + (c + 1) * rows_per, :])
        xn = x_ref[rows, :] + mod_ref[5:6, :] * f
        xo_ref[rows, :] = _rms(xn) * gf_ref[...] if final else xn

    @pl.when(i == n - 1)
    def _():
        wait_gather(1 - slot)


def _combine(xall, route, mod, g_final, yb, pos, tiles, final):
    r, d = xall.shape
    tm = ROW_TILE
    nt = tiles.count
    last = nt - 1
    assert r == nt * tm

    def mod_idx(i):
        return (tiles.mod_row(i), 0, 0)

    row = lambda w: pl.BlockSpec((tm, w), lambda i: (i, 0))
    return pl.pallas_call(
        functools.partial(_combine_kernel, final=final),
        grid=(nt,),
        in_specs=[
            pl.BlockSpec((None, 1, TOP_K * tm), lambda i: (i, 0, 0), memory_space=pltpu.SMEM),
            pl.BlockSpec((None, 1, TOP_K * tm), lambda i: (jnp.minimum(i + 1, last), 0, 0),
                         memory_space=pltpu.SMEM),
            row(d), row(LANE),
            pl.BlockSpec((None, N_MOD, d), mod_idx),
            pl.BlockSpec((1, d), lambda i: (0, 0)),
            pl.BlockSpec(memory_space=pl.ANY),
        ],
        out_specs=row(d),
        out_shape=jax.ShapeDtypeStruct((r, d), F32),
        scratch_shapes=[pltpu.VMEM((2, TOP_K * tm, d), F32), pltpu.SemaphoreType.DMA((2,))],
        compiler_params=_cparams("arbitrary"),
        name="moe_combine",
    )(pos, pos, xall, route, mod, g_final, yb)


def _dispatch_plan(route, counts_f, n_tiles):
    n = route.shape[0]
    bm = MOE_BLOCK
    n_blocks = -(-(n * TOP_K + N_EXPERTS * (bm - 1)) // bm)
    counts = counts_f[:N_EXPERTS].astype(jnp.int32)
    padded = (counts + bm - 1) // bm * bm
    pend = jnp.cumsum(padded)
    seg_off = pend - padded
    eid = route[:, 0:TOP_K].astype(jnp.int32)
    rank = route[:, 4:4 + TOP_K].astype(jnp.int32)
    experts = jnp.arange(N_EXPERTS, dtype=jnp.int32)
    pos = jnp.sum(jnp.where(eid[..., None] == experts, seg_off, 0), axis=-1) + rank
    blk0 = jnp.arange(n_blocks, dtype=jnp.int32) * bm
    block_e = jnp.minimum(jnp.sum((blk0[:, None] >= pend[None, :]).astype(jnp.int32), axis=1), N_EXPERTS - 1)
    seg_end = seg_off + counts
    n_valid = jnp.clip(seg_end[block_e] - blk0, 0, bm).astype(jnp.int32)
    pos_tiles = pos.reshape(n_tiles, n // n_tiles, TOP_K).transpose(0, 2, 1).reshape(n_tiles, 1, -1)
    used = counts > 0
    last_blocks = jnp.sort(jnp.where(used, pend // bm - 1, n_blocks)).astype(jnp.int32)
    n_live = jnp.stack([pend[-1] // bm, jnp.sum(used)]).astype(jnp.int32)
    return block_e.astype(jnp.int32), n_valid, last_blocks, n_live, pos_tiles, n_blocks * bm


def _rope_tables(seq, ctx_rows):
    t = jnp.arange(seq - ctx_rows)
    inv = ROPE_BASE ** (-jnp.arange(0, ROT_AX, 2, dtype=F32) / ROT_AX)
    ang_r = (t // GRID_W).astype(F32)[:, None] * inv
    ang_c = (t % GRID_W).astype(F32)[:, None] * inv
    cos64 = jnp.concatenate([jnp.cos(ang_r)] * 2 + [jnp.cos(ang_c)] * 2, axis=1)
    sin64 = jnp.concatenate([-jnp.sin(ang_r), jnp.sin(ang_r), -jnp.sin(ang_c), jnp.sin(ang_c)], axis=1)
    cos = jnp.concatenate([jnp.ones((ctx_rows, DK), F32), cos64], axis=0)
    sin = jnp.concatenate([jnp.zeros((ctx_rows, DK), F32), sin64], axis=0)
    return jnp.tile(cos, (1, LANE // DK)), jnp.tile(sin, (1, LANE // DK))


def kernel(x, c, ctx, c_ctx, w_mod, b_mod, g_norm1, g_norm2, w_in, diff_lambda, g_diff_subln, w_gmlp_s, b_gmlp_s, g_gmlp_v, w_conv_m, b_conv_m, w_qkv_m, w_if_m, b_if_m, g_mlstm_norm, skip_m, w_branch, w_out, w_route_g, b_route_g, w_route_e, b_route_e, w_e_gate, w_e_up, w_e_down, g_final):
    n_batch, t_lat, d = x.shape
    ctx_rows = ctx.shape[1]
    depth = w_in.shape[0]
    seq = ctx_rows + t_lat
    assert d == D_MODEL and ctx_rows == ROW_TILE and t_lat % ROW_TILE == 0 and t_lat % GRID_W == 0
    tps = seq // ROW_TILE
    all_tiles = _Tiles(n_batch, tps, False)

    xall = jnp.concatenate([ctx, x], axis=1).reshape(n_batch * seq, d)
    mb = -(-(n_batch + 1) // 8) * 8
    c_all = jnp.zeros((mb, d), F32).at[:n_batch].set(c).at[n_batch].set(c_ctx)
    mod_all = _modulation(c_all, w_mod, b_mod)[:, :n_batch + 1].reshape(depth, n_batch + 1, N_MOD, d)
    cos, sin_signed = _rope_tables(seq, ctx_rows)
    w_in_b = w_in.astype(MXU_DT)

    for l in range(depth):
        lam_init = 0.8 - 0.6 * math.exp(-0.3 * l)
        mod = mod_all[l]
        last = l == depth - 1
        tiles = _Tiles(n_batch, tps, last)
        p, k_t = _inproj(xall, g_norm1[l][None], mod, cos, sin_signed, w_in_b, l, all_tiles)

        ya = _attention(p, k_t, diff_lambda[l], g_diff_subln[l][None], lam_init, tiles, ctx_rows)

        b_full = jnp.repeat(b_gmlp_s[l].T, W_B // G_B, axis=1)
        yb = _gmlp(p, g_gmlp_v[l][None], w_gmlp_s[l].astype(MXU_DT), b_full, tiles)

        w_if_pad = jnp.zeros((3 * W_C, LANE), F32).at[:, :4 * H_C].set(w_if_m[l]).astype(MXU_DT)
        b_if_pad = jnp.zeros((1, LANE), F32).at[0, :4 * H_C].set(b_if_m[l])
        xconv, q_m, k_m, v_m, gates_t = _mlstm_features(
            p, w_conv_m[l], b_conv_m[l][None], w_qkv_m[l].astype(MXU_DT), w_if_pad, b_if_pad, tps)
        gate_pack, r_t = _mlstm_gate_prep(gates_t, n_batch, seq)
        hf, hb = _mlstm_scan(q_m, k_m, v_m, gate_pack, r_t, n_batch, seq, ctx_rows)

        w_route = (jnp.zeros((d, LANE), F32).at[:, :N_GROUPS].set(w_route_g[l])
                   .at[:, N_GROUPS:N_GROUPS + N_EXPERTS].set(w_route_e[l]))
        w_route = jnp.stack(_split_bf16(w_route))
        b_route = (jnp.zeros((1, LANE), F32).at[0, :N_GROUPS].set(b_route_g[l])
                   .at[0, N_GROUPS:N_GROUPS + N_EXPERTS].set(b_route_e[l]))
        xall, route, counts = _merge(xall, p, ya, yb, hf, hb, xconv, g_mlstm_norm[l][None], skip_m[l][None],
                                         w_branch[l].astype(MXU_DT), w_out[l].astype(MXU_DT), mod,
                                         g_norm2[l][None], w_route, b_route, tiles)

        block_e, n_valid, fill_blocks, n_live, pos_tiles, n_slots = _dispatch_plan(route, counts[0], tiles.count)
        xs = _moe_dispatch(xall, g_norm2[l][None], mod, fill_blocks, n_live, pos_tiles, n_slots, tiles)
        y_sorted = _moe_experts(xs, block_e, n_valid, w_e_gate, w_e_up, w_e_down, l)
        xall = _combine(xall, route, mod, g_final[None], y_sorted, pos_tiles, tiles, last)

    return xall.reshape(n_batch, t_lat, d)
```

```python
import functools
import math
from typing import NamedTuple

import jax
import jax.numpy as jnp
from jax import lax
from jax.experimental import pallas as pl
from jax.experimental.pallas import tpu as pltpu

F32 = jnp.float32
MXU_DT = jnp.bfloat16
ACT_DT = jnp.bfloat16
HIGHEST = lax.Precision.HIGHEST

D_MODEL = 1024
N_MOD = 6
EPS = 1e-6
GRID_W = 64
H_A, DK = 4, 64
DV = 2 * DK
ROT_AX = DK // 2
ROPE_BASE = 10000.0
W_B, G_B, CHUNK_B = 512, 4, 128
H_C, DH_C, CONV_K, MLSTM_CHUNK = 4, 128, 3, 128
W_C = H_C * DH_C
N_GROUPS, EXP_PER_GROUP, TOP_K, D_EXPERT = 4, 8, 2, 512
N_EXPERTS = N_GROUPS * EXP_PER_GROUP
W_BR = 512

GATE_W = 3 * D_MODEL
P_GATE = 0
P_K = GATE_W
P_V = P_K + 512
P_XM = P_V + 512
P_Q = P_XM + 512
P_UV = P_Q + 512
P_Z = P_UV + 2 * W_B
IN_COLS = P_Z + W_C
ORIG_GATE0 = IN_COLS - GATE_W

LANE = 128
ROW_TILE = 256
PROJ_CHUNK = 512
MOE_BLOCK = 512
HALO = 16
DMA_UNROLL = 8
ATTN_KV_CHUNK = 256
COMBINE_GROUPS = 8
N_DMA_PRIORITIES = 2
LOG2_E = math.log2(math.e)
VMEM_LIMIT = 56 * 1024 * 1024
NEG_BIG = -1e30


class _Tiles(NamedTuple):
    n_batch: int
    tps: int
    skip_ctx: bool

    @property
    def per_seq(self):
        return self.tps - 1 if self.skip_ctx else self.tps

    @property
    def count(self):
        return self.n_batch * self.per_seq

    def full(self, i):
        if not self.skip_ctx:
            return i
        return (i // self.per_seq) * self.tps + 1 + i % self.per_seq

    def mod_row(self, i):
        if self.skip_ctx:
            return i // self.per_seq
        return jnp.where(i % self.tps == 0, self.n_batch, i // self.tps)


def _cparams(*sem):
    return pltpu.CompilerParams(dimension_semantics=sem, vmem_limit_bytes=VMEM_LIMIT)


def _rms(x):
    return x * lax.rsqrt(jnp.mean(x * x, axis=-1, keepdims=True) + EPS)


def _sigmoid(x):
    return 0.5 * jnp.tanh(0.5 * x) + 0.5


def _split_bf16(x):
    hi = x.astype(jnp.bfloat16)
    return hi, (x - hi.astype(F32)).astype(jnp.bfloat16)


def _mod_kernel(c_ref, w_ref, b_ref, o_ref):
    c = c_ref[...]
    s = c * jax.nn.sigmoid(c)
    o_ref[...] = jnp.dot(s, w_ref[...], preferred_element_type=F32, precision=HIGHEST) + b_ref[...]


def _modulation(c_all, w_mod, b_mod):
    n_layer = w_mod.shape[0]
    mb, d = c_all.shape
    tn = 1024
    return pl.pallas_call(
        _mod_kernel,
        grid=(n_layer, N_MOD * d // tn),
        in_specs=[
            pl.BlockSpec((mb, d), lambda l, j: (0, 0)),
            pl.BlockSpec((None, d, tn), lambda l, j: (l, 0, j)),
            pl.BlockSpec((None, 1, tn), lambda l, j: (l, 0, j)),
        ],
        out_specs=pl.BlockSpec((None, mb, tn), lambda l, j: (l, 0, j)),
        out_shape=jax.ShapeDtypeStruct((n_layer, mb, N_MOD * d), F32),
        compiler_params=_cparams("parallel", "parallel"),
        name="modulation",
    )(c_all, w_mod, b_mod.reshape(n_layer, 1, N_MOD * d))


def _rope(acc, cos, sin_signed):
    w = acc.shape[1]
    half = ROT_AX // 2
    lane = lax.broadcasted_iota(jnp.int32, acc.shape, 1)
    partner = jnp.where((lane & half) == 0, pltpu.roll(acc, w - half, 1), pltpu.roll(acc, half, 1))
    reps = w // cos.shape[1]
    return acc * jnp.tile(cos, (1, reps)) + partner * jnp.tile(sin_signed, (1, reps))


def _inproj_first_kernel(ctx_ref, lat_ref, g_ref, mod_ref, cos_ref, sin_ref, w_ref, o_ref, kt_ref, xo_ref, *, tps):
    is_ctx = pl.program_id(0) % tps == 0
    xo_ref[...] = jnp.where(is_ctx, ctx_ref[...], lat_ref[...])
    _inproj_kernel(xo_ref, g_ref, mod_ref, cos_ref, sin_ref, w_ref, o_ref, kt_ref)


def _inproj_kernel(x_ref, g_ref, mod_ref, cos_ref, sin_ref, w_ref, o_ref, kt_ref):
    x = x_ref[...]
    h = (_rms(x) * g_ref[...] * (1.0 + mod_ref[1:2, :]) + mod_ref[0:1, :]).astype(MXU_DT)
    k_chunk = P_K // PROJ_CHUNK
    q_chunk = P_Q // PROJ_CHUNK
    chunks = range(IN_COLS // PROJ_CHUNK)
    cols = lambda c: slice(c * PROJ_CHUNK, (c + 1) * PROJ_CHUNK)

    def project(c):
        w0 = (ORIG_GATE0 + c * PROJ_CHUNK) % IN_COLS
        acc = jnp.dot(h, w_ref[:, w0:w0 + PROJ_CHUNK], preferred_element_type=F32)
        if c == k_chunk:
            acc = _rope(acc, cos_ref[...], sin_ref[...])
            kt_ref[...] = acc.T.astype(kt_ref.dtype)
        elif c == q_chunk:
            acc = _rope(acc, cos_ref[...], sin_ref[...]) * (DK ** -0.5 * LOG2_E)
        o_ref[:, cols(c)] = acc.astype(o_ref.dtype)

    for c in chunks:
        project(c)


def _inproj(xall, g1, mod, cos, sin_signed, w_in, layer, tiles):
    tm = ROW_TILE
    tps = tiles.tps
    r = tiles.count * tm
    assert ORIG_GATE0 % PROJ_CHUNK == 0 and not tiles.skip_ctx
    first = isinstance(xall, tuple)
    d = xall[0].shape[1] if first else xall.shape[1]

    def mod_idx(i):
        return (tiles.mod_row(i), 0, 0)

    row = pl.BlockSpec((tm, d), lambda i: (i, 0))
    if first:
        x_specs = [pl.BlockSpec((tm, d), lambda i: (i // tps, 0)),
                   pl.BlockSpec((tm, d), lambda i: ((i // tps) * (tps - 1) + jnp.maximum(i % tps - 1, 0), 0))]
        x_args = list(xall)
        kern = functools.partial(_inproj_first_kernel, tps=tps)
    else:
        x_specs, x_args, kern = [row], [xall], _inproj_kernel
    return pl.pallas_call(
        kern,
        grid=(r // tm,),
        in_specs=x_specs + [
            pl.BlockSpec((1, d), lambda i: (0, 0)),
            pl.BlockSpec((None, N_MOD, d), mod_idx),
            pl.BlockSpec((tm, LANE), lambda i: (i % tps, 0)),
            pl.BlockSpec((tm, LANE), lambda i: (i % tps, 0)),
            pl.BlockSpec((None, d, IN_COLS), lambda i: (layer, 0, 0)),
        ],
        out_specs=[pl.BlockSpec((tm, IN_COLS), lambda i: (i, 0)),
                   pl.BlockSpec((PROJ_CHUNK, tm), lambda i: (0, i))] + ([row] if first else []),
        out_shape=[jax.ShapeDtypeStruct((r, IN_COLS), ACT_DT), jax.ShapeDtypeStruct((PROJ_CHUNK, r), ACT_DT)]
        + ([jax.ShapeDtypeStruct((r, d), F32)] if first else []),
        compiler_params=_cparams("parallel"),
        name="inproj",
    )(*x_args, g1, mod, cos, sin_signed, w_in)


def _attn_kernel(dl_ref, g_ref, q_ref, kt_ref, v_ref, o_ref, vext_ref, s_ref, mx_ref, *,
                 lam_init, ctx_rows, has_ctx):
    tq = s_ref.shape[2]
    seq = kt_ref.shape[1]
    one_hot = lax.broadcasted_iota(jnp.int32, v_ref.shape, 1) == 0
    vext_ref[:, :DV] = v_ref[...]
    vext_ref[:, DV:] = jnp.where(one_hot, 1.0, 0.0).astype(vext_ref.dtype)

    dl = dl_ref[...]
    lam = (jnp.exp(jnp.sum(dl[0:1] * dl[1:2], keepdims=True))
           - jnp.exp(jnp.sum(dl[2:3] * dl[3:4], keepdims=True)) + lam_init)
    def chunk_list(kv_rows):
        kc = min(ATTN_KV_CHUNK, kv_rows)
        assert kv_rows % kc == 0
        return [slice(c * kc, (c + 1) * kc) for c in range(kv_rows // kc)]

    def score_pass(kv_rows, q_rows, buf):
        q = q_ref[q_rows, :]
        lane = lax.broadcasted_iota(jnp.int32, q.shape, 1)
        zero = jnp.zeros_like(q)
        qms = (jnp.where(lane < DK, q, zero), jnp.where(lane >= DK, q, zero))
        mruns = [None, None]
        for cols in chunk_list(kv_rows):
            for m in range(2):
                s = jnp.dot(qms[m], kt_ref[:, cols], preferred_element_type=F32)
                s_ref[buf, m, :, cols] = s
                for g in range(s.shape[1] // LANE):
                    blk = s[:, g * LANE:(g + 1) * LANE]
                    mruns[m] = blk if mruns[m] is None else jnp.maximum(mruns[m], blk)
            yield
        for m in range(2):
            mx = jnp.max(mruns[m], axis=-1, keepdims=True)
            mx_ref[buf, m] = jnp.broadcast_to(mx, mx_ref.shape[2:])
        yield

    def value_pass(kv_rows, out_rows, buf):
        mxs = [mx_ref[buf, m][:, 0:1] for m in range(2)]
        accs = [jnp.zeros((tq, 2 * DV), F32)] * 2
        for cols in chunk_list(kv_rows):
            for m in range(2):
                p = jnp.exp2(s_ref[buf, m, :, cols] - mxs[m])
                accs[m] = accs[m] + jnp.dot(p.astype(MXU_DT), vext_ref[cols, :], preferred_element_type=F32)
            yield
        o = (accs[0][:, :DV] * (1.0 / accs[0][:, DV:DV + 1])
             - accs[1][:, :DV] * (lam / accs[1][:, DV:DV + 1]))
        o_ref[out_rows, :] = (_rms(o) * g_ref[...] * (1.0 - lam_init)).astype(o_ref.dtype)
        yield

    def run(*passes):
        live = list(passes)
        while live:
            live = [g for g in live if next(g, "done") != "done"]

    n_lat = (seq - ctx_rows) // tq
    q_at = lambda i: pl.ds(pl.multiple_of(ctx_rows + i * tq, tq), tq)
    assert n_lat % 2 == 0
    out_off = 0
    if has_ctx:
        out_off = ctx_rows
        run(score_pass(ctx_rows, slice(0, ctx_rows), 1))
        run(score_pass(seq, q_at(0), 0), value_pass(ctx_rows, slice(0, ctx_rows), 1))
    else:
        run(score_pass(seq, q_at(0), 0))
    o_at = lambda i: pl.ds(pl.multiple_of(out_off + i * tq, tq), tq)

    def tile_pair(j, carry):
        a = 2 * j
        run(score_pass(seq, q_at(a + 1), 1), value_pass(seq, o_at(a), 0))
        run(score_pass(seq, q_at(a + 2), 0), value_pass(seq, o_at(a + 1), 1))
        return carry

    lax.fori_loop(0, n_lat // 2 - 1, tile_pair, 0)
    run(score_pass(seq, q_at(n_lat - 1), 1), value_pass(seq, o_at(n_lat - 2), 0))
    run(value_pass(seq, o_at(n_lat - 1), 1))


def _attention(p, k_t, diff_lambda, g_sub, lam_init, tiles, ctx_rows):
    tq = ROW_TILE
    tps = tiles.tps
    seq = tps * tq
    out_rows = tiles.per_seq * tq
    assert ctx_rows == tq
    kern = functools.partial(_attn_kernel, lam_init=lam_init, ctx_rows=ctx_rows, has_ctx=not tiles.skip_ctx)
    return pl.pallas_call(
        kern,
        grid=(tiles.n_batch, H_A),
        in_specs=[
            pl.BlockSpec((4, DK), lambda b, h: (0, 0)),
            pl.BlockSpec((1, DV), lambda b, h: (0, 0)),
            pl.BlockSpec((seq, LANE), lambda b, h: (b, P_Q // LANE + h)),
            pl.BlockSpec((LANE, seq), lambda b, h: (h, b)),
            pl.BlockSpec((seq, LANE), lambda b, h: (b, P_V // LANE + h)),
        ],
        out_specs=pl.BlockSpec((out_rows, LANE), lambda b, h: (b, h)),
        out_shape=jax.ShapeDtypeStruct((tiles.count * tq, H_A * DV), ACT_DT),
        scratch_shapes=[pltpu.VMEM((seq, 2 * DV), MXU_DT), pltpu.VMEM((2, 2, tq, seq), F32),
                        pltpu.VMEM((2, 2, tq, LANE), F32)],
        compiler_params=_cparams("parallel", "parallel"),
        name="diff_attention",
    )(diff_lambda, g_sub, p, k_t, p)


def _gmlp_kernel(uv_ref, gv_ref, ws_ref, bs_ref, o_ref):
    uv = jax.nn.gelu(uv_ref[...])
    u = uv[:, :W_B].astype(F32)
    v = (_rms(uv[:, W_B:].astype(F32)) * gv_ref[...]).astype(MXU_DT)
    gw = W_B // G_B
    for c in range(uv.shape[0] // CHUNK_B):
        rows = slice(c * CHUNK_B, (c + 1) * CHUNK_B)
        for g in range(G_B):
            cols = slice(g * gw, (g + 1) * gw)
            mixed = jnp.dot(ws_ref[g], v[rows, cols], preferred_element_type=F32) + bs_ref[:, cols]
            o_ref[rows, cols] = (u[rows, cols] * mixed).astype(o_ref.dtype)


def _gmlp(p, g_v, w_s, b_full, tiles):
    tm = ROW_TILE
    r = tiles.count * tm
    return pl.pallas_call(
        _gmlp_kernel,
        grid=(tiles.count,),
        in_specs=[
            pl.BlockSpec((tm, 2 * W_B), lambda i: (tiles.full(i), P_UV // (2 * W_B))),
            pl.BlockSpec((1, W_B), lambda i: (0, 0)),
            pl.BlockSpec((G_B, CHUNK_B, CHUNK_B), lambda i: (0, 0, 0)),
            pl.BlockSpec((CHUNK_B, W_B), lambda i: (0, 0)),
        ],
        out_specs=pl.BlockSpec((tm, W_B), lambda i: (i, 0)),
        out_shape=jax.ShapeDtypeStruct((r, W_B), ACT_DT),
        compiler_params=_cparams("parallel"),
        name="gmlp",
    )(p, g_v, w_s, b_full)


def _mfeat_kernel(x_ref, prev_ref, next_ref, wc_ref, bc_ref, wqkv_ref, wif_ref, bif_ref,
                  xc_ref, q_ref, k_ref, v_ref, g_ref, *, tps):
    tm = x_ref.shape[0]
    j = pl.program_id(0) % tps
    seg_start = jnp.logical_or(j == 0, j == 1)
    seg_end = jnp.logical_or(j == 0, j == tps - 1)
    x = x_ref[...].astype(F32)
    prow = jnp.where(seg_start, 0.0, prev_ref[...].astype(F32)[HALO - 1:HALO, :])
    nrow = jnp.where(seg_end, 0.0, next_ref[...].astype(F32)[0:1, :])
    rid = lax.broadcasted_iota(jnp.int32, x.shape, 0)
    xp = jnp.where(rid == 0, prow, pltpu.roll(x, 1, 0))
    xn = jnp.where(rid == tm - 1, nrow, pltpu.roll(x, tm - 1, 0))
    xc = wc_ref[0:1, :] * xp + wc_ref[1:2, :] * x + wc_ref[2:3, :] * xn + bc_ref[...]
    xc = xc * _sigmoid(xc)
    xc_ref[...] = xc.astype(xc_ref.dtype)
    xcb = xc.astype(MXU_DT)
    xb = x_ref[...]
    qs, ks, vs = [], [], []
    for h in range(H_C):
        cols = slice(h * DH_C, (h + 1) * DH_C)
        qs.append(jnp.dot(xcb[:, cols], wqkv_ref[0, h], preferred_element_type=F32))
        ks.append(jnp.dot(xcb[:, cols], wqkv_ref[1, h], preferred_element_type=F32))
        vs.append(jnp.dot(xb[:, cols], wqkv_ref[2, h], preferred_element_type=F32))
    q = jnp.concatenate(qs, axis=1)
    k = jnp.concatenate(ks, axis=1)
    v = jnp.concatenate(vs, axis=1)
    qkv = jnp.concatenate([q, k, v], axis=1).astype(MXU_DT)
    gates = jnp.dot(qkv, wif_ref[...], preferred_element_type=F32) + bif_ref[...]
    g_ref[...] = gates.T[0:g_ref.shape[0], :]
    q_ref[...] = q.T.astype(q_ref.dtype)
    k_ref[...] = (k * (DH_C ** -0.5)).astype(k_ref.dtype)
    v_ref[...] = v.T.astype(v_ref.dtype)


def _mlstm_features(p, w_conv, b_conv, w_qkv, w_if_pad, b_if_pad, tps):
    r = p.shape[0]
    tm = ROW_TILE
    hpt = tm // HALO
    xm_blk = P_XM // W_C
    last_halo = r // HALO - 1
    kern = functools.partial(_mfeat_kernel, tps=tps)
    act = jax.ShapeDtypeStruct((r, W_C), ACT_DT)
    act_t = jax.ShapeDtypeStruct((W_C, r), ACT_DT)
    row_spec = pl.BlockSpec((tm, W_C), lambda i: (i, 0))
    col_spec = pl.BlockSpec((W_C, tm), lambda i: (0, i))
    return pl.pallas_call(
        kern,
        grid=(r // tm,),
        in_specs=[
            pl.BlockSpec((tm, W_C), lambda i: (i, xm_blk)),
            pl.BlockSpec((HALO, W_C), lambda i: (jnp.maximum(i * hpt - 1, 0), xm_blk)),
            pl.BlockSpec((HALO, W_C), lambda i: (jnp.minimum((i + 1) * hpt, last_halo), xm_blk)),
            pl.BlockSpec((CONV_K, W_C), lambda i: (0, 0)),
            pl.BlockSpec((1, W_C), lambda i: (0, 0)),
            pl.BlockSpec((3, H_C, DH_C, DH_C), lambda i: (0, 0, 0, 0)),
            pl.BlockSpec((3 * W_C, LANE), lambda i: (0, 0)),
            pl.BlockSpec((1, LANE), lambda i: (0, 0)),
        ],
        out_specs=[row_spec, col_spec, row_spec, col_spec,
                   pl.BlockSpec((2 * GATE_ROWS, tm), lambda i: (0, i))],
        out_shape=[act, act_t, act, act_t, jax.ShapeDtypeStruct((2 * GATE_ROWS, r), F32)],
        compiler_params=_cparams("parallel"),
        name="mlstm_features",
    )(p, p, p, w_conv, b_conv, w_qkv, w_if_pad, b_if_pad)


GATE_ROWS = 2 * H_C
GATE_PACK = ("r", "cmx", "bcum")


def _gate_prep_kernel(g_ref, pack_ref, rt_ref):
    seg = MLSTM_CHUNK
    gt_fwd = g_ref[0:GATE_ROWS, :]
    gt_bwd = g_ref[GATE_ROWS:2 * GATE_ROWS, :]
    n = gt_fwd.shape[1]
    lane = lax.broadcasted_iota(jnp.int32, gt_fwd.shape, 1)
    in_seg = lane & (seg - 1)

    def scan_both(xf, xb, op):
        sh = 1
        while sh < seg:
            xf = jnp.where(in_seg >= sh, op(xf, pltpu.roll(xf, sh, 1)), xf)
            xb = jnp.where(in_seg < seg - sh, op(xb, pltpu.roll(xb, n - sh, 1)), xb)
            sh *= 2
        return xf, xb

    bcum_f, bcum_b = scan_both(jax.nn.log_sigmoid(pltpu.roll(gt_fwd, H_C, 0)),
                               jax.nn.log_sigmoid(pltpu.roll(gt_bwd, H_C, 0)), jnp.add)
    r_f, r_b = gt_fwd - bcum_f, gt_bwd - bcum_b
    cmx_f, cmx_b = scan_both(r_f, r_b, jnp.maximum)
    pack_ref[...] = jnp.concatenate([r_f, cmx_f, bcum_f, r_b, cmx_b, bcum_b], axis=0)
    rt_ref[...] = jnp.concatenate([r_f, r_b, jnp.zeros((LANE - 2 * GATE_ROWS, n), F32)], axis=0).T


def _mlstm_gate_prep(gates_t, n_batch, seq):
    rows = 2 * len(GATE_PACK) * GATE_ROWS
    r = gates_t.shape[1]
    return pl.pallas_call(
        _gate_prep_kernel,
        grid=(n_batch,),
        in_specs=[pl.BlockSpec((2 * GATE_ROWS, seq), lambda b: (0, b))],
        out_specs=[pl.BlockSpec((rows, seq), lambda b: (0, b)), pl.BlockSpec((seq, LANE), lambda b: (b, 0))],
        out_shape=[jax.ShapeDtypeStruct((rows, r), F32), jax.ShapeDtypeStruct((r, LANE), F32)],
        compiler_params=_cparams("parallel"),
        name="mlstm_gate_prep",
    )(gates_t)


def _mlstm_gates(pack, m_ref, reverse):
    ln = pack.shape[1]
    part = {name: pack[i * GATE_ROWS:(i + 1) * GATE_ROWS] for i, name in enumerate(GATE_PACK)}
    end = slice(0, 1) if reverse else slice(ln - 1, ln)
    g = part["bcum"][:, end]
    r_max = part["cmx"][:, end]
    m_loc = g + r_max
    w_end = jnp.exp(part["r"] - r_max)
    m_old = m_ref[:, 0:1]
    m_new = jnp.maximum(g + m_old, m_loc)
    a_old = jnp.exp(g + m_old - m_new)
    a_new = jnp.exp(m_loc - m_new)
    mx = jnp.maximum(m_old, part["cmx"])
    w_inter = jnp.exp(m_old - mx)
    e_neg = jnp.exp(-(part["bcum"] + mx))
    s_id = lax.broadcasted_iota(jnp.int32, (ln, ln), 0)
    t_id = lax.broadcasted_iota(jnp.int32, (ln, ln), 1)
    keep = (s_id >= t_id) if reverse else (s_id <= t_id)
    return dict(mx=mx, w_inter=w_inter, e_neg=e_neg, w_end=w_end, a_old=a_old, a_new=a_new,
                m_new=m_new, keep=keep)


def _mscan_chunk(gf_ref, rf_ref, qf_ref, kf_ref, vf_ref, gb_ref, rb_ref, qb_ref, kb_ref, vb_ref,
                 hf_ref, hb_ref, cf_ref, mf_ref, cb_ref, mb_ref):
    ln = qf_ref.shape[1]
    pr = len(GATE_PACK) * GATE_ROWS
    dirs = ((_mlstm_gates(gf_ref[0:pr, :], mf_ref, False), qf_ref, kf_ref, vf_ref, cf_ref, hf_ref, rf_ref),
            (_mlstm_gates(gb_ref[pr:2 * pr, :], mb_ref, True), qb_ref, kb_ref, vb_ref, cb_ref, hb_ref, rb_ref))
    pairs = [(d, h) for d in range(2) for h in range(H_C)]
    ones_row = jnp.where(lax.broadcasted_iota(jnp.int32, (DH_C, ln), 0) == 0, 1.0, 0.0).astype(MXU_DT)
    head = lambda h: slice(h * DH_C, (h + 1) * DH_C)
    row = lambda h: slice(h, h + 1)

    ks = [dirs[d][2][:, head(h)] for d, h in pairs]
    qts = [dirs[d][1][head(h), :] for d, h in pairs]
    vexts = [jnp.concatenate([dirs[d][3][head(h), :], ones_row], axis=0) for d, h in pairs]
    cexts = [dirs[d][4][h] for d, h in pairs]
    qks = [jnp.dot(k, q_t, preferred_element_type=F32) for k, q_t in zip(ks, qts)]
    rhss = []
    for i, (d, h) in enumerate(pairs):
        g = dirs[d][0]
        r_col = dirs[d][6][:, d * GATE_ROWS + h:d * GATE_ROWS + h + 1]
        d_t = jnp.exp(jnp.where(g["keep"], r_col - g["mx"][row(h), :], NEG_BIG))
        wq = (qts[i].astype(F32) * g["w_inter"][row(h), :]).astype(MXU_DT)
        rhss.append(jnp.concatenate([(qks[i] * d_t).astype(MXU_DT), wq], axis=0))
    ress = [jnp.dot(jnp.concatenate([vexts[i], cexts[i].astype(MXU_DT)], axis=1), rhss[i],
                    preferred_element_type=F32) for i in range(len(pairs))]
    for i, (d, h) in enumerate(pairs):
        g = dirs[d][0]
        den = ress[i][DH_C:DH_C + 1, :]
        hh = ress[i][:DH_C, :] / jnp.maximum(jnp.abs(den), g["e_neg"][row(h), :])
        dirs[d][5][head(h), :] = hh.astype(dirs[d][5].dtype)
    clocs = []
    for i, (d, h) in enumerate(pairs):
        vw_t = (vexts[i].astype(F32) * dirs[d][0]["w_end"][row(h), :]).astype(MXU_DT)
        clocs.append(jnp.dot(vw_t, ks[i], preferred_element_type=F32))
    for i, (d, h) in enumerate(pairs):
        g = dirs[d][0]
        dirs[d][4][h] = g["a_old"][row(h), :] * cexts[i] + g["a_new"][row(h), :] * clocs[i]
    mf_ref[...] = jnp.broadcast_to(dirs[0][0]["m_new"], mf_ref.shape)
    mb_ref[...] = jnp.broadcast_to(dirs[1][0]["m_new"], mb_ref.shape)


def _mscan_kernel(g_ref, r_ref, q_ref, k_ref, v_ref, hf_ref, hb_ref, cf_ref, mf_ref, cb_ref, mb_ref, *, ncc):
    ln = MLSTM_CHUNK
    nck = q_ref.shape[1] // ln
    cf_ref[...] = jnp.zeros_like(cf_ref)
    cb_ref[...] = jnp.zeros_like(cb_ref)
    mf_ref[...] = jnp.zeros_like(mf_ref)
    mb_ref[...] = jnp.zeros_like(mb_ref)

    def chunk_views(c):
        at = pl.ds(pl.multiple_of(c * ln, ln), ln)
        return ((g_ref.at[:, at], r_ref.at[at, :], q_ref.at[:, at], k_ref.at[at, :], v_ref.at[:, at]),
                (hf_ref.at[:, at], hb_ref.at[:, at]))

    def trip(s, carry):
        ins_f, (hf, _) = chunk_views(s)
        ins_b, (_, hb) = chunk_views(jnp.where(s < ncc, ncc - 1 - s, nck - 1 + ncc - s))
        _mscan_chunk(*ins_f, *ins_b, hf, hb, cf_ref, mf_ref, cb_ref, mb_ref)
        return carry

    lax.fori_loop(0, nck, trip, 0)


def _mlstm_scan(q_t, k, v_t, gate_pack, r_t, n_batch, seq, ctx_rows):
    rows = lambda w: pl.BlockSpec((seq, w), lambda b: (b, 0))
    cols = lambda h: pl.BlockSpec((h, seq), lambda b: (0, b))
    out = jax.ShapeDtypeStruct(q_t.shape, ACT_DT)
    return pl.pallas_call(
        functools.partial(_mscan_kernel, ncc=ctx_rows // MLSTM_CHUNK),
        grid=(n_batch,),
        in_specs=[cols(gate_pack.shape[0]), rows(LANE), cols(W_C), rows(W_C), cols(W_C)],
        out_specs=[cols(W_C), cols(W_C)],
        out_shape=[out, out],
        scratch_shapes=[
            pltpu.VMEM((H_C, 2 * DH_C, DH_C), F32), pltpu.VMEM((8, LANE), F32),
            pltpu.VMEM((H_C, 2 * DH_C, DH_C), F32), pltpu.VMEM((8, LANE), F32),
        ],
        compiler_params=_cparams("parallel"),
        name="mlstm_scan",
    )(gate_pack, r_t, q_t, k, v_t)


def _route(logits, carry):
    lane = lax.broadcasted_iota(jnp.int32, logits.shape, 1).astype(F32)
    big = float(4 * LANE)
    gl = jnp.where(lane < N_GROUPS, logits, NEG_BIG)
    gmax = jnp.max(gl, axis=-1, keepdims=True)
    g_star = jnp.min(jnp.where(gl == gmax, lane, big), axis=-1, keepdims=True)
    p_g = 1.0 / jnp.sum(jnp.exp(gl - gmax), axis=-1, keepdims=True)
    e_lo = g_star * EXP_PER_GROUP + N_GROUPS
    in_group = jnp.logical_and(lane >= e_lo, lane < e_lo + EXP_PER_GROUP)
    el = jnp.where(in_group, logits, NEG_BIG)
    v1 = jnp.max(el, axis=-1, keepdims=True)
    i1 = jnp.min(jnp.where(el == v1, lane, big), axis=-1, keepdims=True)
    el2 = jnp.where(lane == i1, NEG_BIG, el)
    v2 = jnp.max(el2, axis=-1, keepdims=True)
    i2 = jnp.min(jnp.where(el2 == v2, lane, big), axis=-1, keepdims=True)
    e21 = jnp.exp(v2 - v1)
    w1 = p_g / (1.0 + e21)
    w2 = p_g * e21 / (1.0 + e21)
    e1 = i1 - N_GROUPS
    e2 = i2 - N_GROUPS
    oh1 = lane == e1
    oh2 = lane == e2
    both = jnp.where(jnp.logical_or(oh1, oh2), 1.0, 0.0)
    tm = logits.shape[0]
    lower = (lax.broadcasted_iota(jnp.int32, (tm, tm), 1) < lax.broadcasted_iota(jnp.int32, (tm, tm), 0))
    before = jnp.dot(jnp.where(lower, 1.0, 0.0).astype(MXU_DT), both.astype(MXU_DT),
                     preferred_element_type=F32) + carry
    rank1 = jnp.sum(jnp.where(oh1, before, 0.0), axis=-1, keepdims=True)
    rank2 = jnp.sum(jnp.where(oh2, before, 0.0), axis=-1, keepdims=True)
    out = jnp.where(lane == 0, e1, 0.0)
    out = jnp.where(lane == 1, e2, out)
    out = jnp.where(lane == 2, w1, out)
    out = jnp.where(lane == 3, w2, out)
    out = jnp.where(lane == 4, rank1, out)
    out = jnp.where(lane == 5, rank2, out)
    return out, carry + jnp.sum(both, axis=0, keepdims=True)


def _merge_kernel(x_ref, gate_ref, z_ref, ya_ref, yb_ref, hf_ref, hb_ref, xc_ref, gm_ref, sk_ref,
                  wbr_ref, wo_ref, mod_ref, g2_ref, wr_ref, br_ref, xo_ref, rt_ref, cnt_ref, carry_ref):
    @pl.when(pl.program_id(0) == 0)
    def _():
        carry_ref[...] = jnp.zeros_like(carry_ref)

    hs_t = hf_ref[...].astype(F32) + hb_ref[...].astype(F32)
    hn_t = []
    for h in range(H_C):
        part = hs_t[h * DH_C:(h + 1) * DH_C, :]
        hn_t.append(part * lax.rsqrt(jnp.mean(part * part, axis=0, keepdims=True) + EPS))
    hn = jnp.concatenate(hn_t, axis=0).T * gm_ref[...]
    yc = (hn + sk_ref[...] * xc_ref[...].astype(F32)) * _sigmoid(z_ref[...].astype(F32))
    ys = (ya_ref[...], yb_ref[...], yc.astype(MXU_DT))
    merged = None
    for i in range(3):
        gate = _sigmoid(gate_ref[:, i * D_MODEL:(i + 1) * D_MODEL].astype(MXU_DT))
        term = gate * jnp.dot(ys[i], wbr_ref[i], preferred_element_type=F32).astype(MXU_DT)
        merged = term if merged is None else merged + term
    y = jnp.dot(merged, wo_ref[...], preferred_element_type=F32)
    xn = x_ref[...] + mod_ref[2:3, :] * y
    xo_ref[...] = xn
    h2 = _rms(xn) * g2_ref[...] * (1.0 + mod_ref[4:5, :]) + mod_ref[3:4, :]
    h_hi, h_lo = _split_bf16(h2)
    logits = (jnp.dot(h_hi, wr_ref[0], preferred_element_type=F32)
              + jnp.dot(h_lo, wr_ref[0], preferred_element_type=F32)
              + jnp.dot(h_hi, wr_ref[1], preferred_element_type=F32)) + br_ref[...]
    route, carry = _route(logits, carry_ref[0:1, :])
    rt_ref[...] = route
    carry_ref[0:1, :] = carry
    cnt_ref[...] = jnp.broadcast_to(carry, cnt_ref.shape)


def _merge(xall, p, ya, yb, hf, hb, xconv, g_m, skip, w_br, w_o, mod, g2, w_route, b_route, tiles):
    d = xall.shape[1]
    tm = ROW_TILE
    r = tiles.count * tm

    def mod_idx(i):
        return (tiles.mod_row(i), 0, 0)

    row = lambda w: pl.BlockSpec((tm, w), lambda i: (i, 0))
    frow = lambda w: pl.BlockSpec((tm, w), lambda i: (tiles.full(i), 0))
    fcol = lambda w: pl.BlockSpec((w, tm), lambda i: (0, tiles.full(i)))
    full = lambda shape: pl.BlockSpec(shape, lambda i: (0,) * len(shape))
    return pl.pallas_call(
        _merge_kernel,
        grid=(tiles.count,),
        in_specs=[
            frow(d),
            pl.BlockSpec((tm, GATE_W), lambda i: (tiles.full(i), P_GATE // GATE_W)),
            pl.BlockSpec((tm, W_C), lambda i: (tiles.full(i), P_Z // W_C)),
            row(W_BR), row(W_BR), fcol(W_C), fcol(W_C), frow(W_C),
            full((1, W_C)), full((1, W_C)),
            full((3, W_BR, d)), full((d, d)),
            pl.BlockSpec((None, N_MOD, d), mod_idx),
            full((1, d)), full((2, d, LANE)), full((1, LANE)),
        ],
        out_specs=[row(d), row(LANE), full((8, LANE))],
        out_shape=[jax.ShapeDtypeStruct((r, d), F32), jax.ShapeDtypeStruct((r, LANE), F32),
                   jax.ShapeDtypeStruct((8, LANE), F32)],
        scratch_shapes=[pltpu.VMEM((8, LANE), F32)],
        compiler_params=_cparams("arbitrary"),
        name="merge_route",
    )(xall, p, p, ya, yb, hf, hb, xconv, g_m, skip, w_br, w_o, mod, g2, w_route, b_route)


def _row_copy(src_hbm, src_row, dst_ref, dst_row, sem):
    return pltpu.make_async_copy(src_hbm.at[pl.ds(src_row, 1), :], dst_ref.at[pl.ds(dst_row, 1), :], sem)


def _dispatch_kernel(fill_ref, nlive_ref, pos_ref, x_ref, g2_ref, mod_ref, xs_hbm, hbuf, zbuf, zsem, sem):
    i = pl.program_id(0)
    n = pl.num_programs(0)
    tm = x_ref.shape[0]
    slot = i % 2
    bm = zbuf.shape[0]
    n_blocks = xs_hbm.shape[0] // bm

    @pl.when(i == 0)
    def _():
        zbuf[...] = jnp.zeros_like(zbuf)

        def fill_block(b):
            return pltpu.make_async_copy(zbuf, xs_hbm.at[pl.ds(b * bm, bm), :], zsem)

        def fill_last(k, carry):
            fill_block(fill_ref[k]).start()
            return carry

        def fill_unused(b, carry):
            fill_block(b).start()
            return carry

        def wait_fill(k, carry):
            fill_block(0).wait()
            return carry

        n_live = nlive_ref[0]
        n_used = nlive_ref[1]
        lax.fori_loop(0, n_used, fill_last, 0)
        lax.fori_loop(n_live, n_blocks, fill_unused, 0)
        lax.fori_loop(0, n_used + n_blocks - n_live, wait_fill, 0)

    hbuf[slot] = _rms(x_ref[...]) * g2_ref[...] * (1.0 + mod_ref[4:5, :]) + mod_ref[3:4, :]

    for r in range(tm):
        for j in range(TOP_K):
            pltpu.make_async_copy(hbuf.at[slot, pl.ds(r, 1), :], xs_hbm.at[pl.ds(pos_ref[0, j * tm + r], 1), :],
                                  sem.at[slot]).start(priority=j % N_DMA_PRIORITIES)

    def wait_tile(s):
        pltpu.make_async_copy(hbuf.at[s], xs_hbm.at[pl.ds(0, tm), :], sem.at[s]).wait()
        pltpu.make_async_copy(hbuf.at[s], xs_hbm.at[pl.ds(0, tm), :], sem.at[s]).wait()

    @pl.when(i > 0)
    def _():
        wait_tile(1 - slot)

    @pl.when(i == n - 1)
    def _():
        wait_tile(slot)


def _moe_dispatch(xall, g2, mod, fill_blocks, n_live, pos_tiles, n_slots, tiles):
    r, d = xall.shape
    tm = ROW_TILE
    assert TOP_K == 2 and r == tiles.count * tm

    def mod_idx(i, fb, nl):
        return (tiles.mod_row(i), 0, 0)

    grid_spec = pltpu.PrefetchScalarGridSpec(
        num_scalar_prefetch=2,
        grid=(r // tm,),
        in_specs=[
            pl.BlockSpec((None, 1, TOP_K * tm), lambda i, fb, nl: (i, 0, 0), memory_space=pltpu.SMEM),
            pl.BlockSpec((tm, d), lambda i, fb, nl: (i, 0)),
            pl.BlockSpec((1, d), lambda i, fb, nl: (0, 0)),
            pl.BlockSpec((None, N_MOD, d), mod_idx),
        ],
        out_specs=pl.BlockSpec(memory_space=pl.ANY),
        scratch_shapes=[pltpu.VMEM((2, tm, d), F32), pltpu.VMEM((MOE_BLOCK, d), F32),
                        pltpu.SemaphoreType.DMA(()), pltpu.SemaphoreType.DMA((2,))],
    )
    return pl.pallas_call(
        _dispatch_kernel,
        grid_spec=grid_spec,
        out_shape=jax.ShapeDtypeStruct((n_slots, d), F32),
        compiler_params=_cparams("arbitrary"),
        name="moe_dispatch",
    )(fill_blocks, n_live, pos_tiles, xall, g2, mod)


def _moe_kernel(be_ref, nv_ref, x_ref, wg_ref, wu_ref, wd_ref, y_ref, wg_mx, wu_mx, wd_mx):
    i = pl.program_id(0)
    n_valid = nv_ref[i]

    @pl.when(jnp.logical_or(i == 0, be_ref[i] != be_ref[jnp.maximum(i - 1, 0)]))
    def _():
        wg_mx[...] = wg_ref[...].astype(wg_mx.dtype)
        wu_mx[...] = wu_ref[...].astype(wu_mx.dtype)
        wd_mx[...] = wd_ref[...].astype(wd_mx.dtype)

    @pl.when(n_valid > 0)
    def _():
        x = x_ref[...].astype(MXU_DT)
        a = jnp.dot(x, wg_mx[...], preferred_element_type=F32)
        u = jnp.dot(x, wu_mx[...], preferred_element_type=F32)
        act = (a * _sigmoid(a) * u).astype(MXU_DT)
        y_ref[...] = jnp.dot(act, wd_mx[...], preferred_element_type=F32)

    @pl.when(n_valid <= 0)
    def _():
        y_ref[...] = jnp.zeros_like(y_ref)


def _moe_experts(xs, block_e, n_valid, w_gate, w_up, w_down, layer):
    n_slots, d = xs.shape
    bm = MOE_BLOCK
    grid_spec = pltpu.PrefetchScalarGridSpec(
        num_scalar_prefetch=2,
        grid=(n_slots // bm,),
        in_specs=[
            pl.BlockSpec((bm, d), lambda i, be, nv: (i, 0)),
            pl.BlockSpec((None, None, d, D_EXPERT), lambda i, be, nv: (layer, be[i], 0, 0)),
            pl.BlockSpec((None, None, d, D_EXPERT), lambda i, be, nv: (layer, be[i], 0, 0)),
            pl.BlockSpec((None, None, D_EXPERT, d), lambda i, be, nv: (layer, be[i], 0, 0)),
        ],
        out_specs=pl.BlockSpec((bm, d), lambda i, be, nv: (i, 0)),
        scratch_shapes=[pltpu.VMEM((d, D_EXPERT), MXU_DT), pltpu.VMEM((d, D_EXPERT), MXU_DT),
                        pltpu.VMEM((D_EXPERT, d), MXU_DT)],
    )
    return pl.pallas_call(
        _moe_kernel,
        grid_spec=grid_spec,
        out_shape=jax.ShapeDtypeStruct((n_slots, d), F32),
        compiler_params=_cparams("arbitrary"),
        name="moe_experts",
    )(block_e, n_valid, xs, w_gate, w_up, w_down)


def _combine_kernel(pos_ref, pos_next_ref, x_ref, rt_ref, mod_ref, gf_ref, yb_hbm, xo_ref, ybuf, sem, *, final):
    i = pl.program_id(0)
    n = pl.num_programs(0)
    tm = x_ref.shape[0]

    def start_gather(ids_ref, slot):
        def body(c, carry):
            for u in range(DMA_UNROLL):
                r = c * DMA_UNROLL + u
                _row_copy(yb_hbm, ids_ref[0, r], ybuf.at[slot], r, sem.at[slot]).start()
            return carry
        lax.fori_loop(0, TOP_K * tm // DMA_UNROLL, body, 0)

    def wait_gather(slot):
        pltpu.make_async_copy(yb_hbm.at[pl.ds(0, TOP_K * tm), :], ybuf.at[slot], sem.at[slot]).wait()

    @pl.when(i == 0)
    def _():
        start_gather(pos_ref, 0)

    slot = i % 2
    wait_gather(slot)
    per = TOP_K * tm // COMBINE_GROUPS
    rows_per = tm // COMBINE_GROUPS
    for c in range(COMBINE_GROUPS):
        for r in range(c * per, (c + 1) * per):
            _row_copy(yb_hbm, pos_next_ref[0, r], ybuf.at[1 - slot], r,
                      sem.at[1 - slot]).start(priority=r % N_DMA_PRIORITIES)
        rows = slice(c * rows_per, (c + 1) * rows_per)
        rt = rt_ref[rows, :]
        f = (rt[:, 2:3] * ybuf[slot, rows, :]
             + rt[:, 3:4] * ybuf[slot, tm + c * rows_per:tm + (c + 1) * rows_per, :])
        xn = x_ref[rows, :] + mod_ref[5:6, :] * f
        xo_ref[rows, :] = _rms(xn) * gf_ref[...] if final else xn

    @pl.when(i == n - 1)
    def _():
        wait_gather(1 - slot)


def _combine(xall, route, mod, g_final, yb, pos, tiles, final):
    r, d = xall.shape
    tm = ROW_TILE
    nt = tiles.count
    last = nt - 1
    assert r == nt * tm

    def mod_idx(i):
        return (tiles.mod_row(i), 0, 0)

    row = lambda w: pl.BlockSpec((tm, w), lambda i: (i, 0))
    return pl.pallas_call(
        functools.partial(_combine_kernel, final=final),
        grid=(nt,),
        in_specs=[
            pl.BlockSpec((None, 1, TOP_K * tm), lambda i: (i, 0, 0), memory_space=pltpu.SMEM),
            pl.BlockSpec((None, 1, TOP_K * tm), lambda i: (jnp.minimum(i + 1, last), 0, 0),
                         memory_space=pltpu.SMEM),
            row(d), row(LANE),
            pl.BlockSpec((None, N_MOD, d), mod_idx),
            pl.BlockSpec((1, d), lambda i: (0, 0)),
            pl.BlockSpec(memory_space=pl.ANY),
        ],
        out_specs=row(d),
        out_shape=jax.ShapeDtypeStruct((r, d), F32),
        scratch_shapes=[pltpu.VMEM((2, TOP_K * tm, d), F32), pltpu.SemaphoreType.DMA((2,))],
        compiler_params=_cparams("arbitrary"),
        name="moe_combine",
    )(pos, pos, xall, route, mod, g_final, yb)


def _dispatch_plan(route, counts_f, n_tiles):
    n = route.shape[0]
    bm = MOE_BLOCK
    n_blocks = -(-(n * TOP_K + N_EXPERTS * (bm - 1)) // bm)
    counts = counts_f[:N_EXPERTS].astype(jnp.int32)
    padded = (counts + bm - 1) // bm * bm
    pend = jnp.cumsum(padded)
    seg_off = pend - padded
    eid = route[:, 0:TOP_K].astype(jnp.int32)
    rank = route[:, 4:4 + TOP_K].astype(jnp.int32)
    experts = jnp.arange(N_EXPERTS, dtype=jnp.int32)
    pos = jnp.sum(jnp.where(eid[..., None] == experts, seg_off, 0), axis=-1) + rank
    blk0 = jnp.arange(n_blocks, dtype=jnp.int32) * bm
    block_e = jnp.minimum(jnp.sum((blk0[:, None] >= pend[None, :]).astype(jnp.int32), axis=1), N_EXPERTS - 1)
    seg_end = seg_off + counts
    n_valid = jnp.clip(seg_end[block_e] - blk0, 0, bm).astype(jnp.int32)
    pos_tiles = pos.reshape(n_tiles, n // n_tiles, TOP_K).transpose(0, 2, 1).reshape(n_tiles, 1, -1)
    used = counts > 0
    last_blocks = jnp.sort(jnp.where(used, pend // bm - 1, n_blocks)).astype(jnp.int32)
    n_live = jnp.stack([pend[-1] // bm, jnp.sum(used)]).astype(jnp.int32)
    return block_e.astype(jnp.int32), n_valid, last_blocks, n_live, pos_tiles, n_blocks * bm


def _rope_tables(seq, ctx_rows):
    t = jnp.arange(seq - ctx_rows)
    inv = ROPE_BASE ** (-jnp.arange(0, ROT_AX, 2, dtype=F32) / ROT_AX)
    ang_r = (t // GRID_W).astype(F32)[:, None] * inv
    ang_c = (t % GRID_W).astype(F32)[:, None] * inv
    cos64 = jnp.concatenate([jnp.cos(ang_r)] * 2 + [jnp.cos(ang_c)] * 2, axis=1)
    sin64 = jnp.concatenate([-jnp.sin(ang_r), jnp.sin(ang_r), -jnp.sin(ang_c), jnp.sin(ang_c)], axis=1)
    cos = jnp.concatenate([jnp.ones((ctx_rows, DK), F32), cos64], axis=0)
    sin = jnp.concatenate([jnp.zeros((ctx_rows, DK), F32), sin64], axis=0)
    return jnp.tile(cos, (1, LANE // DK)), jnp.tile(sin, (1, LANE // DK))


def kernel(x, c, ctx, c_ctx, w_mod, b_mod, g_norm1, g_norm2, w_in, diff_lambda, g_diff_subln, w_gmlp_s, b_gmlp_s, g_gmlp_v, w_conv_m, b_conv_m, w_qkv_m, w_if_m, b_if_m, g_mlstm_norm, skip_m, w_branch, w_out, w_route_g, b_route_g, w_route_e, b_route_e, w_e_gate, w_e_up, w_e_down, g_final):
    n_batch, t_lat, d = x.shape
    ctx_rows = ctx.shape[1]
    depth = w_in.shape[0]
    seq = ctx_rows + t_lat
    assert d == D_MODEL and ctx_rows == ROW_TILE and t_lat % ROW_TILE == 0 and t_lat % GRID_W == 0
    tps = seq // ROW_TILE
    all_tiles = _Tiles(n_batch, tps, False)

    xall = (ctx.reshape(n_batch * ctx_rows, d), x.reshape(n_batch * t_lat, d))
    mb = -(-(n_batch + 1) // 8) * 8
    c_all = jnp.zeros((mb, d), F32).at[:n_batch].set(c).at[n_batch].set(c_ctx)
    mod_all = _modulation(c_all, w_mod, b_mod)[:, :n_batch + 1].reshape(depth, n_batch + 1, N_MOD, d)
    cos, sin_signed = _rope_tables(seq, ctx_rows)
    w_in_b = w_in.astype(MXU_DT)

    for l in range(depth):
        lam_init = 0.8 - 0.6 * math.exp(-0.3 * l)
        mod = mod_all[l]
        last = l == depth - 1
        tiles = _Tiles(n_batch, tps, last)
        if l == 0:
            p, k_t, xall = _inproj(xall, g_norm1[l][None], mod, cos, sin_signed, w_in_b, l, all_tiles)
        else:
            p, k_t = _inproj(xall, g_norm1[l][None], mod, cos, sin_signed, w_in_b, l, all_tiles)

        ya = _attention(p, k_t, diff_lambda[l], g_diff_subln[l][None], lam_init, tiles, ctx_rows)

        b_full = jnp.repeat(b_gmlp_s[l].T, W_B // G_B, axis=1)
        yb = _gmlp(p, g_gmlp_v[l][None], w_gmlp_s[l].astype(MXU_DT), b_full, tiles)

        w_if_pad = jnp.zeros((3 * W_C, LANE), F32).at[:, :4 * H_C].set(w_if_m[l]).astype(MXU_DT)
        b_if_pad = jnp.zeros((1, LANE), F32).at[0, :4 * H_C].set(b_if_m[l])
        xconv, q_m, k_m, v_m, gates_t = _mlstm_features(
            p, w_conv_m[l], b_conv_m[l][None], w_qkv_m[l].astype(MXU_DT), w_if_pad, b_if_pad, tps)
        gate_pack, r_t = _mlstm_gate_prep(gates_t, n_batch, seq)
        hf, hb = _mlstm_scan(q_m, k_m, v_m, gate_pack, r_t, n_batch, seq, ctx_rows)

        w_route = (jnp.zeros((d, LANE), F32).at[:, :N_GROUPS].set(w_route_g[l])
                   .at[:, N_GROUPS:N_GROUPS + N_EXPERTS].set(w_route_e[l]))
        w_route = jnp.stack(_split_bf16(w_route))
        b_route = (jnp.zeros((1, LANE), F32).at[0, :N_GROUPS].set(b_route_g[l])
                   .at[0, N_GROUPS:N_GROUPS + N_EXPERTS].set(b_route_e[l]))
        xall, route, counts = _merge(xall, p, ya, yb, hf, hb, xconv, g_mlstm_norm[l][None], skip_m[l][None],
                                         w_branch[l].astype(MXU_DT), w_out[l].astype(MXU_DT), mod,
                                         g_norm2[l][None], w_route, b_route, tiles)

        block_e, n_valid, fill_blocks, n_live, pos_tiles, n_slots = _dispatch_plan(route, counts[0], tiles.count)
        xs = _moe_dispatch(xall, g_norm2[l][None], mod, fill_blocks, n_live, pos_tiles, n_slots, tiles)
        y_sorted = _moe_experts(xs, block_e, n_valid, w_e_gate, w_e_up, w_e_down, l)
        xall = _combine(xall, route, mod, g_final[None], y_sorted, pos_tiles, tiles, last)

    return xall.reshape(n_batch, t_lat, d)
```

```python
import functools
import math
from typing import NamedTuple

import jax
import jax.numpy as jnp
from jax import lax
from jax.experimental import pallas as pl
from jax.experimental.pallas import tpu as pltpu

F32 = jnp.float32
MXU_DT = jnp.bfloat16
ACT_DT = jnp.bfloat16
HIGHEST = lax.Precision.HIGHEST

D_MODEL = 1024
N_MOD = 6
EPS = 1e-6
GRID_W = 64
H_A, DK = 4, 64
DV = 2 * DK
ROT_AX = DK // 2
ROPE_BASE = 10000.0
W_B, G_B, CHUNK_B = 512, 4, 128
H_C, DH_C, CONV_K, MLSTM_CHUNK = 4, 128, 3, 128
W_C = H_C * DH_C
N_GROUPS, EXP_PER_GROUP, TOP_K, D_EXPERT = 4, 8, 2, 512
N_EXPERTS = N_GROUPS * EXP_PER_GROUP
W_BR = 512

GATE_W = 3 * D_MODEL
P_GATE = 0
P_K = GATE_W
P_V = P_K + 512
P_XM = P_V + 512
P_Q = P_XM + 512
P_UV = P_Q + 512
P_Z = P_UV + 2 * W_B
IN_COLS = P_Z + W_C
ORIG_GATE0 = IN_COLS - GATE_W

LANE = 128
ROW_TILE = 256
PROJ_CHUNK = 512
MOE_BLOCK = 512
HALO = 16
DMA_UNROLL = 8
ATTN_KV_CHUNK = 256
COMBINE_GROUPS = 8
N_DMA_PRIORITIES = 2
LOG2_E = math.log2(math.e)
VMEM_LIMIT = 56 * 1024 * 1024
NEG_BIG = -1e30


class _Tiles(NamedTuple):
    n_batch: int
    tps: int
    skip_ctx: bool

    @property
    def per_seq(self):
        return self.tps - 1 if self.skip_ctx else self.tps

    @property
    def count(self):
        return self.n_batch * self.per_seq

    def full(self, i):
        if not self.skip_ctx:
            return i
        return (i // self.per_seq) * self.tps + 1 + i % self.per_seq

    def mod_row(self, i):
        if self.skip_ctx:
            return i // self.per_seq
        return jnp.where(i % self.tps == 0, self.n_batch, i // self.tps)


def _cparams(*sem):
    return pltpu.CompilerParams(dimension_semantics=sem, vmem_limit_bytes=VMEM_LIMIT)


def _rms(x):
    return x * lax.rsqrt(jnp.mean(x * x, axis=-1, keepdims=True) + EPS)


def _sigmoid(x):
    return 0.5 * jnp.tanh(0.5 * x) + 0.5


def _split_bf16(x):
    hi = x.astype(jnp.bfloat16)
    return hi, (x - hi.astype(F32)).astype(jnp.bfloat16)


def _mod_kernel(c_ref, w_ref, b_ref, o_ref):
    c = c_ref[...]
    s = c * jax.nn.sigmoid(c)
    o_ref[...] = jnp.dot(s, w_ref[...], preferred_element_type=F32, precision=HIGHEST) + b_ref[...]


def _modulation(c_all, w_mod, b_mod):
    n_layer = w_mod.shape[0]
    mb, d = c_all.shape
    tn = 1024
    return pl.pallas_call(
        _mod_kernel,
        grid=(n_layer, N_MOD * d // tn),
        in_specs=[
            pl.BlockSpec((mb, d), lambda l, j: (0, 0)),
            pl.BlockSpec((None, d, tn), lambda l, j: (l, 0, j)),
            pl.BlockSpec((None, 1, tn), lambda l, j: (l, 0, j)),
        ],
        out_specs=pl.BlockSpec((None, mb, tn), lambda l, j: (l, 0, j)),
        out_shape=jax.ShapeDtypeStruct((n_layer, mb, N_MOD * d), F32),
        compiler_params=_cparams("parallel", "parallel"),
        name="modulation",
    )(c_all, w_mod, b_mod.reshape(n_layer, 1, N_MOD * d))


def _rope(acc, cos, sin_signed):
    w = acc.shape[1]
    half = ROT_AX // 2
    lane = lax.broadcasted_iota(jnp.int32, acc.shape, 1)
    partner = jnp.where((lane & half) == 0, pltpu.roll(acc, w - half, 1), pltpu.roll(acc, half, 1))
    reps = w // cos.shape[1]
    return acc * jnp.tile(cos, (1, reps)) + partner * jnp.tile(sin_signed, (1, reps))


def _inproj_first_kernel(ctx_ref, lat_ref, g_ref, mod_ref, cos_ref, sin_ref, w_ref, o_ref, kt_ref, xo_ref, *, tps):
    is_ctx = pl.program_id(0) % tps == 0
    xo_ref[...] = jnp.where(is_ctx, ctx_ref[...], lat_ref[...])
    _inproj_kernel(xo_ref, g_ref, mod_ref, cos_ref, sin_ref, w_ref, o_ref, kt_ref)


def _inproj_kernel(x_ref, g_ref, mod_ref, cos_ref, sin_ref, w_ref, o_ref, kt_ref):
    x = x_ref[...]
    h = (_rms(x) * g_ref[...] * (1.0 + mod_ref[1:2, :]) + mod_ref[0:1, :]).astype(MXU_DT)
    k_chunk = P_K // PROJ_CHUNK
    q_chunk = P_Q // PROJ_CHUNK
    chunks = range(IN_COLS // PROJ_CHUNK)
    cols = lambda c: slice(c * PROJ_CHUNK, (c + 1) * PROJ_CHUNK)

    def project(c):
        w0 = (ORIG_GATE0 + c * PROJ_CHUNK) % IN_COLS
        acc = jnp.dot(h, w_ref[:, w0:w0 + PROJ_CHUNK], preferred_element_type=F32)
        if c == k_chunk:
            acc = _rope(acc, cos_ref[...], sin_ref[...])
            kt_ref[...] = acc.T.astype(kt_ref.dtype)
        elif c == q_chunk:
            acc = _rope(acc, cos_ref[...], sin_ref[...]) * (DK ** -0.5 * LOG2_E)
        o_ref[:, cols(c)] = acc.astype(o_ref.dtype)

    for c in chunks:
        project(c)


def _inproj(xall, g1, mod, cos, sin_signed, w_in, layer, tiles):
    tm = ROW_TILE
    tps = tiles.tps
    r = tiles.count * tm
    assert ORIG_GATE0 % PROJ_CHUNK == 0 and not tiles.skip_ctx
    first = isinstance(xall, tuple)
    d = xall[0].shape[1] if first else xall.shape[1]

    def mod_idx(i):
        return (tiles.mod_row(i), 0, 0)

    row = pl.BlockSpec((tm, d), lambda i: (i, 0))
    if first:
        x_specs = [pl.BlockSpec((tm, d), lambda i: (i // tps, 0)),
                   pl.BlockSpec((tm, d), lambda i: ((i // tps) * (tps - 1) + jnp.maximum(i % tps - 1, 0), 0))]
        x_args = list(xall)
        kern = functools.partial(_inproj_first_kernel, tps=tps)
    else:
        x_specs, x_args, kern = [row], [xall], _inproj_kernel
    return pl.pallas_call(
        kern,
        grid=(r // tm,),
        in_specs=x_specs + [
            pl.BlockSpec((1, d), lambda i: (0, 0)),
            pl.BlockSpec((None, N_MOD, d), mod_idx),
            pl.BlockSpec((tm, LANE), lambda i: (i % tps, 0)),
            pl.BlockSpec((tm, LANE), lambda i: (i % tps, 0)),
            pl.BlockSpec((None, d, IN_COLS), lambda i: (layer, 0, 0)),
        ],
        out_specs=[pl.BlockSpec((tm, IN_COLS), lambda i: (i, 0)),
                   pl.BlockSpec((PROJ_CHUNK, tm), lambda i: (0, i))] + ([row] if first else []),
        out_shape=[jax.ShapeDtypeStruct((r, IN_COLS), ACT_DT), jax.ShapeDtypeStruct((PROJ_CHUNK, r), ACT_DT)]
        + ([jax.ShapeDtypeStruct((r, d), F32)] if first else []),
        compiler_params=_cparams("parallel"),
        name="inproj",
    )(*x_args, g1, mod, cos, sin_signed, w_in)


def _attn_kernel(dl_ref, g_ref, q_ref, kt_ref, v_ref, o_ref, vext_ref, s_ref, mx_ref, *,
                 lam_init, ctx_rows, has_ctx):
    tq = s_ref.shape[2]
    seq = kt_ref.shape[1]
    one_hot = lax.broadcasted_iota(jnp.int32, v_ref.shape, 1) == 0
    vext_ref[:, :DV] = v_ref[...]
    vext_ref[:, DV:] = jnp.where(one_hot, 1.0, 0.0).astype(vext_ref.dtype)

    dl = dl_ref[...]
    lam = (jnp.exp(jnp.sum(dl[0:1] * dl[1:2], keepdims=True))
           - jnp.exp(jnp.sum(dl[2:3] * dl[3:4], keepdims=True)) + lam_init)
    def chunk_list(kv_rows):
        kc = min(ATTN_KV_CHUNK, kv_rows)
        assert kv_rows % kc == 0
        return [slice(c * kc, (c + 1) * kc) for c in range(kv_rows // kc)]

    def score_pass(kv_rows, q_rows, buf):
        q = q_ref[q_rows, :]
        lane = lax.broadcasted_iota(jnp.int32, q.shape, 1)
        zero = jnp.zeros_like(q)
        qms = (jnp.where(lane < DK, q, zero), jnp.where(lane >= DK, q, zero))
        mruns = [None, None]
        for cols in chunk_list(kv_rows):
            for m in range(2):
                s = jnp.dot(qms[m], kt_ref[:, cols], preferred_element_type=F32)
                s_ref[buf, m, :, cols] = s
                for g in range(s.shape[1] // LANE):
                    blk = s[:, g * LANE:(g + 1) * LANE]
                    mruns[m] = blk if mruns[m] is None else jnp.maximum(mruns[m], blk)
            yield
        for m in range(2):
            mx = jnp.max(mruns[m], axis=-1, keepdims=True)
            mx_ref[buf, m] = jnp.broadcast_to(mx, mx_ref.shape[2:])
        yield

    def value_pass(kv_rows, out_rows, buf):
        mxs = [mx_ref[buf, m][:, 0:1] for m in range(2)]
        accs = [jnp.zeros((tq, 2 * DV), F32)] * 2
        for cols in chunk_list(kv_rows):
            for m in range(2):
                p = jnp.exp2(s_ref[buf, m, :, cols] - mxs[m])
                accs[m] = accs[m] + jnp.dot(p.astype(MXU_DT), vext_ref[cols, :], preferred_element_type=F32)
            yield
        o = (accs[0][:, :DV] * (1.0 / accs[0][:, DV:DV + 1])
             - accs[1][:, :DV] * (lam / accs[1][:, DV:DV + 1]))
        o_ref[out_rows, :] = (_rms(o) * g_ref[...] * (1.0 - lam_init)).astype(o_ref.dtype)
        yield

    def run(*passes):
        live = list(passes)
        while live:
            live = [g for g in live if next(g, "done") != "done"]

    n_lat = (seq - ctx_rows) // tq
    q_at = lambda i: pl.ds(pl.multiple_of(ctx_rows + i * tq, tq), tq)
    assert n_lat % 2 == 0
    out_off = 0
    if has_ctx:
        out_off = ctx_rows
        run(score_pass(ctx_rows, slice(0, ctx_rows), 1))
        run(score_pass(seq, q_at(0), 0), value_pass(ctx_rows, slice(0, ctx_rows), 1))
    else:
        run(score_pass(seq, q_at(0), 0))
    o_at = lambda i: pl.ds(pl.multiple_of(out_off + i * tq, tq), tq)

    def tile_pair(j, carry):
        a = 2 * j
        run(score_pass(seq, q_at(a + 1), 1), value_pass(seq, o_at(a), 0))
        run(score_pass(seq, q_at(a + 2), 0), value_pass(seq, o_at(a + 1), 1))
        return carry

    lax.fori_loop(0, n_lat // 2 - 1, tile_pair, 0)
    run(score_pass(seq, q_at(n_lat - 1), 1), value_pass(seq, o_at(n_lat - 2), 0))
    run(value_pass(seq, o_at(n_lat - 1), 1))


def _attention(p, k_t, diff_lambda, g_sub, lam_init, tiles, ctx_rows):
    tq = ROW_TILE
    tps = tiles.tps
    seq = tps * tq
    out_rows = tiles.per_seq * tq
    assert ctx_rows == tq
    kern = functools.partial(_attn_kernel, lam_init=lam_init, ctx_rows=ctx_rows, has_ctx=not tiles.skip_ctx)
    return pl.pallas_call(
        kern,
        grid=(tiles.n_batch, H_A),
        in_specs=[
            pl.BlockSpec((4, DK), lambda b, h: (0, 0)),
            pl.BlockSpec((1, DV), lambda b, h: (0, 0)),
            pl.BlockSpec((seq, LANE), lambda b, h: (b, P_Q // LANE + h)),
            pl.BlockSpec((LANE, seq), lambda b, h: (h, b)),
            pl.BlockSpec((seq, LANE), lambda b, h: (b, P_V // LANE + h)),
        ],
        out_specs=pl.BlockSpec((out_rows, LANE), lambda b, h: (b, h)),
        out_shape=jax.ShapeDtypeStruct((tiles.count * tq, H_A * DV), ACT_DT),
        scratch_shapes=[pltpu.VMEM((seq, 2 * DV), MXU_DT), pltpu.VMEM((2, 2, tq, seq), F32),
                        pltpu.VMEM((2, 2, tq, LANE), F32)],
        compiler_params=_cparams("parallel", "parallel"),
        name="diff_attention",
    )(diff_lambda, g_sub, p, k_t, p)


def _gmlp_kernel(uv_ref, gv_ref, ws_ref, bs_ref, o_ref):
    uv = jax.nn.gelu(uv_ref[...])
    u = uv[:, :W_B].astype(F32)
    v = (_rms(uv[:, W_B:].astype(F32)) * gv_ref[...]).astype(MXU_DT)
    gw = W_B // G_B
    for c in range(uv.shape[0] // CHUNK_B):
        rows = slice(c * CHUNK_B, (c + 1) * CHUNK_B)
        for g in range(G_B):
            cols = slice(g * gw, (g + 1) * gw)
            mixed = jnp.dot(ws_ref[g], v[rows, cols], preferred_element_type=F32) + bs_ref[:, cols]
            o_ref[rows, cols] = (u[rows, cols] * mixed).astype(o_ref.dtype)


def _gmlp(p, g_v, w_s, b_full, tiles):
    tm = ROW_TILE
    r = tiles.count * tm
    return pl.pallas_call(
        _gmlp_kernel,
        grid=(tiles.count,),
        in_specs=[
            pl.BlockSpec((tm, 2 * W_B), lambda i: (tiles.full(i), P_UV // (2 * W_B))),
            pl.BlockSpec((1, W_B), lambda i: (0, 0)),
            pl.BlockSpec((G_B, CHUNK_B, CHUNK_B), lambda i: (0, 0, 0)),
            pl.BlockSpec((CHUNK_B, W_B), lambda i: (0, 0)),
        ],
        out_specs=pl.BlockSpec((tm, W_B), lambda i: (i, 0)),
        out_shape=jax.ShapeDtypeStruct((r, W_B), ACT_DT),
        compiler_params=_cparams("parallel"),
        name="gmlp",
    )(p, g_v, w_s, b_full)


def _mfeat_kernel(x_ref, prev_ref, next_ref, wc_ref, bc_ref, wqkv_ref, wif_ref, bif_ref,
                  xc_ref, q_ref, k_ref, v_ref, g_ref, *, tps):
    tm = x_ref.shape[0]
    j = pl.program_id(0) % tps
    seg_start = jnp.logical_or(j == 0, j == 1)
    seg_end = jnp.logical_or(j == 0, j == tps - 1)
    x = x_ref[...].astype(F32)
    prow = jnp.where(seg_start, 0.0, prev_ref[...].astype(F32)[HALO - 1:HALO, :])
    nrow = jnp.where(seg_end, 0.0, next_ref[...].astype(F32)[0:1, :])
    rid = lax.broadcasted_iota(jnp.int32, x.shape, 0)
    xp = jnp.where(rid == 0, prow, pltpu.roll(x, 1, 0))
    xn = jnp.where(rid == tm - 1, nrow, pltpu.roll(x, tm - 1, 0))
    xc = wc_ref[0:1, :] * xp + wc_ref[1:2, :] * x + wc_ref[2:3, :] * xn + bc_ref[...]
    xc = xc * _sigmoid(xc)
    xc_ref[...] = xc.astype(xc_ref.dtype)
    xcb = xc.astype(MXU_DT)
    xb = x_ref[...]
    qs, ks, vs = [], [], []
    for h in range(H_C):
        cols = slice(h * DH_C, (h + 1) * DH_C)
        qs.append(jnp.dot(xcb[:, cols], wqkv_ref[0, h], preferred_element_type=F32))
        ks.append(jnp.dot(xcb[:, cols], wqkv_ref[1, h], preferred_element_type=F32))
        vs.append(jnp.dot(xb[:, cols], wqkv_ref[2, h], preferred_element_type=F32))
    q = jnp.concatenate(qs, axis=1)
    k = jnp.concatenate(ks, axis=1)
    v = jnp.concatenate(vs, axis=1)
    qkv = jnp.concatenate([q, k, v], axis=1).astype(MXU_DT)
    gates = jnp.dot(qkv, wif_ref[...], preferred_element_type=F32) + bif_ref[...]
    g_ref[...] = gates.T[0:g_ref.shape[0], :]
    q_ref[...] = q.T.astype(q_ref.dtype)
    k_ref[...] = (k * (DH_C ** -0.5)).astype(k_ref.dtype)
    v_ref[...] = v.T.astype(v_ref.dtype)


def _mlstm_features(p, w_conv, b_conv, w_qkv, w_if_pad, b_if_pad, tps):
    r = p.shape[0]
    tm = ROW_TILE
    hpt = tm // HALO
    xm_blk = P_XM // W_C
    last_halo = r // HALO - 1
    kern = functools.partial(_mfeat_kernel, tps=tps)
    act = jax.ShapeDtypeStruct((r, W_C), ACT_DT)
    act_t = jax.ShapeDtypeStruct((W_C, r), ACT_DT)
    row_spec = pl.BlockSpec((tm, W_C), lambda i: (i, 0))
    col_spec = pl.BlockSpec((W_C, tm), lambda i: (0, i))
    return pl.pallas_call(
        kern,
        grid=(r // tm,),
        in_specs=[
            pl.BlockSpec((tm, W_C), lambda i: (i, xm_blk)),
            pl.BlockSpec((HALO, W_C), lambda i: (jnp.maximum(i * hpt - 1, 0), xm_blk)),
            pl.BlockSpec((HALO, W_C), lambda i: (jnp.minimum((i + 1) * hpt, last_halo), xm_blk)),
            pl.BlockSpec((CONV_K, W_C), lambda i: (0, 0)),
            pl.BlockSpec((1, W_C), lambda i: (0, 0)),
            pl.BlockSpec((3, H_C, DH_C, DH_C), lambda i: (0, 0, 0, 0)),
            pl.BlockSpec((3 * W_C, LANE), lambda i: (0, 0)),
            pl.BlockSpec((1, LANE), lambda i: (0, 0)),
        ],
        out_specs=[row_spec, col_spec, row_spec, col_spec,
                   pl.BlockSpec((2 * GATE_ROWS, tm), lambda i: (0, i))],
        out_shape=[act, act_t, act, act_t, jax.ShapeDtypeStruct((2 * GATE_ROWS, r), F32)],
        compiler_params=_cparams("parallel"),
        name="mlstm_features",
    )(p, p, p, w_conv, b_conv, w_qkv, w_if_pad, b_if_pad)


GATE_ROWS = 2 * H_C
GATE_PACK = ("r", "cmx", "bcum")


def _gate_prep_kernel(g_ref, pack_ref, rt_ref):
    seg = MLSTM_CHUNK
    gt_fwd = g_ref[0:GATE_ROWS, :]
    gt_bwd = g_ref[GATE_ROWS:2 * GATE_ROWS, :]
    n = gt_fwd.shape[1]
    lane = lax.broadcasted_iota(jnp.int32, gt_fwd.shape, 1)
    in_seg = lane & (seg - 1)

    def scan_both(xf, xb, op):
        sh = 1
        while sh < seg:
            xf = jnp.where(in_seg >= sh, op(xf, pltpu.roll(xf, sh, 1)), xf)
            xb = jnp.where(in_seg < seg - sh, op(xb, pltpu.roll(xb, n - sh, 1)), xb)
            sh *= 2
        return xf, xb

    bcum_f, bcum_b = scan_both(jax.nn.log_sigmoid(pltpu.roll(gt_fwd, H_C, 0)),
                               jax.nn.log_sigmoid(pltpu.roll(gt_bwd, H_C, 0)), jnp.add)
    r_f, r_b = gt_fwd - bcum_f, gt_bwd - bcum_b
    cmx_f, cmx_b = scan_both(r_f, r_b, jnp.maximum)
    pack_ref[...] = jnp.concatenate([r_f, cmx_f, bcum_f, r_b, cmx_b, bcum_b], axis=0)
    rt_ref[...] = jnp.concatenate([r_f, r_b, jnp.zeros((LANE - 2 * GATE_ROWS, n), F32)], axis=0).T


def _mlstm_gate_prep(gates_t, n_batch, seq):
    rows = 2 * len(GATE_PACK) * GATE_ROWS
    r = gates_t.shape[1]
    return pl.pallas_call(
        _gate_prep_kernel,
        grid=(n_batch,),
        in_specs=[pl.BlockSpec((2 * GATE_ROWS, seq), lambda b: (0, b))],
        out_specs=[pl.BlockSpec((rows, seq), lambda b: (0, b)), pl.BlockSpec((seq, LANE), lambda b: (b, 0))],
        out_shape=[jax.ShapeDtypeStruct((rows, r), F32), jax.ShapeDtypeStruct((r, LANE), F32)],
        compiler_params=_cparams("parallel"),
        name="mlstm_gate_prep",
    )(gates_t)


def _mlstm_gates(pack, m_ref, reverse):
    ln = pack.shape[1]
    part = {name: pack[i * GATE_ROWS:(i + 1) * GATE_ROWS] for i, name in enumerate(GATE_PACK)}
    end = slice(0, 1) if reverse else slice(ln - 1, ln)
    g = part["bcum"][:, end]
    r_max = part["cmx"][:, end]
    m_loc = g + r_max
    w_end = jnp.exp(part["r"] - r_max)
    m_old = m_ref[:, 0:1]
    m_new = jnp.maximum(g + m_old, m_loc)
    a_old = jnp.exp(g + m_old - m_new)
    a_new = jnp.exp(m_loc - m_new)
    mx = jnp.maximum(m_old, part["cmx"])
    w_inter = jnp.exp(m_old - mx)
    e_neg = jnp.exp(-(part["bcum"] + mx))
    s_id = lax.broadcasted_iota(jnp.int32, (ln, ln), 0)
    t_id = lax.broadcasted_iota(jnp.int32, (ln, ln), 1)
    keep = (s_id >= t_id) if reverse else (s_id <= t_id)
    return dict(mx=mx, w_inter=w_inter, e_neg=e_neg, w_end=w_end, a_old=a_old, a_new=a_new,
                m_new=m_new, keep=keep)


def _mscan_chunk(gf_ref, rf_ref, qf_ref, kf_ref, vf_ref, gb_ref, rb_ref, qb_ref, kb_ref, vb_ref,
                 hf_ref, hb_ref, cf_ref, mf_ref, cb_ref, mb_ref):
    ln = qf_ref.shape[1]
    pr = len(GATE_PACK) * GATE_ROWS
    dirs = ((_mlstm_gates(gf_ref[0:pr, :], mf_ref, False), qf_ref, kf_ref, vf_ref, cf_ref, hf_ref, rf_ref),
            (_mlstm_gates(gb_ref[pr:2 * pr, :], mb_ref, True), qb_ref, kb_ref, vb_ref, cb_ref, hb_ref, rb_ref))
    pairs = [(d, h) for d in range(2) for h in range(H_C)]
    ones_row = jnp.where(lax.broadcasted_iota(jnp.int32, (DH_C, ln), 0) == 0, 1.0, 0.0).astype(MXU_DT)
    head = lambda h: slice(h * DH_C, (h + 1) * DH_C)
    row = lambda h: slice(h, h + 1)

    ks = [dirs[d][2][:, head(h)] for d, h in pairs]
    qts = [dirs[d][1][head(h), :] for d, h in pairs]
    vexts = [jnp.concatenate([dirs[d][3][head(h), :], ones_row], axis=0) for d, h in pairs]
    cexts = [dirs[d][4][h] for d, h in pairs]
    qks = [jnp.dot(k, q_t, preferred_element_type=F32) for k, q_t in zip(ks, qts)]
    rhss = []
    for i, (d, h) in enumerate(pairs):
        g = dirs[d][0]
        r_col = dirs[d][6][:, d * GATE_ROWS + h:d * GATE_ROWS + h + 1]
        d_t = jnp.exp(jnp.where(g["keep"], r_col - g["mx"][row(h), :], NEG_BIG))
        wq = (qts[i].astype(F32) * g["w_inter"][row(h), :]).astype(MXU_DT)
        rhss.append(jnp.concatenate([(qks[i] * d_t).astype(MXU_DT), wq], axis=0))
    ress = [jnp.dot(jnp.concatenate([vexts[i], cexts[i].astype(MXU_DT)], axis=1), rhss[i],
                    preferred_element_type=F32) for i in range(len(pairs))]
    for i, (d, h) in enumerate(pairs):
        g = dirs[d][0]
        den = ress[i][DH_C:DH_C + 1, :]
        hh = ress[i][:DH_C, :] / jnp.maximum(jnp.abs(den), g["e_neg"][row(h), :])
        dirs[d][5][head(h), :] = hh.astype(dirs[d][5].dtype)
    clocs = []
    for i, (d, h) in enumerate(pairs):
        vw_t = (vexts[i].astype(F32) * dirs[d][0]["w_end"][row(h), :]).astype(MXU_DT)
        clocs.append(jnp.dot(vw_t, ks[i], preferred_element_type=F32))
    for i, (d, h) in enumerate(pairs):
        g = dirs[d][0]
        dirs[d][4][h] = g["a_old"][row(h), :] * cexts[i] + g["a_new"][row(h), :] * clocs[i]
    mf_ref[...] = jnp.broadcast_to(dirs[0][0]["m_new"], mf_ref.shape)
    mb_ref[...] = jnp.broadcast_to(dirs[1][0]["m_new"], mb_ref.shape)


def _mscan_kernel(g_ref, r_ref, q_ref, k_ref, v_ref, hf_ref, hb_ref, cf_ref, mf_ref, cb_ref, mb_ref, *, ncc):
    ln = MLSTM_CHUNK
    nck = q_ref.shape[1] // ln
    cf_ref[...] = jnp.zeros_like(cf_ref)
    cb_ref[...] = jnp.zeros_like(cb_ref)
    mf_ref[...] = jnp.zeros_like(mf_ref)
    mb_ref[...] = jnp.zeros_like(mb_ref)

    def chunk_views(c):
        at = pl.ds(pl.multiple_of(c * ln, ln), ln)
        return ((g_ref.at[:, at], r_ref.at[at, :], q_ref.at[:, at], k_ref.at[at, :], v_ref.at[:, at]),
                (hf_ref.at[:, at], hb_ref.at[:, at]))

    def trip(s, carry):
        ins_f, (hf, _) = chunk_views(s)
        ins_b, (_, hb) = chunk_views(jnp.where(s < ncc, ncc - 1 - s, nck - 1 + ncc - s))
        _mscan_chunk(*ins_f, *ins_b, hf, hb, cf_ref, mf_ref, cb_ref, mb_ref)
        return carry

    lax.fori_loop(0, nck, trip, 0)


def _mlstm_scan(q_t, k, v_t, gate_pack, r_t, n_batch, seq, ctx_rows):
    rows = lambda w: pl.BlockSpec((seq, w), lambda b: (b, 0))
    cols = lambda h: pl.BlockSpec((h, seq), lambda b: (0, b))
    out = jax.ShapeDtypeStruct(q_t.shape, ACT_DT)
    return pl.pallas_call(
        functools.partial(_mscan_kernel, ncc=ctx_rows // MLSTM_CHUNK),
        grid=(n_batch,),
        in_specs=[cols(gate_pack.shape[0]), rows(LANE), cols(W_C), rows(W_C), cols(W_C)],
        out_specs=[cols(W_C), cols(W_C)],
        out_shape=[out, out],
        scratch_shapes=[
            pltpu.VMEM((H_C, 2 * DH_C, DH_C), F32), pltpu.VMEM((8, LANE), F32),
            pltpu.VMEM((H_C, 2 * DH_C, DH_C), F32), pltpu.VMEM((8, LANE), F32),
        ],
        compiler_params=_cparams("parallel"),
        name="mlstm_scan",
    )(gate_pack, r_t, q_t, k, v_t)


def _route(logits, carry):
    lane = lax.broadcasted_iota(jnp.int32, logits.shape, 1).astype(F32)
    big = float(4 * LANE)
    gl = jnp.where(lane < N_GROUPS, logits, NEG_BIG)
    gmax = jnp.max(gl, axis=-1, keepdims=True)
    g_star = jnp.min(jnp.where(gl == gmax, lane, big), axis=-1, keepdims=True)
    p_g = 1.0 / jnp.sum(jnp.exp(gl - gmax), axis=-1, keepdims=True)
    e_lo = g_star * EXP_PER_GROUP + N_GROUPS
    in_group = jnp.logical_and(lane >= e_lo, lane < e_lo + EXP_PER_GROUP)
    el = jnp.where(in_group, logits, NEG_BIG)
    v1 = jnp.max(el, axis=-1, keepdims=True)
    i1 = jnp.min(jnp.where(el == v1, lane, big), axis=-1, keepdims=True)
    el2 = jnp.where(lane == i1, NEG_BIG, el)
    v2 = jnp.max(el2, axis=-1, keepdims=True)
    i2 = jnp.min(jnp.where(el2 == v2, lane, big), axis=-1, keepdims=True)
    e21 = jnp.exp(v2 - v1)
    w1 = p_g / (1.0 + e21)
    w2 = p_g * e21 / (1.0 + e21)
    e1 = i1 - N_GROUPS
    e2 = i2 - N_GROUPS
    oh1 = lane == e1
    oh2 = lane == e2
    both = jnp.where(jnp.logical_or(oh1, oh2), 1.0, 0.0)
    tm = logits.shape[0]
    lower = (lax.broadcasted_iota(jnp.int32, (tm, tm), 1) < lax.broadcasted_iota(jnp.int32, (tm, tm), 0))
    before = jnp.dot(jnp.where(lower, 1.0, 0.0).astype(MXU_DT), both.astype(MXU_DT),
                     preferred_element_type=F32) + carry
    rank1 = jnp.sum(jnp.where(oh1, before, 0.0), axis=-1, keepdims=True)
    rank2 = jnp.sum(jnp.where(oh2, before, 0.0), axis=-1, keepdims=True)
    out = jnp.where(lane == 0, e1, 0.0)
    out = jnp.where(lane == 1, e2, out)
    out = jnp.where(lane == 2, w1, out)
    out = jnp.where(lane == 3, w2, out)
    out = jnp.where(lane == 4, rank1, out)
    out = jnp.where(lane == 5, rank2, out)
    return out, carry + jnp.sum(both, axis=0, keepdims=True)


def _merge_kernel(x_ref, gate_ref, z_ref, ya_ref, yb_ref, hf_ref, hb_ref, xc_ref, gm_ref, sk_ref,
                  wbr_ref, wo_ref, mod_ref, g2_ref, wr_ref, br_ref, xo_ref, rt_ref, cnt_ref, carry_ref):
    @pl.when(pl.program_id(0) == 0)
    def _():
        carry_ref[...] = jnp.zeros_like(carry_ref)

    hs_t = hf_ref[...].astype(F32) + hb_ref[...].astype(F32)
    hn_t = []
    for h in range(H_C):
        part = hs_t[h * DH_C:(h + 1) * DH_C, :]
        hn_t.append(part * lax.rsqrt(jnp.mean(part * part, axis=0, keepdims=True) + EPS))
    hn = jnp.concatenate(hn_t, axis=0).T * gm_ref[...]
    yc = (hn + sk_ref[...] * xc_ref[...].astype(F32)) * _sigmoid(z_ref[...].astype(F32))
    ys = (ya_ref[...], yb_ref[...], yc.astype(MXU_DT))
    merged = None
    for i in range(3):
        gate = _sigmoid(gate_ref[:, i * D_MODEL:(i + 1) * D_MODEL].astype(MXU_DT))
        term = gate * jnp.dot(ys[i], wbr_ref[i], preferred_element_type=F32).astype(MXU_DT)
        merged = term if merged is None else merged + term
    y = jnp.dot(merged, wo_ref[...], preferred_element_type=F32)
    xn = x_ref[...] + mod_ref[2:3, :] * y
    xo_ref[...] = xn
    h2 = _rms(xn) * g2_ref[...] * (1.0 + mod_ref[4:5, :]) + mod_ref[3:4, :]
    h_hi, h_lo = _split_bf16(h2)
    logits = (jnp.dot(h_hi, wr_ref[0], preferred_element_type=F32)
              + jnp.dot(h_lo, wr_ref[0], preferred_element_type=F32)
              + jnp.dot(h_hi, wr_ref[1], preferred_element_type=F32)) + br_ref[...]
    route, carry = _route(logits, carry_ref[0:1, :])
    rt_ref[...] = route
    carry_ref[0:1, :] = carry
    cnt_ref[...] = jnp.broadcast_to(carry, cnt_ref.shape)


def _merge(xall, p, ya, yb, hf, hb, xconv, g_m, skip, w_br, w_o, mod, g2, w_route, b_route, tiles):
    d = xall.shape[1]
    tm = ROW_TILE
    r = tiles.count * tm

    def mod_idx(i):
        return (tiles.mod_row(i), 0, 0)

    row = lambda w: pl.BlockSpec((tm, w), lambda i: (i, 0))
    frow = lambda w: pl.BlockSpec((tm, w), lambda i: (tiles.full(i), 0))
    fcol = lambda w: pl.BlockSpec((w, tm), lambda i: (0, tiles.full(i)))
    full = lambda shape: pl.BlockSpec(shape, lambda i: (0,) * len(shape))
    return pl.pallas_call(
        _merge_kernel,
        grid=(tiles.count,),
        in_specs=[
            frow(d),
            pl.BlockSpec((tm, GATE_W), lambda i: (tiles.full(i), P_GATE // GATE_W)),
            pl.BlockSpec((tm, W_C), lambda i: (tiles.full(i), P_Z // W_C)),
            row(W_BR), row(W_BR), fcol(W_C), fcol(W_C), frow(W_C),
            full((1, W_C)), full((1, W_C)),
            full((3, W_BR, d)), full((d, d)),
            pl.BlockSpec((None, N_MOD, d), mod_idx),
            full((1, d)), full((2, d, LANE)), full((1, LANE)),
        ],
        out_specs=[row(d), row(LANE), full((8, LANE))],
        out_shape=[jax.ShapeDtypeStruct((r, d), F32), jax.ShapeDtypeStruct((r, LANE), F32),
                   jax.ShapeDtypeStruct((8, LANE), F32)],
        scratch_shapes=[pltpu.VMEM((8, LANE), F32)],
        compiler_params=_cparams("arbitrary"),
        name="merge_route",
    )(xall, p, p, ya, yb, hf, hb, xconv, g_m, skip, w_br, w_o, mod, g2, w_route, b_route)


def _row_copy(src_hbm, src_row, dst_ref, dst_row, sem):
    return pltpu.make_async_copy(src_hbm.at[pl.ds(src_row, 1), :], dst_ref.at[pl.ds(dst_row, 1), :], sem)


def _dispatch_kernel(fill_ref, nlive_ref, pos_ref, x_ref, g2_ref, mod_ref, xs_hbm, hbuf, zbuf, zsem, sem):
    i = pl.program_id(0)
    n = pl.num_programs(0)
    tm = x_ref.shape[0]
    slot = i % 2
    bm = zbuf.shape[0]
    n_blocks = xs_hbm.shape[0] // bm

    @pl.when(i == 0)
    def _():
        zbuf[...] = jnp.zeros_like(zbuf)

        def fill_block(b):
            return pltpu.make_async_copy(zbuf, xs_hbm.at[pl.ds(b * bm, bm), :], zsem)

        def fill_last(k, carry):
            fill_block(fill_ref[k]).start()
            return carry

        def fill_unused(b, carry):
            fill_block(b).start()
            return carry

        def wait_fill(k, carry):
            fill_block(0).wait()
            return carry

        n_live = nlive_ref[0]
        n_used = nlive_ref[1]
        lax.fori_loop(0, n_used, fill_last, 0)
        lax.fori_loop(n_live, n_blocks, fill_unused, 0)
        lax.fori_loop(0, n_used + n_blocks - n_live, wait_fill, 0)

    hbuf[slot] = _rms(x_ref[...]) * g2_ref[...] * (1.0 + mod_ref[4:5, :]) + mod_ref[3:4, :]

    for r in range(tm):
        for j in range(TOP_K):
            pltpu.make_async_copy(hbuf.at[slot, pl.ds(r, 1), :], xs_hbm.at[pl.ds(pos_ref[0, j * tm + r], 1), :],
                                  sem.at[slot]).start(priority=j % N_DMA_PRIORITIES)

    def wait_tile(s):
        pltpu.make_async_copy(hbuf.at[s], xs_hbm.at[pl.ds(0, tm), :], sem.at[s]).wait()
        pltpu.make_async_copy(hbuf.at[s], xs_hbm.at[pl.ds(0, tm), :], sem.at[s]).wait()

    @pl.when(i > 0)
    def _():
        wait_tile(1 - slot)

    @pl.when(i == n - 1)
    def _():
        wait_tile(slot)


def _moe_dispatch(xall, g2, mod, fill_blocks, n_live, pos_tiles, n_slots, tiles):
    r, d = xall.shape
    tm = ROW_TILE
    assert TOP_K == 2 and r == tiles.count * tm

    def mod_idx(i, fb, nl):
        return (tiles.mod_row(i), 0, 0)

    grid_spec = pltpu.PrefetchScalarGridSpec(
        num_scalar_prefetch=2,
        grid=(r // tm,),
        in_specs=[
            pl.BlockSpec((None, 1, TOP_K * tm), lambda i, fb, nl: (i, 0, 0), memory_space=pltpu.SMEM),
            pl.BlockSpec((tm, d), lambda i, fb, nl: (i, 0)),
            pl.BlockSpec((1, d), lambda i, fb, nl: (0, 0)),
            pl.BlockSpec((None, N_MOD, d), mod_idx),
        ],
        out_specs=pl.BlockSpec(memory_space=pl.ANY),
        scratch_shapes=[pltpu.VMEM((2, tm, d), F32), pltpu.VMEM((MOE_BLOCK, d), F32),
                        pltpu.SemaphoreType.DMA(()), pltpu.SemaphoreType.DMA((2,))],
    )
    return pl.pallas_call(
        _dispatch_kernel,
        grid_spec=grid_spec,
        out_shape=jax.ShapeDtypeStruct((n_slots, d), F32),
        compiler_params=_cparams("arbitrary"),
        name="moe_dispatch",
    )(fill_blocks, n_live, pos_tiles, xall, g2, mod)


def _moe_kernel(be_ref, nv_ref, ne_ref, x_ref, wg_hbm, wu_hbm, wd_hbm, y_ref,
                wg_mx, wu_mx, wd_mx, wg_in, wu_in, wd_in, par_ref, sem, *, layer):
    i = pl.program_id(0)
    n_valid = nv_ref[i]

    def fetch(e, slot):
        return [pltpu.make_async_copy(src.at[layer, e], dst.at[slot], sem.at[slot])
                for src, dst in ((wg_hbm, wg_in), (wu_hbm, wu_in), (wd_hbm, wd_in))]

    @pl.when(i == 0)
    def _():
        par_ref[0] = 0
        for cp in fetch(be_ref[0], 0):
            cp.start()

    run_start = jnp.logical_or(i == 0, be_ref[i] != be_ref[jnp.maximum(i - 1, 0)])

    @pl.when(jnp.logical_and(n_valid > 0, run_start))
    def _():
        slot = par_ref[0]
        for cp in fetch(be_ref[i], slot):
            cp.wait()
        wg_mx[...] = wg_in[slot].astype(wg_mx.dtype)
        wu_mx[...] = wu_in[slot].astype(wu_mx.dtype)
        wd_mx[...] = wd_in[slot].astype(wd_mx.dtype)
        nxt = ne_ref[i]

        @pl.when(nxt >= 0)
        def _():
            for cp in fetch(nxt, 1 - slot):
                cp.start()

        par_ref[0] = 1 - slot

    @pl.when(n_valid > 0)
    def _():
        x = x_ref[...].astype(MXU_DT)
        a = jnp.dot(x, wg_mx[...], preferred_element_type=F32)
        u = jnp.dot(x, wu_mx[...], preferred_element_type=F32)
        act = (a * _sigmoid(a) * u).astype(MXU_DT)
        y_ref[...] = jnp.dot(act, wd_mx[...], preferred_element_type=F32)

    @pl.when(n_valid <= 0)
    def _():
        y_ref[...] = jnp.zeros_like(y_ref)


def _moe_experts(xs, block_e, n_valid, next_e, w_gate, w_up, w_down, layer):
    n_slots, d = xs.shape
    bm = MOE_BLOCK
    wdt = w_gate.dtype
    grid_spec = pltpu.PrefetchScalarGridSpec(
        num_scalar_prefetch=3,
        grid=(n_slots // bm,),
        in_specs=[
            pl.BlockSpec((bm, d), lambda i, be, nv, ne: (i, 0)),
            pl.BlockSpec(memory_space=pl.ANY), pl.BlockSpec(memory_space=pl.ANY), pl.BlockSpec(memory_space=pl.ANY),
        ],
        out_specs=pl.BlockSpec((bm, d), lambda i, be, nv, ne: (i, 0)),
        scratch_shapes=[pltpu.VMEM((d, D_EXPERT), MXU_DT), pltpu.VMEM((d, D_EXPERT), MXU_DT),
                        pltpu.VMEM((D_EXPERT, d), MXU_DT),
                        pltpu.VMEM((2, d, D_EXPERT), wdt), pltpu.VMEM((2, d, D_EXPERT), wdt),
                        pltpu.VMEM((2, D_EXPERT, d), wdt),
                        pltpu.SMEM((1,), jnp.int32), pltpu.SemaphoreType.DMA((2,))],
    )
    return pl.pallas_call(
        functools.partial(_moe_kernel, layer=layer),
        grid_spec=grid_spec,
        out_shape=jax.ShapeDtypeStruct((n_slots, d), F32),
        compiler_params=_cparams("arbitrary"),
        name="moe_experts",
    )(block_e, n_valid, next_e, xs, w_gate, w_up, w_down)


def _combine_kernel(pos_ref, pos_next_ref, x_ref, rt_ref, mod_ref, gf_ref, yb_hbm, xo_ref, ybuf, sem, *, final):
    i = pl.program_id(0)
    n = pl.num_programs(0)
    tm = x_ref.shape[0]

    def start_gather(ids_ref, slot):
        def body(c, carry):
            for u in range(DMA_UNROLL):
                r = c * DMA_UNROLL + u
                _row_copy(yb_hbm, ids_ref[0, r], ybuf.at[slot], r, sem.at[slot]).start()
            return carry
        lax.fori_loop(0, TOP_K * tm // DMA_UNROLL, body, 0)

    def wait_gather(slot):
        pltpu.make_async_copy(yb_hbm.at[pl.ds(0, TOP_K * tm), :], ybuf.at[slot], sem.at[slot]).wait()

    @pl.when(i == 0)
    def _():
        start_gather(pos_ref, 0)

    slot = i % 2
    wait_gather(slot)
    per = TOP_K * tm // COMBINE_GROUPS
    rows_per = tm // COMBINE_GROUPS
    for c in range(COMBINE_GROUPS):
        for r in range(c * per, (c + 1) * per):
            _row_copy(yb_hbm, pos_next_ref[0, r], ybuf.at[1 - slot], r,
                      sem.at[1 - slot]).start(priority=r % N_DMA_PRIORITIES)
        rows = slice(c * rows_per, (c + 1) * rows_per)
        rt = rt_ref[rows, :]
        f = (rt[:, 2:3] * ybuf[slot, rows, :]
             + rt[:, 3:4] * ybuf[slot, tm + c * rows_per:tm + (c + 1) * rows_per, :])
        xn = x_ref[rows, :] + mod_ref[5:6, :] * f
        xo_ref[rows, :] = _rms(xn) * gf_ref[...] if final else xn

    @pl.when(i == n - 1)
    def _():
        wait_gather(1 - slot)


def _combine(xall, route, mod, g_final, yb, pos, tiles, final):
    r, d = xall.shape
    tm = ROW_TILE
    nt = tiles.count
    last = nt - 1
    assert r == nt * tm

    def mod_idx(i):
        return (tiles.mod_row(i), 0, 0)

    row = lambda w: pl.BlockSpec((tm, w), lambda i: (i, 0))
    return pl.pallas_call(
        functools.partial(_combine_kernel, final=final),
        grid=(nt,),
        in_specs=[
            pl.BlockSpec((None, 1, TOP_K * tm), lambda i: (i, 0, 0), memory_space=pltpu.SMEM),
            pl.BlockSpec((None, 1, TOP_K * tm), lambda i: (jnp.minimum(i + 1, last), 0, 0),
                         memory_space=pltpu.SMEM),
            row(d), row(LANE),
            pl.BlockSpec((None, N_MOD, d), mod_idx),
            pl.BlockSpec((1, d), lambda i: (0, 0)),
            pl.BlockSpec(memory_space=pl.ANY),
        ],
        out_specs=row(d),
        out_shape=jax.ShapeDtypeStruct((r, d), F32),
        scratch_shapes=[pltpu.VMEM((2, TOP_K * tm, d), F32), pltpu.SemaphoreType.DMA((2,))],
        compiler_params=_cparams("arbitrary"),
        name="moe_combine",
    )(pos, pos, xall, route, mod, g_final, yb)


def _dispatch_plan(route, counts_f, n_tiles):
    n = route.shape[0]
    bm = MOE_BLOCK
    n_blocks = -(-(n * TOP_K + N_EXPERTS * (bm - 1)) // bm)
    counts = counts_f[:N_EXPERTS].astype(jnp.int32)
    padded = (counts + bm - 1) // bm * bm
    pend = jnp.cumsum(padded)
    seg_off = pend - padded
    eid = route[:, 0:TOP_K].astype(jnp.int32)
    rank = route[:, 4:4 + TOP_K].astype(jnp.int32)
    experts = jnp.arange(N_EXPERTS, dtype=jnp.int32)
    pos = jnp.sum(jnp.where(eid[..., None] == experts, seg_off, 0), axis=-1) + rank
    blk0 = jnp.arange(n_blocks, dtype=jnp.int32) * bm
    block_e = jnp.minimum(jnp.sum((blk0[:, None] >= pend[None, :]).astype(jnp.int32), axis=1), N_EXPERTS - 1)
    seg_end = seg_off + counts
    n_valid = jnp.clip(seg_end[block_e] - blk0, 0, bm).astype(jnp.int32)
    pos_tiles = pos.reshape(n_tiles, n // n_tiles, TOP_K).transpose(0, 2, 1).reshape(n_tiles, 1, -1)
    used = counts > 0
    last_blocks = jnp.sort(jnp.where(used, pend // bm - 1, n_blocks)).astype(jnp.int32)
    n_live = jnp.stack([pend[-1] // bm, jnp.sum(used)]).astype(jnp.int32)
    later_used = jnp.logical_and(experts[None, :] > experts[:, None], used[None, :])
    next_used = jnp.min(jnp.where(later_used, experts[None, :], N_EXPERTS), axis=1)
    next_e = jnp.where(next_used < N_EXPERTS, next_used, -1)[block_e].astype(jnp.int32)
    return block_e.astype(jnp.int32), n_valid, next_e, last_blocks, n_live, pos_tiles, n_blocks * bm


def _rope_tables(seq, ctx_rows):
    t = jnp.arange(seq - ctx_rows)
    inv = ROPE_BASE ** (-jnp.arange(0, ROT_AX, 2, dtype=F32) / ROT_AX)
    ang_r = (t // GRID_W).astype(F32)[:, None] * inv
    ang_c = (t % GRID_W).astype(F32)[:, None] * inv
    cos64 = jnp.concatenate([jnp.cos(ang_r)] * 2 + [jnp.cos(ang_c)] * 2, axis=1)
    sin64 = jnp.concatenate([-jnp.sin(ang_r), jnp.sin(ang_r), -jnp.sin(ang_c), jnp.sin(ang_c)], axis=1)
    cos = jnp.concatenate([jnp.ones((ctx_rows, DK), F32), cos64], axis=0)
    sin = jnp.concatenate([jnp.zeros((ctx_rows, DK), F32), sin64], axis=0)
    return jnp.tile(cos, (1, LANE // DK)), jnp.tile(sin, (1, LANE // DK))


def kernel(x, c, ctx, c_ctx, w_mod, b_mod, g_norm1, g_norm2, w_in, diff_lambda, g_diff_subln, w_gmlp_s, b_gmlp_s, g_gmlp_v, w_conv_m, b_conv_m, w_qkv_m, w_if_m, b_if_m, g_mlstm_norm, skip_m, w_branch, w_out, w_route_g, b_route_g, w_route_e, b_route_e, w_e_gate, w_e_up, w_e_down, g_final):
    n_batch, t_lat, d = x.shape
    ctx_rows = ctx.shape[1]
    depth = w_in.shape[0]
    seq = ctx_rows + t_lat
    assert d == D_MODEL and ctx_rows == ROW_TILE and t_lat % ROW_TILE == 0 and t_lat % GRID_W == 0
    tps = seq // ROW_TILE
    all_tiles = _Tiles(n_batch, tps, False)

    xall = (ctx.reshape(n_batch * ctx_rows, d), x.reshape(n_batch * t_lat, d))
    mb = -(-(n_batch + 1) // 8) * 8
    c_all = jnp.zeros((mb, d), F32).at[:n_batch].set(c).at[n_batch].set(c_ctx)
    mod_all = _modulation(c_all, w_mod, b_mod)[:, :n_batch + 1].reshape(depth, n_batch + 1, N_MOD, d)
    cos, sin_signed = _rope_tables(seq, ctx_rows)
    w_in_b = w_in.astype(MXU_DT)

    for l in range(depth):
        lam_init = 0.8 - 0.6 * math.exp(-0.3 * l)
        mod = mod_all[l]
        last = l == depth - 1
        tiles = _Tiles(n_batch, tps, last)
        if l == 0:
            p, k_t, xall = _inproj(xall, g_norm1[l][None], mod, cos, sin_signed, w_in_b, l, all_tiles)
        else:
            p, k_t = _inproj(xall, g_norm1[l][None], mod, cos, sin_signed, w_in_b, l, all_tiles)

        ya = _attention(p, k_t, diff_lambda[l], g_diff_subln[l][None], lam_init, tiles, ctx_rows)

        b_full = jnp.repeat(b_gmlp_s[l].T, W_B // G_B, axis=1)
        yb = _gmlp(p, g_gmlp_v[l][None], w_gmlp_s[l].astype(MXU_DT), b_full, tiles)

        w_if_pad = jnp.zeros((3 * W_C, LANE), F32).at[:, :4 * H_C].set(w_if_m[l]).astype(MXU_DT)
        b_if_pad = jnp.zeros((1, LANE), F32).at[0, :4 * H_C].set(b_if_m[l])
        xconv, q_m, k_m, v_m, gates_t = _mlstm_features(
            p, w_conv_m[l], b_conv_m[l][None], w_qkv_m[l].astype(MXU_DT), w_if_pad, b_if_pad, tps)
        gate_pack, r_t = _mlstm_gate_prep(gates_t, n_batch, seq)
        hf, hb = _mlstm_scan(q_m, k_m, v_m, gate_pack, r_t, n_batch, seq, ctx_rows)

        w_route = (jnp.zeros((d, LANE), F32).at[:, :N_GROUPS].set(w_route_g[l])
                   .at[:, N_GROUPS:N_GROUPS + N_EXPERTS].set(w_route_e[l]))
        w_route = jnp.stack(_split_bf16(w_route))
        b_route = (jnp.zeros((1, LANE), F32).at[0, :N_GROUPS].set(b_route_g[l])
                   .at[0, N_GROUPS:N_GROUPS + N_EXPERTS].set(b_route_e[l]))
        xall, route, counts = _merge(xall, p, ya, yb, hf, hb, xconv, g_mlstm_norm[l][None], skip_m[l][None],
                                         w_branch[l].astype(MXU_DT), w_out[l].astype(MXU_DT), mod,
                                         g_norm2[l][None], w_route, b_route, tiles)

        block_e, n_valid, next_e, fill_blocks, n_live, pos_tiles, n_slots = _dispatch_plan(
            route, counts[0], tiles.count)
        xs = _moe_dispatch(xall, g_norm2[l][None], mod, fill_blocks, n_live, pos_tiles, n_slots, tiles)
        y_sorted = _moe_experts(xs, block_e, n_valid, next_e, w_e_gate, w_e_up, w_e_down, l)
        xall = _combine(xall, route, mod, g_final[None], y_sorted, pos_tiles, tiles, last)

    return xall.reshape(n_batch, t_lat, d)
```

```python
import functools
import math
from typing import NamedTuple

import jax
import jax.numpy as jnp
from jax import lax
from jax.experimental import pallas as pl
from jax.experimental.pallas import tpu as pltpu

F32 = jnp.float32
MXU_DT = jnp.bfloat16
ACT_DT = jnp.bfloat16
HIGHEST = lax.Precision.HIGHEST

D_MODEL = 1024
N_MOD = 6
EPS = 1e-6
GRID_W = 64
H_A, DK = 4, 64
DV = 2 * DK
ROT_AX = DK // 2
ROPE_BASE = 10000.0
W_B, G_B, CHUNK_B = 512, 4, 128
H_C, DH_C, CONV_K, MLSTM_CHUNK = 4, 128, 3, 128
W_C = H_C * DH_C
N_GROUPS, EXP_PER_GROUP, TOP_K, D_EXPERT = 4, 8, 2, 512
N_EXPERTS = N_GROUPS * EXP_PER_GROUP
W_BR = 512

GATE_W = 3 * D_MODEL
P_GATE = 0
P_K = GATE_W
P_V = P_K + 512
P_XM = P_V + 512
P_Q = P_XM + 512
P_UV = P_Q + 512
P_Z = P_UV + 2 * W_B
IN_COLS = P_Z + W_C
ORIG_GATE0 = IN_COLS - GATE_W

LANE = 128
ROW_TILE = 256
PROJ_CHUNK = 512
MOE_BLOCK = 512
HALO = 16
DMA_UNROLL = 8
ATTN_KV_CHUNK = 256
COMBINE_GROUPS = 8
N_DMA_PRIORITIES = 2
LOG2_E = math.log2(math.e)
VMEM_LIMIT = 56 * 1024 * 1024
NEG_BIG = -1e30


class _Tiles(NamedTuple):
    n_batch: int
    tps: int
    skip_ctx: bool

    @property
    def per_seq(self):
        return self.tps - 1 if self.skip_ctx else self.tps

    @property
    def count(self):
        return self.n_batch * self.per_seq

    def full(self, i):
        if not self.skip_ctx:
            return i
        return (i // self.per_seq) * self.tps + 1 + i % self.per_seq

    def mod_row(self, i):
        if self.skip_ctx:
            return i // self.per_seq
        return jnp.where(i % self.tps == 0, self.n_batch, i // self.tps)


def _cparams(*sem):
    return pltpu.CompilerParams(dimension_semantics=sem, vmem_limit_bytes=VMEM_LIMIT)


def _rms(x):
    return x * lax.rsqrt(jnp.mean(x * x, axis=-1, keepdims=True) + EPS)


def _sigmoid(x):
    return 0.5 * jnp.tanh(0.5 * x) + 0.5


def _split_bf16(x):
    hi = x.astype(jnp.bfloat16)
    return hi, (x - hi.astype(F32)).astype(jnp.bfloat16)


def _mod_kernel(c_ref, w_ref, b_ref, o_ref):
    c = c_ref[...]
    s = c * jax.nn.sigmoid(c)
    o_ref[...] = jnp.dot(s, w_ref[...], preferred_element_type=F32, precision=HIGHEST) + b_ref[...]


def _modulation(c_all, w_mod, b_mod):
    n_layer = w_mod.shape[0]
    mb, d = c_all.shape
    tn = 1024
    return pl.pallas_call(
        _mod_kernel,
        grid=(n_layer, N_MOD * d // tn),
        in_specs=[
            pl.BlockSpec((mb, d), lambda l, j: (0, 0)),
            pl.BlockSpec((None, d, tn), lambda l, j: (l, 0, j)),
            pl.BlockSpec((None, 1, tn), lambda l, j: (l, 0, j)),
        ],
        out_specs=pl.BlockSpec((None, mb, tn), lambda l, j: (l, 0, j)),
        out_shape=jax.ShapeDtypeStruct((n_layer, mb, N_MOD * d), F32),
        compiler_params=_cparams("parallel", "parallel"),
        name="modulation",
    )(c_all, w_mod, b_mod.reshape(n_layer, 1, N_MOD * d))


def _rope(acc, cos, sin_signed):
    w = acc.shape[1]
    half = ROT_AX // 2
    lane = lax.broadcasted_iota(jnp.int32, acc.shape, 1)
    partner = jnp.where((lane & half) == 0, pltpu.roll(acc, w - half, 1), pltpu.roll(acc, half, 1))
    reps = w // cos.shape[1]
    return acc * jnp.tile(cos, (1, reps)) + partner * jnp.tile(sin_signed, (1, reps))


def _inproj_first_kernel(ctx_ref, lat_ref, g_ref, mod_ref, cos_ref, sin_ref, w_ref, o_ref, kt_ref, xo_ref, *, tps):
    is_ctx = pl.program_id(0) % tps == 0
    xo_ref[...] = jnp.where(is_ctx, ctx_ref[...], lat_ref[...])
    _inproj_kernel(xo_ref, g_ref, mod_ref, cos_ref, sin_ref, w_ref, o_ref, kt_ref)


def _inproj_kernel(x_ref, g_ref, mod_ref, cos_ref, sin_ref, w_ref, o_ref, kt_ref):
    x = x_ref[...]
    h = (_rms(x) * g_ref[...] * (1.0 + mod_ref[1:2, :]) + mod_ref[0:1, :]).astype(MXU_DT)
    k_chunk = P_K // PROJ_CHUNK
    q_chunk = P_Q // PROJ_CHUNK
    chunks = range(IN_COLS // PROJ_CHUNK)
    cols = lambda c: slice(c * PROJ_CHUNK, (c + 1) * PROJ_CHUNK)

    def project(c):
        w0 = (ORIG_GATE0 + c * PROJ_CHUNK) % IN_COLS
        acc = jnp.dot(h, w_ref[:, w0:w0 + PROJ_CHUNK], preferred_element_type=F32)
        if c == k_chunk:
            acc = _rope(acc, cos_ref[...], sin_ref[...])
            kt_ref[...] = acc.T.astype(kt_ref.dtype)
        elif c == q_chunk:
            acc = _rope(acc, cos_ref[...], sin_ref[...]) * (DK ** -0.5 * LOG2_E)
        o_ref[:, cols(c)] = acc.astype(o_ref.dtype)

    for c in chunks:
        project(c)


def _inproj(xall, g1, mod, cos, sin_signed, w_in, layer, tiles):
    tm = ROW_TILE
    tps = tiles.tps
    r = tiles.count * tm
    assert ORIG_GATE0 % PROJ_CHUNK == 0 and not tiles.skip_ctx
    first = isinstance(xall, tuple)
    d = xall[0].shape[1] if first else xall.shape[1]

    def mod_idx(i):
        return (tiles.mod_row(i), 0, 0)

    row = pl.BlockSpec((tm, d), lambda i: (i, 0))
    if first:
        x_specs = [pl.BlockSpec((tm, d), lambda i: (i // tps, 0)),
                   pl.BlockSpec((tm, d), lambda i: ((i // tps) * (tps - 1) + jnp.maximum(i % tps - 1, 0), 0))]
        x_args = list(xall)
        kern = functools.partial(_inproj_first_kernel, tps=tps)
    else:
        x_specs, x_args, kern = [row], [xall], _inproj_kernel
    return pl.pallas_call(
        kern,
        grid=(r // tm,),
        in_specs=x_specs + [
            pl.BlockSpec((1, d), lambda i: (0, 0)),
            pl.BlockSpec((None, N_MOD, d), mod_idx),
            pl.BlockSpec((tm, LANE), lambda i: (i % tps, 0)),
            pl.BlockSpec((tm, LANE), lambda i: (i % tps, 0)),
            pl.BlockSpec((None, d, IN_COLS), lambda i: (layer, 0, 0)),
        ],
        out_specs=[pl.BlockSpec((tm, IN_COLS), lambda i: (i, 0)),
                   pl.BlockSpec((PROJ_CHUNK, tm), lambda i: (0, i))] + ([row] if first else []),
        out_shape=[jax.ShapeDtypeStruct((r, IN_COLS), ACT_DT), jax.ShapeDtypeStruct((PROJ_CHUNK, r), ACT_DT)]
        + ([jax.ShapeDtypeStruct((r, d), F32)] if first else []),
        compiler_params=_cparams("parallel"),
        name="inproj",
    )(*x_args, g1, mod, cos, sin_signed, w_in)


def _attn_kernel(dl_ref, g_ref, q_ref, kt_ref, v_ref, o_ref, vext_ref, s_ref, mx_ref, *,
                 lam_init, ctx_rows, has_ctx):
    tq = s_ref.shape[2]
    seq = kt_ref.shape[1]
    one_hot = lax.broadcasted_iota(jnp.int32, v_ref.shape, 1) == 0
    vext_ref[:, :DV] = v_ref[...]
    vext_ref[:, DV:] = jnp.where(one_hot, 1.0, 0.0).astype(vext_ref.dtype)

    dl = dl_ref[...]
    lam = (jnp.exp(jnp.sum(dl[0:1] * dl[1:2], keepdims=True))
           - jnp.exp(jnp.sum(dl[2:3] * dl[3:4], keepdims=True)) + lam_init)
    def chunk_list(kv_rows):
        kc = min(ATTN_KV_CHUNK, kv_rows)
        assert kv_rows % kc == 0
        return [slice(c * kc, (c + 1) * kc) for c in range(kv_rows // kc)]

    def score_pass(kv_rows, q_rows, buf):
        q = q_ref[q_rows, :]
        lane = lax.broadcasted_iota(jnp.int32, q.shape, 1)
        zero = jnp.zeros_like(q)
        qms = (jnp.where(lane < DK, q, zero), jnp.where(lane >= DK, q, zero))
        mruns = [None, None]
        for cols in chunk_list(kv_rows):
            for m in range(2):
                s = jnp.dot(qms[m], kt_ref[:, cols], preferred_element_type=F32)
                s_ref[buf, m, :, cols] = s
                for g in range(s.shape[1] // LANE):
                    blk = s[:, g * LANE:(g + 1) * LANE]
                    mruns[m] = blk if mruns[m] is None else jnp.maximum(mruns[m], blk)
            yield
        for m in range(2):
            mx = jnp.max(mruns[m], axis=-1, keepdims=True)
            mx_ref[buf, m] = jnp.broadcast_to(mx, mx_ref.shape[2:])
        yield

    def value_pass(kv_rows, out_rows, buf):
        mxs = [mx_ref[buf, m][:, 0:1] for m in range(2)]
        accs = [jnp.zeros((tq, 2 * DV), F32)] * 2
        for cols in chunk_list(kv_rows):
            for m in range(2):
                p = jnp.exp2(s_ref[buf, m, :, cols] - mxs[m])
                accs[m] = accs[m] + jnp.dot(p.astype(MXU_DT), vext_ref[cols, :], preferred_element_type=F32)
            yield
        o = (accs[0][:, :DV] * (1.0 / accs[0][:, DV:DV + 1])
             - accs[1][:, :DV] * (lam / accs[1][:, DV:DV + 1]))
        o_ref[out_rows, :] = (_rms(o) * g_ref[...] * (1.0 - lam_init)).astype(o_ref.dtype)
        yield

    def run(*passes):
        live = list(passes)
        while live:
            live = [g for g in live if next(g, "done") != "done"]

    n_lat = (seq - ctx_rows) // tq
    q_at = lambda i: pl.ds(pl.multiple_of(ctx_rows + i * tq, tq), tq)
    assert n_lat % 2 == 0
    out_off = 0
    if has_ctx:
        out_off = ctx_rows
        run(score_pass(ctx_rows, slice(0, ctx_rows), 1))
        run(score_pass(seq, q_at(0), 0), value_pass(ctx_rows, slice(0, ctx_rows), 1))
    else:
        run(score_pass(seq, q_at(0), 0))
    o_at = lambda i: pl.ds(pl.multiple_of(out_off + i * tq, tq), tq)

    def tile_pair(j, carry):
        a = 2 * j
        run(score_pass(seq, q_at(a + 1), 1), value_pass(seq, o_at(a), 0))
        run(score_pass(seq, q_at(a + 2), 0), value_pass(seq, o_at(a + 1), 1))
        return carry

    lax.fori_loop(0, n_lat // 2 - 1, tile_pair, 0)
    run(score_pass(seq, q_at(n_lat - 1), 1), value_pass(seq, o_at(n_lat - 2), 0))
    run(value_pass(seq, o_at(n_lat - 1), 1))


def _attention(p, k_t, diff_lambda, g_sub, lam_init, tiles, ctx_rows):
    tq = ROW_TILE
    tps = tiles.tps
    seq = tps * tq
    out_rows = tiles.per_seq * tq
    assert ctx_rows == tq
    kern = functools.partial(_attn_kernel, lam_init=lam_init, ctx_rows=ctx_rows, has_ctx=not tiles.skip_ctx)
    return pl.pallas_call(
        kern,
        grid=(tiles.n_batch, H_A),
        in_specs=[
            pl.BlockSpec((4, DK), lambda b, h: (0, 0)),
            pl.BlockSpec((1, DV), lambda b, h: (0, 0)),
            pl.BlockSpec((seq, LANE), lambda b, h: (b, P_Q // LANE + h)),
            pl.BlockSpec((LANE, seq), lambda b, h: (h, b)),
            pl.BlockSpec((seq, LANE), lambda b, h: (b, P_V // LANE + h)),
        ],
        out_specs=pl.BlockSpec((out_rows, LANE), lambda b, h: (b, h)),
        out_shape=jax.ShapeDtypeStruct((tiles.count * tq, H_A * DV), ACT_DT),
        scratch_shapes=[pltpu.VMEM((seq, 2 * DV), MXU_DT), pltpu.VMEM((2, 2, tq, seq), F32),
                        pltpu.VMEM((2, 2, tq, LANE), F32)],
        compiler_params=_cparams("parallel", "parallel"),
        name="diff_attention",
    )(diff_lambda, g_sub, p, k_t, p)


def _gmlp_kernel(uv_ref, gv_ref, ws_ref, bs_ref, o_ref):
    uv = jax.nn.gelu(uv_ref[...])
    u = uv[:, :W_B].astype(F32)
    v = (_rms(uv[:, W_B:].astype(F32)) * gv_ref[...]).astype(MXU_DT)
    gw = W_B // G_B
    for c in range(uv.shape[0] // CHUNK_B):
        rows = slice(c * CHUNK_B, (c + 1) * CHUNK_B)
        for g in range(G_B):
            cols = slice(g * gw, (g + 1) * gw)
            mixed = jnp.dot(ws_ref[g], v[rows, cols], preferred_element_type=F32) + bs_ref[:, cols]
            o_ref[rows, cols] = (u[rows, cols] * mixed).astype(o_ref.dtype)


def _gmlp(p, g_v, w_s, b_full, tiles):
    tm = ROW_TILE
    r = tiles.count * tm
    return pl.pallas_call(
        _gmlp_kernel,
        grid=(tiles.count,),
        in_specs=[
            pl.BlockSpec((tm, 2 * W_B), lambda i: (tiles.full(i), P_UV // (2 * W_B))),
            pl.BlockSpec((1, W_B), lambda i: (0, 0)),
            pl.BlockSpec((G_B, CHUNK_B, CHUNK_B), lambda i: (0, 0, 0)),
            pl.BlockSpec((CHUNK_B, W_B), lambda i: (0, 0)),
        ],
        out_specs=pl.BlockSpec((tm, W_B), lambda i: (i, 0)),
        out_shape=jax.ShapeDtypeStruct((r, W_B), ACT_DT),
        compiler_params=_cparams("parallel"),
        name="gmlp",
    )(p, g_v, w_s, b_full)


def _mfeat_kernel(x_ref, prev_ref, next_ref, wc_ref, bc_ref, wqkv_ref, wif_ref, bif_ref,
                  xc_ref, q_ref, k_ref, v_ref, g_ref, *, tps):
    tm = x_ref.shape[0]
    j = pl.program_id(0) % tps
    seg_start = jnp.logical_or(j == 0, j == 1)
    seg_end = jnp.logical_or(j == 0, j == tps - 1)
    x = x_ref[...].astype(F32)
    prow = jnp.where(seg_start, 0.0, prev_ref[...].astype(F32)[HALO - 1:HALO, :])
    nrow = jnp.where(seg_end, 0.0, next_ref[...].astype(F32)[0:1, :])
    rid = lax.broadcasted_iota(jnp.int32, x.shape, 0)
    xp = jnp.where(rid == 0, prow, pltpu.roll(x, 1, 0))
    xn = jnp.where(rid == tm - 1, nrow, pltpu.roll(x, tm - 1, 0))
    xc = wc_ref[0:1, :] * xp + wc_ref[1:2, :] * x + wc_ref[2:3, :] * xn + bc_ref[...]
    xc = xc * _sigmoid(xc)
    xc_ref[...] = xc.astype(xc_ref.dtype)
    xcb = xc.astype(MXU_DT)
    xb = x_ref[...]
    qs, ks, vs = [], [], []
    for h in range(H_C):
        cols = slice(h * DH_C, (h + 1) * DH_C)
        qs.append(jnp.dot(xcb[:, cols], wqkv_ref[0, h], preferred_element_type=F32))
        ks.append(jnp.dot(xcb[:, cols], wqkv_ref[1, h], preferred_element_type=F32))
        vs.append(jnp.dot(xb[:, cols], wqkv_ref[2, h], preferred_element_type=F32))
    q = jnp.concatenate(qs, axis=1)
    k = jnp.concatenate(ks, axis=1)
    v = jnp.concatenate(vs, axis=1)
    qkv = jnp.concatenate([q, k, v], axis=1).astype(MXU_DT)
    gates = jnp.dot(qkv, wif_ref[...], preferred_element_type=F32) + bif_ref[...]
    g_ref[...] = gates.T[0:g_ref.shape[0], :]
    q_ref[...] = q.T.astype(q_ref.dtype)
    k_ref[...] = (k * (DH_C ** -0.5)).astype(k_ref.dtype)
    v_ref[...] = v.T.astype(v_ref.dtype)


def _mlstm_features(p, w_conv, b_conv, w_qkv, w_if_pad, b_if_pad, tps):
    r = p.shape[0]
    tm = ROW_TILE
    hpt = tm // HALO
    xm_blk = P_XM // W_C
    last_halo = r // HALO - 1
    kern = functools.partial(_mfeat_kernel, tps=tps)
    act = jax.ShapeDtypeStruct((r, W_C), ACT_DT)
    act_t = jax.ShapeDtypeStruct((W_C, r), ACT_DT)
    row_spec = pl.BlockSpec((tm, W_C), lambda i: (i, 0))
    col_spec = pl.BlockSpec((W_C, tm), lambda i: (0, i))
    return pl.pallas_call(
        kern,
        grid=(r // tm,),
        in_specs=[
            pl.BlockSpec((tm, W_C), lambda i: (i, xm_blk)),
            pl.BlockSpec((HALO, W_C), lambda i: (jnp.maximum(i * hpt - 1, 0), xm_blk)),
            pl.BlockSpec((HALO, W_C), lambda i: (jnp.minimum((i + 1) * hpt, last_halo), xm_blk)),
            pl.BlockSpec((CONV_K, W_C), lambda i: (0, 0)),
            pl.BlockSpec((1, W_C), lambda i: (0, 0)),
            pl.BlockSpec((3, H_C, DH_C, DH_C), lambda i: (0, 0, 0, 0)),
            pl.BlockSpec((3 * W_C, LANE), lambda i: (0, 0)),
            pl.BlockSpec((1, LANE), lambda i: (0, 0)),
        ],
        out_specs=[row_spec, col_spec, row_spec, col_spec,
                   pl.BlockSpec((2 * GATE_ROWS, tm), lambda i: (0, i))],
        out_shape=[act, act_t, act, act_t, jax.ShapeDtypeStruct((2 * GATE_ROWS, r), F32)],
        compiler_params=_cparams("parallel"),
        name="mlstm_features",
    )(p, p, p, w_conv, b_conv, w_qkv, w_if_pad, b_if_pad)


GATE_ROWS = 2 * H_C
GATE_PACK = ("r", "cmx", "bcum")


def _gate_prep_kernel(g_ref, pack_ref, rt_ref):
    seg = MLSTM_CHUNK
    gt_fwd = g_ref[0:GATE_ROWS, :]
    gt_bwd = g_ref[GATE_ROWS:2 * GATE_ROWS, :]
    n = gt_fwd.shape[1]
    lane = lax.broadcasted_iota(jnp.int32, gt_fwd.shape, 1)
    in_seg = lane & (seg - 1)

    def scan_both(xf, xb, op):
        sh = 1
        while sh < seg:
            xf = jnp.where(in_seg >= sh, op(xf, pltpu.roll(xf, sh, 1)), xf)
            xb = jnp.where(in_seg < seg - sh, op(xb, pltpu.roll(xb, n - sh, 1)), xb)
            sh *= 2
        return xf, xb

    bcum_f, bcum_b = scan_both(jax.nn.log_sigmoid(pltpu.roll(gt_fwd, H_C, 0)),
                               jax.nn.log_sigmoid(pltpu.roll(gt_bwd, H_C, 0)), jnp.add)
    r_f, r_b = gt_fwd - bcum_f, gt_bwd - bcum_b
    cmx_f, cmx_b = scan_both(r_f, r_b, jnp.maximum)
    pack_ref[...] = jnp.concatenate([r_f, cmx_f, bcum_f, r_b, cmx_b, bcum_b], axis=0)
    rt_ref[...] = jnp.concatenate([r_f, r_b, jnp.zeros((LANE - 2 * GATE_ROWS, n), F32)], axis=0).T


def _mlstm_gate_prep(gates_t, n_batch, seq):
    rows = 2 * len(GATE_PACK) * GATE_ROWS
    r = gates_t.shape[1]
    return pl.pallas_call(
        _gate_prep_kernel,
        grid=(n_batch,),
        in_specs=[pl.BlockSpec((2 * GATE_ROWS, seq), lambda b: (0, b))],
        out_specs=[pl.BlockSpec((rows, seq), lambda b: (0, b)), pl.BlockSpec((seq, LANE), lambda b: (b, 0))],
        out_shape=[jax.ShapeDtypeStruct((rows, r), F32), jax.ShapeDtypeStruct((r, LANE), F32)],
        compiler_params=_cparams("parallel"),
        name="mlstm_gate_prep",
    )(gates_t)


def _mlstm_gates(pack, m_ref, reverse):
    ln = pack.shape[1]
    part = {name: pack[i * GATE_ROWS:(i + 1) * GATE_ROWS] for i, name in enumerate(GATE_PACK)}
    end = slice(0, 1) if reverse else slice(ln - 1, ln)
    g = part["bcum"][:, end]
    r_max = part["cmx"][:, end]
    m_loc = g + r_max
    w_end = jnp.exp(part["r"] - r_max)
    m_old = m_ref[:, 0:1]
    m_new = jnp.maximum(g + m_old, m_loc)
    a_old = jnp.exp(g + m_old - m_new)
    a_new = jnp.exp(m_loc - m_new)
    mx = jnp.maximum(m_old, part["cmx"])
    w_inter = jnp.exp(m_old - mx)
    e_neg = jnp.exp(-(part["bcum"] + mx))
    s_id = lax.broadcasted_iota(jnp.int32, (ln, ln), 0)
    t_id = lax.broadcasted_iota(jnp.int32, (ln, ln), 1)
    keep = (s_id >= t_id) if reverse else (s_id <= t_id)
    return dict(mx=mx, w_inter=w_inter, e_neg=e_neg, w_end=w_end, a_old=a_old, a_new=a_new,
                m_new=m_new, keep=keep)


def _mscan_chunk(gf_ref, rf_ref, qf_ref, kf_ref, vf_ref, gb_ref, rb_ref, qb_ref, kb_ref, vb_ref,
                 hf_ref, hb_ref, cf_ref, mf_ref, cb_ref, mb_ref):
    ln = qf_ref.shape[1]
    pr = len(GATE_PACK) * GATE_ROWS
    dirs = ((_mlstm_gates(gf_ref[0:pr, :], mf_ref, False), qf_ref, kf_ref, vf_ref, cf_ref, hf_ref, rf_ref),
            (_mlstm_gates(gb_ref[pr:2 * pr, :], mb_ref, True), qb_ref, kb_ref, vb_ref, cb_ref, hb_ref, rb_ref))
    pairs = [(d, h) for d in range(2) for h in range(H_C)]
    ones_row = jnp.where(lax.broadcasted_iota(jnp.int32, (DH_C, ln), 0) == 0, 1.0, 0.0).astype(MXU_DT)
    head = lambda h: slice(h * DH_C, (h + 1) * DH_C)
    row = lambda h: slice(h, h + 1)

    ks = [dirs[d][2][:, head(h)] for d, h in pairs]
    qts = [dirs[d][1][head(h), :] for d, h in pairs]
    vexts = [jnp.concatenate([dirs[d][3][head(h), :], ones_row], axis=0) for d, h in pairs]
    cexts = [dirs[d][4][h] for d, h in pairs]
    qks = [jnp.dot(k, q_t, preferred_element_type=F32) for k, q_t in zip(ks, qts)]
    rhss = []
    for i, (d, h) in enumerate(pairs):
        g = dirs[d][0]
        r_col = dirs[d][6][:, d * GATE_ROWS + h:d * GATE_ROWS + h + 1]
        d_t = jnp.exp(jnp.where(g["keep"], r_col - g["mx"][row(h), :], NEG_BIG))
        wq = (qts[i].astype(F32) * g["w_inter"][row(h), :]).astype(MXU_DT)
        rhss.append(jnp.concatenate([(qks[i] * d_t).astype(MXU_DT), wq], axis=0))
    ress = [jnp.dot(jnp.concatenate([vexts[i], cexts[i].astype(MXU_DT)], axis=1), rhss[i],
                    preferred_element_type=F32) for i in range(len(pairs))]
    for i, (d, h) in enumerate(pairs):
        g = dirs[d][0]
        den = ress[i][DH_C:DH_C + 1, :]
        hh = ress[i][:DH_C, :] / jnp.maximum(jnp.abs(den), g["e_neg"][row(h), :])
        dirs[d][5][head(h), :] = hh.astype(dirs[d][5].dtype)
    clocs = []
    for i, (d, h) in enumerate(pairs):
        vw_t = (vexts[i].astype(F32) * dirs[d][0]["w_end"][row(h), :]).astype(MXU_DT)
        clocs.append(jnp.dot(vw_t, ks[i], preferred_element_type=F32))
    for i, (d, h) in enumerate(pairs):
        g = dirs[d][0]
        dirs[d][4][h] = g["a_old"][row(h), :] * cexts[i] + g["a_new"][row(h), :] * clocs[i]
    mf_ref[...] = jnp.broadcast_to(dirs[0][0]["m_new"], mf_ref.shape)
    mb_ref[...] = jnp.broadcast_to(dirs[1][0]["m_new"], mb_ref.shape)


def _mscan_kernel(g_ref, r_ref, q_ref, k_ref, v_ref, hf_ref, hb_ref, cf_ref, mf_ref, cb_ref, mb_ref, *, ncc):
    ln = MLSTM_CHUNK
    nck = q_ref.shape[1] // ln
    cf_ref[...] = jnp.zeros_like(cf_ref)
    cb_ref[...] = jnp.zeros_like(cb_ref)
    mf_ref[...] = jnp.zeros_like(mf_ref)
    mb_ref[...] = jnp.zeros_like(mb_ref)

    def chunk_views(c):
        at = pl.ds(pl.multiple_of(c * ln, ln), ln)
        return ((g_ref.at[:, at], r_ref.at[at, :], q_ref.at[:, at], k_ref.at[at, :], v_ref.at[:, at]),
                (hf_ref.at[:, at], hb_ref.at[:, at]))

    def trip(s, carry):
        ins_f, (hf, _) = chunk_views(s)
        ins_b, (_, hb) = chunk_views(jnp.where(s < ncc, ncc - 1 - s, nck - 1 + ncc - s))
        _mscan_chunk(*ins_f, *ins_b, hf, hb, cf_ref, mf_ref, cb_ref, mb_ref)
        return carry

    lax.fori_loop(0, nck, trip, 0)


def _mlstm_scan(q_t, k, v_t, gate_pack, r_t, n_batch, seq, ctx_rows):
    rows = lambda w: pl.BlockSpec((seq, w), lambda b: (b, 0))
    cols = lambda h: pl.BlockSpec((h, seq), lambda b: (0, b))
    out = jax.ShapeDtypeStruct(q_t.shape, ACT_DT)
    return pl.pallas_call(
        functools.partial(_mscan_kernel, ncc=ctx_rows // MLSTM_CHUNK),
        grid=(n_batch,),
        in_specs=[cols(gate_pack.shape[0]), rows(LANE), cols(W_C), rows(W_C), cols(W_C)],
        out_specs=[cols(W_C), cols(W_C)],
        out_shape=[out, out],
        scratch_shapes=[
            pltpu.VMEM((H_C, 2 * DH_C, DH_C), F32), pltpu.VMEM((8, LANE), F32),
            pltpu.VMEM((H_C, 2 * DH_C, DH_C), F32), pltpu.VMEM((8, LANE), F32),
        ],
        compiler_params=_cparams("parallel"),
        name="mlstm_scan",
    )(gate_pack, r_t, q_t, k, v_t)


def _route(logits, carry):
    lane = lax.broadcasted_iota(jnp.int32, logits.shape, 1).astype(F32)
    big = float(4 * LANE)
    gl = jnp.where(lane < N_GROUPS, logits, NEG_BIG)
    gmax = jnp.max(gl, axis=-1, keepdims=True)
    g_star = jnp.min(jnp.where(gl == gmax, lane, big), axis=-1, keepdims=True)
    p_g = 1.0 / jnp.sum(jnp.exp(gl - gmax), axis=-1, keepdims=True)
    e_lo = g_star * EXP_PER_GROUP + N_GROUPS
    in_group = jnp.logical_and(lane >= e_lo, lane < e_lo + EXP_PER_GROUP)
    el = jnp.where(in_group, logits, NEG_BIG)
    v1 = jnp.max(el, axis=-1, keepdims=True)
    i1 = jnp.min(jnp.where(el == v1, lane, big), axis=-1, keepdims=True)
    el2 = jnp.where(lane == i1, NEG_BIG, el)
    v2 = jnp.max(el2, axis=-1, keepdims=True)
    i2 = jnp.min(jnp.where(el2 == v2, lane, big), axis=-1, keepdims=True)
    e21 = jnp.exp(v2 - v1)
    w1 = p_g / (1.0 + e21)
    w2 = p_g * e21 / (1.0 + e21)
    e1 = i1 - N_GROUPS
    e2 = i2 - N_GROUPS
    oh1 = lane == e1
    oh2 = lane == e2
    both = jnp.where(jnp.logical_or(oh1, oh2), 1.0, 0.0)
    tm = logits.shape[0]
    lower = (lax.broadcasted_iota(jnp.int32, (tm, tm), 1) < lax.broadcasted_iota(jnp.int32, (tm, tm), 0))
    before = jnp.dot(jnp.where(lower, 1.0, 0.0).astype(MXU_DT), both.astype(MXU_DT),
                     preferred_element_type=F32) + carry
    rank1 = jnp.sum(jnp.where(oh1, before, 0.0), axis=-1, keepdims=True)
    rank2 = jnp.sum(jnp.where(oh2, before, 0.0), axis=-1, keepdims=True)
    out = jnp.where(lane == 0, e1, 0.0)
    out = jnp.where(lane == 1, e2, out)
    out = jnp.where(lane == 2, w1, out)
    out = jnp.where(lane == 3, w2, out)
    out = jnp.where(lane == 4, rank1, out)
    out = jnp.where(lane == 5, rank2, out)
    return out, carry + jnp.sum(both, axis=0, keepdims=True)


def _merge_kernel(x_ref, gate_ref, z_ref, ya_ref, yb_ref, hf_ref, hb_ref, xc_ref, gm_ref, sk_ref,
                  wbr_ref, wo_ref, mod_ref, g2_ref, wr_ref, br_ref, xo_ref, rt_ref, cnt_ref, carry_ref):
    @pl.when(pl.program_id(0) == 0)
    def _():
        carry_ref[...] = jnp.zeros_like(carry_ref)

    hs_t = hf_ref[...].astype(F32) + hb_ref[...].astype(F32)
    hn_t = []
    for h in range(H_C):
        part = hs_t[h * DH_C:(h + 1) * DH_C, :]
        hn_t.append(part * lax.rsqrt(jnp.mean(part * part, axis=0, keepdims=True) + EPS))
    hn = jnp.concatenate(hn_t, axis=0).T * gm_ref[...]
    yc = (hn + sk_ref[...] * xc_ref[...].astype(F32)) * _sigmoid(z_ref[...].astype(F32))
    ys = (ya_ref[...], yb_ref[...], yc.astype(MXU_DT))
    merged = None
    for i in range(3):
        gate = _sigmoid(gate_ref[:, i * D_MODEL:(i + 1) * D_MODEL].astype(MXU_DT))
        term = gate * jnp.dot(ys[i], wbr_ref[i], preferred_element_type=F32).astype(MXU_DT)
        merged = term if merged is None else merged + term
    y = jnp.dot(merged, wo_ref[...], preferred_element_type=F32)
    xn = x_ref[...] + mod_ref[2:3, :] * y
    xo_ref[...] = xn
    h2 = _rms(xn) * g2_ref[...] * (1.0 + mod_ref[4:5, :]) + mod_ref[3:4, :]
    h_hi, h_lo = _split_bf16(h2)
    logits = (jnp.dot(h_hi, wr_ref[0], preferred_element_type=F32)
              + jnp.dot(h_lo, wr_ref[0], preferred_element_type=F32)
              + jnp.dot(h_hi, wr_ref[1], preferred_element_type=F32)) + br_ref[...]
    route, carry = _route(logits, carry_ref[0:1, :])
    rt_ref[...] = route
    carry_ref[0:1, :] = carry
    cnt_ref[...] = jnp.broadcast_to(carry, cnt_ref.shape)


def _merge(xall, p, ya, yb, hf, hb, xconv, g_m, skip, w_br, w_o, mod, g2, w_route, b_route, tiles):
    d = xall.shape[1]
    tm = ROW_TILE
    r = tiles.count * tm

    def mod_idx(i):
        return (tiles.mod_row(i), 0, 0)

    row = lambda w: pl.BlockSpec((tm, w), lambda i: (i, 0))
    frow = lambda w: pl.BlockSpec((tm, w), lambda i: (tiles.full(i), 0))
    fcol = lambda w: pl.BlockSpec((w, tm), lambda i: (0, tiles.full(i)))
    full = lambda shape: pl.BlockSpec(shape, lambda i: (0,) * len(shape))
    return pl.pallas_call(
        _merge_kernel,
        grid=(tiles.count,),
        in_specs=[
            frow(d),
            pl.BlockSpec((tm, GATE_W), lambda i: (tiles.full(i), P_GATE // GATE_W)),
            pl.BlockSpec((tm, W_C), lambda i: (tiles.full(i), P_Z // W_C)),
            row(W_BR), row(W_BR), fcol(W_C), fcol(W_C), frow(W_C),
            full((1, W_C)), full((1, W_C)),
            full((3, W_BR, d)), full((d, d)),
            pl.BlockSpec((None, N_MOD, d), mod_idx),
            full((1, d)), full((2, d, LANE)), full((1, LANE)),
        ],
        out_specs=[row(d), row(LANE), full((8, LANE))],
        out_shape=[jax.ShapeDtypeStruct((r, d), F32), jax.ShapeDtypeStruct((r, LANE), F32),
                   jax.ShapeDtypeStruct((8, LANE), F32)],
        scratch_shapes=[pltpu.VMEM((8, LANE), F32)],
        compiler_params=_cparams("arbitrary"),
        name="merge_route",
    )(xall, p, p, ya, yb, hf, hb, xconv, g_m, skip, w_br, w_o, mod, g2, w_route, b_route)


def _row_copy(src_hbm, src_row, dst_ref, dst_row, sem):
    return pltpu.make_async_copy(src_hbm.at[pl.ds(src_row, 1), :], dst_ref.at[pl.ds(dst_row, 1), :], sem)


def _dispatch_kernel(fill_ref, nlive_ref, pos_ref, x_ref, g2_ref, mod_ref, xs_hbm, hbuf, zbuf, zsem, sem):
    i = pl.program_id(0)
    n = pl.num_programs(0)
    tm = x_ref.shape[0]
    slot = i % 2
    bm = zbuf.shape[0]
    n_blocks = xs_hbm.shape[0] // bm

    @pl.when(i == 0)
    def _():
        zbuf[...] = jnp.zeros_like(zbuf)

        def fill_block(b):
            return pltpu.make_async_copy(zbuf, xs_hbm.at[pl.ds(b * bm, bm), :], zsem)

        def fill_last(k, carry):
            fill_block(fill_ref[k]).start()
            return carry

        def fill_unused(b, carry):
            fill_block(b).start()
            return carry

        def wait_fill(k, carry):
            fill_block(0).wait()
            return carry

        n_live = nlive_ref[0]
        n_used = nlive_ref[1]
        lax.fori_loop(0, n_used, fill_last, 0)
        lax.fori_loop(n_live, n_blocks, fill_unused, 0)
        lax.fori_loop(0, n_used + n_blocks - n_live, wait_fill, 0)

    hbuf[slot] = _rms(x_ref[...]) * g2_ref[...] * (1.0 + mod_ref[4:5, :]) + mod_ref[3:4, :]

    for r in range(tm):
        for j in range(TOP_K):
            pltpu.make_async_copy(hbuf.at[slot, pl.ds(r, 1), :], xs_hbm.at[pl.ds(pos_ref[0, j * tm + r], 1), :],
                                  sem.at[slot]).start(priority=j % N_DMA_PRIORITIES)

    def wait_tile(s):
        pltpu.make_async_copy(hbuf.at[s], xs_hbm.at[pl.ds(0, tm), :], sem.at[s]).wait()
        pltpu.make_async_copy(hbuf.at[s], xs_hbm.at[pl.ds(0, tm), :], sem.at[s]).wait()

    @pl.when(i > 0)
    def _():
        wait_tile(1 - slot)

    @pl.when(i == n - 1)
    def _():
        wait_tile(slot)


def _moe_dispatch(xall, g2, mod, fill_blocks, n_live, pos_tiles, n_slots, tiles):
    r, d = xall.shape
    tm = ROW_TILE
    assert TOP_K == 2 and r == tiles.count * tm

    def mod_idx(i, fb, nl):
        return (tiles.mod_row(i), 0, 0)

    grid_spec = pltpu.PrefetchScalarGridSpec(
        num_scalar_prefetch=2,
        grid=(r // tm,),
        in_specs=[
            pl.BlockSpec((None, 1, TOP_K * tm), lambda i, fb, nl: (i, 0, 0), memory_space=pltpu.SMEM),
            pl.BlockSpec((tm, d), lambda i, fb, nl: (i, 0)),
            pl.BlockSpec((1, d), lambda i, fb, nl: (0, 0)),
            pl.BlockSpec((None, N_MOD, d), mod_idx),
        ],
        out_specs=pl.BlockSpec(memory_space=pl.ANY),
        scratch_shapes=[pltpu.VMEM((2, tm, d), F32), pltpu.VMEM((MOE_BLOCK, d), F32),
                        pltpu.SemaphoreType.DMA(()), pltpu.SemaphoreType.DMA((2,))],
    )
    return pl.pallas_call(
        _dispatch_kernel,
        grid_spec=grid_spec,
        out_shape=jax.ShapeDtypeStruct((n_slots, d), F32),
        compiler_params=_cparams("arbitrary"),
        name="moe_dispatch",
    )(fill_blocks, n_live, pos_tiles, xall, g2, mod)


def _moe_kernel(be_ref, nv_ref, ne_ref, x_ref, wg_hbm, wu_hbm, wd_hbm, y_ref,
                wg_mx, wu_mx, wd_mx, wg_in, wu_in, wd_in, par_ref, sem, *, layer):
    i = pl.program_id(0)
    n_valid = nv_ref[i]

    def fetch(e, slot):
        return [pltpu.make_async_copy(src.at[layer, e], dst.at[slot], sem.at[slot])
                for src, dst in ((wg_hbm, wg_in), (wu_hbm, wu_in), (wd_hbm, wd_in))]

    @pl.when(i == 0)
    def _():
        par_ref[0] = 0
        for cp in fetch(be_ref[0], 0):
            cp.start()

    run_start = jnp.logical_or(i == 0, be_ref[i] != be_ref[jnp.maximum(i - 1, 0)])

    @pl.when(jnp.logical_and(n_valid > 0, run_start))
    def _():
        slot = par_ref[0]
        for cp in fetch(be_ref[i], slot):
            cp.wait()
        wg_mx[...] = wg_in[slot].astype(wg_mx.dtype)
        wu_mx[...] = wu_in[slot].astype(wu_mx.dtype)
        wd_mx[...] = wd_in[slot].astype(wd_mx.dtype)
        nxt = ne_ref[i]

        @pl.when(nxt >= 0)
        def _():
            for cp in fetch(nxt, 1 - slot):
                cp.start(priority=N_DMA_PRIORITIES - 1)

        par_ref[0] = 1 - slot

    @pl.when(n_valid > 0)
    def _():
        x = x_ref[...].astype(MXU_DT)
        a = jnp.dot(x, wg_mx[...], preferred_element_type=F32)
        u = jnp.dot(x, wu_mx[...], preferred_element_type=F32)
        act = (a * _sigmoid(a) * u).astype(MXU_DT)
        y_ref[...] = jnp.dot(act, wd_mx[...], preferred_element_type=F32)

    @pl.when(n_valid <= 0)
    def _():
        y_ref[...] = jnp.zeros_like(y_ref)


def _moe_experts(xs, block_e, n_valid, next_e, w_gate, w_up, w_down, layer):
    n_slots, d = xs.shape
    bm = MOE_BLOCK
    wdt = w_gate.dtype
    grid_spec = pltpu.PrefetchScalarGridSpec(
        num_scalar_prefetch=3,
        grid=(n_slots // bm,),
        in_specs=[
            pl.BlockSpec((bm, d), lambda i, be, nv, ne: (i, 0)),
            pl.BlockSpec(memory_space=pl.ANY), pl.BlockSpec(memory_space=pl.ANY), pl.BlockSpec(memory_space=pl.ANY),
        ],
        out_specs=pl.BlockSpec((bm, d), lambda i, be, nv, ne: (i, 0)),
        scratch_shapes=[pltpu.VMEM((d, D_EXPERT), MXU_DT), pltpu.VMEM((d, D_EXPERT), MXU_DT),
                        pltpu.VMEM((D_EXPERT, d), MXU_DT),
                        pltpu.VMEM((2, d, D_EXPERT), wdt), pltpu.VMEM((2, d, D_EXPERT), wdt),
                        pltpu.VMEM((2, D_EXPERT, d), wdt),
                        pltpu.SMEM((1,), jnp.int32), pltpu.SemaphoreType.DMA((2,))],
    )
    return pl.pallas_call(
        functools.partial(_moe_kernel, layer=layer),
        grid_spec=grid_spec,
        out_shape=jax.ShapeDtypeStruct((n_slots, d), F32),
        compiler_params=_cparams("arbitrary"),
        name="moe_experts",
    )(block_e, n_valid, next_e, xs, w_gate, w_up, w_down)


def _combine_kernel(pos_ref, pos_next_ref, x_ref, rt_ref, mod_ref, gf_ref, yb_hbm, xo_ref, ybuf, sem, *, final):
    i = pl.program_id(0)
    n = pl.num_programs(0)
    tm = x_ref.shape[0]

    def start_gather(ids_ref, slot):
        def body(c, carry):
            for u in range(DMA_UNROLL):
                r = c * DMA_UNROLL + u
                _row_copy(yb_hbm, ids_ref[0, r], ybuf.at[slot], r, sem.at[slot]).start()
            return carry
        lax.fori_loop(0, TOP_K * tm // DMA_UNROLL, body, 0)

    def wait_gather(slot):
        pltpu.make_async_copy(yb_hbm.at[pl.ds(0, TOP_K * tm), :], ybuf.at[slot], sem.at[slot]).wait()

    @pl.when(i == 0)
    def _():
        start_gather(pos_ref, 0)

    slot = i % 2
    wait_gather(slot)
    per = TOP_K * tm // COMBINE_GROUPS
    rows_per = tm // COMBINE_GROUPS
    for c in range(COMBINE_GROUPS):
        for r in range(c * per, (c + 1) * per):
            _row_copy(yb_hbm, pos_next_ref[0, r], ybuf.at[1 - slot], r,
                      sem.at[1 - slot]).start(priority=r % N_DMA_PRIORITIES)
        rows = slice(c * rows_per, (c + 1) * rows_per)
        rt = rt_ref[rows, :]
        f = (rt[:, 2:3] * ybuf[slot, rows, :]
             + rt[:, 3:4] * ybuf[slot, tm + c * rows_per:tm + (c + 1) * rows_per, :])
        xn = x_ref[rows, :] + mod_ref[5:6, :] * f
        xo_ref[rows, :] = _rms(xn) * gf_ref[...] if final else xn

    @pl.when(i == n - 1)
    def _():
        wait_gather(1 - slot)


def _combine(xall, route, mod, g_final, yb, pos, tiles, final):
    r, d = xall.shape
    tm = ROW_TILE
    nt = tiles.count
    last = nt - 1
    assert r == nt * tm

    def mod_idx(i):
        return (tiles.mod_row(i), 0, 0)

    row = lambda w: pl.BlockSpec((tm, w), lambda i: (i, 0))
    return pl.pallas_call(
        functools.partial(_combine_kernel, final=final),
        grid=(nt,),
        in_specs=[
            pl.BlockSpec((None, 1, TOP_K * tm), lambda i: (i, 0, 0), memory_space=pltpu.SMEM),
            pl.BlockSpec((None, 1, TOP_K * tm), lambda i: (jnp.minimum(i + 1, last), 0, 0),
                         memory_space=pltpu.SMEM),
            row(d), row(LANE),
            pl.BlockSpec((None, N_MOD, d), mod_idx),
            pl.BlockSpec((1, d), lambda i: (0, 0)),
            pl.BlockSpec(memory_space=pl.ANY),
        ],
        out_specs=row(d),
        out_shape=jax.ShapeDtypeStruct((r, d), F32),
        scratch_shapes=[pltpu.VMEM((2, TOP_K * tm, d), F32), pltpu.SemaphoreType.DMA((2,))],
        compiler_params=_cparams("arbitrary"),
        name="moe_combine",
    )(pos, pos, xall, route, mod, g_final, yb)


def _dispatch_plan(route, counts_f, n_tiles):
    n = route.shape[0]
    bm = MOE_BLOCK
    n_blocks = -(-(n * TOP_K + N_EXPERTS * (bm - 1)) // bm)
    counts = counts_f[:N_EXPERTS].astype(jnp.int32)
    padded = (counts + bm - 1) // bm * bm
    pend = jnp.cumsum(padded)
    seg_off = pend - padded
    eid = route[:, 0:TOP_K].astype(jnp.int32)
    rank = route[:, 4:4 + TOP_K].astype(jnp.int32)
    experts = jnp.arange(N_EXPERTS, dtype=jnp.int32)
    pos = jnp.sum(jnp.where(eid[..., None] == experts, seg_off, 0), axis=-1) + rank
    blk0 = jnp.arange(n_blocks, dtype=jnp.int32) * bm
    block_e = jnp.minimum(jnp.sum((blk0[:, None] >= pend[None, :]).astype(jnp.int32), axis=1), N_EXPERTS - 1)
    seg_end = seg_off + counts
    n_valid = jnp.clip(seg_end[block_e] - blk0, 0, bm).astype(jnp.int32)
    pos_tiles = pos.reshape(n_tiles, n // n_tiles, TOP_K).transpose(0, 2, 1).reshape(n_tiles, 1, -1)
    used = counts > 0
    last_blocks = jnp.sort(jnp.where(used, pend // bm - 1, n_blocks)).astype(jnp.int32)
    n_live = jnp.stack([pend[-1] // bm, jnp.sum(used)]).astype(jnp.int32)
    later_used = jnp.logical_and(experts[None, :] > experts[:, None], used[None, :])
    next_used = jnp.min(jnp.where(later_used, experts[None, :], N_EXPERTS), axis=1)
    next_e = jnp.where(next_used < N_EXPERTS, next_used, -1)[block_e].astype(jnp.int32)
    return block_e.astype(jnp.int32), n_valid, next_e, last_blocks, n_live, pos_tiles, n_blocks * bm


def _rope_tables(seq, ctx_rows):
    t = jnp.arange(seq - ctx_rows)
    inv = ROPE_BASE ** (-jnp.arange(0, ROT_AX, 2, dtype=F32) / ROT_AX)
    ang_r = (t // GRID_W).astype(F32)[:, None] * inv
    ang_c = (t % GRID_W).astype(F32)[:, None] * inv
    cos64 = jnp.concatenate([jnp.cos(ang_r)] * 2 + [jnp.cos(ang_c)] * 2, axis=1)
    sin64 = jnp.concatenate([-jnp.sin(ang_r), jnp.sin(ang_r), -jnp.sin(ang_c), jnp.sin(ang_c)], axis=1)
    cos = jnp.concatenate([jnp.ones((ctx_rows, DK), F32), cos64], axis=0)
    sin = jnp.concatenate([jnp.zeros((ctx_rows, DK), F32), sin64], axis=0)
    return jnp.tile(cos, (1, LANE // DK)), jnp.tile(sin, (1, LANE // DK))


def kernel(x, c, ctx, c_ctx, w_mod, b_mod, g_norm1, g_norm2, w_in, diff_lambda, g_diff_subln, w_gmlp_s, b_gmlp_s, g_gmlp_v, w_conv_m, b_conv_m, w_qkv_m, w_if_m, b_if_m, g_mlstm_norm, skip_m, w_branch, w_out, w_route_g, b_route_g, w_route_e, b_route_e, w_e_gate, w_e_up, w_e_down, g_final):
    n_batch, t_lat, d = x.shape
    ctx_rows = ctx.shape[1]
    depth = w_in.shape[0]
    seq = ctx_rows + t_lat
    assert d == D_MODEL and ctx_rows == ROW_TILE and t_lat % ROW_TILE == 0 and t_lat % GRID_W == 0
    tps = seq // ROW_TILE
    all_tiles = _Tiles(n_batch, tps, False)

    xall = (ctx.reshape(n_batch * ctx_rows, d), x.reshape(n_batch * t_lat, d))
    mb = -(-(n_batch + 1) // 8) * 8
    c_all = jnp.zeros((mb, d), F32).at[:n_batch].set(c).at[n_batch].set(c_ctx)
    mod_all = _modulation(c_all, w_mod, b_mod)[:, :n_batch + 1].reshape(depth, n_batch + 1, N_MOD, d)
    cos, sin_signed = _rope_tables(seq, ctx_rows)
    w_in_b = w_in.astype(MXU_DT)

    for l in range(depth):
        lam_init = 0.8 - 0.6 * math.exp(-0.3 * l)
        mod = mod_all[l]
        last = l == depth - 1
        tiles = _Tiles(n_batch, tps, last)
        if l == 0:
            p, k_t, xall = _inproj(xall, g_norm1[l][None], mod, cos, sin_signed, w_in_b, l, all_tiles)
        else:
            p, k_t = _inproj(xall, g_norm1[l][None], mod, cos, sin_signed, w_in_b, l, all_tiles)

        ya = _attention(p, k_t, diff_lambda[l], g_diff_subln[l][None], lam_init, tiles, ctx_rows)

        b_full = jnp.repeat(b_gmlp_s[l].T, W_B // G_B, axis=1)
        yb = _gmlp(p, g_gmlp_v[l][None], w_gmlp_s[l].astype(MXU_DT), b_full, tiles)

        w_if_pad = jnp.zeros((3 * W_C, LANE), F32).at[:, :4 * H_C].set(w_if_m[l]).astype(MXU_DT)
        b_if_pad = jnp.zeros((1, LANE), F32).at[0, :4 * H_C].set(b_if_m[l])
        xconv, q_m, k_m, v_m, gates_t = _mlstm_features(
            p, w_conv_m[l], b_conv_m[l][None], w_qkv_m[l].astype(MXU_DT), w_if_pad, b_if_pad, tps)
        gate_pack, r_t = _mlstm_gate_prep(gates_t, n_batch, seq)
        hf, hb = _mlstm_scan(q_m, k_m, v_m, gate_pack, r_t, n_batch, seq, ctx_rows)

        w_route = (jnp.zeros((d, LANE), F32).at[:, :N_GROUPS].set(w_route_g[l])
                   .at[:, N_GROUPS:N_GROUPS + N_EXPERTS].set(w_route_e[l]))
        w_route = jnp.stack(_split_bf16(w_route))
        b_route = (jnp.zeros((1, LANE), F32).at[0, :N_GROUPS].set(b_route_g[l])
                   .at[0, N_GROUPS:N_GROUPS + N_EXPERTS].set(b_route_e[l]))
        xall, route, counts = _merge(xall, p, ya, yb, hf, hb, xconv, g_mlstm_norm[l][None], skip_m[l][None],
                                         w_branch[l].astype(MXU_DT), w_out[l].astype(MXU_DT), mod,
                                         g_norm2[l][None], w_route, b_route, tiles)

        block_e, n_valid, next_e, fill_blocks, n_live, pos_tiles, n_slots = _dispatch_plan(
            route, counts[0], tiles.count)
        xs = _moe_dispatch(xall, g_norm2[l][None], mod, fill_blocks, n_live, pos_tiles, n_slots, tiles)
        y_sorted = _moe_experts(xs, block_e, n_valid, next_e, w_e_gate, w_e_up, w_e_down, l)
        xall = _combine(xall, route, mod, g_final[None], y_sorted, pos_tiles, tiles, last)

    return xall.reshape(n_batch, t_lat, d)
```

```python
import functools
import math
from typing import NamedTuple

import jax
import jax.numpy as jnp
from jax import lax
from jax.experimental import pallas as pl
from jax.experimental.pallas import tpu as pltpu

F32 = jnp.float32
MXU_DT = jnp.bfloat16
ACT_DT = jnp.bfloat16
HIGHEST = lax.Precision.HIGHEST

D_MODEL = 1024
N_MOD = 6
EPS = 1e-6
GRID_W = 64
H_A, DK = 4, 64
DV = 2 * DK
ROT_AX = DK // 2
ROPE_BASE = 10000.0
W_B, G_B, CHUNK_B = 512, 4, 128
H_C, DH_C, CONV_K, MLSTM_CHUNK = 4, 128, 3, 128
W_C = H_C * DH_C
N_GROUPS, EXP_PER_GROUP, TOP_K, D_EXPERT = 4, 8, 2, 512
N_EXPERTS = N_GROUPS * EXP_PER_GROUP
W_BR = 512

GATE_W = 3 * D_MODEL
P_GATE = 0
P_K = GATE_W
P_V = P_K + 512
P_XM = P_V + 512
P_Q = P_XM + 512
P_UV = P_Q + 512
P_Z = P_UV + 2 * W_B
IN_COLS = P_Z + W_C
ORIG_GATE0 = IN_COLS - GATE_W

LANE = 128
ROW_TILE = 256
PROJ_CHUNK = 512
MOE_BLOCK = 512
HALO = 16
DMA_UNROLL = 8
ATTN_KV_CHUNK = 256
COMBINE_GROUPS = 16
N_DMA_PRIORITIES = 2
LOG2_E = math.log2(math.e)
VMEM_LIMIT = 56 * 1024 * 1024
NEG_BIG = -1e30


class _Tiles(NamedTuple):
    n_batch: int
    tps: int
    skip_ctx: bool

    @property
    def per_seq(self):
        return self.tps - 1 if self.skip_ctx else self.tps

    @property
    def count(self):
        return self.n_batch * self.per_seq

    def full(self, i):
        if not self.skip_ctx:
            return i
        return (i // self.per_seq) * self.tps + 1 + i % self.per_seq

    def mod_row(self, i):
        if self.skip_ctx:
            return i // self.per_seq
        return jnp.where(i % self.tps == 0, self.n_batch, i // self.tps)


def _cparams(*sem):
    return pltpu.CompilerParams(dimension_semantics=sem, vmem_limit_bytes=VMEM_LIMIT)


def _rms(x):
    return x * lax.rsqrt(jnp.mean(x * x, axis=-1, keepdims=True) + EPS)


def _sigmoid(x):
    return 0.5 * jnp.tanh(0.5 * x) + 0.5


def _split_bf16(x):
    hi = x.astype(jnp.bfloat16)
    return hi, (x - hi.astype(F32)).astype(jnp.bfloat16)


def _mod_kernel(c_ref, w_ref, b_ref, o_ref):
    c = c_ref[...]
    s = c * jax.nn.sigmoid(c)
    o_ref[...] = jnp.dot(s, w_ref[...], preferred_element_type=F32, precision=HIGHEST) + b_ref[...]


def _modulation(c_all, w_mod, b_mod):
    n_layer = w_mod.shape[0]
    mb, d = c_all.shape
    tn = 1024
    return pl.pallas_call(
        _mod_kernel,
        grid=(n_layer, N_MOD * d // tn),
        in_specs=[
            pl.BlockSpec((mb, d), lambda l, j: (0, 0)),
            pl.BlockSpec((None, d, tn), lambda l, j: (l, 0, j)),
            pl.BlockSpec((None, 1, tn), lambda l, j: (l, 0, j)),
        ],
        out_specs=pl.BlockSpec((None, mb, tn), lambda l, j: (l, 0, j)),
        out_shape=jax.ShapeDtypeStruct((n_layer, mb, N_MOD * d), F32),
        compiler_params=_cparams("parallel", "parallel"),
        name="modulation",
    )(c_all, w_mod, b_mod.reshape(n_layer, 1, N_MOD * d))


def _rope(acc, cos, sin_signed):
    w = acc.shape[1]
    half = ROT_AX // 2
    lane = lax.broadcasted_iota(jnp.int32, acc.shape, 1)
    partner = jnp.where((lane & half) == 0, pltpu.roll(acc, w - half, 1), pltpu.roll(acc, half, 1))
    reps = w // cos.shape[1]
    return acc * jnp.tile(cos, (1, reps)) + partner * jnp.tile(sin_signed, (1, reps))


def _inproj_first_kernel(ctx_ref, lat_ref, g_ref, mod_ref, cos_ref, sin_ref, w_ref, o_ref, kt_ref, xo_ref, *, tps):
    is_ctx = pl.program_id(0) % tps == 0
    xo_ref[...] = jnp.where(is_ctx, ctx_ref[...], lat_ref[...])
    _inproj_kernel(xo_ref, g_ref, mod_ref, cos_ref, sin_ref, w_ref, o_ref, kt_ref)


def _inproj_kernel(x_ref, g_ref, mod_ref, cos_ref, sin_ref, w_ref, o_ref, kt_ref):
    x = x_ref[...]
    h = (_rms(x) * g_ref[...] * (1.0 + mod_ref[1:2, :]) + mod_ref[0:1, :]).astype(MXU_DT)
    k_chunk = P_K // PROJ_CHUNK
    q_chunk = P_Q // PROJ_CHUNK
    chunks = range(IN_COLS // PROJ_CHUNK)
    cols = lambda c: slice(c * PROJ_CHUNK, (c + 1) * PROJ_CHUNK)

    def project(c):
        w0 = (ORIG_GATE0 + c * PROJ_CHUNK) % IN_COLS
        acc = jnp.dot(h, w_ref[:, w0:w0 + PROJ_CHUNK], preferred_element_type=F32)
        if c == k_chunk:
            acc = _rope(acc, cos_ref[...], sin_ref[...])
            kt_ref[...] = acc.T.astype(kt_ref.dtype)
        elif c == q_chunk:
            acc = _rope(acc, cos_ref[...], sin_ref[...]) * (DK ** -0.5 * LOG2_E)
        o_ref[:, cols(c)] = acc.astype(o_ref.dtype)

    for c in chunks:
        project(c)


def _inproj(xall, g1, mod, cos, sin_signed, w_in, layer, tiles):
    tm = ROW_TILE
    tps = tiles.tps
    r = tiles.count * tm
    assert ORIG_GATE0 % PROJ_CHUNK == 0 and not tiles.skip_ctx
    first = isinstance(xall, tuple)
    d = xall[0].shape[1] if first else xall.shape[1]

    def mod_idx(i):
        return (tiles.mod_row(i), 0, 0)

    row = pl.BlockSpec((tm, d), lambda i: (i, 0))
    if first:
        x_specs = [pl.BlockSpec((tm, d), lambda i: (i // tps, 0)),
                   pl.BlockSpec((tm, d), lambda i: ((i // tps) * (tps - 1) + jnp.maximum(i % tps - 1, 0), 0))]
        x_args = list(xall)
        kern = functools.partial(_inproj_first_kernel, tps=tps)
    else:
        x_specs, x_args, kern = [row], [xall], _inproj_kernel
    return pl.pallas_call(
        kern,
        grid=(r // tm,),
        in_specs=x_specs + [
            pl.BlockSpec((1, d), lambda i: (0, 0)),
            pl.BlockSpec((None, N_MOD, d), mod_idx),
            pl.BlockSpec((tm, LANE), lambda i: (i % tps, 0)),
            pl.BlockSpec((tm, LANE), lambda i: (i % tps, 0)),
            pl.BlockSpec((None, d, IN_COLS), lambda i: (layer, 0, 0)),
        ],
        out_specs=[pl.BlockSpec((tm, IN_COLS), lambda i: (i, 0)),
                   pl.BlockSpec((PROJ_CHUNK, tm), lambda i: (0, i))] + ([row] if first else []),
        out_shape=[jax.ShapeDtypeStruct((r, IN_COLS), ACT_DT), jax.ShapeDtypeStruct((PROJ_CHUNK, r), ACT_DT)]
        + ([jax.ShapeDtypeStruct((r, d), F32)] if first else []),
        compiler_params=_cparams("parallel"),
        name="inproj",
    )(*x_args, g1, mod, cos, sin_signed, w_in)


def _attn_kernel(dl_ref, g_ref, q_ref, kt_ref, v_ref, o_ref, vext_ref, s_ref, mx_ref, *,
                 lam_init, ctx_rows, has_ctx):
    tq = s_ref.shape[2]
    seq = kt_ref.shape[1]
    one_hot = lax.broadcasted_iota(jnp.int32, v_ref.shape, 1) == 0
    vext_ref[:, :DV] = v_ref[...]
    vext_ref[:, DV:] = jnp.where(one_hot, 1.0, 0.0).astype(vext_ref.dtype)

    dl = dl_ref[...]
    lam = (jnp.exp(jnp.sum(dl[0:1] * dl[1:2], keepdims=True))
           - jnp.exp(jnp.sum(dl[2:3] * dl[3:4], keepdims=True)) + lam_init)
    def chunk_list(kv_rows):
        kc = min(ATTN_KV_CHUNK, kv_rows)
        assert kv_rows % kc == 0
        return [slice(c * kc, (c + 1) * kc) for c in range(kv_rows // kc)]

    def score_pass(kv_rows, q_rows, buf):
        q = q_ref[q_rows, :]
        lane = lax.broadcasted_iota(jnp.int32, q.shape, 1)
        zero = jnp.zeros_like(q)
        qms = (jnp.where(lane < DK, q, zero), jnp.where(lane >= DK, q, zero))
        mruns = [None, None]
        for cols in chunk_list(kv_rows):
            for m in range(2):
                s = jnp.dot(qms[m], kt_ref[:, cols], preferred_element_type=F32)
                s_ref[buf, m, :, cols] = s
                for g in range(s.shape[1] // LANE):
                    blk = s[:, g * LANE:(g + 1) * LANE]
                    mruns[m] = blk if mruns[m] is None else jnp.maximum(mruns[m], blk)
            yield
        for m in range(2):
            mx = jnp.max(mruns[m], axis=-1, keepdims=True)
            mx_ref[buf, m] = jnp.broadcast_to(mx, mx_ref.shape[2:])
        yield

    def value_pass(kv_rows, out_rows, buf):
        mxs = [mx_ref[buf, m][:, 0:1] for m in range(2)]
        accs = [jnp.zeros((tq, 2 * DV), F32)] * 2
        for cols in chunk_list(kv_rows):
            for m in range(2):
                p = jnp.exp2(s_ref[buf, m, :, cols] - mxs[m])
                accs[m] = accs[m] + jnp.dot(p.astype(MXU_DT), vext_ref[cols, :], preferred_element_type=F32)
            yield
        o = (accs[0][:, :DV] * (1.0 / accs[0][:, DV:DV + 1])
             - accs[1][:, :DV] * (lam / accs[1][:, DV:DV + 1]))
        o_ref[out_rows, :] = (_rms(o) * g_ref[...] * (1.0 - lam_init)).astype(o_ref.dtype)
        yield

    def run(*passes):
        live = list(passes)
        while live:
            live = [g for g in live if next(g, "done") != "done"]

    n_lat = (seq - ctx_rows) // tq
    q_at = lambda i: pl.ds(pl.multiple_of(ctx_rows + i * tq, tq), tq)
    assert n_lat % 2 == 0
    out_off = 0
    if has_ctx:
        out_off = ctx_rows
        run(score_pass(ctx_rows, slice(0, ctx_rows), 1))
        run(score_pass(seq, q_at(0), 0), value_pass(ctx_rows, slice(0, ctx_rows), 1))
    else:
        run(score_pass(seq, q_at(0), 0))
    o_at = lambda i: pl.ds(pl.multiple_of(out_off + i * tq, tq), tq)

    def tile_pair(j, carry):
        a = 2 * j
        run(score_pass(seq, q_at(a + 1), 1), value_pass(seq, o_at(a), 0))
        run(score_pass(seq, q_at(a + 2), 0), value_pass(seq, o_at(a + 1), 1))
        return carry

    lax.fori_loop(0, n_lat // 2 - 1, tile_pair, 0)
    run(score_pass(seq, q_at(n_lat - 1), 1), value_pass(seq, o_at(n_lat - 2), 0))
    run(value_pass(seq, o_at(n_lat - 1), 1))


def _attention(p, k_t, diff_lambda, g_sub, lam_init, tiles, ctx_rows):
    tq = ROW_TILE
    tps = tiles.tps
    seq = tps * tq
    out_rows = tiles.per_seq * tq
    assert ctx_rows == tq
    kern = functools.partial(_attn_kernel, lam_init=lam_init, ctx_rows=ctx_rows, has_ctx=not tiles.skip_ctx)
    return pl.pallas_call(
        kern,
        grid=(tiles.n_batch, H_A),
        in_specs=[
            pl.BlockSpec((4, DK), lambda b, h: (0, 0)),
            pl.BlockSpec((1, DV), lambda b, h: (0, 0)),
            pl.BlockSpec((seq, LANE), lambda b, h: (b, P_Q // LANE + h)),
            pl.BlockSpec((LANE, seq), lambda b, h: (h, b)),
            pl.BlockSpec((seq, LANE), lambda b, h: (b, P_V // LANE + h)),
        ],
        out_specs=pl.BlockSpec((out_rows, LANE), lambda b, h: (b, h)),
        out_shape=jax.ShapeDtypeStruct((tiles.count * tq, H_A * DV), ACT_DT),
        scratch_shapes=[pltpu.VMEM((seq, 2 * DV), MXU_DT), pltpu.VMEM((2, 2, tq, seq), F32),
                        pltpu.VMEM((2, 2, tq, LANE), F32)],
        compiler_params=_cparams("parallel", "parallel"),
        name="diff_attention",
    )(diff_lambda, g_sub, p, k_t, p)


def _gmlp_kernel(uv_ref, gv_ref, ws_ref, bs_ref, o_ref):
    uv = jax.nn.gelu(uv_ref[...])
    u = uv[:, :W_B].astype(F32)
    v = (_rms(uv[:, W_B:].astype(F32)) * gv_ref[...]).astype(MXU_DT)
    gw = W_B // G_B
    for c in range(uv.shape[0] // CHUNK_B):
        rows = slice(c * CHUNK_B, (c + 1) * CHUNK_B)
        for g in range(G_B):
            cols = slice(g * gw, (g + 1) * gw)
            mixed = jnp.dot(ws_ref[g], v[rows, cols], preferred_element_type=F32) + bs_ref[:, cols]
            o_ref[rows, cols] = (u[rows, cols] * mixed).astype(o_ref.dtype)


def _gmlp(p, g_v, w_s, b_full, tiles):
    tm = ROW_TILE
    r = tiles.count * tm
    return pl.pallas_call(
        _gmlp_kernel,
        grid=(tiles.count,),
        in_specs=[
            pl.BlockSpec((tm, 2 * W_B), lambda i: (tiles.full(i), P_UV // (2 * W_B))),
            pl.BlockSpec((1, W_B), lambda i: (0, 0)),
            pl.BlockSpec((G_B, CHUNK_B, CHUNK_B), lambda i: (0, 0, 0)),
            pl.BlockSpec((CHUNK_B, W_B), lambda i: (0, 0)),
        ],
        out_specs=pl.BlockSpec((tm, W_B), lambda i: (i, 0)),
        out_shape=jax.ShapeDtypeStruct((r, W_B), ACT_DT),
        compiler_params=_cparams("parallel"),
        name="gmlp",
    )(p, g_v, w_s, b_full)


def _mfeat_kernel(x_ref, prev_ref, next_ref, wc_ref, bc_ref, wqkv_ref, wif_ref, bif_ref,
                  xc_ref, q_ref, k_ref, v_ref, g_ref, *, tps):
    tm = x_ref.shape[0]
    j = pl.program_id(0) % tps
    seg_start = jnp.logical_or(j == 0, j == 1)
    seg_end = jnp.logical_or(j == 0, j == tps - 1)
    x = x_ref[...].astype(F32)
    prow = jnp.where(seg_start, 0.0, prev_ref[...].astype(F32)[HALO - 1:HALO, :])
    nrow = jnp.where(seg_end, 0.0, next_ref[...].astype(F32)[0:1, :])
    rid = lax.broadcasted_iota(jnp.int32, x.shape, 0)
    xp = jnp.where(rid == 0, prow, pltpu.roll(x, 1, 0))
    xn = jnp.where(rid == tm - 1, nrow, pltpu.roll(x, tm - 1, 0))
    xc = wc_ref[0:1, :] * xp + wc_ref[1:2, :] * x + wc_ref[2:3, :] * xn + bc_ref[...]
    xc = xc * _sigmoid(xc)
    xc_ref[...] = xc.astype(xc_ref.dtype)
    xcb = xc.astype(MXU_DT)
    xb = x_ref[...]
    qs, ks, vs = [], [], []
    for h in range(H_C):
        cols = slice(h * DH_C, (h + 1) * DH_C)
        qs.append(jnp.dot(xcb[:, cols], wqkv_ref[0, h], preferred_element_type=F32))
        ks.append(jnp.dot(xcb[:, cols], wqkv_ref[1, h], preferred_element_type=F32))
        vs.append(jnp.dot(xb[:, cols], wqkv_ref[2, h], preferred_element_type=F32))
    q = jnp.concatenate(qs, axis=1)
    k = jnp.concatenate(ks, axis=1)
    v = jnp.concatenate(vs, axis=1)
    qkv = jnp.concatenate([q, k, v], axis=1).astype(MXU_DT)
    gates = jnp.dot(qkv, wif_ref[...], preferred_element_type=F32) + bif_ref[...]
    g_ref[...] = gates.T[0:g_ref.shape[0], :]
    q_ref[...] = q.T.astype(q_ref.dtype)
    k_ref[...] = (k * (DH_C ** -0.5)).astype(k_ref.dtype)
    v_ref[...] = v.T.astype(v_ref.dtype)


def _mlstm_features(p, w_conv, b_conv, w_qkv, w_if_pad, b_if_pad, tps):
    r = p.shape[0]
    tm = ROW_TILE
    hpt = tm // HALO
    xm_blk = P_XM // W_C
    last_halo = r // HALO - 1
    kern = functools.partial(_mfeat_kernel, tps=tps)
    act = jax.ShapeDtypeStruct((r, W_C), ACT_DT)
    act_t = jax.ShapeDtypeStruct((W_C, r), ACT_DT)
    row_spec = pl.BlockSpec((tm, W_C), lambda i: (i, 0))
    col_spec = pl.BlockSpec((W_C, tm), lambda i: (0, i))
    return pl.pallas_call(
        kern,
        grid=(r // tm,),
        in_specs=[
            pl.BlockSpec((tm, W_C), lambda i: (i, xm_blk)),
            pl.BlockSpec((HALO, W_C), lambda i: (jnp.maximum(i * hpt - 1, 0), xm_blk)),
            pl.BlockSpec((HALO, W_C), lambda i: (jnp.minimum((i + 1) * hpt, last_halo), xm_blk)),
            pl.BlockSpec((CONV_K, W_C), lambda i: (0, 0)),
            pl.BlockSpec((1, W_C), lambda i: (0, 0)),
            pl.BlockSpec((3, H_C, DH_C, DH_C), lambda i: (0, 0, 0, 0)),
            pl.BlockSpec((3 * W_C, LANE), lambda i: (0, 0)),
            pl.BlockSpec((1, LANE), lambda i: (0, 0)),
        ],
        out_specs=[row_spec, col_spec, row_spec, col_spec,
                   pl.BlockSpec((2 * GATE_ROWS, tm), lambda i: (0, i))],
        out_shape=[act, act_t, act, act_t, jax.ShapeDtypeStruct((2 * GATE_ROWS, r), F32)],
        compiler_params=_cparams("parallel"),
        name="mlstm_features",
    )(p, p, p, w_conv, b_conv, w_qkv, w_if_pad, b_if_pad)


GATE_ROWS = 2 * H_C
GATE_PACK = ("r", "cmx", "bcum")


def _gate_prep_kernel(g_ref, pack_ref, rt_ref):
    seg = MLSTM_CHUNK
    gt_fwd = g_ref[0:GATE_ROWS, :]
    gt_bwd = g_ref[GATE_ROWS:2 * GATE_ROWS, :]
    n = gt_fwd.shape[1]
    lane = lax.broadcasted_iota(jnp.int32, gt_fwd.shape, 1)
    in_seg = lane & (seg - 1)

    def scan_both(xf, xb, op):
        sh = 1
        while sh < seg:
            xf = jnp.where(in_seg >= sh, op(xf, pltpu.roll(xf, sh, 1)), xf)
            xb = jnp.where(in_seg < seg - sh, op(xb, pltpu.roll(xb, n - sh, 1)), xb)
            sh *= 2
        return xf, xb

    bcum_f, bcum_b = scan_both(jax.nn.log_sigmoid(pltpu.roll(gt_fwd, H_C, 0)),
                               jax.nn.log_sigmoid(pltpu.roll(gt_bwd, H_C, 0)), jnp.add)
    r_f, r_b = gt_fwd - bcum_f, gt_bwd - bcum_b
    cmx_f, cmx_b = scan_both(r_f, r_b, jnp.maximum)
    pack_ref[...] = jnp.concatenate([r_f, cmx_f, bcum_f, r_b, cmx_b, bcum_b], axis=0)
    rt_ref[...] = jnp.concatenate([r_f, r_b, jnp.zeros((LANE - 2 * GATE_ROWS, n), F32)], axis=0).T


def _mlstm_gate_prep(gates_t, n_batch, seq):
    rows = 2 * len(GATE_PACK) * GATE_ROWS
    r = gates_t.shape[1]
    return pl.pallas_call(
        _gate_prep_kernel,
        grid=(n_batch,),
        in_specs=[pl.BlockSpec((2 * GATE_ROWS, seq), lambda b: (0, b))],
        out_specs=[pl.BlockSpec((rows, seq), lambda b: (0, b)), pl.BlockSpec((seq, LANE), lambda b: (b, 0))],
        out_shape=[jax.ShapeDtypeStruct((rows, r), F32), jax.ShapeDtypeStruct((r, LANE), F32)],
        compiler_params=_cparams("parallel"),
        name="mlstm_gate_prep",
    )(gates_t)


def _mlstm_gates(pack, m_ref, reverse):
    ln = pack.shape[1]
    part = {name: pack[i * GATE_ROWS:(i + 1) * GATE_ROWS] for i, name in enumerate(GATE_PACK)}
    end = slice(0, 1) if reverse else slice(ln - 1, ln)
    g = part["bcum"][:, end]
    r_max = part["cmx"][:, end]
    m_loc = g + r_max
    w_end = jnp.exp(part["r"] - r_max)
    m_old = m_ref[:, 0:1]
    m_new = jnp.maximum(g + m_old, m_loc)
    a_old = jnp.exp(g + m_old - m_new)
    a_new = jnp.exp(m_loc - m_new)
    mx = jnp.maximum(m_old, part["cmx"])
    w_inter = jnp.exp(m_old - mx)
    e_neg = jnp.exp(-(part["bcum"] + mx))
    s_id = lax.broadcasted_iota(jnp.int32, (ln, ln), 0)
    t_id = lax.broadcasted_iota(jnp.int32, (ln, ln), 1)
    keep = (s_id >= t_id) if reverse else (s_id <= t_id)
    return dict(mx=mx, w_inter=w_inter, e_neg=e_neg, w_end=w_end, a_old=a_old, a_new=a_new,
                m_new=m_new, keep=keep)


def _mscan_chunk(gf_ref, rf_ref, qf_ref, kf_ref, vf_ref, gb_ref, rb_ref, qb_ref, kb_ref, vb_ref,
                 hf_ref, hb_ref, cf_ref, mf_ref, cb_ref, mb_ref):
    ln = qf_ref.shape[1]
    pr = len(GATE_PACK) * GATE_ROWS
    dirs = ((_mlstm_gates(gf_ref[0:pr, :], mf_ref, False), qf_ref, kf_ref, vf_ref, cf_ref, hf_ref, rf_ref),
            (_mlstm_gates(gb_ref[pr:2 * pr, :], mb_ref, True), qb_ref, kb_ref, vb_ref, cb_ref, hb_ref, rb_ref))
    pairs = [(d, h) for d in range(2) for h in range(H_C)]
    ones_row = jnp.where(lax.broadcasted_iota(jnp.int32, (DH_C, ln), 0) == 0, 1.0, 0.0).astype(MXU_DT)
    head = lambda h: slice(h * DH_C, (h + 1) * DH_C)
    row = lambda h: slice(h, h + 1)

    ks = [dirs[d][2][:, head(h)] for d, h in pairs]
    qts = [dirs[d][1][head(h), :] for d, h in pairs]
    vexts = [jnp.concatenate([dirs[d][3][head(h), :], ones_row], axis=0) for d, h in pairs]
    cexts = [dirs[d][4][h] for d, h in pairs]
    qks = [jnp.dot(k, q_t, preferred_element_type=F32) for k, q_t in zip(ks, qts)]
    rhss = []
    for i, (d, h) in enumerate(pairs):
        g = dirs[d][0]
        r_col = dirs[d][6][:, d * GATE_ROWS + h:d * GATE_ROWS + h + 1]
        d_t = jnp.exp(jnp.where(g["keep"], r_col - g["mx"][row(h), :], NEG_BIG))
        wq = (qts[i].astype(F32) * g["w_inter"][row(h), :]).astype(MXU_DT)
        rhss.append(jnp.concatenate([(qks[i] * d_t).astype(MXU_DT), wq], axis=0))
    ress = [jnp.dot(jnp.concatenate([vexts[i], cexts[i].astype(MXU_DT)], axis=1), rhss[i],
                    preferred_element_type=F32) for i in range(len(pairs))]
    for i, (d, h) in enumerate(pairs):
        g = dirs[d][0]
        den = ress[i][DH_C:DH_C + 1, :]
        hh = ress[i][:DH_C, :] / jnp.maximum(jnp.abs(den), g["e_neg"][row(h), :])
        dirs[d][5][head(h), :] = hh.astype(dirs[d][5].dtype)
    clocs = []
    for i, (d, h) in enumerate(pairs):
        vw_t = (vexts[i].astype(F32) * dirs[d][0]["w_end"][row(h), :]).astype(MXU_DT)
        clocs.append(jnp.dot(vw_t, ks[i], preferred_element_type=F32))
    for i, (d, h) in enumerate(pairs):
        g = dirs[d][0]
        dirs[d][4][h] = g["a_old"][row(h), :] * cexts[i] + g["a_new"][row(h), :] * clocs[i]
    mf_ref[...] = jnp.broadcast_to(dirs[0][0]["m_new"], mf_ref.shape)
    mb_ref[...] = jnp.broadcast_to(dirs[1][0]["m_new"], mb_ref.shape)


def _mscan_kernel(g_ref, r_ref, q_ref, k_ref, v_ref, hf_ref, hb_ref, cf_ref, mf_ref, cb_ref, mb_ref, *, ncc):
    ln = MLSTM_CHUNK
    nck = q_ref.shape[1] // ln
    cf_ref[...] = jnp.zeros_like(cf_ref)
    cb_ref[...] = jnp.zeros_like(cb_ref)
    mf_ref[...] = jnp.zeros_like(mf_ref)
    mb_ref[...] = jnp.zeros_like(mb_ref)

    def chunk_views(c):
        at = pl.ds(pl.multiple_of(c * ln, ln), ln)
        return ((g_ref.at[:, at], r_ref.at[at, :], q_ref.at[:, at], k_ref.at[at, :], v_ref.at[:, at]),
                (hf_ref.at[:, at], hb_ref.at[:, at]))

    def trip(s, carry):
        ins_f, (hf, _) = chunk_views(s)
        ins_b, (_, hb) = chunk_views(jnp.where(s < ncc, ncc - 1 - s, nck - 1 + ncc - s))
        _mscan_chunk(*ins_f, *ins_b, hf, hb, cf_ref, mf_ref, cb_ref, mb_ref)
        return carry

    lax.fori_loop(0, nck, trip, 0)


def _mlstm_scan(q_t, k, v_t, gate_pack, r_t, n_batch, seq, ctx_rows):
    rows = lambda w: pl.BlockSpec((seq, w), lambda b: (b, 0))
    cols = lambda h: pl.BlockSpec((h, seq), lambda b: (0, b))
    out = jax.ShapeDtypeStruct(q_t.shape, ACT_DT)
    return pl.pallas_call(
        functools.partial(_mscan_kernel, ncc=ctx_rows // MLSTM_CHUNK),
        grid=(n_batch,),
        in_specs=[cols(gate_pack.shape[0]), rows(LANE), cols(W_C), rows(W_C), cols(W_C)],
        out_specs=[cols(W_C), cols(W_C)],
        out_shape=[out, out],
        scratch_shapes=[
            pltpu.VMEM((H_C, 2 * DH_C, DH_C), F32), pltpu.VMEM((8, LANE), F32),
            pltpu.VMEM((H_C, 2 * DH_C, DH_C), F32), pltpu.VMEM((8, LANE), F32),
        ],
        compiler_params=_cparams("parallel"),
        name="mlstm_scan",
    )(gate_pack, r_t, q_t, k, v_t)


def _route(logits, carry):
    lane = lax.broadcasted_iota(jnp.int32, logits.shape, 1).astype(F32)
    big = float(4 * LANE)
    gl = jnp.where(lane < N_GROUPS, logits, NEG_BIG)
    gmax = jnp.max(gl, axis=-1, keepdims=True)
    g_star = jnp.min(jnp.where(gl == gmax, lane, big), axis=-1, keepdims=True)
    p_g = 1.0 / jnp.sum(jnp.exp(gl - gmax), axis=-1, keepdims=True)
    e_lo = g_star * EXP_PER_GROUP + N_GROUPS
    in_group = jnp.logical_and(lane >= e_lo, lane < e_lo + EXP_PER_GROUP)
    el = jnp.where(in_group, logits, NEG_BIG)
    v1 = jnp.max(el, axis=-1, keepdims=True)
    i1 = jnp.min(jnp.where(el == v1, lane, big), axis=-1, keepdims=True)
    el2 = jnp.where(lane == i1, NEG_BIG, el)
    v2 = jnp.max(el2, axis=-1, keepdims=True)
    i2 = jnp.min(jnp.where(el2 == v2, lane, big), axis=-1, keepdims=True)
    e21 = jnp.exp(v2 - v1)
    w1 = p_g / (1.0 + e21)
    w2 = p_g * e21 / (1.0 + e21)
    e1 = i1 - N_GROUPS
    e2 = i2 - N_GROUPS
    oh1 = lane == e1
    oh2 = lane == e2
    both = jnp.where(jnp.logical_or(oh1, oh2), 1.0, 0.0)
    tm = logits.shape[0]
    lower = (lax.broadcasted_iota(jnp.int32, (tm, tm), 1) < lax.broadcasted_iota(jnp.int32, (tm, tm), 0))
    before = jnp.dot(jnp.where(lower, 1.0, 0.0).astype(MXU_DT), both.astype(MXU_DT),
                     preferred_element_type=F32) + carry
    rank1 = jnp.sum(jnp.where(oh1, before, 0.0), axis=-1, keepdims=True)
    rank2 = jnp.sum(jnp.where(oh2, before, 0.0), axis=-1, keepdims=True)
    out = jnp.where(lane == 0, e1, 0.0)
    out = jnp.where(lane == 1, e2, out)
    out = jnp.where(lane == 2, w1, out)
    out = jnp.where(lane == 3, w2, out)
    out = jnp.where(lane == 4, rank1, out)
    out = jnp.where(lane == 5, rank2, out)
    return out, carry + jnp.sum(both, axis=0, keepdims=True)


def _merge_kernel(x_ref, gate_ref, z_ref, ya_ref, yb_ref, hf_ref, hb_ref, xc_ref, gm_ref, sk_ref,
                  wbr_ref, wo_ref, mod_ref, g2_ref, wr_ref, br_ref, xo_ref, rt_ref, cnt_ref, carry_ref):
    @pl.when(pl.program_id(0) == 0)
    def _():
        carry_ref[...] = jnp.zeros_like(carry_ref)

    hs_t = hf_ref[...].astype(F32) + hb_ref[...].astype(F32)
    hn_t = []
    for h in range(H_C):
        part = hs_t[h * DH_C:(h + 1) * DH_C, :]
        hn_t.append(part * lax.rsqrt(jnp.mean(part * part, axis=0, keepdims=True) + EPS))
    hn = jnp.concatenate(hn_t, axis=0).T * gm_ref[...]
    yc = (hn + sk_ref[...] * xc_ref[...].astype(F32)) * _sigmoid(z_ref[...].astype(F32))
    ys = (ya_ref[...], yb_ref[...], yc.astype(MXU_DT))
    merged = None
    for i in range(3):
        gate = _sigmoid(gate_ref[:, i * D_MODEL:(i + 1) * D_MODEL].astype(MXU_DT))
        term = gate * jnp.dot(ys[i], wbr_ref[i], preferred_element_type=F32).astype(MXU_DT)
        merged = term if merged is None else merged + term
    y = jnp.dot(merged, wo_ref[...], preferred_element_type=F32)
    xn = x_ref[...] + mod_ref[2:3, :] * y
    xo_ref[...] = xn
    h2 = _rms(xn) * g2_ref[...] * (1.0 + mod_ref[4:5, :]) + mod_ref[3:4, :]
    h_hi, h_lo = _split_bf16(h2)
    logits = (jnp.dot(h_hi, wr_ref[0], preferred_element_type=F32)
              + jnp.dot(h_lo, wr_ref[0], preferred_element_type=F32)
              + jnp.dot(h_hi, wr_ref[1], preferred_element_type=F32)) + br_ref[...]
    route, carry = _route(logits, carry_ref[0:1, :])
    rt_ref[...] = route
    carry_ref[0:1, :] = carry
    cnt_ref[...] = jnp.broadcast_to(carry, cnt_ref.shape)


def _merge(xall, p, ya, yb, hf, hb, xconv, g_m, skip, w_br, w_o, mod, g2, w_route, b_route, tiles):
    d = xall.shape[1]
    tm = ROW_TILE
    r = tiles.count * tm

    def mod_idx(i):
        return (tiles.mod_row(i), 0, 0)

    row = lambda w: pl.BlockSpec((tm, w), lambda i: (i, 0))
    frow = lambda w: pl.BlockSpec((tm, w), lambda i: (tiles.full(i), 0))
    fcol = lambda w: pl.BlockSpec((w, tm), lambda i: (0, tiles.full(i)))
    full = lambda shape: pl.BlockSpec(shape, lambda i: (0,) * len(shape))
    return pl.pallas_call(
        _merge_kernel,
        grid=(tiles.count,),
        in_specs=[
            frow(d),
            pl.BlockSpec((tm, GATE_W), lambda i: (tiles.full(i), P_GATE // GATE_W)),
            pl.BlockSpec((tm, W_C), lambda i: (tiles.full(i), P_Z // W_C)),
            row(W_BR), row(W_BR), fcol(W_C), fcol(W_C), frow(W_C),
            full((1, W_C)), full((1, W_C)),
            full((3, W_BR, d)), full((d, d)),
            pl.BlockSpec((None, N_MOD, d), mod_idx),
            full((1, d)), full((2, d, LANE)), full((1, LANE)),
        ],
        out_specs=[row(d), row(LANE), full((8, LANE))],
        out_shape=[jax.ShapeDtypeStruct((r, d), F32), jax.ShapeDtypeStruct((r, LANE), F32),
                   jax.ShapeDtypeStruct((8, LANE), F32)],
        scratch_shapes=[pltpu.VMEM((8, LANE), F32)],
        compiler_params=_cparams("arbitrary"),
        name="merge_route",
    )(xall, p, p, ya, yb, hf, hb, xconv, g_m, skip, w_br, w_o, mod, g2, w_route, b_route)


def _row_copy(src_hbm, src_row, dst_ref, dst_row, sem):
    return pltpu.make_async_copy(src_hbm.at[pl.ds(src_row, 1), :], dst_ref.at[pl.ds(dst_row, 1), :], sem)


def _dispatch_kernel(fill_ref, nlive_ref, pos_ref, x_ref, g2_ref, mod_ref, xs_hbm, hbuf, zbuf, zsem, sem):
    i = pl.program_id(0)
    n = pl.num_programs(0)
    tm = x_ref.shape[0]
    slot = i % 2
    bm = zbuf.shape[0]
    n_blocks = xs_hbm.shape[0] // bm

    @pl.when(i == 0)
    def _():
        zbuf[...] = jnp.zeros_like(zbuf)

        def fill_block(b):
            return pltpu.make_async_copy(zbuf, xs_hbm.at[pl.ds(b * bm, bm), :], zsem)

        def fill_last(k, carry):
            fill_block(fill_ref[k]).start()
            return carry

        def fill_unused(b, carry):
            fill_block(b).start()
            return carry

        def wait_fill(k, carry):
            fill_block(0).wait()
            return carry

        n_live = nlive_ref[0]
        n_used = nlive_ref[1]
        lax.fori_loop(0, n_used, fill_last, 0)
        lax.fori_loop(n_live, n_blocks, fill_unused, 0)
        lax.fori_loop(0, n_used + n_blocks - n_live, wait_fill, 0)

    hbuf[slot] = _rms(x_ref[...]) * g2_ref[...] * (1.0 + mod_ref[4:5, :]) + mod_ref[3:4, :]

    for r in range(tm):
        for j in range(TOP_K):
            pltpu.make_async_copy(hbuf.at[slot, pl.ds(r, 1), :], xs_hbm.at[pl.ds(pos_ref[0, j * tm + r], 1), :],
                                  sem.at[slot]).start(priority=j % N_DMA_PRIORITIES)

    def wait_tile(s):
        pltpu.make_async_copy(hbuf.at[s], xs_hbm.at[pl.ds(0, tm), :], sem.at[s]).wait()
        pltpu.make_async_copy(hbuf.at[s], xs_hbm.at[pl.ds(0, tm), :], sem.at[s]).wait()

    @pl.when(i > 0)
    def _():
        wait_tile(1 - slot)

    @pl.when(i == n - 1)
    def _():
        wait_tile(slot)


def _moe_dispatch(xall, g2, mod, fill_blocks, n_live, pos_tiles, n_slots, tiles):
    r, d = xall.shape
    tm = ROW_TILE
    assert TOP_K == 2 and r == tiles.count * tm

    def mod_idx(i, fb, nl):
        return (tiles.mod_row(i), 0, 0)

    grid_spec = pltpu.PrefetchScalarGridSpec(
        num_scalar_prefetch=2,
        grid=(r // tm,),
        in_specs=[
            pl.BlockSpec((None, 1, TOP_K * tm), lambda i, fb, nl: (i, 0, 0), memory_space=pltpu.SMEM),
            pl.BlockSpec((tm, d), lambda i, fb, nl: (i, 0)),
            pl.BlockSpec((1, d), lambda i, fb, nl: (0, 0)),
            pl.BlockSpec((None, N_MOD, d), mod_idx),
        ],
        out_specs=pl.BlockSpec(memory_space=pl.ANY),
        scratch_shapes=[pltpu.VMEM((2, tm, d), F32), pltpu.VMEM((MOE_BLOCK, d), F32),
                        pltpu.SemaphoreType.DMA(()), pltpu.SemaphoreType.DMA((2,))],
    )
    return pl.pallas_call(
        _dispatch_kernel,
        grid_spec=grid_spec,
        out_shape=jax.ShapeDtypeStruct((n_slots, d), F32),
        compiler_params=_cparams("arbitrary"),
        name="moe_dispatch",
    )(fill_blocks, n_live, pos_tiles, xall, g2, mod)


def _moe_kernel(be_ref, nv_ref, ne_ref, x_ref, wg_hbm, wu_hbm, wd_hbm, y_ref,
                wg_mx, wu_mx, wd_mx, wg_in, wu_in, wd_in, par_ref, sem, *, layer):
    i = pl.program_id(0)
    n_valid = nv_ref[i]

    def fetch(e, slot):
        return [pltpu.make_async_copy(src.at[layer, e], dst.at[slot], sem.at[slot])
                for src, dst in ((wg_hbm, wg_in), (wu_hbm, wu_in), (wd_hbm, wd_in))]

    @pl.when(i == 0)
    def _():
        par_ref[0] = 0
        for cp in fetch(be_ref[0], 0):
            cp.start()

    run_start = jnp.logical_or(i == 0, be_ref[i] != be_ref[jnp.maximum(i - 1, 0)])

    @pl.when(jnp.logical_and(n_valid > 0, run_start))
    def _():
        slot = par_ref[0]
        for cp in fetch(be_ref[i], slot):
            cp.wait()
        wg_mx[...] = wg_in[slot].astype(wg_mx.dtype)
        wu_mx[...] = wu_in[slot].astype(wu_mx.dtype)
        wd_mx[...] = wd_in[slot].astype(wd_mx.dtype)
        nxt = ne_ref[i]

        @pl.when(nxt >= 0)
        def _():
            for cp in fetch(nxt, 1 - slot):
                cp.start()

        par_ref[0] = 1 - slot

    @pl.when(n_valid > 0)
    def _():
        x = x_ref[...].astype(MXU_DT)
        a = jnp.dot(x, wg_mx[...], preferred_element_type=F32)
        u = jnp.dot(x, wu_mx[...], preferred_element_type=F32)
        act = (a * _sigmoid(a) * u).astype(MXU_DT)
        y_ref[...] = jnp.dot(act, wd_mx[...], preferred_element_type=F32)

    @pl.when(n_valid <= 0)
    def _():
        y_ref[...] = jnp.zeros_like(y_ref)


def _moe_experts(xs, block_e, n_valid, next_e, w_gate, w_up, w_down, layer):
    n_slots, d = xs.shape
    bm = MOE_BLOCK
    wdt = w_gate.dtype
    grid_spec = pltpu.PrefetchScalarGridSpec(
        num_scalar_prefetch=3,
        grid=(n_slots // bm,),
        in_specs=[
            pl.BlockSpec((bm, d), lambda i, be, nv, ne: (i, 0)),
            pl.BlockSpec(memory_space=pl.ANY), pl.BlockSpec(memory_space=pl.ANY), pl.BlockSpec(memory_space=pl.ANY),
        ],
        out_specs=pl.BlockSpec((bm, d), lambda i, be, nv, ne: (i, 0)),
        scratch_shapes=[pltpu.VMEM((d, D_EXPERT), MXU_DT), pltpu.VMEM((d, D_EXPERT), MXU_DT),
                        pltpu.VMEM((D_EXPERT, d), MXU_DT),
                        pltpu.VMEM((2, d, D_EXPERT), wdt), pltpu.VMEM((2, d, D_EXPERT), wdt),
                        pltpu.VMEM((2, D_EXPERT, d), wdt),
                        pltpu.SMEM((1,), jnp.int32), pltpu.SemaphoreType.DMA((2,))],
    )
    return pl.pallas_call(
        functools.partial(_moe_kernel, layer=layer),
        grid_spec=grid_spec,
        out_shape=jax.ShapeDtypeStruct((n_slots, d), F32),
        compiler_params=_cparams("arbitrary"),
        name="moe_experts",
    )(block_e, n_valid, next_e, xs, w_gate, w_up, w_down)


def _combine_kernel(pos_ref, pos_next_ref, x_ref, rt_ref, mod_ref, gf_ref, yb_hbm, xo_ref, ybuf, sem, *, final):
    i = pl.program_id(0)
    n = pl.num_programs(0)
    tm = x_ref.shape[0]

    def start_gather(ids_ref, slot):
        def body(c, carry):
            for u in range(DMA_UNROLL):
                r = c * DMA_UNROLL + u
                _row_copy(yb_hbm, ids_ref[0, r], ybuf.at[slot], r, sem.at[slot]).start()
            return carry
        lax.fori_loop(0, TOP_K * tm // DMA_UNROLL, body, 0)

    def wait_gather(slot):
        pltpu.make_async_copy(yb_hbm.at[pl.ds(0, TOP_K * tm), :], ybuf.at[slot], sem.at[slot]).wait()

    @pl.when(i == 0)
    def _():
        start_gather(pos_ref, 0)

    slot = i % 2
    wait_gather(slot)
    per = TOP_K * tm // COMBINE_GROUPS
    rows_per = tm // COMBINE_GROUPS
    for c in range(COMBINE_GROUPS):
        for r in range(c * per, (c + 1) * per):
            _row_copy(yb_hbm, pos_next_ref[0, r], ybuf.at[1 - slot], r,
                      sem.at[1 - slot]).start(priority=r % N_DMA_PRIORITIES)
        rows = slice(c * rows_per, (c + 1) * rows_per)
        rt = rt_ref[rows, :]
        f = (rt[:, 2:3] * ybuf[slot, rows, :]
             + rt[:, 3:4] * ybuf[slot, tm + c * rows_per:tm + (c + 1) * rows_per, :])
        xn = x_ref[rows, :] + mod_ref[5:6, :] * f
        xo_ref[rows, :] = _rms(xn) * gf_ref[...] if final else xn

    @pl.when(i == n - 1)
    def _():
        wait_gather(1 - slot)


def _combine(xall, route, mod, g_final, yb, pos, tiles, final):
    r, d = xall.shape
    tm = ROW_TILE
    nt = tiles.count
    last = nt - 1
    assert r == nt * tm

    def mod_idx(i):
        return (tiles.mod_row(i), 0, 0)

    row = lambda w: pl.BlockSpec((tm, w), lambda i: (i, 0))
    return pl.pallas_call(
        functools.partial(_combine_kernel, final=final),
        grid=(nt,),
        in_specs=[
            pl.BlockSpec((None, 1, TOP_K * tm), lambda i: (i, 0, 0), memory_space=pltpu.SMEM),
            pl.BlockSpec((None, 1, TOP_K * tm), lambda i: (jnp.minimum(i + 1, last), 0, 0),
                         memory_space=pltpu.SMEM),
            row(d), row(LANE),
            pl.BlockSpec((None, N_MOD, d), mod_idx),
            pl.BlockSpec((1, d), lambda i: (0, 0)),
            pl.BlockSpec(memory_space=pl.ANY),
        ],
        out_specs=row(d),
        out_shape=jax.ShapeDtypeStruct((r, d), F32),
        scratch_shapes=[pltpu.VMEM((2, TOP_K * tm, d), F32), pltpu.SemaphoreType.DMA((2,))],
        compiler_params=_cparams("arbitrary"),
        name="moe_combine",
    )(pos, pos, xall, route, mod, g_final, yb)


def _dispatch_plan(route, counts_f, n_tiles):
    n = route.shape[0]
    bm = MOE_BLOCK
    n_blocks = -(-(n * TOP_K + N_EXPERTS * (bm - 1)) // bm)
    counts = counts_f[:N_EXPERTS].astype(jnp.int32)
    padded = (counts + bm - 1) // bm * bm
    pend = jnp.cumsum(padded)
    seg_off = pend - padded
    eid = route[:, 0:TOP_K].astype(jnp.int32)
    rank = route[:, 4:4 + TOP_K].astype(jnp.int32)
    experts = jnp.arange(N_EXPERTS, dtype=jnp.int32)
    pos = jnp.sum(jnp.where(eid[..., None] == experts, seg_off, 0), axis=-1) + rank
    blk0 = jnp.arange(n_blocks, dtype=jnp.int32) * bm
    block_e = jnp.minimum(jnp.sum((blk0[:, None] >= pend[None, :]).astype(jnp.int32), axis=1), N_EXPERTS - 1)
    seg_end = seg_off + counts
    n_valid = jnp.clip(seg_end[block_e] - blk0, 0, bm).astype(jnp.int32)
    pos_tiles = pos.reshape(n_tiles, n // n_tiles, TOP_K).transpose(0, 2, 1).reshape(n_tiles, 1, -1)
    used = counts > 0
    last_blocks = jnp.sort(jnp.where(used, pend // bm - 1, n_blocks)).astype(jnp.int32)
    n_live = jnp.stack([pend[-1] // bm, jnp.sum(used)]).astype(jnp.int32)
    later_used = jnp.logical_and(experts[None, :] > experts[:, None], used[None, :])
    next_used = jnp.min(jnp.where(later_used, experts[None, :], N_EXPERTS), axis=1)
    next_e = jnp.where(next_used < N_EXPERTS, next_used, -1)[block_e].astype(jnp.int32)
    return block_e.astype(jnp.int32), n_valid, next_e, last_blocks, n_live, pos_tiles, n_blocks * bm


def _rope_tables(seq, ctx_rows):
    t = jnp.arange(seq - ctx_rows)
    inv = ROPE_BASE ** (-jnp.arange(0, ROT_AX, 2, dtype=F32) / ROT_AX)
    ang_r = (t // GRID_W).astype(F32)[:, None] * inv
    ang_c = (t % GRID_W).astype(F32)[:, None] * inv
    cos64 = jnp.concatenate([jnp.cos(ang_r)] * 2 + [jnp.cos(ang_c)] * 2, axis=1)
    sin64 = jnp.concatenate([-jnp.sin(ang_r), jnp.sin(ang_r), -jnp.sin(ang_c), jnp.sin(ang_c)], axis=1)
    cos = jnp.concatenate([jnp.ones((ctx_rows, DK), F32), cos64], axis=0)
    sin = jnp.concatenate([jnp.zeros((ctx_rows, DK), F32), sin64], axis=0)
    return jnp.tile(cos, (1, LANE // DK)), jnp.tile(sin, (1, LANE // DK))


def kernel(x, c, ctx, c_ctx, w_mod, b_mod, g_norm1, g_norm2, w_in, diff_lambda, g_diff_subln, w_gmlp_s, b_gmlp_s, g_gmlp_v, w_conv_m, b_conv_m, w_qkv_m, w_if_m, b_if_m, g_mlstm_norm, skip_m, w_branch, w_out, w_route_g, b_route_g, w_route_e, b_route_e, w_e_gate, w_e_up, w_e_down, g_final):
    n_batch, t_lat, d = x.shape
    ctx_rows = ctx.shape[1]
    depth = w_in.shape[0]
    seq = ctx_rows + t_lat
    assert d == D_MODEL and ctx_rows == ROW_TILE and t_lat % ROW_TILE == 0 and t_lat % GRID_W == 0
    tps = seq // ROW_TILE
    all_tiles = _Tiles(n_batch, tps, False)

    xall = (ctx.reshape(n_batch * ctx_rows, d), x.reshape(n_batch * t_lat, d))
    mb = -(-(n_batch + 1) // 8) * 8
    c_all = jnp.zeros((mb, d), F32).at[:n_batch].set(c).at[n_batch].set(c_ctx)
    mod_all = _modulation(c_all, w_mod, b_mod)[:, :n_batch + 1].reshape(depth, n_batch + 1, N_MOD, d)
    cos, sin_signed = _rope_tables(seq, ctx_rows)
    w_in_b = w_in.astype(MXU_DT)

    for l in range(depth):
        lam_init = 0.8 - 0.6 * math.exp(-0.3 * l)
        mod = mod_all[l]
        last = l == depth - 1
        tiles = _Tiles(n_batch, tps, last)
        if l == 0:
            p, k_t, xall = _inproj(xall, g_norm1[l][None], mod, cos, sin_signed, w_in_b, l, all_tiles)
        else:
            p, k_t = _inproj(xall, g_norm1[l][None], mod, cos, sin_signed, w_in_b, l, all_tiles)

        ya = _attention(p, k_t, diff_lambda[l], g_diff_subln[l][None], lam_init, tiles, ctx_rows)

        b_full = jnp.repeat(b_gmlp_s[l].T, W_B // G_B, axis=1)
        yb = _gmlp(p, g_gmlp_v[l][None], w_gmlp_s[l].astype(MXU_DT), b_full, tiles)

        w_if_pad = jnp.zeros((3 * W_C, LANE), F32).at[:, :4 * H_C].set(w_if_m[l]).astype(MXU_DT)
        b_if_pad = jnp.zeros((1, LANE), F32).at[0, :4 * H_C].set(b_if_m[l])
        xconv, q_m, k_m, v_m, gates_t = _mlstm_features(
            p, w_conv_m[l], b_conv_m[l][None], w_qkv_m[l].astype(MXU_DT), w_if_pad, b_if_pad, tps)
        gate_pack, r_t = _mlstm_gate_prep(gates_t, n_batch, seq)
        hf, hb = _mlstm_scan(q_m, k_m, v_m, gate_pack, r_t, n_batch, seq, ctx_rows)

        w_route = (jnp.zeros((d, LANE), F32).at[:, :N_GROUPS].set(w_route_g[l])
                   .at[:, N_GROUPS:N_GROUPS + N_EXPERTS].set(w_route_e[l]))
        w_route = jnp.stack(_split_bf16(w_route))
        b_route = (jnp.zeros((1, LANE), F32).at[0, :N_GROUPS].set(b_route_g[l])
                   .at[0, N_GROUPS:N_GROUPS + N_EXPERTS].set(b_route_e[l]))
        xall, route, counts = _merge(xall, p, ya, yb, hf, hb, xconv, g_mlstm_norm[l][None], skip_m[l][None],
                                         w_branch[l].astype(MXU_DT), w_out[l].astype(MXU_DT), mod,
                                         g_norm2[l][None], w_route, b_route, tiles)

        block_e, n_valid, next_e, fill_blocks, n_live, pos_tiles, n_slots = _dispatch_plan(
            route, counts[0], tiles.count)
        xs = _moe_dispatch(xall, g_norm2[l][None], mod, fill_blocks, n_live, pos_tiles, n_slots, tiles)
        y_sorted = _moe_experts(xs, block_e, n_valid, next_e, w_e_gate, w_e_up, w_e_down, l)
        xall = _combine(xall, route, mod, g_final[None], y_sorted, pos_tiles, tiles, last)

    return xall.reshape(n_batch, t_lat, d)
```
